```python
import math
import jax, jax.numpy as jnp
from jax import lax
import numpy as np

D_MODEL = 1024
BATCH = 8
SEQ = 2048
DEPTH = 1

D_MIX = D_MODEL
D_POOL = D_MIX // 2
POOL_WINDOWS = (2, 4, 8, 16)
N_POOL_GROUPS = len(POOL_WINDOWS)
POOL_GROUP = D_POOL // N_POOL_GROUPS
D_ATTN = D_MIX - D_POOL
HEAD_DIM = 64
N_HEADS = D_ATTN // HEAD_DIM
Q_BLOCK = 128
D_IN = D_POOL + 3 * D_ATTN + N_HEADS
N_EXPERTS = 64
N_EXPERT_GROUPS = 8
TOPK_GROUPS = 4
TOP_K = 8
D_EXPERT = 256
D_SHARED = 256
ROUTED_SCALE = 2.5
EXPERT_BLOCK = 128
EPS = 1e-6

kernel_name = 'hybrid_pool_fox_moe_block'


def rms_norm(x, g):
    xf = x.astype(jnp.float32)
    y = xf * lax.rsqrt(jnp.mean(xf * xf, axis=-1, keepdims=True) + EPS)
    return (y * g.astype(jnp.float32)).astype(x.dtype)


def modulate(h, shift, scale):
    return h * (1 + scale[:, None, :]) + shift[:, None, :]


def pool_mixer(u, w_pool, pool_scale):
    b, s, _ = u.shape
    ug = u.astype(jnp.float32).reshape(b, s, N_POOL_GROUPS, POOL_GROUP)
    cs = jnp.cumsum(ug, axis=1)
    pos = jnp.arange(1, s + 1, dtype=jnp.float32)
    outs = []
    for g, w in enumerate(POOL_WINDOWS):
        csg = cs[:, :, g]
        prev = jnp.pad(csg, ((0, 0), (w, 0), (0, 0)))[:, :s]
        mean = (csg - prev) / jnp.minimum(pos, float(w))[None, :, None]
        outs.append(mean - ug[:, :, g])
    pooled = jnp.stack(outs, axis=2).astype(u.dtype)
    y = jnp.einsum('bsgc,gcd->bsgd', pooled, w_pool)
    return y.reshape(b, s, D_POOL) * pool_scale


def fox_attention(q, k, v, logf):
    b, s, h, dh = q.shape
    cum = jnp.cumsum(logf.astype(jnp.float32), axis=1).transpose(0, 2, 1)
    qh = q.transpose(0, 2, 1, 3)
    kh = k.transpose(0, 2, 1, 3)
    vh = v.transpose(0, 2, 1, 3)
    scale = HEAD_DIM ** -0.5
    outs = []
    for i in range(s // Q_BLOCK):
        q0, q1 = i * Q_BLOCK, (i + 1) * Q_BLOCK
        qb, kb, vb = qh[:, :, q0:q1], kh[:, :, :q1], vh[:, :, :q1]
        logits = jnp.einsum('bhqd,bhkd->bhqk', qb, kb).astype(jnp.float32) * scale
        logits = logits + cum[:, :, q0:q1, None] - cum[:, :, None, :q1]
        causal = jnp.arange(q0, q1)[:, None] >= jnp.arange(q1)[None, :]
        logits = jnp.where(causal[None, None], logits, -jnp.inf)
        p = jax.nn.softmax(logits, axis=-1)
        outs.append(jnp.einsum('bhqk,bhkd->bhqd', p.astype(vb.dtype), vb))
    o = jnp.concatenate(outs, axis=2)
    return o.transpose(0, 2, 1, 3).reshape(b, s, h * dh)


def swiglu(x, wg, wu, wd):
    return (jax.nn.silu(x @ wg) * (x @ wu)) @ wd


def moe_ffn(h, w_router, router_bias, w_gate, w_up, w_down, ws_gate, ws_up, ws_down):
    b, s, d = h.shape
    t = b * s
    xf = h.reshape(t, d)
    scores = jax.nn.sigmoid((xf @ w_router).astype(jnp.float32))
    sel = scores + router_bias.astype(jnp.float32)
    grp = sel.reshape(t, N_EXPERT_GROUPS, N_EXPERTS // N_EXPERT_GROUPS)
    group_score = lax.top_k(grp, 2)[0].sum(-1)
    _, top_groups = lax.top_k(group_score, TOPK_GROUPS)
    group_mask = jnp.any(top_groups[:, :, None] == jnp.arange(N_EXPERT_GROUPS)[None, None, :], axis=1)
    expert_mask = jnp.repeat(group_mask, N_EXPERTS // N_EXPERT_GROUPS, axis=1)
    _, idx = lax.top_k(jnp.where(expert_mask, sel, -jnp.inf), TOP_K)
    wts = jnp.take_along_axis(scores, idx, axis=1)
    wts = wts / jnp.sum(wts, axis=-1, keepdims=True) * ROUTED_SCALE
    a = t * TOP_K
    flat_e = idx.reshape(a)
    flat_tok = jnp.arange(a, dtype=jnp.int32) // TOP_K
    flat_w = wts.reshape(a)
    order = jnp.argsort(flat_e, stable=True)
    sorted_e = flat_e[order]
    counts = jnp.zeros((N_EXPERTS,), jnp.int32).at[flat_e].add(1)
    start = jnp.cumsum(counts) - counts
    padded = (counts + EXPERT_BLOCK - 1) // EXPERT_BLOCK * EXPERT_BLOCK
    pend = jnp.cumsum(padded)
    pstart = pend - padded
    dest = pstart[sorted_e] + (jnp.arange(a, dtype=jnp.int32) - start[sorted_e])
    n_blocks = -(-a // EXPERT_BLOCK) + N_EXPERTS
    p = n_blocks * EXPERT_BLOCK
    slot_tok = jnp.full((p,), t, jnp.int32).at[dest].set(flat_tok[order])
    slot_w = jnp.zeros((p,), jnp.float32).at[dest].set(flat_w[order])
    block_start = jnp.arange(n_blocks, dtype=jnp.int32) * EXPERT_BLOCK
    block_expert = jnp.clip(jnp.searchsorted(pend, block_start, side='right'), 0, N_EXPERTS - 1)
    x_pad = jnp.concatenate([xf, jnp.zeros((1, d), xf.dtype)], axis=0)

    def expert_block(args):
        tok, e, wt = args
        xb = x_pad[tok]
        yb = swiglu(xb, w_gate[e], w_up[e], w_down[e])
        return yb * wt[:, None].astype(yb.dtype)

    yb = lax.map(expert_block, (slot_tok.reshape(n_blocks, EXPERT_BLOCK), block_expert,
                                slot_w.reshape(n_blocks, EXPERT_BLOCK)))
    routed = jax.ops.segment_sum(yb.reshape(p, d), slot_tok, num_segments=t + 1)[:t]
    shared = swiglu(xf, ws_gate, ws_up, ws_down)
    return (routed + shared).reshape(b, s, d)


def setup_inputs(seed: int = 0) -> dict:
    key = jax.random.key(seed)
    ks = jax.random.split(key, 24)
    f32 = jnp.float32

    def nrm(k, shape, scale):
        return jax.random.normal(k, shape, f32) * scale

    def gain(k, n):
        return 1.0 + 0.05 * jax.random.normal(k, (DEPTH, n), f32)

    return {
        'x': nrm(ks[0], (BATCH, SEQ, D_MODEL), 1.0),
        'c': nrm(ks[1], (BATCH, D_MODEL), 1.0),
        'w_ada': nrm(ks[2], (DEPTH, D_MODEL, 6 * D_MODEL), 0.5 * D_MODEL ** -0.5),
        'b_ada': nrm(ks[3], (DEPTH, 6 * D_MODEL), 0.02),
        'g_pre_mix': gain(ks[4], D_MODEL),
        'g_post_mix': gain(ks[5], D_MODEL),
        'g_pre_ffn': gain(ks[6], D_MODEL),
        'g_post_ffn': gain(ks[7], D_MODEL),
        'w_in': nrm(ks[8], (DEPTH, D_MODEL, D_IN), D_MODEL ** -0.5),
        'b_forget': jax.random.uniform(ks[9], (DEPTH, N_HEADS), f32, 1.0, 5.0),
        'w_pool': nrm(ks[10], (DEPTH, N_POOL_GROUPS, POOL_GROUP, POOL_GROUP), POOL_GROUP ** -0.5),
        'pool_scale': gain(ks[11], D_POOL) + 0.05 * jax.random.normal(ks[12], (DEPTH, D_POOL), f32),
        'g_pool_out': gain(ks[13], D_POOL),
        'g_attn_out': gain(ks[14], D_ATTN),
        'w_out': nrm(ks[15], (DEPTH, D_MIX, D_MODEL), D_MIX ** -0.5),
        'w_router': nrm(ks[16], (DEPTH, D_MODEL, N_EXPERTS), D_MODEL ** -0.5),
        'router_bias': nrm(ks[17], (DEPTH, N_EXPERTS), 0.01),
        'w_gate': nrm(ks[18], (DEPTH, N_EXPERTS, D_MODEL, D_EXPERT), D_MODEL ** -0.5),
        'w_up': nrm(ks[19], (DEPTH, N_EXPERTS, D_MODEL, D_EXPERT), D_MODEL ** -0.5),
        'w_down': nrm(ks[20], (DEPTH, N_EXPERTS, D_EXPERT, D_MODEL), D_EXPERT ** -0.5),
        'ws_gate': nrm(ks[21], (DEPTH, D_MODEL, D_SHARED), D_MODEL ** -0.5),
        'ws_up': nrm(ks[22], (DEPTH, D_MODEL, D_SHARED), D_MODEL ** -0.5),
        'ws_down': nrm(ks[23], (DEPTH, D_SHARED, D_MODEL), D_SHARED ** -0.5),
    }


def reference(x, c, w_ada, b_ada, g_pre_mix, g_post_mix, g_pre_ffn, g_post_ffn, w_in, b_forget,
              w_pool, pool_scale, g_pool_out, g_attn_out, w_out, w_router, router_bias,
              w_gate, w_up, w_down, ws_gate, ws_up, ws_down):
    b, s, _ = x.shape
    for l in range(DEPTH):
        mod = c @ w_ada[l] + b_ada[l]
        shift_m, scale_m, gate_m, shift_f, scale_f, gate_f = jnp.split(mod, 6, axis=-1)

        h = modulate(rms_norm(x, g_pre_mix[l]), shift_m, scale_m)
        proj = h @ w_in[l]
        u = proj[..., :D_POOL]
        q = proj[..., D_POOL:D_POOL + D_ATTN].reshape(b, s, N_HEADS, HEAD_DIM)
        k = proj[..., D_POOL + D_ATTN:D_POOL + 2 * D_ATTN].reshape(b, s, N_HEADS, HEAD_DIM)
        v = proj[..., D_POOL + 2 * D_ATTN:D_POOL + 3 * D_ATTN].reshape(b, s, N_HEADS, HEAD_DIM)
        logf = jax.nn.log_sigmoid((proj[..., D_POOL + 3 * D_ATTN:] + b_forget[l]).astype(jnp.float32))
        y_pool = pool_mixer(u, w_pool[l], pool_scale[l])
        y_attn = fox_attention(q, k, v, logf)
        mixed = jnp.concatenate([rms_norm(y_pool, g_pool_out[l]), rms_norm(y_attn, g_attn_out[l])], axis=-1)
        mixed = mixed @ w_out[l]
        x = x + gate_m[:, None, :] * rms_norm(mixed, g_post_mix[l])

        h2 = modulate(rms_norm(x, g_pre_ffn[l]), shift_f, scale_f)
        ff = moe_ffn(h2, w_router[l], router_bias[l], w_gate[l], w_up[l], w_down[l],
                     ws_gate[l], ws_up[l], ws_down[l])
        x = x + gate_f[:, None, :] * rms_norm(ff, g_post_ffn[l])
    return x
```

```python
import functools

import numpy as np
import jax
import jax.numpy as jnp
from jax import lax
from jax.experimental import pallas as pl
from jax.experimental.pallas import tpu as pltpu

D_MODEL = 1024
D_POOL = 512
POOL_WINDOWS = (2, 4, 8, 16)
POOL_GROUP = 128
MAX_WINDOW = max(POOL_WINDOWS)
D_ATTN = 512
HEAD_DIM = 64
N_HEADS = 8
N_EXPERTS = 64
N_EXPERT_GROUPS = 8
GROUP_SIZE = N_EXPERTS // N_EXPERT_GROUPS
TOPK_GROUPS = 4
TOP_K = 8
D_EXPERT = 256
D_SHARED = 256
ROUTED_SCALE = 2.5
EPS = 1e-6

LANES = 128
N_SPLIT = 3
AUG = LANES

TS_PRE = 256
TS_POST = 256
TQ = 512
TK = 512
TR = 512
TM = 512

F32 = jnp.float32
BF16 = jnp.bfloat16
VMEM_LIMIT = 56 * 1024 * 1024


def _rms(v, g):
    return v * lax.rsqrt(jnp.mean(v * v, axis=-1, keepdims=True) + EPS) * g


def _split3(v):
    hi = v.astype(BF16)
    r1 = v - hi.astype(F32)
    mid = r1.astype(BF16)
    r2 = r1 - mid.astype(F32)
    lo = r2.astype(BF16)
    return hi, mid, lo


def _dot(a, b):
    return jnp.dot(a, b, preferred_element_type=F32)


def _ada_kernel(c_ref, w_ref, b_ref, o_ref):
    o_ref[...] = _dot(c_ref[...].astype(BF16), w_ref[...].astype(BF16)) + b_ref[...]


def _ada(c, w, b):
    bsz = c.shape[0]
    n = w.shape[1]
    return pl.pallas_call(
        _ada_kernel,
        name="ada",
        grid=(n // D_MODEL,),
        in_specs=[
            pl.BlockSpec((bsz, D_MODEL), lambda j: (0, 0)),
            pl.BlockSpec((D_MODEL, D_MODEL), lambda j: (0, j)),
            pl.BlockSpec((1, D_MODEL), lambda j: (0, j)),
        ],
        out_specs=pl.BlockSpec((bsz, D_MODEL), lambda j: (0, j)),
        out_shape=jax.ShapeDtypeStruct((bsz, n), F32),
    )(c, w, b)


def _premix_kernel(x_ref, shift_ref, scale_ref, g_ref, w1_ref, wf_ref, bf_ref, wpool_ref,
                   pscale_ref, gpool_ref, eq_ref, ek_ref, oneq_ref, onek_ref, fmask_ref,
                   pool_ref, qt_ref, k_ref, vt_ref,
                   uext_ref, cum_ref):
    s = pl.program_id(1)
    ts = x_ref.shape[1]

    @pl.when(s == 0)
    def _():
        uext_ref[0:MAX_WINDOW, :] = jnp.zeros((MAX_WINDOW, D_POOL), F32)
        cum_ref[...] = jnp.zeros_like(cum_ref)

    x = x_ref[0]
    h = _rms(x, g_ref[...]) * (1.0 + scale_ref[0]) + shift_ref[0]
    hb = h.astype(BF16)
    proj = _dot(hb, w1_ref[...])
    u = proj[:, :D_POOL]
    q = proj[:, D_POOL:D_POOL + D_ATTN]
    k = proj[:, D_POOL + D_ATTN:D_POOL + 2 * D_ATTN]
    v = proj[:, D_POOL + 2 * D_ATTN:]

    uext_ref[MAX_WINDOW:, :] = u
    pos = (s * ts + lax.broadcasted_iota(jnp.int32, (ts, 1), 0) + 1).astype(F32)
    ys = []
    for g, w in enumerate(POOL_WINDOWS):
        c0 = g * POOL_GROUP
        acc = uext_ref[MAX_WINDOW:, c0:c0 + POOL_GROUP]
        for j in range(1, w):
            acc = acc + uext_ref[MAX_WINDOW - j:MAX_WINDOW - j + ts, c0:c0 + POOL_GROUP]
        pooled = acc / jnp.minimum(pos, float(w)) - u[:, c0:c0 + POOL_GROUP]
        ys.append(_dot(pooled.astype(BF16), wpool_ref[g]))
    ypool = jnp.concatenate(ys, axis=1) * pscale_ref[...]
    pool_ref[0] = _rms(ypool, gpool_ref[...]).astype(BF16)
    uext_ref[0:MAX_WINDOW, :] = uext_ref[ts:ts + MAX_WINDOW, :]

    z = _dot(hb, wf_ref[...]) + bf_ref[...]
    logf = jnp.minimum(z, 0.0) - jnp.log1p(jnp.exp(-jnp.abs(z)))
    row = lax.broadcasted_iota(jnp.int32, (ts, ts), 0)
    col = lax.broadcasted_iota(jnp.int32, (ts, ts), 1)
    tri = (col <= row).astype(BF16)
    cum = cum_ref[...]
    for piece in _split3(logf):
        cum = cum + _dot(tri, piece)
    cum_ref[...] = cum[ts - 1:ts, :]

    cpieces = _split3(cum)
    aug_q = oneq_ref[...]
    aug_k = onek_ref[...]
    for p in range(N_SPLIT):
        aug_q = aug_q + _dot(cpieces[p], eq_ref[p])
        aug_k = aug_k + _dot(cpieces[p], ek_ref[p])

    def expand(a):
        blocks = []
        for j in range(D_ATTN // LANES):
            blk = a[:, j * LANES:(j + 1) * LANES]
            blocks += [blk, blk]
        return jnp.concatenate(blocks, axis=1)

    fmask = fmask_ref[...]
    qa = expand(q) * fmask + aug_q
    ka = expand(k) * fmask + aug_k
    qt_ref[0] = qa.T.astype(BF16)
    k_ref[0] = ka.astype(BF16)
    vt_ref[0] = v.T.astype(BF16)


def _aug_constants():
    eq = np.zeros((N_SPLIT, LANES, N_HEADS * AUG), np.float32)
    ek = np.zeros((N_SPLIT, LANES, N_HEADS * AUG), np.float32)
    oneq = np.zeros((1, N_HEADS * AUG), np.float32)
    onek = np.zeros((1, N_HEADS * AUG), np.float32)
    fmask = np.zeros((1, N_HEADS * AUG), np.float32)
    for h in range(N_HEADS):
        feat0 = h * AUG + (0 if h % 2 == 0 else HEAD_DIM)
        aug0 = h * AUG + (HEAD_DIM if h % 2 == 0 else 0)
        fmask[0, feat0:feat0 + HEAD_DIM] = 1.0
        for p in range(N_SPLIT):
            eq[p, h, aug0 + p] = 1.0
            onek[0, aug0 + p] = 1.0
            ek[p, h, aug0 + N_SPLIT + p] = -1.0
            oneq[0, aug0 + N_SPLIT + p] = 1.0
    return (jnp.asarray(eq, BF16), jnp.asarray(ek, BF16), jnp.asarray(oneq), jnp.asarray(onek),
            jnp.asarray(fmask))


def _premix(x, shift, scale, g, w1, wf, bfp, wpool, pscale, gpool):
    bsz, seq, _ = x.shape
    ts = TS_PRE
    eq, ek, oneq, onek, fmask = _aug_constants()
    full = lambda shape: pl.BlockSpec(shape, lambda b, s: (0,) * len(shape))
    per_batch = pl.BlockSpec((1, 1, D_MODEL), lambda b, s: (b, 0, 0))
    return pl.pallas_call(
        _premix_kernel,
        name="premix",
        grid=(bsz, seq // ts),
        in_specs=[
            pl.BlockSpec((1, ts, D_MODEL), lambda b, s: (b, s, 0)),
            per_batch, per_batch,
            full((1, D_MODEL)),
            full((D_MODEL, D_POOL + 3 * D_ATTN)),
            full((D_MODEL, LANES)),
            full((1, LANES)),
            full((len(POOL_WINDOWS), POOL_GROUP, POOL_GROUP)),
            full((1, D_POOL)),
            full((1, D_POOL)),
            full((N_SPLIT, LANES, N_HEADS * AUG)),
            full((N_SPLIT, LANES, N_HEADS * AUG)),
            full((1, N_HEADS * AUG)),
            full((1, N_HEADS * AUG)),
            full((1, N_HEADS * AUG)),
        ],
        out_specs=[
            pl.BlockSpec((1, ts, D_POOL), lambda b, s: (b, s, 0)),
            pl.BlockSpec((1, N_HEADS * AUG, ts), lambda b, s: (b, 0, s)),
            pl.BlockSpec((1, ts, N_HEADS * AUG), lambda b, s: (b, s, 0)),
            pl.BlockSpec((1, D_ATTN, ts), lambda b, s: (b, 0, s)),
        ],
        out_shape=[
            jax.ShapeDtypeStruct((bsz, seq, D_POOL), BF16),
            jax.ShapeDtypeStruct((bsz, N_HEADS * AUG, seq), BF16),
            jax.ShapeDtypeStruct((bsz, seq, N_HEADS * AUG), BF16),
            jax.ShapeDtypeStruct((bsz, D_ATTN, seq), BF16),
        ],
        scratch_shapes=[
            pltpu.VMEM((ts + MAX_WINDOW, D_POOL), F32),
            pltpu.VMEM((1, LANES), F32),
        ],
        compiler_params=pltpu.CompilerParams(
            dimension_semantics=("arbitrary", "arbitrary"), vmem_limit_bytes=VMEM_LIMIT),
    )(x, shift, scale, g, w1, wf, bfp, wpool, pscale, gpool, eq, ek, oneq, onek, fmask)


def _attn_kernel(qt_ref, k_ref, vt_ref, o_ref):
    seq = k_ref.shape[1]
    key_i = lax.broadcasted_iota(jnp.int32, (TK, TQ), 0)
    qry_i = lax.broadcasted_iota(jnp.int32, (TK, TQ), 1)
    causal = key_i <= qry_i
    for qi in range(seq // TQ):
        qt = qt_ref[0, :, qi * TQ:(qi + 1) * TQ]
        m = jnp.full((1, TQ), -jnp.inf, F32)
        l = jnp.zeros((1, TQ), F32)
        acc = jnp.zeros((HEAD_DIM, TQ), F32)
        for kj in range(qi + 1):
            kt = k_ref[0, kj * TK:(kj + 1) * TK, :]
            st = _dot(kt, qt)
            if kj == qi:
                st = jnp.where(causal, st, -jnp.inf)
            m_new = jnp.maximum(m, jnp.max(st, axis=0, keepdims=True))
            alpha = jnp.exp(m - m_new)
            p = jnp.exp(st - m_new)
            l = alpha * l + jnp.sum(p, axis=0, keepdims=True)
            acc = alpha * acc + _dot(vt_ref[0, :, kj * TK:(kj + 1) * TK], p.astype(BF16))
            m = m_new
        o_ref[0, :, qi * TQ:(qi + 1) * TQ] = acc / l


def _attn(qt, ka, vt):
    bsz, _, seq = qt.shape
    return pl.pallas_call(
        _attn_kernel,
        name="attn",
        grid=(bsz, N_HEADS),
        in_specs=[
            pl.BlockSpec((1, AUG, seq), lambda b, h: (b, h, 0)),
            pl.BlockSpec((1, seq, AUG), lambda b, h: (b, 0, h)),
            pl.BlockSpec((1, HEAD_DIM, seq), lambda b, h: (b, h, 0)),
        ],
        out_specs=pl.BlockSpec((1, HEAD_DIM, seq), lambda b, h: (b, h, 0)),
        out_shape=jax.ShapeDtypeStruct((bsz, D_ATTN, seq), F32),
        compiler_params=pltpu.CompilerParams(
            dimension_semantics=("arbitrary", "arbitrary"), vmem_limit_bytes=VMEM_LIMIT),
    )(qt, ka, vt)


def _postmix_kernel(x_ref, pool_ref, ot_ref, wout_ref, gattn_ref, gpost_ref, gate_ref,
                    gffn_ref, shift_ref, scale_ref, wrh_ref, wrl_ref,
                    x1_ref, h2_ref, st_ref):
    ya = ot_ref[0].T
    ya = _rms(ya, gattn_ref[...]).astype(BF16)
    mixed = _dot(pool_ref[0], wout_ref[:D_POOL, :]) + _dot(ya, wout_ref[D_POOL:, :])
    x1 = x_ref[0] + gate_ref[0] * _rms(mixed, gpost_ref[...])
    x1_ref[0] = x1
    h2 = _rms(x1, gffn_ref[...]) * (1.0 + scale_ref[0]) + shift_ref[0]
    h2b = h2.astype(BF16)
    h2_ref[0] = h2b
    h2l = (h2 - h2b.astype(F32)).astype(BF16)
    logits = _dot(h2b, wrh_ref[...]) + _dot(h2b, wrl_ref[...]) + _dot(h2l, wrh_ref[...])
    st_ref[...] = jax.nn.sigmoid(logits).T


def _postmix(x, pool, ot, wout, gattn, gpost, gate, gffn, shift, scale, wrh, wrl):
    bsz, seq, _ = x.shape
    ts = TS_POST
    ns = seq // ts
    full = lambda shape: pl.BlockSpec(shape, lambda b, s: (0,) * len(shape))
    per_batch = pl.BlockSpec((1, 1, D_MODEL), lambda b, s: (b, 0, 0))
    tok = lambda width: pl.BlockSpec((1, ts, width), lambda b, s: (b, s, 0))
    return pl.pallas_call(
        _postmix_kernel,
        name="postmix",
        grid=(bsz, ns),
        in_specs=[
            tok(D_MODEL), tok(D_POOL),
            pl.BlockSpec((1, D_ATTN, ts), lambda b, s: (b, 0, s)),
            full((D_MODEL, D_MODEL)),
            full((1, D_ATTN)), full((1, D_MODEL)), per_batch,
            full((1, D_MODEL)), per_batch, per_batch,
            full((D_MODEL, LANES)), full((D_MODEL, LANES)),
        ],
        out_specs=[
            tok(D_MODEL), tok(D_MODEL),
            pl.BlockSpec((LANES, ts), lambda b, s: (0, b * ns + s)),
        ],
        out_shape=[
            jax.ShapeDtypeStruct((bsz, seq, D_MODEL), F32),
            jax.ShapeDtypeStruct((bsz, seq, D_MODEL), BF16),
            jax.ShapeDtypeStruct((LANES, bsz * seq), F32),
        ],
        compiler_params=pltpu.CompilerParams(
            dimension_semantics=("arbitrary", "arbitrary"), vmem_limit_bytes=VMEM_LIMIT),
    )(x, pool, ot, wout, gattn, gpost, gate, gffn, shift, scale, wrh, wrl)


def _route_kernel(st_ref, bias_ref, w_ref):
    tr = st_ref.shape[1]
    scores = [st_ref[g * GROUP_SIZE:(g + 1) * GROUP_SIZE, :] for g in range(N_EXPERT_GROUPS)]
    sel = [scores[g] + bias_ref[g * GROUP_SIZE:(g + 1) * GROUP_SIZE, :]
           for g in range(N_EXPERT_GROUPS)]
    sub = lax.broadcasted_iota(jnp.int32, (GROUP_SIZE, tr), 0)
    neg = jnp.float32(-jnp.inf)

    rows = []
    for g in range(N_EXPERT_GROUPS):
        v = sel[g]
        m1 = jnp.max(v, axis=0, keepdims=True)
        first = jnp.min(jnp.where(v == m1, sub, GROUP_SIZE), axis=0, keepdims=True)
        m2 = jnp.max(jnp.where(sub == first, neg, v), axis=0, keepdims=True)
        rows.append(m1 + m2)
    gs = jnp.concatenate(rows, axis=0)

    shape = (GROUP_SIZE, tr)
    one = jnp.ones(shape, jnp.int32)
    zero = jnp.zeros(shape, jnp.int32)
    beaten = zero
    for gp in range(N_EXPERT_GROUPS):
        r = jnp.broadcast_to(gs[gp:gp + 1, :], shape)
        tie = jnp.where(sub > gp, one, zero)
        beaten = beaten + jnp.where(r > gs, one, zero) + jnp.where(r == gs, tie, zero)
    keep = jnp.where(beaten < TOPK_GROUPS, 1.0, 0.0)

    masked = [jnp.where(jnp.broadcast_to(keep[g:g + 1, :], shape) > 0.5, sel[g], neg)
              for g in range(N_EXPERT_GROUPS)]

    cnt = [zero for _ in range(N_EXPERT_GROUPS)]
    for gp in range(N_EXPERT_GROUPS):
        for jp in range(GROUP_SIZE):
            r = jnp.broadcast_to(masked[gp][jp:jp + 1, :], shape)
            for g in range(N_EXPERT_GROUPS):
                if g < gp:
                    beats = jnp.where(r > masked[g], one, zero)
                elif g > gp:
                    beats = jnp.where(r >= masked[g], one, zero)
                else:
                    tie = jnp.where(sub > jp, one, zero)
                    beats = jnp.where(r > masked[g], one, zero) + jnp.where(r == masked[g], tie, zero)
                cnt[g] = cnt[g] + beats
    chosen = [cnt[g] < TOP_K for g in range(N_EXPERT_GROUPS)]

    w = [jnp.where(chosen[g], scores[g], 0.0) for g in range(N_EXPERT_GROUPS)]
    denom = w[0].sum(axis=0, keepdims=True)
    for g in range(1, N_EXPERT_GROUPS):
        denom = denom + w[g].sum(axis=0, keepdims=True)
    wn = [w[g] / denom * ROUTED_SCALE for g in range(N_EXPERT_GROUPS)]
    wt = jnp.concatenate(wn + [jnp.zeros((LANES - N_EXPERTS, tr), F32)], axis=0)
    w_ref[...] = wt.T


def _route(st, bias_col):
    t = st.shape[1]
    return pl.pallas_call(
        _route_kernel,
        name="route",
        grid=(t // TR,),
        in_specs=[
            pl.BlockSpec((LANES, TR), lambda i: (0, i)),
            pl.BlockSpec((N_EXPERTS, 1), lambda i: (0, 0)),
        ],
        out_specs=pl.BlockSpec((TR, LANES), lambda i: (i, 0)),
        out_shape=jax.ShapeDtypeStruct((t, LANES), F32),
        compiler_params=pltpu.CompilerParams(dimension_semantics=("arbitrary",)),
    )(st, bias_col)


def _swiglu(xb, wg, wu):
    g = _dot(xb, wg)
    return (g * jax.nn.sigmoid(g)) * _dot(xb, wu)


def _moe_kernel(h2_ref, w_ref, x1_ref, gate_ref, gpost_ref, wsg_ref, wsu_ref, wsd_ref,
                wg_ref, wu_ref, wd_ref, o_ref, acc_ref):
    e = pl.program_id(1)
    xb = h2_ref[...]

    @pl.when(e == 0)
    def _():
        hs = _swiglu(xb, wsg_ref[...], wsu_ref[...])
        acc_ref[...] = _dot(hs.astype(BF16), wsd_ref[...])

    lane = lax.broadcasted_iota(jnp.int32, w_ref.shape, 1)
    wcol = jnp.sum(jnp.where(lane == e, w_ref[...], 0.0), axis=1, keepdims=True)
    he = _swiglu(xb, wg_ref[0], wu_ref[0])
    acc_ref[...] += _dot(he.astype(BF16), wd_ref[0]) * wcol

    @pl.when(e == N_EXPERTS - 1)
    def _():
        o_ref[...] = x1_ref[...] + gate_ref[0] * _rms(acc_ref[...], gpost_ref[...])


def _moe(h2, w, x1, gate, gpost, wsg, wsu, wsd, wg, wu, wd, seq):
    t = h2.shape[0]
    per_seq = seq // TM
    full = lambda shape: pl.BlockSpec(shape, lambda i, e: (0,) * len(shape))
    tok = lambda width: pl.BlockSpec((TM, width), lambda i, e: (i, 0))
    return pl.pallas_call(
        _moe_kernel,
        name="moe",
        grid=(t // TM, N_EXPERTS),
        in_specs=[
            tok(D_MODEL), tok(LANES), tok(D_MODEL),
            pl.BlockSpec((1, 1, D_MODEL), lambda i, e: (i // per_seq, 0, 0)),
            full((1, D_MODEL)),
            full((D_MODEL, D_SHARED)), full((D_MODEL, D_SHARED)), full((D_SHARED, D_MODEL)),
            pl.BlockSpec((1, D_MODEL, D_EXPERT), lambda i, e: (e, 0, 0)),
            pl.BlockSpec((1, D_MODEL, D_EXPERT), lambda i, e: (e, 0, 0)),
            pl.BlockSpec((1, D_EXPERT, D_MODEL), lambda i, e: (e, 0, 0)),
        ],
        out_specs=tok(D_MODEL),
        out_shape=jax.ShapeDtypeStruct((t, D_MODEL), F32),
        scratch_shapes=[pltpu.VMEM((TM, D_MODEL), F32)],
        compiler_params=pltpu.CompilerParams(
            dimension_semantics=("arbitrary", "arbitrary"), vmem_limit_bytes=VMEM_LIMIT),
    )(h2, w, x1, gate, gpost, wsg, wsu, wsd, wg, wu, wd)


def kernel(x, c, w_ada, b_ada, g_pre_mix, g_post_mix, g_pre_ffn, g_post_ffn, w_in, b_forget,
           w_pool, pool_scale, g_pool_out, g_attn_out, w_out, w_router, router_bias,
           w_gate, w_up, w_down, ws_gate, ws_up, ws_down):
    bsz, seq, d = x.shape
    depth = w_ada.shape[0]
    for l in range(depth):
        mod = _ada(c, w_ada[l], b_ada[l][None, :])
        shift_m, scale_m, gate_m, shift_f, scale_f, gate_f = [
            m.reshape(bsz, 1, d) for m in jnp.split(mod, 6, axis=-1)]

        wi = w_in[l]
        qs = HEAD_DIM ** -0.5
        w1 = jnp.concatenate(
            [wi[:, :D_POOL], wi[:, D_POOL:D_POOL + D_ATTN] * qs,
             wi[:, D_POOL + D_ATTN:D_POOL + 3 * D_ATTN]], axis=1).astype(BF16)
        wf = jnp.pad(wi[:, D_POOL + 3 * D_ATTN:], ((0, 0), (0, LANES - N_HEADS))).astype(BF16)
        bfp = jnp.pad(b_forget[l], (0, LANES - N_HEADS))[None, :]

        pool, qt, ka, vt = _premix(
            x, shift_m, scale_m, g_pre_mix[l][None, :], w1, wf, bfp, w_pool[l].astype(BF16),
            pool_scale[l][None, :], g_pool_out[l][None, :])
        ot = _attn(qt, ka, vt)

        wr = jnp.pad(w_router[l], ((0, 0), (0, LANES - N_EXPERTS)))
        wrh = wr.astype(BF16)
        wrl = (wr - wrh.astype(F32)).astype(BF16)
        x1, h2, st = _postmix(
            x, pool, ot, w_out[l].astype(BF16), g_attn_out[l][None, :], g_post_mix[l][None, :],
            gate_m, g_pre_ffn[l][None, :], shift_f, scale_f, wrh, wrl)

        wtok = _route(st, router_bias[l][:, None])
        out = _moe(
            h2.reshape(bsz * seq, d), wtok, x1.reshape(bsz * seq, d), gate_f,
            g_post_ffn[l][None, :], ws_gate[l].astype(BF16), ws_up[l].astype(BF16),
            ws_down[l].astype(BF16), w_gate[l].astype(BF16), w_up[l].astype(BF16),
            w_down[l].astype(BF16), seq)
        x = out.reshape(bsz, seq, d)
    return x
```

```python
import functools

import numpy as np
import jax
import jax.numpy as jnp
from jax import lax
from jax.experimental import pallas as pl
from jax.experimental.pallas import tpu as pltpu

D_MODEL = 1024
D_POOL = 512
POOL_WINDOWS = (2, 4, 8, 16)
POOL_GROUP = 128
MAX_WINDOW = max(POOL_WINDOWS)
D_ATTN = 512
HEAD_DIM = 64
N_HEADS = 8
N_EXPERTS = 64
N_EXPERT_GROUPS = 8
GROUP_SIZE = N_EXPERTS // N_EXPERT_GROUPS
TOPK_GROUPS = 4
TOP_K = 8
D_EXPERT = 256
D_SHARED = 256
ROUTED_SCALE = 2.5
EPS = 1e-6

LANES = 128
N_SPLIT = 3
AUG = LANES

TS_PRE = 256
TS_POST = 256
TQ = 512
TK = 512
TW = 256
PIECE = 16
RB = 256
LROWS = -(-(TOP_K * TW + N_EXPERTS * (PIECE - 1)) // RB) * RB
LCHUNK = 512

F32 = jnp.float32
BF16 = jnp.bfloat16
VMEM_LIMIT = 56 * 1024 * 1024


def _rms(v, g):
    return v * lax.rsqrt(jnp.mean(v * v, axis=-1, keepdims=True) + EPS) * g


def _split3(v):
    hi = v.astype(BF16)
    r1 = v - hi.astype(F32)
    mid = r1.astype(BF16)
    r2 = r1 - mid.astype(F32)
    lo = r2.astype(BF16)
    return hi, mid, lo


def _dot(a, b):
    return jnp.dot(a, b, preferred_element_type=F32)


def _ada_kernel(c_ref, w_ref, b_ref, o_ref):
    o_ref[...] = _dot(c_ref[...].astype(BF16), w_ref[...].astype(BF16)) + b_ref[...]


def _ada(c, w, b):
    bsz = c.shape[0]
    n = w.shape[1]
    return pl.pallas_call(
        _ada_kernel,
        name="ada",
        grid=(n // D_MODEL,),
        in_specs=[
            pl.BlockSpec((bsz, D_MODEL), lambda j: (0, 0)),
            pl.BlockSpec((D_MODEL, D_MODEL), lambda j: (0, j)),
            pl.BlockSpec((1, D_MODEL), lambda j: (0, j)),
        ],
        out_specs=pl.BlockSpec((bsz, D_MODEL), lambda j: (0, j)),
        out_shape=jax.ShapeDtypeStruct((bsz, n), F32),
    )(c, w, b)


def _premix_kernel(x_ref, shift_ref, scale_ref, g_ref, w1_ref, wf_ref, bf_ref, wpool_ref,
                   pscale_ref, gpool_ref, eq_ref, ek_ref, oneq_ref, onek_ref, fmask_ref,
                   pool_ref, qt_ref, k_ref, vt_ref,
                   uext_ref, cum_ref):
    s = pl.program_id(1)
    ts = x_ref.shape[1]

    @pl.when(s == 0)
    def _():
        uext_ref[0:MAX_WINDOW, :] = jnp.zeros((MAX_WINDOW, D_POOL), F32)
        cum_ref[...] = jnp.zeros_like(cum_ref)

    x = x_ref[0]
    h = _rms(x, g_ref[...]) * (1.0 + scale_ref[0]) + shift_ref[0]
    hb = h.astype(BF16)
    proj = _dot(hb, w1_ref[...])
    u = proj[:, :D_POOL]
    q = proj[:, D_POOL:D_POOL + D_ATTN]
    k = proj[:, D_POOL + D_ATTN:D_POOL + 2 * D_ATTN]
    v = proj[:, D_POOL + 2 * D_ATTN:]

    uext_ref[MAX_WINDOW:, :] = u
    pos = (s * ts + lax.broadcasted_iota(jnp.int32, (ts, 1), 0) + 1).astype(F32)
    ys = []
    for g, w in enumerate(POOL_WINDOWS):
        c0 = g * POOL_GROUP
        acc = uext_ref[MAX_WINDOW:, c0:c0 + POOL_GROUP]
        for j in range(1, w):
            acc = acc + uext_ref[MAX_WINDOW - j:MAX_WINDOW - j + ts, c0:c0 + POOL_GROUP]
        pooled = acc / jnp.minimum(pos, float(w)) - u[:, c0:c0 + POOL_GROUP]
        ys.append(_dot(pooled.astype(BF16), wpool_ref[g]))
    ypool = jnp.concatenate(ys, axis=1) * pscale_ref[...]
    pool_ref[0] = _rms(ypool, gpool_ref[...]).astype(BF16)
    uext_ref[0:MAX_WINDOW, :] = uext_ref[ts:ts + MAX_WINDOW, :]

    z = _dot(hb, wf_ref[...]) + bf_ref[...]
    logf = jnp.minimum(z, 0.0) - jnp.log1p(jnp.exp(-jnp.abs(z)))
    row = lax.broadcasted_iota(jnp.int32, (ts, ts), 0)
    col = lax.broadcasted_iota(jnp.int32, (ts, ts), 1)
    tri = (col <= row).astype(BF16)
    cum = cum_ref[...]
    for piece in _split3(logf):
        cum = cum + _dot(tri, piece)
    cum_ref[...] = cum[ts - 1:ts, :]

    cpieces = _split3(cum)
    aug_q = oneq_ref[...]
    aug_k = onek_ref[...]
    for p in range(N_SPLIT):
        aug_q = aug_q + _dot(cpieces[p], eq_ref[p])
        aug_k = aug_k + _dot(cpieces[p], ek_ref[p])

    def expand(a):
        blocks = []
        for j in range(D_ATTN // LANES):
            blk = a[:, j * LANES:(j + 1) * LANES]
            blocks += [blk, blk]
        return jnp.concatenate(blocks, axis=1)

    fmask = fmask_ref[...]
    qa = expand(q) * fmask + aug_q
    ka = expand(k) * fmask + aug_k
    qt_ref[0] = qa.T.astype(BF16)
    k_ref[0] = ka.astype(BF16)
    vt_ref[0] = v.T.astype(BF16)


def _aug_constants():
    eq = np.zeros((N_SPLIT, LANES, N_HEADS * AUG), np.float32)
    ek = np.zeros((N_SPLIT, LANES, N_HEADS * AUG), np.float32)
    oneq = np.zeros((1, N_HEADS * AUG), np.float32)
    onek = np.zeros((1, N_HEADS * AUG), np.float32)
    fmask = np.zeros((1, N_HEADS * AUG), np.float32)
    for h in range(N_HEADS):
        feat0 = h * AUG + (0 if h % 2 == 0 else HEAD_DIM)
        aug0 = h * AUG + (HEAD_DIM if h % 2 == 0 else 0)
        fmask[0, feat0:feat0 + HEAD_DIM] = 1.0
        for p in range(N_SPLIT):
            eq[p, h, aug0 + p] = 1.0
            onek[0, aug0 + p] = 1.0
            ek[p, h, aug0 + N_SPLIT + p] = -1.0
            oneq[0, aug0 + N_SPLIT + p] = 1.0
    return (jnp.asarray(eq, BF16), jnp.asarray(ek, BF16), jnp.asarray(oneq), jnp.asarray(onek),
            jnp.asarray(fmask))


def _premix(x, shift, scale, g, w1, wf, bfp, wpool, pscale, gpool):
    bsz, seq, _ = x.shape
    ts = TS_PRE
    eq, ek, oneq, onek, fmask = _aug_constants()
    full = lambda shape: pl.BlockSpec(shape, lambda b, s: (0,) * len(shape))
    per_batch = pl.BlockSpec((1, 1, D_MODEL), lambda b, s: (b, 0, 0))
    return pl.pallas_call(
        _premix_kernel,
        name="premix",
        grid=(bsz, seq // ts),
        in_specs=[
            pl.BlockSpec((1, ts, D_MODEL), lambda b, s: (b, s, 0)),
            per_batch, per_batch,
            full((1, D_MODEL)),
            full((D_MODEL, D_POOL + 3 * D_ATTN)),
            full((D_MODEL, LANES)),
            full((1, LANES)),
            full((len(POOL_WINDOWS), POOL_GROUP, POOL_GROUP)),
            full((1, D_POOL)),
            full((1, D_POOL)),
            full((N_SPLIT, LANES, N_HEADS * AUG)),
            full((N_SPLIT, LANES, N_HEADS * AUG)),
            full((1, N_HEADS * AUG)),
            full((1, N_HEADS * AUG)),
            full((1, N_HEADS * AUG)),
        ],
        out_specs=[
            pl.BlockSpec((1, ts, D_POOL), lambda b, s: (b, s, 0)),
            pl.BlockSpec((1, N_HEADS * AUG, ts), lambda b, s: (b, 0, s)),
            pl.BlockSpec((1, ts, N_HEADS * AUG), lambda b, s: (b, s, 0)),
            pl.BlockSpec((1, D_ATTN, ts), lambda b, s: (b, 0, s)),
        ],
        out_shape=[
            jax.ShapeDtypeStruct((bsz, seq, D_POOL), BF16),
            jax.ShapeDtypeStruct((bsz, N_HEADS * AUG, seq), BF16),
            jax.ShapeDtypeStruct((bsz, seq, N_HEADS * AUG), BF16),
            jax.ShapeDtypeStruct((bsz, D_ATTN, seq), BF16),
        ],
        scratch_shapes=[
            pltpu.VMEM((ts + MAX_WINDOW, D_POOL), F32),
            pltpu.VMEM((1, LANES), F32),
        ],
        compiler_params=pltpu.CompilerParams(
            dimension_semantics=("arbitrary", "arbitrary"), vmem_limit_bytes=VMEM_LIMIT),
    )(x, shift, scale, g, w1, wf, bfp, wpool, pscale, gpool, eq, ek, oneq, onek, fmask)


def _attn_kernel(qt_ref, k_ref, vt_ref, o_ref):
    seq = k_ref.shape[1]
    key_i = lax.broadcasted_iota(jnp.int32, (TK, TQ), 0)
    qry_i = lax.broadcasted_iota(jnp.int32, (TK, TQ), 1)
    causal = key_i <= qry_i
    for qi in range(seq // TQ):
        qt = qt_ref[0, :, qi * TQ:(qi + 1) * TQ]
        m = jnp.full((1, TQ), -jnp.inf, F32)
        l = jnp.zeros((1, TQ), F32)
        acc = jnp.zeros((HEAD_DIM, TQ), F32)
        for kj in range(qi + 1):
            kt = k_ref[0, kj * TK:(kj + 1) * TK, :]
            st = _dot(kt, qt)
            if kj == qi:
                st = jnp.where(causal, st, -jnp.inf)
            m_new = jnp.maximum(m, jnp.max(st, axis=0, keepdims=True))
            alpha = jnp.exp(m - m_new)
            p = jnp.exp(st - m_new)
            l = alpha * l + jnp.sum(p, axis=0, keepdims=True)
            acc = alpha * acc + _dot(vt_ref[0, :, kj * TK:(kj + 1) * TK], p.astype(BF16))
            m = m_new
        o_ref[0, :, qi * TQ:(qi + 1) * TQ] = acc / l


def _attn(qt, ka, vt):
    bsz, _, seq = qt.shape
    return pl.pallas_call(
        _attn_kernel,
        name="attn",
        grid=(bsz, N_HEADS),
        in_specs=[
            pl.BlockSpec((1, AUG, seq), lambda b, h: (b, h, 0)),
            pl.BlockSpec((1, seq, AUG), lambda b, h: (b, 0, h)),
            pl.BlockSpec((1, HEAD_DIM, seq), lambda b, h: (b, h, 0)),
        ],
        out_specs=pl.BlockSpec((1, HEAD_DIM, seq), lambda b, h: (b, h, 0)),
        out_shape=jax.ShapeDtypeStruct((bsz, D_ATTN, seq), F32),
        compiler_params=pltpu.CompilerParams(
            dimension_semantics=("arbitrary", "arbitrary"), vmem_limit_bytes=VMEM_LIMIT),
    )(qt, ka, vt)


def _postmix_kernel(x_ref, pool_ref, ot_ref, wout_ref, gattn_ref, gpost_ref, gate_ref,
                    gffn_ref, shift_ref, scale_ref, wrh_ref, wrl_ref,
                    x1_ref, h2_ref, st_ref):
    ya = ot_ref[0].T
    ya = _rms(ya, gattn_ref[...]).astype(BF16)
    mixed = _dot(pool_ref[0], wout_ref[:D_POOL, :]) + _dot(ya, wout_ref[D_POOL:, :])
    x1 = x_ref[0] + gate_ref[0] * _rms(mixed, gpost_ref[...])
    x1_ref[0] = x1
    h2 = _rms(x1, gffn_ref[...]) * (1.0 + scale_ref[0]) + shift_ref[0]
    h2b = h2.astype(BF16)
    h2_ref[0] = h2b
    h2l = (h2 - h2b.astype(F32)).astype(BF16)
    logits = _dot(h2b, wrh_ref[...]) + _dot(h2b, wrl_ref[...]) + _dot(h2l, wrh_ref[...])
    st_ref[...] = jax.nn.sigmoid(logits).T


def _postmix(x, pool, ot, wout, gattn, gpost, gate, gffn, shift, scale, wrh, wrl):
    bsz, seq, _ = x.shape
    ts = TS_POST
    ns = seq // ts
    full = lambda shape: pl.BlockSpec(shape, lambda b, s: (0,) * len(shape))
    per_batch = pl.BlockSpec((1, 1, D_MODEL), lambda b, s: (b, 0, 0))
    tok = lambda width: pl.BlockSpec((1, ts, width), lambda b, s: (b, s, 0))
    return pl.pallas_call(
        _postmix_kernel,
        name="postmix",
        grid=(bsz, ns),
        in_specs=[
            tok(D_MODEL), tok(D_POOL),
            pl.BlockSpec((1, D_ATTN, ts), lambda b, s: (b, 0, s)),
            full((D_MODEL, D_MODEL)),
            full((1, D_ATTN)), full((1, D_MODEL)), per_batch,
            full((1, D_MODEL)), per_batch, per_batch,
            full((D_MODEL, LANES)), full((D_MODEL, LANES)),
        ],
        out_specs=[
            tok(D_MODEL), tok(D_MODEL),
            pl.BlockSpec((LANES, ts), lambda b, s: (0, b * ns + s)),
        ],
        out_shape=[
            jax.ShapeDtypeStruct((bsz, seq, D_MODEL), F32),
            jax.ShapeDtypeStruct((bsz, seq, D_MODEL), BF16),
            jax.ShapeDtypeStruct((LANES, bsz * seq), F32),
        ],
        compiler_params=pltpu.CompilerParams(
            dimension_semantics=("arbitrary", "arbitrary"), vmem_limit_bytes=VMEM_LIMIT),
    )(x, pool, ot, wout, gattn, gpost, gate, gffn, shift, scale, wrh, wrl)


def _route_kernel(st_ref, bias_ref, infok_ref, infot_ref, cnt_ref):
    tile = pl.program_id(0)
    tr = st_ref.shape[1]
    scores = [st_ref[g * GROUP_SIZE:(g + 1) * GROUP_SIZE, :] for g in range(N_EXPERT_GROUPS)]
    sel = [scores[g] + bias_ref[g * GROUP_SIZE:(g + 1) * GROUP_SIZE, :]
           for g in range(N_EXPERT_GROUPS)]
    sub = lax.broadcasted_iota(jnp.int32, (GROUP_SIZE, tr), 0)
    neg = jnp.float32(-jnp.inf)

    rows = []
    for g in range(N_EXPERT_GROUPS):
        v = sel[g]
        m1 = jnp.max(v, axis=0, keepdims=True)
        first = jnp.min(jnp.where(v == m1, sub, GROUP_SIZE), axis=0, keepdims=True)
        m2 = jnp.max(jnp.where(sub == first, neg, v), axis=0, keepdims=True)
        rows.append(m1 + m2)
    gs = jnp.concatenate(rows, axis=0)

    shape = (GROUP_SIZE, tr)
    one = jnp.ones(shape, jnp.int32)
    zero = jnp.zeros(shape, jnp.int32)
    beaten = zero
    for gp in range(N_EXPERT_GROUPS):
        r = jnp.broadcast_to(gs[gp:gp + 1, :], shape)
        tie = jnp.where(sub > gp, one, zero)
        beaten = beaten + jnp.where(r > gs, one, zero) + jnp.where(r == gs, tie, zero)
    keep = jnp.where(beaten < TOPK_GROUPS, 1.0, 0.0)

    masked = [jnp.where(jnp.broadcast_to(keep[g:g + 1, :], shape) > 0.5, sel[g], neg)
              for g in range(N_EXPERT_GROUPS)]

    cnt = [zero for _ in range(N_EXPERT_GROUPS)]
    for gp in range(N_EXPERT_GROUPS):
        for jp in range(GROUP_SIZE):
            r = jnp.broadcast_to(masked[gp][jp:jp + 1, :], shape)
            for g in range(N_EXPERT_GROUPS):
                if g < gp:
                    beats = jnp.where(r > masked[g], one, zero)
                elif g > gp:
                    beats = jnp.where(r >= masked[g], one, zero)
                else:
                    tie = jnp.where(sub > jp, one, zero)
                    beats = jnp.where(r > masked[g], one, zero) + jnp.where(r == masked[g], tie, zero)
                cnt[g] = cnt[g] + beats
    chosen = [cnt[g] < TOP_K for g in range(N_EXPERT_GROUPS)]

    w = [jnp.where(chosen[g], scores[g], 0.0) for g in range(N_EXPERT_GROUPS)]
    denom = w[0].sum(axis=0, keepdims=True)
    for g in range(1, N_EXPERT_GROUPS):
        denom = denom + w[g].sum(axis=0, keepdims=True)
    wfull = jnp.concatenate([w[g] / denom * ROUTED_SCALE for g in range(N_EXPERT_GROUPS)], axis=0)
    mfull = jnp.concatenate([jnp.where(chosen[g], 1.0, 0.0) for g in range(N_EXPERT_GROUPS)],
                            axis=0)
    mb = mfull.astype(BF16)

    e_r = lax.broadcasted_iota(jnp.int32, (N_EXPERTS, N_EXPERTS), 0)
    e_c = lax.broadcasted_iota(jnp.int32, (N_EXPERTS, N_EXPERTS), 1)
    before_e = (e_c < e_r).astype(BF16)
    t_r = lax.broadcasted_iota(jnp.int32, (tr, tr), 0)
    t_c = lax.broadcasted_iota(jnp.int32, (tr, tr), 1)
    before_t = (t_r < t_c).astype(BF16)
    ordinal = _dot(before_e, mb)
    rank = _dot(mb, before_t)
    n = jnp.sum(mfull, axis=1, keepdims=True)
    pieces = jnp.floor((n + (PIECE - 1.0)) * (1.0 / PIECE))
    run_start = PIECE * _dot(before_e, jnp.broadcast_to(pieces, (N_EXPERTS, LANES)).astype(BF16))
    pos = run_start[:, 0:1] + rank

    sub8 = lax.broadcasted_iota(jnp.int32, (TOP_K, tr), 0)
    pos8 = jnp.zeros((TOP_K, tr), F32)
    w8 = jnp.zeros((TOP_K, tr), F32)
    for k in range(TOP_K):
        selk = jnp.where(ordinal == float(k), mfull, 0.0)
        pk = jnp.sum(selk * pos, axis=0, keepdims=True)
        wk = jnp.sum(selk * wfull, axis=0, keepdims=True)
        pos8 = jnp.where(sub8 == k, jnp.broadcast_to(pk, (TOP_K, tr)), pos8)
        w8 = jnp.where(sub8 == k, jnp.broadcast_to(wk, (TOP_K, tr)), w8)
    infok_ref[...] = jnp.concatenate([pos8, w8], axis=0)
    info = jnp.concatenate([pos8, w8, jnp.zeros((LANES - 2 * TOP_K, tr), F32)], axis=0)
    infot_ref[...] = info.T

    @pl.when(tile == 0)
    def _():
        cnt_ref[...] = jnp.zeros_like(cnt_ref)

    lane = lax.broadcasted_iota(jnp.int32, cnt_ref.shape, 1)
    cnt_ref[...] = jnp.where(lane == tile, jnp.broadcast_to(n, cnt_ref.shape), cnt_ref[...])


def _route(st, bias_col):
    t = st.shape[1]
    assert t // TW <= LANES
    return pl.pallas_call(
        _route_kernel,
        name="route",
        grid=(t // TW,),
        in_specs=[
            pl.BlockSpec((LANES, TW), lambda i: (0, i)),
            pl.BlockSpec((N_EXPERTS, 1), lambda i: (0, 0)),
        ],
        out_specs=[
            pl.BlockSpec((2 * TOP_K, TW), lambda i: (0, i)),
            pl.BlockSpec((TW, LANES), lambda i: (i, 0)),
            pl.BlockSpec((N_EXPERTS, LANES), lambda i: (0, 0)),
        ],
        out_shape=[
            jax.ShapeDtypeStruct((2 * TOP_K, t), F32),
            jax.ShapeDtypeStruct((t, LANES), F32),
            jax.ShapeDtypeStruct((N_EXPERTS, LANES), F32),
        ],
        compiler_params=pltpu.CompilerParams(dimension_semantics=("arbitrary",)),
    )(st, bias_col)


def _swiglu(xb, wg, wu):
    g = _dot(xb, wg)
    return (g * jax.nn.sigmoid(g)) * _dot(xb, wu)


def _piece_copy(src_ref, src_row, dst_ref, dst_row, sem):
    src = src_ref.at[pl.ds(pl.multiple_of(src_row * PIECE, PIECE), PIECE), :]
    dst = dst_ref.at[pl.ds(pl.multiple_of(dst_row * PIECE, PIECE), PIECE), :]
    return pltpu.make_async_copy(src, dst, sem)


def _for_each_piece(n_ref, local_ref, global_ref, tile, fn):
    def per_expert(e, carry):
        idx = tile * N_EXPERTS + e
        loc = local_ref[idx]
        glob = global_ref[idx]

        def per_piece(p, inner):
            fn(loc + p, glob + p)
            return inner

        lax.fori_loop(0, n_ref[idx], per_piece, 0)
        return carry

    lax.fori_loop(0, N_EXPERTS, per_expert, 0)


def _dispatch_kernel(n_ref, local_ref, global_ref, tail_ref, ntail_ref,
                     h2_ref, infok_ref, xs_ref, lbuf, zbuf, sem, zsem):
    tile = pl.program_id(0)
    last = pl.num_programs(0) - 1
    slot = tile % 2

    def start(t, s):
        _for_each_piece(n_ref, local_ref, global_ref, t,
                        lambda lp, gp: _piece_copy(lbuf.at[s], lp, xs_ref, gp, sem.at[s]).start())

    def wait(t, s):
        _for_each_piece(n_ref, local_ref, global_ref, t,
                        lambda lp, gp: _piece_copy(lbuf.at[s], lp, xs_ref, gp, sem.at[s]).wait())

    @pl.when(tile == 0)
    def _():
        zbuf[...] = jnp.zeros_like(zbuf)

        def tails(fn):
            def per_expert(e, carry):
                def per_piece(p, inner):
                    fn(tail_ref[e] + p)
                    return inner
                lax.fori_loop(0, ntail_ref[e], per_piece, 0)
                return carry
            lax.fori_loop(0, N_EXPERTS, per_expert, 0)

        tails(lambda gp: _piece_copy(zbuf, 0, xs_ref, gp, zsem).start())
        tails(lambda gp: _piece_copy(zbuf, 0, xs_ref, gp, zsem).wait())

    @pl.when(tile >= 2)
    def _():
        wait(tile - 2, slot)

    xb = h2_ref[...]
    for j in range(LROWS // LCHUNK):
        r = (lax.broadcasted_iota(jnp.int32, (LCHUNK, TW), 0) + j * LCHUNK).astype(F32)
        onehot = jnp.zeros((LCHUNK, TW), F32)
        for k in range(TOP_K):
            onehot = onehot + jnp.where(r == infok_ref[k:k + 1, :], 1.0, 0.0)
        lbuf[slot, j * LCHUNK:(j + 1) * LCHUNK, :] = _dot(onehot.astype(BF16), xb).astype(BF16)

    start(tile, slot)

    @pl.when(tile == last)
    def _():
        @pl.when(tile >= 1)
        def _():
            wait(tile - 1, 1 - slot)
        wait(tile, slot)


def _dispatch(tables, h2, infok, n_rows):
    t = h2.shape[0]
    return pl.pallas_call(
        _dispatch_kernel,
        name="dispatch",
        grid_spec=pltpu.PrefetchScalarGridSpec(
            num_scalar_prefetch=5,
            grid=(t // TW,),
            in_specs=[
                pl.BlockSpec((TW, D_MODEL), lambda i, *_: (i, 0)),
                pl.BlockSpec((2 * TOP_K, TW), lambda i, *_: (0, i)),
            ],
            out_specs=pl.BlockSpec(memory_space=pl.ANY),
            scratch_shapes=[
                pltpu.VMEM((2, LROWS, D_MODEL), BF16),
                pltpu.VMEM((PIECE, D_MODEL), BF16),
                pltpu.SemaphoreType.DMA((2,)),
                pltpu.SemaphoreType.DMA(()),
            ],
        ),
        out_shape=jax.ShapeDtypeStruct((n_rows, D_MODEL), BF16),
        compiler_params=pltpu.CompilerParams(
            dimension_semantics=("arbitrary",), vmem_limit_bytes=VMEM_LIMIT),
    )(*tables, h2, infok)


def _experts_kernel(be_ref, nv_ref, xs_ref, wg_ref, wu_ref, wd_ref, ys_ref):
    @pl.when(pl.program_id(0) < nv_ref[0])
    def _():
        h = _swiglu(xs_ref[...], wg_ref[0], wu_ref[0])
        ys_ref[...] = _dot(h.astype(BF16), wd_ref[0]).astype(BF16)


def _experts(block_expert, n_valid, xs, wg, wu, wd):
    n_blocks = xs.shape[0] // RB
    rows = pl.BlockSpec((RB, D_MODEL), lambda i, be, nv: (jnp.minimum(i, nv[0] - 1), 0))
    return pl.pallas_call(
        _experts_kernel,
        name="experts",
        grid_spec=pltpu.PrefetchScalarGridSpec(
            num_scalar_prefetch=2,
            grid=(n_blocks,),
            in_specs=[
                rows,
                pl.BlockSpec((1, D_MODEL, D_EXPERT), lambda i, be, nv: (be[i], 0, 0)),
                pl.BlockSpec((1, D_MODEL, D_EXPERT), lambda i, be, nv: (be[i], 0, 0)),
                pl.BlockSpec((1, D_EXPERT, D_MODEL), lambda i, be, nv: (be[i], 0, 0)),
            ],
            out_specs=rows,
        ),
        out_shape=jax.ShapeDtypeStruct(xs.shape, BF16),
        compiler_params=pltpu.CompilerParams(
            dimension_semantics=("arbitrary",), vmem_limit_bytes=VMEM_LIMIT),
    )(block_expert, n_valid, xs, wg, wu, wd)


def _combine_kernel(n_ref, local_ref, global_ref,
                    infot_ref, h2_ref, x1_ref, gate_ref, gpost_ref, wsg_ref, wsu_ref, wsd_ref,
                    ys_ref, o_ref, ybuf, sem):
    tile = pl.program_id(0)
    last = pl.num_programs(0) - 1
    slot = tile % 2

    def start(t, s):
        _for_each_piece(n_ref, local_ref, global_ref, t,
                        lambda lp, gp: _piece_copy(ys_ref, gp, ybuf.at[s], lp, sem.at[s]).start())

    def wait(t, s):
        _for_each_piece(n_ref, local_ref, global_ref, t,
                        lambda lp, gp: _piece_copy(ys_ref, gp, ybuf.at[s], lp, sem.at[s]).wait())

    @pl.when(tile == 0)
    def _():
        ybuf[...] = jnp.zeros_like(ybuf)
        start(tile, slot)

    @pl.when(tile < last)
    def _():
        start(tile + 1, 1 - slot)

    wait(tile, slot)

    info = infot_ref[...]
    routed = jnp.zeros((TW, D_MODEL), F32)
    for j in range(LROWS // LCHUNK):
        r = (lax.broadcasted_iota(jnp.int32, (TW, LCHUNK), 1) + j * LCHUNK).astype(F32)
        wmat = jnp.zeros((TW, LCHUNK), F32)
        for k in range(TOP_K):
            wmat = wmat + jnp.where(r == info[:, k:k + 1], info[:, TOP_K + k:TOP_K + k + 1], 0.0)
        routed = routed + _dot(wmat.astype(BF16), ybuf[slot, j * LCHUNK:(j + 1) * LCHUNK, :])

    hs = _swiglu(h2_ref[...], wsg_ref[...], wsu_ref[...])
    ff = routed + _dot(hs.astype(BF16), wsd_ref[...])
    o_ref[...] = x1_ref[...] + gate_ref[0] * _rms(ff, gpost_ref[...])


def _combine(tables, infot, h2, x1, gate, gpost, wsg, wsu, wsd, ys, seq):
    t = h2.shape[0]
    per_seq = seq // TW
    full = lambda shape: pl.BlockSpec(shape, lambda i, *_: (0,) * len(shape))
    tok = lambda width: pl.BlockSpec((TW, width), lambda i, *_: (i, 0))
    return pl.pallas_call(
        _combine_kernel,
        name="combine",
        grid_spec=pltpu.PrefetchScalarGridSpec(
            num_scalar_prefetch=3,
            grid=(t // TW,),
            in_specs=[
                tok(LANES), tok(D_MODEL), tok(D_MODEL),
                pl.BlockSpec((1, 1, D_MODEL), lambda i, *_: (i // per_seq, 0, 0)),
                full((1, D_MODEL)),
                full((D_MODEL, D_SHARED)), full((D_MODEL, D_SHARED)), full((D_SHARED, D_MODEL)),
                pl.BlockSpec(memory_space=pl.ANY),
            ],
            out_specs=tok(D_MODEL),
            scratch_shapes=[
                pltpu.VMEM((2, LROWS, D_MODEL), BF16),
                pltpu.SemaphoreType.DMA((2,)),
            ],
        ),
        out_shape=jax.ShapeDtypeStruct((t, D_MODEL), F32),
        compiler_params=pltpu.CompilerParams(
            dimension_semantics=("arbitrary",), vmem_limit_bytes=VMEM_LIMIT),
    )(*tables, infot, h2, x1, gate, gpost, wsg, wsu, wsd, ys)


def _dispatch_plan(cnt, n_tiles):
    n = cnt[:, :n_tiles].astype(jnp.int32)
    pieces = (n + PIECE - 1) // PIECE
    local = jnp.cumsum(pieces, axis=0) - pieces
    seg = jnp.sum(pieces, axis=1)
    per_block = RB // PIECE
    seg_pad = (seg + per_block - 1) // per_block * per_block
    seg_end = jnp.cumsum(seg_pad)
    seg_start = seg_end - seg_pad
    glob = seg_start[:, None] + jnp.cumsum(pieces, axis=1) - pieces
    n_blocks = (TOP_K * TW * n_tiles + N_EXPERTS * n_tiles * (PIECE - 1)
                + N_EXPERTS * (RB - PIECE) + RB - 1) // RB
    n_valid = seg_end[-1] // per_block
    blk = jnp.minimum(jnp.arange(n_blocks, dtype=jnp.int32), n_valid - 1)
    block_end = seg_end // per_block
    block_expert = jnp.sum((block_end[None, :] <= blk[:, None]).astype(jnp.int32), axis=1)
    block_expert = jnp.minimum(block_expert, N_EXPERTS - 1)
    flat = lambda a: a.T.reshape(-1).astype(jnp.int32)
    tables = (flat(pieces), flat(local), flat(glob))
    tails = ((seg_start + seg).astype(jnp.int32), (seg_pad - seg).astype(jnp.int32))
    return tables, tails, block_expert, n_valid.reshape(1).astype(jnp.int32), n_blocks * RB


def kernel(x, c, w_ada, b_ada, g_pre_mix, g_post_mix, g_pre_ffn, g_post_ffn, w_in, b_forget,
           w_pool, pool_scale, g_pool_out, g_attn_out, w_out, w_router, router_bias,
           w_gate, w_up, w_down, ws_gate, ws_up, ws_down):
    bsz, seq, d = x.shape
    depth = w_ada.shape[0]
    for l in range(depth):
        mod = _ada(c, w_ada[l], b_ada[l][None, :])
        shift_m, scale_m, gate_m, shift_f, scale_f, gate_f = [
            m.reshape(bsz, 1, d) for m in jnp.split(mod, 6, axis=-1)]

        wi = w_in[l]
        qs = HEAD_DIM ** -0.5
        w1 = jnp.concatenate(
            [wi[:, :D_POOL], wi[:, D_POOL:D_POOL + D_ATTN] * qs,
             wi[:, D_POOL + D_ATTN:D_POOL + 3 * D_ATTN]], axis=1).astype(BF16)
        wf = jnp.pad(wi[:, D_POOL + 3 * D_ATTN:], ((0, 0), (0, LANES - N_HEADS))).astype(BF16)
        bfp = jnp.pad(b_forget[l], (0, LANES - N_HEADS))[None, :]

        pool, qt, ka, vt = _premix(
            x, shift_m, scale_m, g_pre_mix[l][None, :], w1, wf, bfp, w_pool[l].astype(BF16),
            pool_scale[l][None, :], g_pool_out[l][None, :])
        ot = _attn(qt, ka, vt)

        wr = jnp.pad(w_router[l], ((0, 0), (0, LANES - N_EXPERTS)))
        wrh = wr.astype(BF16)
        wrl = (wr - wrh.astype(F32)).astype(BF16)
        x1, h2, st = _postmix(
            x, pool, ot, w_out[l].astype(BF16), g_attn_out[l][None, :], g_post_mix[l][None, :],
            gate_m, g_pre_ffn[l][None, :], shift_f, scale_f, wrh, wrl)

        t = bsz * seq
        infok, infot, cnt = _route(st, router_bias[l][:, None])
        tables, tails, block_expert, n_valid, n_rows = _dispatch_plan(cnt, t // TW)
        h2f = h2.reshape(t, d)
        xs = _dispatch(tables + tails, h2f, infok, n_rows)
        ys = _experts(block_expert, n_valid, xs, w_gate[l].astype(BF16), w_up[l].astype(BF16),
                      w_down[l].astype(BF16))
        out = _combine(tables, infot, h2f, x1.reshape(t, d), gate_f, g_post_ffn[l][None, :],
                       ws_gate[l].astype(BF16), ws_up[l].astype(BF16), ws_down[l].astype(BF16),
                       ys, seq)
        x = out.reshape(bsz, seq, d)
    return x
```

```python
import functools

import numpy as np
import jax
import jax.numpy as jnp
from jax import lax
from jax.experimental import pallas as pl
from jax.experimental.pallas import tpu as pltpu

D_MODEL = 1024
D_POOL = 512
POOL_WINDOWS = (2, 4, 8, 16)
POOL_GROUP = 128
MAX_WINDOW = max(POOL_WINDOWS)
D_ATTN = 512
HEAD_DIM = 64
N_HEADS = 8
N_EXPERTS = 64
N_EXPERT_GROUPS = 8
GROUP_SIZE = N_EXPERTS // N_EXPERT_GROUPS
TOPK_GROUPS = 4
TOP_K = 8
D_EXPERT = 256
D_SHARED = 256
ROUTED_SCALE = 2.5
EPS = 1e-6

LANES = 128
N_SPLIT = 3
AUG = LANES

TS_PRE = 256
TS_POST = 256
TQ = 512
TK = 512
TW = 256
PIECE = 16
RB = 512
RSUB = 256
LCHUNK = 512
LROWS = -(-(TOP_K * TW + N_EXPERTS * (PIECE - 1)) // LCHUNK) * LCHUNK

F32 = jnp.float32
BF16 = jnp.bfloat16
VMEM_LIMIT = 56 * 1024 * 1024


def _rms(v, g):
    return v * lax.rsqrt(jnp.mean(v * v, axis=-1, keepdims=True) + EPS) * g


def _split3(v):
    hi = v.astype(BF16)
    r1 = v - hi.astype(F32)
    mid = r1.astype(BF16)
    r2 = r1 - mid.astype(F32)
    lo = r2.astype(BF16)
    return hi, mid, lo


def _dot(a, b):
    return jnp.dot(a, b, preferred_element_type=F32)


def _ada_kernel(c_ref, w_ref, b_ref, o_ref):
    o_ref[...] = _dot(c_ref[...].astype(BF16), w_ref[...].astype(BF16)) + b_ref[...]


def _ada(c, w, b):
    bsz = c.shape[0]
    n = w.shape[1]
    return pl.pallas_call(
        _ada_kernel,
        name="ada",
        grid=(n // D_MODEL,),
        in_specs=[
            pl.BlockSpec((bsz, D_MODEL), lambda j: (0, 0)),
            pl.BlockSpec((D_MODEL, D_MODEL), lambda j: (0, j)),
            pl.BlockSpec((1, D_MODEL), lambda j: (0, j)),
        ],
        out_specs=pl.BlockSpec((bsz, D_MODEL), lambda j: (0, j)),
        out_shape=jax.ShapeDtypeStruct((bsz, n), F32),
    )(c, w, b)


def _premix_kernel(x_ref, shift_ref, scale_ref, g_ref, w1_ref, wf_ref, bf_ref, wpool_ref,
                   pscale_ref, gpool_ref, eq_ref, ek_ref, oneq_ref, onek_ref, fmask_ref,
                   pool_ref, qt_ref, k_ref, vt_ref,
                   uext_ref, cum_ref):
    s = pl.program_id(1)
    ts = x_ref.shape[1]

    @pl.when(s == 0)
    def _():
        uext_ref[0:MAX_WINDOW, :] = jnp.zeros((MAX_WINDOW, D_POOL), F32)
        cum_ref[...] = jnp.zeros_like(cum_ref)

    x = x_ref[0]
    h = _rms(x, g_ref[...]) * (1.0 + scale_ref[0]) + shift_ref[0]
    hb = h.astype(BF16)
    proj = _dot(hb, w1_ref[...])
    u = proj[:, :D_POOL]
    q = proj[:, D_POOL:D_POOL + D_ATTN]
    k = proj[:, D_POOL + D_ATTN:D_POOL + 2 * D_ATTN]
    v = proj[:, D_POOL + 2 * D_ATTN:]

    uext_ref[MAX_WINDOW:, :] = u
    pos = (s * ts + lax.broadcasted_iota(jnp.int32, (ts, 1), 0) + 1).astype(F32)
    ys = []
    for g, w in enumerate(POOL_WINDOWS):
        c0 = g * POOL_GROUP
        acc = uext_ref[MAX_WINDOW:, c0:c0 + POOL_GROUP]
        for j in range(1, w):
            acc = acc + uext_ref[MAX_WINDOW - j:MAX_WINDOW - j + ts, c0:c0 + POOL_GROUP]
        pooled = acc / jnp.minimum(pos, float(w)) - u[:, c0:c0 + POOL_GROUP]
        ys.append(_dot(pooled.astype(BF16), wpool_ref[g]))
    ypool = jnp.concatenate(ys, axis=1) * pscale_ref[...]
    pool_ref[0] = _rms(ypool, gpool_ref[...]).astype(BF16)
    uext_ref[0:MAX_WINDOW, :] = uext_ref[ts:ts + MAX_WINDOW, :]

    z = _dot(hb, wf_ref[...]) + bf_ref[...]
    logf = jnp.minimum(z, 0.0) - jnp.log1p(jnp.exp(-jnp.abs(z)))
    row = lax.broadcasted_iota(jnp.int32, (ts, ts), 0)
    col = lax.broadcasted_iota(jnp.int32, (ts, ts), 1)
    tri = (col <= row).astype(BF16)
    cum = cum_ref[...]
    for piece in _split3(logf):
        cum = cum + _dot(tri, piece)
    cum_ref[...] = cum[ts - 1:ts, :]

    cpieces = _split3(cum)
    aug_q = oneq_ref[...]
    aug_k = onek_ref[...]
    for p in range(N_SPLIT):
        aug_q = aug_q + _dot(cpieces[p], eq_ref[p])
        aug_k = aug_k + _dot(cpieces[p], ek_ref[p])

    def expand(a):
        blocks = []
        for j in range(D_ATTN // LANES):
            blk = a[:, j * LANES:(j + 1) * LANES]
            blocks += [blk, blk]
        return jnp.concatenate(blocks, axis=1)

    fmask = fmask_ref[...]
    qa = expand(q) * fmask + aug_q
    ka = expand(k) * fmask + aug_k
    qt_ref[0] = qa.T.astype(BF16)
    k_ref[0] = ka.astype(BF16)
    vt_ref[0] = v.T.astype(BF16)


def _aug_constants():
    eq = np.zeros((N_SPLIT, LANES, N_HEADS * AUG), np.float32)
    ek = np.zeros((N_SPLIT, LANES, N_HEADS * AUG), np.float32)
    oneq = np.zeros((1, N_HEADS * AUG), np.float32)
    onek = np.zeros((1, N_HEADS * AUG), np.float32)
    fmask = np.zeros((1, N_HEADS * AUG), np.float32)
    for h in range(N_HEADS):
        feat0 = h * AUG + (0 if h % 2 == 0 else HEAD_DIM)
        aug0 = h * AUG + (HEAD_DIM if h % 2 == 0 else 0)
        fmask[0, feat0:feat0 + HEAD_DIM] = 1.0
        for p in range(N_SPLIT):
            eq[p, h, aug0 + p] = 1.0
            onek[0, aug0 + p] = 1.0
            ek[p, h, aug0 + N_SPLIT + p] = -1.0
            oneq[0, aug0 + N_SPLIT + p] = 1.0
    return (jnp.asarray(eq, BF16), jnp.asarray(ek, BF16), jnp.asarray(oneq), jnp.asarray(onek),
            jnp.asarray(fmask))


def _premix(x, shift, scale, g, w1, wf, bfp, wpool, pscale, gpool):
    bsz, seq, _ = x.shape
    ts = TS_PRE
    eq, ek, oneq, onek, fmask = _aug_constants()
    full = lambda shape: pl.BlockSpec(shape, lambda b, s: (0,) * len(shape))
    per_batch = pl.BlockSpec((1, 1, D_MODEL), lambda b, s: (b, 0, 0))
    return pl.pallas_call(
        _premix_kernel,
        name="premix",
        grid=(bsz, seq // ts),
        in_specs=[
            pl.BlockSpec((1, ts, D_MODEL), lambda b, s: (b, s, 0)),
            per_batch, per_batch,
            full((1, D_MODEL)),
            full((D_MODEL, D_POOL + 3 * D_ATTN)),
            full((D_MODEL, LANES)),
            full((1, LANES)),
            full((len(POOL_WINDOWS), POOL_GROUP, POOL_GROUP)),
            full((1, D_POOL)),
            full((1, D_POOL)),
            full((N_SPLIT, LANES, N_HEADS * AUG)),
            full((N_SPLIT, LANES, N_HEADS * AUG)),
            full((1, N_HEADS * AUG)),
            full((1, N_HEADS * AUG)),
            full((1, N_HEADS * AUG)),
        ],
        out_specs=[
            pl.BlockSpec((1, ts, D_POOL), lambda b, s: (b, s, 0)),
            pl.BlockSpec((1, N_HEADS * AUG, ts), lambda b, s: (b, 0, s)),
            pl.BlockSpec((1, ts, N_HEADS * AUG), lambda b, s: (b, s, 0)),
            pl.BlockSpec((1, D_ATTN, ts), lambda b, s: (b, 0, s)),
        ],
        out_shape=[
            jax.ShapeDtypeStruct((bsz, seq, D_POOL), BF16),
            jax.ShapeDtypeStruct((bsz, N_HEADS * AUG, seq), BF16),
            jax.ShapeDtypeStruct((bsz, seq, N_HEADS * AUG), BF16),
            jax.ShapeDtypeStruct((bsz, D_ATTN, seq), BF16),
        ],
        scratch_shapes=[
            pltpu.VMEM((ts + MAX_WINDOW, D_POOL), F32),
            pltpu.VMEM((1, LANES), F32),
        ],
        compiler_params=pltpu.CompilerParams(
            dimension_semantics=("arbitrary", "arbitrary"), vmem_limit_bytes=VMEM_LIMIT),
    )(x, shift, scale, g, w1, wf, bfp, wpool, pscale, gpool, eq, ek, oneq, onek, fmask)


def _attn_kernel(qt_ref, k_ref, vt_ref, o_ref):
    seq = k_ref.shape[1]
    key_i = lax.broadcasted_iota(jnp.int32, (TK, TQ), 0)
    qry_i = lax.broadcasted_iota(jnp.int32, (TK, TQ), 1)
    causal = key_i <= qry_i
    for qi in range(seq // TQ):
        qt = qt_ref[0, :, qi * TQ:(qi + 1) * TQ]
        m = jnp.full((1, TQ), -jnp.inf, F32)
        l = jnp.zeros((1, TQ), F32)
        acc = jnp.zeros((HEAD_DIM, TQ), F32)
        for kj in range(qi + 1):
            kt = k_ref[0, kj * TK:(kj + 1) * TK, :]
            st = _dot(kt, qt)
            if kj == qi:
                st = jnp.where(causal, st, -jnp.inf)
            m_new = jnp.maximum(m, jnp.max(st, axis=0, keepdims=True))
            alpha = jnp.exp(m - m_new)
            p = jnp.exp(st - m_new)
            l = alpha * l + jnp.sum(p, axis=0, keepdims=True)
            acc = alpha * acc + _dot(vt_ref[0, :, kj * TK:(kj + 1) * TK], p.astype(BF16))
            m = m_new
        o_ref[0, :, qi * TQ:(qi + 1) * TQ] = acc / l


def _attn(qt, ka, vt):
    bsz, _, seq = qt.shape
    return pl.pallas_call(
        _attn_kernel,
        name="attn",
        grid=(bsz, N_HEADS),
        in_specs=[
            pl.BlockSpec((1, AUG, seq), lambda b, h: (b, h, 0)),
            pl.BlockSpec((1, seq, AUG), lambda b, h: (b, 0, h)),
            pl.BlockSpec((1, HEAD_DIM, seq), lambda b, h: (b, h, 0)),
        ],
        out_specs=pl.BlockSpec((1, HEAD_DIM, seq), lambda b, h: (b, h, 0)),
        out_shape=jax.ShapeDtypeStruct((bsz, D_ATTN, seq), F32),
        compiler_params=pltpu.CompilerParams(
            dimension_semantics=("arbitrary", "arbitrary"), vmem_limit_bytes=VMEM_LIMIT),
    )(qt, ka, vt)


def _postmix_kernel(x_ref, pool_ref, ot_ref, wout_ref, gattn_ref, gpost_ref, gate_ref,
                    gffn_ref, shift_ref, scale_ref, wrh_ref, wrl_ref,
                    x1_ref, h2_ref, st_ref):
    ya = ot_ref[0].T
    ya = _rms(ya, gattn_ref[...]).astype(BF16)
    mixed = _dot(pool_ref[0], wout_ref[:D_POOL, :]) + _dot(ya, wout_ref[D_POOL:, :])
    x1 = x_ref[0] + gate_ref[0] * _rms(mixed, gpost_ref[...])
    x1_ref[0] = x1
    h2 = _rms(x1, gffn_ref[...]) * (1.0 + scale_ref[0]) + shift_ref[0]
    h2b = h2.astype(BF16)
    h2_ref[0] = h2b
    h2l = (h2 - h2b.astype(F32)).astype(BF16)
    logits = _dot(h2b, wrh_ref[...]) + _dot(h2b, wrl_ref[...]) + _dot(h2l, wrh_ref[...])
    st_ref[...] = jax.nn.sigmoid(logits).T


def _postmix(x, pool, ot, wout, gattn, gpost, gate, gffn, shift, scale, wrh, wrl):
    bsz, seq, _ = x.shape
    ts = TS_POST
    ns = seq // ts
    full = lambda shape: pl.BlockSpec(shape, lambda b, s: (0,) * len(shape))
    per_batch = pl.BlockSpec((1, 1, D_MODEL), lambda b, s: (b, 0, 0))
    tok = lambda width: pl.BlockSpec((1, ts, width), lambda b, s: (b, s, 0))
    return pl.pallas_call(
        _postmix_kernel,
        name="postmix",
        grid=(bsz, ns),
        in_specs=[
            tok(D_MODEL), tok(D_POOL),
            pl.BlockSpec((1, D_ATTN, ts), lambda b, s: (b, 0, s)),
            full((D_MODEL, D_MODEL)),
            full((1, D_ATTN)), full((1, D_MODEL)), per_batch,
            full((1, D_MODEL)), per_batch, per_batch,
            full((D_MODEL, LANES)), full((D_MODEL, LANES)),
        ],
        out_specs=[
            tok(D_MODEL), tok(D_MODEL),
            pl.BlockSpec((LANES, ts), lambda b, s: (0, b * ns + s)),
        ],
        out_shape=[
            jax.ShapeDtypeStruct((bsz, seq, D_MODEL), F32),
            jax.ShapeDtypeStruct((bsz, seq, D_MODEL), BF16),
            jax.ShapeDtypeStruct((LANES, bsz * seq), F32),
        ],
        compiler_params=pltpu.CompilerParams(
            dimension_semantics=("arbitrary", "arbitrary"), vmem_limit_bytes=VMEM_LIMIT),
    )(x, pool, ot, wout, gattn, gpost, gate, gffn, shift, scale, wrh, wrl)


def _route_kernel(st_ref, bias_ref, infok_ref, infot_ref, cnt_ref):
    tile = pl.program_id(0)
    tr = st_ref.shape[1]
    scores = [st_ref[g * GROUP_SIZE:(g + 1) * GROUP_SIZE, :] for g in range(N_EXPERT_GROUPS)]
    sel = [scores[g] + bias_ref[g * GROUP_SIZE:(g + 1) * GROUP_SIZE, :]
           for g in range(N_EXPERT_GROUPS)]
    sub = lax.broadcasted_iota(jnp.int32, (GROUP_SIZE, tr), 0)
    neg = jnp.float32(-jnp.inf)

    rows = []
    for g in range(N_EXPERT_GROUPS):
        v = sel[g]
        m1 = jnp.max(v, axis=0, keepdims=True)
        first = jnp.min(jnp.where(v == m1, sub, GROUP_SIZE), axis=0, keepdims=True)
        m2 = jnp.max(jnp.where(sub == first, neg, v), axis=0, keepdims=True)
        rows.append(m1 + m2)
    gs = jnp.concatenate(rows, axis=0)

    shape = (GROUP_SIZE, tr)
    one = jnp.ones(shape, jnp.int32)
    zero = jnp.zeros(shape, jnp.int32)
    beaten = zero
    for gp in range(N_EXPERT_GROUPS):
        r = jnp.broadcast_to(gs[gp:gp + 1, :], shape)
        tie = jnp.where(sub > gp, one, zero)
        beaten = beaten + jnp.where(r > gs, one, zero) + jnp.where(r == gs, tie, zero)
    keep = jnp.where(beaten < TOPK_GROUPS, 1.0, 0.0)

    masked = [jnp.where(jnp.broadcast_to(keep[g:g + 1, :], shape) > 0.5, sel[g], neg)
              for g in range(N_EXPERT_GROUPS)]

    cnt = [zero for _ in range(N_EXPERT_GROUPS)]
    for gp in range(N_EXPERT_GROUPS):
        for jp in range(GROUP_SIZE):
            r = jnp.broadcast_to(masked[gp][jp:jp + 1, :], shape)
            for g in range(N_EXPERT_GROUPS):
                if g < gp:
                    beats = jnp.where(r > masked[g], one, zero)
                elif g > gp:
                    beats = jnp.where(r >= masked[g], one, zero)
                else:
                    tie = jnp.where(sub > jp, one, zero)
                    beats = jnp.where(r > masked[g], one, zero) + jnp.where(r == masked[g], tie, zero)
                cnt[g] = cnt[g] + beats
    chosen = [cnt[g] < TOP_K for g in range(N_EXPERT_GROUPS)]

    w = [jnp.where(chosen[g], scores[g], 0.0) for g in range(N_EXPERT_GROUPS)]
    denom = w[0].sum(axis=0, keepdims=True)
    for g in range(1, N_EXPERT_GROUPS):
        denom = denom + w[g].sum(axis=0, keepdims=True)
    wfull = jnp.concatenate([w[g] / denom * ROUTED_SCALE for g in range(N_EXPERT_GROUPS)], axis=0)
    mfull = jnp.concatenate([jnp.where(chosen[g], 1.0, 0.0) for g in range(N_EXPERT_GROUPS)],
                            axis=0)
    mb = mfull.astype(BF16)

    e_r = lax.broadcasted_iota(jnp.int32, (N_EXPERTS, N_EXPERTS), 0)
    e_c = lax.broadcasted_iota(jnp.int32, (N_EXPERTS, N_EXPERTS), 1)
    before_e = (e_c < e_r).astype(BF16)
    t_r = lax.broadcasted_iota(jnp.int32, (tr, tr), 0)
    t_c = lax.broadcasted_iota(jnp.int32, (tr, tr), 1)
    before_t = (t_r < t_c).astype(BF16)
    ordinal = _dot(before_e, mb)
    rank = _dot(mb, before_t)
    n = jnp.sum(mfull, axis=1, keepdims=True)
    pieces = jnp.floor((n + (PIECE - 1.0)) * (1.0 / PIECE))
    run_start = PIECE * _dot(before_e, jnp.broadcast_to(pieces, (N_EXPERTS, LANES)).astype(BF16))
    pos = run_start[:, 0:1] + rank

    sub8 = lax.broadcasted_iota(jnp.int32, (TOP_K, tr), 0)
    pos8 = jnp.zeros((TOP_K, tr), F32)
    w8 = jnp.zeros((TOP_K, tr), F32)
    for k in range(TOP_K):
        selk = jnp.where(ordinal == float(k), mfull, 0.0)
        pk = jnp.sum(selk * pos, axis=0, keepdims=True)
        wk = jnp.sum(selk * wfull, axis=0, keepdims=True)
        pos8 = jnp.where(sub8 == k, jnp.broadcast_to(pk, (TOP_K, tr)), pos8)
        w8 = jnp.where(sub8 == k, jnp.broadcast_to(wk, (TOP_K, tr)), w8)
    infok_ref[...] = jnp.concatenate([pos8, w8], axis=0)
    info = jnp.concatenate([pos8, w8, jnp.zeros((LANES - 2 * TOP_K, tr), F32)], axis=0)
    infot_ref[...] = info.T

    @pl.when(tile == 0)
    def _():
        cnt_ref[...] = jnp.zeros_like(cnt_ref)

    lane = lax.broadcasted_iota(jnp.int32, cnt_ref.shape, 1)
    cnt_ref[...] = jnp.where(lane == tile, jnp.broadcast_to(n, cnt_ref.shape), cnt_ref[...])


def _route(st, bias_col):
    t = st.shape[1]
    assert t // TW <= LANES
    return pl.pallas_call(
        _route_kernel,
        name="route",
        grid=(t // TW,),
        in_specs=[
            pl.BlockSpec((LANES, TW), lambda i: (0, i)),
            pl.BlockSpec((N_EXPERTS, 1), lambda i: (0, 0)),
        ],
        out_specs=[
            pl.BlockSpec((2 * TOP_K, TW), lambda i: (0, i)),
            pl.BlockSpec((TW, LANES), lambda i: (i, 0)),
            pl.BlockSpec((N_EXPERTS, LANES), lambda i: (0, 0)),
        ],
        out_shape=[
            jax.ShapeDtypeStruct((2 * TOP_K, t), F32),
            jax.ShapeDtypeStruct((t, LANES), F32),
            jax.ShapeDtypeStruct((N_EXPERTS, LANES), F32),
        ],
        compiler_params=pltpu.CompilerParams(dimension_semantics=("arbitrary",)),
    )(st, bias_col)


def _swiglu(xb, wg, wu):
    g = _dot(xb, wg)
    return (g * jax.nn.sigmoid(g)) * _dot(xb, wu)


def _piece_copy(src_ref, src_row, dst_ref, dst_row, sem, n_pieces=1):
    rows = n_pieces * PIECE
    src = src_ref.at[pl.ds(pl.multiple_of(src_row * PIECE, PIECE), rows), :]
    dst = dst_ref.at[pl.ds(pl.multiple_of(dst_row * PIECE, PIECE), rows), :]
    return pltpu.make_async_copy(src, dst, sem)


def _for_each_piece(n_ref, local_ref, global_ref, tile, fn):
    def per_expert(e, carry):
        idx = tile * N_EXPERTS + e
        loc = local_ref[idx]
        glob = global_ref[idx]
        n = n_ref[idx]

        def per_pair(p, inner):
            fn(loc + 2 * p, glob + 2 * p, 2)
            return inner

        lax.fori_loop(0, lax.shift_right_logical(n, 1), per_pair, 0)

        @pl.when((n & 1) == 1)
        def _():
            fn(loc + n - 1, glob + n - 1, 1)

        return carry

    lax.fori_loop(0, N_EXPERTS, per_expert, 0)


def _dispatch_kernel(n_ref, local_ref, global_ref, tail_ref, ntail_ref, nv_ref,
                     h2_ref, infok_ref, xs_ref, lbuf, zbuf, sem, zsem):
    tile = pl.program_id(0)
    last = pl.num_programs(0) - 1
    slot = tile % 2
    n_blocks = xs_ref.shape[0] // RB

    def spare_blocks(fn):
        def per_block(b, carry):
            fn(b)
            return carry
        lax.fori_loop(nv_ref[0], n_blocks, per_block, 0)

    def zero_block(b):
        dst = xs_ref.at[pl.ds(pl.multiple_of(b * RB, RB), RB), :]
        return pltpu.make_async_copy(zbuf, dst, zsem)

    def start(t, s):
        _for_each_piece(
            n_ref, local_ref, global_ref, t,
            lambda lp, gp, n: _piece_copy(lbuf.at[s], lp, xs_ref, gp, sem.at[s], n).start())

    def wait(t, s):
        _for_each_piece(
            n_ref, local_ref, global_ref, t,
            lambda lp, gp, n: _piece_copy(lbuf.at[s], lp, xs_ref, gp, sem.at[s], n).wait())

    @pl.when(tile == 0)
    def _():
        zbuf[...] = jnp.zeros_like(zbuf)

        def tails(fn):
            def per_expert(e, carry):
                def per_piece(p, inner):
                    fn(tail_ref[e] + p)
                    return inner
                lax.fori_loop(0, ntail_ref[e], per_piece, 0)
                return carry
            lax.fori_loop(0, N_EXPERTS, per_expert, 0)

        tails(lambda gp: _piece_copy(zbuf, 0, xs_ref, gp, zsem).start())
        tails(lambda gp: _piece_copy(zbuf, 0, xs_ref, gp, zsem).wait())
        spare_blocks(lambda b: zero_block(b).start())

    @pl.when(tile >= 2)
    def _():
        wait(tile - 2, slot)

    xb = h2_ref[...]
    for j in range(LROWS // LCHUNK):
        r = (lax.broadcasted_iota(jnp.int32, (LCHUNK, TW), 0) + j * LCHUNK).astype(F32)
        onehot = jnp.zeros((LCHUNK, TW), F32)
        for k in range(TOP_K):
            onehot = jnp.where(r == infok_ref[k:k + 1, :], 1.0, onehot)
        lbuf[slot, j * LCHUNK:(j + 1) * LCHUNK, :] = _dot(onehot.astype(BF16), xb).astype(BF16)

    start(tile, slot)

    @pl.when(tile == last)
    def _():
        @pl.when(tile >= 1)
        def _():
            wait(tile - 1, 1 - slot)
        wait(tile, slot)
        spare_blocks(lambda b: zero_block(b).wait())


def _dispatch(tables, h2, infok, n_rows):
    t = h2.shape[0]
    return pl.pallas_call(
        _dispatch_kernel,
        name="dispatch",
        grid_spec=pltpu.PrefetchScalarGridSpec(
            num_scalar_prefetch=6,
            grid=(t // TW,),
            in_specs=[
                pl.BlockSpec((TW, D_MODEL), lambda i, *_: (i, 0)),
                pl.BlockSpec((2 * TOP_K, TW), lambda i, *_: (0, i)),
            ],
            out_specs=pl.BlockSpec(memory_space=pl.ANY),
            scratch_shapes=[
                pltpu.VMEM((2, LROWS, D_MODEL), BF16),
                pltpu.VMEM((RB, D_MODEL), BF16),
                pltpu.SemaphoreType.DMA((2,)),
                pltpu.SemaphoreType.DMA(()),
            ],
        ),
        out_shape=jax.ShapeDtypeStruct((n_rows, D_MODEL), BF16),
        compiler_params=pltpu.CompilerParams(
            dimension_semantics=("arbitrary",), vmem_limit_bytes=VMEM_LIMIT),
    )(*tables, h2, infok)


def _experts_kernel(be_ref, nv_ref, xs_ref, wg_ref, wu_ref, wd_ref, ys_ref, wgb, wub, wdb):
    i = pl.program_id(0)

    @pl.when(i < nv_ref[0])
    def _():
        @pl.when((i == 0) | (be_ref[i] != be_ref[jnp.maximum(i - 1, 0)]))
        def _():
            wgb[...] = wg_ref[0].astype(BF16)
            wub[...] = wu_ref[0].astype(BF16)
            wdb[...] = wd_ref[0].astype(BF16)

        for j in range(RB // RSUB):
            rows = slice(j * RSUB, (j + 1) * RSUB)
            h = _swiglu(xs_ref[rows, :], wgb[...], wub[...])
            ys_ref[rows, :] = _dot(h.astype(BF16), wdb[...]).astype(BF16)


def _experts(block_expert, n_valid, xs, wg, wu, wd):
    n_blocks = xs.shape[0] // RB
    rows = pl.BlockSpec((RB, D_MODEL), lambda i, be, nv: (jnp.minimum(i, nv[0] - 1), 0))
    return pl.pallas_call(
        _experts_kernel,
        name="experts",
        grid_spec=pltpu.PrefetchScalarGridSpec(
            num_scalar_prefetch=2,
            grid=(n_blocks,),
            in_specs=[
                rows,
                pl.BlockSpec((1, D_MODEL, D_EXPERT), lambda i, be, nv: (be[i], 0, 0)),
                pl.BlockSpec((1, D_MODEL, D_EXPERT), lambda i, be, nv: (be[i], 0, 0)),
                pl.BlockSpec((1, D_EXPERT, D_MODEL), lambda i, be, nv: (be[i], 0, 0)),
            ],
            out_specs=rows,
            scratch_shapes=[
                pltpu.VMEM((D_MODEL, D_EXPERT), BF16),
                pltpu.VMEM((D_MODEL, D_EXPERT), BF16),
                pltpu.VMEM((D_EXPERT, D_MODEL), BF16),
            ],
        ),
        out_shape=jax.ShapeDtypeStruct(xs.shape, BF16),
        input_output_aliases={2: 0},
        compiler_params=pltpu.CompilerParams(
            dimension_semantics=("arbitrary",), vmem_limit_bytes=VMEM_LIMIT),
    )(block_expert, n_valid, xs, wg, wu, wd)


def _combine_kernel(n_ref, local_ref, global_ref,
                    infot_ref, h2_ref, x1_ref, gate_ref, gpost_ref, wsg_ref, wsu_ref, wsd_ref,
                    ys_ref, o_ref, ybuf, sem):
    tile = pl.program_id(0)
    last = pl.num_programs(0) - 1
    slot = tile % 2

    def start(t, s):
        _for_each_piece(
            n_ref, local_ref, global_ref, t,
            lambda lp, gp, n: _piece_copy(ys_ref, gp, ybuf.at[s], lp, sem.at[s], n).start())

    def wait(t, s):
        _for_each_piece(
            n_ref, local_ref, global_ref, t,
            lambda lp, gp, n: _piece_copy(ys_ref, gp, ybuf.at[s], lp, sem.at[s], n).wait())

    @pl.when(tile == 0)
    def _():
        ybuf[...] = jnp.zeros_like(ybuf)
        start(tile, slot)

    @pl.when(tile < last)
    def _():
        start(tile + 1, 1 - slot)

    wait(tile, slot)

    info = infot_ref[...]
    routed = jnp.zeros((TW, D_MODEL), F32)
    for j in range(LROWS // LCHUNK):
        r = (lax.broadcasted_iota(jnp.int32, (TW, LCHUNK), 1) + j * LCHUNK).astype(F32)
        wmat = jnp.zeros((TW, LCHUNK), F32)
        for k in range(TOP_K):
            wmat = jnp.where(r == info[:, k:k + 1], info[:, TOP_K + k:TOP_K + k + 1], wmat)
        routed = routed + _dot(wmat.astype(BF16), ybuf[slot, j * LCHUNK:(j + 1) * LCHUNK, :])

    hs = _swiglu(h2_ref[...], wsg_ref[...], wsu_ref[...])
    ff = routed + _dot(hs.astype(BF16), wsd_ref[...])
    o_ref[...] = x1_ref[...] + gate_ref[0] * _rms(ff, gpost_ref[...])


def _combine(tables, infot, h2, x1, gate, gpost, wsg, wsu, wsd, ys, seq):
    t = h2.shape[0]
    per_seq = seq // TW
    full = lambda shape: pl.BlockSpec(shape, lambda i, *_: (0,) * len(shape))
    tok = lambda width: pl.BlockSpec((TW, width), lambda i, *_: (i, 0))
    return pl.pallas_call(
        _combine_kernel,
        name="combine",
        grid_spec=pltpu.PrefetchScalarGridSpec(
            num_scalar_prefetch=3,
            grid=(t // TW,),
            in_specs=[
                tok(LANES), tok(D_MODEL), tok(D_MODEL),
                pl.BlockSpec((1, 1, D_MODEL), lambda i, *_: (i // per_seq, 0, 0)),
                full((1, D_MODEL)),
                full((D_MODEL, D_SHARED)), full((D_MODEL, D_SHARED)), full((D_SHARED, D_MODEL)),
                pl.BlockSpec(memory_space=pl.ANY),
            ],
            out_specs=tok(D_MODEL),
            scratch_shapes=[
                pltpu.VMEM((2, LROWS, D_MODEL), BF16),
                pltpu.SemaphoreType.DMA((2,)),
            ],
        ),
        out_shape=jax.ShapeDtypeStruct((t, D_MODEL), F32),
        compiler_params=pltpu.CompilerParams(
            dimension_semantics=("arbitrary",), vmem_limit_bytes=VMEM_LIMIT),
    )(*tables, infot, h2, x1, gate, gpost, wsg, wsu, wsd, ys)


def _dispatch_plan(cnt, n_tiles):
    n = cnt[:, :n_tiles].astype(jnp.int32)
    pieces = (n + PIECE - 1) // PIECE
    local = jnp.cumsum(pieces, axis=0) - pieces
    seg = jnp.sum(pieces, axis=1)
    per_block = RB // PIECE
    seg_pad = (seg + per_block - 1) // per_block * per_block
    seg_end = jnp.cumsum(seg_pad)
    seg_start = seg_end - seg_pad
    glob = seg_start[:, None] + jnp.cumsum(pieces, axis=1) - pieces
    n_blocks = (TOP_K * TW * n_tiles + N_EXPERTS * n_tiles * (PIECE - 1)
                + N_EXPERTS * (RB - PIECE) + RB - 1) // RB
    n_valid = seg_end[-1] // per_block
    blk = jnp.minimum(jnp.arange(n_blocks, dtype=jnp.int32), n_valid - 1)
    block_end = seg_end // per_block
    block_expert = jnp.sum((block_end[None, :] <= blk[:, None]).astype(jnp.int32), axis=1)
    block_expert = jnp.minimum(block_expert, N_EXPERTS - 1)
    flat = lambda a: a.T.reshape(-1).astype(jnp.int32)
    tables = (flat(pieces), flat(local), flat(glob))
    tails = ((seg_start + seg).astype(jnp.int32), (seg_pad - seg).astype(jnp.int32))
    return tables, tails, block_expert, n_valid.reshape(1).astype(jnp.int32), n_blocks * RB


def kernel(x, c, w_ada, b_ada, g_pre_mix, g_post_mix, g_pre_ffn, g_post_ffn, w_in, b_forget,
           w_pool, pool_scale, g_pool_out, g_attn_out, w_out, w_router, router_bias,
           w_gate, w_up, w_down, ws_gate, ws_up, ws_down):
    bsz, seq, d = x.shape
    depth = w_ada.shape[0]
    for l in range(depth):
        mod = _ada(c, w_ada[l], b_ada[l][None, :])
        shift_m, scale_m, gate_m, shift_f, scale_f, gate_f = [
            m.reshape(bsz, 1, d) for m in jnp.split(mod, 6, axis=-1)]

        wi = w_in[l]
        qs = HEAD_DIM ** -0.5
        w1 = jnp.concatenate(
            [wi[:, :D_POOL], wi[:, D_POOL:D_POOL + D_ATTN] * qs,
             wi[:, D_POOL + D_ATTN:D_POOL + 3 * D_ATTN]], axis=1).astype(BF16)
        wf = jnp.pad(wi[:, D_POOL + 3 * D_ATTN:], ((0, 0), (0, LANES - N_HEADS))).astype(BF16)
        bfp = jnp.pad(b_forget[l], (0, LANES - N_HEADS))[None, :]

        pool, qt, ka, vt = _premix(
            x, shift_m, scale_m, g_pre_mix[l][None, :], w1, wf, bfp, w_pool[l].astype(BF16),
            pool_scale[l][None, :], g_pool_out[l][None, :])
        ot = _attn(qt, ka, vt)

        wr = jnp.pad(w_router[l], ((0, 0), (0, LANES - N_EXPERTS)))
        wrh = wr.astype(BF16)
        wrl = (wr - wrh.astype(F32)).astype(BF16)
        x1, h2, st = _postmix(
            x, pool, ot, w_out[l].astype(BF16), g_attn_out[l][None, :], g_post_mix[l][None, :],
            gate_m, g_pre_ffn[l][None, :], shift_f, scale_f, wrh, wrl)

        t = bsz * seq
        infok, infot, cnt = _route(st, router_bias[l][:, None])
        tables, tails, block_expert, n_valid, n_rows = _dispatch_plan(cnt, t // TW)
        h2f = h2.reshape(t, d)
        xs = _dispatch(tables + tails + (n_valid,), h2f, infok, n_rows)
        ys = _experts(block_expert, n_valid, xs, w_gate[l], w_up[l], w_down[l])
        out = _combine(tables, infot, h2f, x1.reshape(t, d), gate_f, g_post_ffn[l][None, :],
                       ws_gate[l].astype(BF16), ws_up[l].astype(BF16), ws_down[l].astype(BF16),
                       ys, seq)
        x = out.reshape(bsz, seq, d)
    return x
```

```python
import functools

import numpy as np
import jax
import jax.numpy as jnp
from jax import lax
from jax.experimental import pallas as pl
from jax.experimental.pallas import tpu as pltpu

D_MODEL = 1024
D_POOL = 512
POOL_WINDOWS = (2, 4, 8, 16)
POOL_GROUP = 128
MAX_WINDOW = max(POOL_WINDOWS)
D_ATTN = 512
HEAD_DIM = 64
N_HEADS = 8
N_EXPERTS = 64
N_EXPERT_GROUPS = 8
GROUP_SIZE = N_EXPERTS // N_EXPERT_GROUPS
TOPK_GROUPS = 4
TOP_K = 8
D_EXPERT = 256
D_SHARED = 256
ROUTED_SCALE = 2.5
EPS = 1e-6

LOG2E = 1.4426950408889634
SUM_ROWS = 16
LANES = 128
N_SPLIT = 3
AUG = LANES

TS_PRE = 256
TS_POST = 256
TQ = 512
TK = 512
TW = 256
PIECE = 16
RB = 512
RSUB = 256
LCHUNK = 512
LROWS = -(-(TOP_K * TW + N_EXPERTS * (PIECE - 1)) // LCHUNK) * LCHUNK

F32 = jnp.float32
BF16 = jnp.bfloat16
VMEM_LIMIT = 56 * 1024 * 1024


def _rms(v, g):
    return v * lax.rsqrt(jnp.mean(v * v, axis=-1, keepdims=True) + EPS) * g


def _split3(v):
    hi = v.astype(BF16)
    r1 = v - hi.astype(F32)
    mid = r1.astype(BF16)
    r2 = r1 - mid.astype(F32)
    lo = r2.astype(BF16)
    return hi, mid, lo


def _dot(a, b):
    return jnp.dot(a, b, preferred_element_type=F32)


def _ada_kernel(c_ref, w_ref, b_ref, o_ref):
    o_ref[...] = _dot(c_ref[...].astype(BF16), w_ref[...].astype(BF16)) + b_ref[...]


def _ada(c, w, b):
    bsz = c.shape[0]
    n = w.shape[1]
    return pl.pallas_call(
        _ada_kernel,
        name="ada",
        grid=(n // D_MODEL,),
        in_specs=[
            pl.BlockSpec((bsz, D_MODEL), lambda j: (0, 0)),
            pl.BlockSpec((D_MODEL, D_MODEL), lambda j: (0, j)),
            pl.BlockSpec((1, D_MODEL), lambda j: (0, j)),
        ],
        out_specs=pl.BlockSpec((bsz, D_MODEL), lambda j: (0, j)),
        out_shape=jax.ShapeDtypeStruct((bsz, n), F32),
    )(c, w, b)


def _premix_kernel(x_ref, shift_ref, scale_ref, g_ref, w1_ref, wf_ref, bf_ref, wpool_ref,
                   pscale_ref, gpool_ref, place_ref, ones_ref, fmask_ref,
                   pool_ref, qt_ref, k_ref, vt_ref,
                   uext_ref, cum_ref):
    s = pl.program_id(1)
    ts = x_ref.shape[1]

    @pl.when(s == 0)
    def _():
        uext_ref[0:MAX_WINDOW, :] = jnp.zeros((MAX_WINDOW, D_POOL), F32)
        cum_ref[...] = jnp.zeros_like(cum_ref)

    x = x_ref[0]
    h = _rms(x, g_ref[...]) * (1.0 + scale_ref[0]) + shift_ref[0]
    hb = h.astype(BF16)
    proj = _dot(hb, w1_ref[...])
    u = proj[:, :D_POOL]
    q = proj[:, D_POOL:D_POOL + D_ATTN]
    k = proj[:, D_POOL + D_ATTN:D_POOL + 2 * D_ATTN]
    v = proj[:, D_POOL + 2 * D_ATTN:]

    uext_ref[MAX_WINDOW:, :] = u
    pos = (s * ts + lax.broadcasted_iota(jnp.int32, (ts, 1), 0) + 1).astype(F32)
    ys = []
    for g, w in enumerate(POOL_WINDOWS):
        c0 = g * POOL_GROUP
        acc = uext_ref[MAX_WINDOW:, c0:c0 + POOL_GROUP]
        for j in range(1, w):
            acc = acc + uext_ref[MAX_WINDOW - j:MAX_WINDOW - j + ts, c0:c0 + POOL_GROUP]
        pooled = acc / jnp.minimum(pos, float(w)) - u[:, c0:c0 + POOL_GROUP]
        ys.append(_dot(pooled.astype(BF16), wpool_ref[g]))
    ypool = jnp.concatenate(ys, axis=1) * pscale_ref[...]
    pool_ref[0] = _rms(ypool, gpool_ref[...]).astype(BF16)
    uext_ref[0:MAX_WINDOW, :] = uext_ref[ts:ts + MAX_WINDOW, :]

    z = _dot(hb, wf_ref[...]) + bf_ref[...]
    logf = jnp.minimum(z, 0.0) - jnp.log1p(jnp.exp(-jnp.abs(z)))
    row = lax.broadcasted_iota(jnp.int32, (ts, ts), 0)
    col = lax.broadcasted_iota(jnp.int32, (ts, ts), 1)
    tri = (col <= row).astype(BF16)
    cum = cum_ref[...]
    for piece in _split3(logf):
        cum = cum + _dot(tri, piece)
    cum_ref[...] = cum[ts - 1:ts, :]

    hi, mid, lo = [piece.astype(F32) for piece in _split3(cum * LOG2E)]
    lane = lax.broadcasted_iota(jnp.int32, (ts, LANES), 1)
    pieces = jnp.where(lane < N_HEADS, hi, jnp.where(lane < 2 * N_HEADS, mid, lo))
    aug = _dot(pieces.astype(BF16), place_ref[...]) + ones_ref[...]
    aug_q = aug[:, :N_HEADS * AUG]
    aug_k = aug[:, N_HEADS * AUG:]

    def expand(a):
        blocks = []
        for j in range(D_ATTN // LANES):
            blk = a[:, j * LANES:(j + 1) * LANES]
            blocks += [blk, blk]
        return jnp.concatenate(blocks, axis=1)

    fmask = fmask_ref[...]
    qa = expand(q) * fmask + aug_q
    ka = expand(k) * fmask + aug_k
    qt_ref[0] = qa.T.astype(BF16)
    k_ref[0] = ka.astype(BF16)
    vt_ref[0] = v.T.astype(BF16)


def _aug_constants():
    width = N_HEADS * AUG
    place = np.zeros((LANES, 2 * width), np.float32)
    ones = np.zeros((1, 2 * width), np.float32)
    fmask = np.zeros((1, width), np.float32)
    for h in range(N_HEADS):
        feat0 = h * AUG + (0 if h % 2 == 0 else HEAD_DIM)
        aug0 = h * AUG + (HEAD_DIM if h % 2 == 0 else 0)
        fmask[0, feat0:feat0 + HEAD_DIM] = 1.0
        for p in range(N_SPLIT):
            place[p * N_HEADS + h, aug0 + p] = 1.0
            ones[0, width + aug0 + p] = 1.0
            place[p * N_HEADS + h, width + aug0 + N_SPLIT + p] = -1.0
            ones[0, aug0 + N_SPLIT + p] = 1.0
    return jnp.asarray(place, BF16), jnp.asarray(ones), jnp.asarray(fmask)


def _premix(x, shift, scale, g, w1, wf, bfp, wpool, pscale, gpool):
    bsz, seq, _ = x.shape
    ts = TS_PRE
    place, ones, fmask = _aug_constants()
    full = lambda shape: pl.BlockSpec(shape, lambda b, s: (0,) * len(shape))
    per_batch = pl.BlockSpec((1, 1, D_MODEL), lambda b, s: (b, 0, 0))
    return pl.pallas_call(
        _premix_kernel,
        name="premix",
        grid=(bsz, seq // ts),
        in_specs=[
            pl.BlockSpec((1, ts, D_MODEL), lambda b, s: (b, s, 0)),
            per_batch, per_batch,
            full((1, D_MODEL)),
            full((D_MODEL, D_POOL + 3 * D_ATTN)),
            full((D_MODEL, LANES)),
            full((1, LANES)),
            full((len(POOL_WINDOWS), POOL_GROUP, POOL_GROUP)),
            full((1, D_POOL)),
            full((1, D_POOL)),
            full((LANES, 2 * N_HEADS * AUG)),
            full((1, 2 * N_HEADS * AUG)),
            full((1, N_HEADS * AUG)),
        ],
        out_specs=[
            pl.BlockSpec((1, ts, D_POOL), lambda b, s: (b, s, 0)),
            pl.BlockSpec((1, N_HEADS * AUG, ts), lambda b, s: (b, 0, s)),
            pl.BlockSpec((1, ts, N_HEADS * AUG), lambda b, s: (b, s, 0)),
            pl.BlockSpec((1, D_ATTN, ts), lambda b, s: (b, 0, s)),
        ],
        out_shape=[
            jax.ShapeDtypeStruct((bsz, seq, D_POOL), BF16),
            jax.ShapeDtypeStruct((bsz, N_HEADS * AUG, seq), BF16),
            jax.ShapeDtypeStruct((bsz, seq, N_HEADS * AUG), BF16),
            jax.ShapeDtypeStruct((bsz, D_ATTN, seq), BF16),
        ],
        scratch_shapes=[
            pltpu.VMEM((ts + MAX_WINDOW, D_POOL), F32),
            pltpu.VMEM((1, LANES), F32),
        ],
        compiler_params=pltpu.CompilerParams(
            dimension_semantics=("arbitrary", "arbitrary"), vmem_limit_bytes=VMEM_LIMIT),
    )(x, shift, scale, g, w1, wf, bfp, wpool, pscale, gpool, place, ones, fmask)


def _attn_kernel(qt_ref, k_ref, vt_ref, o_ref):
    seq = k_ref.shape[1]
    key_i = lax.broadcasted_iota(jnp.int32, (TK, TQ), 0)
    qry_i = lax.broadcasted_iota(jnp.int32, (TK, TQ), 1)
    causal = key_i <= qry_i
    one_row = lax.broadcasted_iota(jnp.int32, (SUM_ROWS, seq), 0) == 0
    vaug = jnp.concatenate([vt_ref[0], jnp.where(one_row, 1.0, 0.0).astype(BF16)], axis=0)
    for qi in range(seq // TQ):
        qt = qt_ref[0, :, qi * TQ:(qi + 1) * TQ]
        m = jnp.full((1, TQ), -jnp.inf, F32)
        acc = jnp.zeros((HEAD_DIM + SUM_ROWS, TQ), F32)
        for kj in range(qi + 1):
            kt = k_ref[0, kj * TK:(kj + 1) * TK, :]
            st = _dot(kt, qt)
            if kj == qi:
                st = jnp.where(causal, st, -jnp.inf)
            m_new = jnp.maximum(m, jnp.max(st, axis=0, keepdims=True))
            p = jnp.exp2(st - m_new).astype(BF16)
            acc = jnp.exp2(m - m_new) * acc + _dot(vaug[:, kj * TK:(kj + 1) * TK], p)
            m = m_new
        o_ref[0, :, qi * TQ:(qi + 1) * TQ] = acc[:HEAD_DIM] / acc[HEAD_DIM:HEAD_DIM + 1]


def _attn(qt, ka, vt):
    bsz, _, seq = qt.shape
    return pl.pallas_call(
        _attn_kernel,
        name="attn",
        grid=(bsz, N_HEADS),
        in_specs=[
            pl.BlockSpec((1, AUG, seq), lambda b, h: (b, h, 0)),
            pl.BlockSpec((1, seq, AUG), lambda b, h: (b, 0, h)),
            pl.BlockSpec((1, HEAD_DIM, seq), lambda b, h: (b, h, 0)),
        ],
        out_specs=pl.BlockSpec((1, HEAD_DIM, seq), lambda b, h: (b, h, 0)),
        out_shape=jax.ShapeDtypeStruct((bsz, D_ATTN, seq), F32),
        compiler_params=pltpu.CompilerParams(
            dimension_semantics=("arbitrary", "arbitrary"), vmem_limit_bytes=VMEM_LIMIT),
    )(qt, ka, vt)


def _postmix_kernel(x_ref, pool_ref, ot_ref, wout_ref, gattn_ref, gpost_ref, gate_ref,
                    gffn_ref, shift_ref, scale_ref, wrh_ref, wrl_ref,
                    x1_ref, h2_ref, st_ref):
    ya = ot_ref[0].T
    ya = _rms(ya, gattn_ref[...]).astype(BF16)
    mixed = _dot(pool_ref[0], wout_ref[:D_POOL, :]) + _dot(ya, wout_ref[D_POOL:, :])
    x1 = x_ref[0] + gate_ref[0] * _rms(mixed, gpost_ref[...])
    x1_ref[0] = x1
    h2 = _rms(x1, gffn_ref[...]) * (1.0 + scale_ref[0]) + shift_ref[0]
    h2b = h2.astype(BF16)
    h2_ref[0] = h2b
    h2l = (h2 - h2b.astype(F32)).astype(BF16)
    logits = _dot(h2b, wrh_ref[...]) + _dot(h2b, wrl_ref[...]) + _dot(h2l, wrh_ref[...])
    st_ref[...] = jax.nn.sigmoid(logits).T


def _postmix(x, pool, ot, wout, gattn, gpost, gate, gffn, shift, scale, wrh, wrl):
    bsz, seq, _ = x.shape
    ts = TS_POST
    ns = seq // ts
    full = lambda shape: pl.BlockSpec(shape, lambda b, s: (0,) * len(shape))
    per_batch = pl.BlockSpec((1, 1, D_MODEL), lambda b, s: (b, 0, 0))
    tok = lambda width: pl.BlockSpec((1, ts, width), lambda b, s: (b, s, 0))
    return pl.pallas_call(
        _postmix_kernel,
        name="postmix",
        grid=(bsz, ns),
        in_specs=[
            tok(D_MODEL), tok(D_POOL),
            pl.BlockSpec((1, D_ATTN, ts), lambda b, s: (b, 0, s)),
            full((D_MODEL, D_MODEL)),
            full((1, D_ATTN)), full((1, D_MODEL)), per_batch,
            full((1, D_MODEL)), per_batch, per_batch,
            full((D_MODEL, LANES)), full((D_MODEL, LANES)),
        ],
        out_specs=[
            tok(D_MODEL), tok(D_MODEL),
            pl.BlockSpec((LANES, ts), lambda b, s: (0, b * ns + s)),
        ],
        out_shape=[
            jax.ShapeDtypeStruct((bsz, seq, D_MODEL), F32),
            jax.ShapeDtypeStruct((bsz, seq, D_MODEL), BF16),
            jax.ShapeDtypeStruct((LANES, bsz * seq), F32),
        ],
        compiler_params=pltpu.CompilerParams(
            dimension_semantics=("arbitrary", "arbitrary"), vmem_limit_bytes=VMEM_LIMIT),
    )(x, pool, ot, wout, gattn, gpost, gate, gffn, shift, scale, wrh, wrl)


def _route_kernel(st_ref, bias_ref, infok_ref, infot_ref, cnt_ref):
    tile = pl.program_id(0)
    tr = st_ref.shape[1]
    scores = [st_ref[g * GROUP_SIZE:(g + 1) * GROUP_SIZE, :] for g in range(N_EXPERT_GROUPS)]
    sel = [scores[g] + bias_ref[g * GROUP_SIZE:(g + 1) * GROUP_SIZE, :]
           for g in range(N_EXPERT_GROUPS)]
    sub = lax.broadcasted_iota(jnp.int32, (GROUP_SIZE, tr), 0)
    neg = jnp.float32(-jnp.inf)

    rows = []
    for g in range(N_EXPERT_GROUPS):
        v = sel[g]
        m1 = jnp.max(v, axis=0, keepdims=True)
        first = jnp.min(jnp.where(v == m1, sub, GROUP_SIZE), axis=0, keepdims=True)
        m2 = jnp.max(jnp.where(sub == first, neg, v), axis=0, keepdims=True)
        rows.append(m1 + m2)
    gs = jnp.concatenate(rows, axis=0)

    shape = (GROUP_SIZE, tr)
    one = jnp.ones(shape, jnp.int32)
    zero = jnp.zeros(shape, jnp.int32)
    beaten = zero
    for gp in range(N_EXPERT_GROUPS):
        r = jnp.broadcast_to(gs[gp:gp + 1, :], shape)
        tie = jnp.where(sub > gp, one, zero)
        beaten = beaten + jnp.where(r > gs, one, zero) + jnp.where(r == gs, tie, zero)
    keep = jnp.where(beaten < TOPK_GROUPS, 1.0, 0.0)

    masked = [jnp.where(jnp.broadcast_to(keep[g:g + 1, :], shape) > 0.5, sel[g], neg)
              for g in range(N_EXPERT_GROUPS)]

    cnt = [zero for _ in range(N_EXPERT_GROUPS)]
    for gp in range(N_EXPERT_GROUPS):
        for jp in range(GROUP_SIZE):
            r = jnp.broadcast_to(masked[gp][jp:jp + 1, :], shape)
            for g in range(N_EXPERT_GROUPS):
                if g < gp:
                    beats = jnp.where(r > masked[g], one, zero)
                elif g > gp:
                    beats = jnp.where(r >= masked[g], one, zero)
                else:
                    tie = jnp.where(sub > jp, one, zero)
                    beats = jnp.where(r > masked[g], one, zero) + jnp.where(r == masked[g], tie, zero)
                cnt[g] = cnt[g] + beats
    chosen = [cnt[g] < TOP_K for g in range(N_EXPERT_GROUPS)]

    w = [jnp.where(chosen[g], scores[g], 0.0) for g in range(N_EXPERT_GROUPS)]
    denom = w[0].sum(axis=0, keepdims=True)
    for g in range(1, N_EXPERT_GROUPS):
        denom = denom + w[g].sum(axis=0, keepdims=True)
    wfull = jnp.concatenate([w[g] / denom * ROUTED_SCALE for g in range(N_EXPERT_GROUPS)], axis=0)
    mfull = jnp.concatenate([jnp.where(chosen[g], 1.0, 0.0) for g in range(N_EXPERT_GROUPS)],
                            axis=0)
    mb = mfull.astype(BF16)

    e_r = lax.broadcasted_iota(jnp.int32, (N_EXPERTS, N_EXPERTS), 0)
    e_c = lax.broadcasted_iota(jnp.int32, (N_EXPERTS, N_EXPERTS), 1)
    before_e = (e_c < e_r).astype(BF16)
    t_r = lax.broadcasted_iota(jnp.int32, (tr, tr), 0)
    t_c = lax.broadcasted_iota(jnp.int32, (tr, tr), 1)
    before_t = (t_r < t_c).astype(BF16)
    ordinal = _dot(before_e, mb)
    rank = _dot(mb, before_t)
    n = jnp.sum(mfull, axis=1, keepdims=True)
    pieces = jnp.floor((n + (PIECE - 1.0)) * (1.0 / PIECE))
    run_start = PIECE * _dot(before_e, jnp.broadcast_to(pieces, (N_EXPERTS, LANES)).astype(BF16))
    pos = run_start[:, 0:1] + rank

    sub8 = lax.broadcasted_iota(jnp.int32, (TOP_K, tr), 0)
    pos8 = jnp.zeros((TOP_K, tr), F32)
    w8 = jnp.zeros((TOP_K, tr), F32)
    for k in range(TOP_K):
        selk = jnp.where(ordinal == float(k), mfull, 0.0)
        pk = jnp.sum(selk * pos, axis=0, keepdims=True)
        wk = jnp.sum(selk * wfull, axis=0, keepdims=True)
        pos8 = jnp.where(sub8 == k, jnp.broadcast_to(pk, (TOP_K, tr)), pos8)
        w8 = jnp.where(sub8 == k, jnp.broadcast_to(wk, (TOP_K, tr)), w8)
    infok_ref[...] = jnp.concatenate([pos8, w8], axis=0)
    info = jnp.concatenate([pos8, w8, jnp.zeros((LANES - 2 * TOP_K, tr), F32)], axis=0)
    infot_ref[...] = info.T

    @pl.when(tile == 0)
    def _():
        cnt_ref[...] = jnp.zeros_like(cnt_ref)

    lane = lax.broadcasted_iota(jnp.int32, cnt_ref.shape, 1)
    cnt_ref[...] = jnp.where(lane == tile, jnp.broadcast_to(n, cnt_ref.shape), cnt_ref[...])


def _route(st, bias_col):
    t = st.shape[1]
    assert t // TW <= LANES
    return pl.pallas_call(
        _route_kernel,
        name="route",
        grid=(t // TW,),
        in_specs=[
            pl.BlockSpec((LANES, TW), lambda i: (0, i)),
            pl.BlockSpec((N_EXPERTS, 1), lambda i: (0, 0)),
        ],
        out_specs=[
            pl.BlockSpec((2 * TOP_K, TW), lambda i: (0, i)),
            pl.BlockSpec((TW, LANES), lambda i: (i, 0)),
            pl.BlockSpec((N_EXPERTS, LANES), lambda i: (0, 0)),
        ],
        out_shape=[
            jax.ShapeDtypeStruct((2 * TOP_K, t), F32),
            jax.ShapeDtypeStruct((t, LANES), F32),
            jax.ShapeDtypeStruct((N_EXPERTS, LANES), F32),
        ],
        compiler_params=pltpu.CompilerParams(dimension_semantics=("arbitrary",)),
    )(st, bias_col)


def _swiglu(xb, wg, wu):
    g = _dot(xb, wg)
    return (g * jax.nn.sigmoid(g)) * _dot(xb, wu)


def _piece_copy(src_ref, src_row, dst_ref, dst_row, sem, n_pieces=1):
    rows = n_pieces * PIECE
    src = src_ref.at[pl.ds(pl.multiple_of(src_row * PIECE, PIECE), rows), :]
    dst = dst_ref.at[pl.ds(pl.multiple_of(dst_row * PIECE, PIECE), rows), :]
    return pltpu.make_async_copy(src, dst, sem)


def _for_each_piece(n_ref, local_ref, global_ref, tile, fn):
    def per_expert(e, carry):
        idx = tile * N_EXPERTS + e
        loc = local_ref[idx]
        glob = global_ref[idx]
        n = n_ref[idx]

        def per_pair(p, inner):
            fn(loc + 2 * p, glob + 2 * p, 2)
            return inner

        lax.fori_loop(0, lax.shift_right_logical(n, 1), per_pair, 0)

        @pl.when((n & 1) == 1)
        def _():
            fn(loc + n - 1, glob + n - 1, 1)

        return carry

    lax.fori_loop(0, N_EXPERTS, per_expert, 0)


def _dispatch_kernel(n_ref, local_ref, global_ref, tail_ref, ntail_ref, nv_ref,
                     h2_ref, infok_ref, xs_ref, lbuf, zbuf, sem, zsem):
    tile = pl.program_id(0)
    last = pl.num_programs(0) - 1
    slot = tile % 2
    n_blocks = xs_ref.shape[0] // RB

    def spare_blocks(fn):
        def per_block(b, carry):
            fn(b)
            return carry
        lax.fori_loop(nv_ref[0], n_blocks, per_block, 0)

    def zero_block(b):
        dst = xs_ref.at[pl.ds(pl.multiple_of(b * RB, RB), RB), :]
        return pltpu.make_async_copy(zbuf, dst, zsem)

    def start(t, s):
        _for_each_piece(
            n_ref, local_ref, global_ref, t,
            lambda lp, gp, n: _piece_copy(lbuf.at[s], lp, xs_ref, gp, sem.at[s], n).start())

    def wait(t, s):
        _for_each_piece(
            n_ref, local_ref, global_ref, t,
            lambda lp, gp, n: _piece_copy(lbuf.at[s], lp, xs_ref, gp, sem.at[s], n).wait())

    @pl.when(tile == 0)
    def _():
        zbuf[...] = jnp.zeros_like(zbuf)

        def tails(fn):
            def per_expert(e, carry):
                def per_piece(p, inner):
                    fn(tail_ref[e] + p)
                    return inner
                lax.fori_loop(0, ntail_ref[e], per_piece, 0)
                return carry
            lax.fori_loop(0, N_EXPERTS, per_expert, 0)

        tails(lambda gp: _piece_copy(zbuf, 0, xs_ref, gp, zsem).start())
        tails(lambda gp: _piece_copy(zbuf, 0, xs_ref, gp, zsem).wait())
        spare_blocks(lambda b: zero_block(b).start())

    @pl.when(tile >= 2)
    def _():
        wait(tile - 2, slot)

    xb = h2_ref[...]
    for j in range(LROWS // LCHUNK):
        r = (lax.broadcasted_iota(jnp.int32, (LCHUNK, TW), 0) + j * LCHUNK).astype(F32)
        onehot = jnp.zeros((LCHUNK, TW), F32)
        for k in range(TOP_K):
            onehot = jnp.where(r == infok_ref[k:k + 1, :], 1.0, onehot)
        lbuf[slot, j * LCHUNK:(j + 1) * LCHUNK, :] = _dot(onehot.astype(BF16), xb).astype(BF16)

    start(tile, slot)

    @pl.when(tile == last)
    def _():
        @pl.when(tile >= 1)
        def _():
            wait(tile - 1, 1 - slot)
        wait(tile, slot)
        spare_blocks(lambda b: zero_block(b).wait())


def _dispatch(tables, h2, infok, n_rows):
    t = h2.shape[0]
    return pl.pallas_call(
        _dispatch_kernel,
        name="dispatch",
        grid_spec=pltpu.PrefetchScalarGridSpec(
            num_scalar_prefetch=6,
            grid=(t // TW,),
            in_specs=[
                pl.BlockSpec((TW, D_MODEL), lambda i, *_: (i, 0)),
                pl.BlockSpec((2 * TOP_K, TW), lambda i, *_: (0, i)),
            ],
            out_specs=pl.BlockSpec(memory_space=pl.ANY),
            scratch_shapes=[
                pltpu.VMEM((2, LROWS, D_MODEL), BF16),
                pltpu.VMEM((RB, D_MODEL), BF16),
                pltpu.SemaphoreType.DMA((2,)),
                pltpu.SemaphoreType.DMA(()),
            ],
        ),
        out_shape=jax.ShapeDtypeStruct((n_rows, D_MODEL), BF16),
        compiler_params=pltpu.CompilerParams(
            dimension_semantics=("arbitrary",), vmem_limit_bytes=VMEM_LIMIT),
    )(*tables, h2, infok)


def _experts_kernel(be_ref, nv_ref, xs_ref, wg_ref, wu_ref, wd_ref, ys_ref, wgb, wub, wdb):
    i = pl.program_id(0)

    @pl.when(i < nv_ref[0])
    def _():
        @pl.when((i == 0) | (be_ref[i] != be_ref[jnp.maximum(i - 1, 0)]))
        def _():
            wgb[...] = wg_ref[0].astype(BF16)
            wub[...] = wu_ref[0].astype(BF16)
            wdb[...] = wd_ref[0].astype(BF16)

        for j in range(RB // RSUB):
            rows = slice(j * RSUB, (j + 1) * RSUB)
            h = _swiglu(xs_ref[rows, :], wgb[...], wub[...])
            ys_ref[rows, :] = _dot(h.astype(BF16), wdb[...]).astype(BF16)


def _experts(block_expert, n_valid, xs, wg, wu, wd):
    n_blocks = xs.shape[0] // RB
    rows = pl.BlockSpec((RB, D_MODEL), lambda i, be, nv: (jnp.minimum(i, nv[0] - 1), 0))
    return pl.pallas_call(
        _experts_kernel,
        name="experts",
        grid_spec=pltpu.PrefetchScalarGridSpec(
            num_scalar_prefetch=2,
            grid=(n_blocks,),
            in_specs=[
                rows,
                pl.BlockSpec((1, D_MODEL, D_EXPERT), lambda i, be, nv: (be[i], 0, 0)),
                pl.BlockSpec((1, D_MODEL, D_EXPERT), lambda i, be, nv: (be[i], 0, 0)),
                pl.BlockSpec((1, D_EXPERT, D_MODEL), lambda i, be, nv: (be[i], 0, 0)),
            ],
            out_specs=rows,
            scratch_shapes=[
                pltpu.VMEM((D_MODEL, D_EXPERT), BF16),
                pltpu.VMEM((D_MODEL, D_EXPERT), BF16),
                pltpu.VMEM((D_EXPERT, D_MODEL), BF16),
            ],
        ),
        out_shape=jax.ShapeDtypeStruct(xs.shape, BF16),
        input_output_aliases={2: 0},
        compiler_params=pltpu.CompilerParams(
            dimension_semantics=("arbitrary",), vmem_limit_bytes=VMEM_LIMIT),
    )(block_expert, n_valid, xs, wg, wu, wd)


def _combine_kernel(n_ref, local_ref, global_ref,
                    infot_ref, h2_ref, x1_ref, gate_ref, gpost_ref, wsg_ref, wsu_ref, wsd_ref,
                    ys_ref, o_ref, ybuf, sem):
    tile = pl.program_id(0)
    last = pl.num_programs(0) - 1
    slot = tile % 2

    def start(t, s):
        _for_each_piece(
            n_ref, local_ref, global_ref, t,
            lambda lp, gp, n: _piece_copy(ys_ref, gp, ybuf.at[s], lp, sem.at[s], n).start())

    def wait(t, s):
        _for_each_piece(
            n_ref, local_ref, global_ref, t,
            lambda lp, gp, n: _piece_copy(ys_ref, gp, ybuf.at[s], lp, sem.at[s], n).wait())

    @pl.when(tile == 0)
    def _():
        ybuf[...] = jnp.zeros_like(ybuf)
        start(tile, slot)

    @pl.when(tile < last)
    def _():
        start(tile + 1, 1 - slot)

    wait(tile, slot)

    info = infot_ref[...]
    routed = jnp.zeros((TW, D_MODEL), F32)
    for j in range(LROWS // LCHUNK):
        r = (lax.broadcasted_iota(jnp.int32, (TW, LCHUNK), 1) + j * LCHUNK).astype(F32)
        wmat = jnp.zeros((TW, LCHUNK), F32)
        for k in range(TOP_K):
            wmat = jnp.where(r == info[:, k:k + 1], info[:, TOP_K + k:TOP_K + k + 1], wmat)
        routed = routed + _dot(wmat.astype(BF16), ybuf[slot, j * LCHUNK:(j + 1) * LCHUNK, :])

    hs = _swiglu(h2_ref[...], wsg_ref[...], wsu_ref[...])
    ff = routed + _dot(hs.astype(BF16), wsd_ref[...])
    o_ref[...] = x1_ref[...] + gate_ref[0] * _rms(ff, gpost_ref[...])


def _combine(tables, infot, h2, x1, gate, gpost, wsg, wsu, wsd, ys, seq):
    t = h2.shape[0]
    per_seq = seq // TW
    full = lambda shape: pl.BlockSpec(shape, lambda i, *_: (0,) * len(shape))
    tok = lambda width: pl.BlockSpec((TW, width), lambda i, *_: (i, 0))
    return pl.pallas_call(
        _combine_kernel,
        name="combine",
        grid_spec=pltpu.PrefetchScalarGridSpec(
            num_scalar_prefetch=3,
            grid=(t // TW,),
            in_specs=[
                tok(LANES), tok(D_MODEL), tok(D_MODEL),
                pl.BlockSpec((1, 1, D_MODEL), lambda i, *_: (i // per_seq, 0, 0)),
                full((1, D_MODEL)),
                full((D_MODEL, D_SHARED)), full((D_MODEL, D_SHARED)), full((D_SHARED, D_MODEL)),
                pl.BlockSpec(memory_space=pl.ANY),
            ],
            out_specs=tok(D_MODEL),
            scratch_shapes=[
                pltpu.VMEM((2, LROWS, D_MODEL), BF16),
                pltpu.SemaphoreType.DMA((2,)),
            ],
        ),
        out_shape=jax.ShapeDtypeStruct((t, D_MODEL), F32),
        compiler_params=pltpu.CompilerParams(
            dimension_semantics=("arbitrary",), vmem_limit_bytes=VMEM_LIMIT),
    )(*tables, infot, h2, x1, gate, gpost, wsg, wsu, wsd, ys)


def _dispatch_plan(cnt, n_tiles):
    n = cnt[:, :n_tiles].astype(jnp.int32)
    pieces = (n + PIECE - 1) // PIECE
    local = jnp.cumsum(pieces, axis=0) - pieces
    seg = jnp.sum(pieces, axis=1)
    per_block = RB // PIECE
    seg_pad = (seg + per_block - 1) // per_block * per_block
    seg_end = jnp.cumsum(seg_pad)
    seg_start = seg_end - seg_pad
    glob = seg_start[:, None] + jnp.cumsum(pieces, axis=1) - pieces
    n_blocks = (TOP_K * TW * n_tiles + N_EXPERTS * n_tiles * (PIECE - 1)
                + N_EXPERTS * (RB - PIECE) + RB - 1) // RB
    n_valid = seg_end[-1] // per_block
    blk = jnp.minimum(jnp.arange(n_blocks, dtype=jnp.int32), n_valid - 1)
    block_end = seg_end // per_block
    block_expert = jnp.sum((block_end[None, :] <= blk[:, None]).astype(jnp.int32), axis=1)
    block_expert = jnp.minimum(block_expert, N_EXPERTS - 1)
    flat = lambda a: a.T.reshape(-1).astype(jnp.int32)
    tables = (flat(pieces), flat(local), flat(glob))
    tails = ((seg_start + seg).astype(jnp.int32), (seg_pad - seg).astype(jnp.int32))
    return tables, tails, block_expert, n_valid.reshape(1).astype(jnp.int32), n_blocks * RB


def kernel(x, c, w_ada, b_ada, g_pre_mix, g_post_mix, g_pre_ffn, g_post_ffn, w_in, b_forget,
           w_pool, pool_scale, g_pool_out, g_attn_out, w_out, w_router, router_bias,
           w_gate, w_up, w_down, ws_gate, ws_up, ws_down):
    bsz, seq, d = x.shape
    depth = w_ada.shape[0]
    for l in range(depth):
        mod = _ada(c, w_ada[l], b_ada[l][None, :])
        shift_m, scale_m, gate_m, shift_f, scale_f, gate_f = [
            m.reshape(bsz, 1, d) for m in jnp.split(mod, 6, axis=-1)]

        wi = w_in[l]
        qs = HEAD_DIM ** -0.5 * LOG2E
        w1 = jnp.concatenate(
            [wi[:, :D_POOL], wi[:, D_POOL:D_POOL + D_ATTN] * qs,
             wi[:, D_POOL + D_ATTN:D_POOL + 3 * D_ATTN]], axis=1).astype(BF16)
        pad = LANES - N_SPLIT * N_HEADS
        wf = jnp.pad(jnp.tile(wi[:, D_POOL + 3 * D_ATTN:], (1, N_SPLIT)),
                     ((0, 0), (0, pad))).astype(BF16)
        bfp = jnp.pad(jnp.tile(b_forget[l], N_SPLIT), (0, pad))[None, :]

        pool, qt, ka, vt = _premix(
            x, shift_m, scale_m, g_pre_mix[l][None, :], w1, wf, bfp, w_pool[l].astype(BF16),
            pool_scale[l][None, :], g_pool_out[l][None, :])
        ot = _attn(qt, ka, vt)

        wr = jnp.pad(w_router[l], ((0, 0), (0, LANES - N_EXPERTS)))
        wrh = wr.astype(BF16)
        wrl = (wr - wrh.astype(F32)).astype(BF16)
        x1, h2, st = _postmix(
            x, pool, ot, w_out[l].astype(BF16), g_attn_out[l][None, :], g_post_mix[l][None, :],
            gate_m, g_pre_ffn[l][None, :], shift_f, scale_f, wrh, wrl)

        t = bsz * seq
        infok, infot, cnt = _route(st, router_bias[l][:, None])
        tables, tails, block_expert, n_valid, n_rows = _dispatch_plan(cnt, t // TW)
        h2f = h2.reshape(t, d)
        xs = _dispatch(tables + tails + (n_valid,), h2f, infok, n_rows)
        ys = _experts(block_expert, n_valid, xs, w_gate[l], w_up[l], w_down[l])
        out = _combine(tables, infot, h2f, x1.reshape(t, d), gate_f, g_post_ffn[l][None, :],
                       ws_gate[l].astype(BF16), ws_up[l].astype(BF16), ws_down[l].astype(BF16),
                       ys, seq)
        x = out.reshape(bsz, seq, d)
    return x
```

```python
import functools

import numpy as np
import jax
import jax.numpy as jnp
from jax import lax
from jax.experimental import pallas as pl
from jax.experimental.pallas import tpu as pltpu

D_MODEL = 1024
D_POOL = 512
POOL_WINDOWS = (2, 4, 8, 16)
POOL_GROUP = 128
MAX_WINDOW = max(POOL_WINDOWS)
D_ATTN = 512
HEAD_DIM = 64
N_HEADS = 8
N_EXPERTS = 64
N_EXPERT_GROUPS = 8
GROUP_SIZE = N_EXPERTS // N_EXPERT_GROUPS
TOPK_GROUPS = 4
TOP_K = 8
D_EXPERT = 256
D_SHARED = 256
ROUTED_SCALE = 2.5
EPS = 1e-6

LOG2E = 1.4426950408889634
SUM_ROWS = 16
LANES = 128
N_SPLIT = 3
AUG = LANES

TS_PRE = 256
TS_POST = 256
TQ = 512
TK = 512
TW = 256
PIECE = 16
RB = 512
RSUB = 256
LCHUNK = 512
LROWS = -(-(TOP_K * TW + N_EXPERTS * (PIECE - 1)) // LCHUNK) * LCHUNK

F32 = jnp.float32
BF16 = jnp.bfloat16
VMEM_LIMIT = 56 * 1024 * 1024


def _rms(v, g):
    return v * lax.rsqrt(jnp.mean(v * v, axis=-1, keepdims=True) + EPS) * g


def _split3(v):
    hi = v.astype(BF16)
    r1 = v - hi.astype(F32)
    mid = r1.astype(BF16)
    r2 = r1 - mid.astype(F32)
    lo = r2.astype(BF16)
    return hi, mid, lo


def _dot(a, b):
    return jnp.dot(a, b, preferred_element_type=F32)


def _ada_kernel(c_ref, w_ref, b_ref, o_ref):
    o_ref[...] = _dot(c_ref[...].astype(BF16), w_ref[...].astype(BF16)) + b_ref[...]


def _ada(c, w, b):
    bsz = c.shape[0]
    n = w.shape[1]
    return pl.pallas_call(
        _ada_kernel,
        name="ada",
        grid=(n // D_MODEL,),
        in_specs=[
            pl.BlockSpec((bsz, D_MODEL), lambda j: (0, 0)),
            pl.BlockSpec((D_MODEL, D_MODEL), lambda j: (0, j)),
            pl.BlockSpec((1, D_MODEL), lambda j: (0, j)),
        ],
        out_specs=pl.BlockSpec((bsz, D_MODEL), lambda j: (0, j)),
        out_shape=jax.ShapeDtypeStruct((bsz, n), F32),
    )(c, w, b)


def _premix_kernel(x_ref, shift_ref, scale_ref, g_ref, w1_ref, wf_ref, bf_ref, wpool_ref,
                   pscale_ref, gpool_ref, place_ref, ones_ref, fmask_ref,
                   pool_ref, qt_ref, k_ref, vt_ref,
                   uext_ref, cum_ref):
    s = pl.program_id(1)
    ts = x_ref.shape[1]

    @pl.when(s == 0)
    def _():
        uext_ref[0:MAX_WINDOW, :] = jnp.zeros((MAX_WINDOW, D_POOL), F32)
        cum_ref[...] = jnp.zeros_like(cum_ref)

    x = x_ref[0]
    h = _rms(x, g_ref[...]) * (1.0 + scale_ref[0]) + shift_ref[0]
    hb = h.astype(BF16)
    proj = _dot(hb, w1_ref[...])
    u = proj[:, :D_POOL]
    q = proj[:, D_POOL:D_POOL + D_ATTN]
    k = proj[:, D_POOL + D_ATTN:D_POOL + 2 * D_ATTN]
    v = proj[:, D_POOL + 2 * D_ATTN:]

    uext_ref[MAX_WINDOW:, :] = u
    pos = (s * ts + lax.broadcasted_iota(jnp.int32, (ts, 1), 0) + 1).astype(F32)
    ys = []
    for g, w in enumerate(POOL_WINDOWS):
        c0 = g * POOL_GROUP
        acc = uext_ref[MAX_WINDOW:, c0:c0 + POOL_GROUP]
        for j in range(1, w):
            acc = acc + uext_ref[MAX_WINDOW - j:MAX_WINDOW - j + ts, c0:c0 + POOL_GROUP]
        pooled = acc / jnp.minimum(pos, float(w)) - u[:, c0:c0 + POOL_GROUP]
        ys.append(_dot(pooled.astype(BF16), wpool_ref[g]))
    ypool = jnp.concatenate(ys, axis=1) * pscale_ref[...]
    pool_ref[0] = _rms(ypool, gpool_ref[...]).astype(BF16)
    uext_ref[0:MAX_WINDOW, :] = uext_ref[ts:ts + MAX_WINDOW, :]

    z = _dot(hb, wf_ref[...]) + bf_ref[...]
    logf = jnp.minimum(z, 0.0) - jnp.log1p(jnp.exp(-jnp.abs(z)))
    row = lax.broadcasted_iota(jnp.int32, (ts, ts), 0)
    col = lax.broadcasted_iota(jnp.int32, (ts, ts), 1)
    tri = (col <= row).astype(BF16)
    cum = cum_ref[...]
    for piece in _split3(logf):
        cum = cum + _dot(tri, piece)
    cum_ref[...] = cum[ts - 1:ts, :]

    hi, mid, lo = [piece.astype(F32) for piece in _split3(cum * LOG2E)]
    lane = lax.broadcasted_iota(jnp.int32, (ts, LANES), 1)
    pieces = jnp.where(lane < N_HEADS, hi, jnp.where(lane < 2 * N_HEADS, mid, lo))
    aug = _dot(pieces.astype(BF16), place_ref[...]) + ones_ref[...]
    aug_q = aug[:, :N_HEADS * AUG]
    aug_k = aug[:, N_HEADS * AUG:]

    def expand(a):
        blocks = []
        for j in range(D_ATTN // LANES):
            blk = a[:, j * LANES:(j + 1) * LANES]
            blocks += [blk, blk]
        return jnp.concatenate(blocks, axis=1)

    fmask = fmask_ref[...]
    qa = expand(q) * fmask + aug_q
    ka = expand(k) * fmask + aug_k
    qt_ref[0] = qa.T.astype(BF16)
    k_ref[0] = ka.astype(BF16)
    vt_ref[0] = v.T.astype(BF16)


def _aug_constants():
    width = N_HEADS * AUG
    place = np.zeros((LANES, 2 * width), np.float32)
    ones = np.zeros((1, 2 * width), np.float32)
    fmask = np.zeros((1, width), np.float32)
    for h in range(N_HEADS):
        feat0 = h * AUG + (0 if h % 2 == 0 else HEAD_DIM)
        aug0 = h * AUG + (HEAD_DIM if h % 2 == 0 else 0)
        fmask[0, feat0:feat0 + HEAD_DIM] = 1.0
        for p in range(N_SPLIT):
            place[p * N_HEADS + h, aug0 + p] = 1.0
            ones[0, width + aug0 + p] = 1.0
            place[p * N_HEADS + h, width + aug0 + N_SPLIT + p] = -1.0
            ones[0, aug0 + N_SPLIT + p] = 1.0
    return jnp.asarray(place, BF16), jnp.asarray(ones), jnp.asarray(fmask)


def _premix(x, shift, scale, g, w1, wf, bfp, wpool, pscale, gpool):
    bsz, seq, _ = x.shape
    ts = TS_PRE
    place, ones, fmask = _aug_constants()
    full = lambda shape: pl.BlockSpec(shape, lambda b, s: (0,) * len(shape))
    per_batch = pl.BlockSpec((1, 1, D_MODEL), lambda b, s: (b, 0, 0))
    return pl.pallas_call(
        _premix_kernel,
        name="premix",
        grid=(bsz, seq // ts),
        in_specs=[
            pl.BlockSpec((1, ts, D_MODEL), lambda b, s: (b, s, 0)),
            per_batch, per_batch,
            full((1, D_MODEL)),
            full((D_MODEL, D_POOL + 3 * D_ATTN)),
            full((D_MODEL, LANES)),
            full((1, LANES)),
            full((len(POOL_WINDOWS), POOL_GROUP, POOL_GROUP)),
            full((1, D_POOL)),
            full((1, D_POOL)),
            full((LANES, 2 * N_HEADS * AUG)),
            full((1, 2 * N_HEADS * AUG)),
            full((1, N_HEADS * AUG)),
        ],
        out_specs=[
            pl.BlockSpec((1, ts, D_POOL), lambda b, s: (b, s, 0)),
            pl.BlockSpec((1, N_HEADS * AUG, ts), lambda b, s: (b, 0, s)),
            pl.BlockSpec((1, ts, N_HEADS * AUG), lambda b, s: (b, s, 0)),
            pl.BlockSpec((1, D_ATTN, ts), lambda b, s: (b, 0, s)),
        ],
        out_shape=[
            jax.ShapeDtypeStruct((bsz, seq, D_POOL), BF16),
            jax.ShapeDtypeStruct((bsz, N_HEADS * AUG, seq), BF16),
            jax.ShapeDtypeStruct((bsz, seq, N_HEADS * AUG), BF16),
            jax.ShapeDtypeStruct((bsz, D_ATTN, seq), BF16),
        ],
        scratch_shapes=[
            pltpu.VMEM((ts + MAX_WINDOW, D_POOL), F32),
            pltpu.VMEM((1, LANES), F32),
        ],
        compiler_params=pltpu.CompilerParams(
            dimension_semantics=("arbitrary", "arbitrary"), vmem_limit_bytes=VMEM_LIMIT),
    )(x, shift, scale, g, w1, wf, bfp, wpool, pscale, gpool, place, ones, fmask)


def _attn_kernel(qt_ref, k_ref, vt_ref, o_ref):
    seq = k_ref.shape[1]
    key_i = lax.broadcasted_iota(jnp.int32, (TK, TQ), 0)
    qry_i = lax.broadcasted_iota(jnp.int32, (TK, TQ), 1)
    causal = key_i <= qry_i
    one_row = lax.broadcasted_iota(jnp.int32, (SUM_ROWS, seq), 0) == 0
    vaug = jnp.concatenate([vt_ref[0], jnp.where(one_row, 1.0, 0.0).astype(BF16)], axis=0)
    for qi in range(seq // TQ):
        qt = qt_ref[0, :, qi * TQ:(qi + 1) * TQ]
        m = jnp.full((1, TQ), -jnp.inf, F32)
        acc = jnp.zeros((HEAD_DIM + SUM_ROWS, TQ), F32)
        for kj in range(qi + 1):
            kt = k_ref[0, kj * TK:(kj + 1) * TK, :]
            st = _dot(kt, qt)
            if kj == qi:
                st = jnp.where(causal, st, -jnp.inf)
            m_new = jnp.maximum(m, jnp.max(st, axis=0, keepdims=True))
            p = jnp.exp2(st - m_new).astype(BF16)
            acc = jnp.exp2(m - m_new) * acc + _dot(vaug[:, kj * TK:(kj + 1) * TK], p)
            m = m_new
        o_ref[0, :, qi * TQ:(qi + 1) * TQ] = acc[:HEAD_DIM] / acc[HEAD_DIM:HEAD_DIM + 1]


def _attn(qt, ka, vt):
    bsz, _, seq = qt.shape
    return pl.pallas_call(
        _attn_kernel,
        name="attn",
        grid=(bsz, N_HEADS),
        in_specs=[
            pl.BlockSpec((1, AUG, seq), lambda b, h: (b, h, 0)),
            pl.BlockSpec((1, seq, AUG), lambda b, h: (b, 0, h)),
            pl.BlockSpec((1, HEAD_DIM, seq), lambda b, h: (b, h, 0)),
        ],
        out_specs=pl.BlockSpec((1, HEAD_DIM, seq), lambda b, h: (b, h, 0)),
        out_shape=jax.ShapeDtypeStruct((bsz, D_ATTN, seq), F32),
        compiler_params=pltpu.CompilerParams(
            dimension_semantics=("arbitrary", "arbitrary"), vmem_limit_bytes=VMEM_LIMIT),
    )(qt, ka, vt)


def _postmix_kernel(x_ref, pool_ref, ot_ref, wout_ref, gattn_ref, gpost_ref, gate_ref,
                    gffn_ref, shift_ref, scale_ref, wrh_ref, wrl_ref,
                    x1_ref, h2_ref, st_ref):
    ya = ot_ref[0].T
    ya = _rms(ya, gattn_ref[...]).astype(BF16)
    mixed = _dot(pool_ref[0], wout_ref[:D_POOL, :]) + _dot(ya, wout_ref[D_POOL:, :])
    x1 = x_ref[0] + gate_ref[0] * _rms(mixed, gpost_ref[...])
    x1_ref[0] = x1
    h2 = _rms(x1, gffn_ref[...]) * (1.0 + scale_ref[0]) + shift_ref[0]
    h2b = h2.astype(BF16)
    h2_ref[0] = h2b
    h2l = (h2 - h2b.astype(F32)).astype(BF16)
    logits = _dot(h2b, wrh_ref[...]) + _dot(h2b, wrl_ref[...]) + _dot(h2l, wrh_ref[...])
    st_ref[...] = jax.nn.sigmoid(logits).T


def _postmix(x, pool, ot, wout, gattn, gpost, gate, gffn, shift, scale, wrh, wrl):
    bsz, seq, _ = x.shape
    ts = TS_POST
    ns = seq // ts
    full = lambda shape: pl.BlockSpec(shape, lambda b, s: (0,) * len(shape))
    per_batch = pl.BlockSpec((1, 1, D_MODEL), lambda b, s: (b, 0, 0))
    tok = lambda width: pl.BlockSpec((1, ts, width), lambda b, s: (b, s, 0))
    return pl.pallas_call(
        _postmix_kernel,
        name="postmix",
        grid=(bsz, ns),
        in_specs=[
            tok(D_MODEL), tok(D_POOL),
            pl.BlockSpec((1, D_ATTN, ts), lambda b, s: (b, 0, s)),
            full((D_MODEL, D_MODEL)),
            full((1, D_ATTN)), full((1, D_MODEL)), per_batch,
            full((1, D_MODEL)), per_batch, per_batch,
            full((D_MODEL, LANES)), full((D_MODEL, LANES)),
        ],
        out_specs=[
            tok(D_MODEL), tok(D_MODEL),
            pl.BlockSpec((LANES, ts), lambda b, s: (0, b * ns + s)),
        ],
        out_shape=[
            jax.ShapeDtypeStruct((bsz, seq, D_MODEL), F32),
            jax.ShapeDtypeStruct((bsz, seq, D_MODEL), BF16),
            jax.ShapeDtypeStruct((LANES, bsz * seq), F32),
        ],
        compiler_params=pltpu.CompilerParams(
            dimension_semantics=("arbitrary", "arbitrary"), vmem_limit_bytes=VMEM_LIMIT),
    )(x, pool, ot, wout, gattn, gpost, gate, gffn, shift, scale, wrh, wrl)


def _route_kernel(st_ref, bias_ref, infok_ref, infot_ref, cnt_ref):
    tile = pl.program_id(0)
    tr = st_ref.shape[1]
    scores = [st_ref[g * GROUP_SIZE:(g + 1) * GROUP_SIZE, :] for g in range(N_EXPERT_GROUPS)]
    sel = [scores[g] + bias_ref[g * GROUP_SIZE:(g + 1) * GROUP_SIZE, :]
           for g in range(N_EXPERT_GROUPS)]
    sub = lax.broadcasted_iota(jnp.int32, (GROUP_SIZE, tr), 0)
    neg = jnp.float32(-jnp.inf)

    rows = []
    for g in range(N_EXPERT_GROUPS):
        v = sel[g]
        m1 = jnp.max(v, axis=0, keepdims=True)
        first = jnp.min(jnp.where(v == m1, sub, GROUP_SIZE), axis=0, keepdims=True)
        m2 = jnp.max(jnp.where(sub == first, neg, v), axis=0, keepdims=True)
        rows.append(m1 + m2)
    gs = jnp.concatenate(rows, axis=0)

    shape = (GROUP_SIZE, tr)
    one = jnp.ones(shape, jnp.int32)
    zero = jnp.zeros(shape, jnp.int32)
    beaten = zero
    for gp in range(N_EXPERT_GROUPS):
        r = jnp.broadcast_to(gs[gp:gp + 1, :], shape)
        tie = jnp.where(sub > gp, one, zero)
        beaten = beaten + jnp.where(r > gs, one, zero) + jnp.where(r == gs, tie, zero)
    keep = jnp.where(beaten < TOPK_GROUPS, 1.0, 0.0)

    masked = [jnp.where(jnp.broadcast_to(keep[g:g + 1, :], shape) > 0.5, sel[g], neg)
              for g in range(N_EXPERT_GROUPS)]

    cnt = [zero for _ in range(N_EXPERT_GROUPS)]
    for gp in range(N_EXPERT_GROUPS):
        for jp in range(GROUP_SIZE):
            r = jnp.broadcast_to(masked[gp][jp:jp + 1, :], shape)
            for g in range(N_EXPERT_GROUPS):
                if g < gp:
                    beats = jnp.where(r > masked[g], one, zero)
                elif g > gp:
                    beats = jnp.where(r >= masked[g], one, zero)
                else:
                    tie = jnp.where(sub > jp, one, zero)
                    beats = jnp.where(r > masked[g], one, zero) + jnp.where(r == masked[g], tie, zero)
                cnt[g] = cnt[g] + beats
    chosen = [cnt[g] < TOP_K for g in range(N_EXPERT_GROUPS)]

    w = [jnp.where(chosen[g], scores[g], 0.0) for g in range(N_EXPERT_GROUPS)]
    denom = w[0].sum(axis=0, keepdims=True)
    for g in range(1, N_EXPERT_GROUPS):
        denom = denom + w[g].sum(axis=0, keepdims=True)
    wfull = jnp.concatenate([w[g] / denom * ROUTED_SCALE for g in range(N_EXPERT_GROUPS)], axis=0)
    mfull = jnp.concatenate([jnp.where(chosen[g], 1.0, 0.0) for g in range(N_EXPERT_GROUPS)],
                            axis=0)
    mb = mfull.astype(BF16)

    e_r = lax.broadcasted_iota(jnp.int32, (N_EXPERTS, N_EXPERTS), 0)
    e_c = lax.broadcasted_iota(jnp.int32, (N_EXPERTS, N_EXPERTS), 1)
    before_e = (e_c < e_r).astype(BF16)
    t_r = lax.broadcasted_iota(jnp.int32, (tr, tr), 0)
    t_c = lax.broadcasted_iota(jnp.int32, (tr, tr), 1)
    before_t = (t_r < t_c).astype(BF16)
    ordinal = _dot(before_e, mb)
    rank = _dot(mb, before_t)
    n = jnp.sum(mfull, axis=1, keepdims=True)
    pieces = jnp.floor((n + (PIECE - 1.0)) * (1.0 / PIECE))
    run_start = PIECE * _dot(before_e, jnp.broadcast_to(pieces, (N_EXPERTS, LANES)).astype(BF16))
    pos = run_start[:, 0:1] + rank

    sub8 = lax.broadcasted_iota(jnp.int32, (TOP_K, tr), 0)
    pos8 = jnp.zeros((TOP_K, tr), F32)
    w8 = jnp.zeros((TOP_K, tr), F32)
    for k in range(TOP_K):
        selk = jnp.where(ordinal == float(k), mfull, 0.0)
        pk = jnp.sum(selk * pos, axis=0, keepdims=True)
        wk = jnp.sum(selk * wfull, axis=0, keepdims=True)
        pos8 = jnp.where(sub8 == k, jnp.broadcast_to(pk, (TOP_K, tr)), pos8)
        w8 = jnp.where(sub8 == k, jnp.broadcast_to(wk, (TOP_K, tr)), w8)
    infok_ref[...] = jnp.concatenate([pos8, w8], axis=0)
    info = jnp.concatenate([pos8, w8, jnp.zeros((LANES - 2 * TOP_K, tr), F32)], axis=0)
    infot_ref[...] = info.T

    @pl.when(tile == 0)
    def _():
        cnt_ref[...] = jnp.zeros_like(cnt_ref)

    lane = lax.broadcasted_iota(jnp.int32, cnt_ref.shape, 1)
    cnt_ref[...] = jnp.where(lane == tile, jnp.broadcast_to(n, cnt_ref.shape), cnt_ref[...])


def _route(st, bias_col):
    t = st.shape[1]
    assert t // TW <= LANES
    return pl.pallas_call(
        _route_kernel,
        name="route",
        grid=(t // TW,),
        in_specs=[
            pl.BlockSpec((LANES, TW), lambda i: (0, i)),
            pl.BlockSpec((N_EXPERTS, 1), lambda i: (0, 0)),
        ],
        out_specs=[
            pl.BlockSpec((2 * TOP_K, TW), lambda i: (0, i)),
            pl.BlockSpec((TW, LANES), lambda i: (i, 0)),
            pl.BlockSpec((N_EXPERTS, LANES), lambda i: (0, 0)),
        ],
        out_shape=[
            jax.ShapeDtypeStruct((2 * TOP_K, t), F32),
            jax.ShapeDtypeStruct((t, LANES), F32),
            jax.ShapeDtypeStruct((N_EXPERTS, LANES), F32),
        ],
        compiler_params=pltpu.CompilerParams(dimension_semantics=("arbitrary",)),
    )(st, bias_col)


def _swiglu(xb, wg, wu):
    g = _dot(xb, wg)
    return (g * jax.nn.sigmoid(g)) * _dot(xb, wu)


def _piece_copy(src_ref, src_row, dst_ref, dst_row, sem, n_pieces=1):
    rows = n_pieces * PIECE
    src = src_ref.at[pl.ds(pl.multiple_of(src_row * PIECE, PIECE), rows), :]
    dst = dst_ref.at[pl.ds(pl.multiple_of(dst_row * PIECE, PIECE), rows), :]
    return pltpu.make_async_copy(src, dst, sem)


N_BITS = 5
LOCAL_BITS = 8


def _for_each_piece(runs_ref, tile, fn):
    def per_expert(e, carry):
        run = runs_ref[tile * N_EXPERTS + e]
        n = run & ((1 << N_BITS) - 1)
        loc = lax.shift_right_logical(run, N_BITS) & ((1 << LOCAL_BITS) - 1)
        glob = lax.shift_right_logical(run, N_BITS + LOCAL_BITS)

        def per_pair(p, inner):
            fn(loc + 2 * p, glob + 2 * p, 2)
            return inner

        lax.fori_loop(0, lax.shift_right_logical(n, 1), per_pair, 0)

        @pl.when((n & 1) == 1)
        def _():
            fn(loc + n - 1, glob + n - 1, 1)

        return carry

    lax.fori_loop(0, N_EXPERTS, per_expert, 0)


def _wait_pieces(total, copy_of):
    for bit in range((LROWS // PIECE).bit_length()):
        @pl.when((lax.shift_right_logical(total, bit) & 1) == 1)
        def _():
            copy_of(1 << bit).wait()


def _dispatch_kernel(runs_ref, total_ref, tail_ref, ntail_ref, nv_ref,
                     h2_ref, infok_ref, xs_ref, lbuf, zbuf, sem, zsem):
    tile = pl.program_id(0)
    last = pl.num_programs(0) - 1
    slot = tile % 2
    n_blocks = xs_ref.shape[0] // RB

    def spare_blocks(fn):
        def per_block(b, carry):
            fn(b)
            return carry
        lax.fori_loop(nv_ref[0], n_blocks, per_block, 0)

    def zero_block(b):
        dst = xs_ref.at[pl.ds(pl.multiple_of(b * RB, RB), RB), :]
        return pltpu.make_async_copy(zbuf, dst, zsem)

    def start(t, s):
        _for_each_piece(
            runs_ref, t,
            lambda lp, gp, n: _piece_copy(lbuf.at[s], lp, xs_ref, gp, sem.at[s], n).start())

    def wait(t, s):
        _wait_pieces(total_ref[t],
                     lambda n: _piece_copy(lbuf.at[s], 0, xs_ref, 0, sem.at[s], n))

    @pl.when(tile == 0)
    def _():
        zbuf[...] = jnp.zeros_like(zbuf)

        def tails(fn):
            def per_expert(e, carry):
                def per_piece(p, inner):
                    fn(tail_ref[e] + p)
                    return inner
                lax.fori_loop(0, ntail_ref[e], per_piece, 0)
                return carry
            lax.fori_loop(0, N_EXPERTS, per_expert, 0)

        tails(lambda gp: _piece_copy(zbuf, 0, xs_ref, gp, zsem).start())
        tails(lambda gp: _piece_copy(zbuf, 0, xs_ref, gp, zsem).wait())
        spare_blocks(lambda b: zero_block(b).start())

    @pl.when(tile >= 2)
    def _():
        wait(tile - 2, slot)

    xb = h2_ref[...]
    for j in range(LROWS // LCHUNK):
        r = (lax.broadcasted_iota(jnp.int32, (LCHUNK, TW), 0) + j * LCHUNK).astype(F32)
        onehot = jnp.zeros((LCHUNK, TW), F32)
        for k in range(TOP_K):
            onehot = jnp.where(r == infok_ref[k:k + 1, :], 1.0, onehot)
        lbuf[slot, j * LCHUNK:(j + 1) * LCHUNK, :] = _dot(onehot.astype(BF16), xb).astype(BF16)

    start(tile, slot)

    @pl.when(tile == last)
    def _():
        @pl.when(tile >= 1)
        def _():
            wait(tile - 1, 1 - slot)
        wait(tile, slot)
        spare_blocks(lambda b: zero_block(b).wait())


def _dispatch(tables, h2, infok, n_rows):
    t = h2.shape[0]
    return pl.pallas_call(
        _dispatch_kernel,
        name="dispatch",
        grid_spec=pltpu.PrefetchScalarGridSpec(
            num_scalar_prefetch=5,
            grid=(t // TW,),
            in_specs=[
                pl.BlockSpec((TW, D_MODEL), lambda i, *_: (i, 0)),
                pl.BlockSpec((2 * TOP_K, TW), lambda i, *_: (0, i)),
            ],
            out_specs=pl.BlockSpec(memory_space=pl.ANY),
            scratch_shapes=[
                pltpu.VMEM((2, LROWS, D_MODEL), BF16),
                pltpu.VMEM((RB, D_MODEL), BF16),
                pltpu.SemaphoreType.DMA((2,)),
                pltpu.SemaphoreType.DMA(()),
            ],
        ),
        out_shape=jax.ShapeDtypeStruct((n_rows, D_MODEL), BF16),
        compiler_params=pltpu.CompilerParams(
            dimension_semantics=("arbitrary",), vmem_limit_bytes=VMEM_LIMIT),
    )(*tables, h2, infok)


def _experts_kernel(be_ref, nv_ref, xs_ref, wg_ref, wu_ref, wd_ref, ys_ref, wgb, wub, wdb):
    i = pl.program_id(0)

    @pl.when(i < nv_ref[0])
    def _():
        @pl.when((i == 0) | (be_ref[i] != be_ref[jnp.maximum(i - 1, 0)]))
        def _():
            wgb[...] = wg_ref[0].astype(BF16)
            wub[...] = wu_ref[0].astype(BF16)
            wdb[...] = wd_ref[0].astype(BF16)

        for j in range(RB // RSUB):
            rows = slice(j * RSUB, (j + 1) * RSUB)
            h = _swiglu(xs_ref[rows, :], wgb[...], wub[...])
            ys_ref[rows, :] = _dot(h.astype(BF16), wdb[...]).astype(BF16)


def _experts(block_expert, n_valid, xs, wg, wu, wd):
    n_blocks = xs.shape[0] // RB
    rows = pl.BlockSpec((RB, D_MODEL), lambda i, be, nv: (jnp.minimum(i, nv[0] - 1), 0))
    return pl.pallas_call(
        _experts_kernel,
        name="experts",
        grid_spec=pltpu.PrefetchScalarGridSpec(
            num_scalar_prefetch=2,
            grid=(n_blocks,),
            in_specs=[
                rows,
                pl.BlockSpec((1, D_MODEL, D_EXPERT), lambda i, be, nv: (be[i], 0, 0)),
                pl.BlockSpec((1, D_MODEL, D_EXPERT), lambda i, be, nv: (be[i], 0, 0)),
                pl.BlockSpec((1, D_EXPERT, D_MODEL), lambda i, be, nv: (be[i], 0, 0)),
            ],
            out_specs=rows,
            scratch_shapes=[
                pltpu.VMEM((D_MODEL, D_EXPERT), BF16),
                pltpu.VMEM((D_MODEL, D_EXPERT), BF16),
                pltpu.VMEM((D_EXPERT, D_MODEL), BF16),
            ],
        ),
        out_shape=jax.ShapeDtypeStruct(xs.shape, BF16),
        input_output_aliases={2: 0},
        compiler_params=pltpu.CompilerParams(
            dimension_semantics=("arbitrary",), vmem_limit_bytes=VMEM_LIMIT),
    )(block_expert, n_valid, xs, wg, wu, wd)


def _combine_kernel(runs_ref, total_ref,
                    infot_ref, h2_ref, x1_ref, gate_ref, gpost_ref, wsg_ref, wsu_ref, wsd_ref,
                    ys_ref, o_ref, ybuf, sem):
    tile = pl.program_id(0)
    last = pl.num_programs(0) - 1
    slot = tile % 2

    def start(t, s):
        _for_each_piece(
            runs_ref, t,
            lambda lp, gp, n: _piece_copy(ys_ref, gp, ybuf.at[s], lp, sem.at[s], n).start())

    def wait(t, s):
        _wait_pieces(total_ref[t],
                     lambda n: _piece_copy(ys_ref, 0, ybuf.at[s], 0, sem.at[s], n))

    @pl.when(tile == 0)
    def _():
        ybuf[...] = jnp.zeros_like(ybuf)
        start(tile, slot)

    @pl.when(tile < last)
    def _():
        start(tile + 1, 1 - slot)

    wait(tile, slot)

    info = infot_ref[...]
    routed = jnp.zeros((TW, D_MODEL), F32)
    for j in range(LROWS // LCHUNK):
        r = (lax.broadcasted_iota(jnp.int32, (TW, LCHUNK), 1) + j * LCHUNK).astype(F32)
        wmat = jnp.zeros((TW, LCHUNK), F32)
        for k in range(TOP_K):
            wmat = jnp.where(r == info[:, k:k + 1], info[:, TOP_K + k:TOP_K + k + 1], wmat)
        routed = routed + _dot(wmat.astype(BF16), ybuf[slot, j * LCHUNK:(j + 1) * LCHUNK, :])

    hs = _swiglu(h2_ref[...], wsg_ref[...], wsu_ref[...])
    ff = routed + _dot(hs.astype(BF16), wsd_ref[...])
    o_ref[...] = x1_ref[...] + gate_ref[0] * _rms(ff, gpost_ref[...])


def _combine(tables, infot, h2, x1, gate, gpost, wsg, wsu, wsd, ys, seq):
    t = h2.shape[0]
    per_seq = seq // TW
    full = lambda shape: pl.BlockSpec(shape, lambda i, *_: (0,) * len(shape))
    tok = lambda width: pl.BlockSpec((TW, width), lambda i, *_: (i, 0))
    return pl.pallas_call(
        _combine_kernel,
        name="combine",
        grid_spec=pltpu.PrefetchScalarGridSpec(
            num_scalar_prefetch=2,
            grid=(t // TW,),
            in_specs=[
                tok(LANES), tok(D_MODEL), tok(D_MODEL),
                pl.BlockSpec((1, 1, D_MODEL), lambda i, *_: (i // per_seq, 0, 0)),
                full((1, D_MODEL)),
                full((D_MODEL, D_SHARED)), full((D_MODEL, D_SHARED)), full((D_SHARED, D_MODEL)),
                pl.BlockSpec(memory_space=pl.ANY),
            ],
            out_specs=tok(D_MODEL),
            scratch_shapes=[
                pltpu.VMEM((2, LROWS, D_MODEL), BF16),
                pltpu.SemaphoreType.DMA((2,)),
            ],
        ),
        out_shape=jax.ShapeDtypeStruct((t, D_MODEL), F32),
        compiler_params=pltpu.CompilerParams(
            dimension_semantics=("arbitrary",), vmem_limit_bytes=VMEM_LIMIT),
    )(*tables, infot, h2, x1, gate, gpost, wsg, wsu, wsd, ys)


def _dispatch_plan(cnt, n_tiles):
    n = cnt[:, :n_tiles].astype(jnp.int32)
    pieces = (n + PIECE - 1) // PIECE
    local = jnp.cumsum(pieces, axis=0) - pieces
    seg = jnp.sum(pieces, axis=1)
    per_block = RB // PIECE
    seg_pad = (seg + per_block - 1) // per_block * per_block
    seg_end = jnp.cumsum(seg_pad)
    seg_start = seg_end - seg_pad
    glob = seg_start[:, None] + jnp.cumsum(pieces, axis=1) - pieces
    n_blocks = (TOP_K * TW * n_tiles + N_EXPERTS * n_tiles * (PIECE - 1)
                + N_EXPERTS * (RB - PIECE) + RB - 1) // RB
    n_valid = seg_end[-1] // per_block
    blk = jnp.minimum(jnp.arange(n_blocks, dtype=jnp.int32), n_valid - 1)
    block_end = seg_end // per_block
    block_expert = jnp.sum((block_end[None, :] <= blk[:, None]).astype(jnp.int32), axis=1)
    block_expert = jnp.minimum(block_expert, N_EXPERTS - 1)
    assert TW // PIECE < 1 << N_BITS and LROWS // PIECE <= 1 << LOCAL_BITS
    assert n_blocks * per_block < 1 << (31 - N_BITS - LOCAL_BITS)
    runs = pieces | (local << N_BITS) | (glob << (N_BITS + LOCAL_BITS))
    tables = (runs.T.reshape(-1).astype(jnp.int32), jnp.sum(pieces, axis=0).astype(jnp.int32))
    tails = ((seg_start + seg).astype(jnp.int32), (seg_pad - seg).astype(jnp.int32))
    return tables, tails, block_expert, n_valid.reshape(1).astype(jnp.int32), n_blocks * RB


def kernel(x, c, w_ada, b_ada, g_pre_mix, g_post_mix, g_pre_ffn, g_post_ffn, w_in, b_forget,
           w_pool, pool_scale, g_pool_out, g_attn_out, w_out, w_router, router_bias,
           w_gate, w_up, w_down, ws_gate, ws_up, ws_down):
    bsz, seq, d = x.shape
    depth = w_ada.shape[0]
    for l in range(depth):
        mod = _ada(c, w_ada[l], b_ada[l][None, :])
        shift_m, scale_m, gate_m, shift_f, scale_f, gate_f = [
            m.reshape(bsz, 1, d) for m in jnp.split(mod, 6, axis=-1)]

        wi = w_in[l]
        qs = HEAD_DIM ** -0.5 * LOG2E
        w1 = jnp.concatenate(
            [wi[:, :D_POOL], wi[:, D_POOL:D_POOL + D_ATTN] * qs,
             wi[:, D_POOL + D_ATTN:D_POOL + 3 * D_ATTN]], axis=1).astype(BF16)
        pad = LANES - N_SPLIT * N_HEADS
        wf = jnp.pad(jnp.tile(wi[:, D_POOL + 3 * D_ATTN:], (1, N_SPLIT)),
                     ((0, 0), (0, pad))).astype(BF16)
        bfp = jnp.pad(jnp.tile(b_forget[l], N_SPLIT), (0, pad))[None, :]

        pool, qt, ka, vt = _premix(
            x, shift_m, scale_m, g_pre_mix[l][None, :], w1, wf, bfp, w_pool[l].astype(BF16),
            pool_scale[l][None, :], g_pool_out[l][None, :])
        ot = _attn(qt, ka, vt)

        wr = jnp.pad(w_router[l], ((0, 0), (0, LANES - N_EXPERTS)))
        wrh = wr.astype(BF16)
        wrl = (wr - wrh.astype(F32)).astype(BF16)
        x1, h2, st = _postmix(
            x, pool, ot, w_out[l].astype(BF16), g_attn_out[l][None, :], g_post_mix[l][None, :],
            gate_m, g_pre_ffn[l][None, :], shift_f, scale_f, wrh, wrl)

        t = bsz * seq
        infok, infot, cnt = _route(st, router_bias[l][:, None])
        tables, tails, block_expert, n_valid, n_rows = _dispatch_plan(cnt, t // TW)
        h2f = h2.reshape(t, d)
        xs = _dispatch(tables + tails + (n_valid,), h2f, infok, n_rows)
        ys = _experts(block_expert, n_valid, xs, w_gate[l], w_up[l], w_down[l])
        out = _combine(tables, infot, h2f, x1.reshape(t, d), gate_f, g_post_ffn[l][None, :],
                       ws_gate[l].astype(BF16), ws_up[l].astype(BF16), ws_down[l].astype(BF16),
                       ys, seq)
        x = out.reshape(bsz, seq, d)
    return x
```

```python
import functools

import numpy as np
import jax
import jax.numpy as jnp
from jax import lax
from jax.experimental import pallas as pl
from jax.experimental.pallas import tpu as pltpu

D_MODEL = 1024
D_POOL = 512
POOL_WINDOWS = (2, 4, 8, 16)
POOL_GROUP = 128
MAX_WINDOW = max(POOL_WINDOWS)
D_ATTN = 512
HEAD_DIM = 64
N_HEADS = 8
N_EXPERTS = 64
N_EXPERT_GROUPS = 8
GROUP_SIZE = N_EXPERTS // N_EXPERT_GROUPS
TOPK_GROUPS = 4
TOP_K = 8
D_EXPERT = 256
D_SHARED = 256
ROUTED_SCALE = 2.5
EPS = 1e-6

LOG2E = 1.4426950408889634
SUM_ROWS = 16
LANES = 128
N_SPLIT = 3
AUG = LANES

TS_PRE = 256
TS_POST = 256
TQ = 512
TK = 512
TW = 256
PIECE = 16
RB = 1024
RSUB = 256
LCHUNK = 512
LROWS = -(-(TOP_K * TW + N_EXPERTS * (PIECE - 1)) // LCHUNK) * LCHUNK

F32 = jnp.float32
BF16 = jnp.bfloat16
VMEM_LIMIT = 56 * 1024 * 1024


def _rms(v, g):
    return v * lax.rsqrt(jnp.mean(v * v, axis=-1, keepdims=True) + EPS) * g


def _split3(v):
    hi = v.astype(BF16)
    r1 = v - hi.astype(F32)
    mid = r1.astype(BF16)
    r2 = r1 - mid.astype(F32)
    lo = r2.astype(BF16)
    return hi, mid, lo


def _dot(a, b):
    return jnp.dot(a, b, preferred_element_type=F32)


def _ada_kernel(c_ref, w_ref, b_ref, o_ref):
    o_ref[...] = _dot(c_ref[...].astype(BF16), w_ref[...].astype(BF16)) + b_ref[...]


def _ada(c, w, b):
    bsz = c.shape[0]
    n = w.shape[1]
    return pl.pallas_call(
        _ada_kernel,
        name="ada",
        grid=(n // D_MODEL,),
        in_specs=[
            pl.BlockSpec((bsz, D_MODEL), lambda j: (0, 0)),
            pl.BlockSpec((D_MODEL, D_MODEL), lambda j: (0, j)),
            pl.BlockSpec((1, D_MODEL), lambda j: (0, j)),
        ],
        out_specs=pl.BlockSpec((bsz, D_MODEL), lambda j: (0, j)),
        out_shape=jax.ShapeDtypeStruct((bsz, n), F32),
    )(c, w, b)


def _premix_kernel(x_ref, shift_ref, scale_ref, g_ref, w1_ref, wf_ref, bf_ref, wpool_ref,
                   pscale_ref, gpool_ref, place_ref, ones_ref, fmask_ref,
                   pool_ref, qt_ref, k_ref, vt_ref,
                   uext_ref, cum_ref):
    s = pl.program_id(1)
    ts = x_ref.shape[1]

    @pl.when(s == 0)
    def _():
        uext_ref[0:MAX_WINDOW, :] = jnp.zeros((MAX_WINDOW, D_POOL), F32)
        cum_ref[...] = jnp.zeros_like(cum_ref)

    x = x_ref[0]
    h = _rms(x, g_ref[...]) * (1.0 + scale_ref[0]) + shift_ref[0]
    hb = h.astype(BF16)
    proj = _dot(hb, w1_ref[...])
    u = proj[:, :D_POOL]
    q = proj[:, D_POOL:D_POOL + D_ATTN]
    k = proj[:, D_POOL + D_ATTN:D_POOL + 2 * D_ATTN]
    v = proj[:, D_POOL + 2 * D_ATTN:]

    uext_ref[MAX_WINDOW:, :] = u
    pos = (s * ts + lax.broadcasted_iota(jnp.int32, (ts, 1), 0) + 1).astype(F32)
    ys = []
    for g, w in enumerate(POOL_WINDOWS):
        c0 = g * POOL_GROUP
        acc = uext_ref[MAX_WINDOW:, c0:c0 + POOL_GROUP]
        for j in range(1, w):
            acc = acc + uext_ref[MAX_WINDOW - j:MAX_WINDOW - j + ts, c0:c0 + POOL_GROUP]
        pooled = acc / jnp.minimum(pos, float(w)) - u[:, c0:c0 + POOL_GROUP]
        ys.append(_dot(pooled.astype(BF16), wpool_ref[g]))
    ypool = jnp.concatenate(ys, axis=1) * pscale_ref[...]
    pool_ref[0] = _rms(ypool, gpool_ref[...]).astype(BF16)
    uext_ref[0:MAX_WINDOW, :] = uext_ref[ts:ts + MAX_WINDOW, :]

    z = _dot(hb, wf_ref[...]) + bf_ref[...]
    logf = jnp.minimum(z, 0.0) - jnp.log1p(jnp.exp(-jnp.abs(z)))
    row = lax.broadcasted_iota(jnp.int32, (ts, ts), 0)
    col = lax.broadcasted_iota(jnp.int32, (ts, ts), 1)
    tri = (col <= row).astype(BF16)
    cum = cum_ref[...]
    for piece in _split3(logf):
        cum = cum + _dot(tri, piece)
    cum_ref[...] = cum[ts - 1:ts, :]

    hi, mid, lo = [piece.astype(F32) for piece in _split3(cum * LOG2E)]
    lane = lax.broadcasted_iota(jnp.int32, (ts, LANES), 1)
    pieces = jnp.where(lane < N_HEADS, hi, jnp.where(lane < 2 * N_HEADS, mid, lo))
    aug = _dot(pieces.astype(BF16), place_ref[...]) + ones_ref[...]
    aug_q = aug[:, :N_HEADS * AUG]
    aug_k = aug[:, N_HEADS * AUG:]

    def expand(a):
        blocks = []
        for j in range(D_ATTN // LANES):
            blk = a[:, j * LANES:(j + 1) * LANES]
            blocks += [blk, blk]
        return jnp.concatenate(blocks, axis=1)

    fmask = fmask_ref[...]
    qa = expand(q) * fmask + aug_q
    ka = expand(k) * fmask + aug_k
    qt_ref[0] = qa.T.astype(BF16)
    k_ref[0] = ka.astype(BF16)
    vt_ref[0] = v.T.astype(BF16)


def _aug_constants():
    width = N_HEADS * AUG
    place = np.zeros((LANES, 2 * width), np.float32)
    ones = np.zeros((1, 2 * width), np.float32)
    fmask = np.zeros((1, width), np.float32)
    for h in range(N_HEADS):
        feat0 = h * AUG + (0 if h % 2 == 0 else HEAD_DIM)
        aug0 = h * AUG + (HEAD_DIM if h % 2 == 0 else 0)
        fmask[0, feat0:feat0 + HEAD_DIM] = 1.0
        for p in range(N_SPLIT):
            place[p * N_HEADS + h, aug0 + p] = 1.0
            ones[0, width + aug0 + p] = 1.0
            place[p * N_HEADS + h, width + aug0 + N_SPLIT + p] = -1.0
            ones[0, aug0 + N_SPLIT + p] = 1.0
    return jnp.asarray(place, BF16), jnp.asarray(ones), jnp.asarray(fmask)


def _premix(x, shift, scale, g, w1, wf, bfp, wpool, pscale, gpool):
    bsz, seq, _ = x.shape
    ts = TS_PRE
    place, ones, fmask = _aug_constants()
    full = lambda shape: pl.BlockSpec(shape, lambda b, s: (0,) * len(shape))
    per_batch = pl.BlockSpec((1, 1, D_MODEL), lambda b, s: (b, 0, 0))
    return pl.pallas_call(
        _premix_kernel,
        name="premix",
        grid=(bsz, seq // ts),
        in_specs=[
            pl.BlockSpec((1, ts, D_MODEL), lambda b, s: (b, s, 0)),
            per_batch, per_batch,
            full((1, D_MODEL)),
            full((D_MODEL, D_POOL + 3 * D_ATTN)),
            full((D_MODEL, LANES)),
            full((1, LANES)),
            full((len(POOL_WINDOWS), POOL_GROUP, POOL_GROUP)),
            full((1, D_POOL)),
            full((1, D_POOL)),
            full((LANES, 2 * N_HEADS * AUG)),
            full((1, 2 * N_HEADS * AUG)),
            full((1, N_HEADS * AUG)),
        ],
        out_specs=[
            pl.BlockSpec((1, ts, D_POOL), lambda b, s: (b, s, 0)),
            pl.BlockSpec((1, N_HEADS * AUG, ts), lambda b, s: (b, 0, s)),
            pl.BlockSpec((1, ts, N_HEADS * AUG), lambda b, s: (b, s, 0)),
            pl.BlockSpec((1, D_ATTN, ts), lambda b, s: (b, 0, s)),
        ],
        out_shape=[
            jax.ShapeDtypeStruct((bsz, seq, D_POOL), BF16),
            jax.ShapeDtypeStruct((bsz, N_HEADS * AUG, seq), BF16),
            jax.ShapeDtypeStruct((bsz, seq, N_HEADS * AUG), BF16),
            jax.ShapeDtypeStruct((bsz, D_ATTN, seq), BF16),
        ],
        scratch_shapes=[
            pltpu.VMEM((ts + MAX_WINDOW, D_POOL), F32),
            pltpu.VMEM((1, LANES), F32),
        ],
        compiler_params=pltpu.CompilerParams(
            dimension_semantics=("arbitrary", "arbitrary"), vmem_limit_bytes=VMEM_LIMIT),
    )(x, shift, scale, g, w1, wf, bfp, wpool, pscale, gpool, place, ones, fmask)


def _attn_kernel(qt_ref, k_ref, vt_ref, o_ref):
    seq = k_ref.shape[1]
    key_i = lax.broadcasted_iota(jnp.int32, (TK, TQ), 0)
    qry_i = lax.broadcasted_iota(jnp.int32, (TK, TQ), 1)
    causal = key_i <= qry_i
    one_row = lax.broadcasted_iota(jnp.int32, (SUM_ROWS, seq), 0) == 0
    vaug = jnp.concatenate([vt_ref[0], jnp.where(one_row, 1.0, 0.0).astype(BF16)], axis=0)
    for qi in range(seq // TQ):
        qt = qt_ref[0, :, qi * TQ:(qi + 1) * TQ]
        m = jnp.full((1, TQ), -jnp.inf, F32)
        acc = jnp.zeros((HEAD_DIM + SUM_ROWS, TQ), F32)
        for kj in range(qi + 1):
            kt = k_ref[0, kj * TK:(kj + 1) * TK, :]
            st = _dot(kt, qt)
            if kj == qi:
                st = jnp.where(causal, st, -jnp.inf)
            m_new = jnp.maximum(m, jnp.max(st, axis=0, keepdims=True))
            p = jnp.exp2(st - m_new).astype(BF16)
            acc = jnp.exp2(m - m_new) * acc + _dot(vaug[:, kj * TK:(kj + 1) * TK], p)
            m = m_new
        o_ref[0, :, qi * TQ:(qi + 1) * TQ] = acc[:HEAD_DIM] / acc[HEAD_DIM:HEAD_DIM + 1]


def _attn(qt, ka, vt):
    bsz, _, seq = qt.shape
    return pl.pallas_call(
        _attn_kernel,
        name="attn",
        grid=(bsz, N_HEADS),
        in_specs=[
            pl.BlockSpec((1, AUG, seq), lambda b, h: (b, h, 0)),
            pl.BlockSpec((1, seq, AUG), lambda b, h: (b, 0, h)),
            pl.BlockSpec((1, HEAD_DIM, seq), lambda b, h: (b, h, 0)),
        ],
        out_specs=pl.BlockSpec((1, HEAD_DIM, seq), lambda b, h: (b, h, 0)),
        out_shape=jax.ShapeDtypeStruct((bsz, D_ATTN, seq), F32),
        compiler_params=pltpu.CompilerParams(
            dimension_semantics=("arbitrary", "arbitrary"), vmem_limit_bytes=VMEM_LIMIT),
    )(qt, ka, vt)


def _postmix_kernel(x_ref, pool_ref, ot_ref, wout_ref, gattn_ref, gpost_ref, gate_ref,
                    gffn_ref, shift_ref, scale_ref, wrh_ref, wrl_ref,
                    x1_ref, h2_ref, st_ref):
    ya = ot_ref[0].T
    ya = _rms(ya, gattn_ref[...]).astype(BF16)
    mixed = _dot(pool_ref[0], wout_ref[:D_POOL, :]) + _dot(ya, wout_ref[D_POOL:, :])
    x1 = x_ref[0] + gate_ref[0] * _rms(mixed, gpost_ref[...])
    x1_ref[0] = x1
    h2 = _rms(x1, gffn_ref[...]) * (1.0 + scale_ref[0]) + shift_ref[0]
    h2b = h2.astype(BF16)
    h2_ref[0] = h2b
    h2l = (h2 - h2b.astype(F32)).astype(BF16)
    logits = _dot(h2b, wrh_ref[...]) + _dot(h2b, wrl_ref[...]) + _dot(h2l, wrh_ref[...])
    st_ref[...] = jax.nn.sigmoid(logits).T


def _postmix(x, pool, ot, wout, gattn, gpost, gate, gffn, shift, scale, wrh, wrl):
    bsz, seq, _ = x.shape
    ts = TS_POST
    ns = seq // ts
    full = lambda shape: pl.BlockSpec(shape, lambda b, s: (0,) * len(shape))
    per_batch = pl.BlockSpec((1, 1, D_MODEL), lambda b, s: (b, 0, 0))
    tok = lambda width: pl.BlockSpec((1, ts, width), lambda b, s: (b, s, 0))
    return pl.pallas_call(
        _postmix_kernel,
        name="postmix",
        grid=(bsz, ns),
        in_specs=[
            tok(D_MODEL), tok(D_POOL),
            pl.BlockSpec((1, D_ATTN, ts), lambda b, s: (b, 0, s)),
            full((D_MODEL, D_MODEL)),
            full((1, D_ATTN)), full((1, D_MODEL)), per_batch,
            full((1, D_MODEL)), per_batch, per_batch,
            full((D_MODEL, LANES)), full((D_MODEL, LANES)),
        ],
        out_specs=[
            tok(D_MODEL), tok(D_MODEL),
            pl.BlockSpec((LANES, ts), lambda b, s: (0, b * ns + s)),
        ],
        out_shape=[
            jax.ShapeDtypeStruct((bsz, seq, D_MODEL), F32),
            jax.ShapeDtypeStruct((bsz, seq, D_MODEL), BF16),
            jax.ShapeDtypeStruct((LANES, bsz * seq), F32),
        ],
        compiler_params=pltpu.CompilerParams(
            dimension_semantics=("arbitrary", "arbitrary"), vmem_limit_bytes=VMEM_LIMIT),
    )(x, pool, ot, wout, gattn, gpost, gate, gffn, shift, scale, wrh, wrl)


def _route_kernel(st_ref, bias_ref, infok_ref, infot_ref, cnt_ref):
    tile = pl.program_id(0)
    tr = st_ref.shape[1]
    scores = [st_ref[g * GROUP_SIZE:(g + 1) * GROUP_SIZE, :] for g in range(N_EXPERT_GROUPS)]
    sel = [scores[g] + bias_ref[g * GROUP_SIZE:(g + 1) * GROUP_SIZE, :]
           for g in range(N_EXPERT_GROUPS)]
    sub = lax.broadcasted_iota(jnp.int32, (GROUP_SIZE, tr), 0)
    neg = jnp.float32(-jnp.inf)

    rows = []
    for g in range(N_EXPERT_GROUPS):
        v = sel[g]
        m1 = jnp.max(v, axis=0, keepdims=True)
        first = jnp.min(jnp.where(v == m1, sub, GROUP_SIZE), axis=0, keepdims=True)
        m2 = jnp.max(jnp.where(sub == first, neg, v), axis=0, keepdims=True)
        rows.append(m1 + m2)
    gs = jnp.concatenate(rows, axis=0)

    shape = (GROUP_SIZE, tr)
    one = jnp.ones(shape, jnp.int32)
    zero = jnp.zeros(shape, jnp.int32)
    beaten = zero
    for gp in range(N_EXPERT_GROUPS):
        r = jnp.broadcast_to(gs[gp:gp + 1, :], shape)
        tie = jnp.where(sub > gp, one, zero)
        beaten = beaten + jnp.where(r > gs, one, zero) + jnp.where(r == gs, tie, zero)
    keep = jnp.where(beaten < TOPK_GROUPS, 1.0, 0.0)

    masked = [jnp.where(jnp.broadcast_to(keep[g:g + 1, :], shape) > 0.5, sel[g], neg)
              for g in range(N_EXPERT_GROUPS)]

    cnt = [zero for _ in range(N_EXPERT_GROUPS)]
    for gp in range(N_EXPERT_GROUPS):
        for jp in range(GROUP_SIZE):
            r = jnp.broadcast_to(masked[gp][jp:jp + 1, :], shape)
            for g in range(N_EXPERT_GROUPS):
                if g < gp:
                    beats = jnp.where(r > masked[g], one, zero)
                elif g > gp:
                    beats = jnp.where(r >= masked[g], one, zero)
                else:
                    tie = jnp.where(sub > jp, one, zero)
                    beats = jnp.where(r > masked[g], one, zero) + jnp.where(r == masked[g], tie, zero)
                cnt[g] = cnt[g] + beats
    chosen = [cnt[g] < TOP_K for g in range(N_EXPERT_GROUPS)]

    w = [jnp.where(chosen[g], scores[g], 0.0) for g in range(N_EXPERT_GROUPS)]
    denom = w[0].sum(axis=0, keepdims=True)
    for g in range(1, N_EXPERT_GROUPS):
        denom = denom + w[g].sum(axis=0, keepdims=True)
    wfull = jnp.concatenate([w[g] / denom * ROUTED_SCALE for g in range(N_EXPERT_GROUPS)], axis=0)
    mfull = jnp.concatenate([jnp.where(chosen[g], 1.0, 0.0) for g in range(N_EXPERT_GROUPS)],
                            axis=0)
    mb = mfull.astype(BF16)

    e_r = lax.broadcasted_iota(jnp.int32, (N_EXPERTS, N_EXPERTS), 0)
    e_c = lax.broadcasted_iota(jnp.int32, (N_EXPERTS, N_EXPERTS), 1)
    before_e = (e_c < e_r).astype(BF16)
    t_r = lax.broadcasted_iota(jnp.int32, (tr, tr), 0)
    t_c = lax.broadcasted_iota(jnp.int32, (tr, tr), 1)
    before_t = (t_r < t_c).astype(BF16)
    ordinal = _dot(before_e, mb)
    rank = _dot(mb, before_t)
    n = jnp.sum(mfull, axis=1, keepdims=True)
    pieces = jnp.floor((n + (PIECE - 1.0)) * (1.0 / PIECE))
    run_start = PIECE * _dot(before_e, jnp.broadcast_to(pieces, (N_EXPERTS, LANES)).astype(BF16))
    pos = run_start[:, 0:1] + rank

    sub8 = lax.broadcasted_iota(jnp.int32, (TOP_K, tr), 0)
    pos8 = jnp.zeros((TOP_K, tr), F32)
    w8 = jnp.zeros((TOP_K, tr), F32)
    for k in range(TOP_K):
        selk = jnp.where(ordinal == float(k), mfull, 0.0)
        pk = jnp.sum(selk * pos, axis=0, keepdims=True)
        wk = jnp.sum(selk * wfull, axis=0, keepdims=True)
        pos8 = jnp.where(sub8 == k, jnp.broadcast_to(pk, (TOP_K, tr)), pos8)
        w8 = jnp.where(sub8 == k, jnp.broadcast_to(wk, (TOP_K, tr)), w8)
    infok_ref[...] = jnp.concatenate([pos8, w8], axis=0)
    info = jnp.concatenate([pos8, w8, jnp.zeros((LANES - 2 * TOP_K, tr), F32)], axis=0)
    infot_ref[...] = info.T

    @pl.when(tile == 0)
    def _():
        cnt_ref[...] = jnp.zeros_like(cnt_ref)

    lane = lax.broadcasted_iota(jnp.int32, cnt_ref.shape, 1)
    cnt_ref[...] = jnp.where(lane == tile, jnp.broadcast_to(n, cnt_ref.shape), cnt_ref[...])


def _route(st, bias_col):
    t = st.shape[1]
    assert t // TW <= LANES
    return pl.pallas_call(
        _route_kernel,
        name="route",
        grid=(t // TW,),
        in_specs=[
            pl.BlockSpec((LANES, TW), lambda i: (0, i)),
            pl.BlockSpec((N_EXPERTS, 1), lambda i: (0, 0)),
        ],
        out_specs=[
            pl.BlockSpec((2 * TOP_K, TW), lambda i: (0, i)),
            pl.BlockSpec((TW, LANES), lambda i: (i, 0)),
            pl.BlockSpec((N_EXPERTS, LANES), lambda i: (0, 0)),
        ],
        out_shape=[
            jax.ShapeDtypeStruct((2 * TOP_K, t), F32),
            jax.ShapeDtypeStruct((t, LANES), F32),
            jax.ShapeDtypeStruct((N_EXPERTS, LANES), F32),
        ],
        compiler_params=pltpu.CompilerParams(dimension_semantics=("arbitrary",)),
    )(st, bias_col)


def _swiglu(xb, wg, wu):
    g = _dot(xb, wg)
    return (g * jax.nn.sigmoid(g)) * _dot(xb, wu)


def _piece_copy(src_ref, src_row, dst_ref, dst_row, sem, n_pieces=1):
    rows = n_pieces * PIECE
    src = src_ref.at[pl.ds(pl.multiple_of(src_row * PIECE, PIECE), rows), :]
    dst = dst_ref.at[pl.ds(pl.multiple_of(dst_row * PIECE, PIECE), rows), :]
    return pltpu.make_async_copy(src, dst, sem)


N_BITS = 5
LOCAL_BITS = 8


def _for_each_piece(runs_ref, tile, fn):
    def per_expert(e, carry):
        run = runs_ref[tile * N_EXPERTS + e]
        n = run & ((1 << N_BITS) - 1)
        loc = lax.shift_right_logical(run, N_BITS) & ((1 << LOCAL_BITS) - 1)
        glob = lax.shift_right_logical(run, N_BITS + LOCAL_BITS)

        def per_pair(p, inner):
            fn(loc + 2 * p, glob + 2 * p, 2)
            return inner

        lax.fori_loop(0, lax.shift_right_logical(n, 1), per_pair, 0)

        @pl.when((n & 1) == 1)
        def _():
            fn(loc + n - 1, glob + n - 1, 1)

        return carry

    lax.fori_loop(0, N_EXPERTS, per_expert, 0)


def _wait_pieces(total, copy_of):
    for bit in range((LROWS // PIECE).bit_length()):
        @pl.when((lax.shift_right_logical(total, bit) & 1) == 1)
        def _():
            copy_of(1 << bit).wait()


def _dispatch_kernel(runs_ref, total_ref, tail_ref, ntail_ref, nv_ref,
                     h2_ref, infok_ref, xs_ref, lbuf, zbuf, sem, zsem):
    tile = pl.program_id(0)
    last = pl.num_programs(0) - 1
    slot = tile % 2
    n_blocks = xs_ref.shape[0] // RB

    def spare_blocks(fn):
        def per_block(b, carry):
            fn(b)
            return carry
        lax.fori_loop(nv_ref[0], n_blocks, per_block, 0)

    def zero_block(b):
        dst = xs_ref.at[pl.ds(pl.multiple_of(b * RB, RB), RB), :]
        return pltpu.make_async_copy(zbuf, dst, zsem)

    def start(t, s):
        _for_each_piece(
            runs_ref, t,
            lambda lp, gp, n: _piece_copy(lbuf.at[s], lp, xs_ref, gp, sem.at[s], n).start())

    def wait(t, s):
        _wait_pieces(total_ref[t],
                     lambda n: _piece_copy(lbuf.at[s], 0, xs_ref, 0, sem.at[s], n))

    @pl.when(tile == 0)
    def _():
        zbuf[...] = jnp.zeros_like(zbuf)

        def tails(fn):
            def per_expert(e, carry):
                def per_piece(p, inner):
                    fn(tail_ref[e] + p)
                    return inner
                lax.fori_loop(0, ntail_ref[e], per_piece, 0)
                return carry
            lax.fori_loop(0, N_EXPERTS, per_expert, 0)

        tails(lambda gp: _piece_copy(zbuf, 0, xs_ref, gp, zsem).start())
        tails(lambda gp: _piece_copy(zbuf, 0, xs_ref, gp, zsem).wait())
        spare_blocks(lambda b: zero_block(b).start())

    @pl.when(tile >= 2)
    def _():
        wait(tile - 2, slot)

    xb = h2_ref[...]
    for j in range(LROWS // LCHUNK):
        r = (lax.broadcasted_iota(jnp.int32, (LCHUNK, TW), 0) + j * LCHUNK).astype(F32)
        onehot = jnp.zeros((LCHUNK, TW), F32)
        for k in range(TOP_K):
            onehot = jnp.where(r == infok_ref[k:k + 1, :], 1.0, onehot)
        lbuf[slot, j * LCHUNK:(j + 1) * LCHUNK, :] = _dot(onehot.astype(BF16), xb).astype(BF16)

    start(tile, slot)

    @pl.when(tile == last)
    def _():
        @pl.when(tile >= 1)
        def _():
            wait(tile - 1, 1 - slot)
        wait(tile, slot)
        spare_blocks(lambda b: zero_block(b).wait())


def _dispatch(tables, h2, infok, n_rows):
    t = h2.shape[0]
    return pl.pallas_call(
        _dispatch_kernel,
        name="dispatch",
        grid_spec=pltpu.PrefetchScalarGridSpec(
            num_scalar_prefetch=5,
            grid=(t // TW,),
            in_specs=[
                pl.BlockSpec((TW, D_MODEL), lambda i, *_: (i, 0)),
                pl.BlockSpec((2 * TOP_K, TW), lambda i, *_: (0, i)),
            ],
            out_specs=pl.BlockSpec(memory_space=pl.ANY),
            scratch_shapes=[
                pltpu.VMEM((2, LROWS, D_MODEL), BF16),
                pltpu.VMEM((RB, D_MODEL), BF16),
                pltpu.SemaphoreType.DMA((2,)),
                pltpu.SemaphoreType.DMA(()),
            ],
        ),
        out_shape=jax.ShapeDtypeStruct((n_rows, D_MODEL), BF16),
        compiler_params=pltpu.CompilerParams(
            dimension_semantics=("arbitrary",), vmem_limit_bytes=VMEM_LIMIT),
    )(*tables, h2, infok)


def _experts_kernel(be_ref, nv_ref, xs_ref, wg_ref, wu_ref, wd_ref, ys_ref, wgb, wub, wdb):
    i = pl.program_id(0)

    @pl.when(i < nv_ref[0])
    def _():
        @pl.when((i == 0) | (be_ref[i] != be_ref[jnp.maximum(i - 1, 0)]))
        def _():
            wgb[...] = wg_ref[0].astype(BF16)
            wub[...] = wu_ref[0].astype(BF16)
            wdb[...] = wd_ref[0].astype(BF16)

        for j in range(RB // RSUB):
            rows = slice(j * RSUB, (j + 1) * RSUB)
            h = _swiglu(xs_ref[rows, :], wgb[...], wub[...])
            ys_ref[rows, :] = _dot(h.astype(BF16), wdb[...]).astype(BF16)


def _experts(block_expert, n_valid, xs, wg, wu, wd):
    n_blocks = xs.shape[0] // RB
    rows = pl.BlockSpec((RB, D_MODEL), lambda i, be, nv: (jnp.minimum(i, nv[0] - 1), 0))
    return pl.pallas_call(
        _experts_kernel,
        name="experts",
        grid_spec=pltpu.PrefetchScalarGridSpec(
            num_scalar_prefetch=2,
            grid=(n_blocks,),
            in_specs=[
                rows,
                pl.BlockSpec((1, D_MODEL, D_EXPERT), lambda i, be, nv: (be[i], 0, 0)),
                pl.BlockSpec((1, D_MODEL, D_EXPERT), lambda i, be, nv: (be[i], 0, 0)),
                pl.BlockSpec((1, D_EXPERT, D_MODEL), lambda i, be, nv: (be[i], 0, 0)),
            ],
            out_specs=rows,
            scratch_shapes=[
                pltpu.VMEM((D_MODEL, D_EXPERT), BF16),
                pltpu.VMEM((D_MODEL, D_EXPERT), BF16),
                pltpu.VMEM((D_EXPERT, D_MODEL), BF16),
            ],
        ),
        out_shape=jax.ShapeDtypeStruct(xs.shape, BF16),
        input_output_aliases={2: 0},
        compiler_params=pltpu.CompilerParams(
            dimension_semantics=("arbitrary",), vmem_limit_bytes=VMEM_LIMIT),
    )(block_expert, n_valid, xs, wg, wu, wd)


def _combine_kernel(runs_ref, total_ref,
                    infot_ref, h2_ref, x1_ref, gate_ref, gpost_ref, wsg_ref, wsu_ref, wsd_ref,
                    ys_ref, o_ref, ybuf, sem):
    tile = pl.program_id(0)
    last = pl.num_programs(0) - 1
    slot = tile % 2

    def start(t, s):
        _for_each_piece(
            runs_ref, t,
            lambda lp, gp, n: _piece_copy(ys_ref, gp, ybuf.at[s], lp, sem.at[s], n).start())

    def wait(t, s):
        _wait_pieces(total_ref[t],
                     lambda n: _piece_copy(ys_ref, 0, ybuf.at[s], 0, sem.at[s], n))

    @pl.when(tile == 0)
    def _():
        ybuf[...] = jnp.zeros_like(ybuf)
        start(tile, slot)

    @pl.when(tile < last)
    def _():
        start(tile + 1, 1 - slot)

    wait(tile, slot)

    info = infot_ref[...]
    routed = jnp.zeros((TW, D_MODEL), F32)
    for j in range(LROWS // LCHUNK):
        r = (lax.broadcasted_iota(jnp.int32, (TW, LCHUNK), 1) + j * LCHUNK).astype(F32)
        wmat = jnp.zeros((TW, LCHUNK), F32)
        for k in range(TOP_K):
            wmat = jnp.where(r == info[:, k:k + 1], info[:, TOP_K + k:TOP_K + k + 1], wmat)
        routed = routed + _dot(wmat.astype(BF16), ybuf[slot, j * LCHUNK:(j + 1) * LCHUNK, :])

    hs = _swiglu(h2_ref[...], wsg_ref[...], wsu_ref[...])
    ff = routed + _dot(hs.astype(BF16), wsd_ref[...])
    o_ref[...] = x1_ref[...] + gate_ref[0] * _rms(ff, gpost_ref[...])


def _combine(tables, infot, h2, x1, gate, gpost, wsg, wsu, wsd, ys, seq):
    t = h2.shape[0]
    per_seq = seq // TW
    full = lambda shape: pl.BlockSpec(shape, lambda i, *_: (0,) * len(shape))
    tok = lambda width: pl.BlockSpec((TW, width), lambda i, *_: (i, 0))
    return pl.pallas_call(
        _combine_kernel,
        name="combine",
        grid_spec=pltpu.PrefetchScalarGridSpec(
            num_scalar_prefetch=2,
            grid=(t // TW,),
            in_specs=[
                tok(LANES), tok(D_MODEL), tok(D_MODEL),
                pl.BlockSpec((1, 1, D_MODEL), lambda i, *_: (i // per_seq, 0, 0)),
                full((1, D_MODEL)),
                full((D_MODEL, D_SHARED)), full((D_MODEL, D_SHARED)), full((D_SHARED, D_MODEL)),
                pl.BlockSpec(memory_space=pl.ANY),
            ],
            out_specs=tok(D_MODEL),
            scratch_shapes=[
                pltpu.VMEM((2, LROWS, D_MODEL), BF16),
                pltpu.SemaphoreType.DMA((2,)),
            ],
        ),
        out_shape=jax.ShapeDtypeStruct((t, D_MODEL), F32),
        compiler_params=pltpu.CompilerParams(
            dimension_semantics=("arbitrary",), vmem_limit_bytes=VMEM_LIMIT),
    )(*tables, infot, h2, x1, gate, gpost, wsg, wsu, wsd, ys)


def _dispatch_plan(cnt, n_tiles):
    n = cnt[:, :n_tiles].astype(jnp.int32)
    pieces = (n + PIECE - 1) // PIECE
    local = jnp.cumsum(pieces, axis=0) - pieces
    seg = jnp.sum(pieces, axis=1)
    per_block = RB // PIECE
    seg_pad = (seg + per_block - 1) // per_block * per_block
    seg_end = jnp.cumsum(seg_pad)
    seg_start = seg_end - seg_pad
    glob = seg_start[:, None] + jnp.cumsum(pieces, axis=1) - pieces
    n_blocks = (TOP_K * TW * n_tiles + N_EXPERTS * n_tiles * (PIECE - 1)
                + N_EXPERTS * (RB - PIECE) + RB - 1) // RB
    n_valid = seg_end[-1] // per_block
    blk = jnp.minimum(jnp.arange(n_blocks, dtype=jnp.int32), n_valid - 1)
    block_end = seg_end // per_block
    block_expert = jnp.sum((block_end[None, :] <= blk[:, None]).astype(jnp.int32), axis=1)
    block_expert = jnp.minimum(block_expert, N_EXPERTS - 1)
    assert TW // PIECE < 1 << N_BITS and LROWS // PIECE <= 1 << LOCAL_BITS
    assert n_blocks * per_block < 1 << (31 - N_BITS - LOCAL_BITS)
    runs = pieces | (local << N_BITS) | (glob << (N_BITS + LOCAL_BITS))
    tables = (runs.T.reshape(-1).astype(jnp.int32), jnp.sum(pieces, axis=0).astype(jnp.int32))
    tails = ((seg_start + seg).astype(jnp.int32), (seg_pad - seg).astype(jnp.int32))
    return tables, tails, block_expert, n_valid.reshape(1).astype(jnp.int32), n_blocks * RB


def kernel(x, c, w_ada, b_ada, g_pre_mix, g_post_mix, g_pre_ffn, g_post_ffn, w_in, b_forget,
           w_pool, pool_scale, g_pool_out, g_attn_out, w_out, w_router, router_bias,
           w_gate, w_up, w_down, ws_gate, ws_up, ws_down):
    bsz, seq, d = x.shape
    depth = w_ada.shape[0]
    for l in range(depth):
        mod = _ada(c, w_ada[l], b_ada[l][None, :])
        shift_m, scale_m, gate_m, shift_f, scale_f, gate_f = [
            m.reshape(bsz, 1, d) for m in jnp.split(mod, 6, axis=-1)]

        wi = w_in[l]
        qs = HEAD_DIM ** -0.5 * LOG2E
        w1 = jnp.concatenate(
            [wi[:, :D_POOL], wi[:, D_POOL:D_POOL + D_ATTN] * qs,
             wi[:, D_POOL + D_ATTN:D_POOL + 3 * D_ATTN]], axis=1).astype(BF16)
        pad = LANES - N_SPLIT * N_HEADS
        wf = jnp.pad(jnp.tile(wi[:, D_POOL + 3 * D_ATTN:], (1, N_SPLIT)),
                     ((0, 0), (0, pad))).astype(BF16)
        bfp = jnp.pad(jnp.tile(b_forget[l], N_SPLIT), (0, pad))[None, :]

        pool, qt, ka, vt = _premix(
            x, shift_m, scale_m, g_pre_mix[l][None, :], w1, wf, bfp, w_pool[l].astype(BF16),
            pool_scale[l][None, :], g_pool_out[l][None, :])
        ot = _attn(qt, ka, vt)

        wr = jnp.pad(w_router[l], ((0, 0), (0, LANES - N_EXPERTS)))
        wrh = wr.astype(BF16)
        wrl = (wr - wrh.astype(F32)).astype(BF16)
        x1, h2, st = _postmix(
            x, pool, ot, w_out[l].astype(BF16), g_attn_out[l][None, :], g_post_mix[l][None, :],
            gate_m, g_pre_ffn[l][None, :], shift_f, scale_f, wrh, wrl)

        t = bsz * seq
        infok, infot, cnt = _route(st, router_bias[l][:, None])
        tables, tails, block_expert, n_valid, n_rows = _dispatch_plan(cnt, t // TW)
        h2f = h2.reshape(t, d)
        xs = _dispatch(tables + tails + (n_valid,), h2f, infok, n_rows)
        ys = _experts(block_expert, n_valid, xs, w_gate[l], w_up[l], w_down[l])
        out = _combine(tables, infot, h2f, x1.reshape(t, d), gate_f, g_post_ffn[l][None, :],
                       ws_gate[l].astype(BF16), ws_up[l].astype(BF16), ws_down[l].astype(BF16),
                       ys, seq)
        x = out.reshape(bsz, seq, d)
    return x
```

```python
import functools

import numpy as np
import jax
import jax.numpy as jnp
from jax import lax
from jax.experimental import pallas as pl
from jax.experimental.pallas import tpu as pltpu

D_MODEL = 1024
D_POOL = 512
POOL_WINDOWS = (2, 4, 8, 16)
POOL_GROUP = 128
MAX_WINDOW = max(POOL_WINDOWS)
D_ATTN = 512
HEAD_DIM = 64
N_HEADS = 8
N_EXPERTS = 64
N_EXPERT_GROUPS = 8
GROUP_SIZE = N_EXPERTS // N_EXPERT_GROUPS
TOPK_GROUPS = 4
TOP_K = 8
D_EXPERT = 256
D_SHARED = 256
ROUTED_SCALE = 2.5
EPS = 1e-6

LOG2E = 1.4426950408889634
SUM_ROWS = 16
LANES = 128
N_SPLIT = 3
AUG = LANES

TS_PRE = 256
TS_POST = 256
TQ = 512
TK = 512
TW = 256
PIECE = 16
RB = 1024
RSUB = 256
LCHUNK = 512
LROWS = -(-(TOP_K * TW + N_EXPERTS * (PIECE - 1)) // LCHUNK) * LCHUNK

F32 = jnp.float32
BF16 = jnp.bfloat16
VMEM_LIMIT = 56 * 1024 * 1024


def _rms(v, g):
    return v * lax.rsqrt(jnp.mean(v * v, axis=-1, keepdims=True) + EPS) * g


def _split3(v):
    hi = v.astype(BF16)
    r1 = v - hi.astype(F32)
    mid = r1.astype(BF16)
    r2 = r1 - mid.astype(F32)
    lo = r2.astype(BF16)
    return hi, mid, lo


def _dot(a, b):
    return jnp.dot(a, b, preferred_element_type=F32)


def _ada_kernel(c_ref, w_ref, b_ref, o_ref):
    o_ref[...] = _dot(c_ref[...].astype(BF16), w_ref[0].astype(BF16)) + b_ref[...]


def _ada(c, w, b, layer):
    bsz = c.shape[0]
    n = w.shape[2]
    return pl.pallas_call(
        _ada_kernel,
        name="ada",
        grid=(n // D_MODEL,),
        in_specs=[
            pl.BlockSpec((bsz, D_MODEL), lambda j: (0, 0)),
            pl.BlockSpec((1, D_MODEL, D_MODEL), lambda j: (layer, 0, j)),
            pl.BlockSpec((1, D_MODEL), lambda j: (0, j)),
        ],
        out_specs=pl.BlockSpec((bsz, D_MODEL), lambda j: (0, j)),
        out_shape=jax.ShapeDtypeStruct((bsz, n), F32),
    )(c, w, b)


def _premix_kernel(x_ref, shift_ref, scale_ref, g_ref, w1_ref, wf_ref, bf_ref, wpool_ref,
                   pscale_ref, gpool_ref, place_ref, ones_ref, fmask_ref,
                   pool_ref, qt_ref, k_ref, vt_ref,
                   uext_ref, cum_ref):
    s = pl.program_id(1)
    ts = x_ref.shape[1]

    @pl.when(s == 0)
    def _():
        uext_ref[0:MAX_WINDOW, :] = jnp.zeros((MAX_WINDOW, D_POOL), F32)
        cum_ref[...] = jnp.zeros_like(cum_ref)

    x = x_ref[0]
    h = _rms(x, g_ref[...]) * (1.0 + scale_ref[0]) + shift_ref[0]
    hb = h.astype(BF16)
    proj = _dot(hb, w1_ref[...])
    u = proj[:, :D_POOL]
    q = proj[:, D_POOL:D_POOL + D_ATTN]
    k = proj[:, D_POOL + D_ATTN:D_POOL + 2 * D_ATTN]
    v = proj[:, D_POOL + 2 * D_ATTN:]

    uext_ref[MAX_WINDOW:, :] = u
    pos = (s * ts + lax.broadcasted_iota(jnp.int32, (ts, 1), 0) + 1).astype(F32)
    ys = []
    for g, w in enumerate(POOL_WINDOWS):
        c0 = g * POOL_GROUP
        acc = uext_ref[MAX_WINDOW:, c0:c0 + POOL_GROUP]
        for j in range(1, w):
            acc = acc + uext_ref[MAX_WINDOW - j:MAX_WINDOW - j + ts, c0:c0 + POOL_GROUP]
        pooled = acc / jnp.minimum(pos, float(w)) - u[:, c0:c0 + POOL_GROUP]
        ys.append(_dot(pooled.astype(BF16), wpool_ref[g]))
    ypool = jnp.concatenate(ys, axis=1) * pscale_ref[...]
    pool_ref[0] = _rms(ypool, gpool_ref[...]).astype(BF16)
    uext_ref[0:MAX_WINDOW, :] = uext_ref[ts:ts + MAX_WINDOW, :]

    z = _dot(hb, wf_ref[...]) + bf_ref[...]
    logf = jnp.minimum(z, 0.0) - jnp.log1p(jnp.exp(-jnp.abs(z)))
    row = lax.broadcasted_iota(jnp.int32, (ts, ts), 0)
    col = lax.broadcasted_iota(jnp.int32, (ts, ts), 1)
    tri = (col <= row).astype(BF16)
    cum = cum_ref[...]
    for piece in _split3(logf):
        cum = cum + _dot(tri, piece)
    cum_ref[...] = cum[ts - 1:ts, :]

    hi, mid, lo = [piece.astype(F32) for piece in _split3(cum * LOG2E)]
    lane = lax.broadcasted_iota(jnp.int32, (ts, LANES), 1)
    pieces = jnp.where(lane < N_HEADS, hi, jnp.where(lane < 2 * N_HEADS, mid, lo))
    aug = _dot(pieces.astype(BF16), place_ref[...]) + ones_ref[...]
    aug_q = aug[:, :N_HEADS * AUG]
    aug_k = aug[:, N_HEADS * AUG:]

    def expand(a):
        blocks = []
        for j in range(D_ATTN // LANES):
            blk = a[:, j * LANES:(j + 1) * LANES]
            blocks += [blk, blk]
        return jnp.concatenate(blocks, axis=1)

    fmask = fmask_ref[...]
    qa = expand(q) * fmask + aug_q
    ka = expand(k) * fmask + aug_k
    qt_ref[0] = qa.T.astype(BF16)
    k_ref[0] = ka.astype(BF16)
    vt_ref[0] = v.T.astype(BF16)


def _aug_constants():
    width = N_HEADS * AUG
    place = np.zeros((LANES, 2 * width), np.float32)
    ones = np.zeros((1, 2 * width), np.float32)
    fmask = np.zeros((1, width), np.float32)
    for h in range(N_HEADS):
        feat0 = h * AUG + (0 if h % 2 == 0 else HEAD_DIM)
        aug0 = h * AUG + (HEAD_DIM if h % 2 == 0 else 0)
        fmask[0, feat0:feat0 + HEAD_DIM] = 1.0
        for p in range(N_SPLIT):
            place[p * N_HEADS + h, aug0 + p] = 1.0
            ones[0, width + aug0 + p] = 1.0
            place[p * N_HEADS + h, width + aug0 + N_SPLIT + p] = -1.0
            ones[0, aug0 + N_SPLIT + p] = 1.0
    return jnp.asarray(place, BF16), jnp.asarray(ones), jnp.asarray(fmask)


def _premix(x, shift, scale, g, w1, wf, bfp, wpool, pscale, gpool):
    bsz, seq, _ = x.shape
    ts = TS_PRE
    place, ones, fmask = _aug_constants()
    full = lambda shape: pl.BlockSpec(shape, lambda b, s: (0,) * len(shape))
    per_batch = pl.BlockSpec((1, 1, D_MODEL), lambda b, s: (b, 0, 0))
    return pl.pallas_call(
        _premix_kernel,
        name="premix",
        grid=(bsz, seq // ts),
        in_specs=[
            pl.BlockSpec((1, ts, D_MODEL), lambda b, s: (b, s, 0)),
            per_batch, per_batch,
            full((1, D_MODEL)),
            full((D_MODEL, D_POOL + 3 * D_ATTN)),
            full((D_MODEL, LANES)),
            full((1, LANES)),
            full((len(POOL_WINDOWS), POOL_GROUP, POOL_GROUP)),
            full((1, D_POOL)),
            full((1, D_POOL)),
            full((LANES, 2 * N_HEADS * AUG)),
            full((1, 2 * N_HEADS * AUG)),
            full((1, N_HEADS * AUG)),
        ],
        out_specs=[
            pl.BlockSpec((1, ts, D_POOL), lambda b, s: (b, s, 0)),
            pl.BlockSpec((1, N_HEADS * AUG, ts), lambda b, s: (b, 0, s)),
            pl.BlockSpec((1, ts, N_HEADS * AUG), lambda b, s: (b, s, 0)),
            pl.BlockSpec((1, D_ATTN, ts), lambda b, s: (b, 0, s)),
        ],
        out_shape=[
            jax.ShapeDtypeStruct((bsz, seq, D_POOL), BF16),
            jax.ShapeDtypeStruct((bsz, N_HEADS * AUG, seq), BF16),
            jax.ShapeDtypeStruct((bsz, seq, N_HEADS * AUG), BF16),
            jax.ShapeDtypeStruct((bsz, D_ATTN, seq), BF16),
        ],
        scratch_shapes=[
            pltpu.VMEM((ts + MAX_WINDOW, D_POOL), F32),
            pltpu.VMEM((1, LANES), F32),
        ],
        compiler_params=pltpu.CompilerParams(
            dimension_semantics=("arbitrary", "arbitrary"), vmem_limit_bytes=VMEM_LIMIT),
    )(x, shift, scale, g, w1, wf, bfp, wpool, pscale, gpool, place, ones, fmask)


def _attn_kernel(qt_ref, k_ref, vt_ref, o_ref):
    seq = k_ref.shape[1]
    key_i = lax.broadcasted_iota(jnp.int32, (TK, TQ), 0)
    qry_i = lax.broadcasted_iota(jnp.int32, (TK, TQ), 1)
    causal = key_i <= qry_i
    one_row = lax.broadcasted_iota(jnp.int32, (SUM_ROWS, seq), 0) == 0
    vaug = jnp.concatenate([vt_ref[0], jnp.where(one_row, 1.0, 0.0).astype(BF16)], axis=0)
    for qi in range(seq // TQ):
        qt = qt_ref[0, :, qi * TQ:(qi + 1) * TQ]
        m = jnp.full((1, TQ), -jnp.inf, F32)
        acc = jnp.zeros((HEAD_DIM + SUM_ROWS, TQ), F32)
        for kj in range(qi + 1):
            kt = k_ref[0, kj * TK:(kj + 1) * TK, :]
            st = _dot(kt, qt)
            if kj == qi:
                st = jnp.where(causal, st, -jnp.inf)
            m_new = jnp.maximum(m, jnp.max(st, axis=0, keepdims=True))
            p = jnp.exp2(st - m_new).astype(BF16)
            acc = jnp.exp2(m - m_new) * acc + _dot(vaug[:, kj * TK:(kj + 1) * TK], p)
            m = m_new
        o_ref[0, :, qi * TQ:(qi + 1) * TQ] = acc[:HEAD_DIM] / acc[HEAD_DIM:HEAD_DIM + 1]


def _attn(qt, ka, vt):
    bsz, _, seq = qt.shape
    return pl.pallas_call(
        _attn_kernel,
        name="attn",
        grid=(bsz, N_HEADS),
        in_specs=[
            pl.BlockSpec((1, AUG, seq), lambda b, h: (b, h, 0)),
            pl.BlockSpec((1, seq, AUG), lambda b, h: (b, 0, h)),
            pl.BlockSpec((1, HEAD_DIM, seq), lambda b, h: (b, h, 0)),
        ],
        out_specs=pl.BlockSpec((1, HEAD_DIM, seq), lambda b, h: (b, h, 0)),
        out_shape=jax.ShapeDtypeStruct((bsz, D_ATTN, seq), F32),
        compiler_params=pltpu.CompilerParams(
            dimension_semantics=("arbitrary", "arbitrary"), vmem_limit_bytes=VMEM_LIMIT),
    )(qt, ka, vt)


def _postmix_kernel(x_ref, pool_ref, ot_ref, wout_ref, gattn_ref, gpost_ref, gate_ref,
                    gffn_ref, shift_ref, scale_ref, wrh_ref, wrl_ref,
                    x1_ref, h2_ref, st_ref):
    ya = ot_ref[0].T
    ya = _rms(ya, gattn_ref[...]).astype(BF16)
    mixed = _dot(pool_ref[0], wout_ref[:D_POOL, :]) + _dot(ya, wout_ref[D_POOL:, :])
    x1 = x_ref[0] + gate_ref[0] * _rms(mixed, gpost_ref[...])
    x1_ref[0] = x1
    h2 = _rms(x1, gffn_ref[...]) * (1.0 + scale_ref[0]) + shift_ref[0]
    h2b = h2.astype(BF16)
    h2_ref[0] = h2b
    h2l = (h2 - h2b.astype(F32)).astype(BF16)
    logits = _dot(h2b, wrh_ref[...]) + _dot(h2b, wrl_ref[...]) + _dot(h2l, wrh_ref[...])
    st_ref[...] = jax.nn.sigmoid(logits).T


def _postmix(x, pool, ot, wout, gattn, gpost, gate, gffn, shift, scale, wrh, wrl):
    bsz, seq, _ = x.shape
    ts = TS_POST
    ns = seq // ts
    full = lambda shape: pl.BlockSpec(shape, lambda b, s: (0,) * len(shape))
    per_batch = pl.BlockSpec((1, 1, D_MODEL), lambda b, s: (b, 0, 0))
    tok = lambda width: pl.BlockSpec((1, ts, width), lambda b, s: (b, s, 0))
    return pl.pallas_call(
        _postmix_kernel,
        name="postmix",
        grid=(bsz, ns),
        in_specs=[
            tok(D_MODEL), tok(D_POOL),
            pl.BlockSpec((1, D_ATTN, ts), lambda b, s: (b, 0, s)),
            full((D_MODEL, D_MODEL)),
            full((1, D_ATTN)), full((1, D_MODEL)), per_batch,
            full((1, D_MODEL)), per_batch, per_batch,
            full((D_MODEL, LANES)), full((D_MODEL, LANES)),
        ],
        out_specs=[
            tok(D_MODEL), tok(D_MODEL),
            pl.BlockSpec((LANES, ts), lambda b, s: (0, b * ns + s)),
        ],
        out_shape=[
            jax.ShapeDtypeStruct((bsz, seq, D_MODEL), F32),
            jax.ShapeDtypeStruct((bsz, seq, D_MODEL), BF16),
            jax.ShapeDtypeStruct((LANES, bsz * seq), F32),
        ],
        compiler_params=pltpu.CompilerParams(
            dimension_semantics=("arbitrary", "arbitrary"), vmem_limit_bytes=VMEM_LIMIT),
    )(x, pool, ot, wout, gattn, gpost, gate, gffn, shift, scale, wrh, wrl)


def _route_kernel(st_ref, bias_ref, infok_ref, infot_ref, cnt_ref):
    tile = pl.program_id(0)
    tr = st_ref.shape[1]
    scores = [st_ref[g * GROUP_SIZE:(g + 1) * GROUP_SIZE, :] for g in range(N_EXPERT_GROUPS)]
    sel = [scores[g] + bias_ref[g * GROUP_SIZE:(g + 1) * GROUP_SIZE, :]
           for g in range(N_EXPERT_GROUPS)]
    sub = lax.broadcasted_iota(jnp.int32, (GROUP_SIZE, tr), 0)
    neg = jnp.float32(-jnp.inf)

    rows = []
    for g in range(N_EXPERT_GROUPS):
        v = sel[g]
        m1 = jnp.max(v, axis=0, keepdims=True)
        first = jnp.min(jnp.where(v == m1, sub, GROUP_SIZE), axis=0, keepdims=True)
        m2 = jnp.max(jnp.where(sub == first, neg, v), axis=0, keepdims=True)
        rows.append(m1 + m2)
    gs = jnp.concatenate(rows, axis=0)

    shape = (GROUP_SIZE, tr)
    one = jnp.ones(shape, jnp.int32)
    zero = jnp.zeros(shape, jnp.int32)
    beaten = zero
    for gp in range(N_EXPERT_GROUPS):
        r = jnp.broadcast_to(gs[gp:gp + 1, :], shape)
        tie = jnp.where(sub > gp, one, zero)
        beaten = beaten + jnp.where(r > gs, one, zero) + jnp.where(r == gs, tie, zero)
    keep = jnp.where(beaten < TOPK_GROUPS, 1.0, 0.0)

    masked = [jnp.where(jnp.broadcast_to(keep[g:g + 1, :], shape) > 0.5, sel[g], neg)
              for g in range(N_EXPERT_GROUPS)]

    cnt = [zero for _ in range(N_EXPERT_GROUPS)]
    for gp in range(N_EXPERT_GROUPS):
        for jp in range(GROUP_SIZE):
            r = jnp.broadcast_to(masked[gp][jp:jp + 1, :], shape)
            for g in range(N_EXPERT_GROUPS):
                if g < gp:
                    beats = jnp.where(r > masked[g], one, zero)
                elif g > gp:
                    beats = jnp.where(r >= masked[g], one, zero)
                else:
                    tie = jnp.where(sub > jp, one, zero)
                    beats = jnp.where(r > masked[g], one, zero) + jnp.where(r == masked[g], tie, zero)
                cnt[g] = cnt[g] + beats
    chosen = [cnt[g] < TOP_K for g in range(N_EXPERT_GROUPS)]

    w = [jnp.where(chosen[g], scores[g], 0.0) for g in range(N_EXPERT_GROUPS)]
    denom = w[0].sum(axis=0, keepdims=True)
    for g in range(1, N_EXPERT_GROUPS):
        denom = denom + w[g].sum(axis=0, keepdims=True)
    wfull = jnp.concatenate([w[g] / denom * ROUTED_SCALE for g in range(N_EXPERT_GROUPS)], axis=0)
    mfull = jnp.concatenate([jnp.where(chosen[g], 1.0, 0.0) for g in range(N_EXPERT_GROUPS)],
                            axis=0)
    mb = mfull.astype(BF16)

    e_r = lax.broadcasted_iota(jnp.int32, (N_EXPERTS, N_EXPERTS), 0)
    e_c = lax.broadcasted_iota(jnp.int32, (N_EXPERTS, N_EXPERTS), 1)
    before_e = (e_c < e_r).astype(BF16)
    t_r = lax.broadcasted_iota(jnp.int32, (tr, tr), 0)
    t_c = lax.broadcasted_iota(jnp.int32, (tr, tr), 1)
    before_t = (t_r < t_c).astype(BF16)
    ordinal = _dot(before_e, mb)
    rank = _dot(mb, before_t)
    n = jnp.sum(mfull, axis=1, keepdims=True)
    pieces = jnp.floor((n + (PIECE - 1.0)) * (1.0 / PIECE))
    run_start = PIECE * _dot(before_e, jnp.broadcast_to(pieces, (N_EXPERTS, LANES)).astype(BF16))
    pos = run_start[:, 0:1] + rank

    sub8 = lax.broadcasted_iota(jnp.int32, (TOP_K, tr), 0)
    pos8 = jnp.zeros((TOP_K, tr), F32)
    w8 = jnp.zeros((TOP_K, tr), F32)
    for k in range(TOP_K):
        selk = jnp.where(ordinal == float(k), mfull, 0.0)
        pk = jnp.sum(selk * pos, axis=0, keepdims=True)
        wk = jnp.sum(selk * wfull, axis=0, keepdims=True)
        pos8 = jnp.where(sub8 == k, jnp.broadcast_to(pk, (TOP_K, tr)), pos8)
        w8 = jnp.where(sub8 == k, jnp.broadcast_to(wk, (TOP_K, tr)), w8)
    infok_ref[...] = jnp.concatenate([pos8, w8], axis=0)
    info = jnp.concatenate([pos8, w8, jnp.zeros((LANES - 2 * TOP_K, tr), F32)], axis=0)
    infot_ref[...] = info.T

    @pl.when(tile == 0)
    def _():
        cnt_ref[...] = jnp.zeros_like(cnt_ref)

    lane = lax.broadcasted_iota(jnp.int32, cnt_ref.shape, 1)
    cnt_ref[...] = jnp.where(lane == tile, jnp.broadcast_to(n, cnt_ref.shape), cnt_ref[...])


def _route(st, bias_col):
    t = st.shape[1]
    assert t // TW <= LANES
    return pl.pallas_call(
        _route_kernel,
        name="route",
        grid=(t // TW,),
        in_specs=[
            pl.BlockSpec((LANES, TW), lambda i: (0, i)),
            pl.BlockSpec((N_EXPERTS, 1), lambda i: (0, 0)),
        ],
        out_specs=[
            pl.BlockSpec((2 * TOP_K, TW), lambda i: (0, i)),
            pl.BlockSpec((TW, LANES), lambda i: (i, 0)),
            pl.BlockSpec((N_EXPERTS, LANES), lambda i: (0, 0)),
        ],
        out_shape=[
            jax.ShapeDtypeStruct((2 * TOP_K, t), F32),
            jax.ShapeDtypeStruct((t, LANES), F32),
            jax.ShapeDtypeStruct((N_EXPERTS, LANES), F32),
        ],
        compiler_params=pltpu.CompilerParams(dimension_semantics=("arbitrary",)),
    )(st, bias_col)


def _swiglu(xb, wg, wu):
    g = _dot(xb, wg)
    return (g * jax.nn.sigmoid(g)) * _dot(xb, wu)


def _piece_copy(src_ref, src_row, dst_ref, dst_row, sem, n_pieces=1):
    rows = n_pieces * PIECE
    src = src_ref.at[pl.ds(pl.multiple_of(src_row * PIECE, PIECE), rows), :]
    dst = dst_ref.at[pl.ds(pl.multiple_of(dst_row * PIECE, PIECE), rows), :]
    return pltpu.make_async_copy(src, dst, sem)


LOCAL_BITS = 8
MAX_DOUBLES = LROWS // (2 * PIECE)
MAX_SINGLES = N_EXPERTS


def _for_each_copy(doubles_ref, singles_ref, counts_ref, tile, fn):
    def unpack(word):
        return word & ((1 << LOCAL_BITS) - 1), lax.shift_right_logical(word, LOCAL_BITS)

    def double(i, carry):
        fn(*unpack(doubles_ref[tile * MAX_DOUBLES + i]), 2)
        return carry

    def single(i, carry):
        fn(*unpack(singles_ref[tile * MAX_SINGLES + i]), 1)
        return carry

    lax.fori_loop(0, counts_ref[2 * tile], double, 0)
    lax.fori_loop(0, counts_ref[2 * tile + 1], single, 0)


def _wait_pieces(total, copy_of):
    for bit in range((LROWS // PIECE).bit_length()):
        @pl.when((lax.shift_right_logical(total, bit) & 1) == 1)
        def _():
            copy_of(1 << bit).wait()


def _dispatch_kernel(doubles_ref, singles_ref, counts_ref, tail_ref, ntail_ref, nv_ref,
                     h2_ref, infok_ref, xs_ref, lbuf, zbuf, sem, zsem):
    tile = pl.program_id(0)
    last = pl.num_programs(0) - 1
    slot = tile % 2
    n_blocks = xs_ref.shape[0] // RB

    def spare_blocks(fn):
        def per_block(b, carry):
            fn(b)
            return carry
        lax.fori_loop(nv_ref[0], n_blocks, per_block, 0)

    def zero_block(b):
        dst = xs_ref.at[pl.ds(pl.multiple_of(b * RB, RB), RB), :]
        return pltpu.make_async_copy(zbuf, dst, zsem)

    def tails(fn):
        def per_expert(e, carry):
            n = ntail_ref[e]
            for bit in range((RB // PIECE - 1).bit_length()):
                @pl.when((lax.shift_right_logical(n, bit) & 1) == 1)
                def _():
                    done = n & ((1 << bit) - 1)
                    fn(_piece_copy(zbuf, 0, xs_ref, tail_ref[e] + done, zsem, 1 << bit))
            return carry
        lax.fori_loop(0, N_EXPERTS, per_expert, 0)

    def start(t, s):
        _for_each_copy(
            doubles_ref, singles_ref, counts_ref, t,
            lambda lp, gp, n: _piece_copy(lbuf.at[s], lp, xs_ref, gp, sem.at[s], n).start())

    def wait(t, s):
        _wait_pieces(2 * counts_ref[2 * t] + counts_ref[2 * t + 1],
                     lambda n: _piece_copy(lbuf.at[s], 0, xs_ref, 0, sem.at[s], n))

    @pl.when(tile == 0)
    def _():
        zbuf[...] = jnp.zeros_like(zbuf)
        tails(lambda copy: copy.start())
        spare_blocks(lambda b: zero_block(b).start())

    @pl.when(tile >= 2)
    def _():
        wait(tile - 2, slot)

    xb = h2_ref[...]
    for j in range(LROWS // LCHUNK):
        r = (lax.broadcasted_iota(jnp.int32, (LCHUNK, TW), 0) + j * LCHUNK).astype(F32)
        onehot = jnp.zeros((LCHUNK, TW), F32)
        for k in range(TOP_K):
            onehot = jnp.where(r == infok_ref[k:k + 1, :], 1.0, onehot)
        lbuf[slot, j * LCHUNK:(j + 1) * LCHUNK, :] = _dot(onehot.astype(BF16), xb).astype(BF16)

    start(tile, slot)

    @pl.when(tile == last)
    def _():
        @pl.when(tile >= 1)
        def _():
            wait(tile - 1, 1 - slot)
        wait(tile, slot)
        tails(lambda copy: copy.wait())
        spare_blocks(lambda b: zero_block(b).wait())


def _dispatch(tables, h2, infok, n_rows):
    t = h2.shape[0]
    return pl.pallas_call(
        _dispatch_kernel,
        name="dispatch",
        grid_spec=pltpu.PrefetchScalarGridSpec(
            num_scalar_prefetch=6,
            grid=(t // TW,),
            in_specs=[
                pl.BlockSpec((TW, D_MODEL), lambda i, *_: (i, 0)),
                pl.BlockSpec((2 * TOP_K, TW), lambda i, *_: (0, i)),
            ],
            out_specs=pl.BlockSpec(memory_space=pl.ANY),
            scratch_shapes=[
                pltpu.VMEM((2, LROWS, D_MODEL), BF16),
                pltpu.VMEM((RB, D_MODEL), BF16),
                pltpu.SemaphoreType.DMA((2,)),
                pltpu.SemaphoreType.DMA(()),
            ],
        ),
        out_shape=jax.ShapeDtypeStruct((n_rows, D_MODEL), BF16),
        compiler_params=pltpu.CompilerParams(
            dimension_semantics=("arbitrary",), vmem_limit_bytes=VMEM_LIMIT),
    )(*tables, h2, infok)


def _experts_kernel(be_ref, nv_ref, xs_ref, wg_ref, wu_ref, wd_ref, ys_ref, wgb, wub, wdb):
    i = pl.program_id(0)

    @pl.when(i < nv_ref[0])
    def _():
        @pl.when((i == 0) | (be_ref[i] != be_ref[jnp.maximum(i - 1, 0)]))
        def _():
            wgb[...] = wg_ref[0].astype(BF16)
            wub[...] = wu_ref[0].astype(BF16)
            wdb[...] = wd_ref[0].astype(BF16)

        for j in range(RB // RSUB):
            rows = slice(j * RSUB, (j + 1) * RSUB)
            h = _swiglu(xs_ref[rows, :], wgb[...], wub[...])
            ys_ref[rows, :] = _dot(h.astype(BF16), wdb[...]).astype(BF16)


def _experts(block_expert, n_valid, xs, wg, wu, wd):
    n_blocks = xs.shape[0] // RB
    rows = pl.BlockSpec((RB, D_MODEL), lambda i, be, nv: (jnp.minimum(i, nv[0] - 1), 0))
    return pl.pallas_call(
        _experts_kernel,
        name="experts",
        grid_spec=pltpu.PrefetchScalarGridSpec(
            num_scalar_prefetch=2,
            grid=(n_blocks,),
            in_specs=[
                rows,
                pl.BlockSpec((1, D_MODEL, D_EXPERT), lambda i, be, nv: (be[i], 0, 0)),
                pl.BlockSpec((1, D_MODEL, D_EXPERT), lambda i, be, nv: (be[i], 0, 0)),
                pl.BlockSpec((1, D_EXPERT, D_MODEL), lambda i, be, nv: (be[i], 0, 0)),
            ],
            out_specs=rows,
            scratch_shapes=[
                pltpu.VMEM((D_MODEL, D_EXPERT), BF16),
                pltpu.VMEM((D_MODEL, D_EXPERT), BF16),
                pltpu.VMEM((D_EXPERT, D_MODEL), BF16),
            ],
        ),
        out_shape=jax.ShapeDtypeStruct(xs.shape, BF16),
        input_output_aliases={2: 0},
        compiler_params=pltpu.CompilerParams(
            dimension_semantics=("arbitrary",), vmem_limit_bytes=VMEM_LIMIT),
    )(block_expert, n_valid, xs, wg, wu, wd)


def _combine_kernel(doubles_ref, singles_ref, counts_ref,
                    infot_ref, h2_ref, x1_ref, gate_ref, gpost_ref, wsg_ref, wsu_ref, wsd_ref,
                    ys_ref, o_ref, ybuf, sem):
    tile = pl.program_id(0)
    last = pl.num_programs(0) - 1
    slot = tile % 2

    def start(t, s):
        _for_each_copy(
            doubles_ref, singles_ref, counts_ref, t,
            lambda lp, gp, n: _piece_copy(ys_ref, gp, ybuf.at[s], lp, sem.at[s], n).start())

    def wait(t, s):
        _wait_pieces(2 * counts_ref[2 * t] + counts_ref[2 * t + 1],
                     lambda n: _piece_copy(ys_ref, 0, ybuf.at[s], 0, sem.at[s], n))

    @pl.when(tile == 0)
    def _():
        ybuf[...] = jnp.zeros_like(ybuf)
        start(tile, slot)

    @pl.when(tile < last)
    def _():
        start(tile + 1, 1 - slot)

    wait(tile, slot)

    info = infot_ref[...]
    routed = jnp.zeros((TW, D_MODEL), F32)
    for j in range(LROWS // LCHUNK):
        r = (lax.broadcasted_iota(jnp.int32, (TW, LCHUNK), 1) + j * LCHUNK).astype(F32)
        wmat = jnp.zeros((TW, LCHUNK), F32)
        for k in range(TOP_K):
            wmat = jnp.where(r == info[:, k:k + 1], info[:, TOP_K + k:TOP_K + k + 1], wmat)
        routed = routed + _dot(wmat.astype(BF16), ybuf[slot, j * LCHUNK:(j + 1) * LCHUNK, :])

    hs = _swiglu(h2_ref[...], wsg_ref[...], wsu_ref[...])
    ff = routed + _dot(hs.astype(BF16), wsd_ref[...])
    o_ref[...] = x1_ref[...] + gate_ref[0] * _rms(ff, gpost_ref[...])


def _combine(tables, infot, h2, x1, gate, gpost, wsg, wsu, wsd, ys, seq):
    t = h2.shape[0]
    per_seq = seq // TW
    full = lambda shape: pl.BlockSpec(shape, lambda i, *_: (0,) * len(shape))
    tok = lambda width: pl.BlockSpec((TW, width), lambda i, *_: (i, 0))
    return pl.pallas_call(
        _combine_kernel,
        name="combine",
        grid_spec=pltpu.PrefetchScalarGridSpec(
            num_scalar_prefetch=3,
            grid=(t // TW,),
            in_specs=[
                tok(LANES), tok(D_MODEL), tok(D_MODEL),
                pl.BlockSpec((1, 1, D_MODEL), lambda i, *_: (i // per_seq, 0, 0)),
                full((1, D_MODEL)),
                full((D_MODEL, D_SHARED)), full((D_MODEL, D_SHARED)), full((D_SHARED, D_MODEL)),
                pl.BlockSpec(memory_space=pl.ANY),
            ],
            out_specs=tok(D_MODEL),
            scratch_shapes=[
                pltpu.VMEM((2, LROWS, D_MODEL), BF16),
                pltpu.SemaphoreType.DMA((2,)),
            ],
        ),
        out_shape=jax.ShapeDtypeStruct((t, D_MODEL), F32),
        compiler_params=pltpu.CompilerParams(
            dimension_semantics=("arbitrary",), vmem_limit_bytes=VMEM_LIMIT),
    )(*tables, infot, h2, x1, gate, gpost, wsg, wsu, wsd, ys)


def _dispatch_plan(cnt, n_tiles):
    n = cnt[:, :n_tiles].astype(jnp.int32)
    pieces = (n + PIECE - 1) // PIECE
    local = jnp.cumsum(pieces, axis=0) - pieces
    seg = jnp.sum(pieces, axis=1)
    per_block = RB // PIECE
    seg_pad = (seg + per_block - 1) // per_block * per_block
    seg_end = jnp.cumsum(seg_pad)
    seg_start = seg_end - seg_pad
    glob = seg_start[:, None] + jnp.cumsum(pieces, axis=1) - pieces
    n_blocks = (TOP_K * TW * n_tiles + N_EXPERTS * n_tiles * (PIECE - 1)
                + N_EXPERTS * (RB - PIECE) + RB - 1) // RB
    n_valid = seg_end[-1] // per_block
    blk = jnp.minimum(jnp.arange(n_blocks, dtype=jnp.int32), n_valid - 1)
    block_end = seg_end // per_block
    block_expert = jnp.sum((block_end[None, :] <= blk[:, None]).astype(jnp.int32), axis=1)
    block_expert = jnp.minimum(block_expert, N_EXPERTS - 1)
    assert LROWS // PIECE <= 1 << LOCAL_BITS and n_blocks * per_block < 1 << (31 - LOCAL_BITS)

    def copy_list(count, width):
        end = jnp.cumsum(count, axis=0).T
        slot = jnp.arange(width, dtype=jnp.int32)
        expert = jnp.sum((end[:, None, :] <= slot[None, :, None]).astype(jnp.int32), axis=-1)
        pick = expert[:, :, None] == jnp.arange(N_EXPERTS, dtype=jnp.int32)
        take = lambda a: jnp.sum(jnp.where(pick, a.T[:, None, :], 0), axis=-1)
        return slot[None, :] - take(jnp.cumsum(count, axis=0) - count), take

    doubles_n = pieces // 2
    odd = pieces % 2
    j2, take2 = copy_list(doubles_n, MAX_DOUBLES)
    doubles = (take2(local) + 2 * j2) | ((take2(glob) + 2 * j2) << LOCAL_BITS)
    _, take1 = copy_list(odd, MAX_SINGLES)
    last_piece = pieces - 1
    singles = take1(local + last_piece) | (take1(glob + last_piece) << LOCAL_BITS)
    counts = jnp.stack([jnp.sum(doubles_n, axis=0), jnp.sum(odd, axis=0)], axis=1)
    tables = (doubles.reshape(-1).astype(jnp.int32), singles.reshape(-1).astype(jnp.int32),
              counts.reshape(-1).astype(jnp.int32))
    tails = ((seg_start + seg).astype(jnp.int32), (seg_pad - seg).astype(jnp.int32))
    return tables, tails, block_expert, n_valid.reshape(1).astype(jnp.int32), n_blocks * RB


def kernel(x, c, w_ada, b_ada, g_pre_mix, g_post_mix, g_pre_ffn, g_post_ffn, w_in, b_forget,
           w_pool, pool_scale, g_pool_out, g_attn_out, w_out, w_router, router_bias,
           w_gate, w_up, w_down, ws_gate, ws_up, ws_down):
    bsz, seq, d = x.shape
    depth = w_ada.shape[0]
    for l in range(depth):
        mod = _ada(c, w_ada, b_ada[l][None, :], l)
        shift_m, scale_m, gate_m, shift_f, scale_f, gate_f = [
            m.reshape(bsz, 1, d) for m in jnp.split(mod, 6, axis=-1)]

        wi = w_in[l]
        qs = HEAD_DIM ** -0.5 * LOG2E
        w1 = jnp.concatenate(
            [wi[:, :D_POOL], wi[:, D_POOL:D_POOL + D_ATTN] * qs,
             wi[:, D_POOL + D_ATTN:D_POOL + 3 * D_ATTN]], axis=1).astype(BF16)
        pad = LANES - N_SPLIT * N_HEADS
        wf = jnp.pad(jnp.tile(wi[:, D_POOL + 3 * D_ATTN:], (1, N_SPLIT)),
                     ((0, 0), (0, pad))).astype(BF16)
        bfp = jnp.pad(jnp.tile(b_forget[l], N_SPLIT), (0, pad))[None, :]

        pool, qt, ka, vt = _premix(
            x, shift_m, scale_m, g_pre_mix[l][None, :], w1, wf, bfp, w_pool[l].astype(BF16),
            pool_scale[l][None, :], g_pool_out[l][None, :])
        ot = _attn(qt, ka, vt)

        wr = jnp.pad(w_router[l], ((0, 0), (0, LANES - N_EXPERTS)))
        wrh = wr.astype(BF16)
        wrl = (wr - wrh.astype(F32)).astype(BF16)
        x1, h2, st = _postmix(
            x, pool, ot, w_out[l].astype(BF16), g_attn_out[l][None, :], g_post_mix[l][None, :],
            gate_m, g_pre_ffn[l][None, :], shift_f, scale_f, wrh, wrl)

        t = bsz * seq
        infok, infot, cnt = _route(st, router_bias[l][:, None])
        tables, tails, block_expert, n_valid, n_rows = _dispatch_plan(cnt, t // TW)
        h2f = h2.reshape(t, d)
        xs = _dispatch(tables + tails + (n_valid,), h2f, infok, n_rows)
        ys = _experts(block_expert, n_valid, xs, w_gate[l], w_up[l], w_down[l])
        out = _combine(tables, infot, h2f, x1.reshape(t, d), gate_f, g_post_ffn[l][None, :],
                       ws_gate[l].astype(BF16), ws_up[l].astype(BF16), ws_down[l].astype(BF16),
                       ys, seq)
        x = out.reshape(bsz, seq, d)
    return x
```

```python
import functools

import numpy as np
import jax
import jax.numpy as jnp
from jax import lax
from jax.experimental import pallas as pl
from jax.experimental.pallas import tpu as pltpu

D_MODEL = 1024
D_POOL = 512
POOL_WINDOWS = (2, 4, 8, 16)
POOL_GROUP = 128
MAX_WINDOW = max(POOL_WINDOWS)
D_ATTN = 512
HEAD_DIM = 64
N_HEADS = 8
N_EXPERTS = 64
N_EXPERT_GROUPS = 8
GROUP_SIZE = N_EXPERTS // N_EXPERT_GROUPS
TOPK_GROUPS = 4
TOP_K = 8
D_EXPERT = 256
D_SHARED = 256
ROUTED_SCALE = 2.5
EPS = 1e-6

LOG2E = 1.4426950408889634
SUM_ROWS = 16
LANES = 128
N_SPLIT = 3
AUG = LANES

TS_PRE = 256
TS_POST = 256
TQ = 512
TK = 512
TW = 256
PIECE = 16
RB = 1024
RSUB = 256
LCHUNK = 512
LROWS = -(-(TOP_K * TW + N_EXPERTS * (PIECE - 1)) // LCHUNK) * LCHUNK

F32 = jnp.float32
BF16 = jnp.bfloat16
VMEM_LIMIT = 56 * 1024 * 1024


def _rms(v, g):
    return v * lax.rsqrt(jnp.mean(v * v, axis=-1, keepdims=True) + EPS) * g


def _split3(v):
    hi = v.astype(BF16)
    r1 = v - hi.astype(F32)
    mid = r1.astype(BF16)
    r2 = r1 - mid.astype(F32)
    lo = r2.astype(BF16)
    return hi, mid, lo


def _dot(a, b):
    return jnp.dot(a, b, preferred_element_type=F32)


def _ada_kernel(c_ref, w_ref, b_ref, o_ref):
    o_ref[...] = _dot(c_ref[...].astype(BF16), w_ref[0].astype(BF16)) + b_ref[...]


def _ada(c, w, b, layer):
    bsz = c.shape[0]
    n = w.shape[2]
    return pl.pallas_call(
        _ada_kernel,
        name="ada",
        grid=(n // D_MODEL,),
        in_specs=[
            pl.BlockSpec((bsz, D_MODEL), lambda j: (0, 0)),
            pl.BlockSpec((1, D_MODEL, D_MODEL), lambda j: (layer, 0, j)),
            pl.BlockSpec((1, D_MODEL), lambda j: (0, j)),
        ],
        out_specs=pl.BlockSpec((bsz, D_MODEL), lambda j: (0, j)),
        out_shape=jax.ShapeDtypeStruct((bsz, n), F32),
    )(c, w, b)


def _premix_kernel(x_ref, shift_ref, scale_ref, g_ref, w1_ref, wf_ref, bf_ref, wpool_ref,
                   pscale_ref, gpool_ref, place_ref, ones_ref, fmask_ref,
                   pool_ref, qt_ref, k_ref, vt_ref,
                   uext_ref, cum_ref):
    s = pl.program_id(1)
    ts = x_ref.shape[1]

    @pl.when(s == 0)
    def _():
        uext_ref[0:MAX_WINDOW, :] = jnp.zeros((MAX_WINDOW, D_POOL), F32)
        cum_ref[...] = jnp.zeros_like(cum_ref)

    x = x_ref[0]
    h = _rms(x, g_ref[...]) * (1.0 + scale_ref[0]) + shift_ref[0]
    hb = h.astype(BF16)
    proj = _dot(hb, w1_ref[...])
    u = proj[:, :D_POOL]
    q = proj[:, D_POOL:D_POOL + D_ATTN]
    k = proj[:, D_POOL + D_ATTN:D_POOL + 2 * D_ATTN]
    v = proj[:, D_POOL + 2 * D_ATTN:]

    uext_ref[MAX_WINDOW:, :] = u
    pos = (s * ts + lax.broadcasted_iota(jnp.int32, (ts, 1), 0) + 1).astype(F32)
    ys = []
    for g, w in enumerate(POOL_WINDOWS):
        c0 = g * POOL_GROUP
        acc = uext_ref[MAX_WINDOW:, c0:c0 + POOL_GROUP]
        for j in range(1, w):
            acc = acc + uext_ref[MAX_WINDOW - j:MAX_WINDOW - j + ts, c0:c0 + POOL_GROUP]
        pooled = acc / jnp.minimum(pos, float(w)) - u[:, c0:c0 + POOL_GROUP]
        ys.append(_dot(pooled.astype(BF16), wpool_ref[g]))
    ypool = jnp.concatenate(ys, axis=1) * pscale_ref[...]
    pool_ref[0] = _rms(ypool, gpool_ref[...]).astype(BF16)
    uext_ref[0:MAX_WINDOW, :] = uext_ref[ts:ts + MAX_WINDOW, :]

    z = _dot(hb, wf_ref[...]) + bf_ref[...]
    logf = jnp.minimum(z, 0.0) - jnp.log1p(jnp.exp(-jnp.abs(z)))
    row = lax.broadcasted_iota(jnp.int32, (ts, ts), 0)
    col = lax.broadcasted_iota(jnp.int32, (ts, ts), 1)
    tri = (col <= row).astype(BF16)
    cum = cum_ref[...]
    for piece in _split3(logf):
        cum = cum + _dot(tri, piece)
    cum_ref[...] = cum[ts - 1:ts, :]

    hi, mid, lo = [piece.astype(F32) for piece in _split3(cum * LOG2E)]
    lane = lax.broadcasted_iota(jnp.int32, (ts, LANES), 1)
    pieces = jnp.where(lane < N_HEADS, hi, jnp.where(lane < 2 * N_HEADS, mid, lo))
    aug = _dot(pieces.astype(BF16), place_ref[...]) + ones_ref[...]
    aug_q = aug[:, :N_HEADS * AUG]
    aug_k = aug[:, N_HEADS * AUG:]

    def expand(a):
        blocks = []
        for j in range(D_ATTN // LANES):
            blk = a[:, j * LANES:(j + 1) * LANES]
            blocks += [blk, blk]
        return jnp.concatenate(blocks, axis=1)

    fmask = fmask_ref[...]
    qa = expand(q) * fmask + aug_q
    ka = expand(k) * fmask + aug_k
    qt_ref[0] = qa.T.astype(BF16)
    k_ref[0] = ka.astype(BF16)
    vt_ref[0] = v.T.astype(BF16)


def _aug_constants():
    width = N_HEADS * AUG
    place = np.zeros((LANES, 2 * width), np.float32)
    ones = np.zeros((1, 2 * width), np.float32)
    fmask = np.zeros((1, width), np.float32)
    for h in range(N_HEADS):
        feat0 = h * AUG + (0 if h % 2 == 0 else HEAD_DIM)
        aug0 = h * AUG + (HEAD_DIM if h % 2 == 0 else 0)
        fmask[0, feat0:feat0 + HEAD_DIM] = 1.0
        for p in range(N_SPLIT):
            place[p * N_HEADS + h, aug0 + p] = 1.0
            ones[0, width + aug0 + p] = 1.0
            place[p * N_HEADS + h, width + aug0 + N_SPLIT + p] = -1.0
            ones[0, aug0 + N_SPLIT + p] = 1.0
    return jnp.asarray(place, BF16), jnp.asarray(ones), jnp.asarray(fmask)


def _premix(x, shift, scale, g, w1, wf, bfp, wpool, pscale, gpool):
    bsz, seq, _ = x.shape
    ts = TS_PRE
    place, ones, fmask = _aug_constants()
    full = lambda shape: pl.BlockSpec(shape, lambda b, s: (0,) * len(shape))
    per_batch = pl.BlockSpec((1, 1, D_MODEL), lambda b, s: (b, 0, 0))
    return pl.pallas_call(
        _premix_kernel,
        name="premix",
        grid=(bsz, seq // ts),
        in_specs=[
            pl.BlockSpec((1, ts, D_MODEL), lambda b, s: (b, s, 0)),
            per_batch, per_batch,
            full((1, D_MODEL)),
            full((D_MODEL, D_POOL + 3 * D_ATTN)),
            full((D_MODEL, LANES)),
            full((1, LANES)),
            full((len(POOL_WINDOWS), POOL_GROUP, POOL_GROUP)),
            full((1, D_POOL)),
            full((1, D_POOL)),
            full((LANES, 2 * N_HEADS * AUG)),
            full((1, 2 * N_HEADS * AUG)),
            full((1, N_HEADS * AUG)),
        ],
        out_specs=[
            pl.BlockSpec((1, ts, D_POOL), lambda b, s: (b, s, 0)),
            pl.BlockSpec((1, N_HEADS * AUG, ts), lambda b, s: (b, 0, s)),
            pl.BlockSpec((1, ts, N_HEADS * AUG), lambda b, s: (b, s, 0)),
            pl.BlockSpec((1, D_ATTN, ts), lambda b, s: (b, 0, s)),
        ],
        out_shape=[
            jax.ShapeDtypeStruct((bsz, seq, D_POOL), BF16),
            jax.ShapeDtypeStruct((bsz, N_HEADS * AUG, seq), BF16),
            jax.ShapeDtypeStruct((bsz, seq, N_HEADS * AUG), BF16),
            jax.ShapeDtypeStruct((bsz, D_ATTN, seq), BF16),
        ],
        scratch_shapes=[
            pltpu.VMEM((ts + MAX_WINDOW, D_POOL), F32),
            pltpu.VMEM((1, LANES), F32),
        ],
        compiler_params=pltpu.CompilerParams(
            dimension_semantics=("arbitrary", "arbitrary"), vmem_limit_bytes=VMEM_LIMIT),
    )(x, shift, scale, g, w1, wf, bfp, wpool, pscale, gpool, place, ones, fmask)


def _attn_kernel(qt_ref, k_ref, vt_ref, o_ref):
    seq = k_ref.shape[1]
    key_i = lax.broadcasted_iota(jnp.int32, (TK, TQ), 0)
    qry_i = lax.broadcasted_iota(jnp.int32, (TK, TQ), 1)
    causal = key_i <= qry_i
    one_row = lax.broadcasted_iota(jnp.int32, (SUM_ROWS, seq), 0) == 0
    vaug = jnp.concatenate([vt_ref[0], jnp.where(one_row, 1.0, 0.0).astype(BF16)], axis=0)
    for qi in range(seq // TQ):
        qt = qt_ref[0, :, qi * TQ:(qi + 1) * TQ]
        m = jnp.full((1, TQ), -jnp.inf, F32)
        acc = jnp.zeros((HEAD_DIM + SUM_ROWS, TQ), F32)
        for kj in range(qi + 1):
            kt = k_ref[0, kj * TK:(kj + 1) * TK, :]
            st = _dot(kt, qt)
            if kj == qi:
                st = jnp.where(causal, st, -jnp.inf)
            m_new = jnp.maximum(m, jnp.max(st, axis=0, keepdims=True))
            p = jnp.exp2(st - m_new).astype(BF16)
            acc = jnp.exp2(m - m_new) * acc + _dot(vaug[:, kj * TK:(kj + 1) * TK], p)
            m = m_new
        o_ref[0, :, qi * TQ:(qi + 1) * TQ] = acc[:HEAD_DIM] / acc[HEAD_DIM:HEAD_DIM + 1]


def _attn(qt, ka, vt):
    bsz, _, seq = qt.shape
    return pl.pallas_call(
        _attn_kernel,
        name="attn",
        grid=(bsz, N_HEADS),
        in_specs=[
            pl.BlockSpec((1, AUG, seq), lambda b, h: (b, h, 0)),
            pl.BlockSpec((1, seq, AUG), lambda b, h: (b, 0, h)),
            pl.BlockSpec((1, HEAD_DIM, seq), lambda b, h: (b, h, 0)),
        ],
        out_specs=pl.BlockSpec((1, HEAD_DIM, seq), lambda b, h: (b, h, 0)),
        out_shape=jax.ShapeDtypeStruct((bsz, D_ATTN, seq), F32),
        compiler_params=pltpu.CompilerParams(
            dimension_semantics=("arbitrary", "arbitrary"), vmem_limit_bytes=VMEM_LIMIT),
    )(qt, ka, vt)


def _postmix_kernel(x_ref, pool_ref, ot_ref, wout_ref, gattn_ref, gpost_ref, gate_ref,
                    gffn_ref, shift_ref, scale_ref, wr_ref,
                    x1_ref, h2_ref, st_ref):
    ya = ot_ref[0].T
    ya = _rms(ya, gattn_ref[...]).astype(BF16)
    mixed = _dot(pool_ref[0], wout_ref[:D_POOL, :]) + _dot(ya, wout_ref[D_POOL:, :])
    x1 = x_ref[0] + gate_ref[0] * _rms(mixed, gpost_ref[...])
    x1_ref[0] = x1
    h2 = _rms(x1, gffn_ref[...]) * (1.0 + scale_ref[0]) + shift_ref[0]
    h2b = h2.astype(BF16)
    h2_ref[0] = h2b
    h2l = (h2 - h2b.astype(F32)).astype(BF16)
    both = _dot(h2b, wr_ref[...])
    logits = both[:, :LANES] + both[:, LANES:] + _dot(h2l, wr_ref[:, :LANES])
    st_ref[...] = jax.nn.sigmoid(logits).T


def _postmix(x, pool, ot, wout, gattn, gpost, gate, gffn, shift, scale, wr):
    bsz, seq, _ = x.shape
    ts = TS_POST
    ns = seq // ts
    full = lambda shape: pl.BlockSpec(shape, lambda b, s: (0,) * len(shape))
    per_batch = pl.BlockSpec((1, 1, D_MODEL), lambda b, s: (b, 0, 0))
    tok = lambda width: pl.BlockSpec((1, ts, width), lambda b, s: (b, s, 0))
    return pl.pallas_call(
        _postmix_kernel,
        name="postmix",
        grid=(bsz, ns),
        in_specs=[
            tok(D_MODEL), tok(D_POOL),
            pl.BlockSpec((1, D_ATTN, ts), lambda b, s: (b, 0, s)),
            full((D_MODEL, D_MODEL)),
            full((1, D_ATTN)), full((1, D_MODEL)), per_batch,
            full((1, D_MODEL)), per_batch, per_batch,
            full((D_MODEL, 2 * LANES)),
        ],
        out_specs=[
            tok(D_MODEL), tok(D_MODEL),
            pl.BlockSpec((LANES, ts), lambda b, s: (0, b * ns + s)),
        ],
        out_shape=[
            jax.ShapeDtypeStruct((bsz, seq, D_MODEL), F32),
            jax.ShapeDtypeStruct((bsz, seq, D_MODEL), BF16),
            jax.ShapeDtypeStruct((LANES, bsz * seq), F32),
        ],
        compiler_params=pltpu.CompilerParams(
            dimension_semantics=("arbitrary", "arbitrary"), vmem_limit_bytes=VMEM_LIMIT),
    )(x, pool, ot, wout, gattn, gpost, gate, gffn, shift, scale, wr)


def _route_kernel(st_ref, bias_ref, infok_ref, infot_ref, cnt_ref):
    tile = pl.program_id(0)
    tr = st_ref.shape[1]
    scores = [st_ref[g * GROUP_SIZE:(g + 1) * GROUP_SIZE, :] for g in range(N_EXPERT_GROUPS)]
    sel = [scores[g] + bias_ref[g * GROUP_SIZE:(g + 1) * GROUP_SIZE, :]
           for g in range(N_EXPERT_GROUPS)]
    sub = lax.broadcasted_iota(jnp.int32, (GROUP_SIZE, tr), 0)
    neg = jnp.float32(-jnp.inf)

    rows = []
    for g in range(N_EXPERT_GROUPS):
        v = sel[g]
        m1 = jnp.max(v, axis=0, keepdims=True)
        first = jnp.min(jnp.where(v == m1, sub, GROUP_SIZE), axis=0, keepdims=True)
        m2 = jnp.max(jnp.where(sub == first, neg, v), axis=0, keepdims=True)
        rows.append(m1 + m2)
    gs = jnp.concatenate(rows, axis=0)

    shape = (GROUP_SIZE, tr)
    one = jnp.ones(shape, jnp.int32)
    zero = jnp.zeros(shape, jnp.int32)
    beaten = zero
    for gp in range(N_EXPERT_GROUPS):
        r = jnp.broadcast_to(gs[gp:gp + 1, :], shape)
        tie = jnp.where(sub > gp, one, zero)
        beaten = beaten + jnp.where(r > gs, one, zero) + jnp.where(r == gs, tie, zero)
    keep = jnp.where(beaten < TOPK_GROUPS, 1.0, 0.0)

    masked = [jnp.where(jnp.broadcast_to(keep[g:g + 1, :], shape) > 0.5, sel[g], neg)
              for g in range(N_EXPERT_GROUPS)]

    eidx = [sub + g * GROUP_SIZE for g in range(N_EXPERT_GROUPS)]
    picked = [zero for _ in range(N_EXPERT_GROUPS)]
    for _ in range(TOP_K):
        best = masked[0]
        for g in range(1, N_EXPERT_GROUPS):
            best = jnp.maximum(best, masked[g])
        best = jnp.broadcast_to(jnp.max(best, axis=0, keepdims=True), shape)
        first = jnp.where(masked[0] == best, eidx[0], N_EXPERTS)
        for g in range(1, N_EXPERT_GROUPS):
            first = jnp.minimum(first, jnp.where(masked[g] == best, eidx[g], N_EXPERTS))
        first = jnp.broadcast_to(jnp.min(first, axis=0, keepdims=True), shape)
        for g in range(N_EXPERT_GROUPS):
            hit = eidx[g] == first
            picked[g] = jnp.where(hit, one, picked[g])
            masked[g] = jnp.where(hit, neg, masked[g])
    chosen = [picked[g] > 0 for g in range(N_EXPERT_GROUPS)]

    w = [jnp.where(chosen[g], scores[g], 0.0) for g in range(N_EXPERT_GROUPS)]
    denom = w[0].sum(axis=0, keepdims=True)
    for g in range(1, N_EXPERT_GROUPS):
        denom = denom + w[g].sum(axis=0, keepdims=True)
    wfull = jnp.concatenate([w[g] / denom * ROUTED_SCALE for g in range(N_EXPERT_GROUPS)], axis=0)
    mfull = jnp.concatenate([jnp.where(chosen[g], 1.0, 0.0) for g in range(N_EXPERT_GROUPS)],
                            axis=0)
    mb = mfull.astype(BF16)

    e_r = lax.broadcasted_iota(jnp.int32, (N_EXPERTS, N_EXPERTS), 0)
    e_c = lax.broadcasted_iota(jnp.int32, (N_EXPERTS, N_EXPERTS), 1)
    before_e = (e_c < e_r).astype(BF16)
    t_r = lax.broadcasted_iota(jnp.int32, (tr, tr), 0)
    t_c = lax.broadcasted_iota(jnp.int32, (tr, tr), 1)
    before_t = (t_r < t_c).astype(BF16)
    ordinal = _dot(before_e, mb)
    rank = _dot(mb, before_t)
    n = jnp.sum(mfull, axis=1, keepdims=True)
    pieces = jnp.floor((n + (PIECE - 1.0)) * (1.0 / PIECE))
    run_start = PIECE * _dot(before_e, jnp.broadcast_to(pieces, (N_EXPERTS, LANES)).astype(BF16))
    pos = run_start[:, 0:1] + rank

    sub8 = lax.broadcasted_iota(jnp.int32, (TOP_K, tr), 0)
    pos8 = jnp.zeros((TOP_K, tr), F32)
    w8 = jnp.zeros((TOP_K, tr), F32)
    for k in range(TOP_K):
        selk = jnp.where(ordinal == float(k), mfull, 0.0)
        pk = jnp.sum(selk * pos, axis=0, keepdims=True)
        wk = jnp.sum(selk * wfull, axis=0, keepdims=True)
        pos8 = jnp.where(sub8 == k, jnp.broadcast_to(pk, (TOP_K, tr)), pos8)
        w8 = jnp.where(sub8 == k, jnp.broadcast_to(wk, (TOP_K, tr)), w8)
    infok_ref[...] = jnp.concatenate([pos8, w8], axis=0)
    info = jnp.concatenate([pos8, w8, jnp.zeros((LANES - 2 * TOP_K, tr), F32)], axis=0)
    infot_ref[...] = info.T

    @pl.when(tile == 0)
    def _():
        cnt_ref[...] = jnp.zeros_like(cnt_ref)

    lane = lax.broadcasted_iota(jnp.int32, cnt_ref.shape, 1)
    cnt_ref[...] = jnp.where(lane == tile, jnp.broadcast_to(n, cnt_ref.shape), cnt_ref[...])


def _route(st, bias_col):
    t = st.shape[1]
    assert t // TW <= LANES
    return pl.pallas_call(
        _route_kernel,
        name="route",
        grid=(t // TW,),
        in_specs=[
            pl.BlockSpec((LANES, TW), lambda i: (0, i)),
            pl.BlockSpec((N_EXPERTS, 1), lambda i: (0, 0)),
        ],
        out_specs=[
            pl.BlockSpec((2 * TOP_K, TW), lambda i: (0, i)),
            pl.BlockSpec((TW, LANES), lambda i: (i, 0)),
            pl.BlockSpec((N_EXPERTS, LANES), lambda i: (0, 0)),
        ],
        out_shape=[
            jax.ShapeDtypeStruct((2 * TOP_K, t), F32),
            jax.ShapeDtypeStruct((t, LANES), F32),
            jax.ShapeDtypeStruct((N_EXPERTS, LANES), F32),
        ],
        compiler_params=pltpu.CompilerParams(dimension_semantics=("arbitrary",)),
    )(st, bias_col)


def _swiglu(xb, wg, wu):
    g = _dot(xb, wg)
    return (g * jax.nn.sigmoid(g)) * _dot(xb, wu)


def _piece_copy(src_ref, src_row, dst_ref, dst_row, sem, n_pieces=1):
    rows = n_pieces * PIECE
    src = src_ref.at[pl.ds(pl.multiple_of(src_row * PIECE, PIECE), rows), :]
    dst = dst_ref.at[pl.ds(pl.multiple_of(dst_row * PIECE, PIECE), rows), :]
    return pltpu.make_async_copy(src, dst, sem)


LOCAL_BITS = 8
MAX_DOUBLES = LROWS // (2 * PIECE)
MAX_SINGLES = N_EXPERTS


def _for_each_copy(doubles_ref, singles_ref, counts_ref, tile, fn):
    def unpack(word):
        return word & ((1 << LOCAL_BITS) - 1), lax.shift_right_logical(word, LOCAL_BITS)

    def double(i, carry):
        fn(*unpack(doubles_ref[tile * MAX_DOUBLES + i]), 2)
        return carry

    def single(i, carry):
        fn(*unpack(singles_ref[tile * MAX_SINGLES + i]), 1)
        return carry

    lax.fori_loop(0, counts_ref[2 * tile], double, 0)
    lax.fori_loop(0, counts_ref[2 * tile + 1], single, 0)


def _wait_pieces(total, copy_of):
    for bit in range((LROWS // PIECE).bit_length()):
        @pl.when((lax.shift_right_logical(total, bit) & 1) == 1)
        def _():
            copy_of(1 << bit).wait()


def _dispatch_kernel(doubles_ref, singles_ref, counts_ref, tail_ref, ntail_ref, nv_ref,
                     h2_ref, infok_ref, xs_ref, lbuf, zbuf, sem, zsem):
    tile = pl.program_id(0)
    last = pl.num_programs(0) - 1
    slot = tile % 2
    n_blocks = xs_ref.shape[0] // RB

    def spare_blocks(fn):
        def per_block(b, carry):
            fn(b)
            return carry
        lax.fori_loop(nv_ref[0], n_blocks, per_block, 0)

    def zero_block(b):
        dst = xs_ref.at[pl.ds(pl.multiple_of(b * RB, RB), RB), :]
        return pltpu.make_async_copy(zbuf, dst, zsem)

    def tails(fn):
        def per_expert(e, carry):
            n = ntail_ref[e]
            for bit in range((RB // PIECE - 1).bit_length()):
                @pl.when((lax.shift_right_logical(n, bit) & 1) == 1)
                def _():
                    done = n & ((1 << bit) - 1)
                    fn(_piece_copy(zbuf, 0, xs_ref, tail_ref[e] + done, zsem, 1 << bit))
            return carry
        lax.fori_loop(0, N_EXPERTS, per_expert, 0)

    def start(t, s):
        _for_each_copy(
            doubles_ref, singles_ref, counts_ref, t,
            lambda lp, gp, n: _piece_copy(lbuf.at[s], lp, xs_ref, gp, sem.at[s], n).start())

    def wait(t, s):
        _wait_pieces(2 * counts_ref[2 * t] + counts_ref[2 * t + 1],
                     lambda n: _piece_copy(lbuf.at[s], 0, xs_ref, 0, sem.at[s], n))

    @pl.when(tile == 0)
    def _():
        zbuf[...] = jnp.zeros_like(zbuf)
        tails(lambda copy: copy.start())
        spare_blocks(lambda b: zero_block(b).start())

    @pl.when(tile >= 2)
    def _():
        wait(tile - 2, slot)

    xb = h2_ref[...]
    used_rows = PIECE * (2 * counts_ref[2 * tile] + counts_ref[2 * tile + 1])

    def sort_chunk(j):
        r = (lax.broadcasted_iota(jnp.int32, (LCHUNK, TW), 0) + j * LCHUNK).astype(F32)
        onehot = jnp.zeros((LCHUNK, TW), F32)
        for k in range(TOP_K):
            onehot = jnp.where(r == infok_ref[k:k + 1, :], 1.0, onehot)
        lbuf[slot, j * LCHUNK:(j + 1) * LCHUNK, :] = _dot(onehot.astype(BF16), xb).astype(BF16)

    for j in range(LROWS // LCHUNK):
        if (j + 1) * LCHUNK <= TOP_K * TW:
            sort_chunk(j)
        else:
            pl.when(used_rows > j * LCHUNK)(functools.partial(sort_chunk, j))

    start(tile, slot)

    @pl.when(tile == last)
    def _():
        @pl.when(tile >= 1)
        def _():
            wait(tile - 1, 1 - slot)
        wait(tile, slot)
        tails(lambda copy: copy.wait())
        spare_blocks(lambda b: zero_block(b).wait())


def _dispatch(tables, h2, infok, n_rows):
    t = h2.shape[0]
    return pl.pallas_call(
        _dispatch_kernel,
        name="dispatch",
        grid_spec=pltpu.PrefetchScalarGridSpec(
            num_scalar_prefetch=6,
            grid=(t // TW,),
            in_specs=[
                pl.BlockSpec((TW, D_MODEL), lambda i, *_: (i, 0)),
                pl.BlockSpec((2 * TOP_K, TW), lambda i, *_: (0, i)),
            ],
            out_specs=pl.BlockSpec(memory_space=pl.ANY),
            scratch_shapes=[
                pltpu.VMEM((2, LROWS, D_MODEL), BF16),
                pltpu.VMEM((RB, D_MODEL), BF16),
                pltpu.SemaphoreType.DMA((2,)),
                pltpu.SemaphoreType.DMA(()),
            ],
        ),
        out_shape=jax.ShapeDtypeStruct((n_rows, D_MODEL), BF16),
        compiler_params=pltpu.CompilerParams(
            dimension_semantics=("arbitrary",), vmem_limit_bytes=VMEM_LIMIT),
    )(*tables, h2, infok)


def _experts_kernel(be_ref, nv_ref, xs_ref, wg_ref, wu_ref, wd_ref, ys_ref, wgb, wub, wdb):
    i = pl.program_id(0)

    @pl.when(i < nv_ref[0])
    def _():
        @pl.when((i == 0) | (be_ref[i] != be_ref[jnp.maximum(i - 1, 0)]))
        def _():
            wgb[...] = wg_ref[0].astype(BF16)
            wub[...] = wu_ref[0].astype(BF16)
            wdb[...] = wd_ref[0].astype(BF16)

        for j in range(RB // RSUB):
            rows = slice(j * RSUB, (j + 1) * RSUB)
            h = _swiglu(xs_ref[rows, :], wgb[...], wub[...])
            ys_ref[rows, :] = _dot(h.astype(BF16), wdb[...]).astype(BF16)


def _experts(block_expert, n_valid, xs, wg, wu, wd):
    n_blocks = xs.shape[0] // RB
    rows = pl.BlockSpec((RB, D_MODEL), lambda i, be, nv: (jnp.minimum(i, nv[0] - 1), 0))
    return pl.pallas_call(
        _experts_kernel,
        name="experts",
        grid_spec=pltpu.PrefetchScalarGridSpec(
            num_scalar_prefetch=2,
            grid=(n_blocks,),
            in_specs=[
                rows,
                pl.BlockSpec((1, D_MODEL, D_EXPERT), lambda i, be, nv: (be[i], 0, 0)),
                pl.BlockSpec((1, D_MODEL, D_EXPERT), lambda i, be, nv: (be[i], 0, 0)),
                pl.BlockSpec((1, D_EXPERT, D_MODEL), lambda i, be, nv: (be[i], 0, 0)),
            ],
            out_specs=rows,
            scratch_shapes=[
                pltpu.VMEM((D_MODEL, D_EXPERT), BF16),
                pltpu.VMEM((D_MODEL, D_EXPERT), BF16),
                pltpu.VMEM((D_EXPERT, D_MODEL), BF16),
            ],
        ),
        out_shape=jax.ShapeDtypeStruct(xs.shape, BF16),
        input_output_aliases={2: 0},
        compiler_params=pltpu.CompilerParams(
            dimension_semantics=("arbitrary",), vmem_limit_bytes=VMEM_LIMIT),
    )(block_expert, n_valid, xs, wg, wu, wd)


def _combine_kernel(doubles_ref, singles_ref, counts_ref,
                    infot_ref, h2_ref, x1_ref, gate_ref, gpost_ref, wsg_ref, wsu_ref, wsd_ref,
                    ys_ref, o_ref, ybuf, acc_ref, sem):
    tile = pl.program_id(0)
    last = pl.num_programs(0) - 1
    slot = tile % 2

    def start(t, s):
        _for_each_copy(
            doubles_ref, singles_ref, counts_ref, t,
            lambda lp, gp, n: _piece_copy(ys_ref, gp, ybuf.at[s], lp, sem.at[s], n).start())

    def wait(t, s):
        _wait_pieces(2 * counts_ref[2 * t] + counts_ref[2 * t + 1],
                     lambda n: _piece_copy(ys_ref, 0, ybuf.at[s], 0, sem.at[s], n))

    @pl.when(tile == 0)
    def _():
        ybuf[...] = jnp.zeros_like(ybuf)
        start(tile, slot)

    @pl.when(tile < last)
    def _():
        start(tile + 1, 1 - slot)

    wait(tile, slot)

    info = infot_ref[...]
    used_rows = PIECE * (2 * counts_ref[2 * tile] + counts_ref[2 * tile + 1])

    def sum_chunk(j):
        r = (lax.broadcasted_iota(jnp.int32, (TW, LCHUNK), 1) + j * LCHUNK).astype(F32)
        wmat = jnp.zeros((TW, LCHUNK), F32)
        for k in range(TOP_K):
            wmat = jnp.where(r == info[:, k:k + 1], info[:, TOP_K + k:TOP_K + k + 1], wmat)
        return _dot(wmat.astype(BF16), ybuf[slot, j * LCHUNK:(j + 1) * LCHUNK, :])

    def add_chunk(j):
        acc_ref[...] += sum_chunk(j)

    hs = _swiglu(h2_ref[...], wsg_ref[...], wsu_ref[...])
    ff = _dot(hs.astype(BF16), wsd_ref[...])
    always = TOP_K * TW // LCHUNK
    for j in range(always):
        ff = ff + sum_chunk(j)
    acc_ref[...] = ff
    for j in range(always, LROWS // LCHUNK):
        pl.when(used_rows > j * LCHUNK)(functools.partial(add_chunk, j))
    ff = acc_ref[...]
    o_ref[...] = x1_ref[...] + gate_ref[0] * _rms(ff, gpost_ref[...])


def _combine(tables, infot, h2, x1, gate, gpost, wsg, wsu, wsd, ys, seq):
    t = h2.shape[0]
    per_seq = seq // TW
    full = lambda shape: pl.BlockSpec(shape, lambda i, *_: (0,) * len(shape))
    tok = lambda width: pl.BlockSpec((TW, width), lambda i, *_: (i, 0))
    return pl.pallas_call(
        _combine_kernel,
        name="combine",
        grid_spec=pltpu.PrefetchScalarGridSpec(
            num_scalar_prefetch=3,
            grid=(t // TW,),
            in_specs=[
                tok(LANES), tok(D_MODEL), tok(D_MODEL),
                pl.BlockSpec((1, 1, D_MODEL), lambda i, *_: (i // per_seq, 0, 0)),
                full((1, D_MODEL)),
                full((D_MODEL, D_SHARED)), full((D_MODEL, D_SHARED)), full((D_SHARED, D_MODEL)),
                pl.BlockSpec(memory_space=pl.ANY),
            ],
            out_specs=tok(D_MODEL),
            scratch_shapes=[
                pltpu.VMEM((2, LROWS, D_MODEL), BF16),
                pltpu.VMEM((TW, D_MODEL), F32),
                pltpu.SemaphoreType.DMA((2,)),
            ],
        ),
        out_shape=jax.ShapeDtypeStruct((t, D_MODEL), F32),
        compiler_params=pltpu.CompilerParams(
            dimension_semantics=("arbitrary",), vmem_limit_bytes=VMEM_LIMIT),
    )(*tables, infot, h2, x1, gate, gpost, wsg, wsu, wsd, ys)


def _dispatch_plan(cnt, n_tiles):
    n = cnt[:, :n_tiles].astype(jnp.int32)
    pieces = (n + PIECE - 1) // PIECE
    local = jnp.cumsum(pieces, axis=0) - pieces
    seg = jnp.sum(pieces, axis=1)
    per_block = RB // PIECE
    seg_pad = (seg + per_block - 1) // per_block * per_block
    seg_end = jnp.cumsum(seg_pad)
    seg_start = seg_end - seg_pad
    glob = seg_start[:, None] + jnp.cumsum(pieces, axis=1) - pieces
    n_blocks = (TOP_K * TW * n_tiles + N_EXPERTS * n_tiles * (PIECE - 1)
                + N_EXPERTS * (RB - PIECE) + RB - 1) // RB
    n_valid = seg_end[-1] // per_block
    blk = jnp.minimum(jnp.arange(n_blocks, dtype=jnp.int32), n_valid - 1)
    block_end = seg_end // per_block
    block_expert = jnp.sum((block_end[None, :] <= blk[:, None]).astype(jnp.int32), axis=1)
    block_expert = jnp.minimum(block_expert, N_EXPERTS - 1)
    assert LROWS // PIECE <= 1 << LOCAL_BITS and n_blocks * per_block < 1 << (31 - LOCAL_BITS)

    def copy_list(count, width):
        end = jnp.cumsum(count, axis=0).T
        slot = jnp.arange(width, dtype=jnp.int32)
        expert = jnp.sum((end[:, None, :] <= slot[None, :, None]).astype(jnp.int32), axis=-1)
        pick = expert[:, :, None] == jnp.arange(N_EXPERTS, dtype=jnp.int32)
        take = lambda a: jnp.sum(jnp.where(pick, a.T[:, None, :], 0), axis=-1)
        return slot[None, :] - take(jnp.cumsum(count, axis=0) - count), take

    doubles_n = pieces // 2
    odd = pieces % 2
    j2, take2 = copy_list(doubles_n, MAX_DOUBLES)
    doubles = (take2(local) + 2 * j2) | ((take2(glob) + 2 * j2) << LOCAL_BITS)
    _, take1 = copy_list(odd, MAX_SINGLES)
    last_piece = pieces - 1
    singles = take1(local + last_piece) | (take1(glob + last_piece) << LOCAL_BITS)
    counts = jnp.stack([jnp.sum(doubles_n, axis=0), jnp.sum(odd, axis=0)], axis=1)
    tables = (doubles.reshape(-1).astype(jnp.int32), singles.reshape(-1).astype(jnp.int32),
              counts.reshape(-1).astype(jnp.int32))
    tails = ((seg_start + seg).astype(jnp.int32), (seg_pad - seg).astype(jnp.int32))
    return tables, tails, block_expert, n_valid.reshape(1).astype(jnp.int32), n_blocks * RB


def kernel(x, c, w_ada, b_ada, g_pre_mix, g_post_mix, g_pre_ffn, g_post_ffn, w_in, b_forget,
           w_pool, pool_scale, g_pool_out, g_attn_out, w_out, w_router, router_bias,
           w_gate, w_up, w_down, ws_gate, ws_up, ws_down):
    bsz, seq, d = x.shape
    depth = w_ada.shape[0]
    for l in range(depth):
        mod = _ada(c, w_ada, b_ada[l][None, :], l)
        shift_m, scale_m, gate_m, shift_f, scale_f, gate_f = [
            m.reshape(bsz, 1, d) for m in jnp.split(mod, 6, axis=-1)]

        wi = w_in[l]
        qs = HEAD_DIM ** -0.5 * LOG2E
        w1 = jnp.concatenate(
            [wi[:, :D_POOL], wi[:, D_POOL:D_POOL + D_ATTN] * qs,
             wi[:, D_POOL + D_ATTN:D_POOL + 3 * D_ATTN]], axis=1).astype(BF16)
        pad = LANES - N_SPLIT * N_HEADS
        wf = jnp.pad(jnp.tile(wi[:, D_POOL + 3 * D_ATTN:], (1, N_SPLIT)),
                     ((0, 0), (0, pad))).astype(BF16)
        bfp = jnp.pad(jnp.tile(b_forget[l], N_SPLIT), (0, pad))[None, :]

        pool, qt, ka, vt = _premix(
            x, shift_m, scale_m, g_pre_mix[l][None, :], w1, wf, bfp, w_pool[l].astype(BF16),
            pool_scale[l][None, :], g_pool_out[l][None, :])
        ot = _attn(qt, ka, vt)

        wr = jnp.pad(w_router[l], ((0, 0), (0, LANES - N_EXPERTS)))
        wrh = wr.astype(BF16)
        wrl = (wr - wrh.astype(F32)).astype(BF16)
        x1, h2, st = _postmix(
            x, pool, ot, w_out[l].astype(BF16), g_attn_out[l][None, :], g_post_mix[l][None, :],
            gate_m, g_pre_ffn[l][None, :], shift_f, scale_f,
            jnp.concatenate([wrh, wrl], axis=1))

        t = bsz * seq
        infok, infot, cnt = _route(st, router_bias[l][:, None])
        tables, tails, block_expert, n_valid, n_rows = _dispatch_plan(cnt, t // TW)
        h2f = h2.reshape(t, d)
        xs = _dispatch(tables + tails + (n_valid,), h2f, infok, n_rows)
        ys = _experts(block_expert, n_valid, xs, w_gate[l], w_up[l], w_down[l])
        out = _combine(tables, infot, h2f, x1.reshape(t, d), gate_f, g_post_ffn[l][None, :],
                       ws_gate[l].astype(BF16), ws_up[l].astype(BF16), ws_down[l].astype(BF16),
                       ys, seq)
        x = out.reshape(bsz, seq, d)
    return x
```

```python
import functools

import numpy as np
import jax
import jax.numpy as jnp
from jax import lax
from jax.experimental import pallas as pl
from jax.experimental.pallas import tpu as pltpu

D_MODEL = 1024
D_POOL = 512
POOL_WINDOWS = (2, 4, 8, 16)
POOL_GROUP = 128
MAX_WINDOW = max(POOL_WINDOWS)
D_ATTN = 512
HEAD_DIM = 64
N_HEADS = 8
N_EXPERTS = 64
N_EXPERT_GROUPS = 8
GROUP_SIZE = N_EXPERTS // N_EXPERT_GROUPS
TOPK_GROUPS = 4
TOP_K = 8
D_EXPERT = 256
D_SHARED = 256
ROUTED_SCALE = 2.5
EPS = 1e-6

LOG2E = 1.4426950408889634
LANES = 128
N_SPLIT = 3
AUG = LANES

TS_PRE = 256
TS_POST = 256
TQ = 512
TK = 512
HEAD_PAIR = 128 // HEAD_DIM
TW = 256
PIECE = 16
RB = 1024
RSUB = 256
LCHUNK = 512
LROWS = -(-(TOP_K * TW + N_EXPERTS * (PIECE - 1)) // LCHUNK) * LCHUNK

F32 = jnp.float32
BF16 = jnp.bfloat16
VMEM_LIMIT = 56 * 1024 * 1024


def _rms(v, g):
    return v * lax.rsqrt(jnp.mean(v * v, axis=-1, keepdims=True) + EPS) * g


def _split3(v):
    hi = v.astype(BF16)
    r1 = v - hi.astype(F32)
    mid = r1.astype(BF16)
    r2 = r1 - mid.astype(F32)
    lo = r2.astype(BF16)
    return hi, mid, lo


def _dot(a, b):
    return jnp.dot(a, b, preferred_element_type=F32)


def _ada_kernel(c_ref, w_ref, b_ref, o_ref):
    o_ref[...] = _dot(c_ref[...].astype(BF16), w_ref[0].astype(BF16)) + b_ref[...]


def _ada(c, w, b, layer):
    bsz = c.shape[0]
    n = w.shape[2]
    return pl.pallas_call(
        _ada_kernel,
        name="ada",
        grid=(n // D_MODEL,),
        in_specs=[
            pl.BlockSpec((bsz, D_MODEL), lambda j: (0, 0)),
            pl.BlockSpec((1, D_MODEL, D_MODEL), lambda j: (layer, 0, j)),
            pl.BlockSpec((1, D_MODEL), lambda j: (0, j)),
        ],
        out_specs=pl.BlockSpec((bsz, D_MODEL), lambda j: (0, j)),
        out_shape=jax.ShapeDtypeStruct((bsz, n), F32),
    )(c, w, b)


def _premix_kernel(x_ref, shift_ref, scale_ref, g_ref, w1_ref, wf_ref, bf_ref, wpool_ref,
                   pscale_ref, gpool_ref, place_ref, ones_ref, fmask_ref,
                   pool_ref, q_ref, kt_ref, v_ref,
                   uext_ref, cum_ref):
    s = pl.program_id(1)
    ts = x_ref.shape[1]

    @pl.when(s == 0)
    def _():
        uext_ref[0:MAX_WINDOW, :] = jnp.zeros((MAX_WINDOW, D_POOL), F32)
        cum_ref[...] = jnp.zeros_like(cum_ref)

    x = x_ref[0]
    h = _rms(x, g_ref[...]) * (1.0 + scale_ref[0]) + shift_ref[0]
    hb = h.astype(BF16)
    proj = _dot(hb, w1_ref[...])
    u = proj[:, :D_POOL]
    q = proj[:, D_POOL:D_POOL + D_ATTN]
    k = proj[:, D_POOL + D_ATTN:D_POOL + 2 * D_ATTN]
    v = proj[:, D_POOL + 2 * D_ATTN:]

    uext_ref[MAX_WINDOW:, :] = u
    pos = (s * ts + lax.broadcasted_iota(jnp.int32, (ts, 1), 0) + 1).astype(F32)
    ys = []
    for g, w in enumerate(POOL_WINDOWS):
        c0 = g * POOL_GROUP
        acc = uext_ref[MAX_WINDOW:, c0:c0 + POOL_GROUP]
        for j in range(1, w):
            acc = acc + uext_ref[MAX_WINDOW - j:MAX_WINDOW - j + ts, c0:c0 + POOL_GROUP]
        pooled = acc / jnp.minimum(pos, float(w)) - u[:, c0:c0 + POOL_GROUP]
        ys.append(_dot(pooled.astype(BF16), wpool_ref[g]))
    ypool = jnp.concatenate(ys, axis=1) * pscale_ref[...]
    pool_ref[0] = _rms(ypool, gpool_ref[...]).astype(BF16)
    uext_ref[0:MAX_WINDOW, :] = uext_ref[ts:ts + MAX_WINDOW, :]

    z = _dot(hb, wf_ref[...]) + bf_ref[...]
    logf = jnp.minimum(z, 0.0) - jnp.log1p(jnp.exp(-jnp.abs(z)))
    row = lax.broadcasted_iota(jnp.int32, (ts, ts), 0)
    col = lax.broadcasted_iota(jnp.int32, (ts, ts), 1)
    tri = (col <= row).astype(BF16)
    cum = cum_ref[...]
    for piece in _split3(logf):
        cum = cum + _dot(tri, piece)
    cum_ref[...] = cum[ts - 1:ts, :]

    hi, mid, lo = [piece.astype(F32) for piece in _split3(cum * LOG2E)]
    lane = lax.broadcasted_iota(jnp.int32, (ts, LANES), 1)
    pieces = jnp.where(lane < N_HEADS, hi, jnp.where(lane < 2 * N_HEADS, mid, lo))
    aug = _dot(pieces.astype(BF16), place_ref[...]) + ones_ref[...]
    aug_q = aug[:, :N_HEADS * AUG]
    aug_k = aug[:, N_HEADS * AUG:]

    def expand(a):
        blocks = []
        for j in range(D_ATTN // LANES):
            blk = a[:, j * LANES:(j + 1) * LANES]
            blocks += [blk, blk]
        return jnp.concatenate(blocks, axis=1)

    fmask = fmask_ref[...]
    qa = expand(q) * fmask + aug_q
    ka = expand(k) * fmask + aug_k
    q_ref[0] = qa.astype(BF16)
    kt_ref[0] = ka.T.astype(BF16)
    v_ref[0] = v.astype(BF16)


def _aug_constants():
    width = N_HEADS * AUG
    place = np.zeros((LANES, 2 * width), np.float32)
    ones = np.zeros((1, 2 * width), np.float32)
    fmask = np.zeros((1, width), np.float32)
    for h in range(N_HEADS):
        feat0 = h * AUG + (0 if h % 2 == 0 else HEAD_DIM)
        aug0 = h * AUG + (HEAD_DIM if h % 2 == 0 else 0)
        fmask[0, feat0:feat0 + HEAD_DIM] = 1.0
        for p in range(N_SPLIT):
            place[p * N_HEADS + h, aug0 + p] = 1.0
            ones[0, width + aug0 + p] = 1.0
            place[p * N_HEADS + h, width + aug0 + N_SPLIT + p] = -1.0
            ones[0, aug0 + N_SPLIT + p] = 1.0
    return jnp.asarray(place, BF16), jnp.asarray(ones), jnp.asarray(fmask)


def _premix(x, shift, scale, g, w1, wf, bfp, wpool, pscale, gpool):
    bsz, seq, _ = x.shape
    ts = TS_PRE
    place, ones, fmask = _aug_constants()
    full = lambda shape: pl.BlockSpec(shape, lambda b, s: (0,) * len(shape))
    per_batch = pl.BlockSpec((1, 1, D_MODEL), lambda b, s: (b, 0, 0))
    return pl.pallas_call(
        _premix_kernel,
        name="premix",
        grid=(bsz, seq // ts),
        in_specs=[
            pl.BlockSpec((1, ts, D_MODEL), lambda b, s: (b, s, 0)),
            per_batch, per_batch,
            full((1, D_MODEL)),
            full((D_MODEL, D_POOL + 3 * D_ATTN)),
            full((D_MODEL, LANES)),
            full((1, LANES)),
            full((len(POOL_WINDOWS), POOL_GROUP, POOL_GROUP)),
            full((1, D_POOL)),
            full((1, D_POOL)),
            full((LANES, 2 * N_HEADS * AUG)),
            full((1, 2 * N_HEADS * AUG)),
            full((1, N_HEADS * AUG)),
        ],
        out_specs=[
            pl.BlockSpec((1, ts, D_POOL), lambda b, s: (b, s, 0)),
            pl.BlockSpec((1, ts, N_HEADS * AUG), lambda b, s: (b, s, 0)),
            pl.BlockSpec((1, N_HEADS * AUG, ts), lambda b, s: (b, 0, s)),
            pl.BlockSpec((1, ts, D_ATTN), lambda b, s: (b, s, 0)),
        ],
        out_shape=[
            jax.ShapeDtypeStruct((bsz, seq, D_POOL), BF16),
            jax.ShapeDtypeStruct((bsz, seq, N_HEADS * AUG), BF16),
            jax.ShapeDtypeStruct((bsz, N_HEADS * AUG, seq), BF16),
            jax.ShapeDtypeStruct((bsz, seq, D_ATTN), BF16),
        ],
        scratch_shapes=[
            pltpu.VMEM((ts + MAX_WINDOW, D_POOL), F32),
            pltpu.VMEM((1, LANES), F32),
        ],
        compiler_params=pltpu.CompilerParams(
            dimension_semantics=("arbitrary", "arbitrary"), vmem_limit_bytes=VMEM_LIMIT),
    )(x, shift, scale, g, w1, wf, bfp, wpool, pscale, gpool, place, ones, fmask)


def _attn_kernel(q_ref, kt_ref, v_ref, o_ref):
    seq = q_ref.shape[1]
    qry_i = lax.broadcasted_iota(jnp.int32, (TQ, TK), 0)
    key_i = lax.broadcasted_iota(jnp.int32, (TQ, TK), 1)
    causal = key_i <= qry_i
    one_lane = lax.broadcasted_iota(jnp.int32, (seq, LANES), 1) == 0
    vaug = jnp.concatenate([v_ref[0], jnp.where(one_lane, 1.0, 0.0).astype(BF16)], axis=1)
    out_lane = lax.broadcasted_iota(jnp.int32, (TQ, LANES), 1)
    heads = range(HEAD_PAIR)
    for qi in range(seq // TQ):
        rows = slice(qi * TQ, (qi + 1) * TQ)
        q = [q_ref[0, rows, h * AUG:(h + 1) * AUG] for h in heads]
        m = [jnp.full((TQ, 1), -jnp.inf, F32) for _ in heads]
        acc = [jnp.zeros((TQ, 2 * LANES), F32) for _ in heads]
        for kj in range(qi + 1):
            keys = slice(kj * TK, (kj + 1) * TK)
            for h in heads:
                s = _dot(q[h], kt_ref[0, h * AUG:(h + 1) * AUG, keys])
                if kj == qi:
                    s = jnp.where(causal, s, -jnp.inf)
                m_new = jnp.maximum(m[h], jnp.max(s, axis=1, keepdims=True))
                p = jnp.exp2(s - m_new).astype(BF16)
                acc[h] = jnp.exp2(m[h] - m_new) * acc[h] + _dot(p, vaug[keys, :])
                m[h] = m_new
        o = [acc[h][:, :LANES] / acc[h][:, LANES:LANES + 1] for h in heads]
        o_ref[0, rows, :] = jnp.where(out_lane < HEAD_DIM, o[0], o[1])


def _attn(q, kt, v):
    bsz, seq, _ = q.shape
    return pl.pallas_call(
        _attn_kernel,
        name="attn",
        grid=(bsz, N_HEADS // HEAD_PAIR),
        in_specs=[
            pl.BlockSpec((1, seq, HEAD_PAIR * AUG), lambda b, h: (b, 0, h)),
            pl.BlockSpec((1, HEAD_PAIR * AUG, seq), lambda b, h: (b, h, 0)),
            pl.BlockSpec((1, seq, LANES), lambda b, h: (b, 0, h)),
        ],
        out_specs=pl.BlockSpec((1, seq, LANES), lambda b, h: (b, 0, h)),
        out_shape=jax.ShapeDtypeStruct((bsz, seq, D_ATTN), F32),
        compiler_params=pltpu.CompilerParams(
            dimension_semantics=("arbitrary", "arbitrary"), vmem_limit_bytes=VMEM_LIMIT),
    )(q, kt, v)


def _postmix_kernel(x_ref, pool_ref, o_ref, wout_ref, gattn_ref, gpost_ref, gate_ref,
                    gffn_ref, shift_ref, scale_ref, wr_ref,
                    x1_ref, h2_ref, st_ref):
    ya = _rms(o_ref[0], gattn_ref[...]).astype(BF16)
    mixed = _dot(pool_ref[0], wout_ref[:D_POOL, :]) + _dot(ya, wout_ref[D_POOL:, :])
    x1 = x_ref[0] + gate_ref[0] * _rms(mixed, gpost_ref[...])
    x1_ref[0] = x1
    h2 = _rms(x1, gffn_ref[...]) * (1.0 + scale_ref[0]) + shift_ref[0]
    h2b = h2.astype(BF16)
    h2_ref[0] = h2b
    h2l = (h2 - h2b.astype(F32)).astype(BF16)
    both = _dot(h2b, wr_ref[...])
    logits = both[:, :LANES] + both[:, LANES:] + _dot(h2l, wr_ref[:, :LANES])
    st_ref[...] = jax.nn.sigmoid(logits).T


def _postmix(x, pool, ot, wout, gattn, gpost, gate, gffn, shift, scale, wr):
    bsz, seq, _ = x.shape
    ts = TS_POST
    ns = seq // ts
    full = lambda shape: pl.BlockSpec(shape, lambda b, s: (0,) * len(shape))
    per_batch = pl.BlockSpec((1, 1, D_MODEL), lambda b, s: (b, 0, 0))
    tok = lambda width: pl.BlockSpec((1, ts, width), lambda b, s: (b, s, 0))
    return pl.pallas_call(
        _postmix_kernel,
        name="postmix",
        grid=(bsz, ns),
        in_specs=[
            tok(D_MODEL), tok(D_POOL), tok(D_ATTN),
            full((D_MODEL, D_MODEL)),
            full((1, D_ATTN)), full((1, D_MODEL)), per_batch,
            full((1, D_MODEL)), per_batch, per_batch,
            full((D_MODEL, 2 * LANES)),
        ],
        out_specs=[
            tok(D_MODEL), tok(D_MODEL),
            pl.BlockSpec((LANES, ts), lambda b, s: (0, b * ns + s)),
        ],
        out_shape=[
            jax.ShapeDtypeStruct((bsz, seq, D_MODEL), F32),
            jax.ShapeDtypeStruct((bsz, seq, D_MODEL), BF16),
            jax.ShapeDtypeStruct((LANES, bsz * seq), F32),
        ],
        compiler_params=pltpu.CompilerParams(
            dimension_semantics=("arbitrary", "arbitrary"), vmem_limit_bytes=VMEM_LIMIT),
    )(x, pool, ot, wout, gattn, gpost, gate, gffn, shift, scale, wr)


def _route_kernel(st_ref, bias_ref, infok_ref, infot_ref, cnt_ref):
    tile = pl.program_id(0)
    tr = st_ref.shape[1]
    scores = [st_ref[g * GROUP_SIZE:(g + 1) * GROUP_SIZE, :] for g in range(N_EXPERT_GROUPS)]
    sel = [scores[g] + bias_ref[g * GROUP_SIZE:(g + 1) * GROUP_SIZE, :]
           for g in range(N_EXPERT_GROUPS)]
    sub = lax.broadcasted_iota(jnp.int32, (GROUP_SIZE, tr), 0)
    neg = jnp.float32(-jnp.inf)

    rows = []
    for g in range(N_EXPERT_GROUPS):
        v = sel[g]
        m1 = jnp.max(v, axis=0, keepdims=True)
        first = jnp.min(jnp.where(v == m1, sub, GROUP_SIZE), axis=0, keepdims=True)
        m2 = jnp.max(jnp.where(sub == first, neg, v), axis=0, keepdims=True)
        rows.append(m1 + m2)
    gs = jnp.concatenate(rows, axis=0)

    shape = (GROUP_SIZE, tr)
    one = jnp.ones(shape, jnp.int32)
    zero = jnp.zeros(shape, jnp.int32)
    beaten = zero
    for gp in range(N_EXPERT_GROUPS):
        r = jnp.broadcast_to(gs[gp:gp + 1, :], shape)
        tie = jnp.where(sub > gp, one, zero)
        beaten = beaten + jnp.where(r > gs, one, zero) + jnp.where(r == gs, tie, zero)
    keep = jnp.where(beaten < TOPK_GROUPS, 1.0, 0.0)

    masked = [jnp.where(jnp.broadcast_to(keep[g:g + 1, :], shape) > 0.5, sel[g], neg)
              for g in range(N_EXPERT_GROUPS)]

    eidx = [sub + g * GROUP_SIZE for g in range(N_EXPERT_GROUPS)]
    picked = [zero for _ in range(N_EXPERT_GROUPS)]
    for _ in range(TOP_K):
        best = masked[0]
        for g in range(1, N_EXPERT_GROUPS):
            best = jnp.maximum(best, masked[g])
        best = jnp.broadcast_to(jnp.max(best, axis=0, keepdims=True), shape)
        first = jnp.where(masked[0] == best, eidx[0], N_EXPERTS)
        for g in range(1, N_EXPERT_GROUPS):
            first = jnp.minimum(first, jnp.where(masked[g] == best, eidx[g], N_EXPERTS))
        first = jnp.broadcast_to(jnp.min(first, axis=0, keepdims=True), shape)
        for g in range(N_EXPERT_GROUPS):
            hit = eidx[g] == first
            picked[g] = jnp.where(hit, one, picked[g])
            masked[g] = jnp.where(hit, neg, masked[g])
    chosen = [picked[g] > 0 for g in range(N_EXPERT_GROUPS)]

    w = [jnp.where(chosen[g], scores[g], 0.0) for g in range(N_EXPERT_GROUPS)]
    denom = w[0].sum(axis=0, keepdims=True)
    for g in range(1, N_EXPERT_GROUPS):
        denom = denom + w[g].sum(axis=0, keepdims=True)
    wfull = jnp.concatenate([w[g] / denom * ROUTED_SCALE for g in range(N_EXPERT_GROUPS)], axis=0)
    mfull = jnp.concatenate([jnp.where(chosen[g], 1.0, 0.0) for g in range(N_EXPERT_GROUPS)],
                            axis=0)
    mb = mfull.astype(BF16)

    e_r = lax.broadcasted_iota(jnp.int32, (N_EXPERTS, N_EXPERTS), 0)
    e_c = lax.broadcasted_iota(jnp.int32, (N_EXPERTS, N_EXPERTS), 1)
    before_e = (e_c < e_r).astype(BF16)
    t_r = lax.broadcasted_iota(jnp.int32, (tr, tr), 0)
    t_c = lax.broadcasted_iota(jnp.int32, (tr, tr), 1)
    before_t = (t_r < t_c).astype(BF16)
    ordinal = _dot(before_e, mb)
    rank = _dot(mb, before_t)
    n = jnp.sum(mfull, axis=1, keepdims=True)
    pieces = jnp.floor((n + (PIECE - 1.0)) * (1.0 / PIECE))
    run_start = PIECE * _dot(before_e, jnp.broadcast_to(pieces, (N_EXPERTS, LANES)).astype(BF16))
    pos = run_start[:, 0:1] + rank

    sub8 = lax.broadcasted_iota(jnp.int32, (TOP_K, tr), 0)
    pos8 = jnp.zeros((TOP_K, tr), F32)
    w8 = jnp.zeros((TOP_K, tr), F32)
    for k in range(TOP_K):
        selk = jnp.where(ordinal == float(k), mfull, 0.0)
        pk = jnp.sum(selk * pos, axis=0, keepdims=True)
        wk = jnp.sum(selk * wfull, axis=0, keepdims=True)
        pos8 = jnp.where(sub8 == k, jnp.broadcast_to(pk, (TOP_K, tr)), pos8)
        w8 = jnp.where(sub8 == k, jnp.broadcast_to(wk, (TOP_K, tr)), w8)
    infok_ref[...] = jnp.concatenate([pos8, w8], axis=0)
    info = jnp.concatenate([pos8, w8, jnp.zeros((LANES - 2 * TOP_K, tr), F32)], axis=0)
    infot_ref[...] = info.T

    @pl.when(tile == 0)
    def _():
        cnt_ref[...] = jnp.zeros_like(cnt_ref)

    lane = lax.broadcasted_iota(jnp.int32, cnt_ref.shape, 1)
    cnt_ref[...] = jnp.where(lane == tile, jnp.broadcast_to(n, cnt_ref.shape), cnt_ref[...])


def _route(st, bias_col):
    t = st.shape[1]
    assert t // TW <= LANES
    return pl.pallas_call(
        _route_kernel,
        name="route",
        grid=(t // TW,),
        in_specs=[
            pl.BlockSpec((LANES, TW), lambda i: (0, i)),
            pl.BlockSpec((N_EXPERTS, 1), lambda i: (0, 0)),
        ],
        out_specs=[
            pl.BlockSpec((2 * TOP_K, TW), lambda i: (0, i)),
            pl.BlockSpec((TW, LANES), lambda i: (i, 0)),
            pl.BlockSpec((N_EXPERTS, LANES), lambda i: (0, 0)),
        ],
        out_shape=[
            jax.ShapeDtypeStruct((2 * TOP_K, t), F32),
            jax.ShapeDtypeStruct((t, LANES), F32),
            jax.ShapeDtypeStruct((N_EXPERTS, LANES), F32),
        ],
        compiler_params=pltpu.CompilerParams(dimension_semantics=("arbitrary",)),
    )(st, bias_col)


def _swiglu(xb, wg, wu):
    g = _dot(xb, wg)
    return (g * jax.nn.sigmoid(g)) * _dot(xb, wu)


def _piece_copy(src_ref, src_row, dst_ref, dst_row, sem, n_pieces=1):
    rows = n_pieces * PIECE
    src = src_ref.at[pl.ds(pl.multiple_of(src_row * PIECE, PIECE), rows), :]
    dst = dst_ref.at[pl.ds(pl.multiple_of(dst_row * PIECE, PIECE), rows), :]
    return pltpu.make_async_copy(src, dst, sem)


LOCAL_BITS = 8
MAX_DOUBLES = LROWS // (2 * PIECE)
MAX_SINGLES = N_EXPERTS


def _for_each_copy(doubles_ref, singles_ref, counts_ref, tile, fn):
    def unpack(word):
        return word & ((1 << LOCAL_BITS) - 1), lax.shift_right_logical(word, LOCAL_BITS)

    def double(i, carry):
        fn(*unpack(doubles_ref[tile * MAX_DOUBLES + i]), 2)
        return carry

    def single(i, carry):
        fn(*unpack(singles_ref[tile * MAX_SINGLES + i]), 1)
        return carry

    lax.fori_loop(0, counts_ref[2 * tile], double, 0)
    lax.fori_loop(0, counts_ref[2 * tile + 1], single, 0)


def _wait_pieces(total, copy_of):
    for bit in range((LROWS // PIECE).bit_length()):
        @pl.when((lax.shift_right_logical(total, bit) & 1) == 1)
        def _():
            copy_of(1 << bit).wait()


def _dispatch_kernel(doubles_ref, singles_ref, counts_ref, tail_ref, ntail_ref, nv_ref,
                     h2_ref, infok_ref, xs_ref, lbuf, zbuf, sem, zsem):
    tile = pl.program_id(0)
    last = pl.num_programs(0) - 1
    slot = tile % 2
    n_blocks = xs_ref.shape[0] // RB

    def spare_blocks(fn):
        def per_block(b, carry):
            fn(b)
            return carry
        lax.fori_loop(nv_ref[0], n_blocks, per_block, 0)

    def zero_block(b):
        dst = xs_ref.at[pl.ds(pl.multiple_of(b * RB, RB), RB), :]
        return pltpu.make_async_copy(zbuf, dst, zsem)

    def tails(fn):
        def per_expert(e, carry):
            n = ntail_ref[e]
            for bit in range((RB // PIECE - 1).bit_length()):
                @pl.when((lax.shift_right_logical(n, bit) & 1) == 1)
                def _():
                    done = n & ((1 << bit) - 1)
                    fn(_piece_copy(zbuf, 0, xs_ref, tail_ref[e] + done, zsem, 1 << bit))
            return carry
        lax.fori_loop(0, N_EXPERTS, per_expert, 0)

    def start(t, s):
        _for_each_copy(
            doubles_ref, singles_ref, counts_ref, t,
            lambda lp, gp, n: _piece_copy(lbuf.at[s], lp, xs_ref, gp, sem.at[s], n).start())

    def wait(t, s):
        _wait_pieces(2 * counts_ref[2 * t] + counts_ref[2 * t + 1],
                     lambda n: _piece_copy(lbuf.at[s], 0, xs_ref, 0, sem.at[s], n))

    @pl.when(tile == 0)
    def _():
        zbuf[...] = jnp.zeros_like(zbuf)
        tails(lambda copy: copy.start())
        spare_blocks(lambda b: zero_block(b).start())

    @pl.when(tile >= 2)
    def _():
        wait(tile - 2, slot)

    xb = h2_ref[...]
    used_rows = PIECE * (2 * counts_ref[2 * tile] + counts_ref[2 * tile + 1])

    def sort_chunk(j):
        r = (lax.broadcasted_iota(jnp.int32, (LCHUNK, TW), 0) + j * LCHUNK).astype(F32)
        onehot = jnp.zeros((LCHUNK, TW), F32)
        for k in range(TOP_K):
            onehot = jnp.where(r == infok_ref[k:k + 1, :], 1.0, onehot)
        lbuf[slot, j * LCHUNK:(j + 1) * LCHUNK, :] = _dot(onehot.astype(BF16), xb).astype(BF16)

    for j in range(LROWS // LCHUNK):
        if (j + 1) * LCHUNK <= TOP_K * TW:
            sort_chunk(j)
        else:
            pl.when(used_rows > j * LCHUNK)(functools.partial(sort_chunk, j))

    start(tile, slot)

    @pl.when(tile == last)
    def _():
        @pl.when(tile >= 1)
        def _():
            wait(tile - 1, 1 - slot)
        wait(tile, slot)
        tails(lambda copy: copy.wait())
        spare_blocks(lambda b: zero_block(b).wait())


def _dispatch(tables, h2, infok, n_rows):
    t = h2.shape[0]
    return pl.pallas_call(
        _dispatch_kernel,
        name="dispatch",
        grid_spec=pltpu.PrefetchScalarGridSpec(
            num_scalar_prefetch=6,
            grid=(t // TW,),
            in_specs=[
                pl.BlockSpec((TW, D_MODEL), lambda i, *_: (i, 0)),
                pl.BlockSpec((2 * TOP_K, TW), lambda i, *_: (0, i)),
            ],
            out_specs=pl.BlockSpec(memory_space=pl.ANY),
            scratch_shapes=[
                pltpu.VMEM((2, LROWS, D_MODEL), BF16),
                pltpu.VMEM((RB, D_MODEL), BF16),
                pltpu.SemaphoreType.DMA((2,)),
                pltpu.SemaphoreType.DMA(()),
            ],
        ),
        out_shape=jax.ShapeDtypeStruct((n_rows, D_MODEL), BF16),
        compiler_params=pltpu.CompilerParams(
            dimension_semantics=("arbitrary",), vmem_limit_bytes=VMEM_LIMIT),
    )(*tables, h2, infok)


def _experts_kernel(be_ref, nv_ref, xs_ref, wg_ref, wu_ref, wd_ref, ys_ref, wgb, wub, wdb):
    i = pl.program_id(0)

    @pl.when(i < nv_ref[0])
    def _():
        @pl.when((i == 0) | (be_ref[i] != be_ref[jnp.maximum(i - 1, 0)]))
        def _():
            wgb[...] = wg_ref[0].astype(BF16)
            wub[...] = wu_ref[0].astype(BF16)
            wdb[...] = wd_ref[0].astype(BF16)

        for j in range(RB // RSUB):
            rows = slice(j * RSUB, (j + 1) * RSUB)
            h = _swiglu(xs_ref[rows, :], wgb[...], wub[...])
            ys_ref[rows, :] = _dot(h.astype(BF16), wdb[...]).astype(BF16)


def _experts(block_expert, n_valid, xs, wg, wu, wd):
    n_blocks = xs.shape[0] // RB
    rows = pl.BlockSpec((RB, D_MODEL), lambda i, be, nv: (jnp.minimum(i, nv[0] - 1), 0))
    return pl.pallas_call(
        _experts_kernel,
        name="experts",
        grid_spec=pltpu.PrefetchScalarGridSpec(
            num_scalar_prefetch=2,
            grid=(n_blocks,),
            in_specs=[
                rows,
                pl.BlockSpec((1, D_MODEL, D_EXPERT), lambda i, be, nv: (be[i], 0, 0)),
                pl.BlockSpec((1, D_MODEL, D_EXPERT), lambda i, be, nv: (be[i], 0, 0)),
                pl.BlockSpec((1, D_EXPERT, D_MODEL), lambda i, be, nv: (be[i], 0, 0)),
            ],
            out_specs=rows,
            scratch_shapes=[
                pltpu.VMEM((D_MODEL, D_EXPERT), BF16),
                pltpu.VMEM((D_MODEL, D_EXPERT), BF16),
                pltpu.VMEM((D_EXPERT, D_MODEL), BF16),
            ],
        ),
        out_shape=jax.ShapeDtypeStruct(xs.shape, BF16),
        input_output_aliases={2: 0},
        compiler_params=pltpu.CompilerParams(
            dimension_semantics=("arbitrary",), vmem_limit_bytes=VMEM_LIMIT),
    )(block_expert, n_valid, xs, wg, wu, wd)


def _combine_kernel(doubles_ref, singles_ref, counts_ref,
                    infot_ref, h2_ref, x1_ref, gate_ref, gpost_ref, wsg_ref, wsu_ref, wsd_ref,
                    ys_ref, o_ref, ybuf, acc_ref, sem):
    tile = pl.program_id(0)
    last = pl.num_programs(0) - 1
    slot = tile % 2

    def start(t, s):
        _for_each_copy(
            doubles_ref, singles_ref, counts_ref, t,
            lambda lp, gp, n: _piece_copy(ys_ref, gp, ybuf.at[s], lp, sem.at[s], n).start())

    def wait(t, s):
        _wait_pieces(2 * counts_ref[2 * t] + counts_ref[2 * t + 1],
                     lambda n: _piece_copy(ys_ref, 0, ybuf.at[s], 0, sem.at[s], n))

    @pl.when(tile == 0)
    def _():
        ybuf[...] = jnp.zeros_like(ybuf)
        start(tile, slot)

    @pl.when(tile < last)
    def _():
        start(tile + 1, 1 - slot)

    wait(tile, slot)

    info = infot_ref[...]
    used_rows = PIECE * (2 * counts_ref[2 * tile] + counts_ref[2 * tile + 1])

    def sum_chunk(j):
        r = (lax.broadcasted_iota(jnp.int32, (TW, LCHUNK), 1) + j * LCHUNK).astype(F32)
        wmat = jnp.zeros((TW, LCHUNK), F32)
        for k in range(TOP_K):
            wmat = jnp.where(r == info[:, k:k + 1], info[:, TOP_K + k:TOP_K + k + 1], wmat)
        return _dot(wmat.astype(BF16), ybuf[slot, j * LCHUNK:(j + 1) * LCHUNK, :])

    def add_chunk(j):
        acc_ref[...] += sum_chunk(j)

    hs = _swiglu(h2_ref[...], wsg_ref[...], wsu_ref[...])
    ff = _dot(hs.astype(BF16), wsd_ref[...])
    always = TOP_K * TW // LCHUNK
    for j in range(always):
        ff = ff + sum_chunk(j)
    acc_ref[...] = ff
    for j in range(always, LROWS // LCHUNK):
        pl.when(used_rows > j * LCHUNK)(functools.partial(add_chunk, j))
    ff = acc_ref[...]
    o_ref[...] = x1_ref[...] + gate_ref[0] * _rms(ff, gpost_ref[...])


def _combine(tables, infot, h2, x1, gate, gpost, wsg, wsu, wsd, ys, seq):
    t = h2.shape[0]
    per_seq = seq // TW
    full = lambda shape: pl.BlockSpec(shape, lambda i, *_: (0,) * len(shape))
    tok = lambda width: pl.BlockSpec((TW, width), lambda i, *_: (i, 0))
    return pl.pallas_call(
        _combine_kernel,
        name="combine",
        grid_spec=pltpu.PrefetchScalarGridSpec(
            num_scalar_prefetch=3,
            grid=(t // TW,),
            in_specs=[
                tok(LANES), tok(D_MODEL), tok(D_MODEL),
                pl.BlockSpec((1, 1, D_MODEL), lambda i, *_: (i // per_seq, 0, 0)),
                full((1, D_MODEL)),
                full((D_MODEL, D_SHARED)), full((D_MODEL, D_SHARED)), full((D_SHARED, D_MODEL)),
                pl.BlockSpec(memory_space=pl.ANY),
            ],
            out_specs=tok(D_MODEL),
            scratch_shapes=[
                pltpu.VMEM((2, LROWS, D_MODEL), BF16),
                pltpu.VMEM((TW, D_MODEL), F32),
                pltpu.SemaphoreType.DMA((2,)),
            ],
        ),
        out_shape=jax.ShapeDtypeStruct((t, D_MODEL), F32),
        compiler_params=pltpu.CompilerParams(
            dimension_semantics=("arbitrary",), vmem_limit_bytes=VMEM_LIMIT),
    )(*tables, infot, h2, x1, gate, gpost, wsg, wsu, wsd, ys)


def _dispatch_plan(cnt, n_tiles):
    n = cnt[:, :n_tiles].astype(jnp.int32)
    pieces = (n + PIECE - 1) // PIECE
    local = jnp.cumsum(pieces, axis=0) - pieces
    seg = jnp.sum(pieces, axis=1)
    per_block = RB // PIECE
    seg_pad = (seg + per_block - 1) // per_block * per_block
    seg_end = jnp.cumsum(seg_pad)
    seg_start = seg_end - seg_pad
    glob = seg_start[:, None] + jnp.cumsum(pieces, axis=1) - pieces
    n_blocks = (TOP_K * TW * n_tiles + N_EXPERTS * n_tiles * (PIECE - 1)
                + N_EXPERTS * (RB - PIECE) + RB - 1) // RB
    n_valid = seg_end[-1] // per_block
    blk = jnp.minimum(jnp.arange(n_blocks, dtype=jnp.int32), n_valid - 1)
    block_end = seg_end // per_block
    block_expert = jnp.sum((block_end[None, :] <= blk[:, None]).astype(jnp.int32), axis=1)
    block_expert = jnp.minimum(block_expert, N_EXPERTS - 1)
    assert LROWS // PIECE <= 1 << LOCAL_BITS and n_blocks * per_block < 1 << (31 - LOCAL_BITS)

    def copy_list(count, width):
        end = jnp.cumsum(count, axis=0).T
        slot = jnp.arange(width, dtype=jnp.int32)
        expert = jnp.sum((end[:, None, :] <= slot[None, :, None]).astype(jnp.int32), axis=-1)
        pick = expert[:, :, None] == jnp.arange(N_EXPERTS, dtype=jnp.int32)
        take = lambda a: jnp.sum(jnp.where(pick, a.T[:, None, :], 0), axis=-1)
        return slot[None, :] - take(jnp.cumsum(count, axis=0) - count), take

    doubles_n = pieces // 2
    odd = pieces % 2
    j2, take2 = copy_list(doubles_n, MAX_DOUBLES)
    doubles = (take2(local) + 2 * j2) | ((take2(glob) + 2 * j2) << LOCAL_BITS)
    _, take1 = copy_list(odd, MAX_SINGLES)
    last_piece = pieces - 1
    singles = take1(local + last_piece) | (take1(glob + last_piece) << LOCAL_BITS)
    counts = jnp.stack([jnp.sum(doubles_n, axis=0), jnp.sum(odd, axis=0)], axis=1)
    tables = (doubles.reshape(-1).astype(jnp.int32), singles.reshape(-1).astype(jnp.int32),
              counts.reshape(-1).astype(jnp.int32))
    tails = ((seg_start + seg).astype(jnp.int32), (seg_pad - seg).astype(jnp.int32))
    return tables, tails, block_expert, n_valid.reshape(1).astype(jnp.int32), n_blocks * RB


def kernel(x, c, w_ada, b_ada, g_pre_mix, g_post_mix, g_pre_ffn, g_post_ffn, w_in, b_forget,
           w_pool, pool_scale, g_pool_out, g_attn_out, w_out, w_router, router_bias,
           w_gate, w_up, w_down, ws_gate, ws_up, ws_down):
    bsz, seq, d = x.shape
    depth = w_ada.shape[0]
    for l in range(depth):
        mod = _ada(c, w_ada, b_ada[l][None, :], l)
        shift_m, scale_m, gate_m, shift_f, scale_f, gate_f = [
            m.reshape(bsz, 1, d) for m in jnp.split(mod, 6, axis=-1)]

        wi = w_in[l]
        qs = HEAD_DIM ** -0.5 * LOG2E
        w1 = jnp.concatenate(
            [wi[:, :D_POOL], wi[:, D_POOL:D_POOL + D_ATTN] * qs,
             wi[:, D_POOL + D_ATTN:D_POOL + 3 * D_ATTN]], axis=1).astype(BF16)
        pad = LANES - N_SPLIT * N_HEADS
        wf = jnp.pad(jnp.tile(wi[:, D_POOL + 3 * D_ATTN:], (1, N_SPLIT)),
                     ((0, 0), (0, pad))).astype(BF16)
        bfp = jnp.pad(jnp.tile(b_forget[l], N_SPLIT), (0, pad))[None, :]

        pool, qa, kt, v = _premix(
            x, shift_m, scale_m, g_pre_mix[l][None, :], w1, wf, bfp, w_pool[l].astype(BF16),
            pool_scale[l][None, :], g_pool_out[l][None, :])
        ot = _attn(qa, kt, v)

        wr = jnp.pad(w_router[l], ((0, 0), (0, LANES - N_EXPERTS)))
        wrh = wr.astype(BF16)
        wrl = (wr - wrh.astype(F32)).astype(BF16)
        x1, h2, st = _postmix(
            x, pool, ot, w_out[l].astype(BF16), g_attn_out[l][None, :], g_post_mix[l][None, :],
            gate_m, g_pre_ffn[l][None, :], shift_f, scale_f,
            jnp.concatenate([wrh, wrl], axis=1))

        t = bsz * seq
        infok, infot, cnt = _route(st, router_bias[l][:, None])
        tables, tails, block_expert, n_valid, n_rows = _dispatch_plan(cnt, t // TW)
        h2f = h2.reshape(t, d)
        xs = _dispatch(tables + tails + (n_valid,), h2f, infok, n_rows)
        ys = _experts(block_expert, n_valid, xs, w_gate[l], w_up[l], w_down[l])
        out = _combine(tables, infot, h2f, x1.reshape(t, d), gate_f, g_post_ffn[l][None, :],
                       ws_gate[l].astype(BF16), ws_up[l].astype(BF16), ws_down[l].astype(BF16),
                       ys, seq)
        x = out.reshape(bsz, seq, d)
    return x
```

```python
import functools

import numpy as np
import jax
import jax.numpy as jnp
from jax import lax
from jax.experimental import pallas as pl
from jax.experimental.pallas import tpu as pltpu

D_MODEL = 1024
D_POOL = 512
POOL_WINDOWS = (2, 4, 8, 16)
POOL_GROUP = 128
MAX_WINDOW = max(POOL_WINDOWS)
D_ATTN = 512
HEAD_DIM = 64
N_HEADS = 8
N_EXPERTS = 64
N_EXPERT_GROUPS = 8
GROUP_SIZE = N_EXPERTS // N_EXPERT_GROUPS
TOPK_GROUPS = 4
TOP_K = 8
D_EXPERT = 256
D_SHARED = 256
ROUTED_SCALE = 2.5
EPS = 1e-6

LOG2E = 1.4426950408889634
LANES = 128
N_SPLIT = 3
AUG = LANES

TS_PRE = 512
TS_POST = 512
TQ = 512
TK = 512
HEAD_PAIR = 128 // HEAD_DIM
TW = 256
PIECE = 16
RB = 1024
RSUB = 512
LCHUNK = 512
LROWS = -(-(TOP_K * TW + N_EXPERTS * (PIECE - 1)) // LCHUNK) * LCHUNK

F32 = jnp.float32
BF16 = jnp.bfloat16
VMEM_LIMIT = 56 * 1024 * 1024


def _rms(v, g):
    return v * lax.rsqrt(jnp.mean(v * v, axis=-1, keepdims=True) + EPS) * g


def _split3(v):
    hi = v.astype(BF16)
    r1 = v - hi.astype(F32)
    mid = r1.astype(BF16)
    r2 = r1 - mid.astype(F32)
    lo = r2.astype(BF16)
    return hi, mid, lo


def _dot(a, b):
    return jnp.dot(a, b, preferred_element_type=F32)


def _ada_kernel(c_ref, w_ref, b_ref, o_ref):
    o_ref[...] = _dot(c_ref[...].astype(BF16), w_ref[0].astype(BF16)) + b_ref[...]


def _ada(c, w, b, layer):
    bsz = c.shape[0]
    n = w.shape[2]
    return pl.pallas_call(
        _ada_kernel,
        name="ada",
        grid=(n // D_MODEL,),
        in_specs=[
            pl.BlockSpec((bsz, D_MODEL), lambda j: (0, 0)),
            pl.BlockSpec((1, D_MODEL, D_MODEL), lambda j: (layer, 0, j)),
            pl.BlockSpec((1, D_MODEL), lambda j: (0, j)),
        ],
        out_specs=pl.BlockSpec((bsz, D_MODEL), lambda j: (0, j)),
        out_shape=jax.ShapeDtypeStruct((bsz, n), F32),
    )(c, w, b)


def _premix_kernel(x_ref, shift_ref, scale_ref, g_ref, w1_ref, wf_ref, bf_ref, wpool_ref,
                   pscale_ref, gpool_ref, place_ref, ones_ref, fmask_ref,
                   pool_ref, q_ref, kt_ref, v_ref,
                   uext_ref, cum_ref):
    s = pl.program_id(1)
    ts = x_ref.shape[1]

    @pl.when(s == 0)
    def _():
        uext_ref[0:MAX_WINDOW, :] = jnp.zeros((MAX_WINDOW, D_POOL), F32)
        cum_ref[...] = jnp.zeros_like(cum_ref)

    x = x_ref[0]
    h = _rms(x, g_ref[...]) * (1.0 + scale_ref[0]) + shift_ref[0]
    hb = h.astype(BF16)
    proj = _dot(hb, w1_ref[...])
    u = proj[:, :D_POOL]
    q = proj[:, D_POOL:D_POOL + D_ATTN]
    k = proj[:, D_POOL + D_ATTN:D_POOL + 2 * D_ATTN]
    v = proj[:, D_POOL + 2 * D_ATTN:]

    uext_ref[MAX_WINDOW:, :] = u
    pos = (s * ts + lax.broadcasted_iota(jnp.int32, (ts, 1), 0) + 1).astype(F32)
    ys = []
    for g, w in enumerate(POOL_WINDOWS):
        c0 = g * POOL_GROUP
        acc = uext_ref[MAX_WINDOW:, c0:c0 + POOL_GROUP]
        for j in range(1, w):
            acc = acc + uext_ref[MAX_WINDOW - j:MAX_WINDOW - j + ts, c0:c0 + POOL_GROUP]
        pooled = acc / jnp.minimum(pos, float(w)) - u[:, c0:c0 + POOL_GROUP]
        ys.append(_dot(pooled.astype(BF16), wpool_ref[g]))
    ypool = jnp.concatenate(ys, axis=1) * pscale_ref[...]
    pool_ref[0] = _rms(ypool, gpool_ref[...]).astype(BF16)
    uext_ref[0:MAX_WINDOW, :] = uext_ref[ts:ts + MAX_WINDOW, :]

    z = _dot(hb, wf_ref[...]) + bf_ref[...]
    logf = jnp.minimum(z, 0.0) - jnp.log1p(jnp.exp(-jnp.abs(z)))
    row = lax.broadcasted_iota(jnp.int32, (ts, ts), 0)
    col = lax.broadcasted_iota(jnp.int32, (ts, ts), 1)
    tri = (col <= row).astype(BF16)
    cum = cum_ref[...]
    for piece in _split3(logf):
        cum = cum + _dot(tri, piece)
    cum_ref[...] = cum[ts - 1:ts, :]

    hi, mid, lo = [piece.astype(F32) for piece in _split3(cum * LOG2E)]
    lane = lax.broadcasted_iota(jnp.int32, (ts, LANES), 1)
    pieces = jnp.where(lane < N_HEADS, hi, jnp.where(lane < 2 * N_HEADS, mid, lo))
    aug = _dot(pieces.astype(BF16), place_ref[...]) + ones_ref[...]
    aug_q = aug[:, :N_HEADS * AUG]
    aug_k = aug[:, N_HEADS * AUG:]

    def expand(a):
        blocks = []
        for j in range(D_ATTN // LANES):
            blk = a[:, j * LANES:(j + 1) * LANES]
            blocks += [blk, blk]
        return jnp.concatenate(blocks, axis=1)

    fmask = fmask_ref[...]
    qa = expand(q) * fmask + aug_q
    ka = expand(k) * fmask + aug_k
    q_ref[0] = qa.astype(BF16)
    kt_ref[0] = ka.T.astype(BF16)
    v_ref[0] = v.astype(BF16)


def _aug_constants():
    width = N_HEADS * AUG
    place = np.zeros((LANES, 2 * width), np.float32)
    ones = np.zeros((1, 2 * width), np.float32)
    fmask = np.zeros((1, width), np.float32)
    for h in range(N_HEADS):
        feat0 = h * AUG + (0 if h % 2 == 0 else HEAD_DIM)
        aug0 = h * AUG + (HEAD_DIM if h % 2 == 0 else 0)
        fmask[0, feat0:feat0 + HEAD_DIM] = 1.0
        for p in range(N_SPLIT):
            place[p * N_HEADS + h, aug0 + p] = 1.0
            ones[0, width + aug0 + p] = 1.0
            place[p * N_HEADS + h, width + aug0 + N_SPLIT + p] = -1.0
            ones[0, aug0 + N_SPLIT + p] = 1.0
    return jnp.asarray(place, BF16), jnp.asarray(ones), jnp.asarray(fmask)


def _premix(x, shift, scale, g, w1, wf, bfp, wpool, pscale, gpool):
    bsz, seq, _ = x.shape
    ts = TS_PRE
    place, ones, fmask = _aug_constants()
    full = lambda shape: pl.BlockSpec(shape, lambda b, s: (0,) * len(shape))
    per_batch = pl.BlockSpec((1, 1, D_MODEL), lambda b, s: (b, 0, 0))
    return pl.pallas_call(
        _premix_kernel,
        name="premix",
        grid=(bsz, seq // ts),
        in_specs=[
            pl.BlockSpec((1, ts, D_MODEL), lambda b, s: (b, s, 0)),
            per_batch, per_batch,
            full((1, D_MODEL)),
            full((D_MODEL, D_POOL + 3 * D_ATTN)),
            full((D_MODEL, LANES)),
            full((1, LANES)),
            full((len(POOL_WINDOWS), POOL_GROUP, POOL_GROUP)),
            full((1, D_POOL)),
            full((1, D_POOL)),
            full((LANES, 2 * N_HEADS * AUG)),
            full((1, 2 * N_HEADS * AUG)),
            full((1, N_HEADS * AUG)),
        ],
        out_specs=[
            pl.BlockSpec((1, ts, D_POOL), lambda b, s: (b, s, 0)),
            pl.BlockSpec((1, ts, N_HEADS * AUG), lambda b, s: (b, s, 0)),
            pl.BlockSpec((1, N_HEADS * AUG, ts), lambda b, s: (b, 0, s)),
            pl.BlockSpec((1, ts, D_ATTN), lambda b, s: (b, s, 0)),
        ],
        out_shape=[
            jax.ShapeDtypeStruct((bsz, seq, D_POOL), BF16),
            jax.ShapeDtypeStruct((bsz, seq, N_HEADS * AUG), BF16),
            jax.ShapeDtypeStruct((bsz, N_HEADS * AUG, seq), BF16),
            jax.ShapeDtypeStruct((bsz, seq, D_ATTN), BF16),
        ],
        scratch_shapes=[
            pltpu.VMEM((ts + MAX_WINDOW, D_POOL), F32),
            pltpu.VMEM((1, LANES), F32),
        ],
        compiler_params=pltpu.CompilerParams(
            dimension_semantics=("arbitrary", "arbitrary"), vmem_limit_bytes=VMEM_LIMIT),
    )(x, shift, scale, g, w1, wf, bfp, wpool, pscale, gpool, place, ones, fmask)


def _attn_kernel(q_ref, kt_ref, v_ref, o_ref):
    seq = q_ref.shape[1]
    qry_i = lax.broadcasted_iota(jnp.int32, (TQ, TK), 0)
    key_i = lax.broadcasted_iota(jnp.int32, (TQ, TK), 1)
    causal = key_i <= qry_i
    one_lane = lax.broadcasted_iota(jnp.int32, (seq, LANES), 1) == 0
    vaug = jnp.concatenate([v_ref[0], jnp.where(one_lane, 1.0, 0.0).astype(BF16)], axis=1)
    out_lane = lax.broadcasted_iota(jnp.int32, (TQ, LANES), 1)
    heads = range(HEAD_PAIR)
    for qi in range(seq // TQ):
        rows = slice(qi * TQ, (qi + 1) * TQ)
        q = [q_ref[0, rows, h * AUG:(h + 1) * AUG] for h in heads]
        m = [jnp.full((TQ, 1), -jnp.inf, F32) for _ in heads]
        acc = [jnp.zeros((TQ, 2 * LANES), F32) for _ in heads]
        for kj in range(qi + 1):
            keys = slice(kj * TK, (kj + 1) * TK)
            for h in heads:
                s = _dot(q[h], kt_ref[0, h * AUG:(h + 1) * AUG, keys])
                if kj == qi:
                    s = jnp.where(causal, s, -jnp.inf)
                m_new = jnp.maximum(m[h], jnp.max(s, axis=1, keepdims=True))
                p = jnp.exp2(s - m_new).astype(BF16)
                acc[h] = jnp.exp2(m[h] - m_new) * acc[h] + _dot(p, vaug[keys, :])
                m[h] = m_new
        o = [acc[h][:, :LANES] / acc[h][:, LANES:LANES + 1] for h in heads]
        o_ref[0, rows, :] = jnp.where(out_lane < HEAD_DIM, o[0], o[1])


def _attn(q, kt, v):
    bsz, seq, _ = q.shape
    return pl.pallas_call(
        _attn_kernel,
        name="attn",
        grid=(bsz, N_HEADS // HEAD_PAIR),
        in_specs=[
            pl.BlockSpec((1, seq, HEAD_PAIR * AUG), lambda b, h: (b, 0, h)),
            pl.BlockSpec((1, HEAD_PAIR * AUG, seq), lambda b, h: (b, h, 0)),
            pl.BlockSpec((1, seq, LANES), lambda b, h: (b, 0, h)),
        ],
        out_specs=pl.BlockSpec((1, seq, LANES), lambda b, h: (b, 0, h)),
        out_shape=jax.ShapeDtypeStruct((bsz, seq, D_ATTN), F32),
        compiler_params=pltpu.CompilerParams(
            dimension_semantics=("arbitrary", "arbitrary"), vmem_limit_bytes=VMEM_LIMIT),
    )(q, kt, v)


def _postmix_kernel(x_ref, pool_ref, o_ref, wout_ref, gattn_ref, gpost_ref, gate_ref,
                    gffn_ref, shift_ref, scale_ref, wr_ref,
                    x1_ref, h2_ref, st_ref):
    ya = _rms(o_ref[0], gattn_ref[...]).astype(BF16)
    mixed = _dot(pool_ref[0], wout_ref[:D_POOL, :]) + _dot(ya, wout_ref[D_POOL:, :])
    x1 = x_ref[0] + gate_ref[0] * _rms(mixed, gpost_ref[...])
    x1_ref[0] = x1
    h2 = _rms(x1, gffn_ref[...]) * (1.0 + scale_ref[0]) + shift_ref[0]
    h2b = h2.astype(BF16)
    h2_ref[0] = h2b
    h2l = (h2 - h2b.astype(F32)).astype(BF16)
    both = _dot(h2b, wr_ref[...])
    logits = both[:, :LANES] + both[:, LANES:] + _dot(h2l, wr_ref[:, :LANES])
    st_ref[...] = jax.nn.sigmoid(logits).T


def _postmix(x, pool, ot, wout, gattn, gpost, gate, gffn, shift, scale, wr):
    bsz, seq, _ = x.shape
    ts = TS_POST
    ns = seq // ts
    full = lambda shape: pl.BlockSpec(shape, lambda b, s: (0,) * len(shape))
    per_batch = pl.BlockSpec((1, 1, D_MODEL), lambda b, s: (b, 0, 0))
    tok = lambda width: pl.BlockSpec((1, ts, width), lambda b, s: (b, s, 0))
    return pl.pallas_call(
        _postmix_kernel,
        name="postmix",
        grid=(bsz, ns),
        in_specs=[
            tok(D_MODEL), tok(D_POOL), tok(D_ATTN),
            full((D_MODEL, D_MODEL)),
            full((1, D_ATTN)), full((1, D_MODEL)), per_batch,
            full((1, D_MODEL)), per_batch, per_batch,
            full((D_MODEL, 2 * LANES)),
        ],
        out_specs=[
            tok(D_MODEL), tok(D_MODEL),
            pl.BlockSpec((LANES, ts), lambda b, s: (0, b * ns + s)),
        ],
        out_shape=[
            jax.ShapeDtypeStruct((bsz, seq, D_MODEL), F32),
            jax.ShapeDtypeStruct((bsz, seq, D_MODEL), BF16),
            jax.ShapeDtypeStruct((LANES, bsz * seq), F32),
        ],
        compiler_params=pltpu.CompilerParams(
            dimension_semantics=("arbitrary", "arbitrary"), vmem_limit_bytes=VMEM_LIMIT),
    )(x, pool, ot, wout, gattn, gpost, gate, gffn, shift, scale, wr)


def _route_kernel(st_ref, bias_ref, infok_ref, infot_ref, cnt_ref):
    tile = pl.program_id(0)
    tr = st_ref.shape[1]
    scores = [st_ref[g * GROUP_SIZE:(g + 1) * GROUP_SIZE, :] for g in range(N_EXPERT_GROUPS)]
    sel = [scores[g] + bias_ref[g * GROUP_SIZE:(g + 1) * GROUP_SIZE, :]
           for g in range(N_EXPERT_GROUPS)]
    sub = lax.broadcasted_iota(jnp.int32, (GROUP_SIZE, tr), 0)
    neg = jnp.float32(-jnp.inf)

    rows = []
    for g in range(N_EXPERT_GROUPS):
        v = sel[g]
        m1 = jnp.max(v, axis=0, keepdims=True)
        first = jnp.min(jnp.where(v == m1, sub, GROUP_SIZE), axis=0, keepdims=True)
        m2 = jnp.max(jnp.where(sub == first, neg, v), axis=0, keepdims=True)
        rows.append(m1 + m2)
    gs = jnp.concatenate(rows, axis=0)

    shape = (GROUP_SIZE, tr)
    one = jnp.ones(shape, jnp.int32)
    zero = jnp.zeros(shape, jnp.int32)
    beaten = zero
    for gp in range(N_EXPERT_GROUPS):
        r = jnp.broadcast_to(gs[gp:gp + 1, :], shape)
        tie = jnp.where(sub > gp, one, zero)
        beaten = beaten + jnp.where(r > gs, one, zero) + jnp.where(r == gs, tie, zero)
    keep = jnp.where(beaten < TOPK_GROUPS, 1.0, 0.0)

    masked = [jnp.where(jnp.broadcast_to(keep[g:g + 1, :], shape) > 0.5, sel[g], neg)
              for g in range(N_EXPERT_GROUPS)]

    eidx = [sub + g * GROUP_SIZE for g in range(N_EXPERT_GROUPS)]
    picked = [zero for _ in range(N_EXPERT_GROUPS)]
    for _ in range(TOP_K):
        best = masked[0]
        for g in range(1, N_EXPERT_GROUPS):
            best = jnp.maximum(best, masked[g])
        best = jnp.broadcast_to(jnp.max(best, axis=0, keepdims=True), shape)
        first = jnp.where(masked[0] == best, eidx[0], N_EXPERTS)
        for g in range(1, N_EXPERT_GROUPS):
            first = jnp.minimum(first, jnp.where(masked[g] == best, eidx[g], N_EXPERTS))
        first = jnp.broadcast_to(jnp.min(first, axis=0, keepdims=True), shape)
        for g in range(N_EXPERT_GROUPS):
            hit = eidx[g] == first
            picked[g] = jnp.where(hit, one, picked[g])
            masked[g] = jnp.where(hit, neg, masked[g])
    chosen = [picked[g] > 0 for g in range(N_EXPERT_GROUPS)]

    w = [jnp.where(chosen[g], scores[g], 0.0) for g in range(N_EXPERT_GROUPS)]
    denom = w[0].sum(axis=0, keepdims=True)
    for g in range(1, N_EXPERT_GROUPS):
        denom = denom + w[g].sum(axis=0, keepdims=True)
    wfull = jnp.concatenate([w[g] / denom * ROUTED_SCALE for g in range(N_EXPERT_GROUPS)], axis=0)
    mfull = jnp.concatenate([jnp.where(chosen[g], 1.0, 0.0) for g in range(N_EXPERT_GROUPS)],
                            axis=0)
    mb = mfull.astype(BF16)

    e_r = lax.broadcasted_iota(jnp.int32, (N_EXPERTS, N_EXPERTS), 0)
    e_c = lax.broadcasted_iota(jnp.int32, (N_EXPERTS, N_EXPERTS), 1)
    before_e = (e_c < e_r).astype(BF16)
    t_r = lax.broadcasted_iota(jnp.int32, (tr, tr), 0)
    t_c = lax.broadcasted_iota(jnp.int32, (tr, tr), 1)
    before_t = (t_r < t_c).astype(BF16)
    ordinal = _dot(before_e, mb)
    rank = _dot(mb, before_t)
    n = jnp.sum(mfull, axis=1, keepdims=True)
    pieces = jnp.floor((n + (PIECE - 1.0)) * (1.0 / PIECE))
    run_start = PIECE * _dot(before_e, jnp.broadcast_to(pieces, (N_EXPERTS, LANES)).astype(BF16))
    pos = run_start[:, 0:1] + rank

    sub8 = lax.broadcasted_iota(jnp.int32, (TOP_K, tr), 0)
    pos8 = jnp.zeros((TOP_K, tr), F32)
    w8 = jnp.zeros((TOP_K, tr), F32)
    for k in range(TOP_K):
        selk = jnp.where(ordinal == float(k), mfull, 0.0)
        pk = jnp.sum(selk * pos, axis=0, keepdims=True)
        wk = jnp.sum(selk * wfull, axis=0, keepdims=True)
        pos8 = jnp.where(sub8 == k, jnp.broadcast_to(pk, (TOP_K, tr)), pos8)
        w8 = jnp.where(sub8 == k, jnp.broadcast_to(wk, (TOP_K, tr)), w8)
    infok_ref[...] = jnp.concatenate([pos8, w8], axis=0)
    info = jnp.concatenate([pos8, w8, jnp.zeros((LANES - 2 * TOP_K, tr), F32)], axis=0)
    infot_ref[...] = info.T

    @pl.when(tile == 0)
    def _():
        cnt_ref[...] = jnp.zeros_like(cnt_ref)

    lane = lax.broadcasted_iota(jnp.int32, cnt_ref.shape, 1)
    cnt_ref[...] = jnp.where(lane == tile, jnp.broadcast_to(n, cnt_ref.shape), cnt_ref[...])


def _route(st, bias_col):
    t = st.shape[1]
    assert t // TW <= LANES
    return pl.pallas_call(
        _route_kernel,
        name="route",
        grid=(t // TW,),
        in_specs=[
            pl.BlockSpec((LANES, TW), lambda i: (0, i)),
            pl.BlockSpec((N_EXPERTS, 1), lambda i: (0, 0)),
        ],
        out_specs=[
            pl.BlockSpec((2 * TOP_K, TW), lambda i: (0, i)),
            pl.BlockSpec((TW, LANES), lambda i: (i, 0)),
            pl.BlockSpec((N_EXPERTS, LANES), lambda i: (0, 0)),
        ],
        out_shape=[
            jax.ShapeDtypeStruct((2 * TOP_K, t), F32),
            jax.ShapeDtypeStruct((t, LANES), F32),
            jax.ShapeDtypeStruct((N_EXPERTS, LANES), F32),
        ],
        compiler_params=pltpu.CompilerParams(dimension_semantics=("arbitrary",)),
    )(st, bias_col)


def _swiglu(xb, wg, wu):
    g = _dot(xb, wg)
    return (g * jax.nn.sigmoid(g)) * _dot(xb, wu)


def _piece_copy(src_ref, src_row, dst_ref, dst_row, sem, n_pieces=1):
    rows = n_pieces * PIECE
    src = src_ref.at[pl.ds(pl.multiple_of(src_row * PIECE, PIECE), rows), :]
    dst = dst_ref.at[pl.ds(pl.multiple_of(dst_row * PIECE, PIECE), rows), :]
    return pltpu.make_async_copy(src, dst, sem)


LOCAL_BITS = 8
MAX_DOUBLES = LROWS // (2 * PIECE)
MAX_SINGLES = N_EXPERTS


def _for_each_copy(doubles_ref, singles_ref, counts_ref, tile, fn):
    def unpack(word):
        return word & ((1 << LOCAL_BITS) - 1), lax.shift_right_logical(word, LOCAL_BITS)

    def double(i, carry):
        fn(*unpack(doubles_ref[tile * MAX_DOUBLES + i]), 2)
        return carry

    def single(i, carry):
        fn(*unpack(singles_ref[tile * MAX_SINGLES + i]), 1)
        return carry

    lax.fori_loop(0, counts_ref[2 * tile], double, 0)
    lax.fori_loop(0, counts_ref[2 * tile + 1], single, 0)


def _wait_pieces(total, copy_of):
    for bit in range((LROWS // PIECE).bit_length()):
        @pl.when((lax.shift_right_logical(total, bit) & 1) == 1)
        def _():
            copy_of(1 << bit).wait()


def _dispatch_kernel(doubles_ref, singles_ref, counts_ref, tail_ref, ntail_ref, nv_ref,
                     h2_ref, infok_ref, xs_ref, lbuf, zbuf, sem, zsem):
    tile = pl.program_id(0)
    last = pl.num_programs(0) - 1
    slot = tile % 2
    n_blocks = xs_ref.shape[0] // RB

    def spare_blocks(fn):
        def per_block(b, carry):
            fn(b)
            return carry
        lax.fori_loop(nv_ref[0], n_blocks, per_block, 0)

    def zero_block(b):
        dst = xs_ref.at[pl.ds(pl.multiple_of(b * RB, RB), RB), :]
        return pltpu.make_async_copy(zbuf, dst, zsem)

    def tails(fn):
        def per_expert(e, carry):
            n = ntail_ref[e]
            for bit in range((RB // PIECE - 1).bit_length()):
                @pl.when((lax.shift_right_logical(n, bit) & 1) == 1)
                def _():
                    done = n & ((1 << bit) - 1)
                    fn(_piece_copy(zbuf, 0, xs_ref, tail_ref[e] + done, zsem, 1 << bit))
            return carry
        lax.fori_loop(0, N_EXPERTS, per_expert, 0)

    def start(t, s):
        _for_each_copy(
            doubles_ref, singles_ref, counts_ref, t,
            lambda lp, gp, n: _piece_copy(lbuf.at[s], lp, xs_ref, gp, sem.at[s], n).start())

    def wait(t, s):
        _wait_pieces(2 * counts_ref[2 * t] + counts_ref[2 * t + 1],
                     lambda n: _piece_copy(lbuf.at[s], 0, xs_ref, 0, sem.at[s], n))

    @pl.when(tile == 0)
    def _():
        zbuf[...] = jnp.zeros_like(zbuf)
        tails(lambda copy: copy.start())
        spare_blocks(lambda b: zero_block(b).start())

    @pl.when(tile >= 2)
    def _():
        wait(tile - 2, slot)

    xb = h2_ref[...]
    used_rows = PIECE * (2 * counts_ref[2 * tile] + counts_ref[2 * tile + 1])

    def sort_chunk(j):
        r = (lax.broadcasted_iota(jnp.int32, (LCHUNK, TW), 0) + j * LCHUNK).astype(F32)
        onehot = jnp.zeros((LCHUNK, TW), F32)
        for k in range(TOP_K):
            onehot = jnp.where(r == infok_ref[k:k + 1, :], 1.0, onehot)
        lbuf[slot, j * LCHUNK:(j + 1) * LCHUNK, :] = _dot(onehot.astype(BF16), xb).astype(BF16)

    for j in range(LROWS // LCHUNK):
        if (j + 1) * LCHUNK <= TOP_K * TW:
            sort_chunk(j)
        else:
            pl.when(used_rows > j * LCHUNK)(functools.partial(sort_chunk, j))

    start(tile, slot)

    @pl.when(tile == last)
    def _():
        @pl.when(tile >= 1)
        def _():
            wait(tile - 1, 1 - slot)
        wait(tile, slot)
        tails(lambda copy: copy.wait())
        spare_blocks(lambda b: zero_block(b).wait())


def _dispatch(tables, h2, infok, n_rows):
    t = h2.shape[0]
    return pl.pallas_call(
        _dispatch_kernel,
        name="dispatch",
        grid_spec=pltpu.PrefetchScalarGridSpec(
            num_scalar_prefetch=6,
            grid=(t // TW,),
            in_specs=[
                pl.BlockSpec((TW, D_MODEL), lambda i, *_: (i, 0)),
                pl.BlockSpec((2 * TOP_K, TW), lambda i, *_: (0, i)),
            ],
            out_specs=pl.BlockSpec(memory_space=pl.ANY),
            scratch_shapes=[
                pltpu.VMEM((2, LROWS, D_MODEL), BF16),
                pltpu.VMEM((RB, D_MODEL), BF16),
                pltpu.SemaphoreType.DMA((2,)),
                pltpu.SemaphoreType.DMA(()),
            ],
        ),
        out_shape=jax.ShapeDtypeStruct((n_rows, D_MODEL), BF16),
        compiler_params=pltpu.CompilerParams(
            dimension_semantics=("arbitrary",), vmem_limit_bytes=VMEM_LIMIT),
    )(*tables, h2, infok)


def _experts_kernel(be_ref, nv_ref, xs_ref, wg_ref, wu_ref, wd_ref, ys_ref, wgb, wub, wdb):
    i = pl.program_id(0)

    @pl.when(i < nv_ref[0])
    def _():
        @pl.when((i == 0) | (be_ref[i] != be_ref[jnp.maximum(i - 1, 0)]))
        def _():
            wgb[...] = wg_ref[0].astype(BF16)
            wub[...] = wu_ref[0].astype(BF16)
            wdb[...] = wd_ref[0].astype(BF16)

        for j in range(RB // RSUB):
            rows = slice(j * RSUB, (j + 1) * RSUB)
            h = _swiglu(xs_ref[rows, :], wgb[...], wub[...])
            ys_ref[rows, :] = _dot(h.astype(BF16), wdb[...]).astype(BF16)


def _experts(block_expert, n_valid, xs, wg, wu, wd):
    n_blocks = xs.shape[0] // RB
    rows = pl.BlockSpec((RB, D_MODEL), lambda i, be, nv: (jnp.minimum(i, nv[0] - 1), 0))
    return pl.pallas_call(
        _experts_kernel,
        name="experts",
        grid_spec=pltpu.PrefetchScalarGridSpec(
            num_scalar_prefetch=2,
            grid=(n_blocks,),
            in_specs=[
                rows,
                pl.BlockSpec((1, D_MODEL, D_EXPERT), lambda i, be, nv: (be[i], 0, 0)),
                pl.BlockSpec((1, D_MODEL, D_EXPERT), lambda i, be, nv: (be[i], 0, 0)),
                pl.BlockSpec((1, D_EXPERT, D_MODEL), lambda i, be, nv: (be[i], 0, 0)),
            ],
            out_specs=rows,
            scratch_shapes=[
                pltpu.VMEM((D_MODEL, D_EXPERT), BF16),
                pltpu.VMEM((D_MODEL, D_EXPERT), BF16),
                pltpu.VMEM((D_EXPERT, D_MODEL), BF16),
            ],
        ),
        out_shape=jax.ShapeDtypeStruct(xs.shape, BF16),
        input_output_aliases={2: 0},
        compiler_params=pltpu.CompilerParams(
            dimension_semantics=("arbitrary",), vmem_limit_bytes=VMEM_LIMIT),
    )(block_expert, n_valid, xs, wg, wu, wd)


def _combine_kernel(doubles_ref, singles_ref, counts_ref,
                    infot_ref, h2_ref, x1_ref, gate_ref, gpost_ref, wsg_ref, wsu_ref, wsd_ref,
                    ys_ref, o_ref, ybuf, acc_ref, sem):
    tile = pl.program_id(0)
    last = pl.num_programs(0) - 1
    slot = tile % 2

    def start(t, s):
        _for_each_copy(
            doubles_ref, singles_ref, counts_ref, t,
            lambda lp, gp, n: _piece_copy(ys_ref, gp, ybuf.at[s], lp, sem.at[s], n).start())

    def wait(t, s):
        _wait_pieces(2 * counts_ref[2 * t] + counts_ref[2 * t + 1],
                     lambda n: _piece_copy(ys_ref, 0, ybuf.at[s], 0, sem.at[s], n))

    @pl.when(tile == 0)
    def _():
        ybuf[...] = jnp.zeros_like(ybuf)
        start(tile, slot)

    @pl.when(tile < last)
    def _():
        start(tile + 1, 1 - slot)

    wait(tile, slot)

    info = infot_ref[...]
    used_rows = PIECE * (2 * counts_ref[2 * tile] + counts_ref[2 * tile + 1])

    def sum_chunk(j):
        r = (lax.broadcasted_iota(jnp.int32, (TW, LCHUNK), 1) + j * LCHUNK).astype(F32)
        wmat = jnp.zeros((TW, LCHUNK), F32)
        for k in range(TOP_K):
            wmat = jnp.where(r == info[:, k:k + 1], info[:, TOP_K + k:TOP_K + k + 1], wmat)
        return _dot(wmat.astype(BF16), ybuf[slot, j * LCHUNK:(j + 1) * LCHUNK, :])

    def add_chunk(j):
        acc_ref[...] += sum_chunk(j)

    hs = _swiglu(h2_ref[...], wsg_ref[...], wsu_ref[...])
    ff = _dot(hs.astype(BF16), wsd_ref[...])
    always = TOP_K * TW // LCHUNK
    for j in range(always):
        ff = ff + sum_chunk(j)
    acc_ref[...] = ff
    for j in range(always, LROWS // LCHUNK):
        pl.when(used_rows > j * LCHUNK)(functools.partial(add_chunk, j))
    ff = acc_ref[...]
    o_ref[...] = x1_ref[...] + gate_ref[0] * _rms(ff, gpost_ref[...])


def _combine(tables, infot, h2, x1, gate, gpost, wsg, wsu, wsd, ys, seq):
    t = h2.shape[0]
    per_seq = seq // TW
    full = lambda shape: pl.BlockSpec(shape, lambda i, *_: (0,) * len(shape))
    tok = lambda width: pl.BlockSpec((TW, width), lambda i, *_: (i, 0))
    return pl.pallas_call(
        _combine_kernel,
        name="combine",
        grid_spec=pltpu.PrefetchScalarGridSpec(
            num_scalar_prefetch=3,
            grid=(t // TW,),
            in_specs=[
                tok(LANES), tok(D_MODEL), tok(D_MODEL),
                pl.BlockSpec((1, 1, D_MODEL), lambda i, *_: (i // per_seq, 0, 0)),
                full((1, D_MODEL)),
                full((D_MODEL, D_SHARED)), full((D_MODEL, D_SHARED)), full((D_SHARED, D_MODEL)),
                pl.BlockSpec(memory_space=pl.ANY),
            ],
            out_specs=tok(D_MODEL),
            scratch_shapes=[
                pltpu.VMEM((2, LROWS, D_MODEL), BF16),
                pltpu.VMEM((TW, D_MODEL), F32),
                pltpu.SemaphoreType.DMA((2,)),
            ],
        ),
        out_shape=jax.ShapeDtypeStruct((t, D_MODEL), F32),
        compiler_params=pltpu.CompilerParams(
            dimension_semantics=("arbitrary",), vmem_limit_bytes=VMEM_LIMIT),
    )(*tables, infot, h2, x1, gate, gpost, wsg, wsu, wsd, ys)


def _dispatch_plan(cnt, n_tiles):
    n = cnt[:, :n_tiles].astype(jnp.int32)
    pieces = (n + PIECE - 1) // PIECE
    local = jnp.cumsum(pieces, axis=0) - pieces
    seg = jnp.sum(pieces, axis=1)
    per_block = RB // PIECE
    seg_pad = (seg + per_block - 1) // per_block * per_block
    seg_end = jnp.cumsum(seg_pad)
    seg_start = seg_end - seg_pad
    glob = seg_start[:, None] + jnp.cumsum(pieces, axis=1) - pieces
    n_blocks = (TOP_K * TW * n_tiles + N_EXPERTS * n_tiles * (PIECE - 1)
                + N_EXPERTS * (RB - PIECE) + RB - 1) // RB
    n_valid = seg_end[-1] // per_block
    blk = jnp.minimum(jnp.arange(n_blocks, dtype=jnp.int32), n_valid - 1)
    block_end = seg_end // per_block
    block_expert = jnp.sum((block_end[None, :] <= blk[:, None]).astype(jnp.int32), axis=1)
    block_expert = jnp.minimum(block_expert, N_EXPERTS - 1)
    assert LROWS // PIECE <= 1 << LOCAL_BITS and n_blocks * per_block < 1 << (31 - LOCAL_BITS)

    def copy_list(count, width):
        end = jnp.cumsum(count, axis=0).T
        slot = jnp.arange(width, dtype=jnp.int32)
        expert = jnp.sum((end[:, None, :] <= slot[None, :, None]).astype(jnp.int32), axis=-1)
        pick = expert[:, :, None] == jnp.arange(N_EXPERTS, dtype=jnp.int32)
        take = lambda a: jnp.sum(jnp.where(pick, a.T[:, None, :], 0), axis=-1)
        return slot[None, :] - take(jnp.cumsum(count, axis=0) - count), take

    doubles_n = pieces // 2
    odd = pieces % 2
    j2, take2 = copy_list(doubles_n, MAX_DOUBLES)
    doubles = (take2(local) + 2 * j2) | ((take2(glob) + 2 * j2) << LOCAL_BITS)
    _, take1 = copy_list(odd, MAX_SINGLES)
    last_piece = pieces - 1
    singles = take1(local + last_piece) | (take1(glob + last_piece) << LOCAL_BITS)
    counts = jnp.stack([jnp.sum(doubles_n, axis=0), jnp.sum(odd, axis=0)], axis=1)
    tables = (doubles.reshape(-1).astype(jnp.int32), singles.reshape(-1).astype(jnp.int32),
              counts.reshape(-1).astype(jnp.int32))
    tails = ((seg_start + seg).astype(jnp.int32), (seg_pad - seg).astype(jnp.int32))
    return tables, tails, block_expert, n_valid.reshape(1).astype(jnp.int32), n_blocks * RB


def kernel(x, c, w_ada, b_ada, g_pre_mix, g_post_mix, g_pre_ffn, g_post_ffn, w_in, b_forget,
           w_pool, pool_scale, g_pool_out, g_attn_out, w_out, w_router, router_bias,
           w_gate, w_up, w_down, ws_gate, ws_up, ws_down):
    bsz, seq, d = x.shape
    depth = w_ada.shape[0]
    for l in range(depth):
        mod = _ada(c, w_ada, b_ada[l][None, :], l)
        shift_m, scale_m, gate_m, shift_f, scale_f, gate_f = [
            m.reshape(bsz, 1, d) for m in jnp.split(mod, 6, axis=-1)]

        wi = w_in[l]
        qs = HEAD_DIM ** -0.5 * LOG2E
        w1 = jnp.concatenate(
            [wi[:, :D_POOL], wi[:, D_POOL:D_POOL + D_ATTN] * qs,
             wi[:, D_POOL + D_ATTN:D_POOL + 3 * D_ATTN]], axis=1).astype(BF16)
        pad = LANES - N_SPLIT * N_HEADS
        wf = jnp.pad(jnp.tile(wi[:, D_POOL + 3 * D_ATTN:], (1, N_SPLIT)),
                     ((0, 0), (0, pad))).astype(BF16)
        bfp = jnp.pad(jnp.tile(b_forget[l], N_SPLIT), (0, pad))[None, :]

        pool, qa, kt, v = _premix(
            x, shift_m, scale_m, g_pre_mix[l][None, :], w1, wf, bfp, w_pool[l].astype(BF16),
            pool_scale[l][None, :], g_pool_out[l][None, :])
        ot = _attn(qa, kt, v)

        wr = jnp.pad(w_router[l], ((0, 0), (0, LANES - N_EXPERTS)))
        wrh = wr.astype(BF16)
        wrl = (wr - wrh.astype(F32)).astype(BF16)
        x1, h2, st = _postmix(
            x, pool, ot, w_out[l].astype(BF16), g_attn_out[l][None, :], g_post_mix[l][None, :],
            gate_m, g_pre_ffn[l][None, :], shift_f, scale_f,
            jnp.concatenate([wrh, wrl], axis=1))

        t = bsz * seq
        infok, infot, cnt = _route(st, router_bias[l][:, None])
        tables, tails, block_expert, n_valid, n_rows = _dispatch_plan(cnt, t // TW)
        h2f = h2.reshape(t, d)
        xs = _dispatch(tables + tails + (n_valid,), h2f, infok, n_rows)
        ys = _experts(block_expert, n_valid, xs, w_gate[l], w_up[l], w_down[l])
        out = _combine(tables, infot, h2f, x1.reshape(t, d), gate_f, g_post_ffn[l][None, :],
                       ws_gate[l].astype(BF16), ws_up[l].astype(BF16), ws_down[l].astype(BF16),
                       ys, seq)
        x = out.reshape(bsz, seq, d)
    return x
```

```python
import functools

import numpy as np
import jax
import jax.numpy as jnp
from jax import lax
from jax.experimental import pallas as pl
from jax.experimental.pallas import tpu as pltpu

D_MODEL = 1024
D_POOL = 512
POOL_WINDOWS = (2, 4, 8, 16)
POOL_GROUP = 128
MAX_WINDOW = max(POOL_WINDOWS)
D_ATTN = 512
HEAD_DIM = 64
N_HEADS = 8
N_EXPERTS = 64
N_EXPERT_GROUPS = 8
GROUP_SIZE = N_EXPERTS // N_EXPERT_GROUPS
TOPK_GROUPS = 4
TOP_K = 8
D_EXPERT = 256
D_SHARED = 256
ROUTED_SCALE = 2.5
EPS = 1e-6

LOG2E = 1.4426950408889634
LANES = 128
N_SPLIT = 3
AUG = LANES

TS_PRE = 512
TS_POST = 512
TQ = 512
TK = 512
HEAD_PAIR = 128 // HEAD_DIM
TW = 256
PIECE = 16
RB = 1024
RSUB = 512
LCHUNK = 512
ONEHOT_GROUP = 64
LROWS = -(-(TOP_K * TW + N_EXPERTS * (PIECE - 1)) // LCHUNK) * LCHUNK

F32 = jnp.float32
BF16 = jnp.bfloat16
VMEM_LIMIT = 56 * 1024 * 1024


def _rms(v, g):
    return v * lax.rsqrt(jnp.mean(v * v, axis=-1, keepdims=True) + EPS) * g


def _split3(v):
    hi = v.astype(BF16)
    r1 = v - hi.astype(F32)
    mid = r1.astype(BF16)
    r2 = r1 - mid.astype(F32)
    lo = r2.astype(BF16)
    return hi, mid, lo


def _dot(a, b):
    return jnp.dot(a, b, preferred_element_type=F32)


def _ada_kernel(c_ref, w_ref, b_ref, o_ref):
    o_ref[...] = _dot(c_ref[...].astype(BF16), w_ref[0].astype(BF16)) + b_ref[...]


def _ada(c, w, b, layer):
    bsz = c.shape[0]
    n = w.shape[2]
    return pl.pallas_call(
        _ada_kernel,
        name="ada",
        grid=(n // D_MODEL,),
        in_specs=[
            pl.BlockSpec((bsz, D_MODEL), lambda j: (0, 0)),
            pl.BlockSpec((1, D_MODEL, D_MODEL), lambda j: (layer, 0, j)),
            pl.BlockSpec((1, D_MODEL), lambda j: (0, j)),
        ],
        out_specs=pl.BlockSpec((bsz, D_MODEL), lambda j: (0, j)),
        out_shape=jax.ShapeDtypeStruct((bsz, n), F32),
    )(c, w, b)


def _premix_kernel(x_ref, shift_ref, scale_ref, g_ref, w1_ref, wf_ref, bf_ref, wpool_ref,
                   pscale_ref, gpool_ref, place_ref, ones_ref, fmask_ref,
                   pool_ref, q_ref, kt_ref, v_ref,
                   uext_ref, cum_ref):
    s = pl.program_id(1)
    ts = x_ref.shape[1]

    @pl.when(s == 0)
    def _():
        uext_ref[0:MAX_WINDOW, :] = jnp.zeros((MAX_WINDOW, D_POOL), F32)
        cum_ref[...] = jnp.zeros_like(cum_ref)

    x = x_ref[0]
    h = _rms(x, g_ref[...]) * (1.0 + scale_ref[0]) + shift_ref[0]
    hb = h.astype(BF16)
    proj = _dot(hb, w1_ref[...])
    u = proj[:, :D_POOL]
    q = proj[:, D_POOL:D_POOL + D_ATTN]
    k = proj[:, D_POOL + D_ATTN:D_POOL + 2 * D_ATTN]
    v = proj[:, D_POOL + 2 * D_ATTN:]

    uext_ref[MAX_WINDOW:, :] = u
    pos = (s * ts + lax.broadcasted_iota(jnp.int32, (ts, 1), 0) + 1).astype(F32)
    ys = []
    for g, w in enumerate(POOL_WINDOWS):
        c0 = g * POOL_GROUP
        acc = uext_ref[MAX_WINDOW:, c0:c0 + POOL_GROUP]
        for j in range(1, w):
            acc = acc + uext_ref[MAX_WINDOW - j:MAX_WINDOW - j + ts, c0:c0 + POOL_GROUP]
        pooled = acc / jnp.minimum(pos, float(w)) - u[:, c0:c0 + POOL_GROUP]
        ys.append(_dot(pooled.astype(BF16), wpool_ref[g]))
    ypool = jnp.concatenate(ys, axis=1) * pscale_ref[...]
    pool_ref[0] = _rms(ypool, gpool_ref[...]).astype(BF16)
    uext_ref[0:MAX_WINDOW, :] = uext_ref[ts:ts + MAX_WINDOW, :]

    z = _dot(hb, wf_ref[...]) + bf_ref[...]
    logf = jnp.minimum(z, 0.0) - jnp.log1p(jnp.exp(-jnp.abs(z)))
    row = lax.broadcasted_iota(jnp.int32, (ts, ts), 0)
    col = lax.broadcasted_iota(jnp.int32, (ts, ts), 1)
    tri = (col <= row).astype(BF16)
    cum = cum_ref[...]
    for piece in _split3(logf):
        cum = cum + _dot(tri, piece)
    cum_ref[...] = cum[ts - 1:ts, :]

    hi, mid, lo = [piece.astype(F32) for piece in _split3(cum * LOG2E)]
    lane = lax.broadcasted_iota(jnp.int32, (ts, LANES), 1)
    pieces = jnp.where(lane < N_HEADS, hi, jnp.where(lane < 2 * N_HEADS, mid, lo))
    aug = _dot(pieces.astype(BF16), place_ref[...]) + ones_ref[...]
    aug_q = aug[:, :N_HEADS * AUG]
    aug_k = aug[:, N_HEADS * AUG:]

    def expand(a):
        blocks = []
        for j in range(D_ATTN // LANES):
            blk = a[:, j * LANES:(j + 1) * LANES]
            blocks += [blk, blk]
        return jnp.concatenate(blocks, axis=1)

    fmask = fmask_ref[...]
    qa = expand(q) * fmask + aug_q
    ka = expand(k) * fmask + aug_k
    q_ref[0] = qa.astype(BF16)
    kt_ref[0] = ka.T.astype(BF16)
    v_ref[0] = v.astype(BF16)


def _aug_constants():
    width = N_HEADS * AUG
    place = np.zeros((LANES, 2 * width), np.float32)
    ones = np.zeros((1, 2 * width), np.float32)
    fmask = np.zeros((1, width), np.float32)
    for h in range(N_HEADS):
        feat0 = h * AUG + (0 if h % 2 == 0 else HEAD_DIM)
        aug0 = h * AUG + (HEAD_DIM if h % 2 == 0 else 0)
        fmask[0, feat0:feat0 + HEAD_DIM] = 1.0
        for p in range(N_SPLIT):
            place[p * N_HEADS + h, aug0 + p] = 1.0
            ones[0, width + aug0 + p] = 1.0
            place[p * N_HEADS + h, width + aug0 + N_SPLIT + p] = -1.0
            ones[0, aug0 + N_SPLIT + p] = 1.0
    return jnp.asarray(place, BF16), jnp.asarray(ones), jnp.asarray(fmask)


def _premix(x, shift, scale, g, w1, wf, bfp, wpool, pscale, gpool):
    bsz, seq, _ = x.shape
    ts = TS_PRE
    place, ones, fmask = _aug_constants()
    full = lambda shape: pl.BlockSpec(shape, lambda b, s: (0,) * len(shape))
    per_batch = pl.BlockSpec((1, 1, D_MODEL), lambda b, s: (b, 0, 0))
    return pl.pallas_call(
        _premix_kernel,
        name="premix",
        grid=(bsz, seq // ts),
        in_specs=[
            pl.BlockSpec((1, ts, D_MODEL), lambda b, s: (b, s, 0)),
            per_batch, per_batch,
            full((1, D_MODEL)),
            full((D_MODEL, D_POOL + 3 * D_ATTN)),
            full((D_MODEL, LANES)),
            full((1, LANES)),
            full((len(POOL_WINDOWS), POOL_GROUP, POOL_GROUP)),
            full((1, D_POOL)),
            full((1, D_POOL)),
            full((LANES, 2 * N_HEADS * AUG)),
            full((1, 2 * N_HEADS * AUG)),
            full((1, N_HEADS * AUG)),
        ],
        out_specs=[
            pl.BlockSpec((1, ts, D_POOL), lambda b, s: (b, s, 0)),
            pl.BlockSpec((1, ts, N_HEADS * AUG), lambda b, s: (b, s, 0)),
            pl.BlockSpec((1, N_HEADS * AUG, ts), lambda b, s: (b, 0, s)),
            pl.BlockSpec((1, ts, D_ATTN), lambda b, s: (b, s, 0)),
        ],
        out_shape=[
            jax.ShapeDtypeStruct((bsz, seq, D_POOL), BF16),
            jax.ShapeDtypeStruct((bsz, seq, N_HEADS * AUG), BF16),
            jax.ShapeDtypeStruct((bsz, N_HEADS * AUG, seq), BF16),
            jax.ShapeDtypeStruct((bsz, seq, D_ATTN), BF16),
        ],
        scratch_shapes=[
            pltpu.VMEM((ts + MAX_WINDOW, D_POOL), F32),
            pltpu.VMEM((1, LANES), F32),
        ],
        compiler_params=pltpu.CompilerParams(
            dimension_semantics=("arbitrary", "arbitrary"), vmem_limit_bytes=VMEM_LIMIT),
    )(x, shift, scale, g, w1, wf, bfp, wpool, pscale, gpool, place, ones, fmask)


def _attn_kernel(q_ref, kt_ref, v_ref, o_ref):
    seq = q_ref.shape[1]
    qry_i = lax.broadcasted_iota(jnp.int32, (TQ, TK), 0)
    key_i = lax.broadcasted_iota(jnp.int32, (TQ, TK), 1)
    causal = key_i <= qry_i
    one_lane = lax.broadcasted_iota(jnp.int32, (seq, LANES), 1) == 0
    vaug = jnp.concatenate([v_ref[0], jnp.where(one_lane, 1.0, 0.0).astype(BF16)], axis=1)
    out_lane = lax.broadcasted_iota(jnp.int32, (TQ, LANES), 1)
    heads = range(HEAD_PAIR)
    for qi in range(seq // TQ):
        rows = slice(qi * TQ, (qi + 1) * TQ)
        q = [q_ref[0, rows, h * AUG:(h + 1) * AUG] for h in heads]
        m = [jnp.full((TQ, 1), -jnp.inf, F32) for _ in heads]
        acc = [jnp.zeros((TQ, 2 * LANES), F32) for _ in heads]
        for kj in range(qi + 1):
            keys = slice(kj * TK, (kj + 1) * TK)
            for h in heads:
                s = _dot(q[h], kt_ref[0, h * AUG:(h + 1) * AUG, keys])
                if kj == qi:
                    s = jnp.where(causal, s, -jnp.inf)
                m_new = jnp.maximum(m[h], jnp.max(s, axis=1, keepdims=True))
                p = jnp.exp2(s - m_new).astype(BF16)
                acc[h] = jnp.exp2(m[h] - m_new) * acc[h] + _dot(p, vaug[keys, :])
                m[h] = m_new
        o = [acc[h][:, :LANES] / acc[h][:, LANES:LANES + 1] for h in heads]
        o_ref[0, rows, :] = jnp.where(out_lane < HEAD_DIM, o[0], o[1])


def _attn(q, kt, v):
    bsz, seq, _ = q.shape
    return pl.pallas_call(
        _attn_kernel,
        name="attn",
        grid=(bsz, N_HEADS // HEAD_PAIR),
        in_specs=[
            pl.BlockSpec((1, seq, HEAD_PAIR * AUG), lambda b, h: (b, 0, h)),
            pl.BlockSpec((1, HEAD_PAIR * AUG, seq), lambda b, h: (b, h, 0)),
            pl.BlockSpec((1, seq, LANES), lambda b, h: (b, 0, h)),
        ],
        out_specs=pl.BlockSpec((1, seq, LANES), lambda b, h: (b, 0, h)),
        out_shape=jax.ShapeDtypeStruct((bsz, seq, D_ATTN), F32),
        compiler_params=pltpu.CompilerParams(
            dimension_semantics=("arbitrary", "arbitrary"), vmem_limit_bytes=VMEM_LIMIT),
    )(q, kt, v)


def _postmix_kernel(x_ref, pool_ref, o_ref, wout_ref, gattn_ref, gpost_ref, gate_ref,
                    gffn_ref, shift_ref, scale_ref, wr_ref,
                    x1_ref, h2_ref, st_ref):
    ya = _rms(o_ref[0], gattn_ref[...]).astype(BF16)
    mixed = _dot(pool_ref[0], wout_ref[:D_POOL, :]) + _dot(ya, wout_ref[D_POOL:, :])
    x1 = x_ref[0] + gate_ref[0] * _rms(mixed, gpost_ref[...])
    x1_ref[0] = x1
    h2 = _rms(x1, gffn_ref[...]) * (1.0 + scale_ref[0]) + shift_ref[0]
    h2b = h2.astype(BF16)
    h2_ref[0] = h2b
    h2l = (h2 - h2b.astype(F32)).astype(BF16)
    both = _dot(h2b, wr_ref[...])
    logits = both[:, :LANES] + both[:, LANES:] + _dot(h2l, wr_ref[:, :LANES])
    st_ref[...] = jax.nn.sigmoid(logits).T


def _postmix(x, pool, ot, wout, gattn, gpost, gate, gffn, shift, scale, wr):
    bsz, seq, _ = x.shape
    ts = TS_POST
    ns = seq // ts
    full = lambda shape: pl.BlockSpec(shape, lambda b, s: (0,) * len(shape))
    per_batch = pl.BlockSpec((1, 1, D_MODEL), lambda b, s: (b, 0, 0))
    tok = lambda width: pl.BlockSpec((1, ts, width), lambda b, s: (b, s, 0))
    return pl.pallas_call(
        _postmix_kernel,
        name="postmix",
        grid=(bsz, ns),
        in_specs=[
            tok(D_MODEL), tok(D_POOL), tok(D_ATTN),
            full((D_MODEL, D_MODEL)),
            full((1, D_ATTN)), full((1, D_MODEL)), per_batch,
            full((1, D_MODEL)), per_batch, per_batch,
            full((D_MODEL, 2 * LANES)),
        ],
        out_specs=[
            tok(D_MODEL), tok(D_MODEL),
            pl.BlockSpec((LANES, ts), lambda b, s: (0, b * ns + s)),
        ],
        out_shape=[
            jax.ShapeDtypeStruct((bsz, seq, D_MODEL), F32),
            jax.ShapeDtypeStruct((bsz, seq, D_MODEL), BF16),
            jax.ShapeDtypeStruct((LANES, bsz * seq), F32),
        ],
        compiler_params=pltpu.CompilerParams(
            dimension_semantics=("arbitrary", "arbitrary"), vmem_limit_bytes=VMEM_LIMIT),
    )(x, pool, ot, wout, gattn, gpost, gate, gffn, shift, scale, wr)


def _route_kernel(st_ref, bias_ref, infok_ref, cnt_ref):
    tile = pl.program_id(0)
    tr = st_ref.shape[1]
    scores = [st_ref[g * GROUP_SIZE:(g + 1) * GROUP_SIZE, :] for g in range(N_EXPERT_GROUPS)]
    sel = [scores[g] + bias_ref[g * GROUP_SIZE:(g + 1) * GROUP_SIZE, :]
           for g in range(N_EXPERT_GROUPS)]
    sub = lax.broadcasted_iota(jnp.int32, (GROUP_SIZE, tr), 0)
    neg = jnp.float32(-jnp.inf)

    rows = []
    for g in range(N_EXPERT_GROUPS):
        v = sel[g]
        m1 = jnp.max(v, axis=0, keepdims=True)
        first = jnp.min(jnp.where(v == m1, sub, GROUP_SIZE), axis=0, keepdims=True)
        m2 = jnp.max(jnp.where(sub == first, neg, v), axis=0, keepdims=True)
        rows.append(m1 + m2)
    gs = jnp.concatenate(rows, axis=0)

    shape = (GROUP_SIZE, tr)
    one = jnp.ones(shape, jnp.int32)
    zero = jnp.zeros(shape, jnp.int32)
    beaten = zero
    for gp in range(N_EXPERT_GROUPS):
        r = jnp.broadcast_to(gs[gp:gp + 1, :], shape)
        tie = jnp.where(sub > gp, one, zero)
        beaten = beaten + jnp.where(r > gs, one, zero) + jnp.where(r == gs, tie, zero)
    keep = jnp.where(beaten < TOPK_GROUPS, 1.0, 0.0)

    masked = [jnp.where(jnp.broadcast_to(keep[g:g + 1, :], shape) > 0.5, sel[g], neg)
              for g in range(N_EXPERT_GROUPS)]

    eidx = [sub + g * GROUP_SIZE for g in range(N_EXPERT_GROUPS)]
    picked = [zero for _ in range(N_EXPERT_GROUPS)]
    for _ in range(TOP_K):
        best = masked[0]
        for g in range(1, N_EXPERT_GROUPS):
            best = jnp.maximum(best, masked[g])
        best = jnp.broadcast_to(jnp.max(best, axis=0, keepdims=True), shape)
        first = jnp.where(masked[0] == best, eidx[0], N_EXPERTS)
        for g in range(1, N_EXPERT_GROUPS):
            first = jnp.minimum(first, jnp.where(masked[g] == best, eidx[g], N_EXPERTS))
        first = jnp.broadcast_to(jnp.min(first, axis=0, keepdims=True), shape)
        for g in range(N_EXPERT_GROUPS):
            hit = eidx[g] == first
            picked[g] = jnp.where(hit, one, picked[g])
            masked[g] = jnp.where(hit, neg, masked[g])
    chosen = [picked[g] > 0 for g in range(N_EXPERT_GROUPS)]

    w = [jnp.where(chosen[g], scores[g], 0.0) for g in range(N_EXPERT_GROUPS)]
    denom = w[0].sum(axis=0, keepdims=True)
    for g in range(1, N_EXPERT_GROUPS):
        denom = denom + w[g].sum(axis=0, keepdims=True)
    wfull = jnp.concatenate([w[g] / denom * ROUTED_SCALE for g in range(N_EXPERT_GROUPS)], axis=0)
    mfull = jnp.concatenate([jnp.where(chosen[g], 1.0, 0.0) for g in range(N_EXPERT_GROUPS)],
                            axis=0)
    mb = mfull.astype(BF16)

    e_r = lax.broadcasted_iota(jnp.int32, (N_EXPERTS, N_EXPERTS), 0)
    e_c = lax.broadcasted_iota(jnp.int32, (N_EXPERTS, N_EXPERTS), 1)
    before_e = (e_c < e_r).astype(BF16)
    t_r = lax.broadcasted_iota(jnp.int32, (tr, tr), 0)
    t_c = lax.broadcasted_iota(jnp.int32, (tr, tr), 1)
    before_t = (t_r < t_c).astype(BF16)
    ordinal = _dot(before_e, mb)
    rank = _dot(mb, before_t)
    n = jnp.sum(mfull, axis=1, keepdims=True)
    pieces = jnp.floor((n + (PIECE - 1.0)) * (1.0 / PIECE))
    run_start = PIECE * _dot(before_e, jnp.broadcast_to(pieces, (N_EXPERTS, LANES)).astype(BF16))
    pos = run_start[:, 0:1] + rank

    sub8 = lax.broadcasted_iota(jnp.int32, (TOP_K, tr), 0)
    pos8 = jnp.zeros((TOP_K, tr), F32)
    w8 = jnp.zeros((TOP_K, tr), F32)
    for k in range(TOP_K):
        selk = jnp.where(ordinal == float(k), mfull, 0.0)
        pk = jnp.sum(selk * pos, axis=0, keepdims=True)
        wk = jnp.sum(selk * wfull, axis=0, keepdims=True)
        pos8 = jnp.where(sub8 == k, jnp.broadcast_to(pk, (TOP_K, tr)), pos8)
        w8 = jnp.where(sub8 == k, jnp.broadcast_to(wk, (TOP_K, tr)), w8)
    infok_ref[...] = jnp.concatenate([pos8, w8], axis=0)

    @pl.when(tile == 0)
    def _():
        cnt_ref[...] = jnp.zeros_like(cnt_ref)

    lane = lax.broadcasted_iota(jnp.int32, cnt_ref.shape, 1)
    cnt_ref[...] = jnp.where(lane == tile, jnp.broadcast_to(n, cnt_ref.shape), cnt_ref[...])


def _route(st, bias_col):
    t = st.shape[1]
    assert t // TW <= LANES
    return pl.pallas_call(
        _route_kernel,
        name="route",
        grid=(t // TW,),
        in_specs=[
            pl.BlockSpec((LANES, TW), lambda i: (0, i)),
            pl.BlockSpec((N_EXPERTS, 1), lambda i: (0, 0)),
        ],
        out_specs=[
            pl.BlockSpec((2 * TOP_K, TW), lambda i: (0, i)),
            pl.BlockSpec((N_EXPERTS, LANES), lambda i: (0, 0)),
        ],
        out_shape=[
            jax.ShapeDtypeStruct((2 * TOP_K, t), F32),
            jax.ShapeDtypeStruct((N_EXPERTS, LANES), F32),
        ],
        compiler_params=pltpu.CompilerParams(dimension_semantics=("arbitrary",)),
    )(st, bias_col)


def _swiglu(xb, wg, wu):
    g = _dot(xb, wg)
    return (g * jax.nn.sigmoid(g)) * _dot(xb, wu)


def _piece_copy(src_ref, src_row, dst_ref, dst_row, sem, n_pieces=1):
    rows = n_pieces * PIECE
    src = src_ref.at[pl.ds(pl.multiple_of(src_row * PIECE, PIECE), rows), :]
    dst = dst_ref.at[pl.ds(pl.multiple_of(dst_row * PIECE, PIECE), rows), :]
    return pltpu.make_async_copy(src, dst, sem)


LOCAL_BITS = 8
MAX_DOUBLES = LROWS // (2 * PIECE)
MAX_SINGLES = N_EXPERTS


def _for_each_copy(doubles_ref, singles_ref, counts_ref, tile, fn):
    def unpack(word):
        return word & ((1 << LOCAL_BITS) - 1), lax.shift_right_logical(word, LOCAL_BITS)

    def double(i, carry):
        fn(*unpack(doubles_ref[tile * MAX_DOUBLES + i]), 2)
        return carry

    def single(i, carry):
        fn(*unpack(singles_ref[tile * MAX_SINGLES + i]), 1)
        return carry

    lax.fori_loop(0, counts_ref[2 * tile], double, 0)
    lax.fori_loop(0, counts_ref[2 * tile + 1], single, 0)


def _wait_pieces(total, copy_of):
    for bit in range((LROWS // PIECE).bit_length()):
        @pl.when((lax.shift_right_logical(total, bit) & 1) == 1)
        def _():
            copy_of(1 << bit).wait()


def _onehot_chunks(infok_ref, weighted):
    pos = infok_ref[0:TOP_K, :]
    grp = jnp.floor(pos * (1.0 / ONEHOT_GROUP))
    off = pos - ONEHOT_GROUP * grp
    sub = lax.broadcasted_iota(jnp.int32, (ONEHOT_GROUP, TW), 0).astype(F32)
    offs, grps = [], []
    for k in range(TOP_K):
        value = infok_ref[TOP_K + k:TOP_K + k + 1, :] if weighted else 1.0
        offs.append(jnp.where(sub == off[k:k + 1, :], value, 0.0).astype(BF16))
        grps.append(jnp.broadcast_to(grp[k:k + 1, :], (PIECE, TW)).astype(BF16))
    zero = jnp.zeros((PIECE, TW), BF16)

    def chunk(j):
        tiles = []
        for a in range(j * LCHUNK // ONEHOT_GROUP, (j + 1) * LCHUNK // ONEHOT_GROUP):
            hit = [grps[k] == a for k in range(TOP_K)]
            for i in range(ONEHOT_GROUP // PIECE):
                rows = zero
                for k in range(TOP_K):
                    rows = rows + jnp.where(hit[k], offs[k][i * PIECE:(i + 1) * PIECE, :], zero)
                tiles.append(rows)
        return jnp.concatenate(tiles, axis=0)

    return chunk


def _dispatch_kernel(doubles_ref, singles_ref, counts_ref, tail_ref, ntail_ref, nv_ref,
                     h2_ref, infok_ref, xs_ref, lbuf, zbuf, sem, zsem):
    tile = pl.program_id(0)
    last = pl.num_programs(0) - 1
    slot = tile % 2
    n_blocks = xs_ref.shape[0] // RB

    def spare_blocks(fn):
        def per_block(b, carry):
            fn(b)
            return carry
        lax.fori_loop(nv_ref[0], n_blocks, per_block, 0)

    def zero_block(b):
        dst = xs_ref.at[pl.ds(pl.multiple_of(b * RB, RB), RB), :]
        return pltpu.make_async_copy(zbuf, dst, zsem)

    def tails(fn):
        def per_expert(e, carry):
            n = ntail_ref[e]
            for bit in range((RB // PIECE - 1).bit_length()):
                @pl.when((lax.shift_right_logical(n, bit) & 1) == 1)
                def _():
                    done = n & ((1 << bit) - 1)
                    fn(_piece_copy(zbuf, 0, xs_ref, tail_ref[e] + done, zsem, 1 << bit))
            return carry
        lax.fori_loop(0, N_EXPERTS, per_expert, 0)

    def start(t, s):
        _for_each_copy(
            doubles_ref, singles_ref, counts_ref, t,
            lambda lp, gp, n: _piece_copy(lbuf.at[s], lp, xs_ref, gp, sem.at[s], n).start())

    def wait(t, s):
        _wait_pieces(2 * counts_ref[2 * t] + counts_ref[2 * t + 1],
                     lambda n: _piece_copy(lbuf.at[s], 0, xs_ref, 0, sem.at[s], n))

    @pl.when(tile == 0)
    def _():
        zbuf[...] = jnp.zeros_like(zbuf)
        tails(lambda copy: copy.start())
        spare_blocks(lambda b: zero_block(b).start())

    @pl.when(tile >= 2)
    def _():
        wait(tile - 2, slot)

    onehot = _onehot_chunks(infok_ref, weighted=False)
    xb = h2_ref[...]
    for j in range(LROWS // LCHUNK):
        lbuf[slot, j * LCHUNK:(j + 1) * LCHUNK, :] = _dot(onehot(j), xb).astype(BF16)

    start(tile, slot)

    @pl.when(tile == last)
    def _():
        @pl.when(tile >= 1)
        def _():
            wait(tile - 1, 1 - slot)
        wait(tile, slot)
        tails(lambda copy: copy.wait())
        spare_blocks(lambda b: zero_block(b).wait())


def _dispatch(tables, h2, infok, n_rows):
    t = h2.shape[0]
    return pl.pallas_call(
        _dispatch_kernel,
        name="dispatch",
        grid_spec=pltpu.PrefetchScalarGridSpec(
            num_scalar_prefetch=6,
            grid=(t // TW,),
            in_specs=[
                pl.BlockSpec((TW, D_MODEL), lambda i, *_: (i, 0)),
                pl.BlockSpec((2 * TOP_K, TW), lambda i, *_: (0, i)),
            ],
            out_specs=pl.BlockSpec(memory_space=pl.ANY),
            scratch_shapes=[
                pltpu.VMEM((2, LROWS, D_MODEL), BF16),
                pltpu.VMEM((RB, D_MODEL), BF16),
                pltpu.SemaphoreType.DMA((2,)),
                pltpu.SemaphoreType.DMA(()),
            ],
        ),
        out_shape=jax.ShapeDtypeStruct((n_rows, D_MODEL), BF16),
        compiler_params=pltpu.CompilerParams(
            dimension_semantics=("arbitrary",), vmem_limit_bytes=VMEM_LIMIT),
    )(*tables, h2, infok)


def _experts_kernel(be_ref, nv_ref, xs_ref, wg_ref, wu_ref, wd_ref, ys_ref, wgb, wub, wdb):
    i = pl.program_id(0)

    @pl.when(i < nv_ref[0])
    def _():
        @pl.when((i == 0) | (be_ref[i] != be_ref[jnp.maximum(i - 1, 0)]))
        def _():
            wgb[...] = wg_ref[0].astype(BF16)
            wub[...] = wu_ref[0].astype(BF16)
            wdb[...] = wd_ref[0].astype(BF16)

        for j in range(RB // RSUB):
            rows = slice(j * RSUB, (j + 1) * RSUB)
            h = _swiglu(xs_ref[rows, :], wgb[...], wub[...])
            ys_ref[rows, :] = _dot(h.astype(BF16), wdb[...]).astype(BF16)


def _experts(block_expert, n_valid, xs, wg, wu, wd):
    n_blocks = xs.shape[0] // RB
    rows = pl.BlockSpec((RB, D_MODEL), lambda i, be, nv: (jnp.minimum(i, nv[0] - 1), 0))
    return pl.pallas_call(
        _experts_kernel,
        name="experts",
        grid_spec=pltpu.PrefetchScalarGridSpec(
            num_scalar_prefetch=2,
            grid=(n_blocks,),
            in_specs=[
                rows,
                pl.BlockSpec((1, D_MODEL, D_EXPERT), lambda i, be, nv: (be[i], 0, 0)),
                pl.BlockSpec((1, D_MODEL, D_EXPERT), lambda i, be, nv: (be[i], 0, 0)),
                pl.BlockSpec((1, D_EXPERT, D_MODEL), lambda i, be, nv: (be[i], 0, 0)),
            ],
            out_specs=rows,
            scratch_shapes=[
                pltpu.VMEM((D_MODEL, D_EXPERT), BF16),
                pltpu.VMEM((D_MODEL, D_EXPERT), BF16),
                pltpu.VMEM((D_EXPERT, D_MODEL), BF16),
            ],
        ),
        out_shape=jax.ShapeDtypeStruct(xs.shape, BF16),
        input_output_aliases={2: 0},
        compiler_params=pltpu.CompilerParams(
            dimension_semantics=("arbitrary",), vmem_limit_bytes=VMEM_LIMIT),
    )(block_expert, n_valid, xs, wg, wu, wd)


def _combine_kernel(doubles_ref, singles_ref, counts_ref,
                    infok_ref, h2_ref, x1_ref, gate_ref, gpost_ref, wsg_ref, wsu_ref, wsd_ref,
                    ys_ref, o_ref, ybuf, sem):
    tile = pl.program_id(0)
    last = pl.num_programs(0) - 1
    slot = tile % 2

    def start(t, s):
        _for_each_copy(
            doubles_ref, singles_ref, counts_ref, t,
            lambda lp, gp, n: _piece_copy(ys_ref, gp, ybuf.at[s], lp, sem.at[s], n).start())

    def wait(t, s):
        _wait_pieces(2 * counts_ref[2 * t] + counts_ref[2 * t + 1],
                     lambda n: _piece_copy(ys_ref, 0, ybuf.at[s], 0, sem.at[s], n))

    @pl.when(tile == 0)
    def _():
        ybuf[...] = jnp.zeros_like(ybuf)
        start(tile, slot)

    @pl.when(tile < last)
    def _():
        start(tile + 1, 1 - slot)

    wait(tile, slot)

    weights = _onehot_chunks(infok_ref, weighted=True)
    hs = _swiglu(h2_ref[...], wsg_ref[...], wsu_ref[...])
    ff = _dot(hs.astype(BF16), wsd_ref[...])
    for j in range(LROWS // LCHUNK):
        ff = ff + lax.dot_general(weights(j), ybuf[slot, j * LCHUNK:(j + 1) * LCHUNK, :],
                                  (((0,), (0,)), ((), ())), preferred_element_type=F32)
    o_ref[...] = x1_ref[...] + gate_ref[0] * _rms(ff, gpost_ref[...])


def _combine(tables, infok, h2, x1, gate, gpost, wsg, wsu, wsd, ys, seq):
    t = h2.shape[0]
    per_seq = seq // TW
    full = lambda shape: pl.BlockSpec(shape, lambda i, *_: (0,) * len(shape))
    tok = lambda width: pl.BlockSpec((TW, width), lambda i, *_: (i, 0))
    return pl.pallas_call(
        _combine_kernel,
        name="combine",
        grid_spec=pltpu.PrefetchScalarGridSpec(
            num_scalar_prefetch=3,
            grid=(t // TW,),
            in_specs=[
                pl.BlockSpec((2 * TOP_K, TW), lambda i, *_: (0, i)), tok(D_MODEL), tok(D_MODEL),
                pl.BlockSpec((1, 1, D_MODEL), lambda i, *_: (i // per_seq, 0, 0)),
                full((1, D_MODEL)),
                full((D_MODEL, D_SHARED)), full((D_MODEL, D_SHARED)), full((D_SHARED, D_MODEL)),
                pl.BlockSpec(memory_space=pl.ANY),
            ],
            out_specs=tok(D_MODEL),
            scratch_shapes=[
                pltpu.VMEM((2, LROWS, D_MODEL), BF16),
                pltpu.SemaphoreType.DMA((2,)),
            ],
        ),
        out_shape=jax.ShapeDtypeStruct((t, D_MODEL), F32),
        compiler_params=pltpu.CompilerParams(
            dimension_semantics=("arbitrary",), vmem_limit_bytes=VMEM_LIMIT),
    )(*tables, infok, h2, x1, gate, gpost, wsg, wsu, wsd, ys)


def _dispatch_plan(cnt, n_tiles):
    n = cnt[:, :n_tiles].astype(jnp.int32)
    pieces = (n + PIECE - 1) // PIECE
    local = jnp.cumsum(pieces, axis=0) - pieces
    seg = jnp.sum(pieces, axis=1)
    per_block = RB // PIECE
    seg_pad = (seg + per_block - 1) // per_block * per_block
    seg_end = jnp.cumsum(seg_pad)
    seg_start = seg_end - seg_pad
    glob = seg_start[:, None] + jnp.cumsum(pieces, axis=1) - pieces
    n_blocks = (TOP_K * TW * n_tiles + N_EXPERTS * n_tiles * (PIECE - 1)
                + N_EXPERTS * (RB - PIECE) + RB - 1) // RB
    n_valid = seg_end[-1] // per_block
    blk = jnp.minimum(jnp.arange(n_blocks, dtype=jnp.int32), n_valid - 1)
    block_end = seg_end // per_block
    block_expert = jnp.sum((block_end[None, :] <= blk[:, None]).astype(jnp.int32), axis=1)
    block_expert = jnp.minimum(block_expert, N_EXPERTS - 1)
    assert LROWS // PIECE <= 1 << LOCAL_BITS and n_blocks * per_block < 1 << (31 - LOCAL_BITS)

    def copy_list(count, width):
        end = jnp.cumsum(count, axis=0).T
        slot = jnp.arange(width, dtype=jnp.int32)
        expert = jnp.sum((end[:, None, :] <= slot[None, :, None]).astype(jnp.int32), axis=-1)
        pick = expert[:, :, None] == jnp.arange(N_EXPERTS, dtype=jnp.int32)
        take = lambda a: jnp.sum(jnp.where(pick, a.T[:, None, :], 0), axis=-1)
        return slot[None, :] - take(jnp.cumsum(count, axis=0) - count), take

    doubles_n = pieces // 2
    odd = pieces % 2
    j2, take2 = copy_list(doubles_n, MAX_DOUBLES)
    doubles = (take2(local) + 2 * j2) | ((take2(glob) + 2 * j2) << LOCAL_BITS)
    _, take1 = copy_list(odd, MAX_SINGLES)
    last_piece = pieces - 1
    singles = take1(local + last_piece) | (take1(glob + last_piece) << LOCAL_BITS)
    counts = jnp.stack([jnp.sum(doubles_n, axis=0), jnp.sum(odd, axis=0)], axis=1)
    tables = (doubles.reshape(-1).astype(jnp.int32), singles.reshape(-1).astype(jnp.int32),
              counts.reshape(-1).astype(jnp.int32))
    tails = ((seg_start + seg).astype(jnp.int32), (seg_pad - seg).astype(jnp.int32))
    return tables, tails, block_expert, n_valid.reshape(1).astype(jnp.int32), n_blocks * RB


def kernel(x, c, w_ada, b_ada, g_pre_mix, g_post_mix, g_pre_ffn, g_post_ffn, w_in, b_forget,
           w_pool, pool_scale, g_pool_out, g_attn_out, w_out, w_router, router_bias,
           w_gate, w_up, w_down, ws_gate, ws_up, ws_down):
    bsz, seq, d = x.shape
    depth = w_ada.shape[0]
    for l in range(depth):
        mod = _ada(c, w_ada, b_ada[l][None, :], l)
        shift_m, scale_m, gate_m, shift_f, scale_f, gate_f = [
            m.reshape(bsz, 1, d) for m in jnp.split(mod, 6, axis=-1)]

        wi = w_in[l]
        qs = HEAD_DIM ** -0.5 * LOG2E
        w1 = jnp.concatenate(
            [wi[:, :D_POOL], wi[:, D_POOL:D_POOL + D_ATTN] * qs,
             wi[:, D_POOL + D_ATTN:D_POOL + 3 * D_ATTN]], axis=1).astype(BF16)
        pad = LANES - N_SPLIT * N_HEADS
        wf = jnp.pad(jnp.tile(wi[:, D_POOL + 3 * D_ATTN:], (1, N_SPLIT)),
                     ((0, 0), (0, pad))).astype(BF16)
        bfp = jnp.pad(jnp.tile(b_forget[l], N_SPLIT), (0, pad))[None, :]

        pool, qa, kt, v = _premix(
            x, shift_m, scale_m, g_pre_mix[l][None, :], w1, wf, bfp, w_pool[l].astype(BF16),
            pool_scale[l][None, :], g_pool_out[l][None, :])
        ot = _attn(qa, kt, v)

        wr = jnp.pad(w_router[l], ((0, 0), (0, LANES - N_EXPERTS)))
        wrh = wr.astype(BF16)
        wrl = (wr - wrh.astype(F32)).astype(BF16)
        x1, h2, st = _postmix(
            x, pool, ot, w_out[l].astype(BF16), g_attn_out[l][None, :], g_post_mix[l][None, :],
            gate_m, g_pre_ffn[l][None, :], shift_f, scale_f,
            jnp.concatenate([wrh, wrl], axis=1))

        t = bsz * seq
        infok, cnt = _route(st, router_bias[l][:, None])
        tables, tails, block_expert, n_valid, n_rows = _dispatch_plan(cnt, t // TW)
        h2f = h2.reshape(t, d)
        xs = _dispatch(tables + tails + (n_valid,), h2f, infok, n_rows)
        ys = _experts(block_expert, n_valid, xs, w_gate[l], w_up[l], w_down[l])
        out = _combine(tables, infok, h2f, x1.reshape(t, d), gate_f, g_post_ffn[l][None, :],
                       ws_gate[l].astype(BF16), ws_up[l].astype(BF16), ws_down[l].astype(BF16),
                       ys, seq)
        x = out.reshape(bsz, seq, d)
    return x
```

```python
import functools

import numpy as np
import jax
import jax.numpy as jnp
from jax import lax
from jax.experimental import pallas as pl
from jax.experimental.pallas import tpu as pltpu

D_MODEL = 1024
D_POOL = 512
POOL_WINDOWS = (2, 4, 8, 16)
POOL_GROUP = 128
MAX_WINDOW = max(POOL_WINDOWS)
D_ATTN = 512
HEAD_DIM = 64
N_HEADS = 8
N_EXPERTS = 64
N_EXPERT_GROUPS = 8
GROUP_SIZE = N_EXPERTS // N_EXPERT_GROUPS
TOPK_GROUPS = 4
TOP_K = 8
D_EXPERT = 256
D_SHARED = 256
ROUTED_SCALE = 2.5
EPS = 1e-6

LOG2E = 1.4426950408889634
LANES = 128
N_SPLIT = 3
AUG = LANES

TS_PRE = 512
TS_POST = 512
TQ = 512
TK = 512
HEAD_PAIR = 128 // HEAD_DIM
TW = 256
PIECE = 16
RB = 1024
RSUB = 512
LCHUNK = 512
ONEHOT_GROUP = 64
LROWS = -(-(TOP_K * TW + N_EXPERTS * (PIECE - 1)) // LCHUNK) * LCHUNK

F32 = jnp.float32
BF16 = jnp.bfloat16
VMEM_LIMIT = 56 * 1024 * 1024


def _rms(v, g):
    return v * lax.rsqrt(jnp.mean(v * v, axis=-1, keepdims=True) + EPS) * g


def _split3(v):
    hi = v.astype(BF16)
    r1 = v - hi.astype(F32)
    mid = r1.astype(BF16)
    r2 = r1 - mid.astype(F32)
    lo = r2.astype(BF16)
    return hi, mid, lo


def _dot(a, b):
    return jnp.dot(a, b, preferred_element_type=F32)


def _ada_kernel(c_ref, w_ref, b_ref, o_ref):
    o_ref[...] = _dot(c_ref[...].astype(BF16), w_ref[0].astype(BF16)) + b_ref[...]


def _ada(c, w, b, layer):
    bsz = c.shape[0]
    n = w.shape[2]
    return pl.pallas_call(
        _ada_kernel,
        name="ada",
        grid=(n // D_MODEL,),
        in_specs=[
            pl.BlockSpec((bsz, D_MODEL), lambda j: (0, 0)),
            pl.BlockSpec((1, D_MODEL, D_MODEL), lambda j: (layer, 0, j)),
            pl.BlockSpec((1, D_MODEL), lambda j: (0, j)),
        ],
        out_specs=pl.BlockSpec((bsz, D_MODEL), lambda j: (0, j)),
        out_shape=jax.ShapeDtypeStruct((bsz, n), F32),
    )(c, w, b)


def _premix_kernel(x_ref, shift_ref, scale_ref, g_ref, w1_ref, wf_ref, bf_ref, wpool_ref,
                   pscale_ref, gpool_ref, place_ref, ones_ref, fmask_ref,
                   pool_ref, q_ref, kt_ref, v_ref,
                   uext_ref, cum_ref):
    s = pl.program_id(1)
    ts = x_ref.shape[1]

    @pl.when(s == 0)
    def _():
        uext_ref[0:MAX_WINDOW, :] = jnp.zeros((MAX_WINDOW, D_POOL), F32)
        cum_ref[...] = jnp.zeros_like(cum_ref)

    x = x_ref[0]
    h = _rms(x, g_ref[...]) * (1.0 + scale_ref[0]) + shift_ref[0]
    hb = h.astype(BF16)
    proj = _dot(hb, w1_ref[...])
    u = proj[:, :D_POOL]
    q = proj[:, D_POOL:D_POOL + D_ATTN]
    k = proj[:, D_POOL + D_ATTN:D_POOL + 2 * D_ATTN]
    v = proj[:, D_POOL + 2 * D_ATTN:]

    uext_ref[MAX_WINDOW:, :] = u
    pos = (s * ts + lax.broadcasted_iota(jnp.int32, (ts, 1), 0) + 1).astype(F32)
    ys = []
    for g, w in enumerate(POOL_WINDOWS):
        c0 = g * POOL_GROUP
        acc = uext_ref[MAX_WINDOW:, c0:c0 + POOL_GROUP]
        for j in range(1, w):
            acc = acc + uext_ref[MAX_WINDOW - j:MAX_WINDOW - j + ts, c0:c0 + POOL_GROUP]
        pooled = acc / jnp.minimum(pos, float(w)) - u[:, c0:c0 + POOL_GROUP]
        ys.append(_dot(pooled.astype(BF16), wpool_ref[g]))
    ypool = jnp.concatenate(ys, axis=1) * pscale_ref[...]
    pool_ref[0] = _rms(ypool, gpool_ref[...]).astype(BF16)
    uext_ref[0:MAX_WINDOW, :] = uext_ref[ts:ts + MAX_WINDOW, :]

    z = _dot(hb, wf_ref[...]) + bf_ref[...]
    logf = jnp.minimum(z, 0.0) - jnp.log1p(jnp.exp(-jnp.abs(z)))
    row = lax.broadcasted_iota(jnp.int32, (ts, ts), 0)
    col = lax.broadcasted_iota(jnp.int32, (ts, ts), 1)
    tri = (col <= row).astype(BF16)
    cum = cum_ref[...]
    for piece in _split3(logf):
        cum = cum + _dot(tri, piece)
    cum_ref[...] = cum[ts - 1:ts, :]

    hi, mid, lo = [piece.astype(F32) for piece in _split3(cum * LOG2E)]
    lane = lax.broadcasted_iota(jnp.int32, (ts, LANES), 1)
    pieces = jnp.where(lane < N_HEADS, hi, jnp.where(lane < 2 * N_HEADS, mid, lo))
    aug = _dot(pieces.astype(BF16), place_ref[...]) + ones_ref[...]
    aug_q = aug[:, :N_HEADS * AUG]
    aug_k = aug[:, N_HEADS * AUG:]

    def expand(a):
        blocks = []
        for j in range(D_ATTN // LANES):
            blk = a[:, j * LANES:(j + 1) * LANES]
            blocks += [blk, blk]
        return jnp.concatenate(blocks, axis=1)

    fmask = fmask_ref[...]
    qa = expand(q) * fmask + aug_q
    ka = expand(k) * fmask + aug_k
    q_ref[0] = qa.astype(BF16)
    kt_ref[0] = ka.T.astype(BF16)
    v_ref[0] = v.astype(BF16)


def _aug_constants():
    width = N_HEADS * AUG
    place = np.zeros((LANES, 2 * width), np.float32)
    ones = np.zeros((1, 2 * width), np.float32)
    fmask = np.zeros((1, width), np.float32)
    for h in range(N_HEADS):
        feat0 = h * AUG + (0 if h % 2 == 0 else HEAD_DIM)
        aug0 = h * AUG + (HEAD_DIM if h % 2 == 0 else 0)
        fmask[0, feat0:feat0 + HEAD_DIM] = 1.0
        for p in range(N_SPLIT):
            place[p * N_HEADS + h, aug0 + p] = 1.0
            ones[0, width + aug0 + p] = 1.0
            place[p * N_HEADS + h, width + aug0 + N_SPLIT + p] = -1.0
            ones[0, aug0 + N_SPLIT + p] = 1.0
    return jnp.asarray(place, BF16), jnp.asarray(ones), jnp.asarray(fmask)


def _premix(x, shift, scale, g, w1, wf, bfp, wpool, pscale, gpool):
    bsz, seq, _ = x.shape
    ts = TS_PRE
    place, ones, fmask = _aug_constants()
    full = lambda shape: pl.BlockSpec(shape, lambda b, s: (0,) * len(shape))
    per_batch = pl.BlockSpec((1, 1, D_MODEL), lambda b, s: (b, 0, 0))
    return pl.pallas_call(
        _premix_kernel,
        name="premix",
        grid=(bsz, seq // ts),
        in_specs=[
            pl.BlockSpec((1, ts, D_MODEL), lambda b, s: (b, s, 0)),
            per_batch, per_batch,
            full((1, D_MODEL)),
            full((D_MODEL, D_POOL + 3 * D_ATTN)),
            full((D_MODEL, LANES)),
            full((1, LANES)),
            full((len(POOL_WINDOWS), POOL_GROUP, POOL_GROUP)),
            full((1, D_POOL)),
            full((1, D_POOL)),
            full((LANES, 2 * N_HEADS * AUG)),
            full((1, 2 * N_HEADS * AUG)),
            full((1, N_HEADS * AUG)),
        ],
        out_specs=[
            pl.BlockSpec((1, ts, D_POOL), lambda b, s: (b, s, 0)),
            pl.BlockSpec((1, ts, N_HEADS * AUG), lambda b, s: (b, s, 0)),
            pl.BlockSpec((1, N_HEADS * AUG, ts), lambda b, s: (b, 0, s)),
            pl.BlockSpec((1, ts, D_ATTN), lambda b, s: (b, s, 0)),
        ],
        out_shape=[
            jax.ShapeDtypeStruct((bsz, seq, D_POOL), BF16),
            jax.ShapeDtypeStruct((bsz, seq, N_HEADS * AUG), BF16),
            jax.ShapeDtypeStruct((bsz, N_HEADS * AUG, seq), BF16),
            jax.ShapeDtypeStruct((bsz, seq, D_ATTN), BF16),
        ],
        scratch_shapes=[
            pltpu.VMEM((ts + MAX_WINDOW, D_POOL), F32),
            pltpu.VMEM((1, LANES), F32),
        ],
        compiler_params=pltpu.CompilerParams(
            dimension_semantics=("arbitrary", "arbitrary"), vmem_limit_bytes=VMEM_LIMIT),
    )(x, shift, scale, g, w1, wf, bfp, wpool, pscale, gpool, place, ones, fmask)


def _attn_kernel(q_ref, kt_ref, v_ref, o_ref):
    seq = q_ref.shape[1]
    qry_i = lax.broadcasted_iota(jnp.int32, (TQ, TK), 0)
    key_i = lax.broadcasted_iota(jnp.int32, (TQ, TK), 1)
    causal = key_i <= qry_i
    one_lane = lax.broadcasted_iota(jnp.int32, (seq, LANES), 1) == 0
    vaug = jnp.concatenate([v_ref[0], jnp.where(one_lane, 1.0, 0.0).astype(BF16)], axis=1)
    out_lane = lax.broadcasted_iota(jnp.int32, (TQ, LANES), 1)
    heads = range(HEAD_PAIR)
    for qi in range(seq // TQ):
        rows = slice(qi * TQ, (qi + 1) * TQ)
        q = [q_ref[0, rows, h * AUG:(h + 1) * AUG] for h in heads]
        m = [jnp.full((TQ, 1), -jnp.inf, F32) for _ in heads]
        acc = [jnp.zeros((TQ, 2 * LANES), F32) for _ in heads]
        for kj in range(qi + 1):
            keys = slice(kj * TK, (kj + 1) * TK)
            for h in heads:
                s = _dot(q[h], kt_ref[0, h * AUG:(h + 1) * AUG, keys])
                if kj == qi:
                    s = jnp.where(causal, s, -jnp.inf)
                m_new = jnp.maximum(m[h], jnp.max(s, axis=1, keepdims=True))
                p = jnp.exp2(s - m_new).astype(BF16)
                acc[h] = jnp.exp2(m[h] - m_new) * acc[h] + _dot(p, vaug[keys, :])
                m[h] = m_new
        o = [acc[h][:, :LANES] / acc[h][:, LANES:LANES + 1] for h in heads]
        o_ref[0, rows, :] = jnp.where(out_lane < HEAD_DIM, o[0], o[1])


def _attn(q, kt, v):
    bsz, seq, _ = q.shape
    return pl.pallas_call(
        _attn_kernel,
        name="attn",
        grid=(bsz, N_HEADS // HEAD_PAIR),
        in_specs=[
            pl.BlockSpec((1, seq, HEAD_PAIR * AUG), lambda b, h: (b, 0, h)),
            pl.BlockSpec((1, HEAD_PAIR * AUG, seq), lambda b, h: (b, h, 0)),
            pl.BlockSpec((1, seq, LANES), lambda b, h: (b, 0, h)),
        ],
        out_specs=pl.BlockSpec((1, seq, LANES), lambda b, h: (b, 0, h)),
        out_shape=jax.ShapeDtypeStruct((bsz, seq, D_ATTN), F32),
        compiler_params=pltpu.CompilerParams(
            dimension_semantics=("arbitrary", "arbitrary"), vmem_limit_bytes=VMEM_LIMIT),
    )(q, kt, v)


def _postmix_kernel(x_ref, pool_ref, o_ref, wout_ref, gattn_ref, gpost_ref, gate_ref,
                    gffn_ref, shift_ref, scale_ref, wr_ref, bias_ref,
                    x1_ref, h2_ref, infok_ref, cnt_ref):
    ya = _rms(o_ref[0], gattn_ref[...]).astype(BF16)
    mixed = _dot(pool_ref[0], wout_ref[:D_POOL, :]) + _dot(ya, wout_ref[D_POOL:, :])
    x1 = x_ref[0] + gate_ref[0] * _rms(mixed, gpost_ref[...])
    x1_ref[0] = x1
    h2 = _rms(x1, gffn_ref[...]) * (1.0 + scale_ref[0]) + shift_ref[0]
    h2b = h2.astype(BF16)
    h2_ref[0] = h2b
    h2l = (h2 - h2b.astype(F32)).astype(BF16)
    both = _dot(h2b, wr_ref[...])
    logits = both[:, :LANES] + both[:, LANES:] + _dot(h2l, wr_ref[:, :LANES])
    st = jax.nn.sigmoid(logits).T

    step = pl.program_id(0) * pl.num_programs(1) + pl.program_id(1)

    @pl.when(step == 0)
    def _():
        cnt_ref[...] = jnp.zeros_like(cnt_ref)

    lane = lax.broadcasted_iota(jnp.int32, cnt_ref.shape, 1)
    tiles = x_ref.shape[1] // TW
    for i in range(tiles):
        info, n = _route_tile(st[:, i * TW:(i + 1) * TW], bias_ref)
        infok_ref[:, i * TW:(i + 1) * TW] = info
        cnt_ref[...] = jnp.where(lane == step * tiles + i, jnp.broadcast_to(n, cnt_ref.shape),
                                 cnt_ref[...])


def _postmix(x, pool, ot, wout, gattn, gpost, gate, gffn, shift, scale, wr, bias_col):
    bsz, seq, _ = x.shape
    ts = TS_POST
    ns = seq // ts
    full = lambda shape: pl.BlockSpec(shape, lambda b, s: (0,) * len(shape))
    per_batch = pl.BlockSpec((1, 1, D_MODEL), lambda b, s: (b, 0, 0))
    tok = lambda width: pl.BlockSpec((1, ts, width), lambda b, s: (b, s, 0))
    return pl.pallas_call(
        _postmix_kernel,
        name="postmix",
        grid=(bsz, ns),
        in_specs=[
            tok(D_MODEL), tok(D_POOL), tok(D_ATTN),
            full((D_MODEL, D_MODEL)),
            full((1, D_ATTN)), full((1, D_MODEL)), per_batch,
            full((1, D_MODEL)), per_batch, per_batch,
            full((D_MODEL, 2 * LANES)),
            full((N_EXPERTS, 1)),
        ],
        out_specs=[
            tok(D_MODEL), tok(D_MODEL),
            pl.BlockSpec((2 * TOP_K, ts), lambda b, s: (0, b * ns + s)),
            full((N_EXPERTS, LANES)),
        ],
        out_shape=[
            jax.ShapeDtypeStruct((bsz, seq, D_MODEL), F32),
            jax.ShapeDtypeStruct((bsz, seq, D_MODEL), BF16),
            jax.ShapeDtypeStruct((2 * TOP_K, bsz * seq), F32),
            jax.ShapeDtypeStruct((N_EXPERTS, LANES), F32),
        ],
        compiler_params=pltpu.CompilerParams(
            dimension_semantics=("arbitrary", "arbitrary"), vmem_limit_bytes=VMEM_LIMIT),
    )(x, pool, ot, wout, gattn, gpost, gate, gffn, shift, scale, wr, bias_col)


def _route_tile(st, bias_ref):
    tr = st.shape[1]
    scores = [st[g * GROUP_SIZE:(g + 1) * GROUP_SIZE, :] for g in range(N_EXPERT_GROUPS)]
    sel = [scores[g] + bias_ref[g * GROUP_SIZE:(g + 1) * GROUP_SIZE, :]
           for g in range(N_EXPERT_GROUPS)]
    sub = lax.broadcasted_iota(jnp.int32, (GROUP_SIZE, tr), 0)
    neg = jnp.float32(-jnp.inf)

    rows = []
    for g in range(N_EXPERT_GROUPS):
        v = sel[g]
        m1 = jnp.max(v, axis=0, keepdims=True)
        first = jnp.min(jnp.where(v == m1, sub, GROUP_SIZE), axis=0, keepdims=True)
        m2 = jnp.max(jnp.where(sub == first, neg, v), axis=0, keepdims=True)
        rows.append(m1 + m2)
    gs = jnp.concatenate(rows, axis=0)

    shape = (GROUP_SIZE, tr)
    one = jnp.ones(shape, jnp.int32)
    zero = jnp.zeros(shape, jnp.int32)
    beaten = zero
    for gp in range(N_EXPERT_GROUPS):
        r = jnp.broadcast_to(gs[gp:gp + 1, :], shape)
        tie = jnp.where(sub > gp, one, zero)
        beaten = beaten + jnp.where(r > gs, one, zero) + jnp.where(r == gs, tie, zero)
    keep = jnp.where(beaten < TOPK_GROUPS, 1.0, 0.0)

    masked = [jnp.where(jnp.broadcast_to(keep[g:g + 1, :], shape) > 0.5, sel[g], neg)
              for g in range(N_EXPERT_GROUPS)]

    eidx = [sub + g * GROUP_SIZE for g in range(N_EXPERT_GROUPS)]
    picked = [zero for _ in range(N_EXPERT_GROUPS)]
    for _ in range(TOP_K):
        best = masked[0]
        for g in range(1, N_EXPERT_GROUPS):
            best = jnp.maximum(best, masked[g])
        best = jnp.broadcast_to(jnp.max(best, axis=0, keepdims=True), shape)
        first = jnp.where(masked[0] == best, eidx[0], N_EXPERTS)
        for g in range(1, N_EXPERT_GROUPS):
            first = jnp.minimum(first, jnp.where(masked[g] == best, eidx[g], N_EXPERTS))
        first = jnp.broadcast_to(jnp.min(first, axis=0, keepdims=True), shape)
        for g in range(N_EXPERT_GROUPS):
            hit = eidx[g] == first
            picked[g] = jnp.where(hit, one, picked[g])
            masked[g] = jnp.where(hit, neg, masked[g])
    chosen = [picked[g] > 0 for g in range(N_EXPERT_GROUPS)]

    w = [jnp.where(chosen[g], scores[g], 0.0) for g in range(N_EXPERT_GROUPS)]
    denom = w[0].sum(axis=0, keepdims=True)
    for g in range(1, N_EXPERT_GROUPS):
        denom = denom + w[g].sum(axis=0, keepdims=True)
    wfull = jnp.concatenate([w[g] / denom * ROUTED_SCALE for g in range(N_EXPERT_GROUPS)], axis=0)
    mfull = jnp.concatenate([jnp.where(chosen[g], 1.0, 0.0) for g in range(N_EXPERT_GROUPS)],
                            axis=0)
    mb = mfull.astype(BF16)

    e_r = lax.broadcasted_iota(jnp.int32, (N_EXPERTS, N_EXPERTS), 0)
    e_c = lax.broadcasted_iota(jnp.int32, (N_EXPERTS, N_EXPERTS), 1)
    before_e = (e_c < e_r).astype(BF16)
    t_r = lax.broadcasted_iota(jnp.int32, (tr, tr), 0)
    t_c = lax.broadcasted_iota(jnp.int32, (tr, tr), 1)
    before_t = (t_r < t_c).astype(BF16)
    ordinal = _dot(before_e, mb)
    rank = _dot(mb, before_t)
    n = jnp.sum(mfull, axis=1, keepdims=True)
    pieces = jnp.floor((n + (PIECE - 1.0)) * (1.0 / PIECE))
    run_start = PIECE * _dot(before_e, jnp.broadcast_to(pieces, (N_EXPERTS, LANES)).astype(BF16))
    pos = run_start[:, 0:1] + rank

    sub8 = lax.broadcasted_iota(jnp.int32, (TOP_K, tr), 0)
    pos8 = jnp.zeros((TOP_K, tr), F32)
    w8 = jnp.zeros((TOP_K, tr), F32)
    for k in range(TOP_K):
        selk = jnp.where(ordinal == float(k), mfull, 0.0)
        pk = jnp.sum(selk * pos, axis=0, keepdims=True)
        wk = jnp.sum(selk * wfull, axis=0, keepdims=True)
        pos8 = jnp.where(sub8 == k, jnp.broadcast_to(pk, (TOP_K, tr)), pos8)
        w8 = jnp.where(sub8 == k, jnp.broadcast_to(wk, (TOP_K, tr)), w8)
    return jnp.concatenate([pos8, w8], axis=0), n


def _swiglu(xb, wg, wu):
    g = _dot(xb, wg)
    return (g * jax.nn.sigmoid(g)) * _dot(xb, wu)


def _piece_copy(src_ref, src_row, dst_ref, dst_row, sem, n_pieces=1):
    rows = n_pieces * PIECE
    src = src_ref.at[pl.ds(pl.multiple_of(src_row * PIECE, PIECE), rows), :]
    dst = dst_ref.at[pl.ds(pl.multiple_of(dst_row * PIECE, PIECE), rows), :]
    return pltpu.make_async_copy(src, dst, sem)


LOCAL_BITS = 8
MAX_DOUBLES = LROWS // (2 * PIECE)
MAX_SINGLES = N_EXPERTS


def _for_each_copy(doubles_ref, singles_ref, counts_ref, tile, fn):
    def unpack(word):
        return word & ((1 << LOCAL_BITS) - 1), lax.shift_right_logical(word, LOCAL_BITS)

    def double(i, carry):
        fn(*unpack(doubles_ref[tile * MAX_DOUBLES + i]), 2)
        return carry

    def single(i, carry):
        fn(*unpack(singles_ref[tile * MAX_SINGLES + i]), 1)
        return carry

    lax.fori_loop(0, counts_ref[2 * tile], double, 0)
    lax.fori_loop(0, counts_ref[2 * tile + 1], single, 0)


def _wait_pieces(total, copy_of):
    for bit in range((LROWS // PIECE).bit_length()):
        @pl.when((lax.shift_right_logical(total, bit) & 1) == 1)
        def _():
            copy_of(1 << bit).wait()


def _onehot_chunks(infok_ref, weighted):
    pos = infok_ref[0:TOP_K, :]
    grp = jnp.floor(pos * (1.0 / ONEHOT_GROUP))
    off = pos - ONEHOT_GROUP * grp
    sub = lax.broadcasted_iota(jnp.int32, (ONEHOT_GROUP, TW), 0).astype(F32)
    offs, grps = [], []
    for k in range(TOP_K):
        value = infok_ref[TOP_K + k:TOP_K + k + 1, :] if weighted else 1.0
        offs.append(jnp.where(sub == off[k:k + 1, :], value, 0.0).astype(BF16))
        grps.append(jnp.broadcast_to(grp[k:k + 1, :], (PIECE, TW)).astype(BF16))
    zero = jnp.zeros((PIECE, TW), BF16)

    def chunk(j):
        tiles = []
        for a in range(j * LCHUNK // ONEHOT_GROUP, (j + 1) * LCHUNK // ONEHOT_GROUP):
            hit = [grps[k] == a for k in range(TOP_K)]
            for i in range(ONEHOT_GROUP // PIECE):
                rows = zero
                for k in range(TOP_K):
                    rows = rows + jnp.where(hit[k], offs[k][i * PIECE:(i + 1) * PIECE, :], zero)
                tiles.append(rows)
        return jnp.concatenate(tiles, axis=0)

    return chunk


def _dispatch_kernel(doubles_ref, singles_ref, counts_ref, tail_ref, ntail_ref, nv_ref,
                     h2_ref, infok_ref, xs_ref, lbuf, zbuf, sem, zsem):
    tile = pl.program_id(0)
    last = pl.num_programs(0) - 1
    slot = tile % 2
    n_blocks = xs_ref.shape[0] // RB

    def spare_blocks(fn):
        def per_block(b, carry):
            fn(b)
            return carry
        lax.fori_loop(nv_ref[0], n_blocks, per_block, 0)

    def zero_block(b):
        dst = xs_ref.at[pl.ds(pl.multiple_of(b * RB, RB), RB), :]
        return pltpu.make_async_copy(zbuf, dst, zsem)

    def tails(fn):
        def per_expert(e, carry):
            n = ntail_ref[e]
            for bit in range((RB // PIECE - 1).bit_length()):
                @pl.when((lax.shift_right_logical(n, bit) & 1) == 1)
                def _():
                    done = n & ((1 << bit) - 1)
                    fn(_piece_copy(zbuf, 0, xs_ref, tail_ref[e] + done, zsem, 1 << bit))
            return carry
        lax.fori_loop(0, N_EXPERTS, per_expert, 0)

    def start(t, s):
        _for_each_copy(
            doubles_ref, singles_ref, counts_ref, t,
            lambda lp, gp, n: _piece_copy(lbuf.at[s], lp, xs_ref, gp, sem.at[s], n).start())

    def wait(t, s):
        _wait_pieces(2 * counts_ref[2 * t] + counts_ref[2 * t + 1],
                     lambda n: _piece_copy(lbuf.at[s], 0, xs_ref, 0, sem.at[s], n))

    @pl.when(tile == 0)
    def _():
        zbuf[...] = jnp.zeros_like(zbuf)
        tails(lambda copy: copy.start())
        spare_blocks(lambda b: zero_block(b).start())

    @pl.when(tile >= 2)
    def _():
        wait(tile - 2, slot)

    onehot = _onehot_chunks(infok_ref, weighted=False)
    xb = h2_ref[...]
    for j in range(LROWS // LCHUNK):
        lbuf[slot, j * LCHUNK:(j + 1) * LCHUNK, :] = _dot(onehot(j), xb).astype(BF16)

    start(tile, slot)

    @pl.when(tile == last)
    def _():
        @pl.when(tile >= 1)
        def _():
            wait(tile - 1, 1 - slot)
        wait(tile, slot)
        tails(lambda copy: copy.wait())
        spare_blocks(lambda b: zero_block(b).wait())


def _dispatch(tables, h2, infok, n_rows):
    t = h2.shape[0]
    return pl.pallas_call(
        _dispatch_kernel,
        name="dispatch",
        grid_spec=pltpu.PrefetchScalarGridSpec(
            num_scalar_prefetch=6,
            grid=(t // TW,),
            in_specs=[
                pl.BlockSpec((TW, D_MODEL), lambda i, *_: (i, 0)),
                pl.BlockSpec((2 * TOP_K, TW), lambda i, *_: (0, i)),
            ],
            out_specs=pl.BlockSpec(memory_space=pl.ANY),
            scratch_shapes=[
                pltpu.VMEM((2, LROWS, D_MODEL), BF16),
                pltpu.VMEM((RB, D_MODEL), BF16),
                pltpu.SemaphoreType.DMA((2,)),
                pltpu.SemaphoreType.DMA(()),
            ],
        ),
        out_shape=jax.ShapeDtypeStruct((n_rows, D_MODEL), BF16),
        compiler_params=pltpu.CompilerParams(
            dimension_semantics=("arbitrary",), vmem_limit_bytes=VMEM_LIMIT),
    )(*tables, h2, infok)


def _experts_kernel(be_ref, nv_ref, xs_ref, wg_ref, wu_ref, wd_ref, ys_ref, wgb, wub, wdb):
    i = pl.program_id(0)

    @pl.when(i < nv_ref[0])
    def _():
        @pl.when((i == 0) | (be_ref[i] != be_ref[jnp.maximum(i - 1, 0)]))
        def _():
            wgb[...] = wg_ref[0].astype(BF16)
            wub[...] = wu_ref[0].astype(BF16)
            wdb[...] = wd_ref[0].astype(BF16)

        for j in range(RB // RSUB):
            rows = slice(j * RSUB, (j + 1) * RSUB)
            h = _swiglu(xs_ref[rows, :], wgb[...], wub[...])
            ys_ref[rows, :] = _dot(h.astype(BF16), wdb[...]).astype(BF16)


def _experts(block_expert, n_valid, xs, wg, wu, wd):
    n_blocks = xs.shape[0] // RB
    rows = pl.BlockSpec((RB, D_MODEL), lambda i, be, nv: (jnp.minimum(i, nv[0] - 1), 0))
    return pl.pallas_call(
        _experts_kernel,
        name="experts",
        grid_spec=pltpu.PrefetchScalarGridSpec(
            num_scalar_prefetch=2,
            grid=(n_blocks,),
            in_specs=[
                rows,
                pl.BlockSpec((1, D_MODEL, D_EXPERT), lambda i, be, nv: (be[i], 0, 0)),
                pl.BlockSpec((1, D_MODEL, D_EXPERT), lambda i, be, nv: (be[i], 0, 0)),
                pl.BlockSpec((1, D_EXPERT, D_MODEL), lambda i, be, nv: (be[i], 0, 0)),
            ],
            out_specs=rows,
            scratch_shapes=[
                pltpu.VMEM((D_MODEL, D_EXPERT), BF16),
                pltpu.VMEM((D_MODEL, D_EXPERT), BF16),
                pltpu.VMEM((D_EXPERT, D_MODEL), BF16),
            ],
        ),
        out_shape=jax.ShapeDtypeStruct(xs.shape, BF16),
        input_output_aliases={2: 0},
        compiler_params=pltpu.CompilerParams(
            dimension_semantics=("arbitrary",), vmem_limit_bytes=VMEM_LIMIT),
    )(block_expert, n_valid, xs, wg, wu, wd)


def _combine_kernel(doubles_ref, singles_ref, counts_ref,
                    infok_ref, h2_ref, x1_ref, gate_ref, gpost_ref, wsg_ref, wsu_ref, wsd_ref,
                    ys_ref, o_ref, ybuf, sem):
    tile = pl.program_id(0)
    last = pl.num_programs(0) - 1
    slot = tile % 2

    def start(t, s):
        _for_each_copy(
            doubles_ref, singles_ref, counts_ref, t,
            lambda lp, gp, n: _piece_copy(ys_ref, gp, ybuf.at[s], lp, sem.at[s], n).start())

    def wait(t, s):
        _wait_pieces(2 * counts_ref[2 * t] + counts_ref[2 * t + 1],
                     lambda n: _piece_copy(ys_ref, 0, ybuf.at[s], 0, sem.at[s], n))

    @pl.when(tile == 0)
    def _():
        ybuf[...] = jnp.zeros_like(ybuf)
        start(tile, slot)

    @pl.when(tile < last)
    def _():
        start(tile + 1, 1 - slot)

    wait(tile, slot)

    weights = _onehot_chunks(infok_ref, weighted=True)
    hs = _swiglu(h2_ref[...], wsg_ref[...], wsu_ref[...])
    ff = _dot(hs.astype(BF16), wsd_ref[...])
    for j in range(LROWS // LCHUNK):
        ff = ff + lax.dot_general(weights(j), ybuf[slot, j * LCHUNK:(j + 1) * LCHUNK, :],
                                  (((0,), (0,)), ((), ())), preferred_element_type=F32)
    o_ref[...] = x1_ref[...] + gate_ref[0] * _rms(ff, gpost_ref[...])


def _combine(tables, infok, h2, x1, gate, gpost, wsg, wsu, wsd, ys, seq):
    t = h2.shape[0]
    per_seq = seq // TW
    full = lambda shape: pl.BlockSpec(shape, lambda i, *_: (0,) * len(shape))
    tok = lambda width: pl.BlockSpec((TW, width), lambda i, *_: (i, 0))
    return pl.pallas_call(
        _combine_kernel,
        name="combine",
        grid_spec=pltpu.PrefetchScalarGridSpec(
            num_scalar_prefetch=3,
            grid=(t // TW,),
            in_specs=[
                pl.BlockSpec((2 * TOP_K, TW), lambda i, *_: (0, i)), tok(D_MODEL), tok(D_MODEL),
                pl.BlockSpec((1, 1, D_MODEL), lambda i, *_: (i // per_seq, 0, 0)),
                full((1, D_MODEL)),
                full((D_MODEL, D_SHARED)), full((D_MODEL, D_SHARED)), full((D_SHARED, D_MODEL)),
                pl.BlockSpec(memory_space=pl.ANY),
            ],
            out_specs=tok(D_MODEL),
            scratch_shapes=[
                pltpu.VMEM((2, LROWS, D_MODEL), BF16),
                pltpu.SemaphoreType.DMA((2,)),
            ],
        ),
        out_shape=jax.ShapeDtypeStruct((t, D_MODEL), F32),
        compiler_params=pltpu.CompilerParams(
            dimension_semantics=("arbitrary",), vmem_limit_bytes=VMEM_LIMIT),
    )(*tables, infok, h2, x1, gate, gpost, wsg, wsu, wsd, ys)


def _dispatch_plan(cnt, n_tiles):
    n = cnt[:, :n_tiles].astype(jnp.int32)
    pieces = (n + PIECE - 1) // PIECE
    local = jnp.cumsum(pieces, axis=0) - pieces
    seg = jnp.sum(pieces, axis=1)
    per_block = RB // PIECE
    seg_pad = (seg + per_block - 1) // per_block * per_block
    seg_end = jnp.cumsum(seg_pad)
    seg_start = seg_end - seg_pad
    glob = seg_start[:, None] + jnp.cumsum(pieces, axis=1) - pieces
    n_blocks = (TOP_K * TW * n_tiles + N_EXPERTS * n_tiles * (PIECE - 1)
                + N_EXPERTS * (RB - PIECE) + RB - 1) // RB
    n_valid = seg_end[-1] // per_block
    blk = jnp.minimum(jnp.arange(n_blocks, dtype=jnp.int32), n_valid - 1)
    block_end = seg_end // per_block
    block_expert = jnp.sum((block_end[None, :] <= blk[:, None]).astype(jnp.int32), axis=1)
    block_expert = jnp.minimum(block_expert, N_EXPERTS - 1)
    assert LROWS // PIECE <= 1 << LOCAL_BITS and n_blocks * per_block < 1 << (31 - LOCAL_BITS)

    def copy_list(count, width):
        end = jnp.cumsum(count, axis=0).T
        slot = jnp.arange(width, dtype=jnp.int32)
        expert = jnp.sum((end[:, None, :] <= slot[None, :, None]).astype(jnp.int32), axis=-1)
        pick = expert[:, :, None] == jnp.arange(N_EXPERTS, dtype=jnp.int32)
        take = lambda a: jnp.sum(jnp.where(pick, a.T[:, None, :], 0), axis=-1)
        return slot[None, :] - take(jnp.cumsum(count, axis=0) - count), take

    doubles_n = pieces // 2
    odd = pieces % 2
    j2, take2 = copy_list(doubles_n, MAX_DOUBLES)
    doubles = (take2(local) + 2 * j2) | ((take2(glob) + 2 * j2) << LOCAL_BITS)
    _, take1 = copy_list(odd, MAX_SINGLES)
    last_piece = pieces - 1
    singles = take1(local + last_piece) | (take1(glob + last_piece) << LOCAL_BITS)
    counts = jnp.stack([jnp.sum(doubles_n, axis=0), jnp.sum(odd, axis=0)], axis=1)
    tables = (doubles.reshape(-1).astype(jnp.int32), singles.reshape(-1).astype(jnp.int32),
              counts.reshape(-1).astype(jnp.int32))
    tails = ((seg_start + seg).astype(jnp.int32), (seg_pad - seg).astype(jnp.int32))
    return tables, tails, block_expert, n_valid.reshape(1).astype(jnp.int32), n_blocks * RB


def kernel(x, c, w_ada, b_ada, g_pre_mix, g_post_mix, g_pre_ffn, g_post_ffn, w_in, b_forget,
           w_pool, pool_scale, g_pool_out, g_attn_out, w_out, w_router, router_bias,
           w_gate, w_up, w_down, ws_gate, ws_up, ws_down):
    bsz, seq, d = x.shape
    depth = w_ada.shape[0]
    for l in range(depth):
        mod = _ada(c, w_ada, b_ada[l][None, :], l)
        shift_m, scale_m, gate_m, shift_f, scale_f, gate_f = [
            m.reshape(bsz, 1, d) for m in jnp.split(mod, 6, axis=-1)]

        wi = w_in[l]
        qs = HEAD_DIM ** -0.5 * LOG2E
        w1 = jnp.concatenate(
            [wi[:, :D_POOL], wi[:, D_POOL:D_POOL + D_ATTN] * qs,
             wi[:, D_POOL + D_ATTN:D_POOL + 3 * D_ATTN]], axis=1).astype(BF16)
        pad = LANES - N_SPLIT * N_HEADS
        wf = jnp.pad(jnp.tile(wi[:, D_POOL + 3 * D_ATTN:], (1, N_SPLIT)),
                     ((0, 0), (0, pad))).astype(BF16)
        bfp = jnp.pad(jnp.tile(b_forget[l], N_SPLIT), (0, pad))[None, :]

        pool, qa, kt, v = _premix(
            x, shift_m, scale_m, g_pre_mix[l][None, :], w1, wf, bfp, w_pool[l].astype(BF16),
            pool_scale[l][None, :], g_pool_out[l][None, :])
        ot = _attn(qa, kt, v)

        wr = jnp.pad(w_router[l], ((0, 0), (0, LANES - N_EXPERTS)))
        wrh = wr.astype(BF16)
        wrl = (wr - wrh.astype(F32)).astype(BF16)
        t = bsz * seq
        assert t // TW <= LANES
        x1, h2, infok, cnt = _postmix(
            x, pool, ot, w_out[l].astype(BF16), g_attn_out[l][None, :], g_post_mix[l][None, :],
            gate_m, g_pre_ffn[l][None, :], shift_f, scale_f,
            jnp.concatenate([wrh, wrl], axis=1), router_bias[l][:, None])
        tables, tails, block_expert, n_valid, n_rows = _dispatch_plan(cnt, t // TW)
        h2f = h2.reshape(t, d)
        xs = _dispatch(tables + tails + (n_valid,), h2f, infok, n_rows)
        ys = _experts(block_expert, n_valid, xs, w_gate[l], w_up[l], w_down[l])
        out = _combine(tables, infok, h2f, x1.reshape(t, d), gate_f, g_post_ffn[l][None, :],
                       ws_gate[l].astype(BF16), ws_up[l].astype(BF16), ws_down[l].astype(BF16),
                       ys, seq)
        x = out.reshape(bsz, seq, d)
    return x
```

```python
import functools

import numpy as np
import jax
import jax.numpy as jnp
from jax import lax
from jax.experimental import pallas as pl
from jax.experimental.pallas import tpu as pltpu

D_MODEL = 1024
D_POOL = 512
POOL_WINDOWS = (2, 4, 8, 16)
POOL_GROUP = 128
MAX_WINDOW = max(POOL_WINDOWS)
D_ATTN = 512
HEAD_DIM = 64
N_HEADS = 8
N_EXPERTS = 64
N_EXPERT_GROUPS = 8
GROUP_SIZE = N_EXPERTS // N_EXPERT_GROUPS
TOPK_GROUPS = 4
TOP_K = 8
D_EXPERT = 256
D_SHARED = 256
ROUTED_SCALE = 2.5
EPS = 1e-6

LOG2E = 1.4426950408889634
LANES = 128
N_SPLIT = 3
AUG = LANES

TS_PRE = 512
CUM_BLOCK = 256
TS_POST = 512
TQ = 512
TK = 512
HEAD_PAIR = 128 // HEAD_DIM
TW = 256
PIECE = 16
RB = 1024
RSUB = 512
LCHUNK = 512
ONEHOT_GROUP = 64
LROWS = -(-(TOP_K * TW + N_EXPERTS * (PIECE - 1)) // LCHUNK) * LCHUNK

F32 = jnp.float32
BF16 = jnp.bfloat16
VMEM_LIMIT = 56 * 1024 * 1024


def _rms(v, g):
    return v * lax.rsqrt(jnp.mean(v * v, axis=-1, keepdims=True) + EPS) * g


def _split3(v):
    hi = v.astype(BF16)
    r1 = v - hi.astype(F32)
    mid = r1.astype(BF16)
    r2 = r1 - mid.astype(F32)
    lo = r2.astype(BF16)
    return hi, mid, lo


def _dot(a, b):
    return jnp.dot(a, b, preferred_element_type=F32)


def _ada_kernel(c_ref, w_ref, b_ref, o_ref):
    o_ref[...] = _dot(c_ref[...].astype(BF16), w_ref[0].astype(BF16)) + b_ref[...]


def _ada(c, w, b, layer):
    bsz = c.shape[0]
    n = w.shape[2]
    return pl.pallas_call(
        _ada_kernel,
        name="ada",
        grid=(n // D_MODEL,),
        in_specs=[
            pl.BlockSpec((bsz, D_MODEL), lambda j: (0, 0)),
            pl.BlockSpec((1, D_MODEL, D_MODEL), lambda j: (layer, 0, j)),
            pl.BlockSpec((1, D_MODEL), lambda j: (0, j)),
        ],
        out_specs=pl.BlockSpec((bsz, D_MODEL), lambda j: (0, j)),
        out_shape=jax.ShapeDtypeStruct((bsz, n), F32),
    )(c, w, b)


def _premix_kernel(x_ref, shift_ref, scale_ref, g_ref, win_ref, bf_ref, wpool_ref,
                   pscale_ref, gpool_ref, place_ref, ones_ref, fmask_ref,
                   pool_ref, q_ref, kt_ref, v_ref,
                   uext_ref, cum_ref, w1_ref, wf_ref):
    s = pl.program_id(1)
    ts = x_ref.shape[1]

    @pl.when((pl.program_id(0) == 0) & (s == 0))
    def _():
        q0, k0 = D_POOL, D_POOL + D_ATTN
        w1_ref[:, :q0] = win_ref[0, :, :q0].astype(BF16)
        w1_ref[:, q0:k0] = (win_ref[0, :, q0:k0] * (HEAD_DIM ** -0.5 * LOG2E)).astype(BF16)
        w1_ref[:, k0:] = win_ref[0, :, k0:k0 + 2 * D_ATTN].astype(BF16)
        wf_ref[...] = jnp.zeros_like(wf_ref)
        wf_ref[:, :N_HEADS] = win_ref[0, :, k0 + 2 * D_ATTN:].astype(BF16)

    @pl.when(s == 0)
    def _():
        uext_ref[0:MAX_WINDOW, :] = jnp.zeros((MAX_WINDOW, D_POOL), F32)
        cum_ref[...] = jnp.zeros_like(cum_ref)

    x = x_ref[0]
    h = _rms(x, g_ref[...]) * (1.0 + scale_ref[0]) + shift_ref[0]
    hb = h.astype(BF16)
    proj = _dot(hb, w1_ref[...])
    u = proj[:, :D_POOL]
    q = proj[:, D_POOL:D_POOL + D_ATTN]
    k = proj[:, D_POOL + D_ATTN:D_POOL + 2 * D_ATTN]
    v = proj[:, D_POOL + 2 * D_ATTN:]

    uext_ref[MAX_WINDOW:, :] = u
    pos = (s * ts + lax.broadcasted_iota(jnp.int32, (ts, 1), 0) + 1).astype(F32)
    ys = []
    for g, w in enumerate(POOL_WINDOWS):
        c0 = g * POOL_GROUP
        acc = uext_ref[MAX_WINDOW:, c0:c0 + POOL_GROUP]
        for j in range(1, w):
            acc = acc + uext_ref[MAX_WINDOW - j:MAX_WINDOW - j + ts, c0:c0 + POOL_GROUP]
        pooled = acc / jnp.minimum(pos, float(w)) - u[:, c0:c0 + POOL_GROUP]
        ys.append(_dot(pooled.astype(BF16), wpool_ref[g]))
    ypool = jnp.concatenate(ys, axis=1) * pscale_ref[...]
    pool_ref[0] = _rms(ypool, gpool_ref[...]).astype(BF16)
    uext_ref[0:MAX_WINDOW, :] = uext_ref[ts:ts + MAX_WINDOW, :]

    z0 = _dot(hb, wf_ref[...]) + bf_ref[...]
    z = z0
    for p in range(1, N_SPLIT):
        z = z + pltpu.roll(z0, p * N_HEADS, 1)
    logf = jnp.minimum(z, 0.0) - jnp.log1p(jnp.exp(-jnp.abs(z)))
    row = lax.broadcasted_iota(jnp.int32, (CUM_BLOCK, CUM_BLOCK), 0)
    col = lax.broadcasted_iota(jnp.int32, (CUM_BLOCK, CUM_BLOCK), 1)
    tri = (col <= row).astype(BF16)
    pieces = _split3(logf)
    carry = cum_ref[...]
    blocks = []
    for r0 in range(0, ts, CUM_BLOCK):
        blk = carry
        for piece in pieces:
            blk = blk + _dot(tri, piece[r0:r0 + CUM_BLOCK, :])
        carry = blk[CUM_BLOCK - 1:CUM_BLOCK, :]
        blocks.append(blk)
    cum = jnp.concatenate(blocks, axis=0)
    cum_ref[...] = carry

    hi, mid, lo = [piece.astype(F32) for piece in _split3(cum * LOG2E)]
    lane = lax.broadcasted_iota(jnp.int32, (ts, LANES), 1)
    pieces = jnp.where(lane < N_HEADS, hi, jnp.where(lane < 2 * N_HEADS, mid, lo))
    aug = _dot(pieces.astype(BF16), place_ref[...]) + ones_ref[...]
    aug_q = aug[:, :N_HEADS * AUG]
    aug_k = aug[:, N_HEADS * AUG:]

    def expand(a):
        blocks = []
        for j in range(D_ATTN // LANES):
            blk = a[:, j * LANES:(j + 1) * LANES]
            blocks += [blk, blk]
        return jnp.concatenate(blocks, axis=1)

    fmask = fmask_ref[...]
    qa = expand(q) * fmask + aug_q
    ka = expand(k) * fmask + aug_k
    q_ref[0] = qa.astype(BF16)
    kt_ref[0] = ka.T.astype(BF16)
    v_ref[0] = v.astype(BF16)


def _aug_constants():
    width = N_HEADS * AUG
    place = np.zeros((LANES, 2 * width), np.float32)
    ones = np.zeros((1, 2 * width), np.float32)
    fmask = np.zeros((1, width), np.float32)
    for h in range(N_HEADS):
        feat0 = h * AUG + (0 if h % 2 == 0 else HEAD_DIM)
        aug0 = h * AUG + (HEAD_DIM if h % 2 == 0 else 0)
        fmask[0, feat0:feat0 + HEAD_DIM] = 1.0
        for p in range(N_SPLIT):
            place[p * N_HEADS + h, aug0 + p] = 1.0
            ones[0, width + aug0 + p] = 1.0
            place[p * N_HEADS + h, width + aug0 + N_SPLIT + p] = -1.0
            ones[0, aug0 + N_SPLIT + p] = 1.0
    return jnp.asarray(place, BF16), jnp.asarray(ones), jnp.asarray(fmask)


def _premix(x, shift, scale, g, w_in, layer, bfp, wpool, pscale, gpool):
    bsz, seq, _ = x.shape
    d_in = w_in.shape[2]
    assert d_in == D_POOL + 3 * D_ATTN + N_HEADS
    ts = TS_PRE
    place, ones, fmask = _aug_constants()
    full = lambda shape: pl.BlockSpec(shape, lambda b, s: (0,) * len(shape))
    per_batch = pl.BlockSpec((1, 1, D_MODEL), lambda b, s: (b, 0, 0))
    return pl.pallas_call(
        _premix_kernel,
        name="premix",
        grid=(bsz, seq // ts),
        in_specs=[
            pl.BlockSpec((1, ts, D_MODEL), lambda b, s: (b, s, 0)),
            per_batch, per_batch,
            full((1, D_MODEL)),
            pl.BlockSpec((1, D_MODEL, d_in), lambda b, s: (layer, 0, 0),
                         pipeline_mode=pl.Buffered(1)),
            full((1, LANES)),
            full((len(POOL_WINDOWS), POOL_GROUP, POOL_GROUP)),
            full((1, D_POOL)),
            full((1, D_POOL)),
            full((LANES, 2 * N_HEADS * AUG)),
            full((1, 2 * N_HEADS * AUG)),
            full((1, N_HEADS * AUG)),
        ],
        out_specs=[
            pl.BlockSpec((1, ts, D_POOL), lambda b, s: (b, s, 0)),
            pl.BlockSpec((1, ts, N_HEADS * AUG), lambda b, s: (b, s, 0)),
            pl.BlockSpec((1, N_HEADS * AUG, ts), lambda b, s: (b, 0, s)),
            pl.BlockSpec((1, ts, D_ATTN), lambda b, s: (b, s, 0)),
        ],
        out_shape=[
            jax.ShapeDtypeStruct((bsz, seq, D_POOL), BF16),
            jax.ShapeDtypeStruct((bsz, seq, N_HEADS * AUG), BF16),
            jax.ShapeDtypeStruct((bsz, N_HEADS * AUG, seq), BF16),
            jax.ShapeDtypeStruct((bsz, seq, D_ATTN), BF16),
        ],
        scratch_shapes=[
            pltpu.VMEM((ts + MAX_WINDOW, D_POOL), F32),
            pltpu.VMEM((1, LANES), F32),
            pltpu.VMEM((D_MODEL, D_POOL + 3 * D_ATTN), BF16),
            pltpu.VMEM((D_MODEL, LANES), BF16),
        ],
        compiler_params=pltpu.CompilerParams(
            dimension_semantics=("arbitrary", "arbitrary"), vmem_limit_bytes=VMEM_LIMIT),
    )(x, shift, scale, g, w_in, bfp, wpool, pscale, gpool, place, ones, fmask)


def _attn_kernel(q_ref, kt_ref, v_ref, o_ref):
    seq = q_ref.shape[1]
    qry_i = lax.broadcasted_iota(jnp.int32, (TQ, TK), 0)
    key_i = lax.broadcasted_iota(jnp.int32, (TQ, TK), 1)
    causal = key_i <= qry_i
    one_lane = lax.broadcasted_iota(jnp.int32, (seq, LANES), 1) == 0
    vaug = jnp.concatenate([v_ref[0], jnp.where(one_lane, 1.0, 0.0).astype(BF16)], axis=1)
    out_lane = lax.broadcasted_iota(jnp.int32, (TQ, LANES), 1)
    heads = range(HEAD_PAIR)
    for qi in range(seq // TQ):
        rows = slice(qi * TQ, (qi + 1) * TQ)
        q = [q_ref[0, rows, h * AUG:(h + 1) * AUG] for h in heads]
        m = [jnp.full((TQ, 1), -jnp.inf, F32) for _ in heads]
        acc = [jnp.zeros((TQ, 2 * LANES), F32) for _ in heads]
        for kj in range(qi + 1):
            keys = slice(kj * TK, (kj + 1) * TK)
            for h in heads:
                s = _dot(q[h], kt_ref[0, h * AUG:(h + 1) * AUG, keys])
                if kj == qi:
                    s = jnp.where(causal, s, -jnp.inf)
                m_new = jnp.maximum(m[h], jnp.max(s, axis=1, keepdims=True))
                p = jnp.exp2(s - m_new).astype(BF16)
                acc[h] = jnp.exp2(m[h] - m_new) * acc[h] + _dot(p, vaug[keys, :])
                m[h] = m_new
        o = [acc[h][:, :LANES] / acc[h][:, LANES:LANES + 1] for h in heads]
        o_ref[0, rows, :] = jnp.where(out_lane < HEAD_DIM, o[0], o[1])


def _attn(q, kt, v):
    bsz, seq, _ = q.shape
    return pl.pallas_call(
        _attn_kernel,
        name="attn",
        grid=(bsz, N_HEADS // HEAD_PAIR),
        in_specs=[
            pl.BlockSpec((1, seq, HEAD_PAIR * AUG), lambda b, h: (b, 0, h)),
            pl.BlockSpec((1, HEAD_PAIR * AUG, seq), lambda b, h: (b, h, 0)),
            pl.BlockSpec((1, seq, LANES), lambda b, h: (b, 0, h)),
        ],
        out_specs=pl.BlockSpec((1, seq, LANES), lambda b, h: (b, 0, h)),
        out_shape=jax.ShapeDtypeStruct((bsz, seq, D_ATTN), F32),
        compiler_params=pltpu.CompilerParams(
            dimension_semantics=("arbitrary", "arbitrary"), vmem_limit_bytes=VMEM_LIMIT),
    )(q, kt, v)


def _postmix_kernel(x_ref, pool_ref, o_ref, wout_ref, gattn_ref, gpost_ref, gate_ref,
                    gffn_ref, shift_ref, scale_ref, wr_ref, bias_ref,
                    x1_ref, h2_ref, infok_ref, cnt_ref):
    ya = _rms(o_ref[0], gattn_ref[...]).astype(BF16)
    mixed = _dot(pool_ref[0], wout_ref[:D_POOL, :]) + _dot(ya, wout_ref[D_POOL:, :])
    x1 = x_ref[0] + gate_ref[0] * _rms(mixed, gpost_ref[...])
    x1_ref[0] = x1
    h2 = _rms(x1, gffn_ref[...]) * (1.0 + scale_ref[0]) + shift_ref[0]
    h2b = h2.astype(BF16)
    h2_ref[0] = h2b
    h2l = (h2 - h2b.astype(F32)).astype(BF16)
    both = _dot(h2b, wr_ref[...])
    logits = both[:, :LANES] + both[:, LANES:] + _dot(h2l, wr_ref[:, :LANES])
    st = jax.nn.sigmoid(logits).T

    step = pl.program_id(0) * pl.num_programs(1) + pl.program_id(1)

    @pl.when(step == 0)
    def _():
        cnt_ref[...] = jnp.zeros_like(cnt_ref)

    lane = lax.broadcasted_iota(jnp.int32, cnt_ref.shape, 1)
    tiles = x_ref.shape[1] // TW
    for i in range(tiles):
        info, n = _route_tile(st[:, i * TW:(i + 1) * TW], bias_ref)
        infok_ref[:, i * TW:(i + 1) * TW] = info
        cnt_ref[...] = jnp.where(lane == step * tiles + i, jnp.broadcast_to(n, cnt_ref.shape),
                                 cnt_ref[...])


def _postmix(x, pool, ot, wout, gattn, gpost, gate, gffn, shift, scale, wr, bias_col):
    bsz, seq, _ = x.shape
    ts = TS_POST
    ns = seq // ts
    full = lambda shape: pl.BlockSpec(shape, lambda b, s: (0,) * len(shape))
    per_batch = pl.BlockSpec((1, 1, D_MODEL), lambda b, s: (b, 0, 0))
    tok = lambda width: pl.BlockSpec((1, ts, width), lambda b, s: (b, s, 0))
    return pl.pallas_call(
        _postmix_kernel,
        name="postmix",
        grid=(bsz, ns),
        in_specs=[
            tok(D_MODEL), tok(D_POOL), tok(D_ATTN),
            full((D_MODEL, D_MODEL)),
            full((1, D_ATTN)), full((1, D_MODEL)), per_batch,
            full((1, D_MODEL)), per_batch, per_batch,
            full((D_MODEL, 2 * LANES)),
            full((N_EXPERTS, 1)),
        ],
        out_specs=[
            tok(D_MODEL), tok(D_MODEL),
            pl.BlockSpec((2 * TOP_K, ts), lambda b, s: (0, b * ns + s)),
            full((N_EXPERTS, LANES)),
        ],
        out_shape=[
            jax.ShapeDtypeStruct((bsz, seq, D_MODEL), F32),
            jax.ShapeDtypeStruct((bsz, seq, D_MODEL), BF16),
            jax.ShapeDtypeStruct((2 * TOP_K, bsz * seq), F32),
            jax.ShapeDtypeStruct((N_EXPERTS, LANES), F32),
        ],
        compiler_params=pltpu.CompilerParams(
            dimension_semantics=("arbitrary", "arbitrary"), vmem_limit_bytes=VMEM_LIMIT),
    )(x, pool, ot, wout, gattn, gpost, gate, gffn, shift, scale, wr, bias_col)


def _route_tile(st, bias_ref):
    tr = st.shape[1]
    scores = [st[g * GROUP_SIZE:(g + 1) * GROUP_SIZE, :] for g in range(N_EXPERT_GROUPS)]
    sel = [scores[g] + bias_ref[g * GROUP_SIZE:(g + 1) * GROUP_SIZE, :]
           for g in range(N_EXPERT_GROUPS)]
    sub = lax.broadcasted_iota(jnp.int32, (GROUP_SIZE, tr), 0)
    neg = jnp.float32(-jnp.inf)

    rows = []
    for g in range(N_EXPERT_GROUPS):
        v = sel[g]
        m1 = jnp.max(v, axis=0, keepdims=True)
        first = jnp.min(jnp.where(v == m1, sub, GROUP_SIZE), axis=0, keepdims=True)
        m2 = jnp.max(jnp.where(sub == first, neg, v), axis=0, keepdims=True)
        rows.append(m1 + m2)
    gs = jnp.concatenate(rows, axis=0)

    shape = (GROUP_SIZE, tr)
    one = jnp.ones(shape, jnp.int32)
    zero = jnp.zeros(shape, jnp.int32)
    beaten = zero
    for gp in range(N_EXPERT_GROUPS):
        r = jnp.broadcast_to(gs[gp:gp + 1, :], shape)
        tie = jnp.where(sub > gp, one, zero)
        beaten = beaten + jnp.where(r > gs, one, zero) + jnp.where(r == gs, tie, zero)
    keep = jnp.where(beaten < TOPK_GROUPS, 1.0, 0.0)

    masked = [jnp.where(jnp.broadcast_to(keep[g:g + 1, :], shape) > 0.5, sel[g], neg)
              for g in range(N_EXPERT_GROUPS)]

    eidx = [sub + g * GROUP_SIZE for g in range(N_EXPERT_GROUPS)]
    picked = [zero for _ in range(N_EXPERT_GROUPS)]
    for _ in range(TOP_K):
        best = masked[0]
        for g in range(1, N_EXPERT_GROUPS):
            best = jnp.maximum(best, masked[g])
        best = jnp.broadcast_to(jnp.max(best, axis=0, keepdims=True), shape)
        first = jnp.where(masked[0] == best, eidx[0], N_EXPERTS)
        for g in range(1, N_EXPERT_GROUPS):
            first = jnp.minimum(first, jnp.where(masked[g] == best, eidx[g], N_EXPERTS))
        first = jnp.broadcast_to(jnp.min(first, axis=0, keepdims=True), shape)
        for g in range(N_EXPERT_GROUPS):
            hit = eidx[g] == first
            picked[g] = jnp.where(hit, one, picked[g])
            masked[g] = jnp.where(hit, neg, masked[g])
    chosen = [picked[g] > 0 for g in range(N_EXPERT_GROUPS)]

    w = [jnp.where(chosen[g], scores[g], 0.0) for g in range(N_EXPERT_GROUPS)]
    denom = w[0].sum(axis=0, keepdims=True)
    for g in range(1, N_EXPERT_GROUPS):
        denom = denom + w[g].sum(axis=0, keepdims=True)
    wfull = jnp.concatenate([w[g] / denom * ROUTED_SCALE for g in range(N_EXPERT_GROUPS)], axis=0)
    mfull = jnp.concatenate([jnp.where(chosen[g], 1.0, 0.0) for g in range(N_EXPERT_GROUPS)],
                            axis=0)
    mb = mfull.astype(BF16)

    e_r = lax.broadcasted_iota(jnp.int32, (N_EXPERTS, N_EXPERTS), 0)
    e_c = lax.broadcasted_iota(jnp.int32, (N_EXPERTS, N_EXPERTS), 1)
    before_e = (e_c < e_r).astype(BF16)
    t_r = lax.broadcasted_iota(jnp.int32, (tr, tr), 0)
    t_c = lax.broadcasted_iota(jnp.int32, (tr, tr), 1)
    before_t = (t_r < t_c).astype(BF16)
    ordinal = _dot(before_e, mb)
    rank = _dot(mb, before_t)
    n = jnp.sum(mfull, axis=1, keepdims=True)
    pieces = jnp.floor((n + (PIECE - 1.0)) * (1.0 / PIECE))
    run_start = PIECE * _dot(before_e, jnp.broadcast_to(pieces, (N_EXPERTS, LANES)).astype(BF16))
    pos = run_start[:, 0:1] + rank

    sub8 = lax.broadcasted_iota(jnp.int32, (TOP_K, tr), 0)
    pos8 = jnp.zeros((TOP_K, tr), F32)
    w8 = jnp.zeros((TOP_K, tr), F32)
    for k in range(TOP_K):
        selk = jnp.where(ordinal == float(k), mfull, 0.0)
        pk = jnp.sum(selk * pos, axis=0, keepdims=True)
        wk = jnp.sum(selk * wfull, axis=0, keepdims=True)
        pos8 = jnp.where(sub8 == k, jnp.broadcast_to(pk, (TOP_K, tr)), pos8)
        w8 = jnp.where(sub8 == k, jnp.broadcast_to(wk, (TOP_K, tr)), w8)
    return jnp.concatenate([pos8, w8], axis=0), n


def _swiglu(xb, wg, wu):
    g = _dot(xb, wg)
    return (g * jax.nn.sigmoid(g)) * _dot(xb, wu)


def _piece_copy(src_ref, src_row, dst_ref, dst_row, sem, n_pieces=1):
    rows = n_pieces * PIECE
    src = src_ref.at[pl.ds(pl.multiple_of(src_row * PIECE, PIECE), rows), :]
    dst = dst_ref.at[pl.ds(pl.multiple_of(dst_row * PIECE, PIECE), rows), :]
    return pltpu.make_async_copy(src, dst, sem)


LOCAL_BITS = 8
MAX_DOUBLES = LROWS // (2 * PIECE)
MAX_SINGLES = N_EXPERTS


def _for_each_copy(doubles_ref, singles_ref, counts_ref, tile, fn):
    def unpack(word):
        return word & ((1 << LOCAL_BITS) - 1), lax.shift_right_logical(word, LOCAL_BITS)

    def double(i, carry):
        fn(*unpack(doubles_ref[tile * MAX_DOUBLES + i]), 2)
        return carry

    def single(i, carry):
        fn(*unpack(singles_ref[tile * MAX_SINGLES + i]), 1)
        return carry

    lax.fori_loop(0, counts_ref[2 * tile], double, 0)
    lax.fori_loop(0, counts_ref[2 * tile + 1], single, 0)


def _wait_pieces(total, copy_of):
    for bit in range((LROWS // PIECE).bit_length()):
        @pl.when((lax.shift_right_logical(total, bit) & 1) == 1)
        def _():
            copy_of(1 << bit).wait()


def _onehot_chunks(infok_ref, weighted):
    pos = infok_ref[0:TOP_K, :]
    grp = jnp.floor(pos * (1.0 / ONEHOT_GROUP))
    off = pos - ONEHOT_GROUP * grp
    sub = lax.broadcasted_iota(jnp.int32, (ONEHOT_GROUP, TW), 0).astype(F32)
    offs, grps = [], []
    for k in range(TOP_K):
        value = infok_ref[TOP_K + k:TOP_K + k + 1, :] if weighted else 1.0
        offs.append(jnp.where(sub == off[k:k + 1, :], value, 0.0).astype(BF16))
        grps.append(jnp.broadcast_to(grp[k:k + 1, :], (PIECE, TW)).astype(BF16))
    zero = jnp.zeros((PIECE, TW), BF16)

    def chunk(j):
        tiles = []
        for a in range(j * LCHUNK // ONEHOT_GROUP, (j + 1) * LCHUNK // ONEHOT_GROUP):
            hit = [grps[k] == a for k in range(TOP_K)]
            for i in range(ONEHOT_GROUP // PIECE):
                rows = zero
                for k in range(TOP_K):
                    rows = rows + jnp.where(hit[k], offs[k][i * PIECE:(i + 1) * PIECE, :], zero)
                tiles.append(rows)
        return jnp.concatenate(tiles, axis=0)

    return chunk


def _dispatch_kernel(doubles_ref, singles_ref, counts_ref, tail_ref, ntail_ref, nv_ref,
                     h2_ref, infok_ref, xs_ref, lbuf, zbuf, sem, zsem):
    tile = pl.program_id(0)
    last = pl.num_programs(0) - 1
    slot = tile % 2
    n_blocks = xs_ref.shape[0] // RB

    def spare_blocks(fn):
        def per_block(b, carry):
            fn(b)
            return carry
        lax.fori_loop(nv_ref[0], n_blocks, per_block, 0)

    def zero_block(b):
        dst = xs_ref.at[pl.ds(pl.multiple_of(b * RB, RB), RB), :]
        return pltpu.make_async_copy(zbuf, dst, zsem)

    def tails(fn):
        def per_expert(e, carry):
            n = ntail_ref[e]
            for bit in range((RB // PIECE - 1).bit_length()):
                @pl.when((lax.shift_right_logical(n, bit) & 1) == 1)
                def _():
                    done = n & ((1 << bit) - 1)
                    fn(_piece_copy(zbuf, 0, xs_ref, tail_ref[e] + done, zsem, 1 << bit))
            return carry
        lax.fori_loop(0, N_EXPERTS, per_expert, 0)

    def start(t, s):
        _for_each_copy(
            doubles_ref, singles_ref, counts_ref, t,
            lambda lp, gp, n: _piece_copy(lbuf.at[s], lp, xs_ref, gp, sem.at[s], n).start())

    def wait(t, s):
        _wait_pieces(2 * counts_ref[2 * t] + counts_ref[2 * t + 1],
                     lambda n: _piece_copy(lbuf.at[s], 0, xs_ref, 0, sem.at[s], n))

    @pl.when(tile == 0)
    def _():
        zbuf[...] = jnp.zeros_like(zbuf)
        tails(lambda copy: copy.start())
        spare_blocks(lambda b: zero_block(b).start())

    @pl.when(tile >= 2)
    def _():
        wait(tile - 2, slot)

    onehot = _onehot_chunks(infok_ref, weighted=False)
    xb = h2_ref[...]
    used_rows = PIECE * (2 * counts_ref[2 * tile] + counts_ref[2 * tile + 1])

    def sort_chunk(j):
        lbuf[slot, j * LCHUNK:(j + 1) * LCHUNK, :] = _dot(onehot(j), xb).astype(BF16)

    n_chunks = LROWS // LCHUNK
    for j in range(n_chunks - 1):
        sort_chunk(j)
    pl.when(used_rows > (n_chunks - 1) * LCHUNK)(functools.partial(sort_chunk, n_chunks - 1))

    start(tile, slot)

    @pl.when(tile == last)
    def _():
        @pl.when(tile >= 1)
        def _():
            wait(tile - 1, 1 - slot)
        wait(tile, slot)
        tails(lambda copy: copy.wait())
        spare_blocks(lambda b: zero_block(b).wait())


def _dispatch(tables, h2, infok, n_rows):
    t = h2.shape[0]
    return pl.pallas_call(
        _dispatch_kernel,
        name="dispatch",
        grid_spec=pltpu.PrefetchScalarGridSpec(
            num_scalar_prefetch=6,
            grid=(t // TW,),
            in_specs=[
                pl.BlockSpec((TW, D_MODEL), lambda i, *_: (i, 0)),
                pl.BlockSpec((2 * TOP_K, TW), lambda i, *_: (0, i)),
            ],
            out_specs=pl.BlockSpec(memory_space=pl.ANY),
            scratch_shapes=[
                pltpu.VMEM((2, LROWS, D_MODEL), BF16),
                pltpu.VMEM((RB, D_MODEL), BF16),
                pltpu.SemaphoreType.DMA((2,)),
                pltpu.SemaphoreType.DMA(()),
            ],
        ),
        out_shape=jax.ShapeDtypeStruct((n_rows, D_MODEL), BF16),
        compiler_params=pltpu.CompilerParams(
            dimension_semantics=("arbitrary",), vmem_limit_bytes=VMEM_LIMIT),
    )(*tables, h2, infok)


def _experts_kernel(be_ref, nv_ref, xs_ref, wg_ref, wu_ref, wd_ref, ys_ref, wgb, wub, wdb):
    i = pl.program_id(0)

    @pl.when(i < nv_ref[0])
    def _():
        @pl.when((i == 0) | (be_ref[i] != be_ref[jnp.maximum(i - 1, 0)]))
        def _():
            wgb[...] = wg_ref[0].astype(BF16)
            wub[...] = wu_ref[0].astype(BF16)
            wdb[...] = wd_ref[0].astype(BF16)

        for j in range(RB // RSUB):
            rows = slice(j * RSUB, (j + 1) * RSUB)
            h = _swiglu(xs_ref[rows, :], wgb[...], wub[...])
            ys_ref[rows, :] = _dot(h.astype(BF16), wdb[...]).astype(BF16)


def _experts(block_expert, n_valid, xs, wg, wu, wd):
    n_blocks = xs.shape[0] // RB
    rows = pl.BlockSpec((RB, D_MODEL), lambda i, be, nv: (jnp.minimum(i, nv[0] - 1), 0))
    return pl.pallas_call(
        _experts_kernel,
        name="experts",
        grid_spec=pltpu.PrefetchScalarGridSpec(
            num_scalar_prefetch=2,
            grid=(n_blocks,),
            in_specs=[
                rows,
                pl.BlockSpec((1, D_MODEL, D_EXPERT), lambda i, be, nv: (be[i], 0, 0)),
                pl.BlockSpec((1, D_MODEL, D_EXPERT), lambda i, be, nv: (be[i], 0, 0)),
                pl.BlockSpec((1, D_EXPERT, D_MODEL), lambda i, be, nv: (be[i], 0, 0)),
            ],
            out_specs=rows,
            scratch_shapes=[
                pltpu.VMEM((D_MODEL, D_EXPERT), BF16),
                pltpu.VMEM((D_MODEL, D_EXPERT), BF16),
                pltpu.VMEM((D_EXPERT, D_MODEL), BF16),
            ],
        ),
        out_shape=jax.ShapeDtypeStruct(xs.shape, BF16),
        input_output_aliases={2: 0},
        compiler_params=pltpu.CompilerParams(
            dimension_semantics=("arbitrary",), vmem_limit_bytes=VMEM_LIMIT),
    )(block_expert, n_valid, xs, wg, wu, wd)


def _combine_kernel(doubles_ref, singles_ref, counts_ref,
                    infok_ref, h2_ref, x1_ref, gate_ref, gpost_ref, wsg_ref, wsu_ref, wsd_ref,
                    ys_ref, o_ref, ybuf, acc_ref, sem):
    tile = pl.program_id(0)
    last = pl.num_programs(0) - 1
    slot = tile % 2

    def start(t, s):
        _for_each_copy(
            doubles_ref, singles_ref, counts_ref, t,
            lambda lp, gp, n: _piece_copy(ys_ref, gp, ybuf.at[s], lp, sem.at[s], n).start())

    def wait(t, s):
        _wait_pieces(2 * counts_ref[2 * t] + counts_ref[2 * t + 1],
                     lambda n: _piece_copy(ys_ref, 0, ybuf.at[s], 0, sem.at[s], n))

    @pl.when(tile == 0)
    def _():
        ybuf[...] = jnp.zeros_like(ybuf)
        start(tile, slot)

    @pl.when(tile < last)
    def _():
        start(tile + 1, 1 - slot)

    wait(tile, slot)

    weights = _onehot_chunks(infok_ref, weighted=True)
    hs = _swiglu(h2_ref[...], wsg_ref[...], wsu_ref[...])
    ff = _dot(hs.astype(BF16), wsd_ref[...])
    used_rows = PIECE * (2 * counts_ref[2 * tile] + counts_ref[2 * tile + 1])

    def sum_chunk(j):
        return lax.dot_general(weights(j), ybuf[slot, j * LCHUNK:(j + 1) * LCHUNK, :],
                               (((0,), (0,)), ((), ())), preferred_element_type=F32)

    def add_chunk(j):
        acc_ref[...] += sum_chunk(j)

    n_chunks = LROWS // LCHUNK
    for j in range(n_chunks - 1):
        ff = ff + sum_chunk(j)
    acc_ref[...] = ff
    pl.when(used_rows > (n_chunks - 1) * LCHUNK)(functools.partial(add_chunk, n_chunks - 1))
    o_ref[...] = x1_ref[...] + gate_ref[0] * _rms(acc_ref[...], gpost_ref[...])


def _combine(tables, infok, h2, x1, gate, gpost, wsg, wsu, wsd, ys, seq):
    t = h2.shape[0]
    per_seq = seq // TW
    full = lambda shape: pl.BlockSpec(shape, lambda i, *_: (0,) * len(shape))
    tok = lambda width: pl.BlockSpec((TW, width), lambda i, *_: (i, 0))
    return pl.pallas_call(
        _combine_kernel,
        name="combine",
        grid_spec=pltpu.PrefetchScalarGridSpec(
            num_scalar_prefetch=3,
            grid=(t // TW,),
            in_specs=[
                pl.BlockSpec((2 * TOP_K, TW), lambda i, *_: (0, i)), tok(D_MODEL), tok(D_MODEL),
                pl.BlockSpec((1, 1, D_MODEL), lambda i, *_: (i // per_seq, 0, 0)),
                full((1, D_MODEL)),
                full((D_MODEL, D_SHARED)), full((D_MODEL, D_SHARED)), full((D_SHARED, D_MODEL)),
                pl.BlockSpec(memory_space=pl.ANY),
            ],
            out_specs=tok(D_MODEL),
            scratch_shapes=[
                pltpu.VMEM((2, LROWS, D_MODEL), BF16),
                pltpu.VMEM((TW, D_MODEL), F32),
                pltpu.SemaphoreType.DMA((2,)),
            ],
        ),
        out_shape=jax.ShapeDtypeStruct((t, D_MODEL), F32),
        compiler_params=pltpu.CompilerParams(
            dimension_semantics=("arbitrary",), vmem_limit_bytes=VMEM_LIMIT),
    )(*tables, infok, h2, x1, gate, gpost, wsg, wsu, wsd, ys)


def _dispatch_plan(cnt, n_tiles):
    n = cnt[:, :n_tiles].astype(jnp.int32)
    pieces = (n + PIECE - 1) // PIECE
    local = jnp.cumsum(pieces, axis=0) - pieces
    seg = jnp.sum(pieces, axis=1)
    per_block = RB // PIECE
    seg_pad = (seg + per_block - 1) // per_block * per_block
    seg_end = jnp.cumsum(seg_pad)
    seg_start = seg_end - seg_pad
    glob = seg_start[:, None] + jnp.cumsum(pieces, axis=1) - pieces
    n_blocks = (TOP_K * TW * n_tiles + N_EXPERTS * n_tiles * (PIECE - 1)
                + N_EXPERTS * (RB - PIECE) + RB - 1) // RB
    n_valid = seg_end[-1] // per_block
    blk = jnp.minimum(jnp.arange(n_blocks, dtype=jnp.int32), n_valid - 1)
    block_end = seg_end // per_block
    block_expert = jnp.sum((block_end[None, :] <= blk[:, None]).astype(jnp.int32), axis=1)
    block_expert = jnp.minimum(block_expert, N_EXPERTS - 1)
    assert LROWS // PIECE <= 1 << LOCAL_BITS and n_blocks * per_block < 1 << (31 - LOCAL_BITS)

    def copy_list(count, width):
        end = jnp.cumsum(count, axis=0).T
        slot = jnp.arange(width, dtype=jnp.int32)
        expert = jnp.sum((end[:, None, :] <= slot[None, :, None]).astype(jnp.int32), axis=-1)
        pick = expert[:, :, None] == jnp.arange(N_EXPERTS, dtype=jnp.int32)
        take = lambda a: jnp.sum(jnp.where(pick, a.T[:, None, :], 0), axis=-1)
        return slot[None, :] - take(jnp.cumsum(count, axis=0) - count), take

    doubles_n = pieces // 2
    odd = pieces % 2
    j2, take2 = copy_list(doubles_n, MAX_DOUBLES)
    doubles = (take2(local) + 2 * j2) | ((take2(glob) + 2 * j2) << LOCAL_BITS)
    _, take1 = copy_list(odd, MAX_SINGLES)
    last_piece = pieces - 1
    singles = take1(local + last_piece) | (take1(glob + last_piece) << LOCAL_BITS)
    counts = jnp.stack([jnp.sum(doubles_n, axis=0), jnp.sum(odd, axis=0)], axis=1)
    tables = (doubles.reshape(-1).astype(jnp.int32), singles.reshape(-1).astype(jnp.int32),
              counts.reshape(-1).astype(jnp.int32))
    tails = ((seg_start + seg).astype(jnp.int32), (seg_pad - seg).astype(jnp.int32))
    return tables, tails, block_expert, n_valid.reshape(1).astype(jnp.int32), n_blocks * RB


def kernel(x, c, w_ada, b_ada, g_pre_mix, g_post_mix, g_pre_ffn, g_post_ffn, w_in, b_forget,
           w_pool, pool_scale, g_pool_out, g_attn_out, w_out, w_router, router_bias,
           w_gate, w_up, w_down, ws_gate, ws_up, ws_down):
    bsz, seq, d = x.shape
    depth = w_ada.shape[0]
    for l in range(depth):
        mod = _ada(c, w_ada, b_ada[l][None, :], l)
        shift_m, scale_m, gate_m, shift_f, scale_f, gate_f = [
            m.reshape(bsz, 1, d) for m in jnp.split(mod, 6, axis=-1)]

        bfp = jnp.pad(b_forget[l], (0, LANES - N_HEADS))[None, :]
        pool, qa, kt, v = _premix(
            x, shift_m, scale_m, g_pre_mix[l][None, :], w_in, l, bfp, w_pool[l].astype(BF16),
            pool_scale[l][None, :], g_pool_out[l][None, :])
        ot = _attn(qa, kt, v)

        wr = jnp.pad(w_router[l], ((0, 0), (0, LANES - N_EXPERTS)))
        wrh = wr.astype(BF16)
        wrl = (wr - wrh.astype(F32)).astype(BF16)
        t = bsz * seq
        assert t // TW <= LANES
        x1, h2, infok, cnt = _postmix(
            x, pool, ot, w_out[l].astype(BF16), g_attn_out[l][None, :], g_post_mix[l][None, :],
            gate_m, g_pre_ffn[l][None, :], shift_f, scale_f,
            jnp.concatenate([wrh, wrl], axis=1), router_bias[l][:, None])
        tables, tails, block_expert, n_valid, n_rows = _dispatch_plan(cnt, t // TW)
        h2f = h2.reshape(t, d)
        xs = _dispatch(tables + tails + (n_valid,), h2f, infok, n_rows)
        ys = _experts(block_expert, n_valid, xs, w_gate[l], w_up[l], w_down[l])
        out = _combine(tables, infok, h2f, x1.reshape(t, d), gate_f, g_post_ffn[l][None, :],
                       ws_gate[l].astype(BF16), ws_up[l].astype(BF16), ws_down[l].astype(BF16),
                       ys, seq)
        x = out.reshape(bsz, seq, d)
    return x
```

```python
import functools

import numpy as np
import jax
import jax.numpy as jnp
from jax import lax
from jax.experimental import pallas as pl
from jax.experimental.pallas import tpu as pltpu

D_MODEL = 1024
D_POOL = 512
POOL_WINDOWS = (2, 4, 8, 16)
POOL_GROUP = 128
MAX_WINDOW = max(POOL_WINDOWS)
D_ATTN = 512
HEAD_DIM = 64
N_HEADS = 8
N_EXPERTS = 64
N_EXPERT_GROUPS = 8
GROUP_SIZE = N_EXPERTS // N_EXPERT_GROUPS
TOPK_GROUPS = 4
TOP_K = 8
D_EXPERT = 256
D_SHARED = 256
ROUTED_SCALE = 2.5
EPS = 1e-6

LOG2E = 1.4426950408889634
LANES = 128
N_SPLIT = 3
AUG = LANES

TS_PRE = 512
CUM_BLOCK = 256
TS_POST = 512
TQ = 512
TK = 512
HEAD_PAIR = 128 // HEAD_DIM
TW = 256
PIECE = 16
RB = 1024
RSUB = 512
LCHUNK = 512
ONEHOT_GROUP = 64
LROWS = -(-(TOP_K * TW + N_EXPERTS * (PIECE - 1)) // LCHUNK) * LCHUNK

F32 = jnp.float32
BF16 = jnp.bfloat16
VMEM_LIMIT = 56 * 1024 * 1024


def _rms(v, g):
    return v * lax.rsqrt(jnp.mean(v * v, axis=-1, keepdims=True) + EPS) * g


def _split3(v):
    hi = v.astype(BF16)
    r1 = v - hi.astype(F32)
    mid = r1.astype(BF16)
    r2 = r1 - mid.astype(F32)
    lo = r2.astype(BF16)
    return hi, mid, lo


def _dot(a, b):
    return jnp.dot(a, b, preferred_element_type=F32)


def _ada_kernel(c_ref, w_ref, b_ref, o_ref):
    o_ref[...] = _dot(c_ref[...].astype(BF16), w_ref[0].astype(BF16)) + b_ref[...]


def _ada(c, w, b, layer):
    bsz = c.shape[0]
    n = w.shape[2]
    return pl.pallas_call(
        _ada_kernel,
        name="ada",
        grid=(n // D_MODEL,),
        in_specs=[
            pl.BlockSpec((bsz, D_MODEL), lambda j: (0, 0)),
            pl.BlockSpec((1, D_MODEL, D_MODEL), lambda j: (layer, 0, j)),
            pl.BlockSpec((1, D_MODEL), lambda j: (0, j)),
        ],
        out_specs=pl.BlockSpec((bsz, D_MODEL), lambda j: (0, j)),
        out_shape=jax.ShapeDtypeStruct((bsz, n), F32),
    )(c, w, b)


def _premix_kernel(x_ref, shift_ref, scale_ref, g_ref, win_ref, bf_ref, wpool_ref,
                   pscale_ref, gpool_ref, place_ref, ones_ref, fmask_ref,
                   pool_ref, q_ref, kt_ref, v_ref,
                   uext_ref, cum_ref, w1_ref, wf_ref):
    s = pl.program_id(1)
    ts = x_ref.shape[1]

    @pl.when((pl.program_id(0) == 0) & (s == 0))
    def _():
        q0, k0 = D_POOL, D_POOL + D_ATTN
        w1_ref[:, :q0] = win_ref[0, :, :q0].astype(BF16)
        w1_ref[:, q0:k0] = (win_ref[0, :, q0:k0] * (HEAD_DIM ** -0.5 * LOG2E)).astype(BF16)
        w1_ref[:, k0:] = win_ref[0, :, k0:k0 + 2 * D_ATTN].astype(BF16)
        wf_ref[...] = jnp.zeros_like(wf_ref)
        wf_ref[:, :N_HEADS] = win_ref[0, :, k0 + 2 * D_ATTN:].astype(BF16)

    @pl.when(s == 0)
    def _():
        uext_ref[0:MAX_WINDOW, :] = jnp.zeros((MAX_WINDOW, D_POOL), F32)
        cum_ref[...] = jnp.zeros_like(cum_ref)

    x = x_ref[0]
    h = _rms(x, g_ref[...]) * (1.0 + scale_ref[0]) + shift_ref[0]
    hb = h.astype(BF16)
    proj = _dot(hb, w1_ref[...])
    u = proj[:, :D_POOL]
    q = proj[:, D_POOL:D_POOL + D_ATTN]
    k = proj[:, D_POOL + D_ATTN:D_POOL + 2 * D_ATTN]
    v = proj[:, D_POOL + 2 * D_ATTN:]

    uext_ref[MAX_WINDOW:, :] = u
    pos = (s * ts + lax.broadcasted_iota(jnp.int32, (ts, 1), 0) + 1).astype(F32)
    ys = []
    for g, w in enumerate(POOL_WINDOWS):
        c0 = g * POOL_GROUP
        acc = uext_ref[MAX_WINDOW:, c0:c0 + POOL_GROUP]
        for j in range(1, w):
            acc = acc + uext_ref[MAX_WINDOW - j:MAX_WINDOW - j + ts, c0:c0 + POOL_GROUP]
        pooled = acc / jnp.minimum(pos, float(w)) - u[:, c0:c0 + POOL_GROUP]
        ys.append(_dot(pooled.astype(BF16), wpool_ref[g]))
    ypool = jnp.concatenate(ys, axis=1) * pscale_ref[...]
    pool_ref[0] = _rms(ypool, gpool_ref[...]).astype(BF16)
    uext_ref[0:MAX_WINDOW, :] = uext_ref[ts:ts + MAX_WINDOW, :]

    z0 = _dot(hb, wf_ref[...]) + bf_ref[...]
    z = z0
    for p in range(1, N_SPLIT):
        z = z + pltpu.roll(z0, p * N_HEADS, 1)
    logf = jnp.minimum(z, 0.0) - jnp.log1p(jnp.exp(-jnp.abs(z)))
    row = lax.broadcasted_iota(jnp.int32, (CUM_BLOCK, CUM_BLOCK), 0)
    col = lax.broadcasted_iota(jnp.int32, (CUM_BLOCK, CUM_BLOCK), 1)
    tri = (col <= row).astype(BF16)
    pieces = _split3(logf)
    carry = cum_ref[...]
    blocks = []
    for r0 in range(0, ts, CUM_BLOCK):
        blk = carry
        for piece in pieces:
            blk = blk + _dot(tri, piece[r0:r0 + CUM_BLOCK, :])
        carry = blk[CUM_BLOCK - 1:CUM_BLOCK, :]
        blocks.append(blk)
    cum = jnp.concatenate(blocks, axis=0)
    cum_ref[...] = carry

    hi, mid, lo = [piece.astype(F32) for piece in _split3(cum * LOG2E)]
    lane = lax.broadcasted_iota(jnp.int32, (ts, LANES), 1)
    pieces = jnp.where(lane < N_HEADS, hi, jnp.where(lane < 2 * N_HEADS, mid, lo))
    aug = _dot(pieces.astype(BF16), place_ref[...]) + ones_ref[...]
    aug_q = aug[:, :N_HEADS * AUG]
    aug_k = aug[:, N_HEADS * AUG:]

    def expand(a):
        blocks = []
        for j in range(D_ATTN // LANES):
            blk = a[:, j * LANES:(j + 1) * LANES]
            blocks += [blk, blk]
        return jnp.concatenate(blocks, axis=1)

    fmask = fmask_ref[...]
    qa = expand(q) * fmask + aug_q
    ka = expand(k) * fmask + aug_k
    q_ref[0] = qa.astype(BF16)
    kt_ref[0] = ka.T.astype(BF16)
    v_ref[0] = v.astype(BF16)


def _aug_constants():
    width = N_HEADS * AUG
    place = np.zeros((LANES, 2 * width), np.float32)
    ones = np.zeros((1, 2 * width), np.float32)
    fmask = np.zeros((1, width), np.float32)
    for h in range(N_HEADS):
        feat0 = h * AUG + (0 if h % 2 == 0 else HEAD_DIM)
        aug0 = h * AUG + (HEAD_DIM if h % 2 == 0 else 0)
        fmask[0, feat0:feat0 + HEAD_DIM] = 1.0
        for p in range(N_SPLIT):
            place[p * N_HEADS + h, aug0 + p] = 1.0
            ones[0, width + aug0 + p] = 1.0
            place[p * N_HEADS + h, width + aug0 + N_SPLIT + p] = -1.0
            ones[0, aug0 + N_SPLIT + p] = 1.0
    return jnp.asarray(place, BF16), jnp.asarray(ones), jnp.asarray(fmask)


def _premix(x, shift, scale, g, w_in, layer, bfp, wpool, pscale, gpool):
    bsz, seq, _ = x.shape
    d_in = w_in.shape[2]
    assert d_in == D_POOL + 3 * D_ATTN + N_HEADS
    ts = TS_PRE
    place, ones, fmask = _aug_constants()
    full = lambda shape: pl.BlockSpec(shape, lambda b, s: (0,) * len(shape))
    per_batch = pl.BlockSpec((1, 1, D_MODEL), lambda b, s: (b, 0, 0))
    return pl.pallas_call(
        _premix_kernel,
        name="premix",
        grid=(bsz, seq // ts),
        in_specs=[
            pl.BlockSpec((1, ts, D_MODEL), lambda b, s: (b, s, 0)),
            per_batch, per_batch,
            full((1, D_MODEL)),
            pl.BlockSpec((1, D_MODEL, d_in), lambda b, s: (layer, 0, 0),
                         pipeline_mode=pl.Buffered(1)),
            full((1, LANES)),
            full((len(POOL_WINDOWS), POOL_GROUP, POOL_GROUP)),
            full((1, D_POOL)),
            full((1, D_POOL)),
            full((LANES, 2 * N_HEADS * AUG)),
            full((1, 2 * N_HEADS * AUG)),
            full((1, N_HEADS * AUG)),
        ],
        out_specs=[
            pl.BlockSpec((1, ts, D_POOL), lambda b, s: (b, s, 0)),
            pl.BlockSpec((1, ts, N_HEADS * AUG), lambda b, s: (b, s, 0)),
            pl.BlockSpec((1, N_HEADS * AUG, ts), lambda b, s: (b, 0, s)),
            pl.BlockSpec((1, ts, D_ATTN), lambda b, s: (b, s, 0)),
        ],
        out_shape=[
            jax.ShapeDtypeStruct((bsz, seq, D_POOL), BF16),
            jax.ShapeDtypeStruct((bsz, seq, N_HEADS * AUG), BF16),
            jax.ShapeDtypeStruct((bsz, N_HEADS * AUG, seq), BF16),
            jax.ShapeDtypeStruct((bsz, seq, D_ATTN), BF16),
        ],
        scratch_shapes=[
            pltpu.VMEM((ts + MAX_WINDOW, D_POOL), F32),
            pltpu.VMEM((1, LANES), F32),
            pltpu.VMEM((D_MODEL, D_POOL + 3 * D_ATTN), BF16),
            pltpu.VMEM((D_MODEL, LANES), BF16),
        ],
        compiler_params=pltpu.CompilerParams(
            dimension_semantics=("arbitrary", "arbitrary"), vmem_limit_bytes=VMEM_LIMIT),
    )(x, shift, scale, g, w_in, bfp, wpool, pscale, gpool, place, ones, fmask)


def _attn_kernel(q_ref, kt_ref, v_ref, o_ref):
    seq = q_ref.shape[1]
    qry_i = lax.broadcasted_iota(jnp.int32, (TQ, TK), 0)
    key_i = lax.broadcasted_iota(jnp.int32, (TQ, TK), 1)
    causal = key_i <= qry_i
    one_lane = lax.broadcasted_iota(jnp.int32, (seq, LANES), 1) == 0
    vaug = jnp.concatenate([v_ref[0], jnp.where(one_lane, 1.0, 0.0).astype(BF16)], axis=1)
    out_lane = lax.broadcasted_iota(jnp.int32, (TQ, LANES), 1)
    heads = range(HEAD_PAIR)
    for qi in range(seq // TQ):
        rows = slice(qi * TQ, (qi + 1) * TQ)
        q = [q_ref[0, rows, h * AUG:(h + 1) * AUG] for h in heads]
        m = [jnp.full((TQ, 1), -jnp.inf, F32) for _ in heads]
        acc = [jnp.zeros((TQ, 2 * LANES), F32) for _ in heads]
        for kj in range(qi + 1):
            keys = slice(kj * TK, (kj + 1) * TK)
            for h in heads:
                s = _dot(q[h], kt_ref[0, h * AUG:(h + 1) * AUG, keys])
                if kj == qi:
                    s = jnp.where(causal, s, -jnp.inf)
                m_new = jnp.maximum(m[h], jnp.max(s, axis=1, keepdims=True))
                p = jnp.exp2(s - m_new).astype(BF16)
                acc[h] = jnp.exp2(m[h] - m_new) * acc[h] + _dot(p, vaug[keys, :])
                m[h] = m_new
        o = [acc[h][:, :LANES] / acc[h][:, LANES:LANES + 1] for h in heads]
        o_ref[0, rows, :] = jnp.where(out_lane < HEAD_DIM, o[0], o[1])


def _attn(q, kt, v):
    bsz, seq, _ = q.shape
    return pl.pallas_call(
        _attn_kernel,
        name="attn",
        grid=(bsz, N_HEADS // HEAD_PAIR),
        in_specs=[
            pl.BlockSpec((1, seq, HEAD_PAIR * AUG), lambda b, h: (b, 0, h)),
            pl.BlockSpec((1, HEAD_PAIR * AUG, seq), lambda b, h: (b, h, 0)),
            pl.BlockSpec((1, seq, LANES), lambda b, h: (b, 0, h)),
        ],
        out_specs=pl.BlockSpec((1, seq, LANES), lambda b, h: (b, 0, h)),
        out_shape=jax.ShapeDtypeStruct((bsz, seq, D_ATTN), F32),
        compiler_params=pltpu.CompilerParams(
            dimension_semantics=("arbitrary", "arbitrary"), vmem_limit_bytes=VMEM_LIMIT),
    )(q, kt, v)


def _postmix_kernel(x_ref, pool_ref, o_ref, wout_ref, gattn_ref, gpost_ref, gate_ref,
                    gffn_ref, shift_ref, scale_ref, wr_ref, bias_ref,
                    x1_ref, h2_ref, infok_ref, cnt_ref):
    ya = _rms(o_ref[0], gattn_ref[...]).astype(BF16)
    mixed = _dot(pool_ref[0], wout_ref[:D_POOL, :]) + _dot(ya, wout_ref[D_POOL:, :])
    x1 = x_ref[0] + gate_ref[0] * _rms(mixed, gpost_ref[...])
    x1_ref[0] = x1
    h2 = _rms(x1, gffn_ref[...]) * (1.0 + scale_ref[0]) + shift_ref[0]
    h2b = h2.astype(BF16)
    h2_ref[0] = h2b
    h2l = (h2 - h2b.astype(F32)).astype(BF16)
    both = _dot(h2b, wr_ref[...])
    logits = both[:, :LANES] + both[:, LANES:] + _dot(h2l, wr_ref[:, :LANES])
    st = jax.nn.sigmoid(logits).T

    step = pl.program_id(0) * pl.num_programs(1) + pl.program_id(1)

    @pl.when(step == 0)
    def _():
        cnt_ref[...] = jnp.zeros_like(cnt_ref)

    lane = lax.broadcasted_iota(jnp.int32, cnt_ref.shape, 1)
    tiles = x_ref.shape[1] // TW
    for i in range(tiles):
        info, n = _route_tile(st[:, i * TW:(i + 1) * TW], bias_ref)
        infok_ref[:, i * TW:(i + 1) * TW] = info
        cnt_ref[...] = jnp.where(lane == step * tiles + i, jnp.broadcast_to(n, cnt_ref.shape),
                                 cnt_ref[...])


def _postmix(x, pool, ot, wout, gattn, gpost, gate, gffn, shift, scale, wr, bias_col):
    bsz, seq, _ = x.shape
    ts = TS_POST
    ns = seq // ts
    full = lambda shape: pl.BlockSpec(shape, lambda b, s: (0,) * len(shape))
    per_batch = pl.BlockSpec((1, 1, D_MODEL), lambda b, s: (b, 0, 0))
    tok = lambda width: pl.BlockSpec((1, ts, width), lambda b, s: (b, s, 0))
    return pl.pallas_call(
        _postmix_kernel,
        name="postmix",
        grid=(bsz, ns),
        in_specs=[
            tok(D_MODEL), tok(D_POOL), tok(D_ATTN),
            full((D_MODEL, D_MODEL)),
            full((1, D_ATTN)), full((1, D_MODEL)), per_batch,
            full((1, D_MODEL)), per_batch, per_batch,
            full((D_MODEL, 2 * LANES)),
            full((N_EXPERTS, 1)),
        ],
        out_specs=[
            tok(D_MODEL), tok(D_MODEL),
            pl.BlockSpec((2 * TOP_K, ts), lambda b, s: (0, b * ns + s)),
            full((N_EXPERTS, LANES)),
        ],
        out_shape=[
            jax.ShapeDtypeStruct((bsz, seq, D_MODEL), F32),
            jax.ShapeDtypeStruct((bsz, seq, D_MODEL), BF16),
            jax.ShapeDtypeStruct((2 * TOP_K, bsz * seq), F32),
            jax.ShapeDtypeStruct((N_EXPERTS, LANES), F32),
        ],
        compiler_params=pltpu.CompilerParams(
            dimension_semantics=("arbitrary", "arbitrary"), vmem_limit_bytes=VMEM_LIMIT),
    )(x, pool, ot, wout, gattn, gpost, gate, gffn, shift, scale, wr, bias_col)


def _route_tile(st, bias_ref):
    tr = st.shape[1]
    scores = [st[g * GROUP_SIZE:(g + 1) * GROUP_SIZE, :] for g in range(N_EXPERT_GROUPS)]
    sel = [scores[g] + bias_ref[g * GROUP_SIZE:(g + 1) * GROUP_SIZE, :]
           for g in range(N_EXPERT_GROUPS)]
    sub = lax.broadcasted_iota(jnp.int32, (GROUP_SIZE, tr), 0)
    neg = jnp.float32(-jnp.inf)

    rows = []
    for g in range(N_EXPERT_GROUPS):
        v = sel[g]
        m1 = jnp.max(v, axis=0, keepdims=True)
        first = jnp.min(jnp.where(v == m1, sub, GROUP_SIZE), axis=0, keepdims=True)
        m2 = jnp.max(jnp.where(sub == first, neg, v), axis=0, keepdims=True)
        rows.append(m1 + m2)
    gs = jnp.concatenate(rows, axis=0)

    shape = (GROUP_SIZE, tr)
    one = jnp.ones(shape, jnp.int32)
    zero = jnp.zeros(shape, jnp.int32)
    beaten = zero
    for gp in range(N_EXPERT_GROUPS):
        r = jnp.broadcast_to(gs[gp:gp + 1, :], shape)
        tie = jnp.where(sub > gp, one, zero)
        beaten = beaten + jnp.where(r > gs, one, zero) + jnp.where(r == gs, tie, zero)
    keep = jnp.where(beaten < TOPK_GROUPS, 1.0, 0.0)

    masked = [jnp.where(jnp.broadcast_to(keep[g:g + 1, :], shape) > 0.5, sel[g], neg)
              for g in range(N_EXPERT_GROUPS)]

    eidx = [sub + g * GROUP_SIZE for g in range(N_EXPERT_GROUPS)]
    picked = [zero for _ in range(N_EXPERT_GROUPS)]
    for _ in range(TOP_K):
        best = masked[0]
        for g in range(1, N_EXPERT_GROUPS):
            best = jnp.maximum(best, masked[g])
        best = jnp.broadcast_to(jnp.max(best, axis=0, keepdims=True), shape)
        first = jnp.where(masked[0] == best, eidx[0], N_EXPERTS)
        for g in range(1, N_EXPERT_GROUPS):
            first = jnp.minimum(first, jnp.where(masked[g] == best, eidx[g], N_EXPERTS))
        first = jnp.broadcast_to(jnp.min(first, axis=0, keepdims=True), shape)
        for g in range(N_EXPERT_GROUPS):
            hit = eidx[g] == first
            picked[g] = jnp.where(hit, one, picked[g])
            masked[g] = jnp.where(hit, neg, masked[g])
    chosen = [picked[g] > 0 for g in range(N_EXPERT_GROUPS)]

    w = [jnp.where(chosen[g], scores[g], 0.0) for g in range(N_EXPERT_GROUPS)]
    denom = w[0].sum(axis=0, keepdims=True)
    for g in range(1, N_EXPERT_GROUPS):
        denom = denom + w[g].sum(axis=0, keepdims=True)
    wfull = jnp.concatenate([w[g] / denom * ROUTED_SCALE for g in range(N_EXPERT_GROUPS)], axis=0)
    mfull = jnp.concatenate([jnp.where(chosen[g], 1.0, 0.0) for g in range(N_EXPERT_GROUPS)],
                            axis=0)
    mb = mfull.astype(BF16)

    e_r = lax.broadcasted_iota(jnp.int32, (N_EXPERTS, N_EXPERTS), 0)
    e_c = lax.broadcasted_iota(jnp.int32, (N_EXPERTS, N_EXPERTS), 1)
    before_e = (e_c < e_r).astype(BF16)
    t_r = lax.broadcasted_iota(jnp.int32, (tr, tr), 0)
    t_c = lax.broadcasted_iota(jnp.int32, (tr, tr), 1)
    before_t = (t_r < t_c).astype(BF16)
    ordinal = _dot(before_e, mb)
    rank = _dot(mb, before_t)
    n = jnp.sum(mfull, axis=1, keepdims=True)
    pieces = jnp.floor((n + (PIECE - 1.0)) * (1.0 / PIECE))
    run_start = PIECE * _dot(before_e, jnp.broadcast_to(pieces, (N_EXPERTS, LANES)).astype(BF16))
    pos = run_start[:, 0:1] + rank

    sub8 = lax.broadcasted_iota(jnp.int32, (TOP_K, tr), 0)
    pos8 = jnp.zeros((TOP_K, tr), F32)
    w8 = jnp.zeros((TOP_K, tr), F32)
    for k in range(TOP_K):
        selk = jnp.where(ordinal == float(k), mfull, 0.0)
        pk = jnp.sum(selk * pos, axis=0, keepdims=True)
        wk = jnp.sum(selk * wfull, axis=0, keepdims=True)
        pos8 = jnp.where(sub8 == k, jnp.broadcast_to(pk, (TOP_K, tr)), pos8)
        w8 = jnp.where(sub8 == k, jnp.broadcast_to(wk, (TOP_K, tr)), w8)
    return jnp.concatenate([pos8, w8], axis=0), n


def _swiglu(xb, wg, wu):
    g = _dot(xb, wg)
    return (g * jax.nn.sigmoid(g)) * _dot(xb, wu)


def _piece_copy(src_ref, src_piece, dst_ref, dst_piece, sem, n_pieces=1):
    src = src_ref.at[pl.ds(src_piece, n_pieces)]
    dst = dst_ref.at[pl.ds(dst_piece, n_pieces)]
    return pltpu.make_async_copy(src, dst, sem)


def _as_pieces(rows):
    return rows.reshape(rows.shape[0] // PIECE, PIECE, rows.shape[1])


LOCAL_BITS = 8
COPY_SIZES = (4, 3, 2, 1)
MAX_COPIES = N_EXPERTS
assert LROWS // (COPY_SIZES[0] * PIECE) <= MAX_COPIES


def _for_each_copy(copies_ref, counts_ref, tile, fn):
    for c, size in enumerate(COPY_SIZES):
        base = (tile * len(COPY_SIZES) + c) * MAX_COPIES

        def body(i, carry, base=base, size=size):
            word = copies_ref[base + i]
            fn(word & ((1 << LOCAL_BITS) - 1), lax.shift_right_logical(word, LOCAL_BITS), size)
            return carry

        lax.fori_loop(0, counts_ref[tile * (len(COPY_SIZES) + 1) + c], body, 0)


def _tile_pieces(counts_ref, tile):
    return counts_ref[tile * (len(COPY_SIZES) + 1) + len(COPY_SIZES)]


def _wait_pieces(total, copy_of):
    for bit in range((LROWS // PIECE).bit_length()):
        @pl.when((lax.shift_right_logical(total, bit) & 1) == 1)
        def _():
            copy_of(1 << bit).wait()


def _onehot_chunks(infok_ref, weighted):
    pos = infok_ref[0:TOP_K, :]
    grp = jnp.floor(pos * (1.0 / ONEHOT_GROUP))
    off = pos - ONEHOT_GROUP * grp
    sub = lax.broadcasted_iota(jnp.int32, (ONEHOT_GROUP, TW), 0).astype(F32)
    offs, grps = [], []
    for k in range(TOP_K):
        value = infok_ref[TOP_K + k:TOP_K + k + 1, :] if weighted else 1.0
        offs.append(jnp.where(sub == off[k:k + 1, :], value, 0.0).astype(BF16))
        grps.append(jnp.broadcast_to(grp[k:k + 1, :], (PIECE, TW)).astype(BF16))
    zero = jnp.zeros((PIECE, TW), BF16)

    def chunk(j):
        tiles = []
        for a in range(j * LCHUNK // ONEHOT_GROUP, (j + 1) * LCHUNK // ONEHOT_GROUP):
            hit = [grps[k] == a for k in range(TOP_K)]
            for i in range(ONEHOT_GROUP // PIECE):
                rows = zero
                for k in range(TOP_K):
                    rows = rows + jnp.where(hit[k], offs[k][i * PIECE:(i + 1) * PIECE, :], zero)
                tiles.append(rows)
        return jnp.concatenate(tiles, axis=0)

    return chunk


def _dispatch_kernel(copies_ref, counts_ref, tail_ref, ntail_ref, nv_ref,
                     h2_ref, infok_ref, xs_ref, lbuf, zbuf, sem, zsem):
    tile = pl.program_id(0)
    last = pl.num_programs(0) - 1
    slot = tile % 2
    block_pieces = RB // PIECE
    n_blocks = xs_ref.shape[0] // block_pieces

    def spare_blocks(fn):
        def per_block(b, carry):
            fn(b)
            return carry
        lax.fori_loop(nv_ref[0], n_blocks, per_block, 0)

    def zero_block(b):
        return pltpu.make_async_copy(zbuf, xs_ref.at[pl.ds(b * block_pieces, block_pieces)], zsem)

    def tails(fn):
        def per_expert(e, carry):
            n = ntail_ref[e]
            for bit in range((RB // PIECE - 1).bit_length()):
                @pl.when((lax.shift_right_logical(n, bit) & 1) == 1)
                def _():
                    done = n & ((1 << bit) - 1)
                    fn(_piece_copy(zbuf, 0, xs_ref, tail_ref[e] + done, zsem, 1 << bit))
            return carry
        lax.fori_loop(0, N_EXPERTS, per_expert, 0)

    def start(t, s):
        _for_each_copy(
            copies_ref, counts_ref, t,
            lambda lp, gp, n: _piece_copy(lbuf.at[s], lp, xs_ref, gp, sem.at[s], n).start())

    def wait(t, s):
        _wait_pieces(_tile_pieces(counts_ref, t),
                     lambda n: _piece_copy(lbuf.at[s], 0, xs_ref, 0, sem.at[s], n))

    @pl.when(tile == 0)
    def _():
        zbuf[...] = jnp.zeros_like(zbuf)
        tails(lambda copy: copy.start())
        spare_blocks(lambda b: zero_block(b).start())

    @pl.when(tile >= 2)
    def _():
        wait(tile - 2, slot)

    onehot = _onehot_chunks(infok_ref, weighted=False)
    xb = h2_ref[...]
    used_rows = PIECE * _tile_pieces(counts_ref, tile)

    def sort_chunk(j):
        lbuf[slot, j * LCHUNK // PIECE:(j + 1) * LCHUNK // PIECE] = _as_pieces(
            _dot(onehot(j), xb).astype(BF16))

    n_chunks = LROWS // LCHUNK
    for j in range(n_chunks - 1):
        sort_chunk(j)
    pl.when(used_rows > (n_chunks - 1) * LCHUNK)(functools.partial(sort_chunk, n_chunks - 1))

    start(tile, slot)

    @pl.when(tile == last)
    def _():
        @pl.when(tile >= 1)
        def _():
            wait(tile - 1, 1 - slot)
        wait(tile, slot)
        tails(lambda copy: copy.wait())
        spare_blocks(lambda b: zero_block(b).wait())


def _dispatch(tables, h2, infok, n_rows):
    t = h2.shape[0]
    return pl.pallas_call(
        _dispatch_kernel,
        name="dispatch",
        grid_spec=pltpu.PrefetchScalarGridSpec(
            num_scalar_prefetch=5,
            grid=(t // TW,),
            in_specs=[
                pl.BlockSpec((TW, D_MODEL), lambda i, *_: (i, 0)),
                pl.BlockSpec((2 * TOP_K, TW), lambda i, *_: (0, i)),
            ],
            out_specs=pl.BlockSpec(memory_space=pl.ANY),
            scratch_shapes=[
                pltpu.VMEM((2, LROWS // PIECE, PIECE, D_MODEL), BF16),
                pltpu.VMEM((RB // PIECE, PIECE, D_MODEL), BF16),
                pltpu.SemaphoreType.DMA((2,)),
                pltpu.SemaphoreType.DMA(()),
            ],
        ),
        out_shape=jax.ShapeDtypeStruct((n_rows // PIECE, PIECE, D_MODEL), BF16),
        compiler_params=pltpu.CompilerParams(
            dimension_semantics=("arbitrary",), vmem_limit_bytes=VMEM_LIMIT),
    )(*tables, h2, infok)


def _experts_kernel(be_ref, nv_ref, xs_ref, wg_ref, wu_ref, wd_ref, ys_ref, wgb, wub, wdb):
    i = pl.program_id(0)

    @pl.when(i < nv_ref[0])
    def _():
        @pl.when((i == 0) | (be_ref[i] != be_ref[jnp.maximum(i - 1, 0)]))
        def _():
            wgb[...] = wg_ref[0].astype(BF16)
            wub[...] = wu_ref[0].astype(BF16)
            wdb[...] = wd_ref[0].astype(BF16)

        for j in range(RB // RSUB):
            rows = slice(j * RSUB, (j + 1) * RSUB)
            h = _swiglu(xs_ref[rows, :], wgb[...], wub[...])
            ys_ref[rows, :] = _dot(h.astype(BF16), wdb[...]).astype(BF16)


def _experts(block_expert, n_valid, xs, wg, wu, wd):
    n_blocks = xs.shape[0] // RB
    rows = pl.BlockSpec((RB, D_MODEL), lambda i, be, nv: (jnp.minimum(i, nv[0] - 1), 0))
    return pl.pallas_call(
        _experts_kernel,
        name="experts",
        grid_spec=pltpu.PrefetchScalarGridSpec(
            num_scalar_prefetch=2,
            grid=(n_blocks,),
            in_specs=[
                rows,
                pl.BlockSpec((1, D_MODEL, D_EXPERT), lambda i, be, nv: (be[i], 0, 0)),
                pl.BlockSpec((1, D_MODEL, D_EXPERT), lambda i, be, nv: (be[i], 0, 0)),
                pl.BlockSpec((1, D_EXPERT, D_MODEL), lambda i, be, nv: (be[i], 0, 0)),
            ],
            out_specs=rows,
            scratch_shapes=[
                pltpu.VMEM((D_MODEL, D_EXPERT), BF16),
                pltpu.VMEM((D_MODEL, D_EXPERT), BF16),
                pltpu.VMEM((D_EXPERT, D_MODEL), BF16),
            ],
        ),
        out_shape=jax.ShapeDtypeStruct(xs.shape, BF16),
        input_output_aliases={2: 0},
        compiler_params=pltpu.CompilerParams(
            dimension_semantics=("arbitrary",), vmem_limit_bytes=VMEM_LIMIT),
    )(block_expert, n_valid, xs, wg, wu, wd)


def _combine_kernel(copies_ref, counts_ref,
                    infok_ref, h2_ref, x1_ref, gate_ref, gpost_ref, wsg_ref, wsu_ref, wsd_ref,
                    ys_ref, o_ref, ybuf, acc_ref, sem):
    tile = pl.program_id(0)
    last = pl.num_programs(0) - 1
    slot = tile % 2

    def start(t, s):
        _for_each_copy(
            copies_ref, counts_ref, t,
            lambda lp, gp, n: _piece_copy(ys_ref, gp, ybuf.at[s], lp, sem.at[s], n).start())

    def wait(t, s):
        _wait_pieces(_tile_pieces(counts_ref, t),
                     lambda n: _piece_copy(ys_ref, 0, ybuf.at[s], 0, sem.at[s], n))

    @pl.when(tile == 0)
    def _():
        ybuf[...] = jnp.zeros_like(ybuf)
        start(tile, slot)

    @pl.when(tile < last)
    def _():
        start(tile + 1, 1 - slot)

    wait(tile, slot)

    weights = _onehot_chunks(infok_ref, weighted=True)
    hs = _swiglu(h2_ref[...], wsg_ref[...], wsu_ref[...])
    ff = _dot(hs.astype(BF16), wsd_ref[...])
    used_rows = PIECE * _tile_pieces(counts_ref, tile)

    def sum_chunk(j):
        rows = ybuf[slot, j * LCHUNK // PIECE:(j + 1) * LCHUNK // PIECE].reshape(LCHUNK, D_MODEL)
        return lax.dot_general(weights(j), rows,
                               (((0,), (0,)), ((), ())), preferred_element_type=F32)

    def add_chunk(j):
        acc_ref[...] += sum_chunk(j)

    n_chunks = LROWS // LCHUNK
    for j in range(n_chunks - 1):
        ff = ff + sum_chunk(j)
    acc_ref[...] = ff
    pl.when(used_rows > (n_chunks - 1) * LCHUNK)(functools.partial(add_chunk, n_chunks - 1))
    o_ref[...] = x1_ref[...] + gate_ref[0] * _rms(acc_ref[...], gpost_ref[...])


def _combine(tables, infok, h2, x1, gate, gpost, wsg, wsu, wsd, ys, seq):
    t = h2.shape[0]
    per_seq = seq // TW
    full = lambda shape: pl.BlockSpec(shape, lambda i, *_: (0,) * len(shape))
    tok = lambda width: pl.BlockSpec((TW, width), lambda i, *_: (i, 0))
    return pl.pallas_call(
        _combine_kernel,
        name="combine",
        grid_spec=pltpu.PrefetchScalarGridSpec(
            num_scalar_prefetch=2,
            grid=(t // TW,),
            in_specs=[
                pl.BlockSpec((2 * TOP_K, TW), lambda i, *_: (0, i)), tok(D_MODEL), tok(D_MODEL),
                pl.BlockSpec((1, 1, D_MODEL), lambda i, *_: (i // per_seq, 0, 0)),
                full((1, D_MODEL)),
                full((D_MODEL, D_SHARED)), full((D_MODEL, D_SHARED)), full((D_SHARED, D_MODEL)),
                pl.BlockSpec(memory_space=pl.ANY),
            ],
            out_specs=tok(D_MODEL),
            scratch_shapes=[
                pltpu.VMEM((2, LROWS // PIECE, PIECE, D_MODEL), BF16),
                pltpu.VMEM((TW, D_MODEL), F32),
                pltpu.SemaphoreType.DMA((2,)),
            ],
        ),
        out_shape=jax.ShapeDtypeStruct((t, D_MODEL), F32),
        compiler_params=pltpu.CompilerParams(
            dimension_semantics=("arbitrary",), vmem_limit_bytes=VMEM_LIMIT),
    )(*tables, infok, h2, x1, gate, gpost, wsg, wsu, wsd, ys)


def _dispatch_plan(cnt, n_tiles):
    n = cnt[:, :n_tiles].astype(jnp.int32)
    pieces = (n + PIECE - 1) // PIECE
    local = jnp.cumsum(pieces, axis=0) - pieces
    seg = jnp.sum(pieces, axis=1)
    per_block = RB // PIECE
    seg_pad = (seg + per_block - 1) // per_block * per_block
    seg_end = jnp.cumsum(seg_pad)
    seg_start = seg_end - seg_pad
    glob = seg_start[:, None] + jnp.cumsum(pieces, axis=1) - pieces
    n_blocks = (TOP_K * TW * n_tiles + N_EXPERTS * n_tiles * (PIECE - 1)
                + N_EXPERTS * (RB - PIECE) + RB - 1) // RB
    n_valid = seg_end[-1] // per_block
    blk = jnp.minimum(jnp.arange(n_blocks, dtype=jnp.int32), n_valid - 1)
    block_end = seg_end // per_block
    block_expert = jnp.sum((block_end[None, :] <= blk[:, None]).astype(jnp.int32), axis=1)
    block_expert = jnp.minimum(block_expert, N_EXPERTS - 1)
    assert LROWS // PIECE <= 1 << LOCAL_BITS and n_blocks * per_block < 1 << (31 - LOCAL_BITS)

    def copy_list(count, width):
        end = jnp.cumsum(count, axis=0).T
        slot = jnp.arange(width, dtype=jnp.int32)
        expert = jnp.sum((end[:, None, :] <= slot[None, :, None]).astype(jnp.int32), axis=-1)
        pick = expert[:, :, None] == jnp.arange(N_EXPERTS, dtype=jnp.int32)
        take = lambda a: jnp.sum(jnp.where(pick, a.T[:, None, :], 0), axis=-1)
        return slot[None, :] - take(jnp.cumsum(count, axis=0) - count), take

    big = COPY_SIZES[0]
    assert COPY_SIZES == tuple(range(big, 0, -1))
    n_big = pieces // big
    rest = pieces % big
    within, take = copy_list(n_big, MAX_COPIES)
    lists = [(take(local) + big * within) | ((take(glob) + big * within) << LOCAL_BITS)]
    counts = [jnp.sum(n_big, axis=0)]
    for size in COPY_SIZES[1:]:
        has = (rest == size).astype(jnp.int32)
        _, take = copy_list(has, MAX_COPIES)
        lists.append(take(local + big * n_big) | (take(glob + big * n_big) << LOCAL_BITS))
        counts.append(jnp.sum(has, axis=0))
    counts.append(jnp.sum(pieces, axis=0))
    tables = (jnp.stack(lists, axis=1).reshape(-1).astype(jnp.int32),
              jnp.stack(counts, axis=1).reshape(-1).astype(jnp.int32))
    tails = ((seg_start + seg).astype(jnp.int32), (seg_pad - seg).astype(jnp.int32))
    return tables, tails, block_expert, n_valid.reshape(1).astype(jnp.int32), n_blocks * RB


def kernel(x, c, w_ada, b_ada, g_pre_mix, g_post_mix, g_pre_ffn, g_post_ffn, w_in, b_forget,
           w_pool, pool_scale, g_pool_out, g_attn_out, w_out, w_router, router_bias,
           w_gate, w_up, w_down, ws_gate, ws_up, ws_down):
    bsz, seq, d = x.shape
    depth = w_ada.shape[0]
    for l in range(depth):
        mod = _ada(c, w_ada, b_ada[l][None, :], l)
        shift_m, scale_m, gate_m, shift_f, scale_f, gate_f = [
            m.reshape(bsz, 1, d) for m in jnp.split(mod, 6, axis=-1)]

        bfp = jnp.pad(b_forget[l], (0, LANES - N_HEADS))[None, :]
        pool, qa, kt, v = _premix(
            x, shift_m, scale_m, g_pre_mix[l][None, :], w_in, l, bfp, w_pool[l].astype(BF16),
            pool_scale[l][None, :], g_pool_out[l][None, :])
        ot = _attn(qa, kt, v)

        wr = jnp.pad(w_router[l], ((0, 0), (0, LANES - N_EXPERTS)))
        wrh = wr.astype(BF16)
        wrl = (wr - wrh.astype(F32)).astype(BF16)
        t = bsz * seq
        assert t // TW <= LANES
        x1, h2, infok, cnt = _postmix(
            x, pool, ot, w_out[l].astype(BF16), g_attn_out[l][None, :], g_post_mix[l][None, :],
            gate_m, g_pre_ffn[l][None, :], shift_f, scale_f,
            jnp.concatenate([wrh, wrl], axis=1), router_bias[l][:, None])
        tables, tails, block_expert, n_valid, n_rows = _dispatch_plan(cnt, t // TW)
        h2f = h2.reshape(t, d)
        xs = _dispatch(tables + tails + (n_valid,), h2f, infok, n_rows)
        ys = _experts(block_expert, n_valid, xs.reshape(n_rows, d), w_gate[l], w_up[l], w_down[l])
        ys = ys.reshape(xs.shape)
        out = _combine(tables, infok, h2f, x1.reshape(t, d), gate_f, g_post_ffn[l][None, :],
                       ws_gate[l].astype(BF16), ws_up[l].astype(BF16), ws_down[l].astype(BF16),
                       ys, seq)
        x = out.reshape(bsz, seq, d)
    return x
```

```python
import functools

import numpy as np
import jax
import jax.numpy as jnp
from jax import lax
from jax.experimental import pallas as pl
from jax.experimental.pallas import tpu as pltpu

D_MODEL = 1024
D_POOL = 512
POOL_WINDOWS = (2, 4, 8, 16)
POOL_GROUP = 128
MAX_WINDOW = max(POOL_WINDOWS)
D_ATTN = 512
HEAD_DIM = 64
N_HEADS = 8
N_EXPERTS = 64
N_EXPERT_GROUPS = 8
GROUP_SIZE = N_EXPERTS // N_EXPERT_GROUPS
TOPK_GROUPS = 4
TOP_K = 8
D_EXPERT = 256
D_SHARED = 256
ROUTED_SCALE = 2.5
EPS = 1e-6

LOG2E = 1.4426950408889634
LANES = 128
N_SPLIT = 3
AUG = LANES

TS_PRE = 1024
CUM_BLOCK = 256
TS_POST = 1024
TQ = 512
TK = 512
HEAD_PAIR = 128 // HEAD_DIM
TW = 256
PIECE = 16
RB = 1024
RSUB = 512
LCHUNK = 512
ONEHOT_GROUP = 64
LROWS = -(-(TOP_K * TW + N_EXPERTS * (PIECE - 1)) // LCHUNK) * LCHUNK

F32 = jnp.float32
BF16 = jnp.bfloat16
VMEM_LIMIT = 56 * 1024 * 1024


def _rms(v, g):
    return v * lax.rsqrt(jnp.mean(v * v, axis=-1, keepdims=True) + EPS) * g


def _split3(v):
    hi = v.astype(BF16)
    r1 = v - hi.astype(F32)
    mid = r1.astype(BF16)
    r2 = r1 - mid.astype(F32)
    lo = r2.astype(BF16)
    return hi, mid, lo


def _dot(a, b):
    return jnp.dot(a, b, preferred_element_type=F32)


def _ada_kernel(c_ref, w_ref, b_ref, o_ref):
    o_ref[...] = _dot(c_ref[...].astype(BF16), w_ref[0].astype(BF16)) + b_ref[...]


def _ada(c, w, b, layer):
    bsz = c.shape[0]
    n = w.shape[2]
    return pl.pallas_call(
        _ada_kernel,
        name="ada",
        grid=(n // D_MODEL,),
        in_specs=[
            pl.BlockSpec((bsz, D_MODEL), lambda j: (0, 0)),
            pl.BlockSpec((1, D_MODEL, D_MODEL), lambda j: (layer, 0, j)),
            pl.BlockSpec((1, D_MODEL), lambda j: (0, j)),
        ],
        out_specs=pl.BlockSpec((bsz, D_MODEL), lambda j: (0, j)),
        out_shape=jax.ShapeDtypeStruct((bsz, n), F32),
    )(c, w, b)


def _premix_kernel(x_ref, shift_ref, scale_ref, g_ref, win_ref, bf_ref, wpool_ref,
                   pscale_ref, gpool_ref, place_ref, ones_ref, fmask_ref,
                   pool_ref, q_ref, kt_ref, v_ref,
                   uext_ref, cum_ref, w1_ref, wf_ref):
    s = pl.program_id(1)
    ts = x_ref.shape[1]

    @pl.when((pl.program_id(0) == 0) & (s == 0))
    def _():
        q0, k0 = D_POOL, D_POOL + D_ATTN
        w1_ref[:, :q0] = win_ref[0, :, :q0].astype(BF16)
        w1_ref[:, q0:k0] = (win_ref[0, :, q0:k0] * (HEAD_DIM ** -0.5 * LOG2E)).astype(BF16)
        w1_ref[:, k0:] = win_ref[0, :, k0:k0 + 2 * D_ATTN].astype(BF16)
        wf_ref[...] = jnp.zeros_like(wf_ref)
        wf_ref[:, :N_HEADS] = win_ref[0, :, k0 + 2 * D_ATTN:].astype(BF16)

    @pl.when(s == 0)
    def _():
        uext_ref[0:MAX_WINDOW, :] = jnp.zeros((MAX_WINDOW, D_POOL), F32)
        cum_ref[...] = jnp.zeros_like(cum_ref)

    x = x_ref[0]
    h = _rms(x, g_ref[...]) * (1.0 + scale_ref[0]) + shift_ref[0]
    hb = h.astype(BF16)
    proj = _dot(hb, w1_ref[...])
    u = proj[:, :D_POOL]
    q = proj[:, D_POOL:D_POOL + D_ATTN]
    k = proj[:, D_POOL + D_ATTN:D_POOL + 2 * D_ATTN]
    v = proj[:, D_POOL + 2 * D_ATTN:]

    uext_ref[MAX_WINDOW:, :] = u
    pos = (s * ts + lax.broadcasted_iota(jnp.int32, (ts, 1), 0) + 1).astype(F32)
    ys = []
    for g, w in enumerate(POOL_WINDOWS):
        c0 = g * POOL_GROUP
        acc = uext_ref[MAX_WINDOW:, c0:c0 + POOL_GROUP]
        for j in range(1, w):
            acc = acc + uext_ref[MAX_WINDOW - j:MAX_WINDOW - j + ts, c0:c0 + POOL_GROUP]
        pooled = acc / jnp.minimum(pos, float(w)) - u[:, c0:c0 + POOL_GROUP]
        ys.append(_dot(pooled.astype(BF16), wpool_ref[g]))
    ypool = jnp.concatenate(ys, axis=1) * pscale_ref[...]
    pool_ref[0] = _rms(ypool, gpool_ref[...]).astype(BF16)
    uext_ref[0:MAX_WINDOW, :] = uext_ref[ts:ts + MAX_WINDOW, :]

    z0 = _dot(hb, wf_ref[...]) + bf_ref[...]
    z = z0
    for p in range(1, N_SPLIT):
        z = z + pltpu.roll(z0, p * N_HEADS, 1)
    logf = jnp.minimum(z, 0.0) - jnp.log1p(jnp.exp(-jnp.abs(z)))
    row = lax.broadcasted_iota(jnp.int32, (CUM_BLOCK, CUM_BLOCK), 0)
    col = lax.broadcasted_iota(jnp.int32, (CUM_BLOCK, CUM_BLOCK), 1)
    tri = (col <= row).astype(BF16)
    pieces = _split3(logf)
    carry = cum_ref[...]
    blocks = []
    for r0 in range(0, ts, CUM_BLOCK):
        blk = carry
        for piece in pieces:
            blk = blk + _dot(tri, piece[r0:r0 + CUM_BLOCK, :])
        carry = blk[CUM_BLOCK - 1:CUM_BLOCK, :]
        blocks.append(blk)
    cum = jnp.concatenate(blocks, axis=0)
    cum_ref[...] = carry

    hi, mid, lo = [piece.astype(F32) for piece in _split3(cum * LOG2E)]
    lane = lax.broadcasted_iota(jnp.int32, (ts, LANES), 1)
    pieces = jnp.where(lane < N_HEADS, hi, jnp.where(lane < 2 * N_HEADS, mid, lo))
    aug = _dot(pieces.astype(BF16), place_ref[...]) + ones_ref[...]
    aug_q = aug[:, :N_HEADS * AUG]
    aug_k = aug[:, N_HEADS * AUG:]

    def expand(a):
        blocks = []
        for j in range(D_ATTN // LANES):
            blk = a[:, j * LANES:(j + 1) * LANES]
            blocks += [blk, blk]
        return jnp.concatenate(blocks, axis=1)

    fmask = fmask_ref[...]
    qa = expand(q) * fmask + aug_q
    ka = expand(k) * fmask + aug_k
    q_ref[0] = qa.astype(BF16)
    kt_ref[0] = ka.T.astype(BF16)
    v_ref[0] = v.astype(BF16)


def _aug_constants():
    width = N_HEADS * AUG
    place = np.zeros((LANES, 2 * width), np.float32)
    ones = np.zeros((1, 2 * width), np.float32)
    fmask = np.zeros((1, width), np.float32)
    for h in range(N_HEADS):
        feat0 = h * AUG + (0 if h % 2 == 0 else HEAD_DIM)
        aug0 = h * AUG + (HEAD_DIM if h % 2 == 0 else 0)
        fmask[0, feat0:feat0 + HEAD_DIM] = 1.0
        for p in range(N_SPLIT):
            place[p * N_HEADS + h, aug0 + p] = 1.0
            ones[0, width + aug0 + p] = 1.0
            place[p * N_HEADS + h, width + aug0 + N_SPLIT + p] = -1.0
            ones[0, aug0 + N_SPLIT + p] = 1.0
    return jnp.asarray(place, BF16), jnp.asarray(ones), jnp.asarray(fmask)


def _premix(x, shift, scale, g, w_in, layer, bfp, wpool, pscale, gpool):
    bsz, seq, _ = x.shape
    d_in = w_in.shape[2]
    assert d_in == D_POOL + 3 * D_ATTN + N_HEADS
    ts = TS_PRE
    place, ones, fmask = _aug_constants()
    full = lambda shape: pl.BlockSpec(shape, lambda b, s: (0,) * len(shape))
    per_batch = pl.BlockSpec((1, 1, D_MODEL), lambda b, s: (b, 0, 0))
    return pl.pallas_call(
        _premix_kernel,
        name="premix",
        grid=(bsz, seq // ts),
        in_specs=[
            pl.BlockSpec((1, ts, D_MODEL), lambda b, s: (b, s, 0)),
            per_batch, per_batch,
            full((1, D_MODEL)),
            pl.BlockSpec((1, D_MODEL, d_in), lambda b, s: (layer, 0, 0),
                         pipeline_mode=pl.Buffered(1)),
            full((1, LANES)),
            full((len(POOL_WINDOWS), POOL_GROUP, POOL_GROUP)),
            full((1, D_POOL)),
            full((1, D_POOL)),
            full((LANES, 2 * N_HEADS * AUG)),
            full((1, 2 * N_HEADS * AUG)),
            full((1, N_HEADS * AUG)),
        ],
        out_specs=[
            pl.BlockSpec((1, ts, D_POOL), lambda b, s: (b, s, 0)),
            pl.BlockSpec((1, ts, N_HEADS * AUG), lambda b, s: (b, s, 0)),
            pl.BlockSpec((1, N_HEADS * AUG, ts), lambda b, s: (b, 0, s)),
            pl.BlockSpec((1, ts, D_ATTN), lambda b, s: (b, s, 0)),
        ],
        out_shape=[
            jax.ShapeDtypeStruct((bsz, seq, D_POOL), BF16),
            jax.ShapeDtypeStruct((bsz, seq, N_HEADS * AUG), BF16),
            jax.ShapeDtypeStruct((bsz, N_HEADS * AUG, seq), BF16),
            jax.ShapeDtypeStruct((bsz, seq, D_ATTN), BF16),
        ],
        scratch_shapes=[
            pltpu.VMEM((ts + MAX_WINDOW, D_POOL), F32),
            pltpu.VMEM((1, LANES), F32),
            pltpu.VMEM((D_MODEL, D_POOL + 3 * D_ATTN), BF16),
            pltpu.VMEM((D_MODEL, LANES), BF16),
        ],
        compiler_params=pltpu.CompilerParams(
            dimension_semantics=("arbitrary", "arbitrary"), vmem_limit_bytes=VMEM_LIMIT),
    )(x, shift, scale, g, w_in, bfp, wpool, pscale, gpool, place, ones, fmask)


def _attn_kernel(q_ref, kt_ref, v_ref, o_ref):
    seq = q_ref.shape[1]
    qry_i = lax.broadcasted_iota(jnp.int32, (TQ, TK), 0)
    key_i = lax.broadcasted_iota(jnp.int32, (TQ, TK), 1)
    causal = key_i <= qry_i
    one_lane = lax.broadcasted_iota(jnp.int32, (seq, LANES), 1) == 0
    vaug = jnp.concatenate([v_ref[0], jnp.where(one_lane, 1.0, 0.0).astype(BF16)], axis=1)
    out_lane = lax.broadcasted_iota(jnp.int32, (TQ, LANES), 1)
    heads = range(HEAD_PAIR)
    for qi in range(seq // TQ):
        rows = slice(qi * TQ, (qi + 1) * TQ)
        q = [q_ref[0, rows, h * AUG:(h + 1) * AUG] for h in heads]
        m = [jnp.full((TQ, 1), -jnp.inf, F32) for _ in heads]
        acc = [jnp.zeros((TQ, 2 * LANES), F32) for _ in heads]
        for kj in range(qi + 1):
            keys = slice(kj * TK, (kj + 1) * TK)
            for h in heads:
                s = _dot(q[h], kt_ref[0, h * AUG:(h + 1) * AUG, keys])
                if kj == qi:
                    s = jnp.where(causal, s, -jnp.inf)
                m_new = jnp.maximum(m[h], jnp.max(s, axis=1, keepdims=True))
                p = jnp.exp2(s - m_new).astype(BF16)
                acc[h] = jnp.exp2(m[h] - m_new) * acc[h] + _dot(p, vaug[keys, :])
                m[h] = m_new
        o = [acc[h][:, :LANES] / acc[h][:, LANES:LANES + 1] for h in heads]
        o_ref[0, rows, :] = jnp.where(out_lane < HEAD_DIM, o[0], o[1])


def _attn(q, kt, v):
    bsz, seq, _ = q.shape
    return pl.pallas_call(
        _attn_kernel,
        name="attn",
        grid=(bsz, N_HEADS // HEAD_PAIR),
        in_specs=[
            pl.BlockSpec((1, seq, HEAD_PAIR * AUG), lambda b, h: (b, 0, h)),
            pl.BlockSpec((1, HEAD_PAIR * AUG, seq), lambda b, h: (b, h, 0)),
            pl.BlockSpec((1, seq, LANES), lambda b, h: (b, 0, h)),
        ],
        out_specs=pl.BlockSpec((1, seq, LANES), lambda b, h: (b, 0, h)),
        out_shape=jax.ShapeDtypeStruct((bsz, seq, D_ATTN), F32),
        compiler_params=pltpu.CompilerParams(
            dimension_semantics=("arbitrary", "arbitrary"), vmem_limit_bytes=VMEM_LIMIT),
    )(q, kt, v)


def _postmix_kernel(x_ref, pool_ref, o_ref, wout_ref, gattn_ref, gpost_ref, gate_ref,
                    gffn_ref, shift_ref, scale_ref, wr_ref, bias_ref,
                    x1_ref, h2_ref, infok_ref, cnt_ref):
    ya = _rms(o_ref[0], gattn_ref[...]).astype(BF16)
    mixed = _dot(pool_ref[0], wout_ref[:D_POOL, :]) + _dot(ya, wout_ref[D_POOL:, :])
    x1 = x_ref[0] + gate_ref[0] * _rms(mixed, gpost_ref[...])
    x1_ref[0] = x1
    h2 = _rms(x1, gffn_ref[...]) * (1.0 + scale_ref[0]) + shift_ref[0]
    h2b = h2.astype(BF16)
    h2_ref[0] = h2b
    h2l = (h2 - h2b.astype(F32)).astype(BF16)
    both = _dot(h2b, wr_ref[...])
    logits = both[:, :LANES] + both[:, LANES:] + _dot(h2l, wr_ref[:, :LANES])
    st = jax.nn.sigmoid(logits).T

    step = pl.program_id(0) * pl.num_programs(1) + pl.program_id(1)

    @pl.when(step == 0)
    def _():
        cnt_ref[...] = jnp.zeros_like(cnt_ref)

    lane = lax.broadcasted_iota(jnp.int32, cnt_ref.shape, 1)
    tiles = x_ref.shape[1] // TW
    for i in range(tiles):
        info, n = _route_tile(st[:, i * TW:(i + 1) * TW], bias_ref)
        infok_ref[:, i * TW:(i + 1) * TW] = info
        cnt_ref[...] = jnp.where(lane == step * tiles + i, jnp.broadcast_to(n, cnt_ref.shape),
                                 cnt_ref[...])


def _postmix(x, pool, ot, wout, gattn, gpost, gate, gffn, shift, scale, wr, bias_col):
    bsz, seq, _ = x.shape
    ts = TS_POST
    ns = seq // ts
    full = lambda shape: pl.BlockSpec(shape, lambda b, s: (0,) * len(shape))
    per_batch = pl.BlockSpec((1, 1, D_MODEL), lambda b, s: (b, 0, 0))
    tok = lambda width: pl.BlockSpec((1, ts, width), lambda b, s: (b, s, 0))
    return pl.pallas_call(
        _postmix_kernel,
        name="postmix",
        grid=(bsz, ns),
        in_specs=[
            tok(D_MODEL), tok(D_POOL), tok(D_ATTN),
            full((D_MODEL, D_MODEL)),
            full((1, D_ATTN)), full((1, D_MODEL)), per_batch,
            full((1, D_MODEL)), per_batch, per_batch,
            full((D_MODEL, 2 * LANES)),
            full((N_EXPERTS, 1)),
        ],
        out_specs=[
            tok(D_MODEL), tok(D_MODEL),
            pl.BlockSpec((2 * TOP_K, ts), lambda b, s: (0, b * ns + s)),
            full((N_EXPERTS, LANES)),
        ],
        out_shape=[
            jax.ShapeDtypeStruct((bsz, seq, D_MODEL), F32),
            jax.ShapeDtypeStruct((bsz, seq, D_MODEL), BF16),
            jax.ShapeDtypeStruct((2 * TOP_K, bsz * seq), F32),
            jax.ShapeDtypeStruct((N_EXPERTS, LANES), F32),
        ],
        compiler_params=pltpu.CompilerParams(
            dimension_semantics=("arbitrary", "arbitrary"), vmem_limit_bytes=VMEM_LIMIT),
    )(x, pool, ot, wout, gattn, gpost, gate, gffn, shift, scale, wr, bias_col)


def _route_tile(st, bias_ref):
    tr = st.shape[1]
    scores = [st[g * GROUP_SIZE:(g + 1) * GROUP_SIZE, :] for g in range(N_EXPERT_GROUPS)]
    sel = [scores[g] + bias_ref[g * GROUP_SIZE:(g + 1) * GROUP_SIZE, :]
           for g in range(N_EXPERT_GROUPS)]
    sub = lax.broadcasted_iota(jnp.int32, (GROUP_SIZE, tr), 0)
    neg = jnp.float32(-jnp.inf)

    rows = []
    for g in range(N_EXPERT_GROUPS):
        v = sel[g]
        m1 = jnp.max(v, axis=0, keepdims=True)
        first = jnp.min(jnp.where(v == m1, sub, GROUP_SIZE), axis=0, keepdims=True)
        m2 = jnp.max(jnp.where(sub == first, neg, v), axis=0, keepdims=True)
        rows.append(m1 + m2)
    gs = jnp.concatenate(rows, axis=0)

    shape = (GROUP_SIZE, tr)
    one = jnp.ones(shape, jnp.int32)
    zero = jnp.zeros(shape, jnp.int32)
    beaten = zero
    for gp in range(N_EXPERT_GROUPS):
        r = jnp.broadcast_to(gs[gp:gp + 1, :], shape)
        tie = jnp.where(sub > gp, one, zero)
        beaten = beaten + jnp.where(r > gs, one, zero) + jnp.where(r == gs, tie, zero)
    keep = jnp.where(beaten < TOPK_GROUPS, 1.0, 0.0)

    masked = [jnp.where(jnp.broadcast_to(keep[g:g + 1, :], shape) > 0.5, sel[g], neg)
              for g in range(N_EXPERT_GROUPS)]

    eidx = [sub + g * GROUP_SIZE for g in range(N_EXPERT_GROUPS)]
    picked = [zero for _ in range(N_EXPERT_GROUPS)]
    for _ in range(TOP_K):
        best = masked[0]
        for g in range(1, N_EXPERT_GROUPS):
            best = jnp.maximum(best, masked[g])
        best = jnp.broadcast_to(jnp.max(best, axis=0, keepdims=True), shape)
        first = jnp.where(masked[0] == best, eidx[0], N_EXPERTS)
        for g in range(1, N_EXPERT_GROUPS):
            first = jnp.minimum(first, jnp.where(masked[g] == best, eidx[g], N_EXPERTS))
        first = jnp.broadcast_to(jnp.min(first, axis=0, keepdims=True), shape)
        for g in range(N_EXPERT_GROUPS):
            hit = eidx[g] == first
            picked[g] = jnp.where(hit, one, picked[g])
            masked[g] = jnp.where(hit, neg, masked[g])
    chosen = [picked[g] > 0 for g in range(N_EXPERT_GROUPS)]

    w = [jnp.where(chosen[g], scores[g], 0.0) for g in range(N_EXPERT_GROUPS)]
    denom = w[0].sum(axis=0, keepdims=True)
    for g in range(1, N_EXPERT_GROUPS):
        denom = denom + w[g].sum(axis=0, keepdims=True)
    wfull = jnp.concatenate([w[g] / denom * ROUTED_SCALE for g in range(N_EXPERT_GROUPS)], axis=0)
    mfull = jnp.concatenate([jnp.where(chosen[g], 1.0, 0.0) for g in range(N_EXPERT_GROUPS)],
                            axis=0)
    mb = mfull.astype(BF16)

    e_r = lax.broadcasted_iota(jnp.int32, (N_EXPERTS, N_EXPERTS), 0)
    e_c = lax.broadcasted_iota(jnp.int32, (N_EXPERTS, N_EXPERTS), 1)
    before_e = (e_c < e_r).astype(BF16)
    t_r = lax.broadcasted_iota(jnp.int32, (tr, tr), 0)
    t_c = lax.broadcasted_iota(jnp.int32, (tr, tr), 1)
    before_t = (t_r < t_c).astype(BF16)
    ordinal = _dot(before_e, mb)
    rank = _dot(mb, before_t)
    n = jnp.sum(mfull, axis=1, keepdims=True)
    pieces = jnp.floor((n + (PIECE - 1.0)) * (1.0 / PIECE))
    run_start = PIECE * _dot(before_e, jnp.broadcast_to(pieces, (N_EXPERTS, LANES)).astype(BF16))
    pos = run_start[:, 0:1] + rank

    sub8 = lax.broadcasted_iota(jnp.int32, (TOP_K, tr), 0)
    pos8 = jnp.zeros((TOP_K, tr), F32)
    w8 = jnp.zeros((TOP_K, tr), F32)
    for k in range(TOP_K):
        selk = jnp.where(ordinal == float(k), mfull, 0.0)
        pk = jnp.sum(selk * pos, axis=0, keepdims=True)
        wk = jnp.sum(selk * wfull, axis=0, keepdims=True)
        pos8 = jnp.where(sub8 == k, jnp.broadcast_to(pk, (TOP_K, tr)), pos8)
        w8 = jnp.where(sub8 == k, jnp.broadcast_to(wk, (TOP_K, tr)), w8)
    return jnp.concatenate([pos8, w8], axis=0), n


def _swiglu(xb, wg, wu):
    g = _dot(xb, wg)
    return (g * jax.nn.sigmoid(g)) * _dot(xb, wu)


def _piece_copy(src_ref, src_piece, dst_ref, dst_piece, sem, n_pieces=1):
    src = src_ref.at[pl.ds(src_piece, n_pieces)]
    dst = dst_ref.at[pl.ds(dst_piece, n_pieces)]
    return pltpu.make_async_copy(src, dst, sem)


def _as_pieces(rows):
    return rows.reshape(rows.shape[0] // PIECE, PIECE, rows.shape[1])


LOCAL_BITS = 8
COPY_SIZES = (4, 3, 2, 1)
MAX_COPIES = N_EXPERTS
assert LROWS // (COPY_SIZES[0] * PIECE) <= MAX_COPIES


def _for_each_copy(copies_ref, counts_ref, tile, fn):
    for c, size in enumerate(COPY_SIZES):
        base = (tile * len(COPY_SIZES) + c) * MAX_COPIES

        def body(i, carry, base=base, size=size):
            word = copies_ref[base + i]
            fn(word & ((1 << LOCAL_BITS) - 1), lax.shift_right_logical(word, LOCAL_BITS), size)
            return carry

        lax.fori_loop(0, counts_ref[tile * (len(COPY_SIZES) + 1) + c], body, 0)


def _tile_pieces(counts_ref, tile):
    return counts_ref[tile * (len(COPY_SIZES) + 1) + len(COPY_SIZES)]


def _wait_pieces(total, copy_of):
    for bit in range((LROWS // PIECE).bit_length()):
        @pl.when((lax.shift_right_logical(total, bit) & 1) == 1)
        def _():
            copy_of(1 << bit).wait()


def _onehot_chunks(infok_ref, weighted):
    pos = infok_ref[0:TOP_K, :]
    grp = jnp.floor(pos * (1.0 / ONEHOT_GROUP))
    off = pos - ONEHOT_GROUP * grp
    sub = lax.broadcasted_iota(jnp.int32, (ONEHOT_GROUP, TW), 0).astype(F32)
    offs, grps = [], []
    for k in range(TOP_K):
        value = infok_ref[TOP_K + k:TOP_K + k + 1, :] if weighted else 1.0
        offs.append(jnp.where(sub == off[k:k + 1, :], value, 0.0).astype(BF16))
        grps.append(jnp.broadcast_to(grp[k:k + 1, :], (PIECE, TW)).astype(BF16))
    zero = jnp.zeros((PIECE, TW), BF16)

    def chunk(j):
        tiles = []
        for a in range(j * LCHUNK // ONEHOT_GROUP, (j + 1) * LCHUNK // ONEHOT_GROUP):
            hit = [grps[k] == a for k in range(TOP_K)]
            for i in range(ONEHOT_GROUP // PIECE):
                rows = zero
                for k in range(TOP_K):
                    rows = rows + jnp.where(hit[k], offs[k][i * PIECE:(i + 1) * PIECE, :], zero)
                tiles.append(rows)
        return jnp.concatenate(tiles, axis=0)

    return chunk


def _dispatch_kernel(copies_ref, counts_ref, tail_ref, ntail_ref, nv_ref,
                     h2_ref, infok_ref, xs_ref, lbuf, zbuf, sem, zsem):
    tile = pl.program_id(0)
    last = pl.num_programs(0) - 1
    slot = tile % 2
    block_pieces = RB // PIECE
    n_blocks = xs_ref.shape[0] // block_pieces

    def spare_blocks(fn):
        def per_block(b, carry):
            fn(b)
            return carry
        lax.fori_loop(nv_ref[0], n_blocks, per_block, 0)

    def zero_block(b):
        return pltpu.make_async_copy(zbuf, xs_ref.at[pl.ds(b * block_pieces, block_pieces)], zsem)

    def tails(fn):
        def per_expert(e, carry):
            n = ntail_ref[e]
            for bit in range((RB // PIECE - 1).bit_length()):
                @pl.when((lax.shift_right_logical(n, bit) & 1) == 1)
                def _():
                    done = n & ((1 << bit) - 1)
                    fn(_piece_copy(zbuf, 0, xs_ref, tail_ref[e] + done, zsem, 1 << bit))
            return carry
        lax.fori_loop(0, N_EXPERTS, per_expert, 0)

    def start(t, s):
        _for_each_copy(
            copies_ref, counts_ref, t,
            lambda lp, gp, n: _piece_copy(lbuf.at[s], lp, xs_ref, gp, sem.at[s], n).start())

    def wait(t, s):
        _wait_pieces(_tile_pieces(counts_ref, t),
                     lambda n: _piece_copy(lbuf.at[s], 0, xs_ref, 0, sem.at[s], n))

    @pl.when(tile == 0)
    def _():
        zbuf[...] = jnp.zeros_like(zbuf)
        tails(lambda copy: copy.start())
        spare_blocks(lambda b: zero_block(b).start())

    @pl.when(tile >= 2)
    def _():
        wait(tile - 2, slot)

    onehot = _onehot_chunks(infok_ref, weighted=False)
    xb = h2_ref[...]
    used_rows = PIECE * _tile_pieces(counts_ref, tile)

    def sort_chunk(j):
        lbuf[slot, j * LCHUNK // PIECE:(j + 1) * LCHUNK // PIECE] = _as_pieces(
            _dot(onehot(j), xb).astype(BF16))

    n_chunks = LROWS // LCHUNK
    for j in range(n_chunks - 1):
        sort_chunk(j)
    pl.when(used_rows > (n_chunks - 1) * LCHUNK)(functools.partial(sort_chunk, n_chunks - 1))

    start(tile, slot)

    @pl.when(tile == last)
    def _():
        @pl.when(tile >= 1)
        def _():
            wait(tile - 1, 1 - slot)
        wait(tile, slot)
        tails(lambda copy: copy.wait())
        spare_blocks(lambda b: zero_block(b).wait())


def _dispatch(tables, h2, infok, n_rows):
    t = h2.shape[0]
    return pl.pallas_call(
        _dispatch_kernel,
        name="dispatch",
        grid_spec=pltpu.PrefetchScalarGridSpec(
            num_scalar_prefetch=5,
            grid=(t // TW,),
            in_specs=[
                pl.BlockSpec((TW, D_MODEL), lambda i, *_: (i, 0)),
                pl.BlockSpec((2 * TOP_K, TW), lambda i, *_: (0, i)),
            ],
            out_specs=pl.BlockSpec(memory_space=pl.ANY),
            scratch_shapes=[
                pltpu.VMEM((2, LROWS // PIECE, PIECE, D_MODEL), BF16),
                pltpu.VMEM((RB // PIECE, PIECE, D_MODEL), BF16),
                pltpu.SemaphoreType.DMA((2,)),
                pltpu.SemaphoreType.DMA(()),
            ],
        ),
        out_shape=jax.ShapeDtypeStruct((n_rows // PIECE, PIECE, D_MODEL), BF16),
        compiler_params=pltpu.CompilerParams(
            dimension_semantics=("arbitrary",), vmem_limit_bytes=VMEM_LIMIT),
    )(*tables, h2, infok)


def _experts_kernel(be_ref, nv_ref, xs_ref, wg_ref, wu_ref, wd_ref, ys_ref, wgb, wub, wdb):
    i = pl.program_id(0)

    @pl.when(i < nv_ref[0])
    def _():
        @pl.when((i == 0) | (be_ref[i] != be_ref[jnp.maximum(i - 1, 0)]))
        def _():
            wgb[...] = wg_ref[0].astype(BF16)
            wub[...] = wu_ref[0].astype(BF16)
            wdb[...] = wd_ref[0].astype(BF16)

        for j in range(RB // RSUB):
            rows = slice(j * RSUB, (j + 1) * RSUB)
            h = _swiglu(xs_ref[rows, :], wgb[...], wub[...])
            ys_ref[rows, :] = _dot(h.astype(BF16), wdb[...]).astype(BF16)


def _experts(block_expert, n_valid, xs, wg, wu, wd):
    n_blocks = xs.shape[0] // RB
    rows = pl.BlockSpec((RB, D_MODEL), lambda i, be, nv: (jnp.minimum(i, nv[0] - 1), 0))
    return pl.pallas_call(
        _experts_kernel,
        name="experts",
        grid_spec=pltpu.PrefetchScalarGridSpec(
            num_scalar_prefetch=2,
            grid=(n_blocks,),
            in_specs=[
                rows,
                pl.BlockSpec((1, D_MODEL, D_EXPERT), lambda i, be, nv: (be[i], 0, 0)),
                pl.BlockSpec((1, D_MODEL, D_EXPERT), lambda i, be, nv: (be[i], 0, 0)),
                pl.BlockSpec((1, D_EXPERT, D_MODEL), lambda i, be, nv: (be[i], 0, 0)),
            ],
            out_specs=rows,
            scratch_shapes=[
                pltpu.VMEM((D_MODEL, D_EXPERT), BF16),
                pltpu.VMEM((D_MODEL, D_EXPERT), BF16),
                pltpu.VMEM((D_EXPERT, D_MODEL), BF16),
            ],
        ),
        out_shape=jax.ShapeDtypeStruct(xs.shape, BF16),
        input_output_aliases={2: 0},
        compiler_params=pltpu.CompilerParams(
            dimension_semantics=("arbitrary",), vmem_limit_bytes=VMEM_LIMIT),
    )(block_expert, n_valid, xs, wg, wu, wd)


def _combine_kernel(copies_ref, counts_ref,
                    infok_ref, h2_ref, x1_ref, gate_ref, gpost_ref, wsg_ref, wsu_ref, wsd_ref,
                    ys_ref, o_ref, ybuf, acc_ref, sem):
    tile = pl.program_id(0)
    last = pl.num_programs(0) - 1
    slot = tile % 2

    def start(t, s):
        _for_each_copy(
            copies_ref, counts_ref, t,
            lambda lp, gp, n: _piece_copy(ys_ref, gp, ybuf.at[s], lp, sem.at[s], n).start())

    def wait(t, s):
        _wait_pieces(_tile_pieces(counts_ref, t),
                     lambda n: _piece_copy(ys_ref, 0, ybuf.at[s], 0, sem.at[s], n))

    @pl.when(tile == 0)
    def _():
        ybuf[...] = jnp.zeros_like(ybuf)
        start(tile, slot)

    @pl.when(tile < last)
    def _():
        start(tile + 1, 1 - slot)

    wait(tile, slot)

    weights = _onehot_chunks(infok_ref, weighted=True)
    hs = _swiglu(h2_ref[...], wsg_ref[...], wsu_ref[...])
    ff = _dot(hs.astype(BF16), wsd_ref[...])
    used_rows = PIECE * _tile_pieces(counts_ref, tile)

    def sum_chunk(j):
        rows = ybuf[slot, j * LCHUNK // PIECE:(j + 1) * LCHUNK // PIECE].reshape(LCHUNK, D_MODEL)
        return lax.dot_general(weights(j), rows,
                               (((0,), (0,)), ((), ())), preferred_element_type=F32)

    def add_chunk(j):
        acc_ref[...] += sum_chunk(j)

    n_chunks = LROWS // LCHUNK
    for j in range(n_chunks - 1):
        ff = ff + sum_chunk(j)
    acc_ref[...] = ff
    pl.when(used_rows > (n_chunks - 1) * LCHUNK)(functools.partial(add_chunk, n_chunks - 1))
    o_ref[...] = x1_ref[...] + gate_ref[0] * _rms(acc_ref[...], gpost_ref[...])


def _combine(tables, infok, h2, x1, gate, gpost, wsg, wsu, wsd, ys, seq):
    t = h2.shape[0]
    per_seq = seq // TW
    full = lambda shape: pl.BlockSpec(shape, lambda i, *_: (0,) * len(shape))
    tok = lambda width: pl.BlockSpec((TW, width), lambda i, *_: (i, 0))
    return pl.pallas_call(
        _combine_kernel,
        name="combine",
        grid_spec=pltpu.PrefetchScalarGridSpec(
            num_scalar_prefetch=2,
            grid=(t // TW,),
            in_specs=[
                pl.BlockSpec((2 * TOP_K, TW), lambda i, *_: (0, i)), tok(D_MODEL), tok(D_MODEL),
                pl.BlockSpec((1, 1, D_MODEL), lambda i, *_: (i // per_seq, 0, 0)),
                full((1, D_MODEL)),
                full((D_MODEL, D_SHARED)), full((D_MODEL, D_SHARED)), full((D_SHARED, D_MODEL)),
                pl.BlockSpec(memory_space=pl.ANY),
            ],
            out_specs=tok(D_MODEL),
            scratch_shapes=[
                pltpu.VMEM((2, LROWS // PIECE, PIECE, D_MODEL), BF16),
                pltpu.VMEM((TW, D_MODEL), F32),
                pltpu.SemaphoreType.DMA((2,)),
            ],
        ),
        out_shape=jax.ShapeDtypeStruct((t, D_MODEL), F32),
        compiler_params=pltpu.CompilerParams(
            dimension_semantics=("arbitrary",), vmem_limit_bytes=VMEM_LIMIT),
    )(*tables, infok, h2, x1, gate, gpost, wsg, wsu, wsd, ys)


def _dispatch_plan(cnt, n_tiles):
    n = cnt[:, :n_tiles].astype(jnp.int32)
    pieces = (n + PIECE - 1) // PIECE
    local = jnp.cumsum(pieces, axis=0) - pieces
    seg = jnp.sum(pieces, axis=1)
    per_block = RB // PIECE
    seg_pad = (seg + per_block - 1) // per_block * per_block
    seg_end = jnp.cumsum(seg_pad)
    seg_start = seg_end - seg_pad
    glob = seg_start[:, None] + jnp.cumsum(pieces, axis=1) - pieces
    n_blocks = (TOP_K * TW * n_tiles + N_EXPERTS * n_tiles * (PIECE - 1)
                + N_EXPERTS * (RB - PIECE) + RB - 1) // RB
    n_valid = seg_end[-1] // per_block
    blk = jnp.minimum(jnp.arange(n_blocks, dtype=jnp.int32), n_valid - 1)
    block_end = seg_end // per_block
    block_expert = jnp.sum((block_end[None, :] <= blk[:, None]).astype(jnp.int32), axis=1)
    block_expert = jnp.minimum(block_expert, N_EXPERTS - 1)
    assert LROWS // PIECE <= 1 << LOCAL_BITS and n_blocks * per_block < 1 << (31 - LOCAL_BITS)

    def copy_list(count, width):
        end = jnp.cumsum(count, axis=0).T
        slot = jnp.arange(width, dtype=jnp.int32)
        expert = jnp.sum((end[:, None, :] <= slot[None, :, None]).astype(jnp.int32), axis=-1)
        pick = expert[:, :, None] == jnp.arange(N_EXPERTS, dtype=jnp.int32)
        take = lambda a: jnp.sum(jnp.where(pick, a.T[:, None, :], 0), axis=-1)
        return slot[None, :] - take(jnp.cumsum(count, axis=0) - count), take

    big = COPY_SIZES[0]
    assert COPY_SIZES == tuple(range(big, 0, -1))
    n_big = pieces // big
    rest = pieces % big
    within, take = copy_list(n_big, MAX_COPIES)
    lists = [(take(local) + big * within) | ((take(glob) + big * within) << LOCAL_BITS)]
    counts = [jnp.sum(n_big, axis=0)]
    for size in COPY_SIZES[1:]:
        has = (rest == size).astype(jnp.int32)
        _, take = copy_list(has, MAX_COPIES)
        lists.append(take(local + big * n_big) | (take(glob + big * n_big) << LOCAL_BITS))
        counts.append(jnp.sum(has, axis=0))
    counts.append(jnp.sum(pieces, axis=0))
    tables = (jnp.stack(lists, axis=1).reshape(-1).astype(jnp.int32),
              jnp.stack(counts, axis=1).reshape(-1).astype(jnp.int32))
    tails = ((seg_start + seg).astype(jnp.int32), (seg_pad - seg).astype(jnp.int32))
    return tables, tails, block_expert, n_valid.reshape(1).astype(jnp.int32), n_blocks * RB


def kernel(x, c, w_ada, b_ada, g_pre_mix, g_post_mix, g_pre_ffn, g_post_ffn, w_in, b_forget,
           w_pool, pool_scale, g_pool_out, g_attn_out, w_out, w_router, router_bias,
           w_gate, w_up, w_down, ws_gate, ws_up, ws_down):
    bsz, seq, d = x.shape
    depth = w_ada.shape[0]
    for l in range(depth):
        mod = _ada(c, w_ada, b_ada[l][None, :], l)
        shift_m, scale_m, gate_m, shift_f, scale_f, gate_f = [
            m.reshape(bsz, 1, d) for m in jnp.split(mod, 6, axis=-1)]

        bfp = jnp.pad(b_forget[l], (0, LANES - N_HEADS))[None, :]
        pool, qa, kt, v = _premix(
            x, shift_m, scale_m, g_pre_mix[l][None, :], w_in, l, bfp, w_pool[l].astype(BF16),
            pool_scale[l][None, :], g_pool_out[l][None, :])
        ot = _attn(qa, kt, v)

        wr = jnp.pad(w_router[l], ((0, 0), (0, LANES - N_EXPERTS)))
        wrh = wr.astype(BF16)
        wrl = (wr - wrh.astype(F32)).astype(BF16)
        t = bsz * seq
        assert t // TW <= LANES
        x1, h2, infok, cnt = _postmix(
            x, pool, ot, w_out[l].astype(BF16), g_attn_out[l][None, :], g_post_mix[l][None, :],
            gate_m, g_pre_ffn[l][None, :], shift_f, scale_f,
            jnp.concatenate([wrh, wrl], axis=1), router_bias[l][:, None])
        tables, tails, block_expert, n_valid, n_rows = _dispatch_plan(cnt, t // TW)
        h2f = h2.reshape(t, d)
        xs = _dispatch(tables + tails + (n_valid,), h2f, infok, n_rows)
        ys = _experts(block_expert, n_valid, xs.reshape(n_rows, d), w_gate[l], w_up[l], w_down[l])
        ys = ys.reshape(xs.shape)
        out = _combine(tables, infok, h2f, x1.reshape(t, d), gate_f, g_post_ffn[l][None, :],
                       ws_gate[l].astype(BF16), ws_up[l].astype(BF16), ws_down[l].astype(BF16),
                       ys, seq)
        x = out.reshape(bsz, seq, d)
    return x
```

```python
import functools

import numpy as np
import jax
import jax.numpy as jnp
from jax import lax
from jax.experimental import pallas as pl
from jax.experimental.pallas import tpu as pltpu

D_MODEL = 1024
D_POOL = 512
POOL_WINDOWS = (2, 4, 8, 16)
POOL_GROUP = 128
MAX_WINDOW = max(POOL_WINDOWS)
D_ATTN = 512
HEAD_DIM = 64
N_HEADS = 8
N_EXPERTS = 64
N_EXPERT_GROUPS = 8
GROUP_SIZE = N_EXPERTS // N_EXPERT_GROUPS
TOPK_GROUPS = 4
TOP_K = 8
D_EXPERT = 256
D_SHARED = 256
ROUTED_SCALE = 2.5
EPS = 1e-6

LOG2E = 1.4426950408889634
LANES = 128
N_SPLIT = 3
AUG = LANES

TS_PRE = 1024
CUM_BLOCK = 256
TS_POST = 1024
TQ = 512
TK = 512
HEAD_PAIR = 128 // HEAD_DIM
TW = 256
PIECE = 16
RB = 1024
RSUB = 512
LCHUNK = 512
ONEHOT_GROUP = 64
LROWS = -(-(TOP_K * TW + N_EXPERTS * (PIECE - 1)) // LCHUNK) * LCHUNK

F32 = jnp.float32
BF16 = jnp.bfloat16
VMEM_LIMIT = 56 * 1024 * 1024


def _rms(v, g):
    return v * lax.rsqrt(jnp.mean(v * v, axis=-1, keepdims=True) + EPS) * g


def _split3(v):
    hi = v.astype(BF16)
    r1 = v - hi.astype(F32)
    mid = r1.astype(BF16)
    r2 = r1 - mid.astype(F32)
    lo = r2.astype(BF16)
    return hi, mid, lo


def _dot(a, b):
    return jnp.dot(a, b, preferred_element_type=F32)


def _ada_kernel(c_ref, w_ref, b_ref, o_ref):
    o_ref[...] = _dot(c_ref[...].astype(BF16), w_ref[0].astype(BF16)) + b_ref[...]


def _ada(c, w, b, layer):
    bsz = c.shape[0]
    n = w.shape[2]
    return pl.pallas_call(
        _ada_kernel,
        name="ada",
        grid=(n // D_MODEL,),
        in_specs=[
            pl.BlockSpec((bsz, D_MODEL), lambda j: (0, 0)),
            pl.BlockSpec((1, D_MODEL, D_MODEL), lambda j: (layer, 0, j)),
            pl.BlockSpec((1, D_MODEL), lambda j: (0, j)),
        ],
        out_specs=pl.BlockSpec((bsz, D_MODEL), lambda j: (0, j)),
        out_shape=jax.ShapeDtypeStruct((bsz, n), F32),
    )(c, w, b)


def _premix_kernel(x_ref, shift_ref, scale_ref, g_ref, win_ref, bf_ref, wpool_ref,
                   pscale_ref, gpool_ref, place_ref, ones_ref, fmask_ref,
                   pool_ref, q_ref, kt_ref, v_ref,
                   uext_ref, cum_ref, w1_ref, wf_ref):
    s = pl.program_id(1)
    ts = x_ref.shape[1]

    @pl.when((pl.program_id(0) == 0) & (s == 0))
    def _():
        q0, k0 = D_POOL, D_POOL + D_ATTN
        w1_ref[:, :q0] = win_ref[0, :, :q0].astype(BF16)
        w1_ref[:, q0:k0] = (win_ref[0, :, q0:k0] * (HEAD_DIM ** -0.5 * LOG2E)).astype(BF16)
        w1_ref[:, k0:] = win_ref[0, :, k0:k0 + 2 * D_ATTN].astype(BF16)
        wf_ref[...] = jnp.zeros_like(wf_ref)
        wf_ref[:, :N_HEADS] = win_ref[0, :, k0 + 2 * D_ATTN:].astype(BF16)

    @pl.when(s == 0)
    def _():
        uext_ref[0:MAX_WINDOW, :] = jnp.zeros((MAX_WINDOW, D_POOL), F32)
        cum_ref[...] = jnp.zeros_like(cum_ref)

    x = x_ref[0]
    h = _rms(x, g_ref[...]) * (1.0 + scale_ref[0]) + shift_ref[0]
    hb = h.astype(BF16)
    proj = _dot(hb, w1_ref[...])
    u = proj[:, :D_POOL]
    q = proj[:, D_POOL:D_POOL + D_ATTN]
    k = proj[:, D_POOL + D_ATTN:D_POOL + 2 * D_ATTN]
    v = proj[:, D_POOL + 2 * D_ATTN:]

    uext_ref[MAX_WINDOW:, :] = u
    pos = (s * ts + lax.broadcasted_iota(jnp.int32, (ts, 1), 0) + 1).astype(F32)
    ys = []
    for g, w in enumerate(POOL_WINDOWS):
        c0 = g * POOL_GROUP
        acc = uext_ref[MAX_WINDOW:, c0:c0 + POOL_GROUP]
        for j in range(1, w):
            acc = acc + uext_ref[MAX_WINDOW - j:MAX_WINDOW - j + ts, c0:c0 + POOL_GROUP]
        pooled = acc / jnp.minimum(pos, float(w)) - u[:, c0:c0 + POOL_GROUP]
        ys.append(_dot(pooled.astype(BF16), wpool_ref[g]))
    ypool = jnp.concatenate(ys, axis=1) * pscale_ref[...]
    pool_ref[0] = _rms(ypool, gpool_ref[...]).astype(BF16)
    uext_ref[0:MAX_WINDOW, :] = uext_ref[ts:ts + MAX_WINDOW, :]

    z0 = _dot(hb, wf_ref[...]) + bf_ref[...]
    z = z0
    for p in range(1, N_SPLIT):
        z = z + pltpu.roll(z0, p * N_HEADS, 1)
    logf = jnp.minimum(z, 0.0) - jnp.log1p(jnp.exp(-jnp.abs(z)))
    row = lax.broadcasted_iota(jnp.int32, (CUM_BLOCK, CUM_BLOCK), 0)
    col = lax.broadcasted_iota(jnp.int32, (CUM_BLOCK, CUM_BLOCK), 1)
    tri = (col <= row).astype(BF16)
    pieces = _split3(logf)
    carry = cum_ref[...]
    blocks = []
    for r0 in range(0, ts, CUM_BLOCK):
        blk = carry
        for piece in pieces:
            blk = blk + _dot(tri, piece[r0:r0 + CUM_BLOCK, :])
        carry = blk[CUM_BLOCK - 1:CUM_BLOCK, :]
        blocks.append(blk)
    cum = jnp.concatenate(blocks, axis=0)
    cum_ref[...] = carry

    hi, mid, lo = [piece.astype(F32) for piece in _split3(cum * LOG2E)]
    lane = lax.broadcasted_iota(jnp.int32, (ts, LANES), 1)
    pieces = jnp.where(lane < N_HEADS, hi, jnp.where(lane < 2 * N_HEADS, mid, lo))
    aug = _dot(pieces.astype(BF16), place_ref[...]) + ones_ref[...]
    aug_q = aug[:, :N_HEADS * AUG]
    aug_k = aug[:, N_HEADS * AUG:]

    def expand(a):
        blocks = []
        for j in range(D_ATTN // LANES):
            blk = a[:, j * LANES:(j + 1) * LANES]
            blocks += [blk, blk]
        return jnp.concatenate(blocks, axis=1)

    fmask = fmask_ref[...]
    qa = expand(q) * fmask + aug_q
    ka = expand(k) * fmask + aug_k
    q_ref[0] = qa.astype(BF16)
    kt_ref[0] = ka.T.astype(BF16)
    v_ref[0] = v.astype(BF16)


def _aug_constants():
    width = N_HEADS * AUG
    place = np.zeros((LANES, 2 * width), np.float32)
    ones = np.zeros((1, 2 * width), np.float32)
    fmask = np.zeros((1, width), np.float32)
    for h in range(N_HEADS):
        feat0 = h * AUG + (0 if h % 2 == 0 else HEAD_DIM)
        aug0 = h * AUG + (HEAD_DIM if h % 2 == 0 else 0)
        fmask[0, feat0:feat0 + HEAD_DIM] = 1.0
        for p in range(N_SPLIT):
            place[p * N_HEADS + h, aug0 + p] = 1.0
            ones[0, width + aug0 + p] = 1.0
            place[p * N_HEADS + h, width + aug0 + N_SPLIT + p] = -1.0
            ones[0, aug0 + N_SPLIT + p] = 1.0
    return jnp.asarray(place, BF16), jnp.asarray(ones), jnp.asarray(fmask)


def _premix(x, shift, scale, g, w_in, layer, bfp, wpool, pscale, gpool):
    bsz, seq, _ = x.shape
    d_in = w_in.shape[2]
    assert d_in == D_POOL + 3 * D_ATTN + N_HEADS
    ts = TS_PRE
    place, ones, fmask = _aug_constants()
    full = lambda shape: pl.BlockSpec(shape, lambda b, s: (0,) * len(shape))
    per_batch = pl.BlockSpec((1, 1, D_MODEL), lambda b, s: (b, 0, 0))
    return pl.pallas_call(
        _premix_kernel,
        name="premix",
        grid=(bsz, seq // ts),
        in_specs=[
            pl.BlockSpec((1, ts, D_MODEL), lambda b, s: (b, s, 0)),
            per_batch, per_batch,
            full((1, D_MODEL)),
            pl.BlockSpec((1, D_MODEL, d_in), lambda b, s: (layer, 0, 0),
                         pipeline_mode=pl.Buffered(1)),
            full((1, LANES)),
            full((len(POOL_WINDOWS), POOL_GROUP, POOL_GROUP)),
            full((1, D_POOL)),
            full((1, D_POOL)),
            full((LANES, 2 * N_HEADS * AUG)),
            full((1, 2 * N_HEADS * AUG)),
            full((1, N_HEADS * AUG)),
        ],
        out_specs=[
            pl.BlockSpec((1, ts, D_POOL), lambda b, s: (b, s, 0)),
            pl.BlockSpec((1, ts, N_HEADS * AUG), lambda b, s: (b, s, 0)),
            pl.BlockSpec((1, N_HEADS * AUG, ts), lambda b, s: (b, 0, s)),
            pl.BlockSpec((1, ts, D_ATTN), lambda b, s: (b, s, 0)),
        ],
        out_shape=[
            jax.ShapeDtypeStruct((bsz, seq, D_POOL), BF16),
            jax.ShapeDtypeStruct((bsz, seq, N_HEADS * AUG), BF16),
            jax.ShapeDtypeStruct((bsz, N_HEADS * AUG, seq), BF16),
            jax.ShapeDtypeStruct((bsz, seq, D_ATTN), BF16),
        ],
        scratch_shapes=[
            pltpu.VMEM((ts + MAX_WINDOW, D_POOL), F32),
            pltpu.VMEM((1, LANES), F32),
            pltpu.VMEM((D_MODEL, D_POOL + 3 * D_ATTN), BF16),
            pltpu.VMEM((D_MODEL, LANES), BF16),
        ],
        compiler_params=pltpu.CompilerParams(
            dimension_semantics=("arbitrary", "arbitrary"), vmem_limit_bytes=VMEM_LIMIT),
    )(x, shift, scale, g, w_in, bfp, wpool, pscale, gpool, place, ones, fmask)


def _attn_kernel(q_ref, kt_ref, v_ref, wg_ref, wu_ref, wd_ref, o_ref, wgb_ref, wub_ref, wdb_ref):
    wgb_ref[...] = wg_ref[...].astype(BF16)
    wub_ref[...] = wu_ref[...].astype(BF16)
    wdb_ref[...] = wd_ref[...].astype(BF16)
    seq = q_ref.shape[1]
    qry_i = lax.broadcasted_iota(jnp.int32, (TQ, TK), 0)
    key_i = lax.broadcasted_iota(jnp.int32, (TQ, TK), 1)
    causal = key_i <= qry_i
    one_lane = lax.broadcasted_iota(jnp.int32, (seq, LANES), 1) == 0
    vaug = jnp.concatenate([v_ref[0], jnp.where(one_lane, 1.0, 0.0).astype(BF16)], axis=1)
    out_lane = lax.broadcasted_iota(jnp.int32, (TQ, LANES), 1)
    heads = range(HEAD_PAIR)
    for qi in range(seq // TQ):
        rows = slice(qi * TQ, (qi + 1) * TQ)
        q = [q_ref[0, rows, h * AUG:(h + 1) * AUG] for h in heads]
        m = [jnp.full((TQ, 1), -jnp.inf, F32) for _ in heads]
        acc = [jnp.zeros((TQ, 2 * LANES), F32) for _ in heads]
        for kj in range(qi + 1):
            keys = slice(kj * TK, (kj + 1) * TK)
            for h in heads:
                s = _dot(q[h], kt_ref[0, h * AUG:(h + 1) * AUG, keys])
                if kj == qi:
                    s = jnp.where(causal, s, -jnp.inf)
                m_new = jnp.maximum(m[h], jnp.max(s, axis=1, keepdims=True))
                p = jnp.exp2(s - m_new).astype(BF16)
                acc[h] = jnp.exp2(m[h] - m_new) * acc[h] + _dot(p, vaug[keys, :])
                m[h] = m_new
        o = [acc[h][:, :LANES] / acc[h][:, LANES:LANES + 1] for h in heads]
        o_ref[0, rows, :] = jnp.where(out_lane < HEAD_DIM, o[0], o[1])


def _attn(q, kt, v, wg, wu, wd):
    bsz, seq, _ = q.shape
    pairs = N_HEADS // HEAD_PAIR
    per_step = N_EXPERTS // (bsz * pairs)
    assert per_step * bsz * pairs == N_EXPERTS
    wspec = lambda shape: pl.BlockSpec((per_step,) + shape, lambda b, h: (b * pairs + h, 0, 0))
    return pl.pallas_call(
        _attn_kernel,
        name="attn",
        grid=(bsz, pairs),
        in_specs=[
            pl.BlockSpec((1, seq, HEAD_PAIR * AUG), lambda b, h: (b, 0, h)),
            pl.BlockSpec((1, HEAD_PAIR * AUG, seq), lambda b, h: (b, h, 0)),
            pl.BlockSpec((1, seq, LANES), lambda b, h: (b, 0, h)),
            wspec((D_MODEL, D_EXPERT)), wspec((D_MODEL, D_EXPERT)), wspec((D_EXPERT, D_MODEL)),
        ],
        out_specs=[
            pl.BlockSpec((1, seq, LANES), lambda b, h: (b, 0, h)),
            wspec((D_MODEL, D_EXPERT)), wspec((D_MODEL, D_EXPERT)), wspec((D_EXPERT, D_MODEL)),
        ],
        out_shape=[
            jax.ShapeDtypeStruct((bsz, seq, D_ATTN), F32),
            jax.ShapeDtypeStruct(wg.shape, BF16),
            jax.ShapeDtypeStruct(wu.shape, BF16),
            jax.ShapeDtypeStruct(wd.shape, BF16),
        ],
        compiler_params=pltpu.CompilerParams(
            dimension_semantics=("arbitrary", "arbitrary"), vmem_limit_bytes=VMEM_LIMIT),
    )(q, kt, v, wg, wu, wd)


def _postmix_kernel(x_ref, pool_ref, o_ref, wout_ref, gattn_ref, gpost_ref, gate_ref,
                    gffn_ref, shift_ref, scale_ref, wr_ref, bias_ref,
                    x1_ref, h2_ref, infok_ref, cnt_ref):
    ya = _rms(o_ref[0], gattn_ref[...]).astype(BF16)
    mixed = _dot(pool_ref[0], wout_ref[:D_POOL, :]) + _dot(ya, wout_ref[D_POOL:, :])
    x1 = x_ref[0] + gate_ref[0] * _rms(mixed, gpost_ref[...])
    x1_ref[0] = x1
    h2 = _rms(x1, gffn_ref[...]) * (1.0 + scale_ref[0]) + shift_ref[0]
    h2b = h2.astype(BF16)
    h2_ref[0] = h2b
    h2l = (h2 - h2b.astype(F32)).astype(BF16)
    both = _dot(h2b, wr_ref[...])
    logits = both[:, :LANES] + both[:, LANES:] + _dot(h2l, wr_ref[:, :LANES])
    st = jax.nn.sigmoid(logits).T

    step = pl.program_id(0) * pl.num_programs(1) + pl.program_id(1)

    @pl.when(step == 0)
    def _():
        cnt_ref[...] = jnp.zeros_like(cnt_ref)

    lane = lax.broadcasted_iota(jnp.int32, cnt_ref.shape, 1)
    tiles = x_ref.shape[1] // TW
    for i in range(tiles):
        info, n = _route_tile(st[:, i * TW:(i + 1) * TW], bias_ref)
        infok_ref[:, i * TW:(i + 1) * TW] = info
        cnt_ref[...] = jnp.where(lane == step * tiles + i, jnp.broadcast_to(n, cnt_ref.shape),
                                 cnt_ref[...])


def _postmix(x, pool, ot, wout, gattn, gpost, gate, gffn, shift, scale, wr, bias_col):
    bsz, seq, _ = x.shape
    ts = TS_POST
    ns = seq // ts
    full = lambda shape: pl.BlockSpec(shape, lambda b, s: (0,) * len(shape))
    per_batch = pl.BlockSpec((1, 1, D_MODEL), lambda b, s: (b, 0, 0))
    tok = lambda width: pl.BlockSpec((1, ts, width), lambda b, s: (b, s, 0))
    return pl.pallas_call(
        _postmix_kernel,
        name="postmix",
        grid=(bsz, ns),
        in_specs=[
            tok(D_MODEL), tok(D_POOL), tok(D_ATTN),
            full((D_MODEL, D_MODEL)),
            full((1, D_ATTN)), full((1, D_MODEL)), per_batch,
            full((1, D_MODEL)), per_batch, per_batch,
            full((D_MODEL, 2 * LANES)),
            full((N_EXPERTS, 1)),
        ],
        out_specs=[
            tok(D_MODEL), tok(D_MODEL),
            pl.BlockSpec((2 * TOP_K, ts), lambda b, s: (0, b * ns + s)),
            full((N_EXPERTS, LANES)),
        ],
        out_shape=[
            jax.ShapeDtypeStruct((bsz, seq, D_MODEL), F32),
            jax.ShapeDtypeStruct((bsz, seq, D_MODEL), BF16),
            jax.ShapeDtypeStruct((2 * TOP_K, bsz * seq), F32),
            jax.ShapeDtypeStruct((N_EXPERTS, LANES), F32),
        ],
        compiler_params=pltpu.CompilerParams(
            dimension_semantics=("arbitrary", "arbitrary"), vmem_limit_bytes=VMEM_LIMIT),
    )(x, pool, ot, wout, gattn, gpost, gate, gffn, shift, scale, wr, bias_col)


def _route_tile(st, bias_ref):
    tr = st.shape[1]
    scores = [st[g * GROUP_SIZE:(g + 1) * GROUP_SIZE, :] for g in range(N_EXPERT_GROUPS)]
    sel = [scores[g] + bias_ref[g * GROUP_SIZE:(g + 1) * GROUP_SIZE, :]
           for g in range(N_EXPERT_GROUPS)]
    sub = lax.broadcasted_iota(jnp.int32, (GROUP_SIZE, tr), 0)
    neg = jnp.float32(-jnp.inf)

    rows = []
    for g in range(N_EXPERT_GROUPS):
        v = sel[g]
        m1 = jnp.max(v, axis=0, keepdims=True)
        first = jnp.min(jnp.where(v == m1, sub, GROUP_SIZE), axis=0, keepdims=True)
        m2 = jnp.max(jnp.where(sub == first, neg, v), axis=0, keepdims=True)
        rows.append(m1 + m2)
    gs = jnp.concatenate(rows, axis=0)

    shape = (GROUP_SIZE, tr)
    one = jnp.ones(shape, jnp.int32)
    zero = jnp.zeros(shape, jnp.int32)
    beaten = zero
    for gp in range(N_EXPERT_GROUPS):
        r = jnp.broadcast_to(gs[gp:gp + 1, :], shape)
        tie = jnp.where(sub > gp, one, zero)
        beaten = beaten + jnp.where(r > gs, one, zero) + jnp.where(r == gs, tie, zero)
    keep = jnp.where(beaten < TOPK_GROUPS, 1.0, 0.0)

    masked = [jnp.where(jnp.broadcast_to(keep[g:g + 1, :], shape) > 0.5, sel[g], neg)
              for g in range(N_EXPERT_GROUPS)]

    eidx = [sub + g * GROUP_SIZE for g in range(N_EXPERT_GROUPS)]
    picked = [zero for _ in range(N_EXPERT_GROUPS)]
    for _ in range(TOP_K):
        best = masked[0]
        for g in range(1, N_EXPERT_GROUPS):
            best = jnp.maximum(best, masked[g])
        best = jnp.broadcast_to(jnp.max(best, axis=0, keepdims=True), shape)
        first = jnp.where(masked[0] == best, eidx[0], N_EXPERTS)
        for g in range(1, N_EXPERT_GROUPS):
            first = jnp.minimum(first, jnp.where(masked[g] == best, eidx[g], N_EXPERTS))
        first = jnp.broadcast_to(jnp.min(first, axis=0, keepdims=True), shape)
        for g in range(N_EXPERT_GROUPS):
            hit = eidx[g] == first
            picked[g] = jnp.where(hit, one, picked[g])
            masked[g] = jnp.where(hit, neg, masked[g])
    chosen = [picked[g] > 0 for g in range(N_EXPERT_GROUPS)]

    w = [jnp.where(chosen[g], scores[g], 0.0) for g in range(N_EXPERT_GROUPS)]
    denom = w[0].sum(axis=0, keepdims=True)
    for g in range(1, N_EXPERT_GROUPS):
        denom = denom + w[g].sum(axis=0, keepdims=True)
    wfull = jnp.concatenate([w[g] / denom * ROUTED_SCALE for g in range(N_EXPERT_GROUPS)], axis=0)
    mfull = jnp.concatenate([jnp.where(chosen[g], 1.0, 0.0) for g in range(N_EXPERT_GROUPS)],
                            axis=0)
    mb = mfull.astype(BF16)

    e_r = lax.broadcasted_iota(jnp.int32, (N_EXPERTS, N_EXPERTS), 0)
    e_c = lax.broadcasted_iota(jnp.int32, (N_EXPERTS, N_EXPERTS), 1)
    before_e = (e_c < e_r).astype(BF16)
    t_r = lax.broadcasted_iota(jnp.int32, (tr, tr), 0)
    t_c = lax.broadcasted_iota(jnp.int32, (tr, tr), 1)
    before_t = (t_r < t_c).astype(BF16)
    ordinal = _dot(before_e, mb)
    rank = _dot(mb, before_t)
    n = jnp.sum(mfull, axis=1, keepdims=True)
    pieces = jnp.floor((n + (PIECE - 1.0)) * (1.0 / PIECE))
    run_start = PIECE * _dot(before_e, jnp.broadcast_to(pieces, (N_EXPERTS, LANES)).astype(BF16))
    pos = run_start[:, 0:1] + rank

    sub8 = lax.broadcasted_iota(jnp.int32, (TOP_K, tr), 0)
    pos8 = jnp.zeros((TOP_K, tr), F32)
    w8 = jnp.zeros((TOP_K, tr), F32)
    for k in range(TOP_K):
        selk = jnp.where(ordinal == float(k), mfull, 0.0)
        pk = jnp.sum(selk * pos, axis=0, keepdims=True)
        wk = jnp.sum(selk * wfull, axis=0, keepdims=True)
        pos8 = jnp.where(sub8 == k, jnp.broadcast_to(pk, (TOP_K, tr)), pos8)
        w8 = jnp.where(sub8 == k, jnp.broadcast_to(wk, (TOP_K, tr)), w8)
    return jnp.concatenate([pos8, w8], axis=0), n


def _swiglu(xb, wg, wu):
    g = _dot(xb, wg)
    return (g * jax.nn.sigmoid(g)) * _dot(xb, wu)


def _piece_copy(src_ref, src_piece, dst_ref, dst_piece, sem, n_pieces=1):
    src = src_ref.at[pl.ds(src_piece, n_pieces)]
    dst = dst_ref.at[pl.ds(dst_piece, n_pieces)]
    return pltpu.make_async_copy(src, dst, sem)


def _as_pieces(rows):
    return rows.reshape(rows.shape[0] // PIECE, PIECE, rows.shape[1])


LOCAL_BITS = 8
COPY_SIZES = (4, 3, 2, 1)
MAX_COPIES = N_EXPERTS
assert LROWS // (COPY_SIZES[0] * PIECE) <= MAX_COPIES


def _for_each_copy(copies_ref, counts_ref, tile, fn):
    for c, size in enumerate(COPY_SIZES):
        base = (tile * len(COPY_SIZES) + c) * MAX_COPIES

        def body(i, carry, base=base, size=size):
            word = copies_ref[base + i]
            fn(word & ((1 << LOCAL_BITS) - 1), lax.shift_right_logical(word, LOCAL_BITS), size)
            return carry

        lax.fori_loop(0, counts_ref[tile * (len(COPY_SIZES) + 1) + c], body, 0)


def _tile_pieces(counts_ref, tile):
    return counts_ref[tile * (len(COPY_SIZES) + 1) + len(COPY_SIZES)]


def _wait_pieces(total, copy_of):
    for bit in range((LROWS // PIECE).bit_length()):
        @pl.when((lax.shift_right_logical(total, bit) & 1) == 1)
        def _():
            copy_of(1 << bit).wait()


def _onehot_chunks(infok_ref, weighted):
    pos = infok_ref[0:TOP_K, :]
    grp = jnp.floor(pos * (1.0 / ONEHOT_GROUP))
    off = pos - ONEHOT_GROUP * grp
    sub = lax.broadcasted_iota(jnp.int32, (ONEHOT_GROUP, TW), 0).astype(F32)
    offs, grps = [], []
    for k in range(TOP_K):
        value = infok_ref[TOP_K + k:TOP_K + k + 1, :] if weighted else 1.0
        offs.append(jnp.where(sub == off[k:k + 1, :], value, 0.0).astype(BF16))
        grps.append(jnp.broadcast_to(grp[k:k + 1, :], (PIECE, TW)).astype(BF16))
    zero = jnp.zeros((PIECE, TW), BF16)

    def chunk(j):
        tiles = []
        for a in range(j * LCHUNK // ONEHOT_GROUP, (j + 1) * LCHUNK // ONEHOT_GROUP):
            hit = [grps[k] == a for k in range(TOP_K)]
            for i in range(ONEHOT_GROUP // PIECE):
                rows = zero
                for k in range(TOP_K):
                    rows = rows + jnp.where(hit[k], offs[k][i * PIECE:(i + 1) * PIECE, :], zero)
                tiles.append(rows)
        return jnp.concatenate(tiles, axis=0)

    return chunk


def _dispatch_kernel(copies_ref, counts_ref, tail_ref, ntail_ref, nv_ref,
                     h2_ref, infok_ref, xs_ref, lbuf, zbuf, sem, zsem):
    tile = pl.program_id(0)
    last = pl.num_programs(0) - 1
    slot = tile % 2
    block_pieces = RB // PIECE
    n_blocks = xs_ref.shape[0] // block_pieces

    def spare_blocks(fn):
        def per_block(b, carry):
            fn(b)
            return carry
        lax.fori_loop(nv_ref[0], n_blocks, per_block, 0)

    def zero_block(b):
        return pltpu.make_async_copy(zbuf, xs_ref.at[pl.ds(b * block_pieces, block_pieces)], zsem)

    def tails(fn):
        def per_expert(e, carry):
            n = ntail_ref[e]
            for bit in range((RB // PIECE - 1).bit_length()):
                @pl.when((lax.shift_right_logical(n, bit) & 1) == 1)
                def _():
                    done = n & ((1 << bit) - 1)
                    fn(_piece_copy(zbuf, 0, xs_ref, tail_ref[e] + done, zsem, 1 << bit))
            return carry
        lax.fori_loop(0, N_EXPERTS, per_expert, 0)

    def start(t, s):
        _for_each_copy(
            copies_ref, counts_ref, t,
            lambda lp, gp, n: _piece_copy(lbuf.at[s], lp, xs_ref, gp, sem.at[s], n).start())

    def wait(t, s):
        _wait_pieces(_tile_pieces(counts_ref, t),
                     lambda n: _piece_copy(lbuf.at[s], 0, xs_ref, 0, sem.at[s], n))

    @pl.when(tile == 0)
    def _():
        zbuf[...] = jnp.zeros_like(zbuf)
        tails(lambda copy: copy.start())
        spare_blocks(lambda b: zero_block(b).start())

    @pl.when(tile >= 2)
    def _():
        wait(tile - 2, slot)

    onehot = _onehot_chunks(infok_ref, weighted=False)
    xb = h2_ref[...]
    used_rows = PIECE * _tile_pieces(counts_ref, tile)

    def sort_chunk(j):
        lbuf[slot, j * LCHUNK // PIECE:(j + 1) * LCHUNK // PIECE] = _as_pieces(
            _dot(onehot(j), xb).astype(BF16))

    n_chunks = LROWS // LCHUNK
    for j in range(n_chunks - 1):
        sort_chunk(j)
    pl.when(used_rows > (n_chunks - 1) * LCHUNK)(functools.partial(sort_chunk, n_chunks - 1))

    start(tile, slot)

    @pl.when(tile == last)
    def _():
        @pl.when(tile >= 1)
        def _():
            wait(tile - 1, 1 - slot)
        wait(tile, slot)
        tails(lambda copy: copy.wait())
        spare_blocks(lambda b: zero_block(b).wait())


def _dispatch(tables, h2, infok, n_rows):
    t = h2.shape[0]
    return pl.pallas_call(
        _dispatch_kernel,
        name="dispatch",
        grid_spec=pltpu.PrefetchScalarGridSpec(
            num_scalar_prefetch=5,
            grid=(t // TW,),
            in_specs=[
                pl.BlockSpec((TW, D_MODEL), lambda i, *_: (i, 0)),
                pl.BlockSpec((2 * TOP_K, TW), lambda i, *_: (0, i)),
            ],
            out_specs=pl.BlockSpec(memory_space=pl.ANY),
            scratch_shapes=[
                pltpu.VMEM((2, LROWS // PIECE, PIECE, D_MODEL), BF16),
                pltpu.VMEM((RB // PIECE, PIECE, D_MODEL), BF16),
                pltpu.SemaphoreType.DMA((2,)),
                pltpu.SemaphoreType.DMA(()),
            ],
        ),
        out_shape=jax.ShapeDtypeStruct((n_rows // PIECE, PIECE, D_MODEL), BF16),
        compiler_params=pltpu.CompilerParams(
            dimension_semantics=("arbitrary",), vmem_limit_bytes=VMEM_LIMIT),
    )(*tables, h2, infok)


def _experts_kernel(be_ref, nv_ref, xs_ref, wg_ref, wu_ref, wd_ref, ys_ref):
    @pl.when(pl.program_id(0) < nv_ref[0])
    def _():
        for j in range(RB // RSUB):
            rows = slice(j * RSUB, (j + 1) * RSUB)
            h = _swiglu(xs_ref[rows, :], wg_ref[0], wu_ref[0])
            ys_ref[rows, :] = _dot(h.astype(BF16), wd_ref[0]).astype(BF16)


def _experts(block_expert, n_valid, xs, wg, wu, wd):
    n_blocks = xs.shape[0] // RB
    rows = pl.BlockSpec((RB, D_MODEL), lambda i, be, nv: (jnp.minimum(i, nv[0] - 1), 0))
    return pl.pallas_call(
        _experts_kernel,
        name="experts",
        grid_spec=pltpu.PrefetchScalarGridSpec(
            num_scalar_prefetch=2,
            grid=(n_blocks,),
            in_specs=[
                rows,
                pl.BlockSpec((1, D_MODEL, D_EXPERT), lambda i, be, nv: (be[i], 0, 0)),
                pl.BlockSpec((1, D_MODEL, D_EXPERT), lambda i, be, nv: (be[i], 0, 0)),
                pl.BlockSpec((1, D_EXPERT, D_MODEL), lambda i, be, nv: (be[i], 0, 0)),
            ],
            out_specs=rows,
        ),
        out_shape=jax.ShapeDtypeStruct(xs.shape, BF16),
        input_output_aliases={2: 0},
        compiler_params=pltpu.CompilerParams(
            dimension_semantics=("arbitrary",), vmem_limit_bytes=VMEM_LIMIT),
    )(block_expert, n_valid, xs, wg, wu, wd)


def _combine_kernel(copies_ref, counts_ref,
                    infok_ref, h2_ref, x1_ref, gate_ref, gpost_ref, wsg_ref, wsu_ref, wsd_ref,
                    ys_ref, o_ref, ybuf, acc_ref, sem):
    tile = pl.program_id(0)
    last = pl.num_programs(0) - 1
    slot = tile % 2

    def start(t, s):
        _for_each_copy(
            copies_ref, counts_ref, t,
            lambda lp, gp, n: _piece_copy(ys_ref, gp, ybuf.at[s], lp, sem.at[s], n).start())

    def wait(t, s):
        _wait_pieces(_tile_pieces(counts_ref, t),
                     lambda n: _piece_copy(ys_ref, 0, ybuf.at[s], 0, sem.at[s], n))

    @pl.when(tile == 0)
    def _():
        ybuf[...] = jnp.zeros_like(ybuf)
        start(tile, slot)

    @pl.when(tile < last)
    def _():
        start(tile + 1, 1 - slot)

    wait(tile, slot)

    weights = _onehot_chunks(infok_ref, weighted=True)
    hs = _swiglu(h2_ref[...], wsg_ref[...], wsu_ref[...])
    ff = _dot(hs.astype(BF16), wsd_ref[...])
    used_rows = PIECE * _tile_pieces(counts_ref, tile)

    def sum_chunk(j):
        rows = ybuf[slot, j * LCHUNK // PIECE:(j + 1) * LCHUNK // PIECE].reshape(LCHUNK, D_MODEL)
        return lax.dot_general(weights(j), rows,
                               (((0,), (0,)), ((), ())), preferred_element_type=F32)

    def add_chunk(j):
        acc_ref[...] += sum_chunk(j)

    n_chunks = LROWS // LCHUNK
    for j in range(n_chunks - 1):
        ff = ff + sum_chunk(j)
    acc_ref[...] = ff
    pl.when(used_rows > (n_chunks - 1) * LCHUNK)(functools.partial(add_chunk, n_chunks - 1))
    o_ref[...] = x1_ref[...] + gate_ref[0] * _rms(acc_ref[...], gpost_ref[...])


def _combine(tables, infok, h2, x1, gate, gpost, wsg, wsu, wsd, ys, seq):
    t = h2.shape[0]
    per_seq = seq // TW
    full = lambda shape: pl.BlockSpec(shape, lambda i, *_: (0,) * len(shape))
    tok = lambda width: pl.BlockSpec((TW, width), lambda i, *_: (i, 0))
    return pl.pallas_call(
        _combine_kernel,
        name="combine",
        grid_spec=pltpu.PrefetchScalarGridSpec(
            num_scalar_prefetch=2,
            grid=(t // TW,),
            in_specs=[
                pl.BlockSpec((2 * TOP_K, TW), lambda i, *_: (0, i)), tok(D_MODEL), tok(D_MODEL),
                pl.BlockSpec((1, 1, D_MODEL), lambda i, *_: (i // per_seq, 0, 0)),
                full((1, D_MODEL)),
                full((D_MODEL, D_SHARED)), full((D_MODEL, D_SHARED)), full((D_SHARED, D_MODEL)),
                pl.BlockSpec(memory_space=pl.ANY),
            ],
            out_specs=tok(D_MODEL),
            scratch_shapes=[
                pltpu.VMEM((2, LROWS // PIECE, PIECE, D_MODEL), BF16),
                pltpu.VMEM((TW, D_MODEL), F32),
                pltpu.SemaphoreType.DMA((2,)),
            ],
        ),
        out_shape=jax.ShapeDtypeStruct((t, D_MODEL), F32),
        compiler_params=pltpu.CompilerParams(
            dimension_semantics=("arbitrary",), vmem_limit_bytes=VMEM_LIMIT),
    )(*tables, infok, h2, x1, gate, gpost, wsg, wsu, wsd, ys)


def _dispatch_plan(cnt, n_tiles):
    n = cnt[:, :n_tiles].astype(jnp.int32)
    pieces = (n + PIECE - 1) // PIECE
    local = jnp.cumsum(pieces, axis=0) - pieces
    seg = jnp.sum(pieces, axis=1)
    per_block = RB // PIECE
    seg_pad = (seg + per_block - 1) // per_block * per_block
    seg_end = jnp.cumsum(seg_pad)
    seg_start = seg_end - seg_pad
    glob = seg_start[:, None] + jnp.cumsum(pieces, axis=1) - pieces
    n_blocks = (TOP_K * TW * n_tiles + N_EXPERTS * n_tiles * (PIECE - 1)
                + N_EXPERTS * (RB - PIECE) + RB - 1) // RB
    n_valid = seg_end[-1] // per_block
    blk = jnp.minimum(jnp.arange(n_blocks, dtype=jnp.int32), n_valid - 1)
    block_end = seg_end // per_block
    block_expert = jnp.sum((block_end[None, :] <= blk[:, None]).astype(jnp.int32), axis=1)
    block_expert = jnp.minimum(block_expert, N_EXPERTS - 1)
    assert LROWS // PIECE <= 1 << LOCAL_BITS and n_blocks * per_block < 1 << (31 - LOCAL_BITS)

    def copy_list(count, width):
        end = jnp.cumsum(count, axis=0).T
        slot = jnp.arange(width, dtype=jnp.int32)
        expert = jnp.sum((end[:, None, :] <= slot[None, :, None]).astype(jnp.int32), axis=-1)
        pick = expert[:, :, None] == jnp.arange(N_EXPERTS, dtype=jnp.int32)
        take = lambda a: jnp.sum(jnp.where(pick, a.T[:, None, :], 0), axis=-1)
        return slot[None, :] - take(jnp.cumsum(count, axis=0) - count), take

    big = COPY_SIZES[0]
    assert COPY_SIZES == tuple(range(big, 0, -1))
    n_big = pieces // big
    rest = pieces % big
    within, take = copy_list(n_big, MAX_COPIES)
    lists = [(take(local) + big * within) | ((take(glob) + big * within) << LOCAL_BITS)]
    counts = [jnp.sum(n_big, axis=0)]
    for size in COPY_SIZES[1:]:
        has = (rest == size).astype(jnp.int32)
        _, take = copy_list(has, MAX_COPIES)
        lists.append(take(local + big * n_big) | (take(glob + big * n_big) << LOCAL_BITS))
        counts.append(jnp.sum(has, axis=0))
    counts.append(jnp.sum(pieces, axis=0))
    tables = (jnp.stack(lists, axis=1).reshape(-1).astype(jnp.int32),
              jnp.stack(counts, axis=1).reshape(-1).astype(jnp.int32))
    tails = ((seg_start + seg).astype(jnp.int32), (seg_pad - seg).astype(jnp.int32))
    return tables, tails, block_expert, n_valid.reshape(1).astype(jnp.int32), n_blocks * RB


def kernel(x, c, w_ada, b_ada, g_pre_mix, g_post_mix, g_pre_ffn, g_post_ffn, w_in, b_forget,
           w_pool, pool_scale, g_pool_out, g_attn_out, w_out, w_router, router_bias,
           w_gate, w_up, w_down, ws_gate, ws_up, ws_down):
    bsz, seq, d = x.shape
    depth = w_ada.shape[0]
    for l in range(depth):
        mod = _ada(c, w_ada, b_ada[l][None, :], l)
        shift_m, scale_m, gate_m, shift_f, scale_f, gate_f = [
            m.reshape(bsz, 1, d) for m in jnp.split(mod, 6, axis=-1)]

        bfp = jnp.pad(b_forget[l], (0, LANES - N_HEADS))[None, :]
        pool, qa, kt, v = _premix(
            x, shift_m, scale_m, g_pre_mix[l][None, :], w_in, l, bfp, w_pool[l].astype(BF16),
            pool_scale[l][None, :], g_pool_out[l][None, :])
        ot, wgb, wub, wdb = _attn(qa, kt, v, w_gate[l], w_up[l], w_down[l])

        wr = jnp.pad(w_router[l], ((0, 0), (0, LANES - N_EXPERTS)))
        wrh = wr.astype(BF16)
        wrl = (wr - wrh.astype(F32)).astype(BF16)
        t = bsz * seq
        assert t // TW <= LANES
        x1, h2, infok, cnt = _postmix(
            x, pool, ot, w_out[l].astype(BF16), g_attn_out[l][None, :], g_post_mix[l][None, :],
            gate_m, g_pre_ffn[l][None, :], shift_f, scale_f,
            jnp.concatenate([wrh, wrl], axis=1), router_bias[l][:, None])
        tables, tails, block_expert, n_valid, n_rows = _dispatch_plan(cnt, t // TW)
        h2f = h2.reshape(t, d)
        xs = _dispatch(tables + tails + (n_valid,), h2f, infok, n_rows)
        ys = _experts(block_expert, n_valid, xs.reshape(n_rows, d), wgb, wub, wdb)
        ys = ys.reshape(xs.shape)
        out = _combine(tables, infok, h2f, x1.reshape(t, d), gate_f, g_post_ffn[l][None, :],
                       ws_gate[l].astype(BF16), ws_up[l].astype(BF16), ws_down[l].astype(BF16),
                       ys, seq)
        x = out.reshape(bsz, seq, d)
    return x
```

```python
import functools

import numpy as np
import jax
import jax.numpy as jnp
from jax import lax
from jax.experimental import pallas as pl
from jax.experimental.pallas import tpu as pltpu

D_MODEL = 1024
D_POOL = 512
POOL_WINDOWS = (2, 4, 8, 16)
POOL_GROUP = 128
MAX_WINDOW = max(POOL_WINDOWS)
D_ATTN = 512
HEAD_DIM = 64
N_HEADS = 8
N_EXPERTS = 64
N_EXPERT_GROUPS = 8
GROUP_SIZE = N_EXPERTS // N_EXPERT_GROUPS
TOPK_GROUPS = 4
TOP_K = 8
D_EXPERT = 256
D_SHARED = 256
ROUTED_SCALE = 2.5
EPS = 1e-6

LOG2E = 1.4426950408889634
LANES = 128
N_SPLIT = 3
AUG = LANES

TS_PRE = 1024
CUM_BLOCK = 256
TS_POST = 1024
TQ = 512
TK = 512
HEAD_PAIR = 128 // HEAD_DIM
TW = 256
PIECE = 16
RB = 1024
RSUB = 512
LCHUNK = 512
ONEHOT_GROUP = 64
LROWS = -(-(TOP_K * TW + N_EXPERTS * (PIECE - 1)) // LCHUNK) * LCHUNK

F32 = jnp.float32
BF16 = jnp.bfloat16
VMEM_LIMIT = 56 * 1024 * 1024


def _rms(v, g):
    return v * lax.rsqrt(jnp.mean(v * v, axis=-1, keepdims=True) + EPS) * g


def _split3(v):
    hi = v.astype(BF16)
    r1 = v - hi.astype(F32)
    mid = r1.astype(BF16)
    r2 = r1 - mid.astype(F32)
    lo = r2.astype(BF16)
    return hi, mid, lo


def _dot(a, b):
    return jnp.dot(a, b, preferred_element_type=F32)


def _ada_kernel(c_ref, w_ref, b_ref, o_ref):
    o_ref[...] = _dot(c_ref[...].astype(BF16), w_ref[0].astype(BF16)) + b_ref[...]


def _ada(c, w, b, layer):
    bsz = c.shape[0]
    n = w.shape[2]
    return pl.pallas_call(
        _ada_kernel,
        name="ada",
        grid=(n // D_MODEL,),
        in_specs=[
            pl.BlockSpec((bsz, D_MODEL), lambda j: (0, 0)),
            pl.BlockSpec((1, D_MODEL, D_MODEL), lambda j: (layer, 0, j)),
            pl.BlockSpec((1, D_MODEL), lambda j: (0, j)),
        ],
        out_specs=pl.BlockSpec((bsz, D_MODEL), lambda j: (0, j)),
        out_shape=jax.ShapeDtypeStruct((bsz, n), F32),
    )(c, w, b)


def _premix_kernel(x_ref, shift_ref, scale_ref, g_ref, win_ref, bf_ref, wpool_ref,
                   pscale_ref, gpool_ref, place_ref, ones_ref, fmask_ref,
                   pool_ref, q_ref, kt_ref, v_ref,
                   uext_ref, cum_ref, w1_ref, wf_ref):
    s = pl.program_id(1)
    ts = x_ref.shape[1]

    @pl.when((pl.program_id(0) == 0) & (s == 0))
    def _():
        q0, k0 = D_POOL, D_POOL + D_ATTN
        w1_ref[:, :q0] = win_ref[0, :, :q0].astype(BF16)
        w1_ref[:, q0:k0] = (win_ref[0, :, q0:k0] * (HEAD_DIM ** -0.5 * LOG2E)).astype(BF16)
        w1_ref[:, k0:] = win_ref[0, :, k0:k0 + 2 * D_ATTN].astype(BF16)
        wf_ref[...] = jnp.zeros_like(wf_ref)
        wf_ref[:, :N_HEADS] = win_ref[0, :, k0 + 2 * D_ATTN:].astype(BF16)

    @pl.when(s == 0)
    def _():
        uext_ref[0:MAX_WINDOW, :] = jnp.zeros((MAX_WINDOW, D_POOL), F32)
        cum_ref[...] = jnp.zeros_like(cum_ref)

    x = x_ref[0]
    h = _rms(x, g_ref[...]) * (1.0 + scale_ref[0]) + shift_ref[0]
    hb = h.astype(BF16)
    proj = _dot(hb, w1_ref[...])
    u = proj[:, :D_POOL]
    q = proj[:, D_POOL:D_POOL + D_ATTN]
    k = proj[:, D_POOL + D_ATTN:D_POOL + 2 * D_ATTN]
    v = proj[:, D_POOL + 2 * D_ATTN:]

    uext_ref[MAX_WINDOW:, :] = u
    pos = (s * ts + lax.broadcasted_iota(jnp.int32, (ts, 1), 0) + 1).astype(F32)
    ys = []
    for g, w in enumerate(POOL_WINDOWS):
        c0 = g * POOL_GROUP
        acc = uext_ref[MAX_WINDOW:, c0:c0 + POOL_GROUP]
        for j in range(1, w):
            acc = acc + uext_ref[MAX_WINDOW - j:MAX_WINDOW - j + ts, c0:c0 + POOL_GROUP]
        pooled = acc / jnp.minimum(pos, float(w)) - u[:, c0:c0 + POOL_GROUP]
        ys.append(_dot(pooled.astype(BF16), wpool_ref[g]))
    ypool = jnp.concatenate(ys, axis=1) * pscale_ref[...]
    pool_ref[0] = _rms(ypool, gpool_ref[...]).astype(BF16)
    uext_ref[0:MAX_WINDOW, :] = uext_ref[ts:ts + MAX_WINDOW, :]

    z0 = _dot(hb, wf_ref[...]) + bf_ref[...]
    z = z0
    for p in range(1, N_SPLIT):
        z = z + pltpu.roll(z0, p * N_HEADS, 1)
    logf = jnp.minimum(z, 0.0) - jnp.log1p(jnp.exp(-jnp.abs(z)))
    row = lax.broadcasted_iota(jnp.int32, (CUM_BLOCK, CUM_BLOCK), 0)
    col = lax.broadcasted_iota(jnp.int32, (CUM_BLOCK, CUM_BLOCK), 1)
    tri = (col <= row).astype(BF16)
    pieces = _split3(logf)
    carry = cum_ref[...]
    blocks = []
    for r0 in range(0, ts, CUM_BLOCK):
        blk = carry
        for piece in pieces:
            blk = blk + _dot(tri, piece[r0:r0 + CUM_BLOCK, :])
        carry = blk[CUM_BLOCK - 1:CUM_BLOCK, :]
        blocks.append(blk)
    cum = jnp.concatenate(blocks, axis=0)
    cum_ref[...] = carry

    hi, mid, lo = [piece.astype(F32) for piece in _split3(cum * LOG2E)]
    lane = lax.broadcasted_iota(jnp.int32, (ts, LANES), 1)
    pieces = jnp.where(lane < N_HEADS, hi, jnp.where(lane < 2 * N_HEADS, mid, lo))
    aug = _dot(pieces.astype(BF16), place_ref[...]) + ones_ref[...]
    aug_q = aug[:, :N_HEADS * AUG]
    aug_k = aug[:, N_HEADS * AUG:]

    def expand(a):
        blocks = []
        for j in range(D_ATTN // LANES):
            blk = a[:, j * LANES:(j + 1) * LANES]
            blocks += [blk, blk]
        return jnp.concatenate(blocks, axis=1)

    fmask = fmask_ref[...]
    qa = expand(q) * fmask + aug_q
    ka = expand(k) * fmask + aug_k
    q_ref[0] = qa.astype(BF16)
    kt_ref[0] = ka.T.astype(BF16)
    v_ref[0] = v.astype(BF16)


def _aug_constants():
    width = N_HEADS * AUG
    place = np.zeros((LANES, 2 * width), np.float32)
    ones = np.zeros((1, 2 * width), np.float32)
    fmask = np.zeros((1, width), np.float32)
    for h in range(N_HEADS):
        feat0 = h * AUG + (0 if h % 2 == 0 else HEAD_DIM)
        aug0 = h * AUG + (HEAD_DIM if h % 2 == 0 else 0)
        fmask[0, feat0:feat0 + HEAD_DIM] = 1.0
        for p in range(N_SPLIT):
            place[p * N_HEADS + h, aug0 + p] = 1.0
            ones[0, width + aug0 + p] = 1.0
            place[p * N_HEADS + h, width + aug0 + N_SPLIT + p] = -1.0
            ones[0, aug0 + N_SPLIT + p] = 1.0
    return jnp.asarray(place, BF16), jnp.asarray(ones), jnp.asarray(fmask)


def _premix(x, shift, scale, g, w_in, layer, bfp, wpool, pscale, gpool):
    bsz, seq, _ = x.shape
    d_in = w_in.shape[2]
    assert d_in == D_POOL + 3 * D_ATTN + N_HEADS
    ts = TS_PRE
    place, ones, fmask = _aug_constants()
    full = lambda shape: pl.BlockSpec(shape, lambda b, s: (0,) * len(shape))
    per_batch = pl.BlockSpec((1, 1, D_MODEL), lambda b, s: (b, 0, 0))
    return pl.pallas_call(
        _premix_kernel,
        name="premix",
        grid=(bsz, seq // ts),
        in_specs=[
            pl.BlockSpec((1, ts, D_MODEL), lambda b, s: (b, s, 0)),
            per_batch, per_batch,
            full((1, D_MODEL)),
            pl.BlockSpec((1, D_MODEL, d_in), lambda b, s: (layer, 0, 0),
                         pipeline_mode=pl.Buffered(1)),
            full((1, LANES)),
            full((len(POOL_WINDOWS), POOL_GROUP, POOL_GROUP)),
            full((1, D_POOL)),
            full((1, D_POOL)),
            full((LANES, 2 * N_HEADS * AUG)),
            full((1, 2 * N_HEADS * AUG)),
            full((1, N_HEADS * AUG)),
        ],
        out_specs=[
            pl.BlockSpec((1, ts, D_POOL), lambda b, s: (b, s, 0)),
            pl.BlockSpec((1, ts, N_HEADS * AUG), lambda b, s: (b, s, 0)),
            pl.BlockSpec((1, N_HEADS * AUG, ts), lambda b, s: (b, 0, s)),
            pl.BlockSpec((1, ts, D_ATTN), lambda b, s: (b, s, 0)),
        ],
        out_shape=[
            jax.ShapeDtypeStruct((bsz, seq, D_POOL), BF16),
            jax.ShapeDtypeStruct((bsz, seq, N_HEADS * AUG), BF16),
            jax.ShapeDtypeStruct((bsz, N_HEADS * AUG, seq), BF16),
            jax.ShapeDtypeStruct((bsz, seq, D_ATTN), BF16),
        ],
        scratch_shapes=[
            pltpu.VMEM((ts + MAX_WINDOW, D_POOL), F32),
            pltpu.VMEM((1, LANES), F32),
            pltpu.VMEM((D_MODEL, D_POOL + 3 * D_ATTN), BF16),
            pltpu.VMEM((D_MODEL, LANES), BF16),
        ],
        compiler_params=pltpu.CompilerParams(
            dimension_semantics=("arbitrary", "arbitrary"), vmem_limit_bytes=VMEM_LIMIT),
    )(x, shift, scale, g, w_in, bfp, wpool, pscale, gpool, place, ones, fmask)


def _attn_kernel(q_ref, kt_ref, v_ref, wg_ref, wu_ref, wd_ref,
                 o_ref, wgb_ref, wub_ref, wdb_ref, xs_ref, zbuf, zsem, *, n_steps):
    step = pl.program_id(0) * pl.num_programs(1) + pl.program_id(1)
    block_pieces = zbuf.shape[0]
    n_blocks = xs_ref.shape[0] // block_pieces
    per_step = -(-n_blocks // n_steps)

    @pl.when(step == 0)
    def _():
        zbuf[...] = jnp.zeros_like(zbuf)

    def zero_fill(fn):
        for j in range(per_step):
            blk = step * per_step + j

            @pl.when(blk < n_blocks)
            def _():
                fn(pltpu.make_async_copy(
                    zbuf, xs_ref.at[pl.ds(blk * block_pieces, block_pieces)], zsem))

    zero_fill(lambda copy: copy.start())
    wgb_ref[...] = wg_ref[...].astype(BF16)
    wub_ref[...] = wu_ref[...].astype(BF16)
    wdb_ref[...] = wd_ref[...].astype(BF16)
    seq = q_ref.shape[1]
    qry_i = lax.broadcasted_iota(jnp.int32, (TQ, TK), 0)
    key_i = lax.broadcasted_iota(jnp.int32, (TQ, TK), 1)
    causal = key_i <= qry_i
    one_lane = lax.broadcasted_iota(jnp.int32, (seq, LANES), 1) == 0
    vaug = jnp.concatenate([v_ref[0], jnp.where(one_lane, 1.0, 0.0).astype(BF16)], axis=1)
    out_lane = lax.broadcasted_iota(jnp.int32, (TQ, LANES), 1)
    heads = range(HEAD_PAIR)
    for qi in range(seq // TQ):
        rows = slice(qi * TQ, (qi + 1) * TQ)
        q = [q_ref[0, rows, h * AUG:(h + 1) * AUG] for h in heads]
        m = [jnp.full((TQ, 1), -jnp.inf, F32) for _ in heads]
        acc = [jnp.zeros((TQ, 2 * LANES), F32) for _ in heads]
        for kj in range(qi + 1):
            keys = slice(kj * TK, (kj + 1) * TK)
            for h in heads:
                s = _dot(q[h], kt_ref[0, h * AUG:(h + 1) * AUG, keys])
                if kj == qi:
                    s = jnp.where(causal, s, -jnp.inf)
                m_new = jnp.maximum(m[h], jnp.max(s, axis=1, keepdims=True))
                p = jnp.exp2(s - m_new).astype(BF16)
                acc[h] = jnp.exp2(m[h] - m_new) * acc[h] + _dot(p, vaug[keys, :])
                m[h] = m_new
        o = [acc[h][:, :LANES] / acc[h][:, LANES:LANES + 1] for h in heads]
        o_ref[0, rows, :] = jnp.where(out_lane < HEAD_DIM, o[0], o[1])
    zero_fill(lambda copy: copy.wait())


def _attn(q, kt, v, wg, wu, wd, n_rows):
    bsz, seq, _ = q.shape
    pairs = N_HEADS // HEAD_PAIR
    per_step = N_EXPERTS // (bsz * pairs)
    assert per_step * bsz * pairs == N_EXPERTS
    wspec = lambda shape: pl.BlockSpec((per_step,) + shape, lambda b, h: (b * pairs + h, 0, 0))
    return pl.pallas_call(
        functools.partial(_attn_kernel, n_steps=bsz * pairs),
        name="attn",
        grid=(bsz, pairs),
        in_specs=[
            pl.BlockSpec((1, seq, HEAD_PAIR * AUG), lambda b, h: (b, 0, h)),
            pl.BlockSpec((1, HEAD_PAIR * AUG, seq), lambda b, h: (b, h, 0)),
            pl.BlockSpec((1, seq, LANES), lambda b, h: (b, 0, h)),
            wspec((D_MODEL, D_EXPERT)), wspec((D_MODEL, D_EXPERT)), wspec((D_EXPERT, D_MODEL)),
        ],
        out_specs=[
            pl.BlockSpec((1, seq, LANES), lambda b, h: (b, 0, h)),
            wspec((D_MODEL, D_EXPERT)), wspec((D_MODEL, D_EXPERT)), wspec((D_EXPERT, D_MODEL)),
            pl.BlockSpec(memory_space=pl.ANY),
        ],
        out_shape=[
            jax.ShapeDtypeStruct((bsz, seq, D_ATTN), F32),
            jax.ShapeDtypeStruct(wg.shape, BF16),
            jax.ShapeDtypeStruct(wu.shape, BF16),
            jax.ShapeDtypeStruct(wd.shape, BF16),
            jax.ShapeDtypeStruct((n_rows // PIECE, PIECE, D_MODEL), BF16),
        ],
        scratch_shapes=[
            pltpu.VMEM((RB // PIECE, PIECE, D_MODEL), BF16),
            pltpu.SemaphoreType.DMA(()),
        ],
        compiler_params=pltpu.CompilerParams(
            dimension_semantics=("arbitrary", "arbitrary"), vmem_limit_bytes=VMEM_LIMIT),
    )(q, kt, v, wg, wu, wd)


def _postmix_kernel(x_ref, pool_ref, o_ref, wout_ref, gattn_ref, gpost_ref, gate_ref,
                    gffn_ref, shift_ref, scale_ref, wr_ref, bias_ref,
                    x1_ref, h2_ref, infok_ref, cnt_ref):
    ya = _rms(o_ref[0], gattn_ref[...]).astype(BF16)
    mixed = _dot(pool_ref[0], wout_ref[:D_POOL, :]) + _dot(ya, wout_ref[D_POOL:, :])
    x1 = x_ref[0] + gate_ref[0] * _rms(mixed, gpost_ref[...])
    x1_ref[0] = x1
    h2 = _rms(x1, gffn_ref[...]) * (1.0 + scale_ref[0]) + shift_ref[0]
    h2b = h2.astype(BF16)
    h2_ref[0] = h2b
    h2l = (h2 - h2b.astype(F32)).astype(BF16)
    both = _dot(h2b, wr_ref[...])
    logits = both[:, :LANES] + both[:, LANES:] + _dot(h2l, wr_ref[:, :LANES])
    st = jax.nn.sigmoid(logits).T

    step = pl.program_id(0) * pl.num_programs(1) + pl.program_id(1)

    @pl.when(step == 0)
    def _():
        cnt_ref[...] = jnp.zeros_like(cnt_ref)

    lane = lax.broadcasted_iota(jnp.int32, cnt_ref.shape, 1)
    tiles = x_ref.shape[1] // TW
    for i in range(tiles):
        info, n = _route_tile(st[:, i * TW:(i + 1) * TW], bias_ref)
        infok_ref[:, i * TW:(i + 1) * TW] = info
        cnt_ref[...] = jnp.where(lane == step * tiles + i, jnp.broadcast_to(n, cnt_ref.shape),
                                 cnt_ref[...])


def _postmix(x, pool, ot, wout, gattn, gpost, gate, gffn, shift, scale, wr, bias_col):
    bsz, seq, _ = x.shape
    ts = TS_POST
    ns = seq // ts
    full = lambda shape: pl.BlockSpec(shape, lambda b, s: (0,) * len(shape))
    per_batch = pl.BlockSpec((1, 1, D_MODEL), lambda b, s: (b, 0, 0))
    tok = lambda width: pl.BlockSpec((1, ts, width), lambda b, s: (b, s, 0))
    return pl.pallas_call(
        _postmix_kernel,
        name="postmix",
        grid=(bsz, ns),
        in_specs=[
            tok(D_MODEL), tok(D_POOL), tok(D_ATTN),
            full((D_MODEL, D_MODEL)),
            full((1, D_ATTN)), full((1, D_MODEL)), per_batch,
            full((1, D_MODEL)), per_batch, per_batch,
            full((D_MODEL, 2 * LANES)),
            full((N_EXPERTS, 1)),
        ],
        out_specs=[
            tok(D_MODEL), tok(D_MODEL),
            pl.BlockSpec((2 * TOP_K, ts), lambda b, s: (0, b * ns + s)),
            full((N_EXPERTS, LANES)),
        ],
        out_shape=[
            jax.ShapeDtypeStruct((bsz, seq, D_MODEL), F32),
            jax.ShapeDtypeStruct((bsz, seq, D_MODEL), BF16),
            jax.ShapeDtypeStruct((2 * TOP_K, bsz * seq), F32),
            jax.ShapeDtypeStruct((N_EXPERTS, LANES), F32),
        ],
        compiler_params=pltpu.CompilerParams(
            dimension_semantics=("arbitrary", "arbitrary"), vmem_limit_bytes=VMEM_LIMIT),
    )(x, pool, ot, wout, gattn, gpost, gate, gffn, shift, scale, wr, bias_col)


def _route_tile(st, bias_ref):
    tr = st.shape[1]
    scores = [st[g * GROUP_SIZE:(g + 1) * GROUP_SIZE, :] for g in range(N_EXPERT_GROUPS)]
    sel = [scores[g] + bias_ref[g * GROUP_SIZE:(g + 1) * GROUP_SIZE, :]
           for g in range(N_EXPERT_GROUPS)]
    sub = lax.broadcasted_iota(jnp.int32, (GROUP_SIZE, tr), 0)
    neg = jnp.float32(-jnp.inf)

    rows = []
    for g in range(N_EXPERT_GROUPS):
        v = sel[g]
        m1 = jnp.max(v, axis=0, keepdims=True)
        first = jnp.min(jnp.where(v == m1, sub, GROUP_SIZE), axis=0, keepdims=True)
        m2 = jnp.max(jnp.where(sub == first, neg, v), axis=0, keepdims=True)
        rows.append(m1 + m2)
    gs = jnp.concatenate(rows, axis=0)

    shape = (GROUP_SIZE, tr)
    one = jnp.ones(shape, jnp.int32)
    zero = jnp.zeros(shape, jnp.int32)
    beaten = zero
    for gp in range(N_EXPERT_GROUPS):
        r = jnp.broadcast_to(gs[gp:gp + 1, :], shape)
        tie = jnp.where(sub > gp, one, zero)
        beaten = beaten + jnp.where(r > gs, one, zero) + jnp.where(r == gs, tie, zero)
    keep = jnp.where(beaten < TOPK_GROUPS, 1.0, 0.0)

    masked = [jnp.where(jnp.broadcast_to(keep[g:g + 1, :], shape) > 0.5, sel[g], neg)
              for g in range(N_EXPERT_GROUPS)]

    eidx = [sub + g * GROUP_SIZE for g in range(N_EXPERT_GROUPS)]
    picked = [zero for _ in range(N_EXPERT_GROUPS)]
    for _ in range(TOP_K):
        best = masked[0]
        for g in range(1, N_EXPERT_GROUPS):
            best = jnp.maximum(best, masked[g])
        best = jnp.broadcast_to(jnp.max(best, axis=0, keepdims=True), shape)
        first = jnp.where(masked[0] == best, eidx[0], N_EXPERTS)
        for g in range(1, N_EXPERT_GROUPS):
            first = jnp.minimum(first, jnp.where(masked[g] == best, eidx[g], N_EXPERTS))
        first = jnp.broadcast_to(jnp.min(first, axis=0, keepdims=True), shape)
        for g in range(N_EXPERT_GROUPS):
            hit = eidx[g] == first
            picked[g] = jnp.where(hit, one, picked[g])
            masked[g] = jnp.where(hit, neg, masked[g])
    chosen = [picked[g] > 0 for g in range(N_EXPERT_GROUPS)]

    w = [jnp.where(chosen[g], scores[g], 0.0) for g in range(N_EXPERT_GROUPS)]
    denom = w[0].sum(axis=0, keepdims=True)
    for g in range(1, N_EXPERT_GROUPS):
        denom = denom + w[g].sum(axis=0, keepdims=True)
    wfull = jnp.concatenate([w[g] / denom * ROUTED_SCALE for g in range(N_EXPERT_GROUPS)], axis=0)
    mfull = jnp.concatenate([jnp.where(chosen[g], 1.0, 0.0) for g in range(N_EXPERT_GROUPS)],
                            axis=0)
    mb = mfull.astype(BF16)

    e_r = lax.broadcasted_iota(jnp.int32, (N_EXPERTS, N_EXPERTS), 0)
    e_c = lax.broadcasted_iota(jnp.int32, (N_EXPERTS, N_EXPERTS), 1)
    before_e = (e_c < e_r).astype(BF16)
    t_r = lax.broadcasted_iota(jnp.int32, (tr, tr), 0)
    t_c = lax.broadcasted_iota(jnp.int32, (tr, tr), 1)
    before_t = (t_r < t_c).astype(BF16)
    ordinal = _dot(before_e, mb)
    rank = _dot(mb, before_t)
    n = jnp.sum(mfull, axis=1, keepdims=True)
    pieces = jnp.floor((n + (PIECE - 1.0)) * (1.0 / PIECE))
    run_start = PIECE * _dot(before_e, jnp.broadcast_to(pieces, (N_EXPERTS, LANES)).astype(BF16))
    pos = run_start[:, 0:1] + rank

    sub8 = lax.broadcasted_iota(jnp.int32, (TOP_K, tr), 0)
    pos8 = jnp.zeros((TOP_K, tr), F32)
    w8 = jnp.zeros((TOP_K, tr), F32)
    for k in range(TOP_K):
        selk = jnp.where(ordinal == float(k), mfull, 0.0)
        pk = jnp.sum(selk * pos, axis=0, keepdims=True)
        wk = jnp.sum(selk * wfull, axis=0, keepdims=True)
        pos8 = jnp.where(sub8 == k, jnp.broadcast_to(pk, (TOP_K, tr)), pos8)
        w8 = jnp.where(sub8 == k, jnp.broadcast_to(wk, (TOP_K, tr)), w8)
    return jnp.concatenate([pos8, w8], axis=0), n


def _swiglu(xb, wg, wu):
    g = _dot(xb, wg)
    return (g * jax.nn.sigmoid(g)) * _dot(xb, wu)


def _piece_copy(src_ref, src_piece, dst_ref, dst_piece, sem, n_pieces=1):
    src = src_ref.at[pl.ds(src_piece, n_pieces)]
    dst = dst_ref.at[pl.ds(dst_piece, n_pieces)]
    return pltpu.make_async_copy(src, dst, sem)


def _as_pieces(rows):
    return rows.reshape(rows.shape[0] // PIECE, PIECE, rows.shape[1])


LOCAL_BITS = 8
COPY_SIZES = (4, 3, 2, 1)
MAX_COPIES = N_EXPERTS
assert LROWS // (COPY_SIZES[0] * PIECE) <= MAX_COPIES


def _for_each_copy(copies_ref, counts_ref, tile, fn):
    for c, size in enumerate(COPY_SIZES):
        base = (tile * len(COPY_SIZES) + c) * MAX_COPIES

        def body(i, carry, base=base, size=size):
            word = copies_ref[base + i]
            fn(word & ((1 << LOCAL_BITS) - 1), lax.shift_right_logical(word, LOCAL_BITS), size)
            return carry

        lax.fori_loop(0, counts_ref[tile * (len(COPY_SIZES) + 1) + c], body, 0)


def _tile_pieces(counts_ref, tile):
    return counts_ref[tile * (len(COPY_SIZES) + 1) + len(COPY_SIZES)]


def _wait_pieces(total, copy_of):
    for bit in range((LROWS // PIECE).bit_length()):
        @pl.when((lax.shift_right_logical(total, bit) & 1) == 1)
        def _():
            copy_of(1 << bit).wait()


def _onehot_chunks(infok_ref, weighted):
    pos = infok_ref[0:TOP_K, :]
    grp = jnp.floor(pos * (1.0 / ONEHOT_GROUP))
    off = pos - ONEHOT_GROUP * grp
    sub = lax.broadcasted_iota(jnp.int32, (ONEHOT_GROUP, TW), 0).astype(F32)
    offs, grps = [], []
    for k in range(TOP_K):
        value = infok_ref[TOP_K + k:TOP_K + k + 1, :] if weighted else 1.0
        offs.append(jnp.where(sub == off[k:k + 1, :], value, 0.0).astype(BF16))
        grps.append(jnp.broadcast_to(grp[k:k + 1, :], (PIECE, TW)).astype(BF16))
    zero = jnp.zeros((PIECE, TW), BF16)

    def chunk(j):
        tiles = []
        for a in range(j * LCHUNK // ONEHOT_GROUP, (j + 1) * LCHUNK // ONEHOT_GROUP):
            hit = [grps[k] == a for k in range(TOP_K)]
            for i in range(ONEHOT_GROUP // PIECE):
                rows = zero
                for k in range(TOP_K):
                    rows = rows + jnp.where(hit[k], offs[k][i * PIECE:(i + 1) * PIECE, :], zero)
                tiles.append(rows)
        return jnp.concatenate(tiles, axis=0)

    return chunk


def _dispatch_kernel(copies_ref, counts_ref, h2_ref, infok_ref, zeros_ref, xs_ref, lbuf, sem):
    del zeros_ref
    tile = pl.program_id(0)
    last = pl.num_programs(0) - 1
    slot = tile % 2

    def start(t, s):
        _for_each_copy(
            copies_ref, counts_ref, t,
            lambda lp, gp, n: _piece_copy(lbuf.at[s], lp, xs_ref, gp, sem.at[s], n).start())

    def wait(t, s):
        _wait_pieces(_tile_pieces(counts_ref, t),
                     lambda n: _piece_copy(lbuf.at[s], 0, xs_ref, 0, sem.at[s], n))

    @pl.when(tile >= 2)
    def _():
        wait(tile - 2, slot)

    onehot = _onehot_chunks(infok_ref, weighted=False)
    xb = h2_ref[...]
    used_rows = PIECE * _tile_pieces(counts_ref, tile)

    def sort_chunk(j):
        lbuf[slot, j * LCHUNK // PIECE:(j + 1) * LCHUNK // PIECE] = _as_pieces(
            _dot(onehot(j), xb).astype(BF16))

    n_chunks = LROWS // LCHUNK
    for j in range(n_chunks - 1):
        sort_chunk(j)
    pl.when(used_rows > (n_chunks - 1) * LCHUNK)(functools.partial(sort_chunk, n_chunks - 1))

    start(tile, slot)

    @pl.when(tile == last)
    def _():
        @pl.when(tile >= 1)
        def _():
            wait(tile - 1, 1 - slot)
        wait(tile, slot)


def _dispatch(tables, h2, infok, zeros):
    t = h2.shape[0]
    return pl.pallas_call(
        _dispatch_kernel,
        name="dispatch",
        grid_spec=pltpu.PrefetchScalarGridSpec(
            num_scalar_prefetch=2,
            grid=(t // TW,),
            in_specs=[
                pl.BlockSpec((TW, D_MODEL), lambda i, *_: (i, 0)),
                pl.BlockSpec((2 * TOP_K, TW), lambda i, *_: (0, i)),
                pl.BlockSpec(memory_space=pl.ANY),
            ],
            out_specs=pl.BlockSpec(memory_space=pl.ANY),
            scratch_shapes=[
                pltpu.VMEM((2, LROWS // PIECE, PIECE, D_MODEL), BF16),
                pltpu.SemaphoreType.DMA((2,)),
            ],
        ),
        out_shape=jax.ShapeDtypeStruct(zeros.shape, BF16),
        input_output_aliases={4: 0},
        compiler_params=pltpu.CompilerParams(
            dimension_semantics=("arbitrary",), vmem_limit_bytes=VMEM_LIMIT),
    )(*tables, h2, infok, zeros)


def _experts_kernel(be_ref, nv_ref, xs_ref, wg_ref, wu_ref, wd_ref, ys_ref):
    @pl.when(pl.program_id(0) < nv_ref[0])
    def _():
        for j in range(RB // RSUB):
            rows = slice(j * RSUB, (j + 1) * RSUB)
            h = _swiglu(xs_ref[rows, :], wg_ref[0], wu_ref[0])
            ys_ref[rows, :] = _dot(h.astype(BF16), wd_ref[0]).astype(BF16)


def _experts(block_expert, n_valid, xs, wg, wu, wd):
    n_blocks = xs.shape[0] // RB
    rows = pl.BlockSpec((RB, D_MODEL), lambda i, be, nv: (jnp.minimum(i, nv[0] - 1), 0))
    return pl.pallas_call(
        _experts_kernel,
        name="experts",
        grid_spec=pltpu.PrefetchScalarGridSpec(
            num_scalar_prefetch=2,
            grid=(n_blocks,),
            in_specs=[
                rows,
                pl.BlockSpec((1, D_MODEL, D_EXPERT), lambda i, be, nv: (be[i], 0, 0)),
                pl.BlockSpec((1, D_MODEL, D_EXPERT), lambda i, be, nv: (be[i], 0, 0)),
                pl.BlockSpec((1, D_EXPERT, D_MODEL), lambda i, be, nv: (be[i], 0, 0)),
            ],
            out_specs=rows,
        ),
        out_shape=jax.ShapeDtypeStruct(xs.shape, BF16),
        input_output_aliases={2: 0},
        compiler_params=pltpu.CompilerParams(
            dimension_semantics=("arbitrary",), vmem_limit_bytes=VMEM_LIMIT),
    )(block_expert, n_valid, xs, wg, wu, wd)


def _combine_kernel(copies_ref, counts_ref,
                    infok_ref, h2_ref, x1_ref, gate_ref, gpost_ref, wsg_ref, wsu_ref, wsd_ref,
                    ys_ref, o_ref, ybuf, acc_ref, sem):
    tile = pl.program_id(0)
    last = pl.num_programs(0) - 1
    slot = tile % 2

    def start(t, s):
        _for_each_copy(
            copies_ref, counts_ref, t,
            lambda lp, gp, n: _piece_copy(ys_ref, gp, ybuf.at[s], lp, sem.at[s], n).start())

    def wait(t, s):
        _wait_pieces(_tile_pieces(counts_ref, t),
                     lambda n: _piece_copy(ys_ref, 0, ybuf.at[s], 0, sem.at[s], n))

    @pl.when(tile == 0)
    def _():
        ybuf[...] = jnp.zeros_like(ybuf)
        start(tile, slot)

    @pl.when(tile < last)
    def _():
        start(tile + 1, 1 - slot)

    wait(tile, slot)

    weights = _onehot_chunks(infok_ref, weighted=True)
    hs = _swiglu(h2_ref[...], wsg_ref[...], wsu_ref[...])
    ff = _dot(hs.astype(BF16), wsd_ref[...])
    used_rows = PIECE * _tile_pieces(counts_ref, tile)

    def sum_chunk(j):
        rows = ybuf[slot, j * LCHUNK // PIECE:(j + 1) * LCHUNK // PIECE].reshape(LCHUNK, D_MODEL)
        return lax.dot_general(weights(j), rows,
                               (((0,), (0,)), ((), ())), preferred_element_type=F32)

    def add_chunk(j):
        acc_ref[...] += sum_chunk(j)

    n_chunks = LROWS // LCHUNK
    for j in range(n_chunks - 1):
        ff = ff + sum_chunk(j)
    acc_ref[...] = ff
    pl.when(used_rows > (n_chunks - 1) * LCHUNK)(functools.partial(add_chunk, n_chunks - 1))
    o_ref[...] = x1_ref[...] + gate_ref[0] * _rms(acc_ref[...], gpost_ref[...])


def _combine(tables, infok, h2, x1, gate, gpost, wsg, wsu, wsd, ys, seq):
    t = h2.shape[0]
    per_seq = seq // TW
    full = lambda shape: pl.BlockSpec(shape, lambda i, *_: (0,) * len(shape))
    tok = lambda width: pl.BlockSpec((TW, width), lambda i, *_: (i, 0))
    return pl.pallas_call(
        _combine_kernel,
        name="combine",
        grid_spec=pltpu.PrefetchScalarGridSpec(
            num_scalar_prefetch=2,
            grid=(t // TW,),
            in_specs=[
                pl.BlockSpec((2 * TOP_K, TW), lambda i, *_: (0, i)), tok(D_MODEL), tok(D_MODEL),
                pl.BlockSpec((1, 1, D_MODEL), lambda i, *_: (i // per_seq, 0, 0)),
                full((1, D_MODEL)),
                full((D_MODEL, D_SHARED)), full((D_MODEL, D_SHARED)), full((D_SHARED, D_MODEL)),
                pl.BlockSpec(memory_space=pl.ANY),
            ],
            out_specs=tok(D_MODEL),
            scratch_shapes=[
                pltpu.VMEM((2, LROWS // PIECE, PIECE, D_MODEL), BF16),
                pltpu.VMEM((TW, D_MODEL), F32),
                pltpu.SemaphoreType.DMA((2,)),
            ],
        ),
        out_shape=jax.ShapeDtypeStruct((t, D_MODEL), F32),
        compiler_params=pltpu.CompilerParams(
            dimension_semantics=("arbitrary",), vmem_limit_bytes=VMEM_LIMIT),
    )(*tables, infok, h2, x1, gate, gpost, wsg, wsu, wsd, ys)


def _row_buffer_blocks(n_tiles):
    return (TOP_K * TW * n_tiles + N_EXPERTS * n_tiles * (PIECE - 1)
            + N_EXPERTS * (RB - PIECE) + RB - 1) // RB


def _dispatch_plan(cnt, n_tiles):
    n = cnt[:, :n_tiles].astype(jnp.int32)
    pieces = (n + PIECE - 1) // PIECE
    local = jnp.cumsum(pieces, axis=0) - pieces
    seg = jnp.sum(pieces, axis=1)
    per_block = RB // PIECE
    seg_pad = (seg + per_block - 1) // per_block * per_block
    seg_end = jnp.cumsum(seg_pad)
    seg_start = seg_end - seg_pad
    glob = seg_start[:, None] + jnp.cumsum(pieces, axis=1) - pieces
    n_blocks = _row_buffer_blocks(n_tiles)
    n_valid = seg_end[-1] // per_block
    blk = jnp.minimum(jnp.arange(n_blocks, dtype=jnp.int32), n_valid - 1)
    block_end = seg_end // per_block
    block_expert = jnp.sum((block_end[None, :] <= blk[:, None]).astype(jnp.int32), axis=1)
    block_expert = jnp.minimum(block_expert, N_EXPERTS - 1)
    assert LROWS // PIECE <= 1 << LOCAL_BITS and n_blocks * per_block < 1 << (31 - LOCAL_BITS)

    def copy_list(count, width):
        end = jnp.cumsum(count, axis=0).T
        slot = jnp.arange(width, dtype=jnp.int32)
        expert = jnp.sum((end[:, None, :] <= slot[None, :, None]).astype(jnp.int32), axis=-1)
        pick = expert[:, :, None] == jnp.arange(N_EXPERTS, dtype=jnp.int32)
        take = lambda a: jnp.sum(jnp.where(pick, a.T[:, None, :], 0), axis=-1)
        return slot[None, :] - take(jnp.cumsum(count, axis=0) - count), take

    big = COPY_SIZES[0]
    assert COPY_SIZES == tuple(range(big, 0, -1))
    n_big = pieces // big
    rest = pieces % big
    packed = local | (glob << LOCAL_BITS)
    step = 1 + (1 << LOCAL_BITS)
    within, take = copy_list(n_big, MAX_COPIES)
    lists = [take(packed) + big * step * within]
    counts = [jnp.sum(n_big, axis=0)]
    rest_start = packed + big * step * n_big
    for size in COPY_SIZES[1:]:
        has = (rest == size).astype(jnp.int32)
        _, take = copy_list(has, MAX_COPIES)
        lists.append(take(rest_start))
        counts.append(jnp.sum(has, axis=0))
    counts.append(jnp.sum(pieces, axis=0))
    tables = (jnp.stack(lists, axis=1).reshape(-1).astype(jnp.int32),
              jnp.stack(counts, axis=1).reshape(-1).astype(jnp.int32))
    return tables, block_expert, n_valid.reshape(1).astype(jnp.int32)


def kernel(x, c, w_ada, b_ada, g_pre_mix, g_post_mix, g_pre_ffn, g_post_ffn, w_in, b_forget,
           w_pool, pool_scale, g_pool_out, g_attn_out, w_out, w_router, router_bias,
           w_gate, w_up, w_down, ws_gate, ws_up, ws_down):
    bsz, seq, d = x.shape
    depth = w_ada.shape[0]
    for l in range(depth):
        mod = _ada(c, w_ada, b_ada[l][None, :], l)
        shift_m, scale_m, gate_m, shift_f, scale_f, gate_f = [
            m.reshape(bsz, 1, d) for m in jnp.split(mod, 6, axis=-1)]

        bfp = jnp.pad(b_forget[l], (0, LANES - N_HEADS))[None, :]
        pool, qa, kt, v = _premix(
            x, shift_m, scale_m, g_pre_mix[l][None, :], w_in, l, bfp, w_pool[l].astype(BF16),
            pool_scale[l][None, :], g_pool_out[l][None, :])
        t = bsz * seq
        assert t // TW <= LANES
        n_rows = _row_buffer_blocks(t // TW) * RB
        ot, wgb, wub, wdb, zeros = _attn(qa, kt, v, w_gate[l], w_up[l], w_down[l], n_rows)

        wr = jnp.pad(w_router[l], ((0, 0), (0, LANES - N_EXPERTS)))
        wrh = wr.astype(BF16)
        wrl = (wr - wrh.astype(F32)).astype(BF16)
        x1, h2, infok, cnt = _postmix(
            x, pool, ot, w_out[l].astype(BF16), g_attn_out[l][None, :], g_post_mix[l][None, :],
            gate_m, g_pre_ffn[l][None, :], shift_f, scale_f,
            jnp.concatenate([wrh, wrl], axis=1), router_bias[l][:, None])
        tables, block_expert, n_valid = _dispatch_plan(cnt, t // TW)
        h2f = h2.reshape(t, d)
        xs = _dispatch(tables, h2f, infok, zeros)
        ys = _experts(block_expert, n_valid, xs.reshape(n_rows, d), wgb, wub, wdb)
        ys = ys.reshape(xs.shape)
        out = _combine(tables, infok, h2f, x1.reshape(t, d), gate_f, g_post_ffn[l][None, :],
                       ws_gate[l].astype(BF16), ws_up[l].astype(BF16), ws_down[l].astype(BF16),
                       ys, seq)
        x = out.reshape(bsz, seq, d)
    return x
```

```python
import functools

import numpy as np
import jax
import jax.numpy as jnp
from jax import lax
from jax.experimental import pallas as pl
from jax.experimental.pallas import tpu as pltpu

D_MODEL = 1024
D_POOL = 512
POOL_WINDOWS = (2, 4, 8, 16)
POOL_GROUP = 128
MAX_WINDOW = max(POOL_WINDOWS)
D_ATTN = 512
HEAD_DIM = 64
N_HEADS = 8
N_EXPERTS = 64
N_EXPERT_GROUPS = 8
GROUP_SIZE = N_EXPERTS // N_EXPERT_GROUPS
TOPK_GROUPS = 4
TOP_K = 8
D_EXPERT = 256
D_SHARED = 256
ROUTED_SCALE = 2.5
EPS = 1e-6

LOG2E = 1.4426950408889634
LANES = 128
N_SPLIT = 3
AUG = LANES

TS_PRE = 1024
CUM_BLOCK = 256
TS_POST = 1024
TQ = 512
TK = 512
HEAD_PAIR = 128 // HEAD_DIM
TW = 256
PIECE = 16
RB = 1024
RSUB = 512
LCHUNK = 512
ONEHOT_GROUP = 64
LROWS = -(-(TOP_K * TW + N_EXPERTS * (PIECE - 1)) // LCHUNK) * LCHUNK

F32 = jnp.float32
BF16 = jnp.bfloat16
VMEM_LIMIT = 56 * 1024 * 1024


def _rms(v, g):
    return v * lax.rsqrt(jnp.mean(v * v, axis=-1, keepdims=True) + EPS) * g


def _split3(v):
    hi = v.astype(BF16)
    r1 = v - hi.astype(F32)
    mid = r1.astype(BF16)
    r2 = r1 - mid.astype(F32)
    lo = r2.astype(BF16)
    return hi, mid, lo


def _dot(a, b):
    return jnp.dot(a, b, preferred_element_type=F32)


def _ada_kernel(c_ref, w_ref, b_ref, o_ref):
    o_ref[...] = _dot(c_ref[...].astype(BF16), w_ref[0].astype(BF16)) + b_ref[...]


def _ada(c, w, b, layer):
    bsz = c.shape[0]
    n = w.shape[2]
    return pl.pallas_call(
        _ada_kernel,
        name="ada",
        grid=(n // D_MODEL,),
        in_specs=[
            pl.BlockSpec((bsz, D_MODEL), lambda j: (0, 0)),
            pl.BlockSpec((1, D_MODEL, D_MODEL), lambda j: (layer, 0, j)),
            pl.BlockSpec((1, D_MODEL), lambda j: (0, j)),
        ],
        out_specs=pl.BlockSpec((bsz, D_MODEL), lambda j: (0, j)),
        out_shape=jax.ShapeDtypeStruct((bsz, n), F32),
    )(c, w, b)


def _premix_kernel(x_ref, shift_ref, scale_ref, g_ref, win_ref, bf_ref, wpool_ref,
                   pscale_ref, gpool_ref, place_ref, ones_ref, fmask_ref,
                   pool_ref, q_ref, kt_ref, v_ref,
                   uext_ref, cum_ref, w1_ref, wf_ref):
    s = pl.program_id(1)
    ts = x_ref.shape[1]

    @pl.when((pl.program_id(0) == 0) & (s == 0))
    def _():
        q0, k0 = D_POOL, D_POOL + D_ATTN
        w1_ref[:, :q0] = win_ref[0, :, :q0].astype(BF16)
        w1_ref[:, q0:k0] = (win_ref[0, :, q0:k0] * (HEAD_DIM ** -0.5 * LOG2E)).astype(BF16)
        w1_ref[:, k0:] = win_ref[0, :, k0:k0 + 2 * D_ATTN].astype(BF16)
        wf_ref[...] = jnp.zeros_like(wf_ref)
        wf_ref[:, :N_HEADS] = win_ref[0, :, k0 + 2 * D_ATTN:].astype(BF16)

    @pl.when(s == 0)
    def _():
        uext_ref[0:MAX_WINDOW, :] = jnp.zeros((MAX_WINDOW, D_POOL), F32)
        cum_ref[...] = jnp.zeros_like(cum_ref)

    x = x_ref[0]
    h = _rms(x, g_ref[...]) * (1.0 + scale_ref[0]) + shift_ref[0]
    hb = h.astype(BF16)
    proj = _dot(hb, w1_ref[...])
    u = proj[:, :D_POOL]
    q = proj[:, D_POOL:D_POOL + D_ATTN]
    k = proj[:, D_POOL + D_ATTN:D_POOL + 2 * D_ATTN]
    v = proj[:, D_POOL + 2 * D_ATTN:]

    uext_ref[MAX_WINDOW:, :] = u
    pos = (s * ts + lax.broadcasted_iota(jnp.int32, (ts, 1), 0) + 1).astype(F32)
    ys = []
    for g, w in enumerate(POOL_WINDOWS):
        c0 = g * POOL_GROUP
        acc = uext_ref[MAX_WINDOW:, c0:c0 + POOL_GROUP]
        for j in range(1, w):
            acc = acc + uext_ref[MAX_WINDOW - j:MAX_WINDOW - j + ts, c0:c0 + POOL_GROUP]
        pooled = acc / jnp.minimum(pos, float(w)) - u[:, c0:c0 + POOL_GROUP]
        ys.append(_dot(pooled.astype(BF16), wpool_ref[g]))
    ypool = jnp.concatenate(ys, axis=1) * pscale_ref[...]
    pool_ref[0] = _rms(ypool, gpool_ref[...]).astype(BF16)
    uext_ref[0:MAX_WINDOW, :] = uext_ref[ts:ts + MAX_WINDOW, :]

    z0 = _dot(hb, wf_ref[...]) + bf_ref[...]
    z = z0
    for p in range(1, N_SPLIT):
        z = z + pltpu.roll(z0, p * N_HEADS, 1)
    logf = jnp.minimum(z, 0.0) - jnp.log1p(jnp.exp(-jnp.abs(z)))
    row = lax.broadcasted_iota(jnp.int32, (CUM_BLOCK, CUM_BLOCK), 0)
    col = lax.broadcasted_iota(jnp.int32, (CUM_BLOCK, CUM_BLOCK), 1)
    tri = (col <= row).astype(BF16)
    pieces = _split3(logf)
    carry = cum_ref[...]
    blocks = []
    for r0 in range(0, ts, CUM_BLOCK):
        blk = carry
        for piece in pieces:
            blk = blk + _dot(tri, piece[r0:r0 + CUM_BLOCK, :])
        carry = blk[CUM_BLOCK - 1:CUM_BLOCK, :]
        blocks.append(blk)
    cum = jnp.concatenate(blocks, axis=0)
    cum_ref[...] = carry

    hi, mid, lo = [piece.astype(F32) for piece in _split3(cum * LOG2E)]
    lane = lax.broadcasted_iota(jnp.int32, (ts, LANES), 1)
    pieces = jnp.where(lane < N_HEADS, hi, jnp.where(lane < 2 * N_HEADS, mid, lo))
    aug = _dot(pieces.astype(BF16), place_ref[...]) + ones_ref[...]
    aug_q = aug[:, :N_HEADS * AUG]
    aug_k = aug[:, N_HEADS * AUG:]

    def expand(a):
        blocks = []
        for j in range(D_ATTN // LANES):
            blk = a[:, j * LANES:(j + 1) * LANES]
            blocks += [blk, blk]
        return jnp.concatenate(blocks, axis=1)

    fmask = fmask_ref[...]
    qa = expand(q) * fmask + aug_q
    ka = expand(k) * fmask + aug_k
    q_ref[0] = qa.astype(BF16)
    kt_ref[0] = ka.T.astype(BF16)
    v_ref[0] = v.astype(BF16)


def _aug_constants():
    width = N_HEADS * AUG
    place = np.zeros((LANES, 2 * width), np.float32)
    ones = np.zeros((1, 2 * width), np.float32)
    fmask = np.zeros((1, width), np.float32)
    for h in range(N_HEADS):
        feat0 = h * AUG + (0 if h % 2 == 0 else HEAD_DIM)
        aug0 = h * AUG + (HEAD_DIM if h % 2 == 0 else 0)
        fmask[0, feat0:feat0 + HEAD_DIM] = 1.0
        for p in range(N_SPLIT):
            place[p * N_HEADS + h, aug0 + p] = 1.0
            ones[0, width + aug0 + p] = 1.0
            place[p * N_HEADS + h, width + aug0 + N_SPLIT + p] = -1.0
            ones[0, aug0 + N_SPLIT + p] = 1.0
    return jnp.asarray(place, BF16), jnp.asarray(ones), jnp.asarray(fmask)


def _premix(x, shift, scale, g, w_in, layer, bfp, wpool, pscale, gpool):
    bsz, seq, _ = x.shape
    d_in = w_in.shape[2]
    assert d_in == D_POOL + 3 * D_ATTN + N_HEADS
    ts = TS_PRE
    place, ones, fmask = _aug_constants()
    full = lambda shape: pl.BlockSpec(shape, lambda b, s: (0,) * len(shape))
    per_batch = pl.BlockSpec((1, 1, D_MODEL), lambda b, s: (b, 0, 0))
    return pl.pallas_call(
        _premix_kernel,
        name="premix",
        grid=(bsz, seq // ts),
        in_specs=[
            pl.BlockSpec((1, ts, D_MODEL), lambda b, s: (b, s, 0)),
            per_batch, per_batch,
            full((1, D_MODEL)),
            pl.BlockSpec((1, D_MODEL, d_in), lambda b, s: (layer, 0, 0),
                         pipeline_mode=pl.Buffered(1)),
            full((1, LANES)),
            full((len(POOL_WINDOWS), POOL_GROUP, POOL_GROUP)),
            full((1, D_POOL)),
            full((1, D_POOL)),
            full((LANES, 2 * N_HEADS * AUG)),
            full((1, 2 * N_HEADS * AUG)),
            full((1, N_HEADS * AUG)),
        ],
        out_specs=[
            pl.BlockSpec((1, ts, D_POOL), lambda b, s: (b, s, 0)),
            pl.BlockSpec((1, ts, N_HEADS * AUG), lambda b, s: (b, s, 0)),
            pl.BlockSpec((1, N_HEADS * AUG, ts), lambda b, s: (b, 0, s)),
            pl.BlockSpec((1, ts, D_ATTN), lambda b, s: (b, s, 0)),
        ],
        out_shape=[
            jax.ShapeDtypeStruct((bsz, seq, D_POOL), BF16),
            jax.ShapeDtypeStruct((bsz, seq, N_HEADS * AUG), BF16),
            jax.ShapeDtypeStruct((bsz, N_HEADS * AUG, seq), BF16),
            jax.ShapeDtypeStruct((bsz, seq, D_ATTN), BF16),
        ],
        scratch_shapes=[
            pltpu.VMEM((ts + MAX_WINDOW, D_POOL), F32),
            pltpu.VMEM((1, LANES), F32),
            pltpu.VMEM((D_MODEL, D_POOL + 3 * D_ATTN), BF16),
            pltpu.VMEM((D_MODEL, LANES), BF16),
        ],
        compiler_params=pltpu.CompilerParams(
            dimension_semantics=("arbitrary", "arbitrary"), vmem_limit_bytes=VMEM_LIMIT),
    )(x, shift, scale, g, w_in, bfp, wpool, pscale, gpool, place, ones, fmask)


def _attn_kernel(q_ref, kt_ref, v_ref, wg_ref, wu_ref, wd_ref,
                 o_ref, wgb_ref, wub_ref, wdb_ref, xs_ref, zbuf, zsem, *, n_steps):
    step = pl.program_id(0) * pl.num_programs(1) + pl.program_id(1)
    block_pieces = zbuf.shape[0]
    n_blocks = xs_ref.shape[0] // block_pieces
    per_step = -(-n_blocks // n_steps)

    @pl.when(step == 0)
    def _():
        zbuf[...] = jnp.zeros_like(zbuf)

    def zero_fill(fn):
        for j in range(per_step):
            blk = step * per_step + j

            @pl.when(blk < n_blocks)
            def _():
                fn(pltpu.make_async_copy(
                    zbuf, xs_ref.at[pl.ds(blk * block_pieces, block_pieces)], zsem), j)

    zero_fill(lambda copy, j: copy.start(priority=j % 2))
    wgb_ref[...] = wg_ref[...].astype(BF16)
    wub_ref[...] = wu_ref[...].astype(BF16)
    wdb_ref[...] = wd_ref[...].astype(BF16)
    seq = q_ref.shape[1]
    qry_i = lax.broadcasted_iota(jnp.int32, (TQ, TK), 0)
    key_i = lax.broadcasted_iota(jnp.int32, (TQ, TK), 1)
    causal = key_i <= qry_i
    one_lane = lax.broadcasted_iota(jnp.int32, (seq, LANES), 1) == 0
    vaug = jnp.concatenate([v_ref[0], jnp.where(one_lane, 1.0, 0.0).astype(BF16)], axis=1)
    out_lane = lax.broadcasted_iota(jnp.int32, (TQ, LANES), 1)
    heads = range(HEAD_PAIR)
    for qi in range(seq // TQ):
        rows = slice(qi * TQ, (qi + 1) * TQ)
        q = [q_ref[0, rows, h * AUG:(h + 1) * AUG] for h in heads]
        m = [jnp.full((TQ, 1), -jnp.inf, F32) for _ in heads]
        acc = [jnp.zeros((TQ, 2 * LANES), F32) for _ in heads]
        for kj in range(qi + 1):
            keys = slice(kj * TK, (kj + 1) * TK)
            for h in heads:
                s = _dot(q[h], kt_ref[0, h * AUG:(h + 1) * AUG, keys])
                if kj == qi:
                    s = jnp.where(causal, s, -jnp.inf)
                m_new = jnp.maximum(m[h], jnp.max(s, axis=1, keepdims=True))
                p = jnp.exp2(s - m_new).astype(BF16)
                acc[h] = jnp.exp2(m[h] - m_new) * acc[h] + _dot(p, vaug[keys, :])
                m[h] = m_new
        o = [acc[h][:, :LANES] / acc[h][:, LANES:LANES + 1] for h in heads]
        o_ref[0, rows, :] = jnp.where(out_lane < HEAD_DIM, o[0], o[1])
    zero_fill(lambda copy, j: copy.wait())


def _attn(q, kt, v, wg, wu, wd, n_rows):
    bsz, seq, _ = q.shape
    pairs = N_HEADS // HEAD_PAIR
    per_step = N_EXPERTS // (bsz * pairs)
    assert per_step * bsz * pairs == N_EXPERTS
    wspec = lambda shape: pl.BlockSpec((per_step,) + shape, lambda b, h: (b * pairs + h, 0, 0))
    return pl.pallas_call(
        functools.partial(_attn_kernel, n_steps=bsz * pairs),
        name="attn",
        grid=(bsz, pairs),
        in_specs=[
            pl.BlockSpec((1, seq, HEAD_PAIR * AUG), lambda b, h: (b, 0, h)),
            pl.BlockSpec((1, HEAD_PAIR * AUG, seq), lambda b, h: (b, h, 0)),
            pl.BlockSpec((1, seq, LANES), lambda b, h: (b, 0, h)),
            wspec((D_MODEL, D_EXPERT)), wspec((D_MODEL, D_EXPERT)), wspec((D_EXPERT, D_MODEL)),
        ],
        out_specs=[
            pl.BlockSpec((1, seq, LANES), lambda b, h: (b, 0, h)),
            wspec((D_MODEL, D_EXPERT)), wspec((D_MODEL, D_EXPERT)), wspec((D_EXPERT, D_MODEL)),
            pl.BlockSpec(memory_space=pl.ANY),
        ],
        out_shape=[
            jax.ShapeDtypeStruct((bsz, seq, D_ATTN), F32),
            jax.ShapeDtypeStruct(wg.shape, BF16),
            jax.ShapeDtypeStruct(wu.shape, BF16),
            jax.ShapeDtypeStruct(wd.shape, BF16),
            jax.ShapeDtypeStruct((n_rows // PIECE, PIECE, D_MODEL), BF16),
        ],
        scratch_shapes=[
            pltpu.VMEM((RB // PIECE, PIECE, D_MODEL), BF16),
            pltpu.SemaphoreType.DMA(()),
        ],
        compiler_params=pltpu.CompilerParams(
            dimension_semantics=("arbitrary", "arbitrary"), vmem_limit_bytes=VMEM_LIMIT),
    )(q, kt, v, wg, wu, wd)


def _postmix_kernel(x_ref, pool_ref, o_ref, wout_ref, gattn_ref, gpost_ref, gate_ref,
                    gffn_ref, shift_ref, scale_ref, wr_ref, bias_ref,
                    x1_ref, h2_ref, infok_ref, cnt_ref):
    ya = _rms(o_ref[0], gattn_ref[...]).astype(BF16)
    mixed = _dot(pool_ref[0], wout_ref[:D_POOL, :]) + _dot(ya, wout_ref[D_POOL:, :])
    x1 = x_ref[0] + gate_ref[0] * _rms(mixed, gpost_ref[...])
    x1_ref[0] = x1
    h2 = _rms(x1, gffn_ref[...]) * (1.0 + scale_ref[0]) + shift_ref[0]
    h2b = h2.astype(BF16)
    h2_ref[0] = h2b
    h2l = (h2 - h2b.astype(F32)).astype(BF16)
    both = _dot(h2b, wr_ref[...])
    logits = both[:, :LANES] + both[:, LANES:] + _dot(h2l, wr_ref[:, :LANES])
    st = jax.nn.sigmoid(logits).T

    step = pl.program_id(0) * pl.num_programs(1) + pl.program_id(1)

    @pl.when(step == 0)
    def _():
        cnt_ref[...] = jnp.zeros_like(cnt_ref)

    lane = lax.broadcasted_iota(jnp.int32, cnt_ref.shape, 1)
    tiles = x_ref.shape[1] // TW
    for i in range(tiles):
        info, n = _route_tile(st[:, i * TW:(i + 1) * TW], bias_ref)
        infok_ref[:, i * TW:(i + 1) * TW] = info
        cnt_ref[...] = jnp.where(lane == step * tiles + i, jnp.broadcast_to(n, cnt_ref.shape),
                                 cnt_ref[...])


def _postmix(x, pool, ot, wout, gattn, gpost, gate, gffn, shift, scale, wr, bias_col):
    bsz, seq, _ = x.shape
    ts = TS_POST
    ns = seq // ts
    full = lambda shape: pl.BlockSpec(shape, lambda b, s: (0,) * len(shape))
    per_batch = pl.BlockSpec((1, 1, D_MODEL), lambda b, s: (b, 0, 0))
    tok = lambda width: pl.BlockSpec((1, ts, width), lambda b, s: (b, s, 0))
    return pl.pallas_call(
        _postmix_kernel,
        name="postmix",
        grid=(bsz, ns),
        in_specs=[
            tok(D_MODEL), tok(D_POOL), tok(D_ATTN),
            full((D_MODEL, D_MODEL)),
            full((1, D_ATTN)), full((1, D_MODEL)), per_batch,
            full((1, D_MODEL)), per_batch, per_batch,
            full((D_MODEL, 2 * LANES)),
            full((N_EXPERTS, 1)),
        ],
        out_specs=[
            tok(D_MODEL), tok(D_MODEL),
            pl.BlockSpec((2 * TOP_K, ts), lambda b, s: (0, b * ns + s)),
            full((N_EXPERTS, LANES)),
        ],
        out_shape=[
            jax.ShapeDtypeStruct((bsz, seq, D_MODEL), F32),
            jax.ShapeDtypeStruct((bsz, seq, D_MODEL), BF16),
            jax.ShapeDtypeStruct((2 * TOP_K, bsz * seq), F32),
            jax.ShapeDtypeStruct((N_EXPERTS, LANES), F32),
        ],
        compiler_params=pltpu.CompilerParams(
            dimension_semantics=("arbitrary", "arbitrary"), vmem_limit_bytes=VMEM_LIMIT),
    )(x, pool, ot, wout, gattn, gpost, gate, gffn, shift, scale, wr, bias_col)


def _route_tile(st, bias_ref):
    tr = st.shape[1]
    scores = [st[g * GROUP_SIZE:(g + 1) * GROUP_SIZE, :] for g in range(N_EXPERT_GROUPS)]
    sel = [scores[g] + bias_ref[g * GROUP_SIZE:(g + 1) * GROUP_SIZE, :]
           for g in range(N_EXPERT_GROUPS)]
    sub = lax.broadcasted_iota(jnp.int32, (GROUP_SIZE, tr), 0)
    neg = jnp.float32(-jnp.inf)

    rows = []
    for g in range(N_EXPERT_GROUPS):
        v = sel[g]
        m1 = jnp.max(v, axis=0, keepdims=True)
        first = jnp.min(jnp.where(v == m1, sub, GROUP_SIZE), axis=0, keepdims=True)
        m2 = jnp.max(jnp.where(sub == first, neg, v), axis=0, keepdims=True)
        rows.append(m1 + m2)
    gs = jnp.concatenate(rows, axis=0)

    shape = (GROUP_SIZE, tr)
    one = jnp.ones(shape, jnp.int32)
    zero = jnp.zeros(shape, jnp.int32)
    beaten = zero
    for gp in range(N_EXPERT_GROUPS):
        r = jnp.broadcast_to(gs[gp:gp + 1, :], shape)
        tie = jnp.where(sub > gp, one, zero)
        beaten = beaten + jnp.where(r > gs, one, zero) + jnp.where(r == gs, tie, zero)
    keep = jnp.where(beaten < TOPK_GROUPS, 1.0, 0.0)

    masked = [jnp.where(jnp.broadcast_to(keep[g:g + 1, :], shape) > 0.5, sel[g], neg)
              for g in range(N_EXPERT_GROUPS)]

    eidx = [sub + g * GROUP_SIZE for g in range(N_EXPERT_GROUPS)]
    picked = [zero for _ in range(N_EXPERT_GROUPS)]
    for _ in range(TOP_K):
        best = masked[0]
        for g in range(1, N_EXPERT_GROUPS):
            best = jnp.maximum(best, masked[g])
        best = jnp.broadcast_to(jnp.max(best, axis=0, keepdims=True), shape)
        first = jnp.where(masked[0] == best, eidx[0], N_EXPERTS)
        for g in range(1, N_EXPERT_GROUPS):
            first = jnp.minimum(first, jnp.where(masked[g] == best, eidx[g], N_EXPERTS))
        first = jnp.broadcast_to(jnp.min(first, axis=0, keepdims=True), shape)
        for g in range(N_EXPERT_GROUPS):
            hit = eidx[g] == first
            picked[g] = jnp.where(hit, one, picked[g])
            masked[g] = jnp.where(hit, neg, masked[g])
    chosen = [picked[g] > 0 for g in range(N_EXPERT_GROUPS)]

    w = [jnp.where(chosen[g], scores[g], 0.0) for g in range(N_EXPERT_GROUPS)]
    denom = w[0].sum(axis=0, keepdims=True)
    for g in range(1, N_EXPERT_GROUPS):
        denom = denom + w[g].sum(axis=0, keepdims=True)
    wfull = jnp.concatenate([w[g] / denom * ROUTED_SCALE for g in range(N_EXPERT_GROUPS)], axis=0)
    mfull = jnp.concatenate([jnp.where(chosen[g], 1.0, 0.0) for g in range(N_EXPERT_GROUPS)],
                            axis=0)
    mb = mfull.astype(BF16)

    e_r = lax.broadcasted_iota(jnp.int32, (N_EXPERTS, N_EXPERTS), 0)
    e_c = lax.broadcasted_iota(jnp.int32, (N_EXPERTS, N_EXPERTS), 1)
    before_e = (e_c < e_r).astype(BF16)
    t_r = lax.broadcasted_iota(jnp.int32, (tr, tr), 0)
    t_c = lax.broadcasted_iota(jnp.int32, (tr, tr), 1)
    before_t = (t_r < t_c).astype(BF16)
    ordinal = _dot(before_e, mb)
    rank = _dot(mb, before_t)
    n = jnp.sum(mfull, axis=1, keepdims=True)
    pieces = jnp.floor((n + (PIECE - 1.0)) * (1.0 / PIECE))
    run_start = PIECE * _dot(before_e, jnp.broadcast_to(pieces, (N_EXPERTS, LANES)).astype(BF16))
    pos = run_start[:, 0:1] + rank

    sub8 = lax.broadcasted_iota(jnp.int32, (TOP_K, tr), 0)
    pos8 = jnp.zeros((TOP_K, tr), F32)
    w8 = jnp.zeros((TOP_K, tr), F32)
    for k in range(TOP_K):
        selk = jnp.where(ordinal == float(k), mfull, 0.0)
        pk = jnp.sum(selk * pos, axis=0, keepdims=True)
        wk = jnp.sum(selk * wfull, axis=0, keepdims=True)
        pos8 = jnp.where(sub8 == k, jnp.broadcast_to(pk, (TOP_K, tr)), pos8)
        w8 = jnp.where(sub8 == k, jnp.broadcast_to(wk, (TOP_K, tr)), w8)
    return jnp.concatenate([pos8, w8], axis=0), n


def _swiglu(xb, wg, wu):
    g = _dot(xb, wg)
    return (g * jax.nn.sigmoid(g)) * _dot(xb, wu)


def _piece_copy(src_ref, src_piece, dst_ref, dst_piece, sem, n_pieces=1):
    src = src_ref.at[pl.ds(src_piece, n_pieces)]
    dst = dst_ref.at[pl.ds(dst_piece, n_pieces)]
    return pltpu.make_async_copy(src, dst, sem)


def _as_pieces(rows):
    return rows.reshape(rows.shape[0] // PIECE, PIECE, rows.shape[1])


LOCAL_BITS = 8
COPY_SIZES = (4, 3, 2, 1)
MAX_COPIES = N_EXPERTS
assert LROWS // (COPY_SIZES[0] * PIECE) <= MAX_COPIES


def _for_each_copy(copies_ref, counts_ref, tile, fn):
    for c, size in enumerate(COPY_SIZES):
        base = (tile * len(COPY_SIZES) + c) * MAX_COPIES

        def body(i, carry, base=base, size=size):
            word = copies_ref[base + i]
            fn(word & ((1 << LOCAL_BITS) - 1), lax.shift_right_logical(word, LOCAL_BITS), size)
            return carry

        lax.fori_loop(0, counts_ref[tile * (len(COPY_SIZES) + 1) + c], body, 0)


def _tile_pieces(counts_ref, tile):
    return counts_ref[tile * (len(COPY_SIZES) + 1) + len(COPY_SIZES)]


def _wait_pieces(total, copy_of):
    for bit in range((LROWS // PIECE).bit_length()):
        @pl.when((lax.shift_right_logical(total, bit) & 1) == 1)
        def _():
            copy_of(1 << bit).wait()


def _onehot_chunks(infok_ref, weighted):
    pos = infok_ref[0:TOP_K, :]
    grp = jnp.floor(pos * (1.0 / ONEHOT_GROUP))
    off = pos - ONEHOT_GROUP * grp
    sub = lax.broadcasted_iota(jnp.int32, (ONEHOT_GROUP, TW), 0).astype(F32)
    offs, grps = [], []
    for k in range(TOP_K):
        value = infok_ref[TOP_K + k:TOP_K + k + 1, :] if weighted else 1.0
        offs.append(jnp.where(sub == off[k:k + 1, :], value, 0.0).astype(BF16))
        grps.append(jnp.broadcast_to(grp[k:k + 1, :], (PIECE, TW)).astype(BF16))
    zero = jnp.zeros((PIECE, TW), BF16)

    def chunk(j):
        tiles = []
        for a in range(j * LCHUNK // ONEHOT_GROUP, (j + 1) * LCHUNK // ONEHOT_GROUP):
            hit = [grps[k] == a for k in range(TOP_K)]
            for i in range(ONEHOT_GROUP // PIECE):
                rows = zero
                for k in range(TOP_K):
                    rows = rows + jnp.where(hit[k], offs[k][i * PIECE:(i + 1) * PIECE, :], zero)
                tiles.append(rows)
        return jnp.concatenate(tiles, axis=0)

    return chunk


def _dispatch_kernel(copies_ref, counts_ref, h2_ref, infok_ref, zeros_ref, xs_ref, lbuf, sem):
    del zeros_ref
    tile = pl.program_id(0)
    last = pl.num_programs(0) - 1
    slot = tile % 2

    def start(t, s):
        _for_each_copy(
            copies_ref, counts_ref, t,
            lambda lp, gp, n: _piece_copy(lbuf.at[s], lp, xs_ref, gp, sem.at[s], n).start())

    def wait(t, s):
        _wait_pieces(_tile_pieces(counts_ref, t),
                     lambda n: _piece_copy(lbuf.at[s], 0, xs_ref, 0, sem.at[s], n))

    @pl.when(tile >= 2)
    def _():
        wait(tile - 2, slot)

    onehot = _onehot_chunks(infok_ref, weighted=False)
    xb = h2_ref[...]
    used_rows = PIECE * _tile_pieces(counts_ref, tile)

    def sort_chunk(j):
        lbuf[slot, j * LCHUNK // PIECE:(j + 1) * LCHUNK // PIECE] = _as_pieces(
            _dot(onehot(j), xb).astype(BF16))

    n_chunks = LROWS // LCHUNK
    for j in range(n_chunks - 1):
        sort_chunk(j)
    pl.when(used_rows > (n_chunks - 1) * LCHUNK)(functools.partial(sort_chunk, n_chunks - 1))

    start(tile, slot)

    @pl.when(tile == last)
    def _():
        @pl.when(tile >= 1)
        def _():
            wait(tile - 1, 1 - slot)
        wait(tile, slot)


def _dispatch(tables, h2, infok, zeros):
    t = h2.shape[0]
    return pl.pallas_call(
        _dispatch_kernel,
        name="dispatch",
        grid_spec=pltpu.PrefetchScalarGridSpec(
            num_scalar_prefetch=2,
            grid=(t // TW,),
            in_specs=[
                pl.BlockSpec((TW, D_MODEL), lambda i, *_: (i, 0)),
                pl.BlockSpec((2 * TOP_K, TW), lambda i, *_: (0, i)),
                pl.BlockSpec(memory_space=pl.ANY),
            ],
            out_specs=pl.BlockSpec(memory_space=pl.ANY),
            scratch_shapes=[
                pltpu.VMEM((2, LROWS // PIECE, PIECE, D_MODEL), BF16),
                pltpu.SemaphoreType.DMA((2,)),
            ],
        ),
        out_shape=jax.ShapeDtypeStruct(zeros.shape, BF16),
        input_output_aliases={4: 0},
        compiler_params=pltpu.CompilerParams(
            dimension_semantics=("arbitrary",), vmem_limit_bytes=VMEM_LIMIT),
    )(*tables, h2, infok, zeros)


def _experts_kernel(be_ref, nv_ref, xs_ref, wg_ref, wu_ref, wd_ref, ys_ref):
    @pl.when(pl.program_id(0) < nv_ref[0])
    def _():
        for j in range(RB // RSUB):
            rows = slice(j * RSUB, (j + 1) * RSUB)
            h = _swiglu(xs_ref[rows, :], wg_ref[0], wu_ref[0])
            ys_ref[rows, :] = _dot(h.astype(BF16), wd_ref[0]).astype(BF16)


def _experts(block_expert, n_valid, xs, wg, wu, wd):
    n_blocks = xs.shape[0] // RB
    rows = pl.BlockSpec((RB, D_MODEL), lambda i, be, nv: (jnp.minimum(i, nv[0] - 1), 0))
    return pl.pallas_call(
        _experts_kernel,
        name="experts",
        grid_spec=pltpu.PrefetchScalarGridSpec(
            num_scalar_prefetch=2,
            grid=(n_blocks,),
            in_specs=[
                rows,
                pl.BlockSpec((1, D_MODEL, D_EXPERT), lambda i, be, nv: (be[i], 0, 0)),
                pl.BlockSpec((1, D_MODEL, D_EXPERT), lambda i, be, nv: (be[i], 0, 0)),
                pl.BlockSpec((1, D_EXPERT, D_MODEL), lambda i, be, nv: (be[i], 0, 0)),
            ],
            out_specs=rows,
        ),
        out_shape=jax.ShapeDtypeStruct(xs.shape, BF16),
        input_output_aliases={2: 0},
        compiler_params=pltpu.CompilerParams(
            dimension_semantics=("arbitrary",), vmem_limit_bytes=VMEM_LIMIT),
    )(block_expert, n_valid, xs, wg, wu, wd)


def _combine_kernel(copies_ref, counts_ref,
                    infok_ref, h2_ref, x1_ref, gate_ref, gpost_ref, wsg_ref, wsu_ref, wsd_ref,
                    ys_ref, o_ref, ybuf, acc_ref, sem):
    tile = pl.program_id(0)
    last = pl.num_programs(0) - 1
    slot = tile % 2

    def start(t, s):
        _for_each_copy(
            copies_ref, counts_ref, t,
            lambda lp, gp, n: _piece_copy(ys_ref, gp, ybuf.at[s], lp, sem.at[s], n).start())

    def wait(t, s):
        _wait_pieces(_tile_pieces(counts_ref, t),
                     lambda n: _piece_copy(ys_ref, 0, ybuf.at[s], 0, sem.at[s], n))

    @pl.when(tile == 0)
    def _():
        ybuf[...] = jnp.zeros_like(ybuf)
        start(tile, slot)

    @pl.when(tile < last)
    def _():
        start(tile + 1, 1 - slot)

    wait(tile, slot)

    weights = _onehot_chunks(infok_ref, weighted=True)
    hs = _swiglu(h2_ref[...], wsg_ref[...], wsu_ref[...])
    ff = _dot(hs.astype(BF16), wsd_ref[...])
    used_rows = PIECE * _tile_pieces(counts_ref, tile)

    def sum_chunk(j):
        rows = ybuf[slot, j * LCHUNK // PIECE:(j + 1) * LCHUNK // PIECE].reshape(LCHUNK, D_MODEL)
        return lax.dot_general(weights(j), rows,
                               (((0,), (0,)), ((), ())), preferred_element_type=F32)

    def add_chunk(j):
        acc_ref[...] += sum_chunk(j)

    n_chunks = LROWS // LCHUNK
    for j in range(n_chunks - 1):
        ff = ff + sum_chunk(j)
    acc_ref[...] = ff
    pl.when(used_rows > (n_chunks - 1) * LCHUNK)(functools.partial(add_chunk, n_chunks - 1))
    o_ref[...] = x1_ref[...] + gate_ref[0] * _rms(acc_ref[...], gpost_ref[...])


def _combine(tables, infok, h2, x1, gate, gpost, wsg, wsu, wsd, ys, seq):
    t = h2.shape[0]
    per_seq = seq // TW
    full = lambda shape: pl.BlockSpec(shape, lambda i, *_: (0,) * len(shape))
    tok = lambda width: pl.BlockSpec((TW, width), lambda i, *_: (i, 0))
    return pl.pallas_call(
        _combine_kernel,
        name="combine",
        grid_spec=pltpu.PrefetchScalarGridSpec(
            num_scalar_prefetch=2,
            grid=(t // TW,),
            in_specs=[
                pl.BlockSpec((2 * TOP_K, TW), lambda i, *_: (0, i)), tok(D_MODEL), tok(D_MODEL),
                pl.BlockSpec((1, 1, D_MODEL), lambda i, *_: (i // per_seq, 0, 0)),
                full((1, D_MODEL)),
                full((D_MODEL, D_SHARED)), full((D_MODEL, D_SHARED)), full((D_SHARED, D_MODEL)),
                pl.BlockSpec(memory_space=pl.ANY),
            ],
            out_specs=tok(D_MODEL),
            scratch_shapes=[
                pltpu.VMEM((2, LROWS // PIECE, PIECE, D_MODEL), BF16),
                pltpu.VMEM((TW, D_MODEL), F32),
                pltpu.SemaphoreType.DMA((2,)),
            ],
        ),
        out_shape=jax.ShapeDtypeStruct((t, D_MODEL), F32),
        compiler_params=pltpu.CompilerParams(
            dimension_semantics=("arbitrary",), vmem_limit_bytes=VMEM_LIMIT),
    )(*tables, infok, h2, x1, gate, gpost, wsg, wsu, wsd, ys)


def _row_buffer_blocks(n_tiles):
    return (TOP_K * TW * n_tiles + N_EXPERTS * n_tiles * (PIECE - 1)
            + N_EXPERTS * (RB - PIECE) + RB - 1) // RB


def _dispatch_plan(cnt, n_tiles):
    n = cnt[:, :n_tiles].astype(jnp.int32)
    pieces = (n + PIECE - 1) // PIECE
    local = jnp.cumsum(pieces, axis=0) - pieces
    seg = jnp.sum(pieces, axis=1)
    per_block = RB // PIECE
    seg_pad = (seg + per_block - 1) // per_block * per_block
    seg_end = jnp.cumsum(seg_pad)
    seg_start = seg_end - seg_pad
    glob = seg_start[:, None] + jnp.cumsum(pieces, axis=1) - pieces
    n_blocks = _row_buffer_blocks(n_tiles)
    n_valid = seg_end[-1] // per_block
    blk = jnp.minimum(jnp.arange(n_blocks, dtype=jnp.int32), n_valid - 1)
    block_end = seg_end // per_block
    block_expert = jnp.sum((block_end[None, :] <= blk[:, None]).astype(jnp.int32), axis=1)
    block_expert = jnp.minimum(block_expert, N_EXPERTS - 1)
    assert LROWS // PIECE <= 1 << LOCAL_BITS and n_blocks * per_block < 1 << (31 - LOCAL_BITS)

    def copy_list(count, width):
        end = jnp.cumsum(count, axis=0).T
        slot = jnp.arange(width, dtype=jnp.int32)
        expert = jnp.sum((end[:, None, :] <= slot[None, :, None]).astype(jnp.int32), axis=-1)
        pick = expert[:, :, None] == jnp.arange(N_EXPERTS, dtype=jnp.int32)
        take = lambda a: jnp.sum(jnp.where(pick, a.T[:, None, :], 0), axis=-1)
        return slot[None, :] - take(jnp.cumsum(count, axis=0) - count), take

    big = COPY_SIZES[0]
    assert COPY_SIZES == tuple(range(big, 0, -1))
    n_big = pieces // big
    rest = pieces % big
    packed = local | (glob << LOCAL_BITS)
    step = 1 + (1 << LOCAL_BITS)
    within, take = copy_list(n_big, MAX_COPIES)
    lists = [take(packed) + big * step * within]
    counts = [jnp.sum(n_big, axis=0)]
    rest_start = packed + big * step * n_big
    for size in COPY_SIZES[1:]:
        has = (rest == size).astype(jnp.int32)
        _, take = copy_list(has, MAX_COPIES)
        lists.append(take(rest_start))
        counts.append(jnp.sum(has, axis=0))
    counts.append(jnp.sum(pieces, axis=0))
    tables = (jnp.stack(lists, axis=1).reshape(-1).astype(jnp.int32),
              jnp.stack(counts, axis=1).reshape(-1).astype(jnp.int32))
    return tables, block_expert, n_valid.reshape(1).astype(jnp.int32)


def kernel(x, c, w_ada, b_ada, g_pre_mix, g_post_mix, g_pre_ffn, g_post_ffn, w_in, b_forget,
           w_pool, pool_scale, g_pool_out, g_attn_out, w_out, w_router, router_bias,
           w_gate, w_up, w_down, ws_gate, ws_up, ws_down):
    bsz, seq, d = x.shape
    depth = w_ada.shape[0]
    for l in range(depth):
        mod = _ada(c, w_ada, b_ada[l][None, :], l)
        shift_m, scale_m, gate_m, shift_f, scale_f, gate_f = [
            m.reshape(bsz, 1, d) for m in jnp.split(mod, 6, axis=-1)]

        bfp = jnp.pad(b_forget[l], (0, LANES - N_HEADS))[None, :]
        pool, qa, kt, v = _premix(
            x, shift_m, scale_m, g_pre_mix[l][None, :], w_in, l, bfp, w_pool[l].astype(BF16),
            pool_scale[l][None, :], g_pool_out[l][None, :])
        t = bsz * seq
        assert t // TW <= LANES
        n_rows = _row_buffer_blocks(t // TW) * RB
        ot, wgb, wub, wdb, zeros = _attn(qa, kt, v, w_gate[l], w_up[l], w_down[l], n_rows)

        wr = jnp.pad(w_router[l], ((0, 0), (0, LANES - N_EXPERTS)))
        wrh = wr.astype(BF16)
        wrl = (wr - wrh.astype(F32)).astype(BF16)
        x1, h2, infok, cnt = _postmix(
            x, pool, ot, w_out[l].astype(BF16), g_attn_out[l][None, :], g_post_mix[l][None, :],
            gate_m, g_pre_ffn[l][None, :], shift_f, scale_f,
            jnp.concatenate([wrh, wrl], axis=1), router_bias[l][:, None])
        tables, block_expert, n_valid = _dispatch_plan(cnt, t // TW)
        h2f = h2.reshape(t, d)
        xs = _dispatch(tables, h2f, infok, zeros)
        ys = _experts(block_expert, n_valid, xs.reshape(n_rows, d), wgb, wub, wdb)
        ys = ys.reshape(xs.shape)
        out = _combine(tables, infok, h2f, x1.reshape(t, d), gate_f, g_post_ffn[l][None, :],
                       ws_gate[l].astype(BF16), ws_up[l].astype(BF16), ws_down[l].astype(BF16),
                       ys, seq)
        x = out.reshape(bsz, seq, d)
    return x
```

```python
import functools

import numpy as np
import jax
import jax.numpy as jnp
from jax import lax
from jax.experimental import pallas as pl
from jax.experimental.pallas import tpu as pltpu

D_MODEL = 1024
D_POOL = 512
POOL_WINDOWS = (2, 4, 8, 16)
POOL_GROUP = 128
MAX_WINDOW = max(POOL_WINDOWS)
D_ATTN = 512
HEAD_DIM = 64
N_HEADS = 8
N_EXPERTS = 64
N_EXPERT_GROUPS = 8
GROUP_SIZE = N_EXPERTS // N_EXPERT_GROUPS
TOPK_GROUPS = 4
TOP_K = 8
D_EXPERT = 256
D_SHARED = 256
ROUTED_SCALE = 2.5
EPS = 1e-6

LOG2E = 1.4426950408889634
LANES = 128
N_SPLIT = 3
AUG = LANES

TS_PRE = 1024
CUM_BLOCK = 256
TS_POST = 1024
TQ = 512
TK = 512
HEAD_PAIR = LANES // HEAD_DIM
assert TQ == TK
TW = 256
PIECE = 16
RB = 1024
RSUB = 512
LCHUNK = 512
ONEHOT_GROUP = 64
LROWS = -(-(TOP_K * TW + N_EXPERTS * (PIECE - 1)) // LCHUNK) * LCHUNK

F32 = jnp.float32
BF16 = jnp.bfloat16
V7X_VMEM_BYTES = 64 * 1024 * 1024
VMEM_LIMIT = V7X_VMEM_BYTES * 7 // 8


def _rms(v, g):
    return v * lax.rsqrt(jnp.mean(v * v, axis=-1, keepdims=True) + EPS) * g


def _split3(v):
    hi = v.astype(BF16)
    r1 = v - hi.astype(F32)
    mid = r1.astype(BF16)
    r2 = r1 - mid.astype(F32)
    lo = r2.astype(BF16)
    return hi, mid, lo


def _dot(a, b):
    return jnp.dot(a, b, preferred_element_type=F32)


def _ada_kernel(c_ref, w_ref, b_ref, o_ref):
    o_ref[...] = _dot(c_ref[...].astype(BF16), w_ref[0].astype(BF16)) + b_ref[...]


def _ada(c, w, b, layer):
    bsz = c.shape[0]
    n = w.shape[2]
    return pl.pallas_call(
        _ada_kernel,
        name="ada",
        grid=(n // D_MODEL,),
        in_specs=[
            pl.BlockSpec((bsz, D_MODEL), lambda j: (0, 0)),
            pl.BlockSpec((1, D_MODEL, D_MODEL), lambda j: (layer, 0, j)),
            pl.BlockSpec((1, D_MODEL), lambda j: (0, j)),
        ],
        out_specs=pl.BlockSpec((bsz, D_MODEL), lambda j: (0, j)),
        out_shape=jax.ShapeDtypeStruct((bsz, n), F32),
    )(c, w, b)


def _premix_kernel(x_ref, shift_ref, scale_ref, g_ref, win_ref, bf_ref, wpool_ref,
                   pscale_ref, gpool_ref, place_ref, ones_ref, fmask_ref,
                   pool_ref, q_ref, kt_ref, v_ref,
                   uext_ref, cum_ref, w1_ref, wf_ref):
    s = pl.program_id(1)
    ts = x_ref.shape[1]

    @pl.when((pl.program_id(0) == 0) & (s == 0))
    def _():
        q0, k0 = D_POOL, D_POOL + D_ATTN
        w1_ref[:, :q0] = win_ref[0, :, :q0].astype(BF16)
        w1_ref[:, q0:k0] = (win_ref[0, :, q0:k0] * (HEAD_DIM ** -0.5 * LOG2E)).astype(BF16)
        w1_ref[:, k0:] = win_ref[0, :, k0:k0 + 2 * D_ATTN].astype(BF16)
        wf_ref[...] = jnp.zeros_like(wf_ref)
        wf_ref[:, :N_HEADS] = win_ref[0, :, k0 + 2 * D_ATTN:].astype(BF16)

    @pl.when(s == 0)
    def _():
        uext_ref[0:MAX_WINDOW, :] = jnp.zeros((MAX_WINDOW, D_POOL), F32)
        cum_ref[...] = jnp.zeros_like(cum_ref)

    x = x_ref[0]
    h = _rms(x, g_ref[...]) * (1.0 + scale_ref[0]) + shift_ref[0]
    hb = h.astype(BF16)
    proj = _dot(hb, w1_ref[...])
    u = proj[:, :D_POOL]
    q = proj[:, D_POOL:D_POOL + D_ATTN]
    k = proj[:, D_POOL + D_ATTN:D_POOL + 2 * D_ATTN]
    v = proj[:, D_POOL + 2 * D_ATTN:]

    uext_ref[MAX_WINDOW:, :] = u
    pos = (s * ts + lax.broadcasted_iota(jnp.int32, (ts, 1), 0) + 1).astype(F32)
    ys = []
    for g, w in enumerate(POOL_WINDOWS):
        c0 = g * POOL_GROUP
        acc = uext_ref[MAX_WINDOW:, c0:c0 + POOL_GROUP]
        for j in range(1, w):
            acc = acc + uext_ref[MAX_WINDOW - j:MAX_WINDOW - j + ts, c0:c0 + POOL_GROUP]
        pooled = acc / jnp.minimum(pos, float(w)) - u[:, c0:c0 + POOL_GROUP]
        ys.append(_dot(pooled.astype(BF16), wpool_ref[g]))
    ypool = jnp.concatenate(ys, axis=1) * pscale_ref[...]
    pool_ref[0] = _rms(ypool, gpool_ref[...]).astype(BF16)
    uext_ref[0:MAX_WINDOW, :] = uext_ref[ts:ts + MAX_WINDOW, :]

    z0 = _dot(hb, wf_ref[...]) + bf_ref[...]
    z = z0
    for p in range(1, N_SPLIT):
        z = z + pltpu.roll(z0, p * N_HEADS, 1)
    logf = jnp.minimum(z, 0.0) - jnp.log1p(jnp.exp(-jnp.abs(z)))
    row = lax.broadcasted_iota(jnp.int32, (CUM_BLOCK, CUM_BLOCK), 0)
    col = lax.broadcasted_iota(jnp.int32, (CUM_BLOCK, CUM_BLOCK), 1)
    tri = (col <= row).astype(BF16)
    pieces = _split3(logf)
    carry = cum_ref[...]
    blocks = []
    for r0 in range(0, ts, CUM_BLOCK):
        blk = carry
        for piece in pieces:
            blk = blk + _dot(tri, piece[r0:r0 + CUM_BLOCK, :])
        carry = blk[CUM_BLOCK - 1:CUM_BLOCK, :]
        blocks.append(blk)
    cum = jnp.concatenate(blocks, axis=0)
    cum_ref[...] = carry

    hi, mid, lo = [piece.astype(F32) for piece in _split3(cum * LOG2E)]
    lane = lax.broadcasted_iota(jnp.int32, (ts, LANES), 1)
    pieces = jnp.where(lane < N_HEADS, hi, jnp.where(lane < 2 * N_HEADS, mid, lo))
    aug = _dot(pieces.astype(BF16), place_ref[...]) + ones_ref[...]
    aug_q = aug[:, :N_HEADS * AUG]
    aug_k = aug[:, N_HEADS * AUG:]

    def expand(a):
        blocks = []
        for j in range(D_ATTN // LANES):
            blk = a[:, j * LANES:(j + 1) * LANES]
            blocks += [blk, blk]
        return jnp.concatenate(blocks, axis=1)

    fmask = fmask_ref[...]
    qa = expand(q) * fmask + aug_q
    ka = expand(k) * fmask + aug_k
    q_ref[0] = qa.astype(BF16)
    kt_ref[0] = ka.T.astype(BF16)
    v_ref[0] = v.astype(BF16)


def _aug_constants():
    width = N_HEADS * AUG
    place = np.zeros((LANES, 2 * width), np.float32)
    ones = np.zeros((1, 2 * width), np.float32)
    fmask = np.zeros((1, width), np.float32)
    for h in range(N_HEADS):
        feat0 = h * AUG + (0 if h % 2 == 0 else HEAD_DIM)
        aug0 = h * AUG + (HEAD_DIM if h % 2 == 0 else 0)
        fmask[0, feat0:feat0 + HEAD_DIM] = 1.0
        for p in range(N_SPLIT):
            place[p * N_HEADS + h, aug0 + p] = 1.0
            ones[0, width + aug0 + p] = 1.0
            place[p * N_HEADS + h, width + aug0 + N_SPLIT + p] = -1.0
            ones[0, aug0 + N_SPLIT + p] = 1.0
    return jnp.asarray(place, BF16), jnp.asarray(ones), jnp.asarray(fmask)


def _premix(x, shift, scale, g, w_in, layer, bfp, wpool, pscale, gpool):
    bsz, seq, _ = x.shape
    d_in = w_in.shape[2]
    assert d_in == D_POOL + 3 * D_ATTN + N_HEADS
    ts = TS_PRE
    place, ones, fmask = _aug_constants()
    full = lambda shape: pl.BlockSpec(shape, lambda b, s: (0,) * len(shape))
    per_batch = pl.BlockSpec((1, 1, D_MODEL), lambda b, s: (b, 0, 0))
    return pl.pallas_call(
        _premix_kernel,
        name="premix",
        grid=(bsz, seq // ts),
        in_specs=[
            pl.BlockSpec((1, ts, D_MODEL), lambda b, s: (b, s, 0)),
            per_batch, per_batch,
            full((1, D_MODEL)),
            pl.BlockSpec((1, D_MODEL, d_in), lambda b, s: (layer, 0, 0),
                         pipeline_mode=pl.Buffered(1)),
            full((1, LANES)),
            full((len(POOL_WINDOWS), POOL_GROUP, POOL_GROUP)),
            full((1, D_POOL)),
            full((1, D_POOL)),
            full((LANES, 2 * N_HEADS * AUG)),
            full((1, 2 * N_HEADS * AUG)),
            full((1, N_HEADS * AUG)),
        ],
        out_specs=[
            pl.BlockSpec((1, ts, D_POOL), lambda b, s: (b, s, 0)),
            pl.BlockSpec((1, ts, N_HEADS * AUG), lambda b, s: (b, s, 0)),
            pl.BlockSpec((1, N_HEADS * AUG, ts), lambda b, s: (b, 0, s)),
            pl.BlockSpec((1, ts, D_ATTN), lambda b, s: (b, s, 0)),
        ],
        out_shape=[
            jax.ShapeDtypeStruct((bsz, seq, D_POOL), BF16),
            jax.ShapeDtypeStruct((bsz, seq, N_HEADS * AUG), BF16),
            jax.ShapeDtypeStruct((bsz, N_HEADS * AUG, seq), BF16),
            jax.ShapeDtypeStruct((bsz, seq, D_ATTN), BF16),
        ],
        scratch_shapes=[
            pltpu.VMEM((ts + MAX_WINDOW, D_POOL), F32),
            pltpu.VMEM((1, LANES), F32),
            pltpu.VMEM((D_MODEL, D_POOL + 3 * D_ATTN), BF16),
            pltpu.VMEM((D_MODEL, LANES), BF16),
        ],
        compiler_params=pltpu.CompilerParams(
            dimension_semantics=("arbitrary", "arbitrary"), vmem_limit_bytes=VMEM_LIMIT),
    )(x, shift, scale, g, w_in, bfp, wpool, pscale, gpool, place, ones, fmask)


def _attn_kernel(q_ref, kt_ref, v_ref, wg_ref, wu_ref, wd_ref, o_ref, wgb_ref, wub_ref, wdb_ref):
    wgb_ref[...] = wg_ref[...].astype(BF16)
    wub_ref[...] = wu_ref[...].astype(BF16)
    wdb_ref[...] = wd_ref[...].astype(BF16)
    seq = q_ref.shape[1]
    qry_i = lax.broadcasted_iota(jnp.int32, (TQ, TK), 0)
    key_i = lax.broadcasted_iota(jnp.int32, (TQ, TK), 1)
    causal = key_i <= qry_i
    one_lane = lax.broadcasted_iota(jnp.int32, (seq, LANES), 1) == 0
    vaug = jnp.concatenate([v_ref[0], jnp.where(one_lane, 1.0, 0.0).astype(BF16)], axis=1)
    out_lane = lax.broadcasted_iota(jnp.int32, (TQ, LANES), 1)
    heads = range(HEAD_PAIR)
    for qi in range(seq // TQ):
        rows = slice(qi * TQ, (qi + 1) * TQ)
        q = [q_ref[0, rows, h * AUG:(h + 1) * AUG] for h in heads]
        m = [jnp.full((TQ, 1), -jnp.inf, F32) for _ in heads]
        acc = [jnp.zeros((TQ, 2 * LANES), F32) for _ in heads]
        for kj in range(qi + 1):
            keys = slice(kj * TK, (kj + 1) * TK)
            for h in heads:
                s = _dot(q[h], kt_ref[0, h * AUG:(h + 1) * AUG, keys])
                if kj == qi:
                    s = jnp.where(causal, s, -jnp.inf)
                m_new = jnp.maximum(m[h], jnp.max(s, axis=1, keepdims=True))
                p = jnp.exp2(s - m_new).astype(BF16)
                acc[h] = jnp.exp2(m[h] - m_new) * acc[h] + _dot(p, vaug[keys, :])
                m[h] = m_new
        o = [acc[h][:, :LANES] / acc[h][:, LANES:LANES + 1] for h in heads]
        o_ref[0, rows, :] = jnp.where(out_lane < HEAD_DIM, o[0], o[1])


def _attn(q, kt, v, wg, wu, wd):
    bsz, seq, _ = q.shape
    pairs = N_HEADS // HEAD_PAIR
    per_step = N_EXPERTS // (bsz * pairs)
    assert per_step * bsz * pairs == N_EXPERTS
    wspec = lambda shape: pl.BlockSpec((per_step,) + shape, lambda b, h: (b * pairs + h, 0, 0))
    return pl.pallas_call(
        _attn_kernel,
        name="attn",
        grid=(bsz, pairs),
        in_specs=[
            pl.BlockSpec((1, seq, HEAD_PAIR * AUG), lambda b, h: (b, 0, h)),
            pl.BlockSpec((1, HEAD_PAIR * AUG, seq), lambda b, h: (b, h, 0)),
            pl.BlockSpec((1, seq, LANES), lambda b, h: (b, 0, h)),
            wspec((D_MODEL, D_EXPERT)), wspec((D_MODEL, D_EXPERT)), wspec((D_EXPERT, D_MODEL)),
        ],
        out_specs=[
            pl.BlockSpec((1, seq, LANES), lambda b, h: (b, 0, h)),
            wspec((D_MODEL, D_EXPERT)), wspec((D_MODEL, D_EXPERT)), wspec((D_EXPERT, D_MODEL)),
        ],
        out_shape=[
            jax.ShapeDtypeStruct((bsz, seq, D_ATTN), F32),
            jax.ShapeDtypeStruct(wg.shape, BF16),
            jax.ShapeDtypeStruct(wu.shape, BF16),
            jax.ShapeDtypeStruct(wd.shape, BF16),
        ],
        compiler_params=pltpu.CompilerParams(
            dimension_semantics=("arbitrary", "arbitrary"), vmem_limit_bytes=VMEM_LIMIT),
    )(q, kt, v, wg, wu, wd)


def _postmix_kernel(x_ref, pool_ref, o_ref, wout_ref, gattn_ref, gpost_ref, gate_ref,
                    gffn_ref, shift_ref, scale_ref, wr_ref, bias_ref,
                    x1_ref, h2_ref, infok_ref, cnt_ref):
    ya = _rms(o_ref[0], gattn_ref[...]).astype(BF16)
    mixed = _dot(pool_ref[0], wout_ref[:D_POOL, :]) + _dot(ya, wout_ref[D_POOL:, :])
    x1 = x_ref[0] + gate_ref[0] * _rms(mixed, gpost_ref[...])
    x1_ref[0] = x1
    h2 = _rms(x1, gffn_ref[...]) * (1.0 + scale_ref[0]) + shift_ref[0]
    h2b = h2.astype(BF16)
    h2_ref[0] = h2b
    h2l = (h2 - h2b.astype(F32)).astype(BF16)
    both = _dot(h2b, wr_ref[...])
    logits = both[:, :LANES] + both[:, LANES:] + _dot(h2l, wr_ref[:, :LANES])
    st = jax.nn.sigmoid(logits).T

    step = pl.program_id(0) * pl.num_programs(1) + pl.program_id(1)

    @pl.when(step == 0)
    def _():
        cnt_ref[...] = jnp.zeros_like(cnt_ref)

    lane = lax.broadcasted_iota(jnp.int32, cnt_ref.shape, 1)
    tiles = x_ref.shape[1] // TW
    for i in range(tiles):
        info, n = _route_tile(st[:, i * TW:(i + 1) * TW], bias_ref)
        infok_ref[:, i * TW:(i + 1) * TW] = info
        cnt_ref[...] = jnp.where(lane == step * tiles + i, jnp.broadcast_to(n, cnt_ref.shape),
                                 cnt_ref[...])


def _postmix(x, pool, ot, wout, gattn, gpost, gate, gffn, shift, scale, wr, bias_col):
    bsz, seq, _ = x.shape
    ts = TS_POST
    ns = seq // ts
    full = lambda shape: pl.BlockSpec(shape, lambda b, s: (0,) * len(shape))
    per_batch = pl.BlockSpec((1, 1, D_MODEL), lambda b, s: (b, 0, 0))
    tok = lambda width: pl.BlockSpec((1, ts, width), lambda b, s: (b, s, 0))
    return pl.pallas_call(
        _postmix_kernel,
        name="postmix",
        grid=(bsz, ns),
        in_specs=[
            tok(D_MODEL), tok(D_POOL), tok(D_ATTN),
            full((D_MODEL, D_MODEL)),
            full((1, D_ATTN)), full((1, D_MODEL)), per_batch,
            full((1, D_MODEL)), per_batch, per_batch,
            full((D_MODEL, 2 * LANES)),
            full((N_EXPERTS, 1)),
        ],
        out_specs=[
            tok(D_MODEL), tok(D_MODEL),
            pl.BlockSpec((2 * TOP_K, ts), lambda b, s: (0, b * ns + s)),
            full((N_EXPERTS, LANES)),
        ],
        out_shape=[
            jax.ShapeDtypeStruct((bsz, seq, D_MODEL), F32),
            jax.ShapeDtypeStruct((bsz, seq, D_MODEL), BF16),
            jax.ShapeDtypeStruct((2 * TOP_K, bsz * seq), F32),
            jax.ShapeDtypeStruct((N_EXPERTS, LANES), F32),
        ],
        compiler_params=pltpu.CompilerParams(
            dimension_semantics=("arbitrary", "arbitrary"), vmem_limit_bytes=VMEM_LIMIT),
    )(x, pool, ot, wout, gattn, gpost, gate, gffn, shift, scale, wr, bias_col)


def _route_tile(st, bias_ref):
    tr = st.shape[1]
    scores = [st[g * GROUP_SIZE:(g + 1) * GROUP_SIZE, :] for g in range(N_EXPERT_GROUPS)]
    sel = [scores[g] + bias_ref[g * GROUP_SIZE:(g + 1) * GROUP_SIZE, :]
           for g in range(N_EXPERT_GROUPS)]
    sub = lax.broadcasted_iota(jnp.int32, (GROUP_SIZE, tr), 0)
    neg = jnp.float32(-jnp.inf)

    rows = []
    for g in range(N_EXPERT_GROUPS):
        v = sel[g]
        m1 = jnp.max(v, axis=0, keepdims=True)
        first = jnp.min(jnp.where(v == m1, sub, GROUP_SIZE), axis=0, keepdims=True)
        m2 = jnp.max(jnp.where(sub == first, neg, v), axis=0, keepdims=True)
        rows.append(m1 + m2)
    gs = jnp.concatenate(rows, axis=0)

    shape = (GROUP_SIZE, tr)
    one = jnp.ones(shape, jnp.int32)
    zero = jnp.zeros(shape, jnp.int32)
    beaten = zero
    for gp in range(N_EXPERT_GROUPS):
        r = jnp.broadcast_to(gs[gp:gp + 1, :], shape)
        tie = jnp.where(sub > gp, one, zero)
        beaten = beaten + jnp.where(r > gs, one, zero) + jnp.where(r == gs, tie, zero)
    keep = jnp.where(beaten < TOPK_GROUPS, 1.0, 0.0)

    masked = [jnp.where(jnp.broadcast_to(keep[g:g + 1, :], shape) > 0.5, sel[g], neg)
              for g in range(N_EXPERT_GROUPS)]

    eidx = [sub + g * GROUP_SIZE for g in range(N_EXPERT_GROUPS)]
    picked = [zero for _ in range(N_EXPERT_GROUPS)]
    for _ in range(TOP_K):
        best = masked[0]
        for g in range(1, N_EXPERT_GROUPS):
            best = jnp.maximum(best, masked[g])
        best = jnp.broadcast_to(jnp.max(best, axis=0, keepdims=True), shape)
        first = jnp.where(masked[0] == best, eidx[0], N_EXPERTS)
        for g in range(1, N_EXPERT_GROUPS):
            first = jnp.minimum(first, jnp.where(masked[g] == best, eidx[g], N_EXPERTS))
        first = jnp.broadcast_to(jnp.min(first, axis=0, keepdims=True), shape)
        for g in range(N_EXPERT_GROUPS):
            hit = eidx[g] == first
            picked[g] = jnp.where(hit, one, picked[g])
            masked[g] = jnp.where(hit, neg, masked[g])
    chosen = [picked[g] > 0 for g in range(N_EXPERT_GROUPS)]

    w = [jnp.where(chosen[g], scores[g], 0.0) for g in range(N_EXPERT_GROUPS)]
    denom = w[0].sum(axis=0, keepdims=True)
    for g in range(1, N_EXPERT_GROUPS):
        denom = denom + w[g].sum(axis=0, keepdims=True)
    wfull = jnp.concatenate([w[g] / denom * ROUTED_SCALE for g in range(N_EXPERT_GROUPS)], axis=0)
    mfull = jnp.concatenate([jnp.where(chosen[g], 1.0, 0.0) for g in range(N_EXPERT_GROUPS)],
                            axis=0)
    mb = mfull.astype(BF16)

    e_r = lax.broadcasted_iota(jnp.int32, (N_EXPERTS, N_EXPERTS), 0)
    e_c = lax.broadcasted_iota(jnp.int32, (N_EXPERTS, N_EXPERTS), 1)
    before_e = (e_c < e_r).astype(BF16)
    t_r = lax.broadcasted_iota(jnp.int32, (tr, tr), 0)
    t_c = lax.broadcasted_iota(jnp.int32, (tr, tr), 1)
    before_t = (t_r < t_c).astype(BF16)
    ordinal = _dot(before_e, mb)
    rank = _dot(mb, before_t)
    n = jnp.sum(mfull, axis=1, keepdims=True)
    pieces = jnp.floor((n + (PIECE - 1.0)) * (1.0 / PIECE))
    run_start = PIECE * _dot(before_e, jnp.broadcast_to(pieces, (N_EXPERTS, LANES)).astype(BF16))
    pos = run_start[:, 0:1] + rank

    sub8 = lax.broadcasted_iota(jnp.int32, (TOP_K, tr), 0)
    pos8 = jnp.zeros((TOP_K, tr), F32)
    w8 = jnp.zeros((TOP_K, tr), F32)
    for k in range(TOP_K):
        selk = jnp.where(ordinal == float(k), mfull, 0.0)
        pk = jnp.sum(selk * pos, axis=0, keepdims=True)
        wk = jnp.sum(selk * wfull, axis=0, keepdims=True)
        pos8 = jnp.where(sub8 == k, jnp.broadcast_to(pk, (TOP_K, tr)), pos8)
        w8 = jnp.where(sub8 == k, jnp.broadcast_to(wk, (TOP_K, tr)), w8)
    return jnp.concatenate([pos8, w8], axis=0), n


def _swiglu(xb, wg, wu):
    g = _dot(xb, wg)
    return (g * jax.nn.sigmoid(g)) * _dot(xb, wu)


def _piece_copy(src_ref, src_piece, dst_ref, dst_piece, sem, n_pieces=1):
    src = src_ref.at[pl.ds(src_piece, n_pieces)]
    dst = dst_ref.at[pl.ds(dst_piece, n_pieces)]
    return pltpu.make_async_copy(src, dst, sem)


def _as_pieces(rows):
    return rows.reshape(rows.shape[0] // PIECE, PIECE, rows.shape[1])


LOCAL_BITS = 8
COPY_SIZES = (4, 3, 2, 1)
MAX_COPIES = N_EXPERTS
assert LROWS // (COPY_SIZES[0] * PIECE) <= MAX_COPIES


def _for_each_copy(copies_ref, counts_ref, tile, fn):
    for c, size in enumerate(COPY_SIZES):
        base = (tile * len(COPY_SIZES) + c) * MAX_COPIES

        def body(i, carry, base=base, size=size):
            word = copies_ref[base + i]
            fn(word & ((1 << LOCAL_BITS) - 1), lax.shift_right_logical(word, LOCAL_BITS), size)
            return carry

        lax.fori_loop(0, counts_ref[tile * (len(COPY_SIZES) + 1) + c], body, 0)


def _tile_pieces(counts_ref, tile):
    return counts_ref[tile * (len(COPY_SIZES) + 1) + len(COPY_SIZES)]


def _wait_pieces(total, copy_of):
    for bit in range((LROWS // PIECE).bit_length()):
        @pl.when((lax.shift_right_logical(total, bit) & 1) == 1)
        def _():
            copy_of(1 << bit).wait()


def _onehot_chunks(infok_ref, weighted):
    pos = infok_ref[0:TOP_K, :]
    grp = jnp.floor(pos * (1.0 / ONEHOT_GROUP))
    off = pos - ONEHOT_GROUP * grp
    sub = lax.broadcasted_iota(jnp.int32, (ONEHOT_GROUP, TW), 0).astype(F32)
    offs, grps = [], []
    for k in range(TOP_K):
        value = infok_ref[TOP_K + k:TOP_K + k + 1, :] if weighted else 1.0
        offs.append(jnp.where(sub == off[k:k + 1, :], value, 0.0).astype(BF16))
        grps.append(jnp.broadcast_to(grp[k:k + 1, :], (PIECE, TW)).astype(BF16))
    zero = jnp.zeros((PIECE, TW), BF16)

    def chunk(j):
        tiles = []
        for a in range(j * LCHUNK // ONEHOT_GROUP, (j + 1) * LCHUNK // ONEHOT_GROUP):
            hit = [grps[k] == a for k in range(TOP_K)]
            for i in range(ONEHOT_GROUP // PIECE):
                rows = zero
                for k in range(TOP_K):
                    rows = rows + jnp.where(hit[k], offs[k][i * PIECE:(i + 1) * PIECE, :], zero)
                tiles.append(rows)
        return jnp.concatenate(tiles, axis=0)

    return chunk


def _dispatch_kernel(copies_ref, counts_ref, tail_ref, ntail_ref, nv_ref,
                     h2_ref, infok_ref, xs_ref, lbuf, zbuf, sem, zsem):
    tile = pl.program_id(0)
    last = pl.num_programs(0) - 1
    slot = tile % 2
    block_pieces = RB // PIECE
    n_blocks = xs_ref.shape[0] // block_pieces

    def spare_blocks(fn):
        def per_block(b, carry):
            fn(b)
            return carry
        lax.fori_loop(nv_ref[0], n_blocks, per_block, 0)

    def zero_block(b):
        return pltpu.make_async_copy(zbuf, xs_ref.at[pl.ds(b * block_pieces, block_pieces)], zsem)

    def tails(fn):
        def per_expert(e, carry):
            n = ntail_ref[e]
            for bit in range((RB // PIECE - 1).bit_length()):
                @pl.when((lax.shift_right_logical(n, bit) & 1) == 1)
                def _():
                    done = n & ((1 << bit) - 1)
                    fn(_piece_copy(zbuf, 0, xs_ref, tail_ref[e] + done, zsem, 1 << bit))
            return carry
        lax.fori_loop(0, N_EXPERTS, per_expert, 0)

    def start(t, s):
        _for_each_copy(
            copies_ref, counts_ref, t,
            lambda lp, gp, n: _piece_copy(lbuf.at[s], lp, xs_ref, gp, sem.at[s], n).start())

    def wait(t, s):
        _wait_pieces(_tile_pieces(counts_ref, t),
                     lambda n: _piece_copy(lbuf.at[s], 0, xs_ref, 0, sem.at[s], n))

    @pl.when(tile == 0)
    def _():
        zbuf[...] = jnp.zeros_like(zbuf)
        tails(lambda copy: copy.start())
        spare_blocks(lambda b: zero_block(b).start())

    @pl.when(tile >= 2)
    def _():
        wait(tile - 2, slot)

    onehot = _onehot_chunks(infok_ref, weighted=False)
    xb = h2_ref[...]
    used_rows = PIECE * _tile_pieces(counts_ref, tile)

    def sort_chunk(j):
        lbuf[slot, j * LCHUNK // PIECE:(j + 1) * LCHUNK // PIECE] = _as_pieces(
            _dot(onehot(j), xb).astype(BF16))

    n_chunks = LROWS // LCHUNK
    for j in range(n_chunks - 1):
        sort_chunk(j)
    pl.when(used_rows > (n_chunks - 1) * LCHUNK)(functools.partial(sort_chunk, n_chunks - 1))

    start(tile, slot)

    @pl.when(tile == last)
    def _():
        @pl.when(tile >= 1)
        def _():
            wait(tile - 1, 1 - slot)
        wait(tile, slot)
        tails(lambda copy: copy.wait())
        spare_blocks(lambda b: zero_block(b).wait())


def _dispatch(tables, h2, infok, n_rows):
    t = h2.shape[0]
    return pl.pallas_call(
        _dispatch_kernel,
        name="dispatch",
        grid_spec=pltpu.PrefetchScalarGridSpec(
            num_scalar_prefetch=5,
            grid=(t // TW,),
            in_specs=[
                pl.BlockSpec((TW, D_MODEL), lambda i, *_: (i, 0)),
                pl.BlockSpec((2 * TOP_K, TW), lambda i, *_: (0, i)),
            ],
            out_specs=pl.BlockSpec(memory_space=pl.ANY),
            scratch_shapes=[
                pltpu.VMEM((2, LROWS // PIECE, PIECE, D_MODEL), BF16),
                pltpu.VMEM((RB // PIECE, PIECE, D_MODEL), BF16),
                pltpu.SemaphoreType.DMA((2,)),
                pltpu.SemaphoreType.DMA(()),
            ],
        ),
        out_shape=jax.ShapeDtypeStruct((n_rows // PIECE, PIECE, D_MODEL), BF16),
        compiler_params=pltpu.CompilerParams(
            dimension_semantics=("arbitrary",), vmem_limit_bytes=VMEM_LIMIT),
    )(*tables, h2, infok)


def _experts_kernel(be_ref, nv_ref, xs_ref, wg_ref, wu_ref, wd_ref, ys_ref):
    @pl.when(pl.program_id(0) < nv_ref[0])
    def _():
        for j in range(RB // RSUB):
            rows = slice(j * RSUB, (j + 1) * RSUB)
            h = _swiglu(xs_ref[rows, :], wg_ref[0], wu_ref[0])
            ys_ref[rows, :] = _dot(h.astype(BF16), wd_ref[0]).astype(BF16)


def _experts(block_expert, n_valid, xs, wg, wu, wd):
    n_blocks = xs.shape[0] // RB
    rows = pl.BlockSpec((RB, D_MODEL), lambda i, be, nv: (jnp.minimum(i, nv[0] - 1), 0))
    return pl.pallas_call(
        _experts_kernel,
        name="experts",
        grid_spec=pltpu.PrefetchScalarGridSpec(
            num_scalar_prefetch=2,
            grid=(n_blocks,),
            in_specs=[
                rows,
                pl.BlockSpec((1, D_MODEL, D_EXPERT), lambda i, be, nv: (be[i], 0, 0)),
                pl.BlockSpec((1, D_MODEL, D_EXPERT), lambda i, be, nv: (be[i], 0, 0)),
                pl.BlockSpec((1, D_EXPERT, D_MODEL), lambda i, be, nv: (be[i], 0, 0)),
            ],
            out_specs=rows,
        ),
        out_shape=jax.ShapeDtypeStruct(xs.shape, BF16),
        input_output_aliases={2: 0},
        compiler_params=pltpu.CompilerParams(
            dimension_semantics=("arbitrary",), vmem_limit_bytes=VMEM_LIMIT),
    )(block_expert, n_valid, xs, wg, wu, wd)


def _combine_kernel(copies_ref, counts_ref,
                    infok_ref, h2_ref, x1_ref, gate_ref, gpost_ref, wsg_ref, wsu_ref, wsd_ref,
                    ys_ref, o_ref, ybuf, acc_ref, sem):
    tile = pl.program_id(0)
    last = pl.num_programs(0) - 1
    slot = tile % 2

    def start(t, s):
        _for_each_copy(
            copies_ref, counts_ref, t,
            lambda lp, gp, n: _piece_copy(ys_ref, gp, ybuf.at[s], lp, sem.at[s], n).start())

    def wait(t, s):
        _wait_pieces(_tile_pieces(counts_ref, t),
                     lambda n: _piece_copy(ys_ref, 0, ybuf.at[s], 0, sem.at[s], n))

    @pl.when(tile == 0)
    def _():
        ybuf[...] = jnp.zeros_like(ybuf)
        start(tile, slot)

    @pl.when(tile < last)
    def _():
        start(tile + 1, 1 - slot)

    wait(tile, slot)

    weights = _onehot_chunks(infok_ref, weighted=True)
    hs = _swiglu(h2_ref[...], wsg_ref[...], wsu_ref[...])
    ff = _dot(hs.astype(BF16), wsd_ref[...])
    used_rows = PIECE * _tile_pieces(counts_ref, tile)

    def sum_chunk(j):
        rows = ybuf[slot, j * LCHUNK // PIECE:(j + 1) * LCHUNK // PIECE].reshape(LCHUNK, D_MODEL)
        return lax.dot_general(weights(j), rows,
                               (((0,), (0,)), ((), ())), preferred_element_type=F32)

    def add_chunk(j):
        acc_ref[...] += sum_chunk(j)

    n_chunks = LROWS // LCHUNK
    for j in range(n_chunks - 1):
        ff = ff + sum_chunk(j)
    acc_ref[...] = ff
    pl.when(used_rows > (n_chunks - 1) * LCHUNK)(functools.partial(add_chunk, n_chunks - 1))
    o_ref[...] = x1_ref[...] + gate_ref[0] * _rms(acc_ref[...], gpost_ref[...])


def _combine(tables, infok, h2, x1, gate, gpost, wsg, wsu, wsd, ys, seq):
    t = h2.shape[0]
    per_seq = seq // TW
    full = lambda shape: pl.BlockSpec(shape, lambda i, *_: (0,) * len(shape))
    tok = lambda width: pl.BlockSpec((TW, width), lambda i, *_: (i, 0))
    return pl.pallas_call(
        _combine_kernel,
        name="combine",
        grid_spec=pltpu.PrefetchScalarGridSpec(
            num_scalar_prefetch=2,
            grid=(t // TW,),
            in_specs=[
                pl.BlockSpec((2 * TOP_K, TW), lambda i, *_: (0, i)), tok(D_MODEL), tok(D_MODEL),
                pl.BlockSpec((1, 1, D_MODEL), lambda i, *_: (i // per_seq, 0, 0)),
                full((1, D_MODEL)),
                full((D_MODEL, D_SHARED)), full((D_MODEL, D_SHARED)), full((D_SHARED, D_MODEL)),
                pl.BlockSpec(memory_space=pl.ANY),
            ],
            out_specs=tok(D_MODEL),
            scratch_shapes=[
                pltpu.VMEM((2, LROWS // PIECE, PIECE, D_MODEL), BF16),
                pltpu.VMEM((TW, D_MODEL), F32),
                pltpu.SemaphoreType.DMA((2,)),
            ],
        ),
        out_shape=jax.ShapeDtypeStruct((t, D_MODEL), F32),
        compiler_params=pltpu.CompilerParams(
            dimension_semantics=("arbitrary",), vmem_limit_bytes=VMEM_LIMIT),
    )(*tables, infok, h2, x1, gate, gpost, wsg, wsu, wsd, ys)


def _row_buffer_blocks(n_tiles):
    return (TOP_K * TW * n_tiles + N_EXPERTS * n_tiles * (PIECE - 1)
            + N_EXPERTS * (RB - PIECE) + RB - 1) // RB


def _dispatch_plan(cnt, n_tiles):
    n = cnt[:, :n_tiles].astype(jnp.int32)
    pieces = (n + PIECE - 1) // PIECE
    local = jnp.cumsum(pieces, axis=0) - pieces
    seg = jnp.sum(pieces, axis=1)
    per_block = RB // PIECE
    seg_pad = (seg + per_block - 1) // per_block * per_block
    seg_end = jnp.cumsum(seg_pad)
    seg_start = seg_end - seg_pad
    glob = seg_start[:, None] + jnp.cumsum(pieces, axis=1) - pieces
    n_blocks = _row_buffer_blocks(n_tiles)
    n_valid = seg_end[-1] // per_block
    blk = jnp.minimum(jnp.arange(n_blocks, dtype=jnp.int32), n_valid - 1)
    block_end = seg_end // per_block
    block_expert = jnp.sum((block_end[None, :] <= blk[:, None]).astype(jnp.int32), axis=1)
    block_expert = jnp.minimum(block_expert, N_EXPERTS - 1)
    assert LROWS // PIECE <= 1 << LOCAL_BITS and n_blocks * per_block < 1 << (31 - LOCAL_BITS)

    def copy_list(count, width):
        end = jnp.cumsum(count, axis=0).T
        slot = jnp.arange(width, dtype=jnp.int32)
        expert = jnp.sum((end[:, None, :] <= slot[None, :, None]).astype(jnp.int32), axis=-1)
        pick = expert[:, :, None] == jnp.arange(N_EXPERTS, dtype=jnp.int32)
        take = lambda a: jnp.sum(jnp.where(pick, a.T[:, None, :], 0), axis=-1)
        return slot[None, :] - take(jnp.cumsum(count, axis=0) - count), take

    big = COPY_SIZES[0]
    assert COPY_SIZES == tuple(range(big, 0, -1))
    n_big = pieces // big
    rest = pieces % big
    packed = local | (glob << LOCAL_BITS)
    step = 1 + (1 << LOCAL_BITS)
    within, take = copy_list(n_big, MAX_COPIES)
    lists = [take(packed) + big * step * within]
    counts = [jnp.sum(n_big, axis=0)]
    rest_start = packed + big * step * n_big
    for size in COPY_SIZES[1:]:
        has = (rest == size).astype(jnp.int32)
        _, take = copy_list(has, MAX_COPIES)
        lists.append(take(rest_start))
        counts.append(jnp.sum(has, axis=0))
    counts.append(jnp.sum(pieces, axis=0))
    tables = (jnp.stack(lists, axis=1).reshape(-1).astype(jnp.int32),
              jnp.stack(counts, axis=1).reshape(-1).astype(jnp.int32))
    tails = ((seg_start + seg).astype(jnp.int32), (seg_pad - seg).astype(jnp.int32))
    return tables, tails, block_expert, n_valid.reshape(1).astype(jnp.int32), n_blocks * RB


def kernel(x, c, w_ada, b_ada, g_pre_mix, g_post_mix, g_pre_ffn, g_post_ffn, w_in, b_forget,
           w_pool, pool_scale, g_pool_out, g_attn_out, w_out, w_router, router_bias,
           w_gate, w_up, w_down, ws_gate, ws_up, ws_down):
    bsz, seq, d = x.shape
    depth = w_ada.shape[0]
    for l in range(depth):
        mod = _ada(c, w_ada, b_ada[l][None, :], l)
        shift_m, scale_m, gate_m, shift_f, scale_f, gate_f = [
            m.reshape(bsz, 1, d) for m in jnp.split(mod, 6, axis=-1)]

        bfp = jnp.pad(b_forget[l], (0, LANES - N_HEADS))[None, :]
        pool, qa, kt, v = _premix(
            x, shift_m, scale_m, g_pre_mix[l][None, :], w_in, l, bfp, w_pool[l].astype(BF16),
            pool_scale[l][None, :], g_pool_out[l][None, :])
        ot, wgb, wub, wdb = _attn(qa, kt, v, w_gate[l], w_up[l], w_down[l])
        t = bsz * seq
        assert t // TW <= LANES

        wr = jnp.pad(w_router[l], ((0, 0), (0, LANES - N_EXPERTS)))
        wrh = wr.astype(BF16)
        wrl = (wr - wrh.astype(F32)).astype(BF16)
        x1, h2, infok, cnt = _postmix(
            x, pool, ot, w_out[l].astype(BF16), g_attn_out[l][None, :], g_post_mix[l][None, :],
            gate_m, g_pre_ffn[l][None, :], shift_f, scale_f,
            jnp.concatenate([wrh, wrl], axis=1), router_bias[l][:, None])
        tables, tails, block_expert, n_valid, n_rows = _dispatch_plan(cnt, t // TW)
        h2f = h2.reshape(t, d)
        xs = _dispatch(tables + tails + (n_valid,), h2f, infok, n_rows)
        ys = _experts(block_expert, n_valid, xs.reshape(n_rows, d), wgb, wub, wdb)
        ys = ys.reshape(xs.shape)
        out = _combine(tables, infok, h2f, x1.reshape(t, d), gate_f, g_post_ffn[l][None, :],
                       ws_gate[l].astype(BF16), ws_up[l].astype(BF16), ws_down[l].astype(BF16),
                       ys, seq)
        x = out.reshape(bsz, seq, d)
    return x
```

```python
import functools

import numpy as np
import jax
import jax.numpy as jnp
from jax import lax
from jax.experimental import pallas as pl
from jax.experimental.pallas import tpu as pltpu

D_MODEL = 1024
D_POOL = 512
POOL_WINDOWS = (2, 4, 8, 16)
POOL_GROUP = 128
MAX_WINDOW = max(POOL_WINDOWS)
D_ATTN = 512
HEAD_DIM = 64
N_HEADS = 8
N_EXPERTS = 64
N_EXPERT_GROUPS = 8
GROUP_SIZE = N_EXPERTS // N_EXPERT_GROUPS
TOPK_GROUPS = 4
TOP_K = 8
D_EXPERT = 256
D_SHARED = 256
ROUTED_SCALE = 2.5
EPS = 1e-6

LOG2E = 1.4426950408889634
LANES = 128
N_SPLIT = 3
AUG = LANES

TS_PRE = 1024
CUM_BLOCK = 256
TS_POST = 1024
TQ = 512
TK = 512
HEAD_PAIR = LANES // HEAD_DIM
assert TQ == TK
TW = 256
PIECE = 16
RB = 1024
RSUB = 512
LCHUNK = 512
ONEHOT_GROUP = 64
LROWS = -(-(TOP_K * TW + N_EXPERTS * (PIECE - 1)) // LCHUNK) * LCHUNK

F32 = jnp.float32
BF16 = jnp.bfloat16
V7X_VMEM_BYTES = 64 * 1024 * 1024
VMEM_LIMIT = V7X_VMEM_BYTES * 7 // 8


def _rms(v, g):
    return v * lax.rsqrt(jnp.mean(v * v, axis=-1, keepdims=True) + EPS) * g


def _split3(v):
    hi = v.astype(BF16)
    r1 = v - hi.astype(F32)
    mid = r1.astype(BF16)
    r2 = r1 - mid.astype(F32)
    lo = r2.astype(BF16)
    return hi, mid, lo


def _dot(a, b):
    return jnp.dot(a, b, preferred_element_type=F32)


def _ada_kernel(c_ref, w_ref, b_ref, o_ref):
    o_ref[...] = _dot(c_ref[...].astype(BF16), w_ref[0].astype(BF16)) + b_ref[...]


def _ada(c, w, b, layer):
    bsz = c.shape[0]
    n = w.shape[2]
    return pl.pallas_call(
        _ada_kernel,
        name="ada",
        grid=(n // D_MODEL,),
        in_specs=[
            pl.BlockSpec((bsz, D_MODEL), lambda j: (0, 0)),
            pl.BlockSpec((1, D_MODEL, D_MODEL), lambda j: (layer, 0, j)),
            pl.BlockSpec((1, D_MODEL), lambda j: (0, j)),
        ],
        out_specs=pl.BlockSpec((bsz, D_MODEL), lambda j: (0, j)),
        out_shape=jax.ShapeDtypeStruct((bsz, n), F32),
    )(c, w, b)


def _premix_kernel(x_ref, shift_ref, scale_ref, g_ref, win_ref, bf_ref, wpool_ref,
                   pscale_ref, gpool_ref, place_ref, ones_ref, fmask_ref,
                   pool_ref, q_ref, kt_ref, v_ref,
                   uext_ref, cum_ref, w1_ref, wf_ref):
    s = pl.program_id(1)
    ts = x_ref.shape[1]

    @pl.when((pl.program_id(0) == 0) & (s == 0))
    def _():
        q0, k0 = D_POOL, D_POOL + D_ATTN
        w1_ref[:, :q0] = win_ref[0, :, :q0].astype(BF16)
        w1_ref[:, q0:k0] = (win_ref[0, :, q0:k0] * (HEAD_DIM ** -0.5 * LOG2E)).astype(BF16)
        w1_ref[:, k0:] = win_ref[0, :, k0:k0 + 2 * D_ATTN].astype(BF16)
        wf_ref[...] = jnp.zeros_like(wf_ref)
        wf_ref[:, :N_HEADS] = win_ref[0, :, k0 + 2 * D_ATTN:].astype(BF16)

    @pl.when(s == 0)
    def _():
        uext_ref[0:MAX_WINDOW, :] = jnp.zeros((MAX_WINDOW, D_POOL), F32)
        cum_ref[...] = jnp.zeros_like(cum_ref)

    x = x_ref[0]
    h = _rms(x, g_ref[...]) * (1.0 + scale_ref[0]) + shift_ref[0]
    hb = h.astype(BF16)
    proj = _dot(hb, w1_ref[...])
    u = proj[:, :D_POOL]
    q = proj[:, D_POOL:D_POOL + D_ATTN]
    k = proj[:, D_POOL + D_ATTN:D_POOL + 2 * D_ATTN]
    v = proj[:, D_POOL + 2 * D_ATTN:]

    uext_ref[MAX_WINDOW:, :] = u
    pos = (s * ts + lax.broadcasted_iota(jnp.int32, (ts, 1), 0) + 1).astype(F32)
    ys = []
    for g, w in enumerate(POOL_WINDOWS):
        c0 = g * POOL_GROUP
        acc = uext_ref[MAX_WINDOW:, c0:c0 + POOL_GROUP]
        for j in range(1, w):
            acc = acc + uext_ref[MAX_WINDOW - j:MAX_WINDOW - j + ts, c0:c0 + POOL_GROUP]
        pooled = acc / jnp.minimum(pos, float(w)) - u[:, c0:c0 + POOL_GROUP]
        ys.append(_dot(pooled.astype(BF16), wpool_ref[g]))
    ypool = jnp.concatenate(ys, axis=1) * pscale_ref[...]
    pool_ref[0] = _rms(ypool, gpool_ref[...]).astype(BF16)
    uext_ref[0:MAX_WINDOW, :] = uext_ref[ts:ts + MAX_WINDOW, :]

    z0 = _dot(hb, wf_ref[...]) + bf_ref[...]
    z = z0
    for p in range(1, N_SPLIT):
        z = z + pltpu.roll(z0, p * N_HEADS, 1)
    logf = jnp.minimum(z, 0.0) - jnp.log1p(jnp.exp(-jnp.abs(z)))
    row = lax.broadcasted_iota(jnp.int32, (CUM_BLOCK, CUM_BLOCK), 0)
    col = lax.broadcasted_iota(jnp.int32, (CUM_BLOCK, CUM_BLOCK), 1)
    tri = (col <= row).astype(BF16)
    pieces = _split3(logf)
    carry = cum_ref[...]
    blocks = []
    for r0 in range(0, ts, CUM_BLOCK):
        blk = carry
        for piece in pieces:
            blk = blk + _dot(tri, piece[r0:r0 + CUM_BLOCK, :])
        carry = blk[CUM_BLOCK - 1:CUM_BLOCK, :]
        blocks.append(blk)
    cum = jnp.concatenate(blocks, axis=0)
    cum_ref[...] = carry

    hi, mid, lo = [piece.astype(F32) for piece in _split3(cum * LOG2E)]
    lane = lax.broadcasted_iota(jnp.int32, (ts, LANES), 1)
    pieces = jnp.where(lane < N_HEADS, hi, jnp.where(lane < 2 * N_HEADS, mid, lo))
    aug = _dot(pieces.astype(BF16), place_ref[...]) + ones_ref[...]
    aug_q = aug[:, :N_HEADS * AUG]
    aug_k = aug[:, N_HEADS * AUG:]

    def expand(a):
        blocks = []
        for j in range(D_ATTN // LANES):
            blk = a[:, j * LANES:(j + 1) * LANES]
            blocks += [blk, blk]
        return jnp.concatenate(blocks, axis=1)

    fmask = fmask_ref[...]
    qa = expand(q) * fmask + aug_q
    ka = expand(k) * fmask + aug_k
    q_ref[0] = qa.astype(BF16)
    kt_ref[0] = ka.T.astype(BF16)
    v_ref[0] = v.astype(BF16)


def _aug_constants():
    width = N_HEADS * AUG
    place = np.zeros((LANES, 2 * width), np.float32)
    ones = np.zeros((1, 2 * width), np.float32)
    fmask = np.zeros((1, width), np.float32)
    for h in range(N_HEADS):
        feat0 = h * AUG + (0 if h % 2 == 0 else HEAD_DIM)
        aug0 = h * AUG + (HEAD_DIM if h % 2 == 0 else 0)
        fmask[0, feat0:feat0 + HEAD_DIM] = 1.0
        for p in range(N_SPLIT):
            place[p * N_HEADS + h, aug0 + p] = 1.0
            ones[0, width + aug0 + p] = 1.0
            place[p * N_HEADS + h, width + aug0 + N_SPLIT + p] = -1.0
            ones[0, aug0 + N_SPLIT + p] = 1.0
    return jnp.asarray(place, BF16), jnp.asarray(ones), jnp.asarray(fmask)


def _premix(x, shift, scale, g, w_in, layer, bfp, wpool, pscale, gpool):
    bsz, seq, _ = x.shape
    d_in = w_in.shape[2]
    assert d_in == D_POOL + 3 * D_ATTN + N_HEADS
    ts = TS_PRE
    place, ones, fmask = _aug_constants()
    full = lambda shape: pl.BlockSpec(shape, lambda b, s: (0,) * len(shape))
    per_batch = pl.BlockSpec((1, 1, D_MODEL), lambda b, s: (b, 0, 0))
    return pl.pallas_call(
        _premix_kernel,
        name="premix",
        grid=(bsz, seq // ts),
        in_specs=[
            pl.BlockSpec((1, ts, D_MODEL), lambda b, s: (b, s, 0)),
            per_batch, per_batch,
            full((1, D_MODEL)),
            pl.BlockSpec((1, D_MODEL, d_in), lambda b, s: (layer, 0, 0),
                         pipeline_mode=pl.Buffered(1)),
            full((1, LANES)),
            full((len(POOL_WINDOWS), POOL_GROUP, POOL_GROUP)),
            full((1, D_POOL)),
            full((1, D_POOL)),
            full((LANES, 2 * N_HEADS * AUG)),
            full((1, 2 * N_HEADS * AUG)),
            full((1, N_HEADS * AUG)),
        ],
        out_specs=[
            pl.BlockSpec((1, ts, D_POOL), lambda b, s: (b, s, 0)),
            pl.BlockSpec((1, ts, N_HEADS * AUG), lambda b, s: (b, s, 0)),
            pl.BlockSpec((1, N_HEADS * AUG, ts), lambda b, s: (b, 0, s)),
            pl.BlockSpec((1, ts, D_ATTN), lambda b, s: (b, s, 0)),
        ],
        out_shape=[
            jax.ShapeDtypeStruct((bsz, seq, D_POOL), BF16),
            jax.ShapeDtypeStruct((bsz, seq, N_HEADS * AUG), BF16),
            jax.ShapeDtypeStruct((bsz, N_HEADS * AUG, seq), BF16),
            jax.ShapeDtypeStruct((bsz, seq, D_ATTN), BF16),
        ],
        scratch_shapes=[
            pltpu.VMEM((ts + MAX_WINDOW, D_POOL), F32),
            pltpu.VMEM((1, LANES), F32),
            pltpu.VMEM((D_MODEL, D_POOL + 3 * D_ATTN), BF16),
            pltpu.VMEM((D_MODEL, LANES), BF16),
        ],
        compiler_params=pltpu.CompilerParams(
            dimension_semantics=("arbitrary", "arbitrary"), vmem_limit_bytes=VMEM_LIMIT),
    )(x, shift, scale, g, w_in, bfp, wpool, pscale, gpool, place, ones, fmask)


def _attn_kernel(q_ref, kt_ref, v_ref, wg_ref, wu_ref, wd_ref, o_ref, wgb_ref, wub_ref, wdb_ref):
    wgb_ref[...] = wg_ref[...].astype(BF16)
    wub_ref[...] = wu_ref[...].astype(BF16)
    wdb_ref[...] = wd_ref[...].astype(BF16)
    seq = q_ref.shape[1]
    qry_i = lax.broadcasted_iota(jnp.int32, (TQ, TK), 0)
    key_i = lax.broadcasted_iota(jnp.int32, (TQ, TK), 1)
    causal = key_i <= qry_i
    one_lane = lax.broadcasted_iota(jnp.int32, (seq, LANES), 1) == 0
    vaug = jnp.concatenate([v_ref[0], jnp.where(one_lane, 1.0, 0.0).astype(BF16)], axis=1)
    out_lane = lax.broadcasted_iota(jnp.int32, (TQ, LANES), 1)
    heads = range(HEAD_PAIR)
    for qi in range(seq // TQ):
        rows = slice(qi * TQ, (qi + 1) * TQ)
        q = [q_ref[0, rows, h * AUG:(h + 1) * AUG] for h in heads]
        m = [jnp.full((TQ, 1), -jnp.inf, F32) for _ in heads]
        acc = [jnp.zeros((TQ, 2 * LANES), F32) for _ in heads]
        for kj in range(qi + 1):
            keys = slice(kj * TK, (kj + 1) * TK)
            for h in heads:
                s = _dot(q[h], kt_ref[0, h * AUG:(h + 1) * AUG, keys])
                if kj == qi:
                    s = jnp.where(causal, s, -jnp.inf)
                m_new = jnp.maximum(m[h], jnp.max(s, axis=1, keepdims=True))
                p = jnp.exp2(s - m_new).astype(BF16)
                acc[h] = jnp.exp2(m[h] - m_new) * acc[h] + _dot(p, vaug[keys, :])
                m[h] = m_new
        o = [acc[h][:, :LANES] / acc[h][:, LANES:LANES + 1] for h in heads]
        o_ref[0, rows, :] = jnp.where(out_lane < HEAD_DIM, o[0], o[1])


def _attn(q, kt, v, wg, wu, wd):
    bsz, seq, _ = q.shape
    pairs = N_HEADS // HEAD_PAIR
    per_step = N_EXPERTS // (bsz * pairs)
    assert per_step * bsz * pairs == N_EXPERTS
    wspec = lambda shape: pl.BlockSpec((per_step,) + shape, lambda b, h: (b * pairs + h, 0, 0))
    return pl.pallas_call(
        _attn_kernel,
        name="attn",
        grid=(bsz, pairs),
        in_specs=[
            pl.BlockSpec((1, seq, HEAD_PAIR * AUG), lambda b, h: (b, 0, h)),
            pl.BlockSpec((1, HEAD_PAIR * AUG, seq), lambda b, h: (b, h, 0)),
            pl.BlockSpec((1, seq, LANES), lambda b, h: (b, 0, h)),
            wspec((D_MODEL, D_EXPERT)), wspec((D_MODEL, D_EXPERT)), wspec((D_EXPERT, D_MODEL)),
        ],
        out_specs=[
            pl.BlockSpec((1, seq, LANES), lambda b, h: (b, 0, h)),
            wspec((D_MODEL, D_EXPERT)), wspec((D_MODEL, D_EXPERT)), wspec((D_EXPERT, D_MODEL)),
        ],
        out_shape=[
            jax.ShapeDtypeStruct((bsz, seq, D_ATTN), F32),
            jax.ShapeDtypeStruct(wg.shape, BF16),
            jax.ShapeDtypeStruct(wu.shape, BF16),
            jax.ShapeDtypeStruct(wd.shape, BF16),
        ],
        compiler_params=pltpu.CompilerParams(
            dimension_semantics=("arbitrary", "arbitrary"), vmem_limit_bytes=VMEM_LIMIT),
    )(q, kt, v, wg, wu, wd)


def _postmix_kernel(x_ref, pool_ref, o_ref, wout_ref, gattn_ref, gpost_ref, gate_ref,
                    gffn_ref, shift_ref, scale_ref, wr_ref, bias_ref,
                    x1_ref, h2_ref, infok_ref, cnt_ref):
    ya = _rms(o_ref[0], gattn_ref[...]).astype(BF16)
    mixed = _dot(pool_ref[0], wout_ref[:D_POOL, :]) + _dot(ya, wout_ref[D_POOL:, :])
    x1 = x_ref[0] + gate_ref[0] * _rms(mixed, gpost_ref[...])
    x1_ref[0] = x1
    h2 = _rms(x1, gffn_ref[...]) * (1.0 + scale_ref[0]) + shift_ref[0]
    h2b = h2.astype(BF16)
    h2_ref[0] = h2b
    h2l = (h2 - h2b.astype(F32)).astype(BF16)
    both = _dot(h2b, wr_ref[...])
    logits = both[:, :LANES] + both[:, LANES:] + _dot(h2l, wr_ref[:, :LANES])
    st = jax.nn.sigmoid(logits).T

    step = pl.program_id(0) * pl.num_programs(1) + pl.program_id(1)

    @pl.when(step == 0)
    def _():
        cnt_ref[...] = jnp.zeros_like(cnt_ref)

    lane = lax.broadcasted_iota(jnp.int32, cnt_ref.shape, 1)
    tiles = x_ref.shape[1] // TW
    for i in range(tiles):
        info, n = _route_tile(st[:, i * TW:(i + 1) * TW], bias_ref)
        infok_ref[:, i * TW:(i + 1) * TW] = info
        cnt_ref[...] = jnp.where(lane == step * tiles + i, jnp.broadcast_to(n, cnt_ref.shape),
                                 cnt_ref[...])


def _postmix(x, pool, ot, wout, gattn, gpost, gate, gffn, shift, scale, wr, bias_col):
    bsz, seq, _ = x.shape
    ts = TS_POST
    ns = seq // ts
    full = lambda shape: pl.BlockSpec(shape, lambda b, s: (0,) * len(shape))
    per_batch = pl.BlockSpec((1, 1, D_MODEL), lambda b, s: (b, 0, 0))
    tok = lambda width: pl.BlockSpec((1, ts, width), lambda b, s: (b, s, 0))
    return pl.pallas_call(
        _postmix_kernel,
        name="postmix",
        grid=(bsz, ns),
        in_specs=[
            tok(D_MODEL), tok(D_POOL), tok(D_ATTN),
            full((D_MODEL, D_MODEL)),
            full((1, D_ATTN)), full((1, D_MODEL)), per_batch,
            full((1, D_MODEL)), per_batch, per_batch,
            full((D_MODEL, 2 * LANES)),
            full((N_EXPERTS, 1)),
        ],
        out_specs=[
            tok(D_MODEL), tok(D_MODEL),
            pl.BlockSpec((2 * TOP_K, ts), lambda b, s: (0, b * ns + s)),
            full((N_EXPERTS, LANES)),
        ],
        out_shape=[
            jax.ShapeDtypeStruct((bsz, seq, D_MODEL), F32),
            jax.ShapeDtypeStruct((bsz, seq, D_MODEL), BF16),
            jax.ShapeDtypeStruct((2 * TOP_K, bsz * seq), F32),
            jax.ShapeDtypeStruct((N_EXPERTS, LANES), F32),
        ],
        compiler_params=pltpu.CompilerParams(
            dimension_semantics=("arbitrary", "arbitrary"), vmem_limit_bytes=VMEM_LIMIT),
    )(x, pool, ot, wout, gattn, gpost, gate, gffn, shift, scale, wr, bias_col)


def _route_tile(st, bias_ref):
    tr = st.shape[1]
    scores = [st[g * GROUP_SIZE:(g + 1) * GROUP_SIZE, :] for g in range(N_EXPERT_GROUPS)]
    sel = [scores[g] + bias_ref[g * GROUP_SIZE:(g + 1) * GROUP_SIZE, :]
           for g in range(N_EXPERT_GROUPS)]
    sub = lax.broadcasted_iota(jnp.int32, (GROUP_SIZE, tr), 0)
    neg = jnp.float32(-jnp.inf)

    rows = []
    for g in range(N_EXPERT_GROUPS):
        v = sel[g]
        m1 = jnp.max(v, axis=0, keepdims=True)
        first = jnp.min(jnp.where(v == m1, sub, GROUP_SIZE), axis=0, keepdims=True)
        m2 = jnp.max(jnp.where(sub == first, neg, v), axis=0, keepdims=True)
        rows.append(m1 + m2)
    gs = jnp.concatenate(rows, axis=0)

    shape = (GROUP_SIZE, tr)
    one = jnp.ones(shape, jnp.int32)
    zero = jnp.zeros(shape, jnp.int32)
    beaten = zero
    for gp in range(N_EXPERT_GROUPS):
        r = jnp.broadcast_to(gs[gp:gp + 1, :], shape)
        tie = jnp.where(sub > gp, one, zero)
        beaten = beaten + jnp.where(r > gs, one, zero) + jnp.where(r == gs, tie, zero)
    keep = jnp.where(beaten < TOPK_GROUPS, 1.0, 0.0)

    masked = [jnp.where(jnp.broadcast_to(keep[g:g + 1, :], shape) > 0.5, sel[g], neg)
              for g in range(N_EXPERT_GROUPS)]

    eidx = [sub + g * GROUP_SIZE for g in range(N_EXPERT_GROUPS)]
    picked = [zero for _ in range(N_EXPERT_GROUPS)]
    for _ in range(TOP_K):
        best = masked[0]
        for g in range(1, N_EXPERT_GROUPS):
            best = jnp.maximum(best, masked[g])
        best = jnp.broadcast_to(jnp.max(best, axis=0, keepdims=True), shape)
        first = jnp.where(masked[0] == best, eidx[0], N_EXPERTS)
        for g in range(1, N_EXPERT_GROUPS):
            first = jnp.minimum(first, jnp.where(masked[g] == best, eidx[g], N_EXPERTS))
        first = jnp.broadcast_to(jnp.min(first, axis=0, keepdims=True), shape)
        for g in range(N_EXPERT_GROUPS):
            hit = eidx[g] == first
            picked[g] = jnp.where(hit, one, picked[g])
            masked[g] = jnp.where(hit, neg, masked[g])
    chosen = [picked[g] > 0 for g in range(N_EXPERT_GROUPS)]

    w = [jnp.where(chosen[g], scores[g], 0.0) for g in range(N_EXPERT_GROUPS)]
    denom = w[0].sum(axis=0, keepdims=True)
    for g in range(1, N_EXPERT_GROUPS):
        denom = denom + w[g].sum(axis=0, keepdims=True)
    wfull = jnp.concatenate([w[g] / denom * ROUTED_SCALE for g in range(N_EXPERT_GROUPS)], axis=0)
    mfull = jnp.concatenate([jnp.where(chosen[g], 1.0, 0.0) for g in range(N_EXPERT_GROUPS)],
                            axis=0)
    mb = mfull.astype(BF16)

    e_r = lax.broadcasted_iota(jnp.int32, (N_EXPERTS, N_EXPERTS), 0)
    e_c = lax.broadcasted_iota(jnp.int32, (N_EXPERTS, N_EXPERTS), 1)
    before_e = (e_c < e_r).astype(BF16)
    t_r = lax.broadcasted_iota(jnp.int32, (tr, tr), 0)
    t_c = lax.broadcasted_iota(jnp.int32, (tr, tr), 1)
    before_t = (t_r < t_c).astype(BF16)
    ordinal = _dot(before_e, mb)
    rank = _dot(mb, before_t)
    n = jnp.sum(mfull, axis=1, keepdims=True)
    pieces = jnp.floor((n + (PIECE - 1.0)) * (1.0 / PIECE))
    run_start = PIECE * _dot(before_e, jnp.broadcast_to(pieces, (N_EXPERTS, LANES)).astype(BF16))
    pos = run_start[:, 0:1] + rank

    sub8 = lax.broadcasted_iota(jnp.int32, (TOP_K, tr), 0)
    pos8 = jnp.zeros((TOP_K, tr), F32)
    w8 = jnp.zeros((TOP_K, tr), F32)
    for k in range(TOP_K):
        selk = jnp.where(ordinal == float(k), mfull, 0.0)
        pk = jnp.sum(selk * pos, axis=0, keepdims=True)
        wk = jnp.sum(selk * wfull, axis=0, keepdims=True)
        pos8 = jnp.where(sub8 == k, jnp.broadcast_to(pk, (TOP_K, tr)), pos8)
        w8 = jnp.where(sub8 == k, jnp.broadcast_to(wk, (TOP_K, tr)), w8)
    return jnp.concatenate([pos8, w8], axis=0), n


def _swiglu(xb, wg, wu):
    g = _dot(xb, wg)
    return (g * jax.nn.sigmoid(g)) * _dot(xb, wu)


def _piece_copy(src_ref, src_piece, dst_ref, dst_piece, sem, n_pieces=1):
    src = src_ref.at[pl.ds(src_piece, n_pieces)]
    dst = dst_ref.at[pl.ds(dst_piece, n_pieces)]
    return pltpu.make_async_copy(src, dst, sem)


def _as_pieces(rows):
    return rows.reshape(rows.shape[0] // PIECE, PIECE, rows.shape[1])


LOCAL_BITS = 8
COPY_SIZES = (4, 3, 2, 1)
MAX_COPIES = N_EXPERTS
assert LROWS // (COPY_SIZES[0] * PIECE) <= MAX_COPIES


def _for_each_copy(copies_ref, counts_ref, tile, fn):
    for c, size in enumerate(COPY_SIZES):
        base = (tile * len(COPY_SIZES) + c) * MAX_COPIES

        def body(i, carry, base=base, size=size):
            word = copies_ref[base + i]
            fn(word & ((1 << LOCAL_BITS) - 1), lax.shift_right_logical(word, LOCAL_BITS), size)
            return carry

        lax.fori_loop(0, counts_ref[tile * (len(COPY_SIZES) + 1) + c], body, 0)


def _tile_pieces(counts_ref, tile):
    return counts_ref[tile * (len(COPY_SIZES) + 1) + len(COPY_SIZES)]


def _wait_pieces(total, copy_of):
    for bit in range((LROWS // PIECE).bit_length()):
        @pl.when((lax.shift_right_logical(total, bit) & 1) == 1)
        def _():
            copy_of(1 << bit).wait()


def _onehot_chunks(infok_ref, weighted):
    pos = infok_ref[0:TOP_K, :]
    grp = jnp.floor(pos * (1.0 / ONEHOT_GROUP))
    off = pos - ONEHOT_GROUP * grp
    sub = lax.broadcasted_iota(jnp.int32, (ONEHOT_GROUP, TW), 0).astype(F32)
    offs, grps = [], []
    for k in range(TOP_K):
        value = infok_ref[TOP_K + k:TOP_K + k + 1, :] if weighted else 1.0
        offs.append(jnp.where(sub == off[k:k + 1, :], value, 0.0).astype(BF16))
        grps.append(jnp.broadcast_to(grp[k:k + 1, :], (PIECE, TW)).astype(BF16))
    zero = jnp.zeros((PIECE, TW), BF16)

    def chunk(j):
        tiles = []
        for a in range(j * LCHUNK // ONEHOT_GROUP, (j + 1) * LCHUNK // ONEHOT_GROUP):
            hit = [grps[k] == a for k in range(TOP_K)]
            for i in range(ONEHOT_GROUP // PIECE):
                rows = zero
                for k in range(TOP_K):
                    rows = rows + jnp.where(hit[k], offs[k][i * PIECE:(i + 1) * PIECE, :], zero)
                tiles.append(rows)
        return jnp.concatenate(tiles, axis=0)

    return chunk


def _dispatch_kernel(copies_ref, counts_ref, tail_ref, ntail_ref, nv_ref,
                     h2_ref, infok_ref, xs_ref, lbuf, zbuf, sem, zsem):
    tile = pl.program_id(0)
    last = pl.num_programs(0) - 1
    slot = tile % 2
    block_pieces = RB // PIECE
    n_blocks = xs_ref.shape[0] // block_pieces

    def spare_blocks(fn):
        def per_block(b, carry):
            fn(b)
            return carry
        lax.fori_loop(nv_ref[0], n_blocks, per_block, 0)

    def zero_block(b):
        return pltpu.make_async_copy(zbuf, xs_ref.at[pl.ds(b * block_pieces, block_pieces)], zsem)

    def tails(fn):
        def per_expert(e, carry):
            n = ntail_ref[e]
            for bit in range((RB // PIECE - 1).bit_length()):
                @pl.when((lax.shift_right_logical(n, bit) & 1) == 1)
                def _():
                    done = n & ((1 << bit) - 1)
                    fn(_piece_copy(zbuf, 0, xs_ref, tail_ref[e] + done, zsem, 1 << bit))
            return carry
        lax.fori_loop(0, N_EXPERTS, per_expert, 0)

    def start(t, s):
        _for_each_copy(
            copies_ref, counts_ref, t,
            lambda lp, gp, n: _piece_copy(lbuf.at[s], lp, xs_ref, gp, sem.at[s], n).start())

    def wait(t, s):
        _wait_pieces(_tile_pieces(counts_ref, t),
                     lambda n: _piece_copy(lbuf.at[s], 0, xs_ref, 0, sem.at[s], n))

    @pl.when(tile == 0)
    def _():
        zbuf[...] = jnp.zeros_like(zbuf)
        tails(lambda copy: copy.start(priority=1))
        spare_blocks(lambda b: zero_block(b).start(priority=1))

    @pl.when(tile >= 2)
    def _():
        wait(tile - 2, slot)

    onehot = _onehot_chunks(infok_ref, weighted=False)
    xb = h2_ref[...]
    used_rows = PIECE * _tile_pieces(counts_ref, tile)

    def sort_chunk(j):
        lbuf[slot, j * LCHUNK // PIECE:(j + 1) * LCHUNK // PIECE] = _as_pieces(
            _dot(onehot(j), xb).astype(BF16))

    n_chunks = LROWS // LCHUNK
    for j in range(n_chunks - 1):
        sort_chunk(j)
    pl.when(used_rows > (n_chunks - 1) * LCHUNK)(functools.partial(sort_chunk, n_chunks - 1))

    start(tile, slot)

    @pl.when(tile == last)
    def _():
        @pl.when(tile >= 1)
        def _():
            wait(tile - 1, 1 - slot)
        wait(tile, slot)
        tails(lambda copy: copy.wait())
        spare_blocks(lambda b: zero_block(b).wait())


def _dispatch(tables, h2, infok, n_rows):
    t = h2.shape[0]
    return pl.pallas_call(
        _dispatch_kernel,
        name="dispatch",
        grid_spec=pltpu.PrefetchScalarGridSpec(
            num_scalar_prefetch=5,
            grid=(t // TW,),
            in_specs=[
                pl.BlockSpec((TW, D_MODEL), lambda i, *_: (i, 0)),
                pl.BlockSpec((2 * TOP_K, TW), lambda i, *_: (0, i)),
            ],
            out_specs=pl.BlockSpec(memory_space=pl.ANY),
            scratch_shapes=[
                pltpu.VMEM((2, LROWS // PIECE, PIECE, D_MODEL), BF16),
                pltpu.VMEM((RB // PIECE, PIECE, D_MODEL), BF16),
                pltpu.SemaphoreType.DMA((2,)),
                pltpu.SemaphoreType.DMA(()),
            ],
        ),
        out_shape=jax.ShapeDtypeStruct((n_rows // PIECE, PIECE, D_MODEL), BF16),
        compiler_params=pltpu.CompilerParams(
            dimension_semantics=("arbitrary",), vmem_limit_bytes=VMEM_LIMIT),
    )(*tables, h2, infok)


def _experts_kernel(be_ref, nv_ref, xs_ref, wg_ref, wu_ref, wd_ref, ys_ref):
    @pl.when(pl.program_id(0) < nv_ref[0])
    def _():
        for j in range(RB // RSUB):
            rows = slice(j * RSUB, (j + 1) * RSUB)
            h = _swiglu(xs_ref[rows, :], wg_ref[0], wu_ref[0])
            ys_ref[rows, :] = _dot(h.astype(BF16), wd_ref[0]).astype(BF16)


def _experts(block_expert, n_valid, xs, wg, wu, wd):
    n_blocks = xs.shape[0] // RB
    rows = pl.BlockSpec((RB, D_MODEL), lambda i, be, nv: (jnp.minimum(i, nv[0] - 1), 0))
    return pl.pallas_call(
        _experts_kernel,
        name="experts",
        grid_spec=pltpu.PrefetchScalarGridSpec(
            num_scalar_prefetch=2,
            grid=(n_blocks,),
            in_specs=[
                rows,
                pl.BlockSpec((1, D_MODEL, D_EXPERT), lambda i, be, nv: (be[i], 0, 0)),
                pl.BlockSpec((1, D_MODEL, D_EXPERT), lambda i, be, nv: (be[i], 0, 0)),
                pl.BlockSpec((1, D_EXPERT, D_MODEL), lambda i, be, nv: (be[i], 0, 0)),
            ],
            out_specs=rows,
        ),
        out_shape=jax.ShapeDtypeStruct(xs.shape, BF16),
        input_output_aliases={2: 0},
        compiler_params=pltpu.CompilerParams(
            dimension_semantics=("arbitrary",), vmem_limit_bytes=VMEM_LIMIT),
    )(block_expert, n_valid, xs, wg, wu, wd)


def _combine_kernel(copies_ref, counts_ref,
                    infok_ref, h2_ref, x1_ref, gate_ref, gpost_ref, wsg_ref, wsu_ref, wsd_ref,
                    ys_ref, o_ref, ybuf, acc_ref, sem):
    tile = pl.program_id(0)
    last = pl.num_programs(0) - 1
    slot = tile % 2

    def start(t, s):
        _for_each_copy(
            copies_ref, counts_ref, t,
            lambda lp, gp, n: _piece_copy(ys_ref, gp, ybuf.at[s], lp, sem.at[s], n).start())

    def wait(t, s):
        _wait_pieces(_tile_pieces(counts_ref, t),
                     lambda n: _piece_copy(ys_ref, 0, ybuf.at[s], 0, sem.at[s], n))

    @pl.when(tile == 0)
    def _():
        ybuf[...] = jnp.zeros_like(ybuf)
        start(tile, slot)

    @pl.when(tile < last)
    def _():
        start(tile + 1, 1 - slot)

    wait(tile, slot)

    weights = _onehot_chunks(infok_ref, weighted=True)
    hs = _swiglu(h2_ref[...], wsg_ref[...], wsu_ref[...])
    ff = _dot(hs.astype(BF16), wsd_ref[...])
    used_rows = PIECE * _tile_pieces(counts_ref, tile)

    def sum_chunk(j):
        rows = ybuf[slot, j * LCHUNK // PIECE:(j + 1) * LCHUNK // PIECE].reshape(LCHUNK, D_MODEL)
        return lax.dot_general(weights(j), rows,
                               (((0,), (0,)), ((), ())), preferred_element_type=F32)

    def add_chunk(j):
        acc_ref[...] += sum_chunk(j)

    n_chunks = LROWS // LCHUNK
    for j in range(n_chunks - 1):
        ff = ff + sum_chunk(j)
    acc_ref[...] = ff
    pl.when(used_rows > (n_chunks - 1) * LCHUNK)(functools.partial(add_chunk, n_chunks - 1))
    o_ref[...] = x1_ref[...] + gate_ref[0] * _rms(acc_ref[...], gpost_ref[...])


def _combine(tables, infok, h2, x1, gate, gpost, wsg, wsu, wsd, ys, seq):
    t = h2.shape[0]
    per_seq = seq // TW
    full = lambda shape: pl.BlockSpec(shape, lambda i, *_: (0,) * len(shape))
    tok = lambda width: pl.BlockSpec((TW, width), lambda i, *_: (i, 0))
    return pl.pallas_call(
        _combine_kernel,
        name="combine",
        grid_spec=pltpu.PrefetchScalarGridSpec(
            num_scalar_prefetch=2,
            grid=(t // TW,),
            in_specs=[
                pl.BlockSpec((2 * TOP_K, TW), lambda i, *_: (0, i)), tok(D_MODEL), tok(D_MODEL),
                pl.BlockSpec((1, 1, D_MODEL), lambda i, *_: (i // per_seq, 0, 0)),
                full((1, D_MODEL)),
                full((D_MODEL, D_SHARED)), full((D_MODEL, D_SHARED)), full((D_SHARED, D_MODEL)),
                pl.BlockSpec(memory_space=pl.ANY),
            ],
            out_specs=tok(D_MODEL),
            scratch_shapes=[
                pltpu.VMEM((2, LROWS // PIECE, PIECE, D_MODEL), BF16),
                pltpu.VMEM((TW, D_MODEL), F32),
                pltpu.SemaphoreType.DMA((2,)),
            ],
        ),
        out_shape=jax.ShapeDtypeStruct((t, D_MODEL), F32),
        compiler_params=pltpu.CompilerParams(
            dimension_semantics=("arbitrary",), vmem_limit_bytes=VMEM_LIMIT),
    )(*tables, infok, h2, x1, gate, gpost, wsg, wsu, wsd, ys)


def _row_buffer_blocks(n_tiles):
    return (TOP_K * TW * n_tiles + N_EXPERTS * n_tiles * (PIECE - 1)
            + N_EXPERTS * (RB - PIECE) + RB - 1) // RB


def _dispatch_plan(cnt, n_tiles):
    n = cnt[:, :n_tiles].astype(jnp.int32)
    pieces = (n + PIECE - 1) // PIECE
    local = jnp.cumsum(pieces, axis=0) - pieces
    seg = jnp.sum(pieces, axis=1)
    per_block = RB // PIECE
    seg_pad = (seg + per_block - 1) // per_block * per_block
    seg_end = jnp.cumsum(seg_pad)
    seg_start = seg_end - seg_pad
    glob = seg_start[:, None] + jnp.cumsum(pieces, axis=1) - pieces
    n_blocks = _row_buffer_blocks(n_tiles)
    n_valid = seg_end[-1] // per_block
    blk = jnp.minimum(jnp.arange(n_blocks, dtype=jnp.int32), n_valid - 1)
    block_end = seg_end // per_block
    block_expert = jnp.sum((block_end[None, :] <= blk[:, None]).astype(jnp.int32), axis=1)
    block_expert = jnp.minimum(block_expert, N_EXPERTS - 1)
    assert LROWS // PIECE <= 1 << LOCAL_BITS and n_blocks * per_block < 1 << (31 - LOCAL_BITS)

    def copy_list(count, width):
        end = jnp.cumsum(count, axis=0).T
        slot = jnp.arange(width, dtype=jnp.int32)
        expert = jnp.sum((end[:, None, :] <= slot[None, :, None]).astype(jnp.int32), axis=-1)
        pick = expert[:, :, None] == jnp.arange(N_EXPERTS, dtype=jnp.int32)
        take = lambda a: jnp.sum(jnp.where(pick, a.T[:, None, :], 0), axis=-1)
        return slot[None, :] - take(jnp.cumsum(count, axis=0) - count), take

    big = COPY_SIZES[0]
    assert COPY_SIZES == tuple(range(big, 0, -1))
    n_big = pieces // big
    rest = pieces % big
    packed = local | (glob << LOCAL_BITS)
    step = 1 + (1 << LOCAL_BITS)
    within, take = copy_list(n_big, MAX_COPIES)
    lists = [take(packed) + big * step * within]
    counts = [jnp.sum(n_big, axis=0)]
    rest_start = packed + big * step * n_big
    for size in COPY_SIZES[1:]:
        has = (rest == size).astype(jnp.int32)
        _, take = copy_list(has, MAX_COPIES)
        lists.append(take(rest_start))
        counts.append(jnp.sum(has, axis=0))
    counts.append(jnp.sum(pieces, axis=0))
    tables = (jnp.stack(lists, axis=1).reshape(-1).astype(jnp.int32),
              jnp.stack(counts, axis=1).reshape(-1).astype(jnp.int32))
    tails = ((seg_start + seg).astype(jnp.int32), (seg_pad - seg).astype(jnp.int32))
    return tables, tails, block_expert, n_valid.reshape(1).astype(jnp.int32), n_blocks * RB


def kernel(x, c, w_ada, b_ada, g_pre_mix, g_post_mix, g_pre_ffn, g_post_ffn, w_in, b_forget,
           w_pool, pool_scale, g_pool_out, g_attn_out, w_out, w_router, router_bias,
           w_gate, w_up, w_down, ws_gate, ws_up, ws_down):
    bsz, seq, d = x.shape
    depth = w_ada.shape[0]
    for l in range(depth):
        mod = _ada(c, w_ada, b_ada[l][None, :], l)
        shift_m, scale_m, gate_m, shift_f, scale_f, gate_f = [
            m.reshape(bsz, 1, d) for m in jnp.split(mod, 6, axis=-1)]

        bfp = jnp.pad(b_forget[l], (0, LANES - N_HEADS))[None, :]
        pool, qa, kt, v = _premix(
            x, shift_m, scale_m, g_pre_mix[l][None, :], w_in, l, bfp, w_pool[l].astype(BF16),
            pool_scale[l][None, :], g_pool_out[l][None, :])
        ot, wgb, wub, wdb = _attn(qa, kt, v, w_gate[l], w_up[l], w_down[l])
        t = bsz * seq
        assert t // TW <= LANES

        wr = jnp.pad(w_router[l], ((0, 0), (0, LANES - N_EXPERTS)))
        wrh = wr.astype(BF16)
        wrl = (wr - wrh.astype(F32)).astype(BF16)
        x1, h2, infok, cnt = _postmix(
            x, pool, ot, w_out[l].astype(BF16), g_attn_out[l][None, :], g_post_mix[l][None, :],
            gate_m, g_pre_ffn[l][None, :], shift_f, scale_f,
            jnp.concatenate([wrh, wrl], axis=1), router_bias[l][:, None])
        tables, tails, block_expert, n_valid, n_rows = _dispatch_plan(cnt, t // TW)
        h2f = h2.reshape(t, d)
        xs = _dispatch(tables + tails + (n_valid,), h2f, infok, n_rows)
        ys = _experts(block_expert, n_valid, xs.reshape(n_rows, d), wgb, wub, wdb)
        ys = ys.reshape(xs.shape)
        out = _combine(tables, infok, h2f, x1.reshape(t, d), gate_f, g_post_ffn[l][None, :],
                       ws_gate[l].astype(BF16), ws_up[l].astype(BF16), ws_down[l].astype(BF16),
                       ys, seq)
        x = out.reshape(bsz, seq, d)
    return x
```

```python
import functools

import numpy as np
import jax
import jax.numpy as jnp
from jax import lax
from jax.experimental import pallas as pl
from jax.experimental.pallas import tpu as pltpu

D_MODEL = 1024
D_POOL = 512
POOL_WINDOWS = (2, 4, 8, 16)
POOL_GROUP = 128
MAX_WINDOW = max(POOL_WINDOWS)
D_ATTN = 512
HEAD_DIM = 64
N_HEADS = 8
N_EXPERTS = 64
N_EXPERT_GROUPS = 8
GROUP_SIZE = N_EXPERTS // N_EXPERT_GROUPS
TOPK_GROUPS = 4
TOP_K = 8
D_EXPERT = 256
D_SHARED = 256
ROUTED_SCALE = 2.5
EPS = 1e-6

LOG2E = 1.4426950408889634
LANES = 128
N_SPLIT = 3
AUG = LANES

TS_PRE = 1024
CUM_BLOCK = 256
TS_POST = 1024
TQ = 512
TK = 512
HEAD_PAIR = LANES // HEAD_DIM
assert TQ == TK
TW = 256
PIECE = 16
RB = 1024
RSUB = 512
LCHUNK = 512
ONEHOT_GROUP = 64
LROWS = -(-(TOP_K * TW + N_EXPERTS * (PIECE - 1)) // LCHUNK) * LCHUNK

F32 = jnp.float32
BF16 = jnp.bfloat16
V7X_VMEM_BYTES = 64 * 1024 * 1024
VMEM_LIMIT = V7X_VMEM_BYTES * 7 // 8


def _rms(v, g):
    return v * lax.rsqrt(jnp.mean(v * v, axis=-1, keepdims=True) + EPS) * g


def _split3(v):
    hi = v.astype(BF16)
    r1 = v - hi.astype(F32)
    mid = r1.astype(BF16)
    r2 = r1 - mid.astype(F32)
    lo = r2.astype(BF16)
    return hi, mid, lo


def _dot(a, b):
    return jnp.dot(a, b, preferred_element_type=F32)


def _ada_kernel(c_ref, w_ref, b_ref, o_ref):
    o_ref[...] = _dot(c_ref[...].astype(BF16), w_ref[0].astype(BF16)) + b_ref[...]


def _ada(c, w, b, layer):
    bsz = c.shape[0]
    n = w.shape[2]
    return pl.pallas_call(
        _ada_kernel,
        name="ada",
        grid=(n // D_MODEL,),
        in_specs=[
            pl.BlockSpec((bsz, D_MODEL), lambda j: (0, 0)),
            pl.BlockSpec((1, D_MODEL, D_MODEL), lambda j: (layer, 0, j)),
            pl.BlockSpec((1, D_MODEL), lambda j: (0, j)),
        ],
        out_specs=pl.BlockSpec((bsz, D_MODEL), lambda j: (0, j)),
        out_shape=jax.ShapeDtypeStruct((bsz, n), F32),
    )(c, w, b)


def _premix_kernel(x_ref, shift_ref, scale_ref, g_ref, win_ref, bf_ref, wpool_ref,
                   pscale_ref, gpool_ref, place_ref, ones_ref, fmask_ref,
                   pool_ref, q_ref, kt_ref, v_ref,
                   uext_ref, cum_ref, w1_ref, wf_ref):
    s = pl.program_id(1)
    ts = x_ref.shape[1]

    @pl.when((pl.program_id(0) == 0) & (s == 0))
    def _():
        q0, k0 = D_POOL, D_POOL + D_ATTN
        w1_ref[:, :q0] = win_ref[0, :, :q0].astype(BF16)
        w1_ref[:, q0:k0] = (win_ref[0, :, q0:k0] * (HEAD_DIM ** -0.5 * LOG2E)).astype(BF16)
        w1_ref[:, k0:] = win_ref[0, :, k0:k0 + 2 * D_ATTN].astype(BF16)
        wf_ref[...] = jnp.zeros_like(wf_ref)
        wf_ref[:, :N_HEADS] = win_ref[0, :, k0 + 2 * D_ATTN:].astype(BF16)

    @pl.when(s == 0)
    def _():
        uext_ref[0:MAX_WINDOW, :] = jnp.zeros((MAX_WINDOW, D_POOL), F32)
        cum_ref[...] = jnp.zeros_like(cum_ref)

    x = x_ref[0]
    h = _rms(x, g_ref[...]) * (1.0 + scale_ref[0]) + shift_ref[0]
    hb = h.astype(BF16)
    proj = _dot(hb, w1_ref[...])
    u = proj[:, :D_POOL]
    q = proj[:, D_POOL:D_POOL + D_ATTN]
    k = proj[:, D_POOL + D_ATTN:D_POOL + 2 * D_ATTN]
    v = proj[:, D_POOL + 2 * D_ATTN:]

    uext_ref[MAX_WINDOW:, :] = u
    pos = (s * ts + lax.broadcasted_iota(jnp.int32, (ts, 1), 0) + 1).astype(F32)
    ys = []
    for g, w in enumerate(POOL_WINDOWS):
        c0 = g * POOL_GROUP
        acc = uext_ref[MAX_WINDOW:, c0:c0 + POOL_GROUP]
        for j in range(1, w):
            acc = acc + uext_ref[MAX_WINDOW - j:MAX_WINDOW - j + ts, c0:c0 + POOL_GROUP]
        pooled = acc / jnp.minimum(pos, float(w)) - u[:, c0:c0 + POOL_GROUP]
        ys.append(_dot(pooled.astype(BF16), wpool_ref[g]))
    ypool = jnp.concatenate(ys, axis=1) * pscale_ref[...]
    pool_ref[0] = _rms(ypool, gpool_ref[...]).astype(BF16)
    uext_ref[0:MAX_WINDOW, :] = uext_ref[ts:ts + MAX_WINDOW, :]

    z0 = _dot(hb, wf_ref[...]) + bf_ref[...]
    z = z0
    for p in range(1, N_SPLIT):
        z = z + pltpu.roll(z0, p * N_HEADS, 1)
    logf = jnp.minimum(z, 0.0) - jnp.log1p(jnp.exp(-jnp.abs(z)))
    row = lax.broadcasted_iota(jnp.int32, (CUM_BLOCK, CUM_BLOCK), 0)
    col = lax.broadcasted_iota(jnp.int32, (CUM_BLOCK, CUM_BLOCK), 1)
    tri = (col <= row).astype(BF16)
    pieces = _split3(logf)
    carry = cum_ref[...]
    blocks = []
    for r0 in range(0, ts, CUM_BLOCK):
        blk = carry
        for piece in pieces:
            blk = blk + _dot(tri, piece[r0:r0 + CUM_BLOCK, :])
        carry = blk[CUM_BLOCK - 1:CUM_BLOCK, :]
        blocks.append(blk)
    cum = jnp.concatenate(blocks, axis=0)
    cum_ref[...] = carry

    hi, mid, lo = [piece.astype(F32) for piece in _split3(cum * LOG2E)]
    lane = lax.broadcasted_iota(jnp.int32, (ts, LANES), 1)
    pieces = jnp.where(lane < N_HEADS, hi, jnp.where(lane < 2 * N_HEADS, mid, lo))
    aug = _dot(pieces.astype(BF16), place_ref[...]) + ones_ref[...]
    aug_q = aug[:, :N_HEADS * AUG]
    aug_k = aug[:, N_HEADS * AUG:]

    def expand(a):
        blocks = []
        for j in range(D_ATTN // LANES):
            blk = a[:, j * LANES:(j + 1) * LANES]
            blocks += [blk, blk]
        return jnp.concatenate(blocks, axis=1)

    fmask = fmask_ref[...]
    qa = expand(q) * fmask + aug_q
    ka = expand(k) * fmask + aug_k
    q_ref[0] = qa.astype(BF16)
    kt_ref[0] = ka.T.astype(BF16)
    v_ref[0] = v.astype(BF16)


def _aug_constants():
    width = N_HEADS * AUG
    place = np.zeros((LANES, 2 * width), np.float32)
    ones = np.zeros((1, 2 * width), np.float32)
    fmask = np.zeros((1, width), np.float32)
    for h in range(N_HEADS):
        feat0 = h * AUG + (0 if h % 2 == 0 else HEAD_DIM)
        aug0 = h * AUG + (HEAD_DIM if h % 2 == 0 else 0)
        fmask[0, feat0:feat0 + HEAD_DIM] = 1.0
        for p in range(N_SPLIT):
            place[p * N_HEADS + h, aug0 + p] = 1.0
            ones[0, width + aug0 + p] = 1.0
            place[p * N_HEADS + h, width + aug0 + N_SPLIT + p] = -1.0
            ones[0, aug0 + N_SPLIT + p] = 1.0
    return jnp.asarray(place, BF16), jnp.asarray(ones), jnp.asarray(fmask)


def _premix(x, shift, scale, g, w_in, layer, bfp, wpool, pscale, gpool):
    bsz, seq, _ = x.shape
    d_in = w_in.shape[2]
    assert d_in == D_POOL + 3 * D_ATTN + N_HEADS
    ts = TS_PRE
    place, ones, fmask = _aug_constants()
    full = lambda shape: pl.BlockSpec(shape, lambda b, s: (0,) * len(shape))
    per_batch = pl.BlockSpec((1, 1, D_MODEL), lambda b, s: (b, 0, 0))
    return pl.pallas_call(
        _premix_kernel,
        name="premix",
        grid=(bsz, seq // ts),
        in_specs=[
            pl.BlockSpec((1, ts, D_MODEL), lambda b, s: (b, s, 0)),
            per_batch, per_batch,
            full((1, D_MODEL)),
            pl.BlockSpec((1, D_MODEL, d_in), lambda b, s: (layer, 0, 0),
                         pipeline_mode=pl.Buffered(1)),
            full((1, LANES)),
            full((len(POOL_WINDOWS), POOL_GROUP, POOL_GROUP)),
            full((1, D_POOL)),
            full((1, D_POOL)),
            full((LANES, 2 * N_HEADS * AUG)),
            full((1, 2 * N_HEADS * AUG)),
            full((1, N_HEADS * AUG)),
        ],
        out_specs=[
            pl.BlockSpec((1, ts, D_POOL), lambda b, s: (b, s, 0)),
            pl.BlockSpec((1, ts, N_HEADS * AUG), lambda b, s: (b, s, 0)),
            pl.BlockSpec((1, N_HEADS * AUG, ts), lambda b, s: (b, 0, s)),
            pl.BlockSpec((1, ts, D_ATTN), lambda b, s: (b, s, 0)),
        ],
        out_shape=[
            jax.ShapeDtypeStruct((bsz, seq, D_POOL), BF16),
            jax.ShapeDtypeStruct((bsz, seq, N_HEADS * AUG), BF16),
            jax.ShapeDtypeStruct((bsz, N_HEADS * AUG, seq), BF16),
            jax.ShapeDtypeStruct((bsz, seq, D_ATTN), BF16),
        ],
        scratch_shapes=[
            pltpu.VMEM((ts + MAX_WINDOW, D_POOL), F32),
            pltpu.VMEM((1, LANES), F32),
            pltpu.VMEM((D_MODEL, D_POOL + 3 * D_ATTN), BF16),
            pltpu.VMEM((D_MODEL, LANES), BF16),
        ],
        compiler_params=pltpu.CompilerParams(
            dimension_semantics=("arbitrary", "arbitrary"), vmem_limit_bytes=VMEM_LIMIT),
    )(x, shift, scale, g, w_in, bfp, wpool, pscale, gpool, place, ones, fmask)


def _attn_kernel(q_ref, kt_ref, v_ref, wg_ref, wu_ref, wd_ref, o_ref, wgb_ref, wub_ref, wdb_ref):
    wgb_ref[...] = wg_ref[...].astype(BF16)
    wub_ref[...] = wu_ref[...].astype(BF16)
    wdb_ref[...] = wd_ref[...].astype(BF16)
    seq = q_ref.shape[1]
    qry_i = lax.broadcasted_iota(jnp.int32, (TQ, TK), 0)
    key_i = lax.broadcasted_iota(jnp.int32, (TQ, TK), 1)
    causal = key_i <= qry_i
    one_lane = lax.broadcasted_iota(jnp.int32, (seq, LANES), 1) == 0
    vaug = jnp.concatenate([v_ref[0], jnp.where(one_lane, 1.0, 0.0).astype(BF16)], axis=1)
    out_lane = lax.broadcasted_iota(jnp.int32, (TQ, LANES), 1)
    heads = range(HEAD_PAIR)
    for qi in range(seq // TQ):
        rows = slice(qi * TQ, (qi + 1) * TQ)
        q = [q_ref[0, rows, h * AUG:(h + 1) * AUG] for h in heads]
        m = [jnp.full((TQ, 1), -jnp.inf, F32) for _ in heads]
        acc = [jnp.zeros((TQ, 2 * LANES), F32) for _ in heads]
        for kj in range(qi + 1):
            keys = slice(kj * TK, (kj + 1) * TK)
            for h in heads:
                s = _dot(q[h], kt_ref[0, h * AUG:(h + 1) * AUG, keys])
                if kj == qi:
                    s = jnp.where(causal, s, -jnp.inf)
                m_new = jnp.maximum(m[h], jnp.max(s, axis=1, keepdims=True))
                p = jnp.exp2(s - m_new).astype(BF16)
                acc[h] = jnp.exp2(m[h] - m_new) * acc[h] + _dot(p, vaug[keys, :])
                m[h] = m_new
        o = [acc[h][:, :LANES] / acc[h][:, LANES:LANES + 1] for h in heads]
        o_ref[0, rows, :] = jnp.where(out_lane < HEAD_DIM, o[0], o[1])


def _attn(q, kt, v, wg, wu, wd):
    bsz, seq, _ = q.shape
    pairs = N_HEADS // HEAD_PAIR
    per_step = N_EXPERTS // (bsz * pairs)
    assert per_step * bsz * pairs == N_EXPERTS
    wspec = lambda shape: pl.BlockSpec((per_step,) + shape, lambda b, h: (b * pairs + h, 0, 0))
    return pl.pallas_call(
        _attn_kernel,
        name="attn",
        grid=(bsz, pairs),
        in_specs=[
            pl.BlockSpec((1, seq, HEAD_PAIR * AUG), lambda b, h: (b, 0, h)),
            pl.BlockSpec((1, HEAD_PAIR * AUG, seq), lambda b, h: (b, h, 0)),
            pl.BlockSpec((1, seq, LANES), lambda b, h: (b, 0, h)),
            wspec((D_MODEL, D_EXPERT)), wspec((D_MODEL, D_EXPERT)), wspec((D_EXPERT, D_MODEL)),
        ],
        out_specs=[
            pl.BlockSpec((1, seq, LANES), lambda b, h: (b, 0, h)),
            wspec((D_MODEL, D_EXPERT)), wspec((D_MODEL, D_EXPERT)), wspec((D_EXPERT, D_MODEL)),
        ],
        out_shape=[
            jax.ShapeDtypeStruct((bsz, seq, D_ATTN), F32),
            jax.ShapeDtypeStruct(wg.shape, BF16),
            jax.ShapeDtypeStruct(wu.shape, BF16),
            jax.ShapeDtypeStruct(wd.shape, BF16),
        ],
        compiler_params=pltpu.CompilerParams(
            dimension_semantics=("arbitrary", "arbitrary"), vmem_limit_bytes=VMEM_LIMIT),
    )(q, kt, v, wg, wu, wd)


def _postmix_kernel(x_ref, pool_ref, o_ref, wout_ref, gattn_ref, gpost_ref, gate_ref,
                    gffn_ref, shift_ref, scale_ref, wr_ref, bias_ref,
                    x1_ref, h2_ref, infok_ref, cnt_ref):
    ya = _rms(o_ref[0], gattn_ref[...]).astype(BF16)
    mixed = _dot(pool_ref[0], wout_ref[:D_POOL, :]) + _dot(ya, wout_ref[D_POOL:, :])
    x1 = x_ref[0] + gate_ref[0] * _rms(mixed, gpost_ref[...])
    x1_ref[0] = x1
    h2 = _rms(x1, gffn_ref[...]) * (1.0 + scale_ref[0]) + shift_ref[0]
    h2b = h2.astype(BF16)
    h2_ref[0] = h2b
    h2l = (h2 - h2b.astype(F32)).astype(BF16)
    both = _dot(h2b, wr_ref[...])
    logits = both[:, :LANES] + both[:, LANES:] + _dot(h2l, wr_ref[:, :LANES])
    st = jax.nn.sigmoid(logits).T

    step = pl.program_id(0) * pl.num_programs(1) + pl.program_id(1)

    @pl.when(step == 0)
    def _():
        cnt_ref[...] = jnp.zeros_like(cnt_ref)

    lane = lax.broadcasted_iota(jnp.int32, cnt_ref.shape, 1)
    tiles = x_ref.shape[1] // TW
    for i in range(tiles):
        info, n = _route_tile(st[:, i * TW:(i + 1) * TW], bias_ref)
        infok_ref[:, i * TW:(i + 1) * TW] = info
        cnt_ref[...] = jnp.where(lane == step * tiles + i, jnp.broadcast_to(n, cnt_ref.shape),
                                 cnt_ref[...])


def _postmix(x, pool, ot, wout, gattn, gpost, gate, gffn, shift, scale, wr, bias_col):
    bsz, seq, _ = x.shape
    ts = TS_POST
    ns = seq // ts
    full = lambda shape: pl.BlockSpec(shape, lambda b, s: (0,) * len(shape))
    per_batch = pl.BlockSpec((1, 1, D_MODEL), lambda b, s: (b, 0, 0))
    tok = lambda width: pl.BlockSpec((1, ts, width), lambda b, s: (b, s, 0))
    return pl.pallas_call(
        _postmix_kernel,
        name="postmix",
        grid=(bsz, ns),
        in_specs=[
            tok(D_MODEL), tok(D_POOL), tok(D_ATTN),
            full((D_MODEL, D_MODEL)),
            full((1, D_ATTN)), full((1, D_MODEL)), per_batch,
            full((1, D_MODEL)), per_batch, per_batch,
            full((D_MODEL, 2 * LANES)),
            full((N_EXPERTS, 1)),
        ],
        out_specs=[
            tok(D_MODEL), tok(D_MODEL),
            pl.BlockSpec((2 * TOP_K, ts), lambda b, s: (0, b * ns + s)),
            full((N_EXPERTS, LANES)),
        ],
        out_shape=[
            jax.ShapeDtypeStruct((bsz, seq, D_MODEL), F32),
            jax.ShapeDtypeStruct((bsz, seq, D_MODEL), BF16),
            jax.ShapeDtypeStruct((2 * TOP_K, bsz * seq), F32),
            jax.ShapeDtypeStruct((N_EXPERTS, LANES), F32),
        ],
        compiler_params=pltpu.CompilerParams(
            dimension_semantics=("arbitrary", "arbitrary"), vmem_limit_bytes=VMEM_LIMIT),
    )(x, pool, ot, wout, gattn, gpost, gate, gffn, shift, scale, wr, bias_col)


def _route_tile(st, bias_ref):
    tr = st.shape[1]
    scores = [st[g * GROUP_SIZE:(g + 1) * GROUP_SIZE, :] for g in range(N_EXPERT_GROUPS)]
    sel = [scores[g] + bias_ref[g * GROUP_SIZE:(g + 1) * GROUP_SIZE, :]
           for g in range(N_EXPERT_GROUPS)]
    sub = lax.broadcasted_iota(jnp.int32, (GROUP_SIZE, tr), 0)
    neg = jnp.float32(-jnp.inf)

    rows = []
    for g in range(N_EXPERT_GROUPS):
        v = sel[g]
        m1 = jnp.max(v, axis=0, keepdims=True)
        first = jnp.min(jnp.where(v == m1, sub, GROUP_SIZE), axis=0, keepdims=True)
        m2 = jnp.max(jnp.where(sub == first, neg, v), axis=0, keepdims=True)
        rows.append(m1 + m2)
    gs = jnp.concatenate(rows, axis=0)

    shape = (GROUP_SIZE, tr)
    one = jnp.ones(shape, jnp.int32)
    zero = jnp.zeros(shape, jnp.int32)
    beaten = zero
    for gp in range(N_EXPERT_GROUPS):
        r = jnp.broadcast_to(gs[gp:gp + 1, :], shape)
        tie = jnp.where(sub > gp, one, zero)
        beaten = beaten + jnp.where(r > gs, one, zero) + jnp.where(r == gs, tie, zero)
    keep = jnp.where(beaten < TOPK_GROUPS, 1.0, 0.0)

    masked = [jnp.where(jnp.broadcast_to(keep[g:g + 1, :], shape) > 0.5, sel[g], neg)
              for g in range(N_EXPERT_GROUPS)]

    eidx = [sub + g * GROUP_SIZE for g in range(N_EXPERT_GROUPS)]
    picked = [zero for _ in range(N_EXPERT_GROUPS)]
    for _ in range(TOP_K):
        best = masked[0]
        for g in range(1, N_EXPERT_GROUPS):
            best = jnp.maximum(best, masked[g])
        best = jnp.broadcast_to(jnp.max(best, axis=0, keepdims=True), shape)
        first = jnp.where(masked[0] == best, eidx[0], N_EXPERTS)
        for g in range(1, N_EXPERT_GROUPS):
            first = jnp.minimum(first, jnp.where(masked[g] == best, eidx[g], N_EXPERTS))
        first = jnp.broadcast_to(jnp.min(first, axis=0, keepdims=True), shape)
        for g in range(N_EXPERT_GROUPS):
            hit = eidx[g] == first
            picked[g] = jnp.where(hit, one, picked[g])
            masked[g] = jnp.where(hit, neg, masked[g])
    chosen = [picked[g] > 0 for g in range(N_EXPERT_GROUPS)]

    w = [jnp.where(chosen[g], scores[g], 0.0) for g in range(N_EXPERT_GROUPS)]
    denom = w[0].sum(axis=0, keepdims=True)
    for g in range(1, N_EXPERT_GROUPS):
        denom = denom + w[g].sum(axis=0, keepdims=True)
    wfull = jnp.concatenate([w[g] / denom * ROUTED_SCALE for g in range(N_EXPERT_GROUPS)], axis=0)
    mfull = jnp.concatenate([jnp.where(chosen[g], 1.0, 0.0) for g in range(N_EXPERT_GROUPS)],
                            axis=0)
    mb = mfull.astype(BF16)

    e_r = lax.broadcasted_iota(jnp.int32, (N_EXPERTS, N_EXPERTS), 0)
    e_c = lax.broadcasted_iota(jnp.int32, (N_EXPERTS, N_EXPERTS), 1)
    before_e = (e_c < e_r).astype(BF16)
    t_r = lax.broadcasted_iota(jnp.int32, (tr, tr), 0)
    t_c = lax.broadcasted_iota(jnp.int32, (tr, tr), 1)
    before_t = (t_r < t_c).astype(BF16)
    ordinal = _dot(before_e, mb)
    rank = _dot(mb, before_t)
    n = jnp.sum(mfull, axis=1, keepdims=True)
    pieces = jnp.floor((n + (PIECE - 1.0)) * (1.0 / PIECE))
    run_start = PIECE * _dot(before_e, jnp.broadcast_to(pieces, (N_EXPERTS, LANES)).astype(BF16))
    pos = run_start[:, 0:1] + rank

    sub8 = lax.broadcasted_iota(jnp.int32, (TOP_K, tr), 0)
    pos8 = jnp.zeros((TOP_K, tr), F32)
    w8 = jnp.zeros((TOP_K, tr), F32)
    for k in range(TOP_K):
        selk = jnp.where(ordinal == float(k), mfull, 0.0)
        pk = jnp.sum(selk * pos, axis=0, keepdims=True)
        wk = jnp.sum(selk * wfull, axis=0, keepdims=True)
        pos8 = jnp.where(sub8 == k, jnp.broadcast_to(pk, (TOP_K, tr)), pos8)
        w8 = jnp.where(sub8 == k, jnp.broadcast_to(wk, (TOP_K, tr)), w8)
    return jnp.concatenate([pos8, w8], axis=0), n


def _swiglu(xb, wg, wu):
    g = _dot(xb, wg)
    return (g * jax.nn.sigmoid(g)) * _dot(xb, wu)


def _piece_copy(src_ref, src_piece, dst_ref, dst_piece, sem, n_pieces=1):
    src = src_ref.at[pl.ds(src_piece, n_pieces)]
    dst = dst_ref.at[pl.ds(dst_piece, n_pieces)]
    return pltpu.make_async_copy(src, dst, sem)


def _as_pieces(rows):
    return rows.reshape(rows.shape[0] // PIECE, PIECE, rows.shape[1])


LOCAL_BITS = 8
COPY_SIZES = (4, 3, 2, 1)
MAX_COPIES = N_EXPERTS
assert LROWS // (COPY_SIZES[0] * PIECE) <= MAX_COPIES


def _for_each_copy(copies_ref, counts_ref, tile, fn):
    for c, size in enumerate(COPY_SIZES):
        base = (tile * len(COPY_SIZES) + c) * MAX_COPIES

        def body(i, carry, base=base, size=size):
            word = copies_ref[base + i]
            fn(word & ((1 << LOCAL_BITS) - 1), lax.shift_right_logical(word, LOCAL_BITS), size)
            return carry

        lax.fori_loop(0, counts_ref[tile * (len(COPY_SIZES) + 1) + c], body, 0)


def _tile_pieces(counts_ref, tile):
    return counts_ref[tile * (len(COPY_SIZES) + 1) + len(COPY_SIZES)]


def _wait_pieces(total, copy_of):
    for bit in range((LROWS // PIECE).bit_length()):
        @pl.when((lax.shift_right_logical(total, bit) & 1) == 1)
        def _():
            copy_of(1 << bit).wait()


def _onehot_chunks(infok_ref, weighted):
    pos = infok_ref[0:TOP_K, :]
    grp = jnp.floor(pos * (1.0 / ONEHOT_GROUP))
    off = pos - ONEHOT_GROUP * grp
    sub = lax.broadcasted_iota(jnp.int32, (ONEHOT_GROUP, TW), 0).astype(F32)
    offs, grps = [], []
    for k in range(TOP_K):
        value = infok_ref[TOP_K + k:TOP_K + k + 1, :] if weighted else 1.0
        offs.append(jnp.where(sub == off[k:k + 1, :], value, 0.0).astype(BF16))
        grps.append(jnp.broadcast_to(grp[k:k + 1, :], (PIECE, TW)).astype(BF16))
    zero = jnp.zeros((PIECE, TW), BF16)

    def chunk(j):
        tiles = []
        for a in range(j * LCHUNK // ONEHOT_GROUP, (j + 1) * LCHUNK // ONEHOT_GROUP):
            hit = [grps[k] == a for k in range(TOP_K)]
            for i in range(ONEHOT_GROUP // PIECE):
                rows = zero
                for k in range(TOP_K):
                    rows = rows + jnp.where(hit[k], offs[k][i * PIECE:(i + 1) * PIECE, :], zero)
                tiles.append(rows)
        return jnp.concatenate(tiles, axis=0)

    return chunk


def _dispatch_kernel(copies_ref, counts_ref, tail_ref, ntail_ref, nv_ref,
                     h2_ref, infok_ref, xs_ref, lbuf, zbuf, sem, zsem):
    tile = pl.program_id(0)
    last = pl.num_programs(0) - 1
    slot = tile % 2
    block_pieces = RB // PIECE
    n_blocks = xs_ref.shape[0] // block_pieces

    def spare_blocks(fn):
        def per_block(b, carry):
            fn(b)
            return carry
        lax.fori_loop(nv_ref[0], n_blocks, per_block, 0)

    def zero_block(b):
        return pltpu.make_async_copy(zbuf, xs_ref.at[pl.ds(b * block_pieces, block_pieces)], zsem)

    def tails(fn):
        def per_expert(e, carry):
            n = ntail_ref[e]
            for bit in range((RB // PIECE - 1).bit_length()):
                @pl.when((lax.shift_right_logical(n, bit) & 1) == 1)
                def _():
                    done = n & ((1 << bit) - 1)
                    fn(_piece_copy(zbuf, 0, xs_ref, tail_ref[e] + done, zsem, 1 << bit))
            return carry
        lax.fori_loop(0, N_EXPERTS, per_expert, 0)

    def start(t, s):
        _for_each_copy(
            copies_ref, counts_ref, t,
            lambda lp, gp, n: _piece_copy(lbuf.at[s], lp, xs_ref, gp, sem.at[s], n).start())

    def wait(t, s):
        _wait_pieces(_tile_pieces(counts_ref, t),
                     lambda n: _piece_copy(lbuf.at[s], 0, xs_ref, 0, sem.at[s], n))

    @pl.when(tile == 0)
    def _():
        zbuf[...] = jnp.zeros_like(zbuf)
        tails(lambda copy: copy.start(priority=1))
        spare_blocks(lambda b: zero_block(b).start(priority=1))

    @pl.when(tile >= 2)
    def _():
        wait(tile - 2, slot)

    onehot = _onehot_chunks(infok_ref, weighted=False)
    xb = h2_ref[...]
    used_rows = PIECE * _tile_pieces(counts_ref, tile)

    def sort_chunk(j):
        lbuf[slot, j * LCHUNK // PIECE:(j + 1) * LCHUNK // PIECE] = _as_pieces(
            _dot(onehot(j), xb).astype(BF16))

    n_chunks = LROWS // LCHUNK
    for j in range(n_chunks - 1):
        sort_chunk(j)
    pl.when(used_rows > (n_chunks - 1) * LCHUNK)(functools.partial(sort_chunk, n_chunks - 1))

    start(tile, slot)

    @pl.when(tile == last)
    def _():
        @pl.when(tile >= 1)
        def _():
            wait(tile - 1, 1 - slot)
        wait(tile, slot)
        tails(lambda copy: copy.wait())
        spare_blocks(lambda b: zero_block(b).wait())


def _dispatch(tables, h2, infok, n_rows):
    t = h2.shape[0]
    return pl.pallas_call(
        _dispatch_kernel,
        name="dispatch",
        grid_spec=pltpu.PrefetchScalarGridSpec(
            num_scalar_prefetch=5,
            grid=(t // TW,),
            in_specs=[
                pl.BlockSpec((TW, D_MODEL), lambda i, *_: (i, 0)),
                pl.BlockSpec((2 * TOP_K, TW), lambda i, *_: (0, i)),
            ],
            out_specs=pl.BlockSpec(memory_space=pl.ANY),
            scratch_shapes=[
                pltpu.VMEM((2, LROWS // PIECE, PIECE, D_MODEL), BF16),
                pltpu.VMEM((RB // PIECE, PIECE, D_MODEL), BF16),
                pltpu.SemaphoreType.DMA((2,)),
                pltpu.SemaphoreType.DMA(()),
            ],
        ),
        out_shape=jax.ShapeDtypeStruct((n_rows // PIECE, PIECE, D_MODEL), BF16),
        compiler_params=pltpu.CompilerParams(
            dimension_semantics=("arbitrary",), vmem_limit_bytes=VMEM_LIMIT),
    )(*tables, h2, infok)


def _experts_kernel(be_ref, nv_ref, xs_ref, wg_ref, wu_ref, wd_ref, ys_ref):
    @pl.when(pl.program_id(0) < nv_ref[0])
    def _():
        for j in range(RB // RSUB):
            rows = slice(j * RSUB, (j + 1) * RSUB)
            h = _swiglu(xs_ref[rows, :], wg_ref[0], wu_ref[0])
            ys_ref[rows, :] = _dot(h.astype(BF16), wd_ref[0]).astype(BF16)


def _experts(block_expert, n_valid, xs, wg, wu, wd):
    n_blocks = xs.shape[0] // RB
    rows = pl.BlockSpec((RB, D_MODEL), lambda i, be, nv: (jnp.minimum(i, nv[0] - 1), 0))
    return pl.pallas_call(
        _experts_kernel,
        name="experts",
        grid_spec=pltpu.PrefetchScalarGridSpec(
            num_scalar_prefetch=2,
            grid=(n_blocks,),
            in_specs=[
                rows,
                pl.BlockSpec((1, D_MODEL, D_EXPERT), lambda i, be, nv: (be[i], 0, 0)),
                pl.BlockSpec((1, D_MODEL, D_EXPERT), lambda i, be, nv: (be[i], 0, 0)),
                pl.BlockSpec((1, D_EXPERT, D_MODEL), lambda i, be, nv: (be[i], 0, 0)),
            ],
            out_specs=rows,
        ),
        out_shape=jax.ShapeDtypeStruct(xs.shape, BF16),
        input_output_aliases={2: 0},
        compiler_params=pltpu.CompilerParams(
            dimension_semantics=("arbitrary",), vmem_limit_bytes=VMEM_LIMIT),
    )(block_expert, n_valid, xs, wg, wu, wd)


def _combine_kernel(copies_ref, counts_ref,
                    infok_ref, h2_ref, x1_ref, gate_ref, gpost_ref, wsg_ref, wsu_ref, wsd_ref,
                    ys_ref, o_ref, ybuf, acc_ref, sem):
    tile = pl.program_id(0)
    last = pl.num_programs(0) - 1
    slot = tile % 2

    def start(t, s):
        _for_each_copy(
            copies_ref, counts_ref, t,
            lambda lp, gp, n: _piece_copy(ys_ref, gp, ybuf.at[s], lp, sem.at[s], n).start(
                priority=1))

    def wait(t, s):
        _wait_pieces(_tile_pieces(counts_ref, t),
                     lambda n: _piece_copy(ys_ref, 0, ybuf.at[s], 0, sem.at[s], n))

    @pl.when(tile == 0)
    def _():
        ybuf[...] = jnp.zeros_like(ybuf)
        start(tile, slot)

    @pl.when(tile < last)
    def _():
        start(tile + 1, 1 - slot)

    wait(tile, slot)

    weights = _onehot_chunks(infok_ref, weighted=True)
    hs = _swiglu(h2_ref[...], wsg_ref[...], wsu_ref[...])
    ff = _dot(hs.astype(BF16), wsd_ref[...])
    used_rows = PIECE * _tile_pieces(counts_ref, tile)

    def sum_chunk(j):
        rows = ybuf[slot, j * LCHUNK // PIECE:(j + 1) * LCHUNK // PIECE].reshape(LCHUNK, D_MODEL)
        return lax.dot_general(weights(j), rows,
                               (((0,), (0,)), ((), ())), preferred_element_type=F32)

    def add_chunk(j):
        acc_ref[...] += sum_chunk(j)

    n_chunks = LROWS // LCHUNK
    for j in range(n_chunks - 1):
        ff = ff + sum_chunk(j)
    acc_ref[...] = ff
    pl.when(used_rows > (n_chunks - 1) * LCHUNK)(functools.partial(add_chunk, n_chunks - 1))
    o_ref[...] = x1_ref[...] + gate_ref[0] * _rms(acc_ref[...], gpost_ref[...])


def _combine(tables, infok, h2, x1, gate, gpost, wsg, wsu, wsd, ys, seq):
    t = h2.shape[0]
    per_seq = seq // TW
    full = lambda shape: pl.BlockSpec(shape, lambda i, *_: (0,) * len(shape))
    tok = lambda width: pl.BlockSpec((TW, width), lambda i, *_: (i, 0))
    return pl.pallas_call(
        _combine_kernel,
        name="combine",
        grid_spec=pltpu.PrefetchScalarGridSpec(
            num_scalar_prefetch=2,
            grid=(t // TW,),
            in_specs=[
                pl.BlockSpec((2 * TOP_K, TW), lambda i, *_: (0, i)), tok(D_MODEL), tok(D_MODEL),
                pl.BlockSpec((1, 1, D_MODEL), lambda i, *_: (i // per_seq, 0, 0)),
                full((1, D_MODEL)),
                full((D_MODEL, D_SHARED)), full((D_MODEL, D_SHARED)), full((D_SHARED, D_MODEL)),
                pl.BlockSpec(memory_space=pl.ANY),
            ],
            out_specs=tok(D_MODEL),
            scratch_shapes=[
                pltpu.VMEM((2, LROWS // PIECE, PIECE, D_MODEL), BF16),
                pltpu.VMEM((TW, D_MODEL), F32),
                pltpu.SemaphoreType.DMA((2,)),
            ],
        ),
        out_shape=jax.ShapeDtypeStruct((t, D_MODEL), F32),
        compiler_params=pltpu.CompilerParams(
            dimension_semantics=("arbitrary",), vmem_limit_bytes=VMEM_LIMIT),
    )(*tables, infok, h2, x1, gate, gpost, wsg, wsu, wsd, ys)


def _row_buffer_blocks(n_tiles):
    return (TOP_K * TW * n_tiles + N_EXPERTS * n_tiles * (PIECE - 1)
            + N_EXPERTS * (RB - PIECE) + RB - 1) // RB


def _dispatch_plan(cnt, n_tiles):
    n = cnt[:, :n_tiles].astype(jnp.int32)
    pieces = (n + PIECE - 1) // PIECE
    local = jnp.cumsum(pieces, axis=0) - pieces
    seg = jnp.sum(pieces, axis=1)
    per_block = RB // PIECE
    seg_pad = (seg + per_block - 1) // per_block * per_block
    seg_end = jnp.cumsum(seg_pad)
    seg_start = seg_end - seg_pad
    glob = seg_start[:, None] + jnp.cumsum(pieces, axis=1) - pieces
    n_blocks = _row_buffer_blocks(n_tiles)
    n_valid = seg_end[-1] // per_block
    blk = jnp.minimum(jnp.arange(n_blocks, dtype=jnp.int32), n_valid - 1)
    block_end = seg_end // per_block
    block_expert = jnp.sum((block_end[None, :] <= blk[:, None]).astype(jnp.int32), axis=1)
    block_expert = jnp.minimum(block_expert, N_EXPERTS - 1)
    assert LROWS // PIECE <= 1 << LOCAL_BITS and n_blocks * per_block < 1 << (31 - LOCAL_BITS)

    def copy_list(count, width):
        end = jnp.cumsum(count, axis=0).T
        slot = jnp.arange(width, dtype=jnp.int32)
        expert = jnp.sum((end[:, None, :] <= slot[None, :, None]).astype(jnp.int32), axis=-1)
        pick = expert[:, :, None] == jnp.arange(N_EXPERTS, dtype=jnp.int32)
        take = lambda a: jnp.sum(jnp.where(pick, a.T[:, None, :], 0), axis=-1)
        return slot[None, :] - take(jnp.cumsum(count, axis=0) - count), take

    big = COPY_SIZES[0]
    assert COPY_SIZES == tuple(range(big, 0, -1))
    n_big = pieces // big
    rest = pieces % big
    packed = local | (glob << LOCAL_BITS)
    step = 1 + (1 << LOCAL_BITS)
    within, take = copy_list(n_big, MAX_COPIES)
    lists = [take(packed) + big * step * within]
    counts = [jnp.sum(n_big, axis=0)]
    rest_start = packed + big * step * n_big
    for size in COPY_SIZES[1:]:
        has = (rest == size).astype(jnp.int32)
        _, take = copy_list(has, MAX_COPIES)
        lists.append(take(rest_start))
        counts.append(jnp.sum(has, axis=0))
    counts.append(jnp.sum(pieces, axis=0))
    tables = (jnp.stack(lists, axis=1).reshape(-1).astype(jnp.int32),
              jnp.stack(counts, axis=1).reshape(-1).astype(jnp.int32))
    tails = ((seg_start + seg).astype(jnp.int32), (seg_pad - seg).astype(jnp.int32))
    return tables, tails, block_expert, n_valid.reshape(1).astype(jnp.int32), n_blocks * RB


def kernel(x, c, w_ada, b_ada, g_pre_mix, g_post_mix, g_pre_ffn, g_post_ffn, w_in, b_forget,
           w_pool, pool_scale, g_pool_out, g_attn_out, w_out, w_router, router_bias,
           w_gate, w_up, w_down, ws_gate, ws_up, ws_down):
    bsz, seq, d = x.shape
    depth = w_ada.shape[0]
    for l in range(depth):
        mod = _ada(c, w_ada, b_ada[l][None, :], l)
        shift_m, scale_m, gate_m, shift_f, scale_f, gate_f = [
            m.reshape(bsz, 1, d) for m in jnp.split(mod, 6, axis=-1)]

        bfp = jnp.pad(b_forget[l], (0, LANES - N_HEADS))[None, :]
        pool, qa, kt, v = _premix(
            x, shift_m, scale_m, g_pre_mix[l][None, :], w_in, l, bfp, w_pool[l].astype(BF16),
            pool_scale[l][None, :], g_pool_out[l][None, :])
        ot, wgb, wub, wdb = _attn(qa, kt, v, w_gate[l], w_up[l], w_down[l])
        t = bsz * seq
        assert t // TW <= LANES

        wr = jnp.pad(w_router[l], ((0, 0), (0, LANES - N_EXPERTS)))
        wrh = wr.astype(BF16)
        wrl = (wr - wrh.astype(F32)).astype(BF16)
        x1, h2, infok, cnt = _postmix(
            x, pool, ot, w_out[l].astype(BF16), g_attn_out[l][None, :], g_post_mix[l][None, :],
            gate_m, g_pre_ffn[l][None, :], shift_f, scale_f,
            jnp.concatenate([wrh, wrl], axis=1), router_bias[l][:, None])
        tables, tails, block_expert, n_valid, n_rows = _dispatch_plan(cnt, t // TW)
        h2f = h2.reshape(t, d)
        xs = _dispatch(tables + tails + (n_valid,), h2f, infok, n_rows)
        ys = _experts(block_expert, n_valid, xs.reshape(n_rows, d), wgb, wub, wdb)
        ys = ys.reshape(xs.shape)
        out = _combine(tables, infok, h2f, x1.reshape(t, d), gate_f, g_post_ffn[l][None, :],
                       ws_gate[l].astype(BF16), ws_up[l].astype(BF16), ws_down[l].astype(BF16),
                       ys, seq)
        x = out.reshape(bsz, seq, d)
    return x
```

```python
import functools

import numpy as np
import jax
import jax.numpy as jnp
from jax import lax
from jax.experimental import pallas as pl
from jax.experimental.pallas import tpu as pltpu

D_MODEL = 1024
D_POOL = 512
POOL_WINDOWS = (2, 4, 8, 16)
POOL_GROUP = 128
MAX_WINDOW = max(POOL_WINDOWS)
D_ATTN = 512
HEAD_DIM = 64
N_HEADS = 8
N_EXPERTS = 64
N_EXPERT_GROUPS = 8
GROUP_SIZE = N_EXPERTS // N_EXPERT_GROUPS
TOPK_GROUPS = 4
TOP_K = 8
D_EXPERT = 256
D_SHARED = 256
ROUTED_SCALE = 2.5
EPS = 1e-6

LOG2E = 1.4426950408889634
LANES = 128
N_SPLIT = 3
AUG = LANES

TS_PRE = 1024
CUM_BLOCK = 256
TS_POST = 1024
TQ = 512
TK = 512
HEAD_PAIR = LANES // HEAD_DIM
assert TQ == TK
TW = 256
TILES_PER_STEP = 2
PIECE = 16
RB = 1024
RSUB = 512
LCHUNK = 512
ONEHOT_GROUP = 64
LROWS = -(-(TOP_K * TW + N_EXPERTS * (PIECE - 1)) // LCHUNK) * LCHUNK

F32 = jnp.float32
BF16 = jnp.bfloat16
V7X_VMEM_BYTES = 64 * 1024 * 1024
VMEM_LIMIT = V7X_VMEM_BYTES * 7 // 8


def _rms(v, g):
    return v * lax.rsqrt(jnp.mean(v * v, axis=-1, keepdims=True) + EPS) * g


def _split3(v):
    hi = v.astype(BF16)
    r1 = v - hi.astype(F32)
    mid = r1.astype(BF16)
    r2 = r1 - mid.astype(F32)
    lo = r2.astype(BF16)
    return hi, mid, lo


def _dot(a, b):
    return jnp.dot(a, b, preferred_element_type=F32)


def _ada_kernel(c_ref, w_ref, b_ref, o_ref):
    o_ref[...] = _dot(c_ref[...].astype(BF16), w_ref[0].astype(BF16)) + b_ref[...]


def _ada(c, w, b, layer):
    bsz = c.shape[0]
    n = w.shape[2]
    return pl.pallas_call(
        _ada_kernel,
        name="ada",
        grid=(n // D_MODEL,),
        in_specs=[
            pl.BlockSpec((bsz, D_MODEL), lambda j: (0, 0)),
            pl.BlockSpec((1, D_MODEL, D_MODEL), lambda j: (layer, 0, j)),
            pl.BlockSpec((1, D_MODEL), lambda j: (0, j)),
        ],
        out_specs=pl.BlockSpec((bsz, D_MODEL), lambda j: (0, j)),
        out_shape=jax.ShapeDtypeStruct((bsz, n), F32),
    )(c, w, b)


def _premix_kernel(x_ref, shift_ref, scale_ref, g_ref, win_ref, bf_ref, wpool_ref,
                   pscale_ref, gpool_ref, place_ref, ones_ref, fmask_ref,
                   pool_ref, q_ref, kt_ref, v_ref,
                   uext_ref, cum_ref, w1_ref, wf_ref):
    s = pl.program_id(1)
    ts = x_ref.shape[1]

    @pl.when((pl.program_id(0) == 0) & (s == 0))
    def _():
        q0, k0 = D_POOL, D_POOL + D_ATTN
        w1_ref[:, :q0] = win_ref[0, :, :q0].astype(BF16)
        w1_ref[:, q0:k0] = (win_ref[0, :, q0:k0] * (HEAD_DIM ** -0.5 * LOG2E)).astype(BF16)
        w1_ref[:, k0:] = win_ref[0, :, k0:k0 + 2 * D_ATTN].astype(BF16)
        wf_ref[...] = jnp.zeros_like(wf_ref)
        wf_ref[:, :N_HEADS] = win_ref[0, :, k0 + 2 * D_ATTN:].astype(BF16)

    @pl.when(s == 0)
    def _():
        uext_ref[0:MAX_WINDOW, :] = jnp.zeros((MAX_WINDOW, D_POOL), F32)
        cum_ref[...] = jnp.zeros_like(cum_ref)

    x = x_ref[0]
    h = _rms(x, g_ref[...]) * (1.0 + scale_ref[0]) + shift_ref[0]
    hb = h.astype(BF16)
    proj = _dot(hb, w1_ref[...])
    u = proj[:, :D_POOL]
    q = proj[:, D_POOL:D_POOL + D_ATTN]
    k = proj[:, D_POOL + D_ATTN:D_POOL + 2 * D_ATTN]
    v = proj[:, D_POOL + 2 * D_ATTN:]

    uext_ref[MAX_WINDOW:, :] = u
    pos = (s * ts + lax.broadcasted_iota(jnp.int32, (ts, 1), 0) + 1).astype(F32)
    ys = []
    for g, w in enumerate(POOL_WINDOWS):
        c0 = g * POOL_GROUP
        acc = uext_ref[MAX_WINDOW:, c0:c0 + POOL_GROUP]
        for j in range(1, w):
            acc = acc + uext_ref[MAX_WINDOW - j:MAX_WINDOW - j + ts, c0:c0 + POOL_GROUP]
        pooled = acc / jnp.minimum(pos, float(w)) - u[:, c0:c0 + POOL_GROUP]
        ys.append(_dot(pooled.astype(BF16), wpool_ref[g]))
    ypool = jnp.concatenate(ys, axis=1) * pscale_ref[...]
    pool_ref[0] = _rms(ypool, gpool_ref[...]).astype(BF16)
    uext_ref[0:MAX_WINDOW, :] = uext_ref[ts:ts + MAX_WINDOW, :]

    z0 = _dot(hb, wf_ref[...]) + bf_ref[...]
    z = z0
    for p in range(1, N_SPLIT):
        z = z + pltpu.roll(z0, p * N_HEADS, 1)
    logf = jnp.minimum(z, 0.0) - jnp.log1p(jnp.exp(-jnp.abs(z)))
    row = lax.broadcasted_iota(jnp.int32, (CUM_BLOCK, CUM_BLOCK), 0)
    col = lax.broadcasted_iota(jnp.int32, (CUM_BLOCK, CUM_BLOCK), 1)
    tri = (col <= row).astype(BF16)
    pieces = _split3(logf)
    carry = cum_ref[...]
    blocks = []
    for r0 in range(0, ts, CUM_BLOCK):
        blk = carry
        for piece in pieces:
            blk = blk + _dot(tri, piece[r0:r0 + CUM_BLOCK, :])
        carry = blk[CUM_BLOCK - 1:CUM_BLOCK, :]
        blocks.append(blk)
    cum = jnp.concatenate(blocks, axis=0)
    cum_ref[...] = carry

    hi, mid, lo = [piece.astype(F32) for piece in _split3(cum * LOG2E)]
    lane = lax.broadcasted_iota(jnp.int32, (ts, LANES), 1)
    pieces = jnp.where(lane < N_HEADS, hi, jnp.where(lane < 2 * N_HEADS, mid, lo))
    aug = _dot(pieces.astype(BF16), place_ref[...]) + ones_ref[...]
    aug_q = aug[:, :N_HEADS * AUG]
    aug_k = aug[:, N_HEADS * AUG:]

    def expand(a):
        blocks = []
        for j in range(D_ATTN // LANES):
            blk = a[:, j * LANES:(j + 1) * LANES]
            blocks += [blk, blk]
        return jnp.concatenate(blocks, axis=1)

    fmask = fmask_ref[...]
    qa = expand(q) * fmask + aug_q
    ka = expand(k) * fmask + aug_k
    q_ref[0] = qa.astype(BF16)
    kt_ref[0] = ka.T.astype(BF16)
    v_ref[0] = v.astype(BF16)


def _aug_constants():
    width = N_HEADS * AUG
    place = np.zeros((LANES, 2 * width), np.float32)
    ones = np.zeros((1, 2 * width), np.float32)
    fmask = np.zeros((1, width), np.float32)
    for h in range(N_HEADS):
        feat0 = h * AUG + (0 if h % 2 == 0 else HEAD_DIM)
        aug0 = h * AUG + (HEAD_DIM if h % 2 == 0 else 0)
        fmask[0, feat0:feat0 + HEAD_DIM] = 1.0
        for p in range(N_SPLIT):
            place[p * N_HEADS + h, aug0 + p] = 1.0
            ones[0, width + aug0 + p] = 1.0
            place[p * N_HEADS + h, width + aug0 + N_SPLIT + p] = -1.0
            ones[0, aug0 + N_SPLIT + p] = 1.0
    return jnp.asarray(place, BF16), jnp.asarray(ones), jnp.asarray(fmask)


def _premix(x, shift, scale, g, w_in, layer, bfp, wpool, pscale, gpool):
    bsz, seq, _ = x.shape
    d_in = w_in.shape[2]
    assert d_in == D_POOL + 3 * D_ATTN + N_HEADS
    ts = TS_PRE
    place, ones, fmask = _aug_constants()
    full = lambda shape: pl.BlockSpec(shape, lambda b, s: (0,) * len(shape))
    per_batch = pl.BlockSpec((1, 1, D_MODEL), lambda b, s: (b, 0, 0))
    return pl.pallas_call(
        _premix_kernel,
        name="premix",
        grid=(bsz, seq // ts),
        in_specs=[
            pl.BlockSpec((1, ts, D_MODEL), lambda b, s: (b, s, 0)),
            per_batch, per_batch,
            full((1, D_MODEL)),
            pl.BlockSpec((1, D_MODEL, d_in), lambda b, s: (layer, 0, 0),
                         pipeline_mode=pl.Buffered(1)),
            full((1, LANES)),
            full((len(POOL_WINDOWS), POOL_GROUP, POOL_GROUP)),
            full((1, D_POOL)),
            full((1, D_POOL)),
            full((LANES, 2 * N_HEADS * AUG)),
            full((1, 2 * N_HEADS * AUG)),
            full((1, N_HEADS * AUG)),
        ],
        out_specs=[
            pl.BlockSpec((1, ts, D_POOL), lambda b, s: (b, s, 0)),
            pl.BlockSpec((1, ts, N_HEADS * AUG), lambda b, s: (b, s, 0)),
            pl.BlockSpec((1, N_HEADS * AUG, ts), lambda b, s: (b, 0, s)),
            pl.BlockSpec((1, ts, D_ATTN), lambda b, s: (b, s, 0)),
        ],
        out_shape=[
            jax.ShapeDtypeStruct((bsz, seq, D_POOL), BF16),
            jax.ShapeDtypeStruct((bsz, seq, N_HEADS * AUG), BF16),
            jax.ShapeDtypeStruct((bsz, N_HEADS * AUG, seq), BF16),
            jax.ShapeDtypeStruct((bsz, seq, D_ATTN), BF16),
        ],
        scratch_shapes=[
            pltpu.VMEM((ts + MAX_WINDOW, D_POOL), F32),
            pltpu.VMEM((1, LANES), F32),
            pltpu.VMEM((D_MODEL, D_POOL + 3 * D_ATTN), BF16),
            pltpu.VMEM((D_MODEL, LANES), BF16),
        ],
        compiler_params=pltpu.CompilerParams(
            dimension_semantics=("arbitrary", "arbitrary"), vmem_limit_bytes=VMEM_LIMIT),
    )(x, shift, scale, g, w_in, bfp, wpool, pscale, gpool, place, ones, fmask)


def _attn_kernel(q_ref, kt_ref, v_ref, wg_ref, wu_ref, wd_ref, o_ref, wgb_ref, wub_ref, wdb_ref):
    wgb_ref[...] = wg_ref[...].astype(BF16)
    wub_ref[...] = wu_ref[...].astype(BF16)
    wdb_ref[...] = wd_ref[...].astype(BF16)
    seq = q_ref.shape[1]
    qry_i = lax.broadcasted_iota(jnp.int32, (TQ, TK), 0)
    key_i = lax.broadcasted_iota(jnp.int32, (TQ, TK), 1)
    causal = key_i <= qry_i
    one_lane = lax.broadcasted_iota(jnp.int32, (seq, LANES), 1) == 0
    vaug = jnp.concatenate([v_ref[0], jnp.where(one_lane, 1.0, 0.0).astype(BF16)], axis=1)
    out_lane = lax.broadcasted_iota(jnp.int32, (TQ, LANES), 1)
    heads = range(HEAD_PAIR)
    for qi in range(seq // TQ):
        rows = slice(qi * TQ, (qi + 1) * TQ)
        q = [q_ref[0, rows, h * AUG:(h + 1) * AUG] for h in heads]
        m = [jnp.full((TQ, 1), -jnp.inf, F32) for _ in heads]
        acc = [jnp.zeros((TQ, 2 * LANES), F32) for _ in heads]
        for kj in range(qi + 1):
            keys = slice(kj * TK, (kj + 1) * TK)
            for h in heads:
                s = _dot(q[h], kt_ref[0, h * AUG:(h + 1) * AUG, keys])
                if kj == qi:
                    s = jnp.where(causal, s, -jnp.inf)
                m_new = jnp.maximum(m[h], jnp.max(s, axis=1, keepdims=True))
                p = jnp.exp2(s - m_new).astype(BF16)
                acc[h] = jnp.exp2(m[h] - m_new) * acc[h] + _dot(p, vaug[keys, :])
                m[h] = m_new
        o = [acc[h][:, :LANES] / acc[h][:, LANES:LANES + 1] for h in heads]
        o_ref[0, rows, :] = jnp.where(out_lane < HEAD_DIM, o[0], o[1])


def _attn(q, kt, v, wg, wu, wd):
    bsz, seq, _ = q.shape
    pairs = N_HEADS // HEAD_PAIR
    per_step = N_EXPERTS // (bsz * pairs)
    assert per_step * bsz * pairs == N_EXPERTS
    wspec = lambda shape: pl.BlockSpec((per_step,) + shape, lambda b, h: (b * pairs + h, 0, 0))
    return pl.pallas_call(
        _attn_kernel,
        name="attn",
        grid=(bsz, pairs),
        in_specs=[
            pl.BlockSpec((1, seq, HEAD_PAIR * AUG), lambda b, h: (b, 0, h)),
            pl.BlockSpec((1, HEAD_PAIR * AUG, seq), lambda b, h: (b, h, 0)),
            pl.BlockSpec((1, seq, LANES), lambda b, h: (b, 0, h)),
            wspec((D_MODEL, D_EXPERT)), wspec((D_MODEL, D_EXPERT)), wspec((D_EXPERT, D_MODEL)),
        ],
        out_specs=[
            pl.BlockSpec((1, seq, LANES), lambda b, h: (b, 0, h)),
            wspec((D_MODEL, D_EXPERT)), wspec((D_MODEL, D_EXPERT)), wspec((D_EXPERT, D_MODEL)),
        ],
        out_shape=[
            jax.ShapeDtypeStruct((bsz, seq, D_ATTN), F32),
            jax.ShapeDtypeStruct(wg.shape, BF16),
            jax.ShapeDtypeStruct(wu.shape, BF16),
            jax.ShapeDtypeStruct(wd.shape, BF16),
        ],
        compiler_params=pltpu.CompilerParams(
            dimension_semantics=("arbitrary", "arbitrary"), vmem_limit_bytes=VMEM_LIMIT),
    )(q, kt, v, wg, wu, wd)


def _postmix_kernel(x_ref, pool_ref, o_ref, wout_ref, gattn_ref, gpost_ref, gate_ref,
                    gffn_ref, shift_ref, scale_ref, wr_ref, bias_ref,
                    x1_ref, h2_ref, infok_ref, cnt_ref):
    ya = _rms(o_ref[0], gattn_ref[...]).astype(BF16)
    mixed = _dot(pool_ref[0], wout_ref[:D_POOL, :]) + _dot(ya, wout_ref[D_POOL:, :])
    x1 = x_ref[0] + gate_ref[0] * _rms(mixed, gpost_ref[...])
    x1_ref[0] = x1
    h2 = _rms(x1, gffn_ref[...]) * (1.0 + scale_ref[0]) + shift_ref[0]
    h2b = h2.astype(BF16)
    h2_ref[0] = h2b
    h2l = (h2 - h2b.astype(F32)).astype(BF16)
    both = _dot(h2b, wr_ref[...])
    logits = both[:, :LANES] + both[:, LANES:] + _dot(h2l, wr_ref[:, :LANES])
    st = jax.nn.sigmoid(logits).T

    step = pl.program_id(0) * pl.num_programs(1) + pl.program_id(1)

    @pl.when(step == 0)
    def _():
        cnt_ref[...] = jnp.zeros_like(cnt_ref)

    lane = lax.broadcasted_iota(jnp.int32, cnt_ref.shape, 1)
    tiles = x_ref.shape[1] // TW
    for i in range(tiles):
        info, n = _route_tile(st[:, i * TW:(i + 1) * TW], bias_ref)
        infok_ref[:, i * TW:(i + 1) * TW] = info
        cnt_ref[...] = jnp.where(lane == step * tiles + i, jnp.broadcast_to(n, cnt_ref.shape),
                                 cnt_ref[...])


def _postmix(x, pool, ot, wout, gattn, gpost, gate, gffn, shift, scale, wr, bias_col):
    bsz, seq, _ = x.shape
    ts = TS_POST
    ns = seq // ts
    full = lambda shape: pl.BlockSpec(shape, lambda b, s: (0,) * len(shape))
    per_batch = pl.BlockSpec((1, 1, D_MODEL), lambda b, s: (b, 0, 0))
    tok = lambda width: pl.BlockSpec((1, ts, width), lambda b, s: (b, s, 0))
    return pl.pallas_call(
        _postmix_kernel,
        name="postmix",
        grid=(bsz, ns),
        in_specs=[
            tok(D_MODEL), tok(D_POOL), tok(D_ATTN),
            full((D_MODEL, D_MODEL)),
            full((1, D_ATTN)), full((1, D_MODEL)), per_batch,
            full((1, D_MODEL)), per_batch, per_batch,
            full((D_MODEL, 2 * LANES)),
            full((N_EXPERTS, 1)),
        ],
        out_specs=[
            tok(D_MODEL), tok(D_MODEL),
            pl.BlockSpec((2 * TOP_K, ts), lambda b, s: (0, b * ns + s)),
            full((N_EXPERTS, LANES)),
        ],
        out_shape=[
            jax.ShapeDtypeStruct((bsz, seq, D_MODEL), F32),
            jax.ShapeDtypeStruct((bsz, seq, D_MODEL), BF16),
            jax.ShapeDtypeStruct((2 * TOP_K, bsz * seq), F32),
            jax.ShapeDtypeStruct((N_EXPERTS, LANES), F32),
        ],
        compiler_params=pltpu.CompilerParams(
            dimension_semantics=("arbitrary", "arbitrary"), vmem_limit_bytes=VMEM_LIMIT),
    )(x, pool, ot, wout, gattn, gpost, gate, gffn, shift, scale, wr, bias_col)


def _route_tile(st, bias_ref):
    tr = st.shape[1]
    scores = [st[g * GROUP_SIZE:(g + 1) * GROUP_SIZE, :] for g in range(N_EXPERT_GROUPS)]
    sel = [scores[g] + bias_ref[g * GROUP_SIZE:(g + 1) * GROUP_SIZE, :]
           for g in range(N_EXPERT_GROUPS)]
    sub = lax.broadcasted_iota(jnp.int32, (GROUP_SIZE, tr), 0)
    neg = jnp.float32(-jnp.inf)

    rows = []
    for g in range(N_EXPERT_GROUPS):
        v = sel[g]
        m1 = jnp.max(v, axis=0, keepdims=True)
        first = jnp.min(jnp.where(v == m1, sub, GROUP_SIZE), axis=0, keepdims=True)
        m2 = jnp.max(jnp.where(sub == first, neg, v), axis=0, keepdims=True)
        rows.append(m1 + m2)
    gs = jnp.concatenate(rows, axis=0)

    shape = (GROUP_SIZE, tr)
    one = jnp.ones(shape, jnp.int32)
    zero = jnp.zeros(shape, jnp.int32)
    beaten = zero
    for gp in range(N_EXPERT_GROUPS):
        r = jnp.broadcast_to(gs[gp:gp + 1, :], shape)
        tie = jnp.where(sub > gp, one, zero)
        beaten = beaten + jnp.where(r > gs, one, zero) + jnp.where(r == gs, tie, zero)
    keep = jnp.where(beaten < TOPK_GROUPS, 1.0, 0.0)

    masked = [jnp.where(jnp.broadcast_to(keep[g:g + 1, :], shape) > 0.5, sel[g], neg)
              for g in range(N_EXPERT_GROUPS)]

    eidx = [sub + g * GROUP_SIZE for g in range(N_EXPERT_GROUPS)]
    picked = [zero for _ in range(N_EXPERT_GROUPS)]
    for _ in range(TOP_K):
        best = masked[0]
        for g in range(1, N_EXPERT_GROUPS):
            best = jnp.maximum(best, masked[g])
        best = jnp.broadcast_to(jnp.max(best, axis=0, keepdims=True), shape)
        first = jnp.where(masked[0] == best, eidx[0], N_EXPERTS)
        for g in range(1, N_EXPERT_GROUPS):
            first = jnp.minimum(first, jnp.where(masked[g] == best, eidx[g], N_EXPERTS))
        first = jnp.broadcast_to(jnp.min(first, axis=0, keepdims=True), shape)
        for g in range(N_EXPERT_GROUPS):
            hit = eidx[g] == first
            picked[g] = jnp.where(hit, one, picked[g])
            masked[g] = jnp.where(hit, neg, masked[g])
    chosen = [picked[g] > 0 for g in range(N_EXPERT_GROUPS)]

    w = [jnp.where(chosen[g], scores[g], 0.0) for g in range(N_EXPERT_GROUPS)]
    denom = w[0].sum(axis=0, keepdims=True)
    for g in range(1, N_EXPERT_GROUPS):
        denom = denom + w[g].sum(axis=0, keepdims=True)
    wfull = jnp.concatenate([w[g] / denom * ROUTED_SCALE for g in range(N_EXPERT_GROUPS)], axis=0)
    mfull = jnp.concatenate([jnp.where(chosen[g], 1.0, 0.0) for g in range(N_EXPERT_GROUPS)],
                            axis=0)
    mb = mfull.astype(BF16)

    e_r = lax.broadcasted_iota(jnp.int32, (N_EXPERTS, N_EXPERTS), 0)
    e_c = lax.broadcasted_iota(jnp.int32, (N_EXPERTS, N_EXPERTS), 1)
    before_e = (e_c < e_r).astype(BF16)
    t_r = lax.broadcasted_iota(jnp.int32, (tr, tr), 0)
    t_c = lax.broadcasted_iota(jnp.int32, (tr, tr), 1)
    before_t = (t_r < t_c).astype(BF16)
    ordinal = _dot(before_e, mb)
    rank = _dot(mb, before_t)
    n = jnp.sum(mfull, axis=1, keepdims=True)
    pieces = jnp.floor((n + (PIECE - 1.0)) * (1.0 / PIECE))
    run_start = PIECE * _dot(before_e, jnp.broadcast_to(pieces, (N_EXPERTS, LANES)).astype(BF16))
    pos = run_start[:, 0:1] + rank

    sub8 = lax.broadcasted_iota(jnp.int32, (TOP_K, tr), 0)
    pos8 = jnp.zeros((TOP_K, tr), F32)
    w8 = jnp.zeros((TOP_K, tr), F32)
    for k in range(TOP_K):
        selk = jnp.where(ordinal == float(k), mfull, 0.0)
        pk = jnp.sum(selk * pos, axis=0, keepdims=True)
        wk = jnp.sum(selk * wfull, axis=0, keepdims=True)
        pos8 = jnp.where(sub8 == k, jnp.broadcast_to(pk, (TOP_K, tr)), pos8)
        w8 = jnp.where(sub8 == k, jnp.broadcast_to(wk, (TOP_K, tr)), w8)
    return jnp.concatenate([pos8, w8], axis=0), n


def _swiglu(xb, wg, wu):
    g = _dot(xb, wg)
    return (g * jax.nn.sigmoid(g)) * _dot(xb, wu)


def _piece_copy(src_ref, src_piece, dst_ref, dst_piece, sem, n_pieces=1):
    src = src_ref.at[pl.ds(src_piece, n_pieces)]
    dst = dst_ref.at[pl.ds(dst_piece, n_pieces)]
    return pltpu.make_async_copy(src, dst, sem)


def _as_pieces(rows):
    return rows.reshape(rows.shape[0] // PIECE, PIECE, rows.shape[1])


LOCAL_BITS = 8
COPY_SIZES = (4, 3, 2, 1)
MAX_COPIES = N_EXPERTS
assert LROWS // (COPY_SIZES[0] * PIECE) <= MAX_COPIES


def _for_each_copy(copies_ref, counts_ref, tile, fn):
    for c, size in enumerate(COPY_SIZES):
        base = (tile * len(COPY_SIZES) + c) * MAX_COPIES

        def body(i, carry, base=base, size=size):
            word = copies_ref[base + i]
            fn(word & ((1 << LOCAL_BITS) - 1), lax.shift_right_logical(word, LOCAL_BITS), size)
            return carry

        lax.fori_loop(0, counts_ref[tile * (len(COPY_SIZES) + 1) + c], body, 0)


def _tile_pieces(counts_ref, tile):
    return counts_ref[tile * (len(COPY_SIZES) + 1) + len(COPY_SIZES)]


def _wait_pieces(total, copy_of):
    for bit in range((LROWS // PIECE).bit_length()):
        @pl.when((lax.shift_right_logical(total, bit) & 1) == 1)
        def _():
            copy_of(1 << bit).wait()


def _onehot_chunks(infok_ref, weighted, cols):
    pos = infok_ref[0:TOP_K, cols]
    grp = jnp.floor(pos * (1.0 / ONEHOT_GROUP))
    off = pos - ONEHOT_GROUP * grp
    sub = lax.broadcasted_iota(jnp.int32, (ONEHOT_GROUP, TW), 0).astype(F32)
    offs, grps = [], []
    for k in range(TOP_K):
        value = infok_ref[TOP_K + k:TOP_K + k + 1, cols] if weighted else 1.0
        offs.append(jnp.where(sub == off[k:k + 1, :], value, 0.0).astype(BF16))
        grps.append(jnp.broadcast_to(grp[k:k + 1, :], (PIECE, TW)).astype(BF16))
    zero = jnp.zeros((PIECE, TW), BF16)

    def chunk(j):
        tiles = []
        for a in range(j * LCHUNK // ONEHOT_GROUP, (j + 1) * LCHUNK // ONEHOT_GROUP):
            hit = [grps[k] == a for k in range(TOP_K)]
            for i in range(ONEHOT_GROUP // PIECE):
                rows = zero
                for k in range(TOP_K):
                    rows = rows + jnp.where(hit[k], offs[k][i * PIECE:(i + 1) * PIECE, :], zero)
                tiles.append(rows)
        return jnp.concatenate(tiles, axis=0)

    return chunk


def _dispatch_kernel(copies_ref, counts_ref, tail_ref, ntail_ref, nv_ref,
                     h2_ref, infok_ref, xs_ref, lbuf, zbuf, sem, zsem):
    step = pl.program_id(0)
    last = pl.num_programs(0) - 1
    block_pieces = RB // PIECE
    n_blocks = xs_ref.shape[0] // block_pieces

    def spare_blocks(fn):
        def per_block(b, carry):
            fn(b)
            return carry
        lax.fori_loop(nv_ref[0], n_blocks, per_block, 0)

    def zero_block(b):
        return pltpu.make_async_copy(zbuf, xs_ref.at[pl.ds(b * block_pieces, block_pieces)], zsem)

    def tails(fn):
        def per_expert(e, carry):
            n = ntail_ref[e]
            for bit in range((RB // PIECE - 1).bit_length()):
                @pl.when((lax.shift_right_logical(n, bit) & 1) == 1)
                def _():
                    done = n & ((1 << bit) - 1)
                    fn(_piece_copy(zbuf, 0, xs_ref, tail_ref[e] + done, zsem, 1 << bit))
            return carry
        lax.fori_loop(0, N_EXPERTS, per_expert, 0)

    def start(t, s):
        _for_each_copy(
            copies_ref, counts_ref, t,
            lambda lp, gp, n: _piece_copy(lbuf.at[s], lp, xs_ref, gp, sem.at[s], n).start())

    def wait(t, s):
        _wait_pieces(_tile_pieces(counts_ref, t),
                     lambda n: _piece_copy(lbuf.at[s], 0, xs_ref, 0, sem.at[s], n))

    @pl.when(step == 0)
    def _():
        zbuf[...] = jnp.zeros_like(zbuf)
        tails(lambda copy: copy.start(priority=1))
        spare_blocks(lambda b: zero_block(b).start(priority=1))

    n_chunks = LROWS // LCHUNK
    for slot in range(TILES_PER_STEP):
        tile = step * TILES_PER_STEP + slot

        @pl.when(step >= 1)
        def _():
            wait(tile - TILES_PER_STEP, slot)

        onehot = _onehot_chunks(infok_ref, False, slice(slot * TW, (slot + 1) * TW))
        xb = h2_ref[slot * TW:(slot + 1) * TW, :]
        used_rows = PIECE * _tile_pieces(counts_ref, tile)

        def sort_chunk(j, slot=slot, onehot=onehot, xb=xb):
            lbuf[slot, j * LCHUNK // PIECE:(j + 1) * LCHUNK // PIECE] = _as_pieces(
                _dot(onehot(j), xb).astype(BF16))

        for j in range(n_chunks - 1):
            sort_chunk(j)
        pl.when(used_rows > (n_chunks - 1) * LCHUNK)(functools.partial(sort_chunk, n_chunks - 1))
        start(tile, slot)

    @pl.when(step == last)
    def _():
        for slot in range(TILES_PER_STEP):
            wait(step * TILES_PER_STEP + slot, slot)
        tails(lambda copy: copy.wait())
        spare_blocks(lambda b: zero_block(b).wait())


def _dispatch(tables, h2, infok, n_rows):
    t = h2.shape[0]
    return pl.pallas_call(
        _dispatch_kernel,
        name="dispatch",
        grid_spec=pltpu.PrefetchScalarGridSpec(
            num_scalar_prefetch=5,
            grid=(t // (TW * TILES_PER_STEP),),
            in_specs=[
                pl.BlockSpec((TW * TILES_PER_STEP, D_MODEL), lambda i, *_: (i, 0)),
                pl.BlockSpec((2 * TOP_K, TW * TILES_PER_STEP), lambda i, *_: (0, i)),
            ],
            out_specs=pl.BlockSpec(memory_space=pl.ANY),
            scratch_shapes=[
                pltpu.VMEM((2, LROWS // PIECE, PIECE, D_MODEL), BF16),
                pltpu.VMEM((RB // PIECE, PIECE, D_MODEL), BF16),
                pltpu.SemaphoreType.DMA((2,)),
                pltpu.SemaphoreType.DMA(()),
            ],
        ),
        out_shape=jax.ShapeDtypeStruct((n_rows // PIECE, PIECE, D_MODEL), BF16),
        compiler_params=pltpu.CompilerParams(
            dimension_semantics=("arbitrary",), vmem_limit_bytes=VMEM_LIMIT),
    )(*tables, h2, infok)


def _experts_kernel(be_ref, nv_ref, xs_ref, wg_ref, wu_ref, wd_ref, ys_ref):
    @pl.when(pl.program_id(0) < nv_ref[0])
    def _():
        for j in range(RB // RSUB):
            rows = slice(j * RSUB, (j + 1) * RSUB)
            h = _swiglu(xs_ref[rows, :], wg_ref[0], wu_ref[0])
            ys_ref[rows, :] = _dot(h.astype(BF16), wd_ref[0]).astype(BF16)


def _experts(block_expert, n_valid, xs, wg, wu, wd):
    n_blocks = xs.shape[0] // RB
    rows = pl.BlockSpec((RB, D_MODEL), lambda i, be, nv: (jnp.minimum(i, nv[0] - 1), 0))
    return pl.pallas_call(
        _experts_kernel,
        name="experts",
        grid_spec=pltpu.PrefetchScalarGridSpec(
            num_scalar_prefetch=2,
            grid=(n_blocks,),
            in_specs=[
                rows,
                pl.BlockSpec((1, D_MODEL, D_EXPERT), lambda i, be, nv: (be[i], 0, 0)),
                pl.BlockSpec((1, D_MODEL, D_EXPERT), lambda i, be, nv: (be[i], 0, 0)),
                pl.BlockSpec((1, D_EXPERT, D_MODEL), lambda i, be, nv: (be[i], 0, 0)),
            ],
            out_specs=rows,
        ),
        out_shape=jax.ShapeDtypeStruct(xs.shape, BF16),
        input_output_aliases={2: 0},
        compiler_params=pltpu.CompilerParams(
            dimension_semantics=("arbitrary",), vmem_limit_bytes=VMEM_LIMIT),
    )(block_expert, n_valid, xs, wg, wu, wd)


def _combine_kernel(copies_ref, counts_ref,
                    infok_ref, h2_ref, x1_ref, gate_ref, gpost_ref, wsg_ref, wsu_ref, wsd_ref,
                    ys_ref, o_ref, ybuf, acc_ref, sem):
    step = pl.program_id(0)
    last = pl.num_programs(0) - 1

    def start(t, s):
        _for_each_copy(
            copies_ref, counts_ref, t,
            lambda lp, gp, n: _piece_copy(ys_ref, gp, ybuf.at[s], lp, sem.at[s], n).start())

    def wait(t, s):
        _wait_pieces(_tile_pieces(counts_ref, t),
                     lambda n: _piece_copy(ys_ref, 0, ybuf.at[s], 0, sem.at[s], n))

    @pl.when(step == 0)
    def _():
        ybuf[...] = jnp.zeros_like(ybuf)
        start(0, 0)

    n_chunks = LROWS // LCHUNK
    for slot in range(TILES_PER_STEP):
        tile = step * TILES_PER_STEP + slot
        rows = slice(slot * TW, (slot + 1) * TW)

        if slot + 1 < TILES_PER_STEP:
            start(tile + 1, slot + 1)
        else:
            @pl.when(step < last)
            def _():
                start(tile + 1, 0)

        wait(tile, slot)
        weights = _onehot_chunks(infok_ref, True, rows)
        hs = _swiglu(h2_ref[rows, :], wsg_ref[...], wsu_ref[...])
        ff = _dot(hs.astype(BF16), wsd_ref[...])
        used_rows = PIECE * _tile_pieces(counts_ref, tile)

        def sum_chunk(j, slot=slot, weights=weights):
            fetched = ybuf[slot, j * LCHUNK // PIECE:(j + 1) * LCHUNK // PIECE]
            return lax.dot_general(weights(j), fetched.reshape(LCHUNK, D_MODEL),
                                   (((0,), (0,)), ((), ())), preferred_element_type=F32)

        def add_chunk(j, sum_chunk=sum_chunk):
            acc_ref[...] += sum_chunk(j)

        for j in range(n_chunks - 1):
            ff = ff + sum_chunk(j)
        acc_ref[...] = ff
        pl.when(used_rows > (n_chunks - 1) * LCHUNK)(functools.partial(add_chunk, n_chunks - 1))
        o_ref[rows, :] = x1_ref[rows, :] + gate_ref[0] * _rms(acc_ref[...], gpost_ref[...])


def _combine(tables, infok, h2, x1, gate, gpost, wsg, wsu, wsd, ys, seq):
    t = h2.shape[0]
    tokens = TW * TILES_PER_STEP
    per_seq = seq // tokens
    full = lambda shape: pl.BlockSpec(shape, lambda i, *_: (0,) * len(shape))
    tok = lambda width: pl.BlockSpec((tokens, width), lambda i, *_: (i, 0))
    return pl.pallas_call(
        _combine_kernel,
        name="combine",
        grid_spec=pltpu.PrefetchScalarGridSpec(
            num_scalar_prefetch=2,
            grid=(t // tokens,),
            in_specs=[
                pl.BlockSpec((2 * TOP_K, tokens), lambda i, *_: (0, i)), tok(D_MODEL), tok(D_MODEL),
                pl.BlockSpec((1, 1, D_MODEL), lambda i, *_: (i // per_seq, 0, 0)),
                full((1, D_MODEL)),
                full((D_MODEL, D_SHARED)), full((D_MODEL, D_SHARED)), full((D_SHARED, D_MODEL)),
                pl.BlockSpec(memory_space=pl.ANY),
            ],
            out_specs=tok(D_MODEL),
            scratch_shapes=[
                pltpu.VMEM((2, LROWS // PIECE, PIECE, D_MODEL), BF16),
                pltpu.VMEM((TW, D_MODEL), F32),
                pltpu.SemaphoreType.DMA((2,)),
            ],
        ),
        out_shape=jax.ShapeDtypeStruct((t, D_MODEL), F32),
        compiler_params=pltpu.CompilerParams(
            dimension_semantics=("arbitrary",), vmem_limit_bytes=VMEM_LIMIT),
    )(*tables, infok, h2, x1, gate, gpost, wsg, wsu, wsd, ys)


def _row_buffer_blocks(n_tiles):
    return (TOP_K * TW * n_tiles + N_EXPERTS * n_tiles * (PIECE - 1)
            + N_EXPERTS * (RB - PIECE) + RB - 1) // RB


def _dispatch_plan(cnt, n_tiles):
    n = cnt[:, :n_tiles].astype(jnp.int32)
    pieces = (n + PIECE - 1) // PIECE
    local = jnp.cumsum(pieces, axis=0) - pieces
    seg = jnp.sum(pieces, axis=1)
    per_block = RB // PIECE
    seg_pad = (seg + per_block - 1) // per_block * per_block
    seg_end = jnp.cumsum(seg_pad)
    seg_start = seg_end - seg_pad
    glob = seg_start[:, None] + jnp.cumsum(pieces, axis=1) - pieces
    n_blocks = _row_buffer_blocks(n_tiles)
    n_valid = seg_end[-1] // per_block
    blk = jnp.minimum(jnp.arange(n_blocks, dtype=jnp.int32), n_valid - 1)
    block_end = seg_end // per_block
    block_expert = jnp.sum((block_end[None, :] <= blk[:, None]).astype(jnp.int32), axis=1)
    block_expert = jnp.minimum(block_expert, N_EXPERTS - 1)
    assert LROWS // PIECE <= 1 << LOCAL_BITS and n_blocks * per_block < 1 << (31 - LOCAL_BITS)

    def copy_list(count, width):
        end = jnp.cumsum(count, axis=0).T
        slot = jnp.arange(width, dtype=jnp.int32)
        expert = jnp.sum((end[:, None, :] <= slot[None, :, None]).astype(jnp.int32), axis=-1)
        pick = expert[:, :, None] == jnp.arange(N_EXPERTS, dtype=jnp.int32)
        take = lambda a: jnp.sum(jnp.where(pick, a.T[:, None, :], 0), axis=-1)
        return slot[None, :] - take(jnp.cumsum(count, axis=0) - count), take

    big = COPY_SIZES[0]
    assert COPY_SIZES == tuple(range(big, 0, -1))
    n_big = pieces // big
    rest = pieces % big
    packed = local | (glob << LOCAL_BITS)
    step = 1 + (1 << LOCAL_BITS)
    within, take = copy_list(n_big, MAX_COPIES)
    lists = [take(packed) + big * step * within]
    counts = [jnp.sum(n_big, axis=0)]
    rest_start = packed + big * step * n_big
    for size in COPY_SIZES[1:]:
        has = (rest == size).astype(jnp.int32)
        _, take = copy_list(has, MAX_COPIES)
        lists.append(take(rest_start))
        counts.append(jnp.sum(has, axis=0))
    counts.append(jnp.sum(pieces, axis=0))
    tables = (jnp.stack(lists, axis=1).reshape(-1).astype(jnp.int32),
              jnp.stack(counts, axis=1).reshape(-1).astype(jnp.int32))
    tails = ((seg_start + seg).astype(jnp.int32), (seg_pad - seg).astype(jnp.int32))
    return tables, tails, block_expert, n_valid.reshape(1).astype(jnp.int32), n_blocks * RB


def kernel(x, c, w_ada, b_ada, g_pre_mix, g_post_mix, g_pre_ffn, g_post_ffn, w_in, b_forget,
           w_pool, pool_scale, g_pool_out, g_attn_out, w_out, w_router, router_bias,
           w_gate, w_up, w_down, ws_gate, ws_up, ws_down):
    bsz, seq, d = x.shape
    depth = w_ada.shape[0]
    for l in range(depth):
        mod = _ada(c, w_ada, b_ada[l][None, :], l)
        shift_m, scale_m, gate_m, shift_f, scale_f, gate_f = [
            m.reshape(bsz, 1, d) for m in jnp.split(mod, 6, axis=-1)]

        bfp = jnp.pad(b_forget[l], (0, LANES - N_HEADS))[None, :]
        pool, qa, kt, v = _premix(
            x, shift_m, scale_m, g_pre_mix[l][None, :], w_in, l, bfp, w_pool[l].astype(BF16),
            pool_scale[l][None, :], g_pool_out[l][None, :])
        ot, wgb, wub, wdb = _attn(qa, kt, v, w_gate[l], w_up[l], w_down[l])
        t = bsz * seq
        assert t // TW <= LANES

        wr = jnp.pad(w_router[l], ((0, 0), (0, LANES - N_EXPERTS)))
        wrh = wr.astype(BF16)
        wrl = (wr - wrh.astype(F32)).astype(BF16)
        x1, h2, infok, cnt = _postmix(
            x, pool, ot, w_out[l].astype(BF16), g_attn_out[l][None, :], g_post_mix[l][None, :],
            gate_m, g_pre_ffn[l][None, :], shift_f, scale_f,
            jnp.concatenate([wrh, wrl], axis=1), router_bias[l][:, None])
        tables, tails, block_expert, n_valid, n_rows = _dispatch_plan(cnt, t // TW)
        h2f = h2.reshape(t, d)
        xs = _dispatch(tables + tails + (n_valid,), h2f, infok, n_rows)
        ys = _experts(block_expert, n_valid, xs.reshape(n_rows, d), wgb, wub, wdb)
        ys = ys.reshape(xs.shape)
        out = _combine(tables, infok, h2f, x1.reshape(t, d), gate_f, g_post_ffn[l][None, :],
                       ws_gate[l].astype(BF16), ws_up[l].astype(BF16), ws_down[l].astype(BF16),
                       ys, seq)
        x = out.reshape(bsz, seq, d)
    return x
```

```python
import functools

import numpy as np
import jax
import jax.numpy as jnp
from jax import lax
from jax.experimental import pallas as pl
from jax.experimental.pallas import tpu as pltpu

D_MODEL = 1024
D_POOL = 512
POOL_WINDOWS = (2, 4, 8, 16)
POOL_GROUP = 128
MAX_WINDOW = max(POOL_WINDOWS)
D_ATTN = 512
HEAD_DIM = 64
N_HEADS = 8
N_EXPERTS = 64
N_EXPERT_GROUPS = 8
GROUP_SIZE = N_EXPERTS // N_EXPERT_GROUPS
TOPK_GROUPS = 4
TOP_K = 8
D_EXPERT = 256
D_SHARED = 256
ROUTED_SCALE = 2.5
EPS = 1e-6

LOG2E = 1.4426950408889634
LANES = 128
N_SPLIT = 3
AUG = LANES

TS_PRE = 1024
CUM_BLOCK = 256
TS_POST = 1024
TQ = 512
TK = 512
HEAD_PAIR = LANES // HEAD_DIM
assert TQ == TK
TW = 256
PIECE = 16
RB = 1024
RSUB = 512
LCHUNK = 512
ONEHOT_GROUP = 64
LROWS = -(-(TOP_K * TW + N_EXPERTS * (PIECE - 1)) // LCHUNK) * LCHUNK

F32 = jnp.float32
BF16 = jnp.bfloat16
V7X_VMEM_BYTES = 64 * 1024 * 1024
VMEM_LIMIT = V7X_VMEM_BYTES * 7 // 8


def _rms(v, g):
    return v * lax.rsqrt(jnp.mean(v * v, axis=-1, keepdims=True) + EPS) * g


def _split3(v):
    hi = v.astype(BF16)
    r1 = v - hi.astype(F32)
    mid = r1.astype(BF16)
    r2 = r1 - mid.astype(F32)
    lo = r2.astype(BF16)
    return hi, mid, lo


def _dot(a, b):
    return jnp.dot(a, b, preferred_element_type=F32)


def _ada_kernel(c_ref, w_ref, b_ref, o_ref):
    o_ref[...] = _dot(c_ref[...].astype(BF16), w_ref[0].astype(BF16)) + b_ref[...]


def _ada(c, w, b, layer):
    bsz = c.shape[0]
    n = w.shape[2]
    return pl.pallas_call(
        _ada_kernel,
        name="ada",
        grid=(n // D_MODEL,),
        in_specs=[
            pl.BlockSpec((bsz, D_MODEL), lambda j: (0, 0)),
            pl.BlockSpec((1, D_MODEL, D_MODEL), lambda j: (layer, 0, j)),
            pl.BlockSpec((1, D_MODEL), lambda j: (0, j)),
        ],
        out_specs=pl.BlockSpec((bsz, D_MODEL), lambda j: (0, j)),
        out_shape=jax.ShapeDtypeStruct((bsz, n), F32),
    )(c, w, b)


def _premix_kernel(x_ref, shift_ref, scale_ref, g_ref, win_ref, bf_ref, wpool_ref,
                   pscale_ref, gpool_ref, place_ref, ones_ref, fmask_ref,
                   pool_ref, q_ref, kt_ref, v_ref,
                   uext_ref, cum_ref, w1_ref, wf_ref):
    s = pl.program_id(1)
    ts = x_ref.shape[1]

    @pl.when((pl.program_id(0) == 0) & (s == 0))
    def _():
        q0, k0 = D_POOL, D_POOL + D_ATTN
        w1_ref[:, :q0] = win_ref[0, :, :q0].astype(BF16)
        w1_ref[:, q0:k0] = (win_ref[0, :, q0:k0] * (HEAD_DIM ** -0.5 * LOG2E)).astype(BF16)
        w1_ref[:, k0:] = win_ref[0, :, k0:k0 + 2 * D_ATTN].astype(BF16)
        wf_ref[...] = jnp.zeros_like(wf_ref)
        wf_ref[:, :N_HEADS] = win_ref[0, :, k0 + 2 * D_ATTN:].astype(BF16)

    @pl.when(s == 0)
    def _():
        uext_ref[0:MAX_WINDOW, :] = jnp.zeros((MAX_WINDOW, D_POOL), F32)
        cum_ref[...] = jnp.zeros_like(cum_ref)

    x = x_ref[0]
    h = _rms(x, g_ref[...]) * (1.0 + scale_ref[0]) + shift_ref[0]
    hb = h.astype(BF16)
    proj = _dot(hb, w1_ref[...])
    u = proj[:, :D_POOL]
    q = proj[:, D_POOL:D_POOL + D_ATTN]
    k = proj[:, D_POOL + D_ATTN:D_POOL + 2 * D_ATTN]
    v = proj[:, D_POOL + 2 * D_ATTN:]

    uext_ref[MAX_WINDOW:, :] = u
    pos = (s * ts + lax.broadcasted_iota(jnp.int32, (ts, 1), 0) + 1).astype(F32)
    ys = []
    for g, w in enumerate(POOL_WINDOWS):
        c0 = g * POOL_GROUP
        acc = uext_ref[MAX_WINDOW:, c0:c0 + POOL_GROUP]
        for j in range(1, w):
            acc = acc + uext_ref[MAX_WINDOW - j:MAX_WINDOW - j + ts, c0:c0 + POOL_GROUP]
        pooled = acc / jnp.minimum(pos, float(w)) - u[:, c0:c0 + POOL_GROUP]
        ys.append(_dot(pooled.astype(BF16), wpool_ref[g]))
    ypool = jnp.concatenate(ys, axis=1) * pscale_ref[...]
    pool_ref[0] = _rms(ypool, gpool_ref[...]).astype(BF16)
    uext_ref[0:MAX_WINDOW, :] = uext_ref[ts:ts + MAX_WINDOW, :]

    z0 = _dot(hb, wf_ref[...]) + bf_ref[...]
    z = z0
    for p in range(1, N_SPLIT):
        z = z + pltpu.roll(z0, p * N_HEADS, 1)
    logf = jnp.minimum(z, 0.0) - jnp.log1p(jnp.exp(-jnp.abs(z)))
    row = lax.broadcasted_iota(jnp.int32, (CUM_BLOCK, CUM_BLOCK), 0)
    col = lax.broadcasted_iota(jnp.int32, (CUM_BLOCK, CUM_BLOCK), 1)
    tri = (col <= row).astype(BF16)
    pieces = _split3(logf)
    carry = cum_ref[...]
    blocks = []
    for r0 in range(0, ts, CUM_BLOCK):
        blk = carry
        for piece in pieces:
            blk = blk + _dot(tri, piece[r0:r0 + CUM_BLOCK, :])
        carry = blk[CUM_BLOCK - 1:CUM_BLOCK, :]
        blocks.append(blk)
    cum = jnp.concatenate(blocks, axis=0)
    cum_ref[...] = carry

    hi, mid, lo = [piece.astype(F32) for piece in _split3(cum * LOG2E)]
    lane = lax.broadcasted_iota(jnp.int32, (ts, LANES), 1)
    pieces = jnp.where(lane < N_HEADS, hi, jnp.where(lane < 2 * N_HEADS, mid, lo))
    aug = _dot(pieces.astype(BF16), place_ref[...]) + ones_ref[...]
    aug_q = aug[:, :N_HEADS * AUG]
    aug_k = aug[:, N_HEADS * AUG:]

    def expand(a):
        blocks = []
        for j in range(D_ATTN // LANES):
            blk = a[:, j * LANES:(j + 1) * LANES]
            blocks += [blk, blk]
        return jnp.concatenate(blocks, axis=1)

    fmask = fmask_ref[...]
    qa = expand(q) * fmask + aug_q
    ka = expand(k) * fmask + aug_k
    q_ref[0] = qa.astype(BF16)
    kt_ref[0] = ka.T.astype(BF16)
    v_ref[0] = v.astype(BF16)


def _aug_constants():
    width = N_HEADS * AUG
    place = np.zeros((LANES, 2 * width), np.float32)
    ones = np.zeros((1, 2 * width), np.float32)
    fmask = np.zeros((1, width), np.float32)
    for h in range(N_HEADS):
        feat0 = h * AUG + (0 if h % 2 == 0 else HEAD_DIM)
        aug0 = h * AUG + (HEAD_DIM if h % 2 == 0 else 0)
        fmask[0, feat0:feat0 + HEAD_DIM] = 1.0
        for p in range(N_SPLIT):
            place[p * N_HEADS + h, aug0 + p] = 1.0
            ones[0, width + aug0 + p] = 1.0
            place[p * N_HEADS + h, width + aug0 + N_SPLIT + p] = -1.0
            ones[0, aug0 + N_SPLIT + p] = 1.0
    return jnp.asarray(place, BF16), jnp.asarray(ones), jnp.asarray(fmask)


def _premix(x, shift, scale, g, w_in, layer, bfp, wpool, pscale, gpool):
    bsz, seq, _ = x.shape
    d_in = w_in.shape[2]
    assert d_in == D_POOL + 3 * D_ATTN + N_HEADS
    ts = TS_PRE
    place, ones, fmask = _aug_constants()
    full = lambda shape: pl.BlockSpec(shape, lambda b, s: (0,) * len(shape))
    per_batch = pl.BlockSpec((1, 1, D_MODEL), lambda b, s: (b, 0, 0))
    return pl.pallas_call(
        _premix_kernel,
        name="premix",
        grid=(bsz, seq // ts),
        in_specs=[
            pl.BlockSpec((1, ts, D_MODEL), lambda b, s: (b, s, 0)),
            per_batch, per_batch,
            full((1, D_MODEL)),
            pl.BlockSpec((1, D_MODEL, d_in), lambda b, s: (layer, 0, 0),
                         pipeline_mode=pl.Buffered(1)),
            full((1, LANES)),
            full((len(POOL_WINDOWS), POOL_GROUP, POOL_GROUP)),
            full((1, D_POOL)),
            full((1, D_POOL)),
            full((LANES, 2 * N_HEADS * AUG)),
            full((1, 2 * N_HEADS * AUG)),
            full((1, N_HEADS * AUG)),
        ],
        out_specs=[
            pl.BlockSpec((1, ts, D_POOL), lambda b, s: (b, s, 0)),
            pl.BlockSpec((1, ts, N_HEADS * AUG), lambda b, s: (b, s, 0)),
            pl.BlockSpec((1, N_HEADS * AUG, ts), lambda b, s: (b, 0, s)),
            pl.BlockSpec((1, ts, D_ATTN), lambda b, s: (b, s, 0)),
        ],
        out_shape=[
            jax.ShapeDtypeStruct((bsz, seq, D_POOL), BF16),
            jax.ShapeDtypeStruct((bsz, seq, N_HEADS * AUG), BF16),
            jax.ShapeDtypeStruct((bsz, N_HEADS * AUG, seq), BF16),
            jax.ShapeDtypeStruct((bsz, seq, D_ATTN), BF16),
        ],
        scratch_shapes=[
            pltpu.VMEM((ts + MAX_WINDOW, D_POOL), F32),
            pltpu.VMEM((1, LANES), F32),
            pltpu.VMEM((D_MODEL, D_POOL + 3 * D_ATTN), BF16),
            pltpu.VMEM((D_MODEL, LANES), BF16),
        ],
        compiler_params=pltpu.CompilerParams(
            dimension_semantics=("arbitrary", "arbitrary"), vmem_limit_bytes=VMEM_LIMIT),
    )(x, shift, scale, g, w_in, bfp, wpool, pscale, gpool, place, ones, fmask)


def _attn_kernel(q_ref, kt_ref, v_ref, wg_ref, wu_ref, wd_ref, o_ref, wgb_ref, wub_ref, wdb_ref):
    wgb_ref[...] = wg_ref[...].astype(BF16)
    wub_ref[...] = wu_ref[...].astype(BF16)
    wdb_ref[...] = wd_ref[...].astype(BF16)
    seq = q_ref.shape[1]
    qry_i = lax.broadcasted_iota(jnp.int32, (TQ, TK), 0)
    key_i = lax.broadcasted_iota(jnp.int32, (TQ, TK), 1)
    causal = key_i <= qry_i
    one_lane = lax.broadcasted_iota(jnp.int32, (seq, LANES), 1) == 0
    vaug = jnp.concatenate([v_ref[0], jnp.where(one_lane, 1.0, 0.0).astype(BF16)], axis=1)
    out_lane = lax.broadcasted_iota(jnp.int32, (TQ, LANES), 1)
    heads = range(HEAD_PAIR)
    for qi in range(seq // TQ):
        rows = slice(qi * TQ, (qi + 1) * TQ)
        q = [q_ref[0, rows, h * AUG:(h + 1) * AUG] for h in heads]
        m = [jnp.full((TQ, 1), -jnp.inf, F32) for _ in heads]
        acc = [jnp.zeros((TQ, 2 * LANES), F32) for _ in heads]
        for kj in range(qi + 1):
            keys = slice(kj * TK, (kj + 1) * TK)
            for h in heads:
                s = _dot(q[h], kt_ref[0, h * AUG:(h + 1) * AUG, keys])
                if kj == qi:
                    s = jnp.where(causal, s, -jnp.inf)
                m_new = jnp.maximum(m[h], jnp.max(s, axis=1, keepdims=True))
                p = jnp.exp2(s - m_new).astype(BF16)
                acc[h] = jnp.exp2(m[h] - m_new) * acc[h] + _dot(p, vaug[keys, :])
                m[h] = m_new
        o = [acc[h][:, :LANES] / acc[h][:, LANES:LANES + 1] for h in heads]
        o_ref[0, rows, :] = jnp.where(out_lane < HEAD_DIM, o[0], o[1])


def _attn(q, kt, v, wg, wu, wd):
    bsz, seq, _ = q.shape
    pairs = N_HEADS // HEAD_PAIR
    per_step = N_EXPERTS // (bsz * pairs)
    assert per_step * bsz * pairs == N_EXPERTS
    wspec = lambda shape: pl.BlockSpec((per_step,) + shape, lambda b, h: (b * pairs + h, 0, 0))
    return pl.pallas_call(
        _attn_kernel,
        name="attn",
        grid=(bsz, pairs),
        in_specs=[
            pl.BlockSpec((1, seq, HEAD_PAIR * AUG), lambda b, h: (b, 0, h)),
            pl.BlockSpec((1, HEAD_PAIR * AUG, seq), lambda b, h: (b, h, 0)),
            pl.BlockSpec((1, seq, LANES), lambda b, h: (b, 0, h)),
            wspec((D_MODEL, D_EXPERT)), wspec((D_MODEL, D_EXPERT)), wspec((D_EXPERT, D_MODEL)),
        ],
        out_specs=[
            pl.BlockSpec((1, seq, LANES), lambda b, h: (b, 0, h)),
            wspec((D_MODEL, D_EXPERT)), wspec((D_MODEL, D_EXPERT)), wspec((D_EXPERT, D_MODEL)),
        ],
        out_shape=[
            jax.ShapeDtypeStruct((bsz, seq, D_ATTN), F32),
            jax.ShapeDtypeStruct(wg.shape, BF16),
            jax.ShapeDtypeStruct(wu.shape, BF16),
            jax.ShapeDtypeStruct(wd.shape, BF16),
        ],
        compiler_params=pltpu.CompilerParams(
            dimension_semantics=("arbitrary", "arbitrary"), vmem_limit_bytes=VMEM_LIMIT),
    )(q, kt, v, wg, wu, wd)


def _postmix_kernel(x_ref, pool_ref, o_ref, wout_ref, gattn_ref, gpost_ref, gate_ref,
                    gffn_ref, shift_ref, scale_ref, wr_ref, bias_ref,
                    x1_ref, h2_ref, infok_ref, cnt_ref):
    ya = _rms(o_ref[0], gattn_ref[...]).astype(BF16)
    mixed = _dot(pool_ref[0], wout_ref[:D_POOL, :]) + _dot(ya, wout_ref[D_POOL:, :])
    x1 = x_ref[0] + gate_ref[0] * _rms(mixed, gpost_ref[...])
    x1_ref[0] = x1
    h2 = _rms(x1, gffn_ref[...]) * (1.0 + scale_ref[0]) + shift_ref[0]
    h2b = h2.astype(BF16)
    h2_ref[0] = h2b
    h2l = (h2 - h2b.astype(F32)).astype(BF16)
    both = _dot(h2b, wr_ref[...])
    logits = both[:, :LANES] + both[:, LANES:] + _dot(h2l, wr_ref[:, :LANES])
    st = jax.nn.sigmoid(logits).T

    step = pl.program_id(0) * pl.num_programs(1) + pl.program_id(1)

    @pl.when(step == 0)
    def _():
        cnt_ref[...] = jnp.zeros_like(cnt_ref)

    lane = lax.broadcasted_iota(jnp.int32, cnt_ref.shape, 1)
    tiles = x_ref.shape[1] // TW
    for i in range(tiles):
        info, n = _route_tile(st[:, i * TW:(i + 1) * TW], bias_ref)
        infok_ref[:, i * TW:(i + 1) * TW] = info
        cnt_ref[...] = jnp.where(lane == step * tiles + i, jnp.broadcast_to(n, cnt_ref.shape),
                                 cnt_ref[...])


def _postmix(x, pool, ot, wout, gattn, gpost, gate, gffn, shift, scale, wr, bias_col):
    bsz, seq, _ = x.shape
    ts = TS_POST
    ns = seq // ts
    full = lambda shape: pl.BlockSpec(shape, lambda b, s: (0,) * len(shape))
    per_batch = pl.BlockSpec((1, 1, D_MODEL), lambda b, s: (b, 0, 0))
    tok = lambda width: pl.BlockSpec((1, ts, width), lambda b, s: (b, s, 0))
    return pl.pallas_call(
        _postmix_kernel,
        name="postmix",
        grid=(bsz, ns),
        in_specs=[
            tok(D_MODEL), tok(D_POOL), tok(D_ATTN),
            full((D_MODEL, D_MODEL)),
            full((1, D_ATTN)), full((1, D_MODEL)), per_batch,
            full((1, D_MODEL)), per_batch, per_batch,
            full((D_MODEL, 2 * LANES)),
            full((N_EXPERTS, 1)),
        ],
        out_specs=[
            tok(D_MODEL), tok(D_MODEL),
            pl.BlockSpec((2 * TOP_K, ts), lambda b, s: (0, b * ns + s)),
            full((N_EXPERTS, LANES)),
        ],
        out_shape=[
            jax.ShapeDtypeStruct((bsz, seq, D_MODEL), F32),
            jax.ShapeDtypeStruct((bsz, seq, D_MODEL), BF16),
            jax.ShapeDtypeStruct((2 * TOP_K, bsz * seq), F32),
            jax.ShapeDtypeStruct((N_EXPERTS, LANES), F32),
        ],
        compiler_params=pltpu.CompilerParams(
            dimension_semantics=("arbitrary", "arbitrary"), vmem_limit_bytes=VMEM_LIMIT),
    )(x, pool, ot, wout, gattn, gpost, gate, gffn, shift, scale, wr, bias_col)


def _route_tile(st, bias_ref):
    tr = st.shape[1]
    scores = [st[g * GROUP_SIZE:(g + 1) * GROUP_SIZE, :] for g in range(N_EXPERT_GROUPS)]
    sel = [scores[g] + bias_ref[g * GROUP_SIZE:(g + 1) * GROUP_SIZE, :]
           for g in range(N_EXPERT_GROUPS)]
    sub = lax.broadcasted_iota(jnp.int32, (GROUP_SIZE, tr), 0)
    neg = jnp.float32(-jnp.inf)

    rows = []
    for g in range(N_EXPERT_GROUPS):
        v = sel[g]
        m1 = jnp.max(v, axis=0, keepdims=True)
        first = jnp.min(jnp.where(v == m1, sub, GROUP_SIZE), axis=0, keepdims=True)
        m2 = jnp.max(jnp.where(sub == first, neg, v), axis=0, keepdims=True)
        rows.append(m1 + m2)
    gs = jnp.concatenate(rows, axis=0)

    shape = (GROUP_SIZE, tr)
    one = jnp.ones(shape, jnp.int32)
    zero = jnp.zeros(shape, jnp.int32)
    beaten = zero
    for gp in range(N_EXPERT_GROUPS):
        r = jnp.broadcast_to(gs[gp:gp + 1, :], shape)
        tie = jnp.where(sub > gp, one, zero)
        beaten = beaten + jnp.where(r > gs, one, zero) + jnp.where(r == gs, tie, zero)
    keep = jnp.where(beaten < TOPK_GROUPS, 1.0, 0.0)

    masked = [jnp.where(jnp.broadcast_to(keep[g:g + 1, :], shape) > 0.5, sel[g], neg)
              for g in range(N_EXPERT_GROUPS)]

    eidx = [sub + g * GROUP_SIZE for g in range(N_EXPERT_GROUPS)]
    picked = [zero for _ in range(N_EXPERT_GROUPS)]
    for _ in range(TOP_K):
        best = masked[0]
        for g in range(1, N_EXPERT_GROUPS):
            best = jnp.maximum(best, masked[g])
        best = jnp.broadcast_to(jnp.max(best, axis=0, keepdims=True), shape)
        first = jnp.where(masked[0] == best, eidx[0], N_EXPERTS)
        for g in range(1, N_EXPERT_GROUPS):
            first = jnp.minimum(first, jnp.where(masked[g] == best, eidx[g], N_EXPERTS))
        first = jnp.broadcast_to(jnp.min(first, axis=0, keepdims=True), shape)
        for g in range(N_EXPERT_GROUPS):
            hit = eidx[g] == first
            picked[g] = jnp.where(hit, one, picked[g])
            masked[g] = jnp.where(hit, neg, masked[g])
    chosen = [picked[g] > 0 for g in range(N_EXPERT_GROUPS)]

    w = [jnp.where(chosen[g], scores[g], 0.0) for g in range(N_EXPERT_GROUPS)]
    denom = w[0].sum(axis=0, keepdims=True)
    for g in range(1, N_EXPERT_GROUPS):
        denom = denom + w[g].sum(axis=0, keepdims=True)
    wfull = jnp.concatenate([w[g] / denom * ROUTED_SCALE for g in range(N_EXPERT_GROUPS)], axis=0)
    mfull = jnp.concatenate([jnp.where(chosen[g], 1.0, 0.0) for g in range(N_EXPERT_GROUPS)],
                            axis=0)
    mb = mfull.astype(BF16)

    e_r = lax.broadcasted_iota(jnp.int32, (N_EXPERTS, N_EXPERTS), 0)
    e_c = lax.broadcasted_iota(jnp.int32, (N_EXPERTS, N_EXPERTS), 1)
    before_e = (e_c < e_r).astype(BF16)
    t_r = lax.broadcasted_iota(jnp.int32, (tr, tr), 0)
    t_c = lax.broadcasted_iota(jnp.int32, (tr, tr), 1)
    before_t = (t_r < t_c).astype(BF16)
    ordinal = _dot(before_e, mb)
    rank = _dot(mb, before_t)
    n = jnp.sum(mfull, axis=1, keepdims=True)
    pieces = jnp.floor((n + (PIECE - 1.0)) * (1.0 / PIECE))
    run_start = PIECE * _dot(before_e, jnp.broadcast_to(pieces, (N_EXPERTS, LANES)).astype(BF16))
    pos = run_start[:, 0:1] + rank

    sub8 = lax.broadcasted_iota(jnp.int32, (TOP_K, tr), 0)
    pos8 = jnp.zeros((TOP_K, tr), F32)
    w8 = jnp.zeros((TOP_K, tr), F32)
    for k in range(TOP_K):
        selk = jnp.where(ordinal == float(k), mfull, 0.0)
        pk = jnp.sum(selk * pos, axis=0, keepdims=True)
        wk = jnp.sum(selk * wfull, axis=0, keepdims=True)
        pos8 = jnp.where(sub8 == k, jnp.broadcast_to(pk, (TOP_K, tr)), pos8)
        w8 = jnp.where(sub8 == k, jnp.broadcast_to(wk, (TOP_K, tr)), w8)
    return jnp.concatenate([pos8, w8], axis=0), n


def _swiglu(xb, wg, wu):
    g = _dot(xb, wg)
    return (g * jax.nn.sigmoid(g)) * _dot(xb, wu)


def _piece_copy(src_ref, src_piece, dst_ref, dst_piece, sem, n_pieces=1):
    src = src_ref.at[pl.ds(src_piece, n_pieces)]
    dst = dst_ref.at[pl.ds(dst_piece, n_pieces)]
    return pltpu.make_async_copy(src, dst, sem)


def _as_pieces(rows):
    return rows.reshape(rows.shape[0] // PIECE, PIECE, rows.shape[1])


LOCAL_BITS = 8
COPY_SIZES = (4, 3, 2, 1)
MAX_COPIES = N_EXPERTS
assert LROWS // (COPY_SIZES[0] * PIECE) <= MAX_COPIES


def _for_each_copy(copies_ref, counts_ref, tile, fn):
    for c, size in enumerate(COPY_SIZES):
        base = (tile * len(COPY_SIZES) + c) * MAX_COPIES

        def body(i, carry, base=base, size=size):
            word = copies_ref[base + i]
            fn(word & ((1 << LOCAL_BITS) - 1), lax.shift_right_logical(word, LOCAL_BITS), size)
            return carry

        lax.fori_loop(0, counts_ref[tile * (len(COPY_SIZES) + 1) + c], body, 0)


def _tile_pieces(counts_ref, tile):
    return counts_ref[tile * (len(COPY_SIZES) + 1) + len(COPY_SIZES)]


def _wait_pieces(total, copy_of):
    for bit in range((LROWS // PIECE).bit_length()):
        @pl.when((lax.shift_right_logical(total, bit) & 1) == 1)
        def _():
            copy_of(1 << bit).wait()


def _onehot_chunks(infok_ref, weighted):
    pos = infok_ref[0:TOP_K, :]
    grp = jnp.floor(pos * (1.0 / ONEHOT_GROUP))
    off = pos - ONEHOT_GROUP * grp
    sub = lax.broadcasted_iota(jnp.int32, (ONEHOT_GROUP, TW), 0).astype(F32)
    offs, grps = [], []
    for k in range(TOP_K):
        value = infok_ref[TOP_K + k:TOP_K + k + 1, :] if weighted else 1.0
        offs.append(jnp.where(sub == off[k:k + 1, :], value, 0.0).astype(BF16))
        grps.append(jnp.broadcast_to(grp[k:k + 1, :], (PIECE, TW)).astype(BF16))
    zero = jnp.zeros((PIECE, TW), BF16)

    def chunk(j):
        tiles = []
        for a in range(j * LCHUNK // ONEHOT_GROUP, (j + 1) * LCHUNK // ONEHOT_GROUP):
            hit = [grps[k] == a for k in range(TOP_K)]
            for i in range(ONEHOT_GROUP // PIECE):
                rows = zero
                for k in range(TOP_K):
                    rows = rows + jnp.where(hit[k], offs[k][i * PIECE:(i + 1) * PIECE, :], zero)
                tiles.append(rows)
        return jnp.concatenate(tiles, axis=0)

    return chunk


def _dispatch_kernel(copies_ref, counts_ref, tail_ref, ntail_ref, nv_ref,
                     h2_ref, infok_ref, xs_ref, lbuf, zbuf, sem, zsem):
    tile = pl.program_id(0)
    last = pl.num_programs(0) - 1
    slot = tile % 2
    block_pieces = RB // PIECE
    n_blocks = xs_ref.shape[0] // block_pieces

    def spare_blocks(fn):
        def per_block(b, carry):
            fn(b)
            return carry
        lax.fori_loop(nv_ref[0], n_blocks, per_block, 0)

    def zero_block(b):
        return pltpu.make_async_copy(zbuf, xs_ref.at[pl.ds(b * block_pieces, block_pieces)], zsem)

    def tails(fn):
        def per_expert(e, carry):
            n = ntail_ref[e]
            for bit in range((RB // PIECE - 1).bit_length()):
                @pl.when((lax.shift_right_logical(n, bit) & 1) == 1)
                def _():
                    done = n & ((1 << bit) - 1)
                    fn(_piece_copy(zbuf, 0, xs_ref, tail_ref[e] + done, zsem, 1 << bit))
            return carry
        lax.fori_loop(0, N_EXPERTS, per_expert, 0)

    def start(t, s):
        _for_each_copy(
            copies_ref, counts_ref, t,
            lambda lp, gp, n: _piece_copy(lbuf.at[s], lp, xs_ref, gp, sem.at[s], n).start())

    def wait(t, s):
        _wait_pieces(_tile_pieces(counts_ref, t),
                     lambda n: _piece_copy(lbuf.at[s], 0, xs_ref, 0, sem.at[s], n))

    @pl.when(tile == 0)
    def _():
        zbuf[...] = jnp.zeros_like(zbuf)
        tails(lambda copy: copy.start(priority=1))
        spare_blocks(lambda b: zero_block(b).start(priority=1))

    @pl.when(tile >= 2)
    def _():
        wait(tile - 2, slot)

    onehot = _onehot_chunks(infok_ref, weighted=False)
    xb = h2_ref[...]
    used_rows = PIECE * _tile_pieces(counts_ref, tile)

    def sort_chunk(j):
        lbuf[slot, j * LCHUNK // PIECE:(j + 1) * LCHUNK // PIECE] = _as_pieces(
            _dot(onehot(j), xb).astype(BF16))

    n_chunks = LROWS // LCHUNK
    for j in range(n_chunks - 1):
        sort_chunk(j)
    pl.when(used_rows > (n_chunks - 1) * LCHUNK)(functools.partial(sort_chunk, n_chunks - 1))

    start(tile, slot)

    @pl.when(tile == last)
    def _():
        @pl.when(tile >= 1)
        def _():
            wait(tile - 1, 1 - slot)
        wait(tile, slot)
        tails(lambda copy: copy.wait())
        spare_blocks(lambda b: zero_block(b).wait())


def _dispatch(tables, h2, infok, n_rows):
    t = h2.shape[0]
    return pl.pallas_call(
        _dispatch_kernel,
        name="dispatch",
        grid_spec=pltpu.PrefetchScalarGridSpec(
            num_scalar_prefetch=5,
            grid=(t // TW,),
            in_specs=[
                pl.BlockSpec((TW, D_MODEL), lambda i, *_: (i, 0)),
                pl.BlockSpec((2 * TOP_K, TW), lambda i, *_: (0, i)),
            ],
            out_specs=pl.BlockSpec(memory_space=pl.ANY),
            scratch_shapes=[
                pltpu.VMEM((2, LROWS // PIECE, PIECE, D_MODEL), BF16),
                pltpu.VMEM((RB // PIECE, PIECE, D_MODEL), BF16),
                pltpu.SemaphoreType.DMA((2,)),
                pltpu.SemaphoreType.DMA(()),
            ],
        ),
        out_shape=jax.ShapeDtypeStruct((n_rows // PIECE, PIECE, D_MODEL), BF16),
        compiler_params=pltpu.CompilerParams(
            dimension_semantics=("arbitrary",), vmem_limit_bytes=VMEM_LIMIT),
    )(*tables, h2, infok)


IN_BUFS = 3
OUT_BUFS = 2
W_BUFS = 2


def _experts_kernel(be_ref, plan_ref, nv_ref, xs_ref, wg_ref, wu_ref, wd_ref, ys_ref,
                    xbuf, ybuf, wgbuf, wubuf, wdbuf, isem, osem, wsem):
    n = nv_ref[0]

    def rows_in(i, slot):
        src = xs_ref.at[pl.ds(pl.multiple_of(i * RB, RB), RB), :]
        return pltpu.make_async_copy(src, xbuf.at[slot], isem.at[slot])

    def rows_out(i, slot):
        dst = ys_ref.at[pl.ds(pl.multiple_of(i * RB, RB), RB), :]
        return pltpu.make_async_copy(ybuf.at[slot], dst, osem.at[slot])

    def weights(e, slot):
        return [pltpu.make_async_copy(w.at[e], buf.at[slot], wsem.at[slot])
                for w, buf in ((wg_ref, wgbuf), (wu_ref, wubuf), (wd_ref, wdbuf))]

    for c in weights(be_ref[0], 0):
        c.start()
    for j in range(IN_BUFS - 1):
        @pl.when(j < n)
        def _():
            rows_in(j, j).start()

    def block(i, wslot):
        plan = plan_ref[i]
        first = (plan & 1) == 1
        wslot = jnp.where(first & (i > 0), 1 - wslot, wslot)

        @pl.when(first)
        def _():
            for c in weights(be_ref[i], wslot):
                c.wait()

            @pl.when((plan & 2) == 2)
            def _():
                for c in weights(lax.shift_right_logical(plan, 2), 1 - wslot):
                    c.start()

        ahead = i + IN_BUFS - 1

        @pl.when(ahead < n)
        def _():
            rows_in(ahead, ahead % IN_BUFS).start()

        islot = i % IN_BUFS
        oslot = i % OUT_BUFS
        rows_in(i, islot).wait()

        @pl.when(i >= OUT_BUFS)
        def _():
            rows_out(i - OUT_BUFS, oslot).wait()

        for j in range(RB // RSUB):
            rows = slice(j * RSUB, (j + 1) * RSUB)
            h = _swiglu(xbuf[islot, rows, :], wgbuf[wslot], wubuf[wslot])
            ybuf[oslot, rows, :] = _dot(h.astype(BF16), wdbuf[wslot]).astype(BF16)
        rows_out(i, oslot).start()
        return wslot

    lax.fori_loop(0, n, block, 0)
    for j in range(OUT_BUFS):
        @pl.when(n - 1 - j >= 0)
        def _():
            rows_out(n - 1 - j, (n - 1 - j) % OUT_BUFS).wait()


def _experts(block_expert, block_plan, n_valid, xs, wg, wu, wd):
    hbm = pl.BlockSpec(memory_space=pl.ANY)
    return pl.pallas_call(
        _experts_kernel,
        name="experts",
        grid_spec=pltpu.PrefetchScalarGridSpec(
            num_scalar_prefetch=3,
            grid=(1,),
            in_specs=[hbm, hbm, hbm, hbm],
            out_specs=hbm,
            scratch_shapes=[
                pltpu.VMEM((IN_BUFS, RB, D_MODEL), BF16),
                pltpu.VMEM((OUT_BUFS, RB, D_MODEL), BF16),
                pltpu.VMEM((W_BUFS, D_MODEL, D_EXPERT), BF16),
                pltpu.VMEM((W_BUFS, D_MODEL, D_EXPERT), BF16),
                pltpu.VMEM((W_BUFS, D_EXPERT, D_MODEL), BF16),
                pltpu.SemaphoreType.DMA((IN_BUFS,)),
                pltpu.SemaphoreType.DMA((OUT_BUFS,)),
                pltpu.SemaphoreType.DMA((W_BUFS,)),
            ],
        ),
        out_shape=jax.ShapeDtypeStruct(xs.shape, BF16),
        input_output_aliases={3: 0},
        compiler_params=pltpu.CompilerParams(
            dimension_semantics=("arbitrary",), vmem_limit_bytes=VMEM_LIMIT),
    )(block_expert, block_plan, n_valid, xs, wg, wu, wd)


def _combine_kernel(copies_ref, counts_ref,
                    infok_ref, h2_ref, x1_ref, gate_ref, gpost_ref, wsg_ref, wsu_ref, wsd_ref,
                    ys_ref, o_ref, ybuf, acc_ref, sem):
    tile = pl.program_id(0)
    last = pl.num_programs(0) - 1
    slot = tile % 2

    def start(t, s):
        _for_each_copy(
            copies_ref, counts_ref, t,
            lambda lp, gp, n: _piece_copy(ys_ref, gp, ybuf.at[s], lp, sem.at[s], n).start())

    def wait(t, s):
        _wait_pieces(_tile_pieces(counts_ref, t),
                     lambda n: _piece_copy(ys_ref, 0, ybuf.at[s], 0, sem.at[s], n))

    @pl.when(tile == 0)
    def _():
        ybuf[...] = jnp.zeros_like(ybuf)
        start(tile, slot)

    @pl.when(tile < last)
    def _():
        start(tile + 1, 1 - slot)

    wait(tile, slot)

    weights = _onehot_chunks(infok_ref, weighted=True)
    hs = _swiglu(h2_ref[...], wsg_ref[...], wsu_ref[...])
    ff = _dot(hs.astype(BF16), wsd_ref[...])
    used_rows = PIECE * _tile_pieces(counts_ref, tile)

    def sum_chunk(j):
        rows = ybuf[slot, j * LCHUNK // PIECE:(j + 1) * LCHUNK // PIECE].reshape(LCHUNK, D_MODEL)
        return lax.dot_general(weights(j), rows,
                               (((0,), (0,)), ((), ())), preferred_element_type=F32)

    def add_chunk(j):
        acc_ref[...] += sum_chunk(j)

    n_chunks = LROWS // LCHUNK
    for j in range(n_chunks - 1):
        ff = ff + sum_chunk(j)
    acc_ref[...] = ff
    pl.when(used_rows > (n_chunks - 1) * LCHUNK)(functools.partial(add_chunk, n_chunks - 1))
    o_ref[...] = x1_ref[...] + gate_ref[0] * _rms(acc_ref[...], gpost_ref[...])


def _combine(tables, infok, h2, x1, gate, gpost, wsg, wsu, wsd, ys, seq):
    t = h2.shape[0]
    per_seq = seq // TW
    full = lambda shape: pl.BlockSpec(shape, lambda i, *_: (0,) * len(shape))
    tok = lambda width: pl.BlockSpec((TW, width), lambda i, *_: (i, 0))
    return pl.pallas_call(
        _combine_kernel,
        name="combine",
        grid_spec=pltpu.PrefetchScalarGridSpec(
            num_scalar_prefetch=2,
            grid=(t // TW,),
            in_specs=[
                pl.BlockSpec((2 * TOP_K, TW), lambda i, *_: (0, i)), tok(D_MODEL), tok(D_MODEL),
                pl.BlockSpec((1, 1, D_MODEL), lambda i, *_: (i // per_seq, 0, 0)),
                full((1, D_MODEL)),
                full((D_MODEL, D_SHARED)), full((D_MODEL, D_SHARED)), full((D_SHARED, D_MODEL)),
                pl.BlockSpec(memory_space=pl.ANY),
            ],
            out_specs=tok(D_MODEL),
            scratch_shapes=[
                pltpu.VMEM((2, LROWS // PIECE, PIECE, D_MODEL), BF16),
                pltpu.VMEM((TW, D_MODEL), F32),
                pltpu.SemaphoreType.DMA((2,)),
            ],
        ),
        out_shape=jax.ShapeDtypeStruct((t, D_MODEL), F32),
        compiler_params=pltpu.CompilerParams(
            dimension_semantics=("arbitrary",), vmem_limit_bytes=VMEM_LIMIT),
    )(*tables, infok, h2, x1, gate, gpost, wsg, wsu, wsd, ys)


def _row_buffer_blocks(n_tiles):
    return (TOP_K * TW * n_tiles + N_EXPERTS * n_tiles * (PIECE - 1)
            + N_EXPERTS * (RB - PIECE) + RB - 1) // RB


def _dispatch_plan(cnt, n_tiles):
    n = cnt[:, :n_tiles].astype(jnp.int32)
    pieces = (n + PIECE - 1) // PIECE
    local = jnp.cumsum(pieces, axis=0) - pieces
    seg = jnp.sum(pieces, axis=1)
    per_block = RB // PIECE
    seg_pad = (seg + per_block - 1) // per_block * per_block
    seg_end = jnp.cumsum(seg_pad)
    seg_start = seg_end - seg_pad
    glob = seg_start[:, None] + jnp.cumsum(pieces, axis=1) - pieces
    n_blocks = _row_buffer_blocks(n_tiles)
    n_valid = seg_end[-1] // per_block
    blk = jnp.minimum(jnp.arange(n_blocks, dtype=jnp.int32), n_valid - 1)
    block_end = seg_end // per_block
    block_expert = jnp.sum((block_end[None, :] <= blk[:, None]).astype(jnp.int32), axis=1)
    block_expert = jnp.minimum(block_expert, N_EXPERTS - 1)
    ids = jnp.arange(n_blocks, dtype=jnp.int32)
    first = block_expert != jnp.concatenate([jnp.full((1,), -1, jnp.int32), block_expert[:-1]])
    experts = jnp.arange(N_EXPERTS, dtype=jnp.int32)
    next_block = jnp.sum(jnp.where(block_expert[:, None] == experts[None, :], block_end[None, :], 0),
                         axis=1)
    next_expert = jnp.sum(jnp.where(next_block[:, None] == ids[None, :], block_expert[None, :], 0),
                          axis=1)
    block_plan = (first.astype(jnp.int32) | ((next_block < n_valid).astype(jnp.int32) << 1)
                  | (next_expert << 2)).astype(jnp.int32)
    assert LROWS // PIECE <= 1 << LOCAL_BITS and n_blocks * per_block < 1 << (31 - LOCAL_BITS)

    def copy_list(count, width):
        end = jnp.cumsum(count, axis=0).T
        slot = jnp.arange(width, dtype=jnp.int32)
        expert = jnp.sum((end[:, None, :] <= slot[None, :, None]).astype(jnp.int32), axis=-1)
        pick = expert[:, :, None] == jnp.arange(N_EXPERTS, dtype=jnp.int32)
        take = lambda a: jnp.sum(jnp.where(pick, a.T[:, None, :], 0), axis=-1)
        return slot[None, :] - take(jnp.cumsum(count, axis=0) - count), take

    big = COPY_SIZES[0]
    assert COPY_SIZES == tuple(range(big, 0, -1))
    n_big = pieces // big
    rest = pieces % big
    packed = local | (glob << LOCAL_BITS)
    step = 1 + (1 << LOCAL_BITS)
    within, take = copy_list(n_big, MAX_COPIES)
    lists = [take(packed) + big * step * within]
    counts = [jnp.sum(n_big, axis=0)]
    rest_start = packed + big * step * n_big
    for size in COPY_SIZES[1:]:
        has = (rest == size).astype(jnp.int32)
        _, take = copy_list(has, MAX_COPIES)
        lists.append(take(rest_start))
        counts.append(jnp.sum(has, axis=0))
    counts.append(jnp.sum(pieces, axis=0))
    tables = (jnp.stack(lists, axis=1).reshape(-1).astype(jnp.int32),
              jnp.stack(counts, axis=1).reshape(-1).astype(jnp.int32))
    tails = ((seg_start + seg).astype(jnp.int32), (seg_pad - seg).astype(jnp.int32))
    blocks = (block_expert, block_plan)
    return tables, tails, blocks, n_valid.reshape(1).astype(jnp.int32), n_blocks * RB


def kernel(x, c, w_ada, b_ada, g_pre_mix, g_post_mix, g_pre_ffn, g_post_ffn, w_in, b_forget,
           w_pool, pool_scale, g_pool_out, g_attn_out, w_out, w_router, router_bias,
           w_gate, w_up, w_down, ws_gate, ws_up, ws_down):
    bsz, seq, d = x.shape
    depth = w_ada.shape[0]
    for l in range(depth):
        mod = _ada(c, w_ada, b_ada[l][None, :], l)
        shift_m, scale_m, gate_m, shift_f, scale_f, gate_f = [
            m.reshape(bsz, 1, d) for m in jnp.split(mod, 6, axis=-1)]

        bfp = jnp.pad(b_forget[l], (0, LANES - N_HEADS))[None, :]
        pool, qa, kt, v = _premix(
            x, shift_m, scale_m, g_pre_mix[l][None, :], w_in, l, bfp, w_pool[l].astype(BF16),
            pool_scale[l][None, :], g_pool_out[l][None, :])
        ot, wgb, wub, wdb = _attn(qa, kt, v, w_gate[l], w_up[l], w_down[l])
        t = bsz * seq
        assert t // TW <= LANES

        wr = jnp.pad(w_router[l], ((0, 0), (0, LANES - N_EXPERTS)))
        wrh = wr.astype(BF16)
        wrl = (wr - wrh.astype(F32)).astype(BF16)
        x1, h2, infok, cnt = _postmix(
            x, pool, ot, w_out[l].astype(BF16), g_attn_out[l][None, :], g_post_mix[l][None, :],
            gate_m, g_pre_ffn[l][None, :], shift_f, scale_f,
            jnp.concatenate([wrh, wrl], axis=1), router_bias[l][:, None])
        tables, tails, blocks, n_valid, n_rows = _dispatch_plan(cnt, t // TW)
        h2f = h2.reshape(t, d)
        xs = _dispatch(tables + tails + (n_valid,), h2f, infok, n_rows)
        ys = _experts(*blocks, n_valid, xs.reshape(n_rows, d), wgb, wub, wdb)
        ys = ys.reshape(xs.shape)
        out = _combine(tables, infok, h2f, x1.reshape(t, d), gate_f, g_post_ffn[l][None, :],
                       ws_gate[l].astype(BF16), ws_up[l].astype(BF16), ws_down[l].astype(BF16),
                       ys, seq)
        x = out.reshape(bsz, seq, d)
    return x
```

```python
import functools

import numpy as np
import jax
import jax.numpy as jnp
from jax import lax
from jax.experimental import pallas as pl
from jax.experimental.pallas import tpu as pltpu

D_MODEL = 1024
D_POOL = 512
POOL_WINDOWS = (2, 4, 8, 16)
POOL_GROUP = 128
MAX_WINDOW = max(POOL_WINDOWS)
D_ATTN = 512
HEAD_DIM = 64
N_HEADS = 8
N_EXPERTS = 64
N_EXPERT_GROUPS = 8
GROUP_SIZE = N_EXPERTS // N_EXPERT_GROUPS
TOPK_GROUPS = 4
TOP_K = 8
D_EXPERT = 256
D_SHARED = 256
ROUTED_SCALE = 2.5
EPS = 1e-6

LOG2E = 1.4426950408889634
LANES = 128
N_SPLIT = 3
AUG = LANES

TS_PRE = 1024
CUM_BLOCK = 256
TS_POST = 1024
TQ = 512
TK = 512
HEAD_PAIR = LANES // HEAD_DIM
assert TQ == TK
TW = 256
PIECE = 16
RB = 1024
RSUB = 512
LCHUNK = 512
ONEHOT_GROUP = 64
LROWS = -(-(TOP_K * TW + N_EXPERTS * (PIECE - 1)) // LCHUNK) * LCHUNK

F32 = jnp.float32
BF16 = jnp.bfloat16
V7X_VMEM_BYTES = 64 * 1024 * 1024
VMEM_LIMIT = V7X_VMEM_BYTES * 7 // 8


def _rms(v, g):
    return v * lax.rsqrt(jnp.mean(v * v, axis=-1, keepdims=True) + EPS) * g


def _split3(v):
    hi = v.astype(BF16)
    r1 = v - hi.astype(F32)
    mid = r1.astype(BF16)
    r2 = r1 - mid.astype(F32)
    lo = r2.astype(BF16)
    return hi, mid, lo


def _dot(a, b):
    return jnp.dot(a, b, preferred_element_type=F32)


def _ada_kernel(c_ref, w_ref, b_ref, o_ref):
    o_ref[...] = _dot(c_ref[...].astype(BF16), w_ref[0].astype(BF16)) + b_ref[...]


def _ada(c, w, b, layer):
    bsz = c.shape[0]
    n = w.shape[2]
    return pl.pallas_call(
        _ada_kernel,
        name="ada",
        grid=(n // D_MODEL,),
        in_specs=[
            pl.BlockSpec((bsz, D_MODEL), lambda j: (0, 0)),
            pl.BlockSpec((1, D_MODEL, D_MODEL), lambda j: (layer, 0, j)),
            pl.BlockSpec((1, D_MODEL), lambda j: (0, j)),
        ],
        out_specs=pl.BlockSpec((bsz, D_MODEL), lambda j: (0, j)),
        out_shape=jax.ShapeDtypeStruct((bsz, n), F32),
    )(c, w, b)


def _premix_kernel(x_ref, shift_ref, scale_ref, g_ref, win_ref, bf_ref, wpool_ref,
                   pscale_ref, gpool_ref, place_ref, ones_ref, fmask_ref,
                   pool_ref, q_ref, kt_ref, v_ref,
                   uext_ref, cum_ref, w1_ref, wf_ref):
    s = pl.program_id(1)
    ts = x_ref.shape[1]

    @pl.when((pl.program_id(0) == 0) & (s == 0))
    def _():
        q0, k0 = D_POOL, D_POOL + D_ATTN
        w1_ref[:, :q0] = win_ref[0, :, :q0].astype(BF16)
        w1_ref[:, q0:k0] = (win_ref[0, :, q0:k0] * (HEAD_DIM ** -0.5 * LOG2E)).astype(BF16)
        w1_ref[:, k0:] = win_ref[0, :, k0:k0 + 2 * D_ATTN].astype(BF16)
        wf_ref[...] = jnp.zeros_like(wf_ref)
        wf_ref[:, :N_HEADS] = win_ref[0, :, k0 + 2 * D_ATTN:].astype(BF16)

    @pl.when(s == 0)
    def _():
        uext_ref[0:MAX_WINDOW, :] = jnp.zeros((MAX_WINDOW, D_POOL), F32)
        cum_ref[...] = jnp.zeros_like(cum_ref)

    x = x_ref[0]
    h = _rms(x, g_ref[...]) * (1.0 + scale_ref[0]) + shift_ref[0]
    hb = h.astype(BF16)
    proj = _dot(hb, w1_ref[...])
    u = proj[:, :D_POOL]
    q = proj[:, D_POOL:D_POOL + D_ATTN]
    k = proj[:, D_POOL + D_ATTN:D_POOL + 2 * D_ATTN]
    v = proj[:, D_POOL + 2 * D_ATTN:]

    uext_ref[MAX_WINDOW:, :] = u
    pos = (s * ts + lax.broadcasted_iota(jnp.int32, (ts, 1), 0) + 1).astype(F32)
    ys = []
    for g, w in enumerate(POOL_WINDOWS):
        c0 = g * POOL_GROUP
        acc = uext_ref[MAX_WINDOW:, c0:c0 + POOL_GROUP]
        for j in range(1, w):
            acc = acc + uext_ref[MAX_WINDOW - j:MAX_WINDOW - j + ts, c0:c0 + POOL_GROUP]
        pooled = acc / jnp.minimum(pos, float(w)) - u[:, c0:c0 + POOL_GROUP]
        ys.append(_dot(pooled.astype(BF16), wpool_ref[g]))
    ypool = jnp.concatenate(ys, axis=1) * pscale_ref[...]
    pool_ref[0] = _rms(ypool, gpool_ref[...]).astype(BF16)
    uext_ref[0:MAX_WINDOW, :] = uext_ref[ts:ts + MAX_WINDOW, :]

    z0 = _dot(hb, wf_ref[...]) + bf_ref[...]
    z = z0
    for p in range(1, N_SPLIT):
        z = z + pltpu.roll(z0, p * N_HEADS, 1)
    logf = jnp.minimum(z, 0.0) - jnp.log1p(jnp.exp(-jnp.abs(z)))
    row = lax.broadcasted_iota(jnp.int32, (CUM_BLOCK, CUM_BLOCK), 0)
    col = lax.broadcasted_iota(jnp.int32, (CUM_BLOCK, CUM_BLOCK), 1)
    tri = (col <= row).astype(BF16)
    pieces = _split3(logf)
    carry = cum_ref[...]
    blocks = []
    for r0 in range(0, ts, CUM_BLOCK):
        blk = carry
        for piece in pieces:
            blk = blk + _dot(tri, piece[r0:r0 + CUM_BLOCK, :])
        carry = blk[CUM_BLOCK - 1:CUM_BLOCK, :]
        blocks.append(blk)
    cum = jnp.concatenate(blocks, axis=0)
    cum_ref[...] = carry

    hi, mid, lo = [piece.astype(F32) for piece in _split3(cum * LOG2E)]
    lane = lax.broadcasted_iota(jnp.int32, (ts, LANES), 1)
    pieces = jnp.where(lane < N_HEADS, hi, jnp.where(lane < 2 * N_HEADS, mid, lo))
    aug = _dot(pieces.astype(BF16), place_ref[...]) + ones_ref[...]
    aug_q = aug[:, :N_HEADS * AUG]
    aug_k = aug[:, N_HEADS * AUG:]

    def expand(a):
        blocks = []
        for j in range(D_ATTN // LANES):
            blk = a[:, j * LANES:(j + 1) * LANES]
            blocks += [blk, blk]
        return jnp.concatenate(blocks, axis=1)

    fmask = fmask_ref[...]
    qa = expand(q) * fmask + aug_q
    ka = expand(k) * fmask + aug_k
    q_ref[0] = qa.astype(BF16)
    kt_ref[0] = ka.T.astype(BF16)
    v_ref[0] = v.astype(BF16)


def _aug_constants():
    width = N_HEADS * AUG
    place = np.zeros((LANES, 2 * width), np.float32)
    ones = np.zeros((1, 2 * width), np.float32)
    fmask = np.zeros((1, width), np.float32)
    for h in range(N_HEADS):
        feat0 = h * AUG + (0 if h % 2 == 0 else HEAD_DIM)
        aug0 = h * AUG + (HEAD_DIM if h % 2 == 0 else 0)
        fmask[0, feat0:feat0 + HEAD_DIM] = 1.0
        for p in range(N_SPLIT):
            place[p * N_HEADS + h, aug0 + p] = 1.0
            ones[0, width + aug0 + p] = 1.0
            place[p * N_HEADS + h, width + aug0 + N_SPLIT + p] = -1.0
            ones[0, aug0 + N_SPLIT + p] = 1.0
    return jnp.asarray(place, BF16), jnp.asarray(ones), jnp.asarray(fmask)


def _premix(x, shift, scale, g, w_in, layer, bfp, wpool, pscale, gpool):
    bsz, seq, _ = x.shape
    d_in = w_in.shape[2]
    assert d_in == D_POOL + 3 * D_ATTN + N_HEADS
    ts = TS_PRE
    place, ones, fmask = _aug_constants()
    full = lambda shape: pl.BlockSpec(shape, lambda b, s: (0,) * len(shape))
    per_batch = pl.BlockSpec((1, 1, D_MODEL), lambda b, s: (b, 0, 0))
    return pl.pallas_call(
        _premix_kernel,
        name="premix",
        grid=(bsz, seq // ts),
        in_specs=[
            pl.BlockSpec((1, ts, D_MODEL), lambda b, s: (b, s, 0)),
            per_batch, per_batch,
            full((1, D_MODEL)),
            pl.BlockSpec((1, D_MODEL, d_in), lambda b, s: (layer, 0, 0),
                         pipeline_mode=pl.Buffered(1)),
            full((1, LANES)),
            full((len(POOL_WINDOWS), POOL_GROUP, POOL_GROUP)),
            full((1, D_POOL)),
            full((1, D_POOL)),
            full((LANES, 2 * N_HEADS * AUG)),
            full((1, 2 * N_HEADS * AUG)),
            full((1, N_HEADS * AUG)),
        ],
        out_specs=[
            pl.BlockSpec((1, ts, D_POOL), lambda b, s: (b, s, 0)),
            pl.BlockSpec((1, ts, N_HEADS * AUG), lambda b, s: (b, s, 0)),
            pl.BlockSpec((1, N_HEADS * AUG, ts), lambda b, s: (b, 0, s)),
            pl.BlockSpec((1, ts, D_ATTN), lambda b, s: (b, s, 0)),
        ],
        out_shape=[
            jax.ShapeDtypeStruct((bsz, seq, D_POOL), BF16),
            jax.ShapeDtypeStruct((bsz, seq, N_HEADS * AUG), BF16),
            jax.ShapeDtypeStruct((bsz, N_HEADS * AUG, seq), BF16),
            jax.ShapeDtypeStruct((bsz, seq, D_ATTN), BF16),
        ],
        scratch_shapes=[
            pltpu.VMEM((ts + MAX_WINDOW, D_POOL), F32),
            pltpu.VMEM((1, LANES), F32),
            pltpu.VMEM((D_MODEL, D_POOL + 3 * D_ATTN), BF16),
            pltpu.VMEM((D_MODEL, LANES), BF16),
        ],
        compiler_params=pltpu.CompilerParams(
            dimension_semantics=("arbitrary", "arbitrary"), vmem_limit_bytes=VMEM_LIMIT),
    )(x, shift, scale, g, w_in, bfp, wpool, pscale, gpool, place, ones, fmask)


def _attn_kernel(q_ref, kt_ref, v_ref, wg_ref, wu_ref, wd_ref, o_ref, wgb_ref, wub_ref, wdb_ref):
    wgb_ref[...] = wg_ref[...].astype(BF16)
    wub_ref[...] = wu_ref[...].astype(BF16)
    wdb_ref[...] = wd_ref[...].astype(BF16)
    seq = q_ref.shape[1]
    qry_i = lax.broadcasted_iota(jnp.int32, (TQ, TK), 0)
    key_i = lax.broadcasted_iota(jnp.int32, (TQ, TK), 1)
    causal = key_i <= qry_i
    one_lane = lax.broadcasted_iota(jnp.int32, (seq, LANES), 1) == 0
    vaug = jnp.concatenate([v_ref[0], jnp.where(one_lane, 1.0, 0.0).astype(BF16)], axis=1)
    out_lane = lax.broadcasted_iota(jnp.int32, (TQ, LANES), 1)
    heads = range(HEAD_PAIR)
    for qi in range(seq // TQ):
        rows = slice(qi * TQ, (qi + 1) * TQ)
        q = [q_ref[0, rows, h * AUG:(h + 1) * AUG] for h in heads]
        m = [jnp.full((TQ, 1), -jnp.inf, F32) for _ in heads]
        acc = [jnp.zeros((TQ, 2 * LANES), F32) for _ in heads]
        for kj in range(qi + 1):
            keys = slice(kj * TK, (kj + 1) * TK)
            for h in heads:
                s = _dot(q[h], kt_ref[0, h * AUG:(h + 1) * AUG, keys])
                if kj == qi:
                    s = jnp.where(causal, s, -jnp.inf)
                m_new = jnp.maximum(m[h], jnp.max(s, axis=1, keepdims=True))
                p = jnp.exp2(s - m_new).astype(BF16)
                acc[h] = jnp.exp2(m[h] - m_new) * acc[h] + _dot(p, vaug[keys, :])
                m[h] = m_new
        o = [acc[h][:, :LANES] / acc[h][:, LANES:LANES + 1] for h in heads]
        o_ref[0, rows, :] = jnp.where(out_lane < HEAD_DIM, o[0], o[1])


def _attn(q, kt, v, wg, wu, wd):
    bsz, seq, _ = q.shape
    pairs = N_HEADS // HEAD_PAIR
    per_step = N_EXPERTS // (bsz * pairs)
    assert per_step * bsz * pairs == N_EXPERTS
    wspec = lambda shape: pl.BlockSpec((per_step,) + shape, lambda b, h: (b * pairs + h, 0, 0))
    return pl.pallas_call(
        _attn_kernel,
        name="attn",
        grid=(bsz, pairs),
        in_specs=[
            pl.BlockSpec((1, seq, HEAD_PAIR * AUG), lambda b, h: (b, 0, h)),
            pl.BlockSpec((1, HEAD_PAIR * AUG, seq), lambda b, h: (b, h, 0)),
            pl.BlockSpec((1, seq, LANES), lambda b, h: (b, 0, h)),
            wspec((D_MODEL, D_EXPERT)), wspec((D_MODEL, D_EXPERT)), wspec((D_EXPERT, D_MODEL)),
        ],
        out_specs=[
            pl.BlockSpec((1, seq, LANES), lambda b, h: (b, 0, h)),
            wspec((D_MODEL, D_EXPERT)), wspec((D_MODEL, D_EXPERT)), wspec((D_EXPERT, D_MODEL)),
        ],
        out_shape=[
            jax.ShapeDtypeStruct((bsz, seq, D_ATTN), F32),
            jax.ShapeDtypeStruct(wg.shape, BF16),
            jax.ShapeDtypeStruct(wu.shape, BF16),
            jax.ShapeDtypeStruct(wd.shape, BF16),
        ],
        compiler_params=pltpu.CompilerParams(
            dimension_semantics=("arbitrary", "arbitrary"), vmem_limit_bytes=VMEM_LIMIT),
    )(q, kt, v, wg, wu, wd)


def _postmix_kernel(x_ref, pool_ref, o_ref, wout_ref, gattn_ref, gpost_ref, gate_ref,
                    gffn_ref, shift_ref, scale_ref, wr_ref, bias_ref,
                    x1_ref, h2_ref, infok_ref, cnt_ref):
    ya = _rms(o_ref[0], gattn_ref[...]).astype(BF16)
    mixed = _dot(pool_ref[0], wout_ref[:D_POOL, :]) + _dot(ya, wout_ref[D_POOL:, :])
    x1 = x_ref[0] + gate_ref[0] * _rms(mixed, gpost_ref[...])
    x1_ref[0] = x1
    h2 = _rms(x1, gffn_ref[...]) * (1.0 + scale_ref[0]) + shift_ref[0]
    h2b = h2.astype(BF16)
    h2_ref[0] = h2b
    h2l = (h2 - h2b.astype(F32)).astype(BF16)
    both = _dot(h2b, wr_ref[...])
    logits = both[:, :LANES] + both[:, LANES:] + _dot(h2l, wr_ref[:, :LANES])
    st = jax.nn.sigmoid(logits).T

    step = pl.program_id(0) * pl.num_programs(1) + pl.program_id(1)

    @pl.when(step == 0)
    def _():
        cnt_ref[...] = jnp.zeros_like(cnt_ref)

    lane = lax.broadcasted_iota(jnp.int32, cnt_ref.shape, 1)
    tiles = x_ref.shape[1] // TW
    for i in range(tiles):
        info, n = _route_tile(st[:, i * TW:(i + 1) * TW], bias_ref)
        infok_ref[:, i * TW:(i + 1) * TW] = info
        cnt_ref[...] = jnp.where(lane == step * tiles + i, jnp.broadcast_to(n, cnt_ref.shape),
                                 cnt_ref[...])


def _postmix(x, pool, ot, wout, gattn, gpost, gate, gffn, shift, scale, wr, bias_col):
    bsz, seq, _ = x.shape
    ts = TS_POST
    ns = seq // ts
    full = lambda shape: pl.BlockSpec(shape, lambda b, s: (0,) * len(shape))
    per_batch = pl.BlockSpec((1, 1, D_MODEL), lambda b, s: (b, 0, 0))
    tok = lambda width: pl.BlockSpec((1, ts, width), lambda b, s: (b, s, 0))
    return pl.pallas_call(
        _postmix_kernel,
        name="postmix",
        grid=(bsz, ns),
        in_specs=[
            tok(D_MODEL), tok(D_POOL), tok(D_ATTN),
            full((D_MODEL, D_MODEL)),
            full((1, D_ATTN)), full((1, D_MODEL)), per_batch,
            full((1, D_MODEL)), per_batch, per_batch,
            full((D_MODEL, 2 * LANES)),
            full((N_EXPERTS, 1)),
        ],
        out_specs=[
            tok(D_MODEL), tok(D_MODEL),
            pl.BlockSpec((2 * TOP_K, ts), lambda b, s: (0, b * ns + s)),
            full((N_EXPERTS, LANES)),
        ],
        out_shape=[
            jax.ShapeDtypeStruct((bsz, seq, D_MODEL), F32),
            jax.ShapeDtypeStruct((bsz, seq, D_MODEL), BF16),
            jax.ShapeDtypeStruct((2 * TOP_K, bsz * seq), F32),
            jax.ShapeDtypeStruct((N_EXPERTS, LANES), F32),
        ],
        compiler_params=pltpu.CompilerParams(
            dimension_semantics=("arbitrary", "arbitrary"), vmem_limit_bytes=VMEM_LIMIT),
    )(x, pool, ot, wout, gattn, gpost, gate, gffn, shift, scale, wr, bias_col)


def _route_tile(st, bias_ref):
    tr = st.shape[1]
    scores = [st[g * GROUP_SIZE:(g + 1) * GROUP_SIZE, :] for g in range(N_EXPERT_GROUPS)]
    sel = [scores[g] + bias_ref[g * GROUP_SIZE:(g + 1) * GROUP_SIZE, :]
           for g in range(N_EXPERT_GROUPS)]
    sub = lax.broadcasted_iota(jnp.int32, (GROUP_SIZE, tr), 0)
    neg = jnp.float32(-jnp.inf)

    rows = []
    for g in range(N_EXPERT_GROUPS):
        v = sel[g]
        m1 = jnp.max(v, axis=0, keepdims=True)
        first = jnp.min(jnp.where(v == m1, sub, GROUP_SIZE), axis=0, keepdims=True)
        m2 = jnp.max(jnp.where(sub == first, neg, v), axis=0, keepdims=True)
        rows.append(m1 + m2)
    gs = jnp.concatenate(rows, axis=0)

    shape = (GROUP_SIZE, tr)
    one = jnp.ones(shape, jnp.int32)
    zero = jnp.zeros(shape, jnp.int32)
    beaten = zero
    for gp in range(N_EXPERT_GROUPS):
        r = jnp.broadcast_to(gs[gp:gp + 1, :], shape)
        tie = jnp.where(sub > gp, one, zero)
        beaten = beaten + jnp.where(r > gs, one, zero) + jnp.where(r == gs, tie, zero)
    keep = jnp.where(beaten < TOPK_GROUPS, 1.0, 0.0)

    masked = [jnp.where(jnp.broadcast_to(keep[g:g + 1, :], shape) > 0.5, sel[g], neg)
              for g in range(N_EXPERT_GROUPS)]

    eidx = [sub + g * GROUP_SIZE for g in range(N_EXPERT_GROUPS)]
    picked = [zero for _ in range(N_EXPERT_GROUPS)]
    for _ in range(TOP_K):
        best = masked[0]
        for g in range(1, N_EXPERT_GROUPS):
            best = jnp.maximum(best, masked[g])
        best = jnp.broadcast_to(jnp.max(best, axis=0, keepdims=True), shape)
        first = jnp.where(masked[0] == best, eidx[0], N_EXPERTS)
        for g in range(1, N_EXPERT_GROUPS):
            first = jnp.minimum(first, jnp.where(masked[g] == best, eidx[g], N_EXPERTS))
        first = jnp.broadcast_to(jnp.min(first, axis=0, keepdims=True), shape)
        for g in range(N_EXPERT_GROUPS):
            hit = eidx[g] == first
            picked[g] = jnp.where(hit, one, picked[g])
            masked[g] = jnp.where(hit, neg, masked[g])
    chosen = [picked[g] > 0 for g in range(N_EXPERT_GROUPS)]

    w = [jnp.where(chosen[g], scores[g], 0.0) for g in range(N_EXPERT_GROUPS)]
    denom = w[0].sum(axis=0, keepdims=True)
    for g in range(1, N_EXPERT_GROUPS):
        denom = denom + w[g].sum(axis=0, keepdims=True)
    wfull = jnp.concatenate([w[g] / denom * ROUTED_SCALE for g in range(N_EXPERT_GROUPS)], axis=0)
    mfull = jnp.concatenate([jnp.where(chosen[g], 1.0, 0.0) for g in range(N_EXPERT_GROUPS)],
                            axis=0)
    mb = mfull.astype(BF16)

    e_r = lax.broadcasted_iota(jnp.int32, (N_EXPERTS, N_EXPERTS), 0)
    e_c = lax.broadcasted_iota(jnp.int32, (N_EXPERTS, N_EXPERTS), 1)
    before_e = (e_c < e_r).astype(BF16)
    t_r = lax.broadcasted_iota(jnp.int32, (tr, tr), 0)
    t_c = lax.broadcasted_iota(jnp.int32, (tr, tr), 1)
    before_t = (t_r < t_c).astype(BF16)
    ordinal = _dot(before_e, mb)
    rank = _dot(mb, before_t)
    n = jnp.sum(mfull, axis=1, keepdims=True)
    pieces = jnp.floor((n + (PIECE - 1.0)) * (1.0 / PIECE))
    run_start = PIECE * _dot(before_e, jnp.broadcast_to(pieces, (N_EXPERTS, LANES)).astype(BF16))
    pos = run_start[:, 0:1] + rank

    sub8 = lax.broadcasted_iota(jnp.int32, (TOP_K, tr), 0)
    pos8 = jnp.zeros((TOP_K, tr), F32)
    w8 = jnp.zeros((TOP_K, tr), F32)
    for k in range(TOP_K):
        selk = jnp.where(ordinal == float(k), mfull, 0.0)
        pk = jnp.sum(selk * pos, axis=0, keepdims=True)
        wk = jnp.sum(selk * wfull, axis=0, keepdims=True)
        pos8 = jnp.where(sub8 == k, jnp.broadcast_to(pk, (TOP_K, tr)), pos8)
        w8 = jnp.where(sub8 == k, jnp.broadcast_to(wk, (TOP_K, tr)), w8)
    return jnp.concatenate([pos8, w8], axis=0), n


def _swiglu(xb, wg, wu):
    g = _dot(xb, wg)
    return (g * jax.nn.sigmoid(g)) * _dot(xb, wu)


def _piece_copy(src_ref, src_piece, dst_ref, dst_piece, sem, n_pieces=1):
    src = src_ref.at[pl.ds(src_piece, n_pieces)]
    dst = dst_ref.at[pl.ds(dst_piece, n_pieces)]
    return pltpu.make_async_copy(src, dst, sem)


def _as_pieces(rows):
    return rows.reshape(rows.shape[0] // PIECE, PIECE, rows.shape[1])


LOCAL_BITS = 8
COPY_SIZES = (4, 3, 2, 1)
MAX_COPIES = N_EXPERTS
assert LROWS // (COPY_SIZES[0] * PIECE) <= MAX_COPIES


def _for_each_copy(copies_ref, counts_ref, tile, fn):
    for c, size in enumerate(COPY_SIZES):
        base = (tile * len(COPY_SIZES) + c) * MAX_COPIES

        def body(i, carry, base=base, size=size):
            word = copies_ref[base + i]
            fn(word & ((1 << LOCAL_BITS) - 1), lax.shift_right_logical(word, LOCAL_BITS), size)
            return carry

        lax.fori_loop(0, counts_ref[tile * (len(COPY_SIZES) + 1) + c], body, 0)


def _tile_pieces(counts_ref, tile):
    return counts_ref[tile * (len(COPY_SIZES) + 1) + len(COPY_SIZES)]


def _wait_pieces(total, copy_of):
    for bit in range((LROWS // PIECE).bit_length()):
        @pl.when((lax.shift_right_logical(total, bit) & 1) == 1)
        def _():
            copy_of(1 << bit).wait()


def _onehot_chunks(infok_ref, weighted):
    pos = infok_ref[0:TOP_K, :]
    grp = jnp.floor(pos * (1.0 / ONEHOT_GROUP))
    off = pos - ONEHOT_GROUP * grp
    sub = lax.broadcasted_iota(jnp.int32, (ONEHOT_GROUP, TW), 0).astype(F32)
    offs, grps = [], []
    for k in range(TOP_K):
        value = infok_ref[TOP_K + k:TOP_K + k + 1, :] if weighted else 1.0
        offs.append(jnp.where(sub == off[k:k + 1, :], value, 0.0).astype(BF16))
        grps.append(jnp.broadcast_to(grp[k:k + 1, :], (PIECE, TW)).astype(BF16))
    zero = jnp.zeros((PIECE, TW), BF16)

    def chunk(j):
        tiles = []
        for a in range(j * LCHUNK // ONEHOT_GROUP, (j + 1) * LCHUNK // ONEHOT_GROUP):
            hit = [grps[k] == a for k in range(TOP_K)]
            for i in range(ONEHOT_GROUP // PIECE):
                rows = zero
                for k in range(TOP_K):
                    rows = rows + jnp.where(hit[k], offs[k][i * PIECE:(i + 1) * PIECE, :], zero)
                tiles.append(rows)
        return jnp.concatenate(tiles, axis=0)

    return chunk


def _dispatch_kernel(copies_ref, counts_ref, tail_ref, ntail_ref, nv_ref,
                     h2_ref, infok_ref, xs_ref, lbuf, zbuf, sem, zsem):
    tile = pl.program_id(0)
    last = pl.num_programs(0) - 1
    slot = tile % 2
    block_pieces = RB // PIECE
    n_blocks = xs_ref.shape[0] // block_pieces

    def spare_blocks(fn):
        def per_block(b, carry):
            fn(b)
            return carry
        lax.fori_loop(nv_ref[0], n_blocks, per_block, 0)

    def zero_block(b):
        return pltpu.make_async_copy(zbuf, xs_ref.at[pl.ds(b * block_pieces, block_pieces)], zsem)

    def tails(fn):
        def per_expert(e, carry):
            n = ntail_ref[e]
            for bit in range((RB // PIECE - 1).bit_length()):
                @pl.when((lax.shift_right_logical(n, bit) & 1) == 1)
                def _():
                    done = n & ((1 << bit) - 1)
                    fn(_piece_copy(zbuf, 0, xs_ref, tail_ref[e] + done, zsem, 1 << bit))
            return carry
        lax.fori_loop(0, N_EXPERTS, per_expert, 0)

    def start(t, s):
        _for_each_copy(
            copies_ref, counts_ref, t,
            lambda lp, gp, n: _piece_copy(lbuf.at[s], lp, xs_ref, gp, sem.at[s], n).start())

    def wait(t, s):
        _wait_pieces(_tile_pieces(counts_ref, t),
                     lambda n: _piece_copy(lbuf.at[s], 0, xs_ref, 0, sem.at[s], n))

    @pl.when(tile == 0)
    def _():
        zbuf[...] = jnp.zeros_like(zbuf)
        tails(lambda copy: copy.start(priority=1))
        spare_blocks(lambda b: zero_block(b).start(priority=1))

    @pl.when(tile >= 2)
    def _():
        wait(tile - 2, slot)

    onehot = _onehot_chunks(infok_ref, weighted=False)
    xb = h2_ref[...]
    used_rows = PIECE * _tile_pieces(counts_ref, tile)

    def sort_chunk(j):
        lbuf[slot, j * LCHUNK // PIECE:(j + 1) * LCHUNK // PIECE] = _as_pieces(
            _dot(onehot(j), xb).astype(BF16))

    n_chunks = LROWS // LCHUNK
    for j in range(n_chunks - 1):
        sort_chunk(j)
    pl.when(used_rows > (n_chunks - 1) * LCHUNK)(functools.partial(sort_chunk, n_chunks - 1))

    start(tile, slot)

    @pl.when(tile == last)
    def _():
        @pl.when(tile >= 1)
        def _():
            wait(tile - 1, 1 - slot)
        wait(tile, slot)
        tails(lambda copy: copy.wait())
        spare_blocks(lambda b: zero_block(b).wait())


def _dispatch(tables, h2, infok, n_rows):
    t = h2.shape[0]
    return pl.pallas_call(
        _dispatch_kernel,
        name="dispatch",
        grid_spec=pltpu.PrefetchScalarGridSpec(
            num_scalar_prefetch=5,
            grid=(t // TW,),
            in_specs=[
                pl.BlockSpec((TW, D_MODEL), lambda i, *_: (i, 0)),
                pl.BlockSpec((2 * TOP_K, TW), lambda i, *_: (0, i)),
            ],
            out_specs=pl.BlockSpec(memory_space=pl.ANY),
            scratch_shapes=[
                pltpu.VMEM((2, LROWS // PIECE, PIECE, D_MODEL), BF16),
                pltpu.VMEM((RB // PIECE, PIECE, D_MODEL), BF16),
                pltpu.SemaphoreType.DMA((2,)),
                pltpu.SemaphoreType.DMA(()),
            ],
        ),
        out_shape=jax.ShapeDtypeStruct((n_rows // PIECE, PIECE, D_MODEL), BF16),
        compiler_params=pltpu.CompilerParams(
            dimension_semantics=("arbitrary",), vmem_limit_bytes=VMEM_LIMIT),
    )(*tables, h2, infok)


IN_BUFS = 3
OUT_BUFS = 2
W_BUFS = 2


def _experts_kernel(be_ref, plan_ref, nv_ref, xs_ref, wg_ref, wu_ref, wd_ref, ys_ref,
                    xbuf, ybuf, hbuf, wgbuf, wubuf, wdbuf, isem, osem, wsem):
    n = nv_ref[0]

    def rows_in(i, slot):
        src = xs_ref.at[pl.ds(pl.multiple_of(i * RB, RB), RB), :]
        return pltpu.make_async_copy(src, xbuf.at[slot], isem.at[slot])

    def rows_out(i, slot):
        dst = ys_ref.at[pl.ds(pl.multiple_of(i * RB, RB), RB), :]
        return pltpu.make_async_copy(ybuf.at[slot], dst, osem.at[slot])

    def weights(e, slot):
        return [pltpu.make_async_copy(w.at[e], buf.at[slot], wsem.at[slot])
                for w, buf in ((wg_ref, wgbuf), (wu_ref, wubuf), (wd_ref, wdbuf))]

    for c in weights(be_ref[0], 0):
        c.start()
    for j in range(IN_BUFS - 1):
        @pl.when(j < n)
        def _():
            rows_in(j, j).start()

    def fetch_ahead(i):
        ahead = i + IN_BUFS - 1

        @pl.when(ahead < n)
        def _():
            rows_in(ahead, ahead % IN_BUFS).start()

    def prefetch_weights(plan, first, into):
        @pl.when(first & ((plan & 2) == 2))
        def _():
            for c in weights(lax.shift_right_logical(plan, 2), into):
                c.start()

    def up(i, wslot):
        for j in range(RB // RSUB):
            rows = slice(j * RSUB, (j + 1) * RSUB)
            hbuf[i % 2, rows, :] = _swiglu(
                xbuf[i % IN_BUFS, rows, :], wgbuf[wslot], wubuf[wslot]).astype(BF16)

    def down(i, wslot):
        for j in range(RB // RSUB):
            rows = slice(j * RSUB, (j + 1) * RSUB)
            ybuf[i % OUT_BUFS, rows, :] = _dot(hbuf[i % 2, rows, :], wdbuf[wslot]).astype(BF16)

    def free_out(i):
        @pl.when(i >= OUT_BUFS)
        def _():
            rows_out(i - OUT_BUFS, i % OUT_BUFS).wait()

    plan0 = plan_ref[0]
    for c in weights(be_ref[0], 0):
        c.wait()
    prefetch_weights(plan0, (plan0 & 1) == 1, 1)
    fetch_ahead(0)
    rows_in(0, 0).wait()
    up(0, 0)

    def block(i, prev_wslot):
        plan = plan_ref[i]
        first = (plan & 1) == 1
        wslot = jnp.where(first, 1 - prev_wslot, prev_wslot)

        @pl.when(first)
        def _():
            for c in weights(be_ref[i], wslot):
                c.wait()

        fetch_ahead(i)
        rows_in(i, i % IN_BUFS).wait()
        free_out(i - 1)
        up(i, wslot)
        down(i - 1, prev_wslot)
        rows_out(i - 1, (i - 1) % OUT_BUFS).start()
        prefetch_weights(plan, first, prev_wslot)
        return wslot

    last_wslot = lax.fori_loop(1, n, block, 0)
    free_out(n - 1)
    down(n - 1, last_wslot)
    rows_out(n - 1, (n - 1) % OUT_BUFS).start()
    for j in range(OUT_BUFS):
        @pl.when(n - 1 - j >= 0)
        def _():
            rows_out(n - 1 - j, (n - 1 - j) % OUT_BUFS).wait()


def _experts(block_expert, block_plan, n_valid, xs, wg, wu, wd):
    hbm = pl.BlockSpec(memory_space=pl.ANY)
    return pl.pallas_call(
        _experts_kernel,
        name="experts",
        grid_spec=pltpu.PrefetchScalarGridSpec(
            num_scalar_prefetch=3,
            grid=(1,),
            in_specs=[hbm, hbm, hbm, hbm],
            out_specs=hbm,
            scratch_shapes=[
                pltpu.VMEM((IN_BUFS, RB, D_MODEL), BF16),
                pltpu.VMEM((OUT_BUFS, RB, D_MODEL), BF16),
                pltpu.VMEM((2, RB, D_EXPERT), BF16),
                pltpu.VMEM((W_BUFS, D_MODEL, D_EXPERT), BF16),
                pltpu.VMEM((W_BUFS, D_MODEL, D_EXPERT), BF16),
                pltpu.VMEM((W_BUFS, D_EXPERT, D_MODEL), BF16),
                pltpu.SemaphoreType.DMA((IN_BUFS,)),
                pltpu.SemaphoreType.DMA((OUT_BUFS,)),
                pltpu.SemaphoreType.DMA((W_BUFS,)),
            ],
        ),
        out_shape=jax.ShapeDtypeStruct(xs.shape, BF16),
        input_output_aliases={3: 0},
        compiler_params=pltpu.CompilerParams(
            dimension_semantics=("arbitrary",), vmem_limit_bytes=VMEM_LIMIT),
    )(block_expert, block_plan, n_valid, xs, wg, wu, wd)


def _combine_kernel(copies_ref, counts_ref,
                    infok_ref, h2_ref, x1_ref, gate_ref, gpost_ref, wsg_ref, wsu_ref, wsd_ref,
                    ys_ref, o_ref, ybuf, acc_ref, sem):
    tile = pl.program_id(0)
    last = pl.num_programs(0) - 1
    slot = tile % 2

    def start(t, s):
        _for_each_copy(
            copies_ref, counts_ref, t,
            lambda lp, gp, n: _piece_copy(ys_ref, gp, ybuf.at[s], lp, sem.at[s], n).start())

    def wait(t, s):
        _wait_pieces(_tile_pieces(counts_ref, t),
                     lambda n: _piece_copy(ys_ref, 0, ybuf.at[s], 0, sem.at[s], n))

    @pl.when(tile == 0)
    def _():
        ybuf[...] = jnp.zeros_like(ybuf)
        start(tile, slot)

    @pl.when(tile < last)
    def _():
        start(tile + 1, 1 - slot)

    wait(tile, slot)

    weights = _onehot_chunks(infok_ref, weighted=True)
    hs = _swiglu(h2_ref[...], wsg_ref[...], wsu_ref[...])
    ff = _dot(hs.astype(BF16), wsd_ref[...])
    used_rows = PIECE * _tile_pieces(counts_ref, tile)

    def sum_chunk(j):
        rows = ybuf[slot, j * LCHUNK // PIECE:(j + 1) * LCHUNK // PIECE].reshape(LCHUNK, D_MODEL)
        return lax.dot_general(weights(j), rows,
                               (((0,), (0,)), ((), ())), preferred_element_type=F32)

    def add_chunk(j):
        acc_ref[...] += sum_chunk(j)

    n_chunks = LROWS // LCHUNK
    for j in range(n_chunks - 1):
        ff = ff + sum_chunk(j)
    acc_ref[...] = ff
    pl.when(used_rows > (n_chunks - 1) * LCHUNK)(functools.partial(add_chunk, n_chunks - 1))
    o_ref[...] = x1_ref[...] + gate_ref[0] * _rms(acc_ref[...], gpost_ref[...])


def _combine(tables, infok, h2, x1, gate, gpost, wsg, wsu, wsd, ys, seq):
    t = h2.shape[0]
    per_seq = seq // TW
    full = lambda shape: pl.BlockSpec(shape, lambda i, *_: (0,) * len(shape))
    tok = lambda width: pl.BlockSpec((TW, width), lambda i, *_: (i, 0))
    return pl.pallas_call(
        _combine_kernel,
        name="combine",
        grid_spec=pltpu.PrefetchScalarGridSpec(
            num_scalar_prefetch=2,
            grid=(t // TW,),
            in_specs=[
                pl.BlockSpec((2 * TOP_K, TW), lambda i, *_: (0, i)), tok(D_MODEL), tok(D_MODEL),
                pl.BlockSpec((1, 1, D_MODEL), lambda i, *_: (i // per_seq, 0, 0)),
                full((1, D_MODEL)),
                full((D_MODEL, D_SHARED)), full((D_MODEL, D_SHARED)), full((D_SHARED, D_MODEL)),
                pl.BlockSpec(memory_space=pl.ANY),
            ],
            out_specs=tok(D_MODEL),
            scratch_shapes=[
                pltpu.VMEM((2, LROWS // PIECE, PIECE, D_MODEL), BF16),
                pltpu.VMEM((TW, D_MODEL), F32),
                pltpu.SemaphoreType.DMA((2,)),
            ],
        ),
        out_shape=jax.ShapeDtypeStruct((t, D_MODEL), F32),
        compiler_params=pltpu.CompilerParams(
            dimension_semantics=("arbitrary",), vmem_limit_bytes=VMEM_LIMIT),
    )(*tables, infok, h2, x1, gate, gpost, wsg, wsu, wsd, ys)


def _row_buffer_blocks(n_tiles):
    return (TOP_K * TW * n_tiles + N_EXPERTS * n_tiles * (PIECE - 1)
            + N_EXPERTS * (RB - PIECE) + RB - 1) // RB


def _dispatch_plan(cnt, n_tiles):
    n = cnt[:, :n_tiles].astype(jnp.int32)
    pieces = (n + PIECE - 1) // PIECE
    local = jnp.cumsum(pieces, axis=0) - pieces
    seg = jnp.sum(pieces, axis=1)
    per_block = RB // PIECE
    seg_pad = (seg + per_block - 1) // per_block * per_block
    seg_end = jnp.cumsum(seg_pad)
    seg_start = seg_end - seg_pad
    glob = seg_start[:, None] + jnp.cumsum(pieces, axis=1) - pieces
    n_blocks = _row_buffer_blocks(n_tiles)
    n_valid = seg_end[-1] // per_block
    blk = jnp.minimum(jnp.arange(n_blocks, dtype=jnp.int32), n_valid - 1)
    block_end = seg_end // per_block
    block_expert = jnp.sum((block_end[None, :] <= blk[:, None]).astype(jnp.int32), axis=1)
    block_expert = jnp.minimum(block_expert, N_EXPERTS - 1)
    ids = jnp.arange(n_blocks, dtype=jnp.int32)
    first = block_expert != jnp.concatenate([jnp.full((1,), -1, jnp.int32), block_expert[:-1]])
    experts = jnp.arange(N_EXPERTS, dtype=jnp.int32)
    next_block = jnp.sum(jnp.where(block_expert[:, None] == experts[None, :], block_end[None, :], 0),
                         axis=1)
    next_expert = jnp.sum(jnp.where(next_block[:, None] == ids[None, :], block_expert[None, :], 0),
                          axis=1)
    block_plan = (first.astype(jnp.int32) | ((next_block < n_valid).astype(jnp.int32) << 1)
                  | (next_expert << 2)).astype(jnp.int32)
    assert LROWS // PIECE <= 1 << LOCAL_BITS and n_blocks * per_block < 1 << (31 - LOCAL_BITS)

    def copy_list(count, width):
        end = jnp.cumsum(count, axis=0).T
        slot = jnp.arange(width, dtype=jnp.int32)
        expert = jnp.sum((end[:, None, :] <= slot[None, :, None]).astype(jnp.int32), axis=-1)
        pick = expert[:, :, None] == jnp.arange(N_EXPERTS, dtype=jnp.int32)
        take = lambda a: jnp.sum(jnp.where(pick, a.T[:, None, :], 0), axis=-1)
        return slot[None, :] - take(jnp.cumsum(count, axis=0) - count), take

    big = COPY_SIZES[0]
    assert COPY_SIZES == tuple(range(big, 0, -1))
    n_big = pieces // big
    rest = pieces % big
    packed = local | (glob << LOCAL_BITS)
    step = 1 + (1 << LOCAL_BITS)
    within, take = copy_list(n_big, MAX_COPIES)
    lists = [take(packed) + big * step * within]
    counts = [jnp.sum(n_big, axis=0)]
    rest_start = packed + big * step * n_big
    for size in COPY_SIZES[1:]:
        has = (rest == size).astype(jnp.int32)
        _, take = copy_list(has, MAX_COPIES)
        lists.append(take(rest_start))
        counts.append(jnp.sum(has, axis=0))
    counts.append(jnp.sum(pieces, axis=0))
    tables = (jnp.stack(lists, axis=1).reshape(-1).astype(jnp.int32),
              jnp.stack(counts, axis=1).reshape(-1).astype(jnp.int32))
    tails = ((seg_start + seg).astype(jnp.int32), (seg_pad - seg).astype(jnp.int32))
    blocks = (block_expert, block_plan)
    return tables, tails, blocks, n_valid.reshape(1).astype(jnp.int32), n_blocks * RB


def kernel(x, c, w_ada, b_ada, g_pre_mix, g_post_mix, g_pre_ffn, g_post_ffn, w_in, b_forget,
           w_pool, pool_scale, g_pool_out, g_attn_out, w_out, w_router, router_bias,
           w_gate, w_up, w_down, ws_gate, ws_up, ws_down):
    bsz, seq, d = x.shape
    depth = w_ada.shape[0]
    for l in range(depth):
        mod = _ada(c, w_ada, b_ada[l][None, :], l)
        shift_m, scale_m, gate_m, shift_f, scale_f, gate_f = [
            m.reshape(bsz, 1, d) for m in jnp.split(mod, 6, axis=-1)]

        bfp = jnp.pad(b_forget[l], (0, LANES - N_HEADS))[None, :]
        pool, qa, kt, v = _premix(
            x, shift_m, scale_m, g_pre_mix[l][None, :], w_in, l, bfp, w_pool[l].astype(BF16),
            pool_scale[l][None, :], g_pool_out[l][None, :])
        ot, wgb, wub, wdb = _attn(qa, kt, v, w_gate[l], w_up[l], w_down[l])
        t = bsz * seq
        assert t // TW <= LANES

        wr = jnp.pad(w_router[l], ((0, 0), (0, LANES - N_EXPERTS)))
        wrh = wr.astype(BF16)
        wrl = (wr - wrh.astype(F32)).astype(BF16)
        x1, h2, infok, cnt = _postmix(
            x, pool, ot, w_out[l].astype(BF16), g_attn_out[l][None, :], g_post_mix[l][None, :],
            gate_m, g_pre_ffn[l][None, :], shift_f, scale_f,
            jnp.concatenate([wrh, wrl], axis=1), router_bias[l][:, None])
        tables, tails, blocks, n_valid, n_rows = _dispatch_plan(cnt, t // TW)
        h2f = h2.reshape(t, d)
        xs = _dispatch(tables + tails + (n_valid,), h2f, infok, n_rows)
        ys = _experts(*blocks, n_valid, xs.reshape(n_rows, d), wgb, wub, wdb)
        ys = ys.reshape(xs.shape)
        out = _combine(tables, infok, h2f, x1.reshape(t, d), gate_f, g_post_ffn[l][None, :],
                       ws_gate[l].astype(BF16), ws_up[l].astype(BF16), ws_down[l].astype(BF16),
                       ys, seq)
        x = out.reshape(bsz, seq, d)
    return x
```
